```python
import math
import jax, jax.numpy as jnp
from jax import lax
import numpy as np

D_MODEL = 1024
BATCH = 8
SEQ = 4096
DEPTH = 1

SSM_EXPAND = 2
SSM_D_INNER = SSM_EXPAND * D_MODEL
SSM_HEAD_DIM = 64
SSM_HEADS = SSM_D_INNER // SSM_HEAD_DIM
SSM_GROUPS = 4
SSM_HEADS_PER_GROUP = SSM_HEADS // SSM_GROUPS
SSM_D_STATE = 128
SSM_CONV = 4
SSM_CHUNK = 128
SSM_BC_DIM = SSM_GROUPS * SSM_D_STATE
SSM_CONV_DIM = SSM_D_INNER + 2 * SSM_BC_DIM
SSM_NORM_GROUP = SSM_D_INNER // SSM_GROUPS
SSM_DT_MIN = 0.001
SSM_DT_MAX = 0.1

LRU_WIDTH = 1280
LRU_BLOCKS = 10
LRU_BLOCK = LRU_WIDTH // LRU_BLOCKS
LRU_CONV = 4
LRU_C = 8.0

FFN_HIDDEN = -(-8 * D_MODEL // (3 * 256)) * 256

RMS_EPS = 1e-6

N_GATES = 2 * D_MODEL
IN_PROJ_DIM = N_GATES + SSM_D_INNER + SSM_CONV_DIM + SSM_HEADS + 2 * LRU_WIDTH
IN_SPLITS = (
    N_GATES,
    N_GATES + SSM_D_INNER,
    N_GATES + SSM_D_INNER + SSM_CONV_DIM,
    N_GATES + SSM_D_INNER + SSM_CONV_DIM + SSM_HEADS,
    N_GATES + SSM_D_INNER + SSM_CONV_DIM + SSM_HEADS + LRU_WIDTH,
)

kernel_name = "hybrid_ssd_rglru_gated_block"


def rmsnorm(x, w, eps=RMS_EPS):
    xf = x.astype(jnp.float32)
    y = xf * lax.rsqrt(jnp.mean(xf * xf, axis=-1, keepdims=True) + eps)
    return (y * w.astype(jnp.float32)).astype(x.dtype)


def causal_dwconv(u, w, b):
    k_width = w.shape[0]
    length = u.shape[1]
    up = jnp.pad(u, ((0, 0), (k_width - 1, 0), (0, 0)))
    out = b + up[:, 0:length] * w[0]
    for k in range(1, k_width):
        out = out + up[:, k:k + length] * w[k]
    return out


def segsum_exp(cs):
    t = cs.shape[-1]
    mask = jnp.tril(jnp.ones((t, t), dtype=bool))
    diff = cs[..., :, None] - cs[..., None, :]
    return jnp.exp(jnp.where(mask, diff, -jnp.inf))


def ssd_chunked(xdt, dA, bm, cm):
    b, length, g, r, p = xdt.shape
    n = bm.shape[-1]
    l = SSM_CHUNK
    c = length // l
    xdt = xdt.reshape(b, c, l, g, r, p)
    dA = dA.reshape(b, c, l, g, r)
    bm = bm.reshape(b, c, l, g, n)
    cm = cm.reshape(b, c, l, g, n)

    cs = jnp.cumsum(dA, axis=2)
    lmat = segsum_exp(jnp.moveaxis(cs, 2, -1))
    cb = jnp.einsum('bclgn,bcsgn->bcgls', cm, bm)
    y_diag = jnp.einsum('bcgls,bcgrls,bcsgrp->bclgrp', cb, lmat, xdt)
    decay_states = jnp.exp(cs[:, :, -1:] - cs)
    states = jnp.einsum('bclgn,bclgr,bclgrp->bcgrpn', bm, decay_states, xdt)
    chunk_tot = jnp.moveaxis(cs[:, :, -1], 1, -1)
    chunk_cs = jnp.cumsum(jnp.pad(chunk_tot, ((0, 0), (0, 0), (0, 0), (1, 0))), axis=-1)
    decay_chunk = segsum_exp(chunk_cs)
    states = jnp.concatenate([jnp.zeros_like(states[:, :1]), states], axis=1)
    states_in = jnp.einsum('bgrzc,bcgrpn->bzgrpn', decay_chunk[..., :-1, :], states)
    y_off = jnp.einsum('bclgn,bcgrpn,bclgr->bclgrp', cm, states_in, jnp.exp(cs))
    return (y_diag + y_off).reshape(b, length, g, r, p)


def mamba2_mixer(z, xbc, dt_raw, conv_w, conv_b, dt_bias, a_log, d_skip, norm_w):
    b, length, _ = z.shape
    f32 = jnp.float32
    xbc = jax.nn.silu(causal_dwconv(xbc, conv_w, conv_b))
    xs, bm, cm = jnp.split(xbc, [SSM_D_INNER, SSM_D_INNER + SSM_BC_DIM], axis=-1)
    dt = jax.nn.softplus(dt_raw.astype(f32) + dt_bias.astype(f32))
    a = -jnp.exp(a_log.astype(f32))
    g, r, p, n = SSM_GROUPS, SSM_HEADS_PER_GROUP, SSM_HEAD_DIM, SSM_D_STATE
    xs_h = xs.astype(f32).reshape(b, length, g, r, p)
    dt_h = dt.reshape(b, length, g, r)
    y = ssd_chunked(xs_h * dt_h[..., None], dt_h * a.reshape(g, r),
                    bm.astype(f32).reshape(b, length, g, n),
                    cm.astype(f32).reshape(b, length, g, n))
    y = y + xs_h * d_skip.astype(f32).reshape(g, r, 1)
    y = y.reshape(b, length, SSM_D_INNER) * jax.nn.silu(z.astype(f32))
    yg = y.reshape(b, length, SSM_GROUPS, SSM_NORM_GROUP)
    yg = yg * lax.rsqrt(jnp.mean(yg * yg, axis=-1, keepdims=True) + RMS_EPS)
    y = yg.reshape(b, length, SSM_D_INNER) * norm_w.astype(f32)
    return y.astype(z.dtype)


def rglru_mixer(xl, yl, conv_w, conv_b, w_r, b_r, w_i, b_i, lam):
    b, length, _ = xl.shape
    f32 = jnp.float32
    u = causal_dwconv(xl, conv_w, conv_b)
    ub = u.reshape(b, length, LRU_BLOCKS, LRU_BLOCK)
    r_gate = jax.nn.sigmoid((jnp.einsum('blhi,hij->blhj', ub, w_r).reshape(b, length, LRU_WIDTH) + b_r).astype(f32))
    i_gate = jax.nn.sigmoid((jnp.einsum('blhi,hij->blhj', ub, w_i).reshape(b, length, LRU_WIDTH) + b_i).astype(f32))
    log_a = -LRU_C * r_gate * jax.nn.softplus(-lam.astype(f32))
    a_t = jnp.exp(log_a)
    b_t = jnp.sqrt(-jnp.expm1(2.0 * log_a)) * (i_gate * u.astype(f32))

    def combine(left, right):
        a1, b1 = left
        a2, b2 = right
        return a1 * a2, a2 * b1 + b2

    _, h = lax.associative_scan(combine, (a_t, b_t), axis=1)
    out = h * jax.nn.gelu(yl.astype(f32), approximate=True)
    return out.astype(xl.dtype)


def _fwd_setup_inputs(seed: int = 0) -> dict:
    key = jax.random.key(seed)
    ks = jax.random.split(key, 32)
    f32 = jnp.float32
    nrm = lambda k, shape, scale: jax.random.normal(k, shape, f32) * scale
    L = DEPTH

    x = jax.random.normal(ks[0], (BATCH, SEQ, D_MODEL), f32)
    norm1_w = 1.0 + nrm(ks[1], (L, D_MODEL), 0.05)
    w_in = nrm(ks[2], (L, D_MODEL, IN_PROJ_DIM), D_MODEL ** -0.5)
    b_branch_gate = nrm(ks[3], (L, N_GATES), 0.1)

    ssm_conv_w = nrm(ks[4], (L, SSM_CONV, SSM_CONV_DIM), SSM_CONV ** -0.5)
    ssm_conv_b = nrm(ks[5], (L, SSM_CONV_DIM), 0.02)
    u = jax.random.uniform(ks[6], (L, SSM_HEADS), f32)
    dt0 = jnp.exp(u * (math.log(SSM_DT_MAX) - math.log(SSM_DT_MIN)) + math.log(SSM_DT_MIN))
    dt0 = jnp.maximum(dt0, 1e-4)
    ssm_dt_bias = dt0 + jnp.log(-jnp.expm1(-dt0))
    ssm_a_log = jnp.log(jax.random.uniform(ks[7], (L, SSM_HEADS), f32, 1.0, 16.0))
    ssm_d = 1.0 + nrm(ks[8], (L, SSM_HEADS), 0.1)
    ssm_norm_w = 1.0 + nrm(ks[9], (L, SSM_D_INNER), 0.05)
    w_out_ssm = nrm(ks[10], (L, SSM_D_INNER, D_MODEL), SSM_D_INNER ** -0.5)

    lru_conv_w = nrm(ks[11], (L, LRU_CONV, LRU_WIDTH), LRU_CONV ** -0.5)
    lru_conv_b = nrm(ks[12], (L, LRU_WIDTH), 0.02)
    lru_w_r = nrm(ks[13], (L, LRU_BLOCKS, LRU_BLOCK, LRU_BLOCK), LRU_BLOCK ** -0.5)
    lru_b_r = nrm(ks[14], (L, LRU_WIDTH), 0.02)
    lru_w_i = nrm(ks[15], (L, LRU_BLOCKS, LRU_BLOCK, LRU_BLOCK), LRU_BLOCK ** -0.5)
    lru_b_i = nrm(ks[16], (L, LRU_WIDTH), 0.02)
    a0 = jax.random.uniform(ks[17], (L, LRU_WIDTH), f32, 0.9, 0.999)
    s0 = a0 ** (1.0 / LRU_C)
    lru_lambda = jnp.log(s0) - jnp.log1p(-s0)
    w_out_lru = nrm(ks[18], (L, LRU_WIDTH, D_MODEL), LRU_WIDTH ** -0.5)

    w_out = nrm(ks[19], (L, D_MODEL, D_MODEL), D_MODEL ** -0.5)
    norm2_w = 1.0 + nrm(ks[20], (L, D_MODEL), 0.05)
    w_ffn_in = nrm(ks[21], (L, D_MODEL, 2 * FFN_HIDDEN), D_MODEL ** -0.5)
    w_ffn_out = nrm(ks[22], (L, FFN_HIDDEN, D_MODEL), FFN_HIDDEN ** -0.5)
    norm_f_w = 1.0 + nrm(ks[23], (D_MODEL,), 0.05)

    return {
        "x": x, "norm1_w": norm1_w, "w_in": w_in, "b_branch_gate": b_branch_gate,
        "ssm_conv_w": ssm_conv_w, "ssm_conv_b": ssm_conv_b, "ssm_dt_bias": ssm_dt_bias,
        "ssm_a_log": ssm_a_log, "ssm_d": ssm_d, "ssm_norm_w": ssm_norm_w, "w_out_ssm": w_out_ssm,
        "lru_conv_w": lru_conv_w, "lru_conv_b": lru_conv_b, "lru_w_r": lru_w_r, "lru_b_r": lru_b_r,
        "lru_w_i": lru_w_i, "lru_b_i": lru_b_i, "lru_lambda": lru_lambda, "w_out_lru": w_out_lru,
        "w_out": w_out, "norm2_w": norm2_w, "w_ffn_in": w_ffn_in, "w_ffn_out": w_ffn_out,
        "norm_f_w": norm_f_w,
    }


def _fwd_reference(x, norm1_w, w_in, b_branch_gate, ssm_conv_w, ssm_conv_b, ssm_dt_bias, ssm_a_log,
              ssm_d, ssm_norm_w, w_out_ssm, lru_conv_w, lru_conv_b, lru_w_r, lru_b_r, lru_w_i,
              lru_b_i, lru_lambda, w_out_lru, w_out, norm2_w, w_ffn_in, w_ffn_out, norm_f_w):
    h = x
    for l in range(DEPTH):
        hn = rmsnorm(h, norm1_w[l])
        proj = hn @ w_in[l]
        gates, z, xbc, dt_raw, lru_x, lru_y = jnp.split(proj, IN_SPLITS, axis=-1)
        gates = jax.nn.sigmoid(gates + b_branch_gate[l])
        g_ssm, g_lru = jnp.split(gates, 2, axis=-1)
        y_ssm = mamba2_mixer(z, xbc, dt_raw, ssm_conv_w[l], ssm_conv_b[l], ssm_dt_bias[l],
                             ssm_a_log[l], ssm_d[l], ssm_norm_w[l]) @ w_out_ssm[l]
        y_lru = rglru_mixer(lru_x, lru_y, lru_conv_w[l], lru_conv_b[l], lru_w_r[l], lru_b_r[l],
                            lru_w_i[l], lru_b_i[l], lru_lambda[l]) @ w_out_lru[l]
        h = h + (g_ssm * y_ssm + g_lru * y_lru) @ w_out[l]
        hn = rmsnorm(h, norm2_w[l])
        gate, up = jnp.split(hn @ w_ffn_in[l], 2, axis=-1)
        h = h + (jax.nn.silu(gate) * up) @ w_ffn_out[l]
    return rmsnorm(h, norm_f_w)


import jax as _jax
import jax.numpy as _jnp

TWIN_FORMAT = 'train_step'
FWD_PARAMS = ['x', 'norm1_w', 'w_in', 'b_branch_gate', 'ssm_conv_w', 'ssm_conv_b', 'ssm_dt_bias', 'ssm_a_log', 'ssm_d', 'ssm_norm_w', 'w_out_ssm', 'lru_conv_w', 'lru_conv_b', 'lru_w_r', 'lru_b_r', 'lru_w_i', 'lru_b_i', 'lru_lambda', 'w_out_lru', 'w_out', 'norm2_w', 'w_ffn_in', 'w_ffn_out', 'norm_f_w']
TWIN_WEIGHTS = ['norm1_w', 'w_in', 'b_branch_gate', 'ssm_conv_w', 'ssm_conv_b', 'ssm_dt_bias', 'ssm_a_log', 'ssm_d', 'ssm_norm_w', 'w_out_ssm', 'lru_conv_w', 'lru_conv_b', 'lru_w_r', 'lru_b_r', 'lru_w_i', 'lru_b_i', 'lru_lambda', 'w_out_lru', 'w_out', 'norm2_w', 'w_ffn_in', 'w_ffn_out', 'norm_f_w']
TWIN_DIFF_INPUT = 'x'
TWIN_INPUTS = ['x', 'norm1_w', 'w_in', 'b_branch_gate', 'ssm_conv_w', 'ssm_conv_b', 'ssm_dt_bias', 'ssm_a_log', 'ssm_d', 'ssm_norm_w', 'w_out_ssm', 'lru_conv_w', 'lru_conv_b', 'lru_w_r', 'lru_b_r', 'lru_w_i', 'lru_b_i', 'lru_lambda', 'w_out_lru', 'w_out', 'norm2_w', 'w_ffn_in', 'w_ffn_out', 'norm_f_w', 'loss_target', 'm_norm1_w', 'm_w_in', 'm_b_branch_gate', 'm_ssm_conv_w', 'm_ssm_conv_b', 'm_ssm_dt_bias', 'm_ssm_a_log', 'm_ssm_d', 'm_ssm_norm_w', 'm_w_out_ssm', 'm_lru_conv_w', 'm_lru_conv_b', 'm_lru_w_r', 'm_lru_b_r', 'm_lru_w_i', 'm_lru_b_i', 'm_lru_lambda', 'm_w_out_lru', 'm_w_out', 'm_norm2_w', 'm_w_ffn_in', 'm_w_ffn_out', 'm_norm_f_w', 'v_norm1_w', 'v_w_in', 'v_b_branch_gate', 'v_ssm_conv_w', 'v_ssm_conv_b', 'v_ssm_dt_bias', 'v_ssm_a_log', 'v_ssm_d', 'v_ssm_norm_w', 'v_w_out_ssm', 'v_lru_conv_w', 'v_lru_conv_b', 'v_lru_w_r', 'v_lru_b_r', 'v_lru_w_i', 'v_lru_b_i', 'v_lru_lambda', 'v_w_out_lru', 'v_w_out', 'v_norm2_w', 'v_w_ffn_in', 'v_w_ffn_out', 'v_norm_f_w']
TWIN_OUTPUTS = ['loss', 'grad_x', 'grad_norm1_w', 'grad_w_in', 'grad_b_branch_gate', 'grad_ssm_conv_w', 'grad_ssm_conv_b', 'grad_ssm_dt_bias', 'grad_ssm_a_log', 'grad_ssm_d', 'grad_ssm_norm_w', 'grad_w_out_ssm', 'grad_lru_conv_w', 'grad_lru_conv_b', 'grad_lru_w_r', 'grad_lru_b_r', 'grad_lru_w_i', 'grad_lru_b_i', 'grad_lru_lambda', 'grad_w_out_lru', 'grad_w_out', 'grad_norm2_w', 'grad_w_ffn_in', 'grad_w_ffn_out', 'grad_norm_f_w', 'delta_norm1_w', 'delta_w_in', 'delta_b_branch_gate', 'delta_ssm_conv_w', 'delta_ssm_conv_b', 'delta_ssm_dt_bias', 'delta_ssm_a_log', 'delta_ssm_d', 'delta_ssm_norm_w', 'delta_w_out_ssm', 'delta_lru_conv_w', 'delta_lru_conv_b', 'delta_lru_w_r', 'delta_lru_b_r', 'delta_lru_w_i', 'delta_lru_b_i', 'delta_lru_lambda', 'delta_w_out_lru', 'delta_w_out', 'delta_norm2_w', 'delta_w_ffn_in', 'delta_w_ffn_out', 'delta_norm_f_w', 'new_m_norm1_w', 'new_m_w_in', 'new_m_b_branch_gate', 'new_m_ssm_conv_w', 'new_m_ssm_conv_b', 'new_m_ssm_dt_bias', 'new_m_ssm_a_log', 'new_m_ssm_d', 'new_m_ssm_norm_w', 'new_m_w_out_ssm', 'new_m_lru_conv_w', 'new_m_lru_conv_b', 'new_m_lru_w_r', 'new_m_lru_b_r', 'new_m_lru_w_i', 'new_m_lru_b_i', 'new_m_lru_lambda', 'new_m_w_out_lru', 'new_m_w_out', 'new_m_norm2_w', 'new_m_w_ffn_in', 'new_m_w_ffn_out', 'new_m_norm_f_w', 'new_v_norm1_w', 'new_v_w_in', 'new_v_b_branch_gate', 'new_v_ssm_conv_w', 'new_v_ssm_conv_b', 'new_v_ssm_dt_bias', 'new_v_ssm_a_log', 'new_v_ssm_d', 'new_v_ssm_norm_w', 'new_v_w_out_ssm', 'new_v_lru_conv_w', 'new_v_lru_conv_b', 'new_v_lru_w_r', 'new_v_lru_b_r', 'new_v_lru_w_i', 'new_v_lru_b_i', 'new_v_lru_lambda', 'new_v_w_out_lru', 'new_v_w_out', 'new_v_norm2_w', 'new_v_w_ffn_in', 'new_v_w_ffn_out', 'new_v_norm_f_w']
TWIN_LEAF_KINDS = {'loss': 'loss', 'grad_x': 'grad_x', 'grad_norm1_w': 'grad_w', 'grad_w_in': 'grad_w', 'grad_b_branch_gate': 'grad_w', 'grad_ssm_conv_w': 'grad_w', 'grad_ssm_conv_b': 'grad_w', 'grad_ssm_dt_bias': 'grad_w', 'grad_ssm_a_log': 'grad_w', 'grad_ssm_d': 'grad_w', 'grad_ssm_norm_w': 'grad_w', 'grad_w_out_ssm': 'grad_w', 'grad_lru_conv_w': 'grad_w', 'grad_lru_conv_b': 'grad_w', 'grad_lru_w_r': 'grad_w', 'grad_lru_b_r': 'grad_w', 'grad_lru_w_i': 'grad_w', 'grad_lru_b_i': 'grad_w', 'grad_lru_lambda': 'grad_w', 'grad_w_out_lru': 'grad_w', 'grad_w_out': 'grad_w', 'grad_norm2_w': 'grad_w', 'grad_w_ffn_in': 'grad_w', 'grad_w_ffn_out': 'grad_w', 'grad_norm_f_w': 'grad_w', 'delta_norm1_w': 'delta_w', 'delta_w_in': 'delta_w', 'delta_b_branch_gate': 'delta_w', 'delta_ssm_conv_w': 'delta_w', 'delta_ssm_conv_b': 'delta_w', 'delta_ssm_dt_bias': 'delta_w', 'delta_ssm_a_log': 'delta_w', 'delta_ssm_d': 'delta_w', 'delta_ssm_norm_w': 'delta_w', 'delta_w_out_ssm': 'delta_w', 'delta_lru_conv_w': 'delta_w', 'delta_lru_conv_b': 'delta_w', 'delta_lru_w_r': 'delta_w', 'delta_lru_b_r': 'delta_w', 'delta_lru_w_i': 'delta_w', 'delta_lru_b_i': 'delta_w', 'delta_lru_lambda': 'delta_w', 'delta_w_out_lru': 'delta_w', 'delta_w_out': 'delta_w', 'delta_norm2_w': 'delta_w', 'delta_w_ffn_in': 'delta_w', 'delta_w_ffn_out': 'delta_w', 'delta_norm_f_w': 'delta_w', 'new_m_norm1_w': 'new_m', 'new_m_w_in': 'new_m', 'new_m_b_branch_gate': 'new_m', 'new_m_ssm_conv_w': 'new_m', 'new_m_ssm_conv_b': 'new_m', 'new_m_ssm_dt_bias': 'new_m', 'new_m_ssm_a_log': 'new_m', 'new_m_ssm_d': 'new_m', 'new_m_ssm_norm_w': 'new_m', 'new_m_w_out_ssm': 'new_m', 'new_m_lru_conv_w': 'new_m', 'new_m_lru_conv_b': 'new_m', 'new_m_lru_w_r': 'new_m', 'new_m_lru_b_r': 'new_m', 'new_m_lru_w_i': 'new_m', 'new_m_lru_b_i': 'new_m', 'new_m_lru_lambda': 'new_m', 'new_m_w_out_lru': 'new_m', 'new_m_w_out': 'new_m', 'new_m_norm2_w': 'new_m', 'new_m_w_ffn_in': 'new_m', 'new_m_w_ffn_out': 'new_m', 'new_m_norm_f_w': 'new_m', 'new_v_norm1_w': 'new_v', 'new_v_w_in': 'new_v', 'new_v_b_branch_gate': 'new_v', 'new_v_ssm_conv_w': 'new_v', 'new_v_ssm_conv_b': 'new_v', 'new_v_ssm_dt_bias': 'new_v', 'new_v_ssm_a_log': 'new_v', 'new_v_ssm_d': 'new_v', 'new_v_ssm_norm_w': 'new_v', 'new_v_w_out_ssm': 'new_v', 'new_v_lru_conv_w': 'new_v', 'new_v_lru_conv_b': 'new_v', 'new_v_lru_w_r': 'new_v', 'new_v_lru_b_r': 'new_v', 'new_v_lru_w_i': 'new_v', 'new_v_lru_b_i': 'new_v', 'new_v_lru_lambda': 'new_v', 'new_v_w_out_lru': 'new_v', 'new_v_w_out': 'new_v', 'new_v_norm2_w': 'new_v', 'new_v_w_ffn_in': 'new_v', 'new_v_w_ffn_out': 'new_v', 'new_v_norm_f_w': 'new_v'}


def _forward(args):
    return _fwd_reference(*[args[k] for k in FWD_PARAMS])


def _output_shape():
    def fwd():
        inp = _fwd_setup_inputs(0)
        return _fwd_reference(*[inp[k] for k in FWD_PARAMS])
    out = _jax.eval_shape(fwd)
    return out.shape, out.dtype

N_MICROBATCH = 1
ADAM_LR = 0.001
ADAM_B1 = 0.9
ADAM_B2 = 0.999
ADAM_EPS = 1e-08
ADAM_WD = 0.01
ADAM_STEP = 10
PER_EXAMPLE_BATCH_AXIS = {'x': 0, 'loss_target': 0}
SHARED_INPUTS = []
_WEIGHT_DTYPES = {'norm1_w': _jnp.float32, 'w_in': _jnp.float32, 'b_branch_gate': _jnp.float32, 'ssm_conv_w': _jnp.float32, 'ssm_conv_b': _jnp.float32, 'ssm_dt_bias': _jnp.float32, 'ssm_a_log': _jnp.float32, 'ssm_d': _jnp.float32, 'ssm_norm_w': _jnp.float32, 'w_out_ssm': _jnp.float32, 'lru_conv_w': _jnp.float32, 'lru_conv_b': _jnp.float32, 'lru_w_r': _jnp.float32, 'lru_b_r': _jnp.float32, 'lru_w_i': _jnp.float32, 'lru_b_i': _jnp.float32, 'lru_lambda': _jnp.float32, 'w_out_lru': _jnp.float32, 'w_out': _jnp.float32, 'norm2_w': _jnp.float32, 'w_ffn_in': _jnp.float32, 'w_ffn_out': _jnp.float32, 'norm_f_w': _jnp.float32}
MOMENT_SCALE = {'norm1_w': 1.584702e-01, 'w_in': 4.955116e-02, 'b_branch_gate': 2.944939e-02, 'ssm_conv_w': 5.568700e-02, 'ssm_conv_b': 8.314750e-02, 'ssm_dt_bias': 1.476594e-01, 'ssm_a_log': 2.236775e-01, 'ssm_d': 3.789816e-01, 'ssm_norm_w': 6.681558e-02, 'w_out_ssm': 9.377434e-02, 'lru_conv_w': 3.798657e-02, 'lru_conv_b': 4.117560e-01, 'lru_w_r': 1.102450e-02, 'lru_b_r': 1.017064e-02, 'lru_w_i': 1.972095e-02, 'lru_b_i': 1.495121e-02, 'lru_lambda': 1.795846e-02, 'w_out_lru': 4.053073e-02, 'w_out': 1.026939e-01, 'norm2_w': 1.146136e-01, 'w_ffn_in': 5.037046e-02, 'w_ffn_out': 8.302584e-02, 'norm_f_w': 3.199003e+01}


def _to_microbatches(a, axis):
    t = _jnp.moveaxis(a, axis, 0)
    t = t.reshape((N_MICROBATCH, t.shape[0] // N_MICROBATCH) + t.shape[1:])
    return _jnp.moveaxis(t, 1, axis + 1)


def setup_inputs(seed: int = 0) -> dict:
    inp = _fwd_setup_inputs(seed)
    key = _jax.random.fold_in(_jax.random.key(seed), 7919)
    shape, _ = _output_shape()
    out = dict(inp)
    out["loss_target"] = _jax.random.normal(_jax.random.fold_in(key, 0), shape, _jnp.float32)
    for i, name in enumerate(TWIN_WEIGHTS):
        w = inp[name].astype(_jnp.float32)
        if MOMENT_SCALE is None:
            s = _jnp.sqrt(_jnp.mean(_jnp.square(w)) + 1e-30)
        else:
            s = MOMENT_SCALE[name]
        km, kv = _jax.random.split(_jax.random.fold_in(key, i + 1))
        out[name] = w
        out["m_" + name] = s * _jax.random.normal(km, w.shape, _jnp.float32)
        out["v_" + name] = (s * s) * _jax.random.uniform(kv, w.shape, _jnp.float32, 0.5, 1.5)
    if N_MICROBATCH > 1:
        for name, axis in PER_EXAMPLE_BATCH_AXIS.items():
            out[name] = _to_microbatches(out[name], axis)
    return {'x': out['x'], 'norm1_w': out['norm1_w'], 'w_in': out['w_in'], 'b_branch_gate': out['b_branch_gate'], 'ssm_conv_w': out['ssm_conv_w'], 'ssm_conv_b': out['ssm_conv_b'], 'ssm_dt_bias': out['ssm_dt_bias'], 'ssm_a_log': out['ssm_a_log'], 'ssm_d': out['ssm_d'], 'ssm_norm_w': out['ssm_norm_w'], 'w_out_ssm': out['w_out_ssm'], 'lru_conv_w': out['lru_conv_w'], 'lru_conv_b': out['lru_conv_b'], 'lru_w_r': out['lru_w_r'], 'lru_b_r': out['lru_b_r'], 'lru_w_i': out['lru_w_i'], 'lru_b_i': out['lru_b_i'], 'lru_lambda': out['lru_lambda'], 'w_out_lru': out['w_out_lru'], 'w_out': out['w_out'], 'norm2_w': out['norm2_w'], 'w_ffn_in': out['w_ffn_in'], 'w_ffn_out': out['w_ffn_out'], 'norm_f_w': out['norm_f_w'], 'loss_target': out['loss_target'], 'm_norm1_w': out['m_norm1_w'], 'm_w_in': out['m_w_in'], 'm_b_branch_gate': out['m_b_branch_gate'], 'm_ssm_conv_w': out['m_ssm_conv_w'], 'm_ssm_conv_b': out['m_ssm_conv_b'], 'm_ssm_dt_bias': out['m_ssm_dt_bias'], 'm_ssm_a_log': out['m_ssm_a_log'], 'm_ssm_d': out['m_ssm_d'], 'm_ssm_norm_w': out['m_ssm_norm_w'], 'm_w_out_ssm': out['m_w_out_ssm'], 'm_lru_conv_w': out['m_lru_conv_w'], 'm_lru_conv_b': out['m_lru_conv_b'], 'm_lru_w_r': out['m_lru_w_r'], 'm_lru_b_r': out['m_lru_b_r'], 'm_lru_w_i': out['m_lru_w_i'], 'm_lru_b_i': out['m_lru_b_i'], 'm_lru_lambda': out['m_lru_lambda'], 'm_w_out_lru': out['m_w_out_lru'], 'm_w_out': out['m_w_out'], 'm_norm2_w': out['m_norm2_w'], 'm_w_ffn_in': out['m_w_ffn_in'], 'm_w_ffn_out': out['m_w_ffn_out'], 'm_norm_f_w': out['m_norm_f_w'], 'v_norm1_w': out['v_norm1_w'], 'v_w_in': out['v_w_in'], 'v_b_branch_gate': out['v_b_branch_gate'], 'v_ssm_conv_w': out['v_ssm_conv_w'], 'v_ssm_conv_b': out['v_ssm_conv_b'], 'v_ssm_dt_bias': out['v_ssm_dt_bias'], 'v_ssm_a_log': out['v_ssm_a_log'], 'v_ssm_d': out['v_ssm_d'], 'v_ssm_norm_w': out['v_ssm_norm_w'], 'v_w_out_ssm': out['v_w_out_ssm'], 'v_lru_conv_w': out['v_lru_conv_w'], 'v_lru_conv_b': out['v_lru_conv_b'], 'v_lru_w_r': out['v_lru_w_r'], 'v_lru_b_r': out['v_lru_b_r'], 'v_lru_w_i': out['v_lru_w_i'], 'v_lru_b_i': out['v_lru_b_i'], 'v_lru_lambda': out['v_lru_lambda'], 'v_w_out_lru': out['v_w_out_lru'], 'v_w_out': out['v_w_out'], 'v_norm2_w': out['v_norm2_w'], 'v_w_ffn_in': out['v_w_ffn_in'], 'v_w_ffn_out': out['v_w_ffn_out'], 'v_norm_f_w': out['v_norm_f_w']}


def _loss(weights, diff, rest, loss_target):
    with _jax.named_scope("forward"):
        args = {**rest, TWIN_DIFF_INPUT: diff, **{k: w.astype(_WEIGHT_DTYPES[k]) for k, w in weights.items()}}
        y = _forward(args)
    with _jax.named_scope("loss_head"):
        err = _jnp.square(y.astype(_jnp.float32) - loss_target)
        return 0.5 * _jnp.sum(_jnp.mean(err, axis=-1)) if err.ndim else 0.5 * err


def _adamw(w, g, m, v):
    m = ADAM_B1 * m + (1.0 - ADAM_B1) * g
    v = ADAM_B2 * v + (1.0 - ADAM_B2) * _jnp.square(g)
    m_hat = m / (1.0 - ADAM_B1 ** ADAM_STEP)
    v_hat = v / (1.0 - ADAM_B2 ** ADAM_STEP)
    delta = -ADAM_LR * (m_hat / (_jnp.sqrt(v_hat) + ADAM_EPS) + ADAM_WD * w)
    return delta, m, v


def reference(x, norm1_w, w_in, b_branch_gate, ssm_conv_w, ssm_conv_b, ssm_dt_bias, ssm_a_log, ssm_d, ssm_norm_w, w_out_ssm, lru_conv_w, lru_conv_b, lru_w_r, lru_b_r, lru_w_i, lru_b_i, lru_lambda, w_out_lru, w_out, norm2_w, w_ffn_in, w_ffn_out, norm_f_w, loss_target, m_norm1_w, m_w_in, m_b_branch_gate, m_ssm_conv_w, m_ssm_conv_b, m_ssm_dt_bias, m_ssm_a_log, m_ssm_d, m_ssm_norm_w, m_w_out_ssm, m_lru_conv_w, m_lru_conv_b, m_lru_w_r, m_lru_b_r, m_lru_w_i, m_lru_b_i, m_lru_lambda, m_w_out_lru, m_w_out, m_norm2_w, m_w_ffn_in, m_w_ffn_out, m_norm_f_w, v_norm1_w, v_w_in, v_b_branch_gate, v_ssm_conv_w, v_ssm_conv_b, v_ssm_dt_bias, v_ssm_a_log, v_ssm_d, v_ssm_norm_w, v_w_out_ssm, v_lru_conv_w, v_lru_conv_b, v_lru_w_r, v_lru_b_r, v_lru_w_i, v_lru_b_i, v_lru_lambda, v_w_out_lru, v_w_out, v_norm2_w, v_w_ffn_in, v_w_ffn_out, v_norm_f_w):
    given = dict(x=x, norm1_w=norm1_w, w_in=w_in, b_branch_gate=b_branch_gate, ssm_conv_w=ssm_conv_w, ssm_conv_b=ssm_conv_b, ssm_dt_bias=ssm_dt_bias, ssm_a_log=ssm_a_log, ssm_d=ssm_d, ssm_norm_w=ssm_norm_w, w_out_ssm=w_out_ssm, lru_conv_w=lru_conv_w, lru_conv_b=lru_conv_b, lru_w_r=lru_w_r, lru_b_r=lru_b_r, lru_w_i=lru_w_i, lru_b_i=lru_b_i, lru_lambda=lru_lambda, w_out_lru=w_out_lru, w_out=w_out, norm2_w=norm2_w, w_ffn_in=w_ffn_in, w_ffn_out=w_ffn_out, norm_f_w=norm_f_w, loss_target=loss_target, m_norm1_w=m_norm1_w, m_w_in=m_w_in, m_b_branch_gate=m_b_branch_gate, m_ssm_conv_w=m_ssm_conv_w, m_ssm_conv_b=m_ssm_conv_b, m_ssm_dt_bias=m_ssm_dt_bias, m_ssm_a_log=m_ssm_a_log, m_ssm_d=m_ssm_d, m_ssm_norm_w=m_ssm_norm_w, m_w_out_ssm=m_w_out_ssm, m_lru_conv_w=m_lru_conv_w, m_lru_conv_b=m_lru_conv_b, m_lru_w_r=m_lru_w_r, m_lru_b_r=m_lru_b_r, m_lru_w_i=m_lru_w_i, m_lru_b_i=m_lru_b_i, m_lru_lambda=m_lru_lambda, m_w_out_lru=m_w_out_lru, m_w_out=m_w_out, m_norm2_w=m_norm2_w, m_w_ffn_in=m_w_ffn_in, m_w_ffn_out=m_w_ffn_out, m_norm_f_w=m_norm_f_w, v_norm1_w=v_norm1_w, v_w_in=v_w_in, v_b_branch_gate=v_b_branch_gate, v_ssm_conv_w=v_ssm_conv_w, v_ssm_conv_b=v_ssm_conv_b, v_ssm_dt_bias=v_ssm_dt_bias, v_ssm_a_log=v_ssm_a_log, v_ssm_d=v_ssm_d, v_ssm_norm_w=v_ssm_norm_w, v_w_out_ssm=v_w_out_ssm, v_lru_conv_w=v_lru_conv_w, v_lru_conv_b=v_lru_conv_b, v_lru_w_r=v_lru_w_r, v_lru_b_r=v_lru_b_r, v_lru_w_i=v_lru_w_i, v_lru_b_i=v_lru_b_i, v_lru_lambda=v_lru_lambda, v_w_out_lru=v_w_out_lru, v_w_out=v_w_out, v_norm2_w=v_norm2_w, v_w_ffn_in=v_w_ffn_in, v_w_ffn_out=v_w_ffn_out, v_norm_f_w=v_norm_f_w)
    weights = {n: given[n] for n in TWIN_WEIGHTS}
    shared = {n: given[n] for n in SHARED_INPUTS}
    per_example = {n: given[n] for n in ['x']}
    grad_fn = _jax.value_and_grad(_loss, argnums=(0, 1))

    def one_microbatch(ex, loss_target):
        ex = dict(ex)
        diff = ex.pop(TWIN_DIFF_INPUT)
        return grad_fn(weights, diff, {**shared, **ex}, loss_target)

    if N_MICROBATCH == 1:
        loss, (grad_w, grad_x) = one_microbatch(per_example, given["loss_target"])
    else:
        def body(carry, xs):
            loss_sum, grad_sum = carry
            l_k, (gw_k, gx_k) = one_microbatch(xs[0], xs[1])
            with _jax.named_scope("update"):
                return (loss_sum + l_k, _jax.tree.map(_jnp.add, grad_sum, gw_k)), gx_k

        init = (_jnp.zeros((), _jnp.float32), _jax.tree.map(_jnp.zeros_like, weights))
        (loss, grad_w), grad_x = _jax.lax.scan(body, init, (per_example, given["loss_target"]))
    with _jax.named_scope("update"):
        delta_w, new_m, new_v = {}, {}, {}
        for n in TWIN_WEIGHTS:
            delta_w[n], new_m[n], new_v[n] = _adamw(weights[n], grad_w[n], given["m_" + n], given["v_" + n])
    return (loss, grad_x, *[grad_w[n] for n in TWIN_WEIGHTS], *[delta_w[n] for n in TWIN_WEIGHTS],
            *[new_m[n] for n in TWIN_WEIGHTS], *[new_v[n] for n in TWIN_WEIGHTS])
```

```python
import functools
import math

import jax
import jax.numpy as jnp
from jax import lax
from jax.experimental import pallas as pl
from jax.experimental.pallas import tpu as pltpu

f32 = jnp.float32
bf16 = jnp.bfloat16

D_MODEL = 1024
SSM_D_INNER = 2048
SSM_HEADS = 32
SSM_HEAD_DIM = 64
SSM_GROUPS = 4
SSM_HPG = 8
SSM_D_STATE = 128
SSM_CHUNK = 128
SSM_GROUP_W = 512
SSM_CONV_DIM = 3072
XBC_GROUP_W = 768
LRU_WIDTH = 1280
LRU_BLOCKS = 10
LRU_BLOCK = 128
LRU_C = 8.0
FFN_HIDDEN = 2816
RMS_EPS = 1e-6
IN_PROJ_DIM = 9760
N_CHIPS = 4

OFF_XBC = 0
OFF_GATES = 3072
OFF_Z = 5120
OFF_LX = 7168
OFF_LY = 8448
OFF_DT = 9728
PROJ_W = 9856

ADAM_LR = 0.001
ADAM_B1 = 0.9
ADAM_B2 = 0.999
ADAM_EPS = 1e-08
ADAM_WD = 0.01
ADAM_STEP = 10

MESH = pl.DeviceIdType.MESH
ANY = pl.BlockSpec(memory_space=pl.ANY)
HI = lax.Precision.HIGHEST

NN = (((1,), (0,)), ((), ()))
NT = (((1,), (1,)), ((), ()))
TN = (((0,), (0,)), ((), ()))


def _pick(n, cap, mult=128):
    best = None
    for t in range(mult, min(n, cap) + 1, mult):
        if n % t == 0:
            best = t
    return best if best is not None else n


def _sigmoid(x):
    return 1.0 / (1.0 + jnp.exp(-x))


def _softplus(x):
    return jnp.maximum(x, 0.0) + jnp.log(1.0 + jnp.exp(-jnp.abs(x)))


def _silu(x):
    return x * _sigmoid(x)


def _dsilu(x):
    s = _sigmoid(x)
    return s * (1.0 + x * (1.0 - s))


_GELU_K = math.sqrt(2.0 / math.pi)


def _gelu(x):
    return 0.5 * x * (1.0 + jnp.tanh(_GELU_K * (x + 0.044715 * x * x * x)))


def _dgelu(x):
    t = jnp.tanh(_GELU_K * (x + 0.044715 * x * x * x))
    return 0.5 * (1.0 + t) + 0.5 * x * (1.0 - t * t) * _GELU_K * (1.0 + 3.0 * 0.044715 * x * x)


def _expm1(x):
    poly = x * (1.0 + x * (0.5 + x * (1.0 / 6.0 + x * (1.0 / 24.0 + x * (1.0 / 120.0 + x * (1.0 / 720.0))))))
    return jnp.where(jnp.abs(x) < 0.1, poly, jnp.exp(x) - 1.0)


def _dot(a, b, dn):
    return lax.dot_general(a.astype(bf16), b.astype(bf16), dn, preferred_element_type=f32)


def _dot_hi(a, b, dn):
    return lax.dot_general(a, b, dn, preferred_element_type=f32, precision=HI)


def mm(a, b, mode, *, name, add=None, out_dtype=f32):
    if mode == "nn":
        (m, k), (k2, n) = a.shape, b.shape
    elif mode == "nt":
        (m, k), (n, k2) = a.shape, b.shape
    else:
        (k, m), (k2, n) = a.shape, b.shape
    assert k == k2, (a.shape, b.shape, mode)
    tm, tn, tk = _pick(m, 512), _pick(n, 1024), _pick(k, 1024)
    nk = k // tk
    dn = {"nn": NN, "nt": NT, "tn": TN}[mode]
    a_spec = pl.BlockSpec((tk, tm), lambda i, j, kk: (kk, i)) if mode == "tn" else pl.BlockSpec((tm, tk), lambda i, j, kk: (i, kk))
    b_spec = pl.BlockSpec((tn, tk), lambda i, j, kk: (j, kk)) if mode == "nt" else pl.BlockSpec((tk, tn), lambda i, j, kk: (kk, j))
    o_spec = pl.BlockSpec((tm, tn), lambda i, j, kk: (i, j))
    has_add = add is not None

    def body(a_ref, b_ref, *rest):
        if has_add:
            add_ref, o_ref, acc = rest
        else:
            o_ref, acc = rest
        kk = pl.program_id(2)

        @pl.when(kk == 0)
        def _():
            acc[...] = jnp.zeros_like(acc)

        acc[...] += _dot(a_ref[...], b_ref[...], dn)

        @pl.when(kk == nk - 1)
        def _():
            r = acc[...]
            if has_add:
                r = r + add_ref[...]
            o_ref[...] = r.astype(out_dtype)

    ins = [a, b] + ([add] if has_add else [])
    in_specs = [a_spec, b_spec] + ([o_spec] if has_add else [])
    return pl.pallas_call(
        body, name=name, grid=(m // tm, n // tn, nk), in_specs=in_specs, out_specs=o_spec,
        out_shape=jax.ShapeDtypeStruct((m, n), out_dtype), scratch_shapes=[pltpu.VMEM((tm, tn), f32)],
        compiler_params=pltpu.CompilerParams(dimension_semantics=("parallel", "parallel", "arbitrary")),
    )(*ins)


def rms_fwd(x, w, *, name):
    t, d = x.shape
    tr = _pick(t, 256, 8)

    def body(x_ref, w_ref, o_ref):
        xv = x_ref[...]
        r = lax.rsqrt(jnp.mean(xv * xv, axis=-1, keepdims=True) + RMS_EPS)
        o_ref[...] = (xv * r * w_ref[...]).astype(bf16)

    return pl.pallas_call(
        body, name=name, grid=(t // tr,),
        in_specs=[pl.BlockSpec((tr, d), lambda i: (i, 0)), pl.BlockSpec((1, d), lambda i: (0, 0))],
        out_specs=pl.BlockSpec((tr, d), lambda i: (i, 0)), out_shape=jax.ShapeDtypeStruct((t, d), bf16),
    )(x, w)


def _rms_bwd_math(xv, wv, dy):
    r = lax.rsqrt(jnp.mean(xv * xv, axis=-1, keepdims=True) + RMS_EPS)
    g = dy * wv
    dx = r * g - xv * (r * r * r) * jnp.mean(g * xv, axis=-1, keepdims=True)
    dw = jnp.sum(dy * xv * r, axis=0, keepdims=True)
    return dx, dw


def rms_bwd(x, w, dy, res, *, name):
    t, d = x.shape
    tr = _pick(t, 256, 8)

    def body(x_ref, w_ref, dy_ref, res_ref, dx_ref, dw_ref):
        dx, dw = _rms_bwd_math(x_ref[...], w_ref[...], dy_ref[...])
        dx_ref[...] = dx + res_ref[...]

        @pl.when(pl.program_id(0) == 0)
        def _():
            dw_ref[...] = jnp.zeros_like(dw_ref)

        dw_ref[...] += dw

    row = pl.BlockSpec((tr, d), lambda i: (i, 0))
    vec = pl.BlockSpec((1, d), lambda i: (0, 0))
    return pl.pallas_call(
        body, name=name, grid=(t // tr,), in_specs=[row, vec, row, row], out_specs=[row, vec],
        out_shape=[jax.ShapeDtypeStruct((t, d), f32), jax.ShapeDtypeStruct((1, d), f32)],
        compiler_params=pltpu.CompilerParams(dimension_semantics=("arbitrary",)),
    )(x, w, dy, res)


def loss_head(h, w, target, *, name):
    t, d = h.shape
    tr = _pick(t, 256, 8)

    def body(h_ref, w_ref, t_ref, loss_ref, dh_ref, dw_ref):
        xv, wv = h_ref[...], w_ref[...]
        r = lax.rsqrt(jnp.mean(xv * xv, axis=-1, keepdims=True) + RMS_EPS)
        err = xv * r * wv - t_ref[...]
        part = 0.5 * jnp.sum(jnp.mean(err * err, axis=-1, keepdims=True), axis=0, keepdims=True)
        dx, dw = _rms_bwd_math(xv, wv, err * (1.0 / d))
        dh_ref[...] = dx

        @pl.when(pl.program_id(0) == 0)
        def _():
            dw_ref[...] = jnp.zeros_like(dw_ref)
            loss_ref[...] = jnp.zeros_like(loss_ref)

        dw_ref[...] += dw
        loss_ref[...] += part

    row = pl.BlockSpec((tr, d), lambda i: (i, 0))
    vec = pl.BlockSpec((1, d), lambda i: (0, 0))
    return pl.pallas_call(
        body, name=name, grid=(t // tr,), in_specs=[row, vec, row],
        out_specs=[pl.BlockSpec((8, 128), lambda i: (0, 0)), row, vec],
        out_shape=[jax.ShapeDtypeStruct((8, 128), f32), jax.ShapeDtypeStruct((t, d), f32), jax.ShapeDtypeStruct((1, d), f32)],
        compiler_params=pltpu.CompilerParams(dimension_semantics=("arbitrary",)),
    )(h, w, target)


CONV_ROWS = 256


def conv_fwd(src, col0, width, w, b, *, silu, name):
    t = src.shape[0]
    tc = _pick(width, 512)
    assert col0 % tc == 0
    cb = col0 // tc
    r = CONV_ROWS

    def body(u_ref, w_ref, b_ref, *rest):
        ext = rest[-1]
        j = pl.program_id(1)

        @pl.when(j == 0)
        def _():
            ext[0:8, :] = jnp.zeros((8, tc), f32)

        @pl.when(j > 0)
        def _():
            ext[0:8, :] = ext[r:r + 8, :]

        ext[8:r + 8, :] = u_ref[...]
        v = ext[...]
        wv = w_ref[...]
        acc = b_ref[...] + wv[3:4, :] * v
        for s in (1, 2, 3):
            acc = acc + wv[3 - s:4 - s, :] * pltpu.roll(v, s, 0)
        pre = acc[8:, :]
        rest[0][...] = pre
        if silu:
            rest[1][...] = _silu(pre)

    tile = pl.BlockSpec((r, tc), lambda c, j: (j, c))
    n_out = 2 if silu else 1
    return pl.pallas_call(
        body, name=name, grid=(width // tc, t // r),
        in_specs=[pl.BlockSpec((r, tc), lambda c, j: (j, cb + c)), pl.BlockSpec((4, tc), lambda c, j: (0, c)),
                  pl.BlockSpec((1, tc), lambda c, j: (0, c))],
        out_specs=[tile] * n_out, out_shape=[jax.ShapeDtypeStruct((t, width), f32)] * n_out,
        scratch_shapes=[pltpu.VMEM((r + 8, tc), f32)],
        compiler_params=pltpu.CompilerParams(dimension_semantics=("parallel", "arbitrary")),
    )(src, w, b)


def conv_bwd(dpost, pre, src, col0, w, *, name):
    t, width = dpost.shape
    tc = _pick(width, 512)
    assert col0 % tc == 0
    cb = col0 // tc
    r = CONV_ROWS
    nt = t // r
    has_pre = pre is not None

    def body(*refs):
        if has_pre:
            d_ref, p_ref, u_ref, w_ref, du_ref, dw_ref, db_ref, ext = refs
        else:
            d_ref, u_ref, w_ref, du_ref, dw_ref, db_ref, ext = refs
        j = pl.program_id(1)

        @pl.when(j == 0)
        def _():
            ext[r:r + 8, :] = jnp.zeros((8, tc), f32)
            dw_ref[...] = jnp.zeros_like(dw_ref)
            db_ref[...] = jnp.zeros_like(db_ref)

        @pl.when(j > 0)
        def _():
            ext[r:r + 8, :] = ext[0:8, :]

        dpre = d_ref[...]
        if has_pre:
            dpre = dpre * _dsilu(p_ref[...])
        ext[0:r, :] = dpre
        v = ext[...]
        wv = w_ref[...]
        uv = u_ref[...]
        du = wv[3:4, :] * dpre
        dw_ref[3:4, :] += jnp.sum(dpre * uv, axis=0, keepdims=True)
        for s in (1, 2, 3):
            sh = pltpu.roll(v, r + 8 - s, 0)[0:r, :]
            du = du + wv[3 - s:4 - s, :] * sh
            dw_ref[3 - s:4 - s, :] += jnp.sum(sh * uv, axis=0, keepdims=True)
        db_ref[...] += jnp.sum(dpre, axis=0, keepdims=True)
        du_ref[...] = du.astype(bf16)

    rev = pl.BlockSpec((r, tc), lambda c, j: (nt - 1 - j, c))
    in_specs = [rev] + ([rev] if has_pre else []) + [pl.BlockSpec((r, tc), lambda c, j: (nt - 1 - j, cb + c)),
                                                     pl.BlockSpec((4, tc), lambda c, j: (0, c))]
    ins = [dpost] + ([pre] if has_pre else []) + [src, w]
    return pl.pallas_call(
        body, name=name, grid=(width // tc, nt), in_specs=in_specs,
        out_specs=[rev, pl.BlockSpec((4, tc), lambda c, j: (0, c)), pl.BlockSpec((1, tc), lambda c, j: (0, c))],
        out_shape=[jax.ShapeDtypeStruct((t, width), bf16), jax.ShapeDtypeStruct((4, width), f32),
                   jax.ShapeDtypeStruct((1, width), f32)],
        scratch_shapes=[pltpu.VMEM((r + 8, tc), f32)],
        compiler_params=pltpu.CompilerParams(dimension_semantics=("parallel", "arbitrary")),
    )(*ins)


def _ssd_common(xbc_ref, dtr_ref, dtrT_ref, par_row_ref, par_col_ref):
    l = SSM_CHUNK
    x = xbc_ref[:, 0:SSM_GROUP_W]
    bm = xbc_ref[:, SSM_GROUP_W:SSM_GROUP_W + SSM_D_STATE]
    cm = xbc_ref[:, SSM_GROUP_W + SSM_D_STATE:XBC_GROUP_W]
    par_row = par_row_ref[0]
    par_col = par_col_ref[0]
    bias_row, alog_row, d_row = par_row[0:1, :], par_row[1:2, :], par_row[2:3, :]
    bias_col, alog_col = par_col[:, 0:1], par_col[:, 1:2]
    dtr = dtr_ref[0]
    dt = _softplus(dtr + bias_row)
    dt_t = _softplus(dtrT_ref[0] + bias_col)
    a_row = -jnp.exp(alog_row)
    a_col = -jnp.exp(alog_col)
    li = lax.broadcasted_iota(jnp.int32, (l, l), 0)
    si = lax.broadcasted_iota(jnp.int32, (l, l), 1)
    tri = (li >= si).astype(f32)
    cs = _dot_hi(tri, dt * a_row, NN)
    cs_t = _dot_hi(dt_t * a_col, tri, NT)
    g = _dot(cm, bm, NT)
    return x, bm, cm, dtr, dt, a_row, d_row, bias_row, tri, li, si, cs, cs_t, g


def ssd_fwd(xbc, dtr, dtr_t, par_row, par_col, *, name):
    t = xbc.shape[0]
    nc = t // SSM_CHUNK
    l, p = SSM_CHUNK, SSM_HEAD_DIM

    def body(xbc_ref, dtr_ref, dtrT_ref, prow_ref, pcol_ref, y_ref, sin_ref, state):
        @pl.when(pl.program_id(1) == 0)
        def _():
            state[...] = jnp.zeros_like(state)

        x, bm, cm, _, dt, _, d_row, _, tri, li, si, cs, cs_t, g = _ssd_common(xbc_ref, dtr_ref, dtrT_ref, prow_ref, pcol_ref)
        s_all = state[...]
        sin_ref[0] = s_all
        for r in range(SSM_HPG):
            sl = slice(r * p, (r + 1) * p)
            xh = x[:, sl]
            xd = xh * dt[:, r:r + 1]
            csc, csr = cs[:, r:r + 1], cs_t[r:r + 1, :]
            lm = jnp.where(li >= si, jnp.exp(jnp.minimum(csc - csr, 0.0)), 0.0)
            yd = _dot(g * lm, xd, NN)
            sh = s_all[sl, :]
            yo = jnp.exp(csc) * _dot(cm, sh, NT)
            y_ref[:, sl] = yd + yo + d_row[:, r:r + 1] * xh
            cl = cs[l - 1:l, r:r + 1]
            dec = jnp.exp(cl - csc)
            state[sl, :] = jnp.exp(cl) * sh + _dot(xd * dec, bm, TN)

    return pl.pallas_call(
        body, name=name, grid=(SSM_GROUPS, nc),
        in_specs=[pl.BlockSpec((l, XBC_GROUP_W), lambda g, c: (c, g)),
                  pl.BlockSpec((1, l, SSM_HPG), lambda g, c: (g, c, 0)),
                  pl.BlockSpec((1, SSM_HPG, l), lambda g, c: (g, 0, c)),
                  pl.BlockSpec((1, 8, 8), lambda g, c: (g, 0, 0)),
                  pl.BlockSpec((1, 8, 8), lambda g, c: (g, 0, 0))],
        out_specs=[pl.BlockSpec((l, SSM_GROUP_W), lambda g, c: (c, g)),
                   pl.BlockSpec((1, SSM_GROUP_W, SSM_D_STATE), lambda g, c: (c, g, 0))],
        out_shape=[jax.ShapeDtypeStruct((t, SSM_D_INNER), f32),
                   jax.ShapeDtypeStruct((nc, SSM_D_INNER, SSM_D_STATE), f32)],
        scratch_shapes=[pltpu.VMEM((SSM_GROUP_W, SSM_D_STATE), f32)],
        compiler_params=pltpu.CompilerParams(dimension_semantics=("parallel", "arbitrary")),
    )(xbc, dtr, dtr_t, par_row, par_col)


def ssd_bwd(xbc, dtr, dtr_t, par_row, par_col, s_in, dy, *, name):
    t = xbc.shape[0]
    nc = t // SSM_CHUNK
    l, p = SSM_CHUNK, SSM_HEAD_DIM

    def body(xbc_ref, dtr_ref, dtrT_ref, prow_ref, pcol_ref, sin_ref, dy_ref, dxbc_ref, ddtr_ref, dpar_ref,
             dstate, dz_buf, xd_buf):
        @pl.when(pl.program_id(1) == 0)
        def _():
            dstate[...] = jnp.zeros_like(dstate)
            dpar_ref[...] = jnp.zeros_like(dpar_ref)

        x, bm, cm, dtr, dt, a_row, d_row, bias_row, tri, li, si, cs, cs_t, g = _ssd_common(
            xbc_ref, dtr_ref, dtrT_ref, prow_ref, pcol_ref)
        s_all = sin_ref[0]
        ds_all = dstate[...]
        dyv = dy_ref[...]
        lane8 = lax.broadcasted_iota(jnp.int32, (l, SSM_HPG), 1)
        row8 = lax.broadcasted_iota(jnp.int32, (l, SSM_HPG), 0)
        lane1 = lax.broadcasted_iota(jnp.int32, (1, SSM_HPG), 1)
        ones8 = jnp.ones((l, SSM_HPG), f32)
        dg = jnp.zeros((l, l), f32)
        dcs = jnp.zeros((l, SSM_HPG), f32)
        ddt = jnp.zeros((l, SSM_HPG), f32)
        dd = jnp.zeros((1, SSM_HPG), f32)
        for r in range(SSM_HPG):
            sl = slice(r * p, (r + 1) * p)
            xh = x[:, sl]
            dtc = dt[:, r:r + 1]
            xd = xh * dtc
            csc, csr = cs[:, r:r + 1], cs_t[r:r + 1, :]
            lm = jnp.where(li >= si, jnp.exp(jnp.minimum(csc - csr, 0.0)), 0.0)
            m = g * lm
            dyh = dyv[:, sl]
            sh = s_all[sl, :]
            dsh = ds_all[sl, :]
            dm = _dot(dyh, xd, NT)
            dxd = _dot(m, dyh, TN)
            pm = dm * m
            dcs_r = jnp.sum(pm, axis=1, keepdims=True) - _dot_hi(pm, ones8, TN)[:, 0:1]
            dg = dg + dm * lm
            e = jnp.exp(csc)
            z = _dot(cm, sh, NT)
            dcs_r = dcs_r + jnp.sum(dyh * z, axis=1, keepdims=True) * e
            dz = e * dyh
            dz_buf[:, sl] = dz
            ds_in = _dot(dz, cm, TN)
            cl = cs[l - 1:l, r:r + 1]
            el = jnp.exp(cl)
            dec = jnp.exp(cl - csc)
            ds_in = ds_in + el * dsh
            dcl = el * jnp.sum(jnp.sum(dsh * sh, axis=1, keepdims=True), axis=0, keepdims=True)
            wv = _dot(bm, dsh, NT)
            dxd = dxd + dec * wv
            ddec = jnp.sum(xd * wv, axis=1, keepdims=True) * dec
            xd_buf[:, sl] = xd * dec
            dcl = dcl + jnp.sum(ddec, axis=0, keepdims=True)
            dcs_r = dcs_r - ddec
            dstate[sl, :] = ds_in
            dcs = jnp.where(lane8 == r, dcs_r + jnp.where(row8 == l - 1, dcl, 0.0), dcs)
            ddt = jnp.where(lane8 == r, jnp.sum(dxd * xh, axis=1, keepdims=True), ddt)
            dd = jnp.where(lane1 == r, jnp.sum(jnp.sum(dyh * xh, axis=1, keepdims=True), axis=0, keepdims=True), dd)
            dxbc_ref[:, sl] = dxd * dtc + d_row[:, r:r + 1] * dyh
        dda = _dot_hi(tri, dcs, TN)
        ddt = ddt + dda * a_row
        dalog = jnp.sum(dda * dt, axis=0, keepdims=True) * a_row
        ddtr = ddt * _sigmoid(dtr + bias_row)
        ddtr_ref[0] = ddtr
        dpar_ref[0, 0:1, :] += jnp.sum(ddtr, axis=0, keepdims=True)
        dpar_ref[0, 1:2, :] += dalog
        dpar_ref[0, 2:3, :] += dd
        dxbc_ref[:, SSM_GROUP_W:SSM_GROUP_W + SSM_D_STATE] = _dot(dg, cm, TN) + _dot(xd_buf[...], ds_all, NN)
        dxbc_ref[:, SSM_GROUP_W + SSM_D_STATE:XBC_GROUP_W] = _dot(dg, bm, NN) + _dot(dz_buf[...], s_all, NN)

    rc = lambda c: nc - 1 - c
    return pl.pallas_call(
        body, name=name, grid=(SSM_GROUPS, nc),
        in_specs=[pl.BlockSpec((l, XBC_GROUP_W), lambda g, c: (rc(c), g)),
                  pl.BlockSpec((1, l, SSM_HPG), lambda g, c: (g, rc(c), 0)),
                  pl.BlockSpec((1, SSM_HPG, l), lambda g, c: (g, 0, rc(c))),
                  pl.BlockSpec((1, 8, 8), lambda g, c: (g, 0, 0)),
                  pl.BlockSpec((1, 8, 8), lambda g, c: (g, 0, 0)),
                  pl.BlockSpec((1, SSM_GROUP_W, SSM_D_STATE), lambda g, c: (rc(c), g, 0)),
                  pl.BlockSpec((l, SSM_GROUP_W), lambda g, c: (rc(c), g))],
        out_specs=[pl.BlockSpec((l, XBC_GROUP_W), lambda g, c: (rc(c), g)),
                   pl.BlockSpec((1, l, SSM_HPG), lambda g, c: (g, rc(c), 0)),
                   pl.BlockSpec((1, 8, 8), lambda g, c: (g, 0, 0))],
        out_shape=[jax.ShapeDtypeStruct((t, SSM_CONV_DIM), f32),
                   jax.ShapeDtypeStruct((SSM_GROUPS, t, SSM_HPG), f32),
                   jax.ShapeDtypeStruct((SSM_GROUPS, 8, 8), f32)],
        scratch_shapes=[pltpu.VMEM((SSM_GROUP_W, SSM_D_STATE), f32), pltpu.VMEM((l, SSM_GROUP_W), f32),
                        pltpu.VMEM((l, SSM_GROUP_W), f32)],
        compiler_params=pltpu.CompilerParams(dimension_semantics=("parallel", "arbitrary")),
    )(xbc, dtr, dtr_t, par_row, par_col, s_in, dy)


def gnorm_fwd(y, proj, w, *, name):
    t = y.shape[0]
    tr = _pick(t, 512, 8)
    gw = SSM_GROUP_W
    zb = OFF_Z // gw

    def body(y_ref, z_ref, w_ref, o_ref):
        y2 = y_ref[...] * _silu(z_ref[...])
        r = lax.rsqrt(jnp.mean(y2 * y2, axis=-1, keepdims=True) + RMS_EPS)
        o_ref[...] = (y2 * r * w_ref[...]).astype(bf16)

    return pl.pallas_call(
        body, name=name, grid=(SSM_GROUPS, t // tr),
        in_specs=[pl.BlockSpec((tr, gw), lambda g, i: (i, g)), pl.BlockSpec((tr, gw), lambda g, i: (i, zb + g)),
                  pl.BlockSpec((1, gw), lambda g, i: (0, g))],
        out_specs=pl.BlockSpec((tr, gw), lambda g, i: (i, g)), out_shape=jax.ShapeDtypeStruct((t, SSM_D_INNER), bf16),
    )(y, proj, w)


def gnorm_bwd(y, proj, w, dout, *, name):
    t = y.shape[0]
    tr = _pick(t, 512, 8)
    gw = SSM_GROUP_W
    zb = OFF_Z // gw

    def body(y_ref, z_ref, w_ref, do_ref, dy_ref, dz_ref, dw_ref):
        yv, zv = y_ref[...], z_ref[...]
        sz = _silu(zv)
        y2 = yv * sz
        dy2, dw = _rms_bwd_math(y2, w_ref[...], do_ref[...])
        dy_ref[...] = dy2 * sz
        dz_ref[...] = (dy2 * yv * _dsilu(zv)).astype(bf16)

        @pl.when(pl.program_id(1) == 0)
        def _():
            dw_ref[...] = jnp.zeros_like(dw_ref)

        dw_ref[...] += dw

    tile = pl.BlockSpec((tr, gw), lambda g, i: (i, g))
    vec = pl.BlockSpec((1, gw), lambda g, i: (0, g))
    return pl.pallas_call(
        body, name=name, grid=(SSM_GROUPS, t // tr),
        in_specs=[tile, pl.BlockSpec((tr, gw), lambda g, i: (i, zb + g)), vec, tile],
        out_specs=[tile, tile, vec],
        out_shape=[jax.ShapeDtypeStruct((t, SSM_D_INNER), f32), jax.ShapeDtypeStruct((t, SSM_D_INNER), bf16),
                   jax.ShapeDtypeStruct((1, SSM_D_INNER), f32)],
        compiler_params=pltpu.CompilerParams(dimension_semantics=("parallel", "arbitrary")),
    )(y, proj, w, dout)


LRU_ROWS = 256


def _lru_gates(uv, wr_ref, wi_ref, br_ref, bi_ref, lam_ref):
    rg = _sigmoid(_dot(uv, wr_ref[0], NN) + br_ref[...])
    ig = _sigmoid(_dot(uv, wi_ref[0], NN) + bi_ref[...])
    sp = _softplus(-lam_ref[...])
    la = -LRU_C * rg * sp
    a = jnp.exp(la)
    s = jnp.sqrt(jnp.maximum(-_expm1(2.0 * la), 0.0))
    return rg, ig, sp, la, a, s


def lru_fwd(u, proj, w_r, b_r, w_i, b_i, lam, *, name):
    t = u.shape[0]
    r = LRU_ROWS
    lb = LRU_BLOCK
    yb = OFF_LY // lb

    def body(u_ref, y_ref, wr_ref, br_ref, wi_ref, bi_ref, lam_ref, h_ref, o_ref, carry):
        @pl.when(pl.program_id(1) == 0)
        def _():
            carry[...] = jnp.zeros_like(carry)

        uv = u_ref[...]
        _, ig, _, _, a, s = _lru_gates(uv, wr_ref, wi_ref, br_ref, bi_ref, lam_ref)
        b = s * ig * uv
        row = lax.broadcasted_iota(jnp.int32, (r, lb), 0)
        d = 1
        while d < r:
            keep = row >= d
            b = b + a * jnp.where(keep, pltpu.roll(b, d, 0), 0.0)
            a = a * jnp.where(keep, pltpu.roll(a, d, 0), 1.0)
            d *= 2
        h = b + a * carry[0:1, :]
        carry[0:1, :] = h[r - 1:r, :]
        h_ref[...] = h
        o_ref[...] = (h * _gelu(y_ref[...])).astype(bf16)

    tile = pl.BlockSpec((r, lb), lambda hb, j: (j, hb))
    vec = pl.BlockSpec((1, lb), lambda hb, j: (0, hb))
    wsp = pl.BlockSpec((1, lb, lb), lambda hb, j: (hb, 0, 0))
    return pl.pallas_call(
        body, name=name, grid=(LRU_BLOCKS, t // r),
        in_specs=[tile, pl.BlockSpec((r, lb), lambda hb, j: (j, yb + hb)), wsp, vec, wsp, vec, vec],
        out_specs=[tile, tile],
        out_shape=[jax.ShapeDtypeStruct((t, LRU_WIDTH), f32), jax.ShapeDtypeStruct((t, LRU_WIDTH), bf16)],
        scratch_shapes=[pltpu.VMEM((8, lb), f32)],
        compiler_params=pltpu.CompilerParams(dimension_semantics=("parallel", "arbitrary")),
    )(u, proj, w_r, b_r, w_i, b_i, lam)


def lru_bwd(u, proj, hseq, dout, w_r, b_r, w_i, b_i, lam, *, name):
    t = u.shape[0]
    r = LRU_ROWS
    nt = t // r
    lb = LRU_BLOCK
    yb = OFF_LY // lb

    def body(u_ref, y_ref, h_ref, hp_ref, do_ref, wr_ref, br_ref, wi_ref, bi_ref, lam_ref,
             du_ref, dy_ref, dwr_ref, dwi_ref, dbr_ref, dbi_ref, dlam_ref, carry_dh, carry_a):
        j = pl.program_id(1)

        @pl.when(j == 0)
        def _():
            carry_dh[...] = jnp.zeros_like(carry_dh)
            carry_a[...] = jnp.zeros_like(carry_a)
            dwr_ref[...] = jnp.zeros_like(dwr_ref)
            dwi_ref[...] = jnp.zeros_like(dwi_ref)
            dbr_ref[...] = jnp.zeros_like(dbr_ref)
            dbi_ref[...] = jnp.zeros_like(dbi_ref)
            dlam_ref[...] = jnp.zeros_like(dlam_ref)

        uv = u_ref[...]
        yv = y_ref[...]
        hv = h_ref[...]
        dov = do_ref[...]
        rg, ig, sp, la, a, s = _lru_gates(uv, wr_ref, wi_ref, br_ref, bi_ref, lam_ref)
        dy_ref[...] = (dov * hv * _dgelu(yv)).astype(bf16)
        gq = dov * _gelu(yv)
        row = lax.broadcasted_iota(jnp.int32, (r, lb), 0)
        an = jnp.where(row < r - 1, pltpu.roll(a, r - 1, 0), carry_a[0:1, :])
        d = 1
        while d < r:
            keep = row < r - d
            gq = gq + an * jnp.where(keep, pltpu.roll(gq, r - d, 0), 0.0)
            an = an * jnp.where(keep, pltpu.roll(an, r - d, 0), 1.0)
            d *= 2
        dh = gq + an * carry_dh[0:1, :]
        carry_dh[0:1, :] = dh[0:1, :]
        carry_a[0:1, :] = a[0:1, :]
        first = jnp.where(j == nt - 1, 0.0, 1.0) * hp_ref[7:8, :]
        hprev = jnp.where(row >= 1, pltpu.roll(hv, 1, 0), first)
        da = dh * hprev
        iu = ig * uv
        e2 = jnp.exp(2.0 * la)
        dla = da * a - dh * iu * e2 / jnp.maximum(s, 1e-30)
        drp = dla * (-LRU_C * sp) * rg * (1.0 - rg)
        dip = dh * s * uv * ig * (1.0 - ig)
        dlam_ref[...] += jnp.sum(dla * (LRU_C * rg) * _sigmoid(-lam_ref[...]), axis=0, keepdims=True)
        du_ref[...] = dh * s * ig + _dot(drp, wr_ref[0], NT) + _dot(dip, wi_ref[0], NT)
        dwr_ref[0] += _dot(uv, drp, TN)
        dwi_ref[0] += _dot(uv, dip, TN)
        dbr_ref[...] += jnp.sum(drp, axis=0, keepdims=True)
        dbi_ref[...] += jnp.sum(dip, axis=0, keepdims=True)

    rj = lambda j: nt - 1 - j
    tile = pl.BlockSpec((r, lb), lambda hb, j: (rj(j), hb))
    vec = pl.BlockSpec((1, lb), lambda hb, j: (0, hb))
    wsp = pl.BlockSpec((1, lb, lb), lambda hb, j: (hb, 0, 0))
    hprev_spec = pl.BlockSpec((8, lb), lambda hb, j: (jnp.maximum(rj(j) * (r // 8) - 1, 0), hb))
    return pl.pallas_call(
        body, name=name, grid=(LRU_BLOCKS, nt),
        in_specs=[tile, pl.BlockSpec((r, lb), lambda hb, j: (rj(j), yb + hb)), tile, hprev_spec, tile, wsp, vec, wsp, vec, vec],
        out_specs=[tile, tile, wsp, wsp, vec, vec, vec],
        out_shape=[jax.ShapeDtypeStruct((t, LRU_WIDTH), f32), jax.ShapeDtypeStruct((t, LRU_WIDTH), bf16),
                   jax.ShapeDtypeStruct((LRU_BLOCKS, lb, lb), f32), jax.ShapeDtypeStruct((LRU_BLOCKS, lb, lb), f32),
                   jax.ShapeDtypeStruct((1, LRU_WIDTH), f32), jax.ShapeDtypeStruct((1, LRU_WIDTH), f32),
                   jax.ShapeDtypeStruct((1, LRU_WIDTH), f32)],
        scratch_shapes=[pltpu.VMEM((8, lb), f32), pltpu.VMEM((8, lb), f32)],
        compiler_params=pltpu.CompilerParams(dimension_semantics=("parallel", "arbitrary")),
    )(u, proj, hseq, hseq, dout, w_r, b_r, w_i, b_i, lam)


def merge_fwd(proj, bg, y_ssm, y_lru, *, name):
    t, d = y_ssm.shape
    tr = _pick(t, 256, 8)
    gb = OFF_GATES // d

    def body(gs_ref, gl_ref, bs_ref, bl_ref, ys_ref, yl_ref, o_ref):
        gs = _sigmoid(gs_ref[...] + bs_ref[...])
        gl = _sigmoid(gl_ref[...] + bl_ref[...])
        o_ref[...] = (gs * ys_ref[...] + gl * yl_ref[...]).astype(bf16)

    row = pl.BlockSpec((tr, d), lambda i: (i, 0))
    return pl.pallas_call(
        body, name=name, grid=(t // tr,),
        in_specs=[pl.BlockSpec((tr, d), lambda i: (i, gb)), pl.BlockSpec((tr, d), lambda i: (i, gb + 1)),
                  pl.BlockSpec((1, d), lambda i: (0, 0)), pl.BlockSpec((1, d), lambda i: (0, 1)), row, row],
        out_specs=row, out_shape=jax.ShapeDtypeStruct((t, d), bf16),
    )(proj, proj, bg, bg, y_ssm, y_lru)


def merge_bwd(proj, bg, y_ssm, y_lru, dmix, *, name):
    t, d = y_ssm.shape
    tr = _pick(t, 256, 8)
    gb = OFF_GATES // d

    def body(gs_ref, gl_ref, bs_ref, bl_ref, ys_ref, yl_ref, dm_ref, dg_ref, dys_ref, dyl_ref, dbg_ref):
        gs = _sigmoid(gs_ref[...] + bs_ref[...])
        gl = _sigmoid(gl_ref[...] + bl_ref[...])
        dm = dm_ref[...]
        dys_ref[...] = (dm * gs).astype(bf16)
        dyl_ref[...] = (dm * gl).astype(bf16)
        dgs = dm * ys_ref[...] * gs * (1.0 - gs)
        dgl = dm * yl_ref[...] * gl * (1.0 - gl)
        dg_ref[:, 0:d] = dgs.astype(bf16)
        dg_ref[:, d:2 * d] = dgl.astype(bf16)

        @pl.when(pl.program_id(0) == 0)
        def _():
            dbg_ref[...] = jnp.zeros_like(dbg_ref)

        dbg_ref[:, 0:d] += jnp.sum(dgs, axis=0, keepdims=True)
        dbg_ref[:, d:2 * d] += jnp.sum(dgl, axis=0, keepdims=True)

    row = pl.BlockSpec((tr, d), lambda i: (i, 0))
    return pl.pallas_call(
        body, name=name, grid=(t // tr,),
        in_specs=[pl.BlockSpec((tr, d), lambda i: (i, gb)), pl.BlockSpec((tr, d), lambda i: (i, gb + 1)),
                  pl.BlockSpec((1, d), lambda i: (0, 0)), pl.BlockSpec((1, d), lambda i: (0, 1)), row, row, row],
        out_specs=[pl.BlockSpec((tr, 2 * d), lambda i: (i, 0)), row, row, pl.BlockSpec((1, 2 * d), lambda i: (0, 0))],
        out_shape=[jax.ShapeDtypeStruct((t, 2 * d), bf16), jax.ShapeDtypeStruct((t, d), bf16),
                   jax.ShapeDtypeStruct((t, d), bf16), jax.ShapeDtypeStruct((1, 2 * d), f32)],
        compiler_params=pltpu.CompilerParams(dimension_semantics=("arbitrary",)),
    )(proj, proj, bg, bg, y_ssm, y_lru, dmix)


def swiglu_fwd(ff, *, name):
    t = ff.shape[0]
    hd = FFN_HIDDEN
    tr = _pick(t, 128, 8)

    def body(f_ref, o_ref):
        o_ref[...] = (_silu(f_ref[:, 0:hd]) * f_ref[:, hd:2 * hd]).astype(bf16)

    return pl.pallas_call(
        body, name=name, grid=(t // tr,), in_specs=[pl.BlockSpec((tr, 2 * hd), lambda i: (i, 0))],
        out_specs=pl.BlockSpec((tr, hd), lambda i: (i, 0)), out_shape=jax.ShapeDtypeStruct((t, hd), bf16),
    )(ff)


def swiglu_bwd(ff, dact, *, name):
    t = ff.shape[0]
    hd = FFN_HIDDEN
    tr = _pick(t, 128, 8)

    def body(f_ref, d_ref, o_ref):
        gate, up, dv = f_ref[:, 0:hd], f_ref[:, hd:2 * hd], d_ref[...]
        o_ref[:, 0:hd] = (dv * up * _dsilu(gate)).astype(bf16)
        o_ref[:, hd:2 * hd] = (dv * _silu(gate)).astype(bf16)

    return pl.pallas_call(
        body, name=name, grid=(t // tr,),
        in_specs=[pl.BlockSpec((tr, 2 * hd), lambda i: (i, 0)), pl.BlockSpec((tr, hd), lambda i: (i, 0))],
        out_specs=pl.BlockSpec((tr, 2 * hd), lambda i: (i, 0)), out_shape=jax.ShapeDtypeStruct((t, 2 * hd), bf16),
    )(ff, dact)


def _adam_math(w, g, m, v):
    m = ADAM_B1 * m + (1.0 - ADAM_B1) * g
    v = ADAM_B2 * v + (1.0 - ADAM_B2) * (g * g)
    m_hat = m / (1.0 - ADAM_B1 ** ADAM_STEP)
    v_hat = v / (1.0 - ADAM_B2 ** ADAM_STEP)
    delta = -ADAM_LR * (m_hat / (jnp.sqrt(v_hat) + ADAM_EPS) + ADAM_WD * w)
    return delta, m, v


def _row_tile(rows, cols):
    cap = max(8, (1 << 18) // cols)
    return _pick(rows, cap, 8) if rows % 8 == 0 else rows


def adamw(w, g, m, v, *, name):
    rows, cols = w.shape
    tr = _row_tile(rows, cols)

    def body(w_ref, g_ref, m_ref, v_ref, d_ref, nm_ref, nv_ref):
        d, nm, nv = _adam_math(w_ref[...], g_ref[...], m_ref[...], v_ref[...])
        d_ref[...] = d
        nm_ref[...] = nm
        nv_ref[...] = nv

    tile = pl.BlockSpec((tr, cols), lambda i: (i, 0))
    return pl.pallas_call(
        body, name=name, grid=(rows // tr,), in_specs=[tile] * 4, out_specs=[tile] * 3,
        out_shape=[jax.ShapeDtypeStruct((rows, cols), f32)] * 3,
    )(w, g, m, v)


def pair_add(dw, rbuf, c_idx, *, name):
    n, rows, cols = dw.shape
    hr = rows // 2
    tr = _row_tile(hr, cols)
    nrt = hr // tr

    def body(c_ref, a_ref, b_ref, o_ref):
        o_ref[...] = a_ref[...] + b_ref[...]

    return pl.pallas_call(
        body, name=name,
        grid_spec=pltpu.PrefetchScalarGridSpec(
            num_scalar_prefetch=1, grid=(n, nrt),
            in_specs=[pl.BlockSpec((1, tr, cols), lambda k, i, c: (k, c[0] * nrt + i, 0)),
                      pl.BlockSpec((1, tr, cols), lambda k, i, c: (k, i, 0))],
            out_specs=pl.BlockSpec((1, tr, cols), lambda k, i, c: (k, i, 0))),
        out_shape=jax.ShapeDtypeStruct((n, hr, cols), f32),
    )(c_idx, dw, rbuf)


def chip_sum(psum, rbuf, me_idx, *, name):
    n, hr, cols = psum.shape
    tr = _row_tile(hr, cols)

    def body(me_ref, a_ref, b_ref, o_ref):
        o_ref[...] = ((a_ref[0] + b_ref[0]) + b_ref[1]) + b_ref[2]

    return pl.pallas_call(
        body, name=name,
        grid_spec=pltpu.PrefetchScalarGridSpec(
            num_scalar_prefetch=1, grid=(hr // tr,),
            in_specs=[pl.BlockSpec((1, tr, cols), lambda i, me: (me[0], i, 0)),
                      pl.BlockSpec((3, tr, cols), lambda i, me: (0, i, 0))],
            out_specs=pl.BlockSpec((tr, cols), lambda i, me: (i, 0))),
        out_shape=jax.ShapeDtypeStruct((hr, cols), f32),
    )(me_idx, psum, rbuf)


def sum8(rbuf, *, name):
    n, rows, cols = rbuf.shape
    tr = _row_tile(rows, cols * n)

    def body(a_ref, o_ref):
        acc = a_ref[0]
        for k in range(1, n):
            acc = acc + a_ref[k]
        o_ref[...] = acc

    return pl.pallas_call(
        body, name=name, grid=(rows // tr,), in_specs=[pl.BlockSpec((n, tr, cols), lambda i: (0, i, 0))],
        out_specs=pl.BlockSpec((tr, cols), lambda i: (i, 0)), out_shape=jax.ShapeDtypeStruct((rows, cols), f32),
    )(rbuf)


def _coords():
    return lax.axis_index("x"), lax.axis_index("y"), lax.axis_index("c")


def _other_chips(x, y):
    return [(1 - x, y), (x, 1 - y), (1 - x, 1 - y)]


def gather_weights(shards, *, name):
    n = len(shards)
    halves = [s.shape[0] // 2 for s in shards]

    def body(*refs):
        ins, outs = refs[:n], refs[n:2 * n]
        send1, recv1, send2, recv2, local = refs[2 * n:]
        x, y, c = _coords()
        me = 2 * x + y
        chips = _other_chips(x, y)
        sibling = (x, y, 1 - c)

        def half(i, k, hc):
            return outs[i].at[k, pl.ds(hc * halves[i], halves[i]), :]

        def ici(i, j):
            return pltpu.make_async_remote_copy(
                src_ref=ins[i].at[pl.ds(c * halves[i], halves[i]), :], dst_ref=half(i, me, c),
                send_sem=send1.at[i, j], recv_sem=recv1.at[i, j], device_id=(*chips[j], c), device_id_type=MESH)

        def landed(i, j):
            kj = 2 * chips[j][0] + chips[j][1]
            return pltpu.make_async_remote_copy(
                src_ref=half(i, kj, c), dst_ref=half(i, kj, c),
                send_sem=send2.at[i, j], recv_sem=recv1.at[i, j], device_id=sibling, device_id_type=MESH)

        def from_sibling(i, j):
            kj = 2 * chips[j][0] + chips[j][1]
            return pltpu.make_async_remote_copy(
                src_ref=half(i, kj, 1 - c), dst_ref=half(i, kj, 1 - c),
                send_sem=send2.at[i, j], recv_sem=recv2.at[i, j], device_id=sibling, device_id_type=MESH)

        def d2d(i, j):
            kj = 2 * chips[j][0] + chips[j][1]
            return pltpu.make_async_remote_copy(
                src_ref=half(i, kj, c), dst_ref=half(i, kj, c),
                send_sem=send2.at[i, j], recv_sem=recv2.at[i, j], device_id=sibling, device_id_type=MESH)

        own = [pltpu.make_async_copy(ins[i], outs[i].at[me], local.at[i]) for i in range(n)]
        for cp in own:
            cp.start()
        for j in range(3):
            for i in range(n):
                ici(i, j).start()
        for j in range(3):
            for i in range(n):
                landed(i, j).wait_recv()
                d2d(i, j).start()
        for j in range(3):
            for i in range(n):
                from_sibling(i, j).wait_recv()
        for j in range(3):
            for i in range(n):
                ici(i, j).wait_send()
                d2d(i, j).wait_send()
        for cp in own:
            cp.wait()

    return pl.pallas_call(
        body, name=name, in_specs=[ANY] * n, out_specs=[ANY] * n,
        out_shape=[jax.ShapeDtypeStruct((N_CHIPS,) + s.shape, s.dtype) for s in shards],
        scratch_shapes=[pltpu.SemaphoreType.DMA((n, 3))] * 4 + [pltpu.SemaphoreType.DMA((n,))],
    )(*shards)


def pair_exchange(grads, *, name):
    n = len(grads)
    halves = [g.shape[1] // 2 for g in grads]

    def body(*refs):
        ins, outs = refs[:n], refs[n:2 * n]
        send, recv = refs[2 * n:]
        x, y, c = _coords()
        cps = [pltpu.make_async_remote_copy(
            src_ref=ins[i].at[:, pl.ds((1 - c) * halves[i], halves[i]), :], dst_ref=outs[i],
            send_sem=send.at[i], recv_sem=recv.at[i], device_id=(x, y, 1 - c), device_id_type=MESH) for i in range(n)]
        for cp in cps:
            cp.start()
        for cp in cps:
            cp.wait()

    return pl.pallas_call(
        body, name=name, in_specs=[ANY] * n, out_specs=[ANY] * n,
        out_shape=[jax.ShapeDtypeStruct((N_CHIPS, g.shape[1] // 2, g.shape[2]), g.dtype) for g in grads],
        scratch_shapes=[pltpu.SemaphoreType.DMA((n,))] * 2,
    )(*grads)


def chip_exchange(psums, *, name):
    n = len(psums)

    def body(*refs):
        ins, outs = refs[:n], refs[n:2 * n]
        send, recv = refs[2 * n:]
        x, y, c = _coords()
        chips = _other_chips(x, y)
        cps = []
        for j in range(3):
            kj = 2 * chips[j][0] + chips[j][1]
            for i in range(n):
                cps.append(pltpu.make_async_remote_copy(
                    src_ref=ins[i].at[kj], dst_ref=outs[i].at[j], send_sem=send.at[i, j], recv_sem=recv.at[i, j],
                    device_id=(*chips[j], c), device_id_type=MESH))
        for cp in cps:
            cp.start()
        for cp in cps:
            cp.wait()

    return pl.pallas_call(
        body, name=name, in_specs=[ANY] * n, out_specs=[ANY] * n,
        out_shape=[jax.ShapeDtypeStruct((3,) + p.shape[1:], p.dtype) for p in psums],
        scratch_shapes=[pltpu.SemaphoreType.DMA((n, 3))] * 2,
    )(*psums)


def pair_gather(halves_in, *, name):
    n = len(halves_in)

    def body(*refs):
        ins, outs = refs[:n], refs[n:2 * n]
        send, recv, local = refs[2 * n:]
        x, y, c = _coords()
        cps, own = [], []
        for i in range(n):
            hr = ins[i].shape[0]
            dst = outs[i].at[pl.ds(c * hr, hr), :]
            own.append(pltpu.make_async_copy(ins[i], dst, local.at[i]))
            cps.append(pltpu.make_async_remote_copy(
                src_ref=ins[i], dst_ref=dst, send_sem=send.at[i], recv_sem=recv.at[i],
                device_id=(x, y, 1 - c), device_id_type=MESH))
        for cp in own + cps:
            cp.start()
        for cp in cps + own:
            cp.wait()

    return pl.pallas_call(
        body, name=name, in_specs=[ANY] * n, out_specs=[ANY] * n,
        out_shape=[jax.ShapeDtypeStruct((2 * h.shape[0], h.shape[1]), h.dtype) for h in halves_in],
        scratch_shapes=[pltpu.SemaphoreType.DMA((n,))] * 3,
    )(*halves_in)


def all_exchange(buf, *, name):
    rows, cols = buf.shape

    def body(in_ref, out_ref, send, recv, local):
        x, y, c = _coords()
        me = 4 * x + 2 * y + c
        cps = []
        for d in range(1, 8):
            px = 1 - x if d & 4 else x
            py = 1 - y if d & 2 else y
            pc = 1 - c if d & 1 else c
            cps.append(pltpu.make_async_remote_copy(
                src_ref=in_ref, dst_ref=out_ref.at[me], send_sem=send.at[d - 1], recv_sem=recv.at[d - 1],
                device_id=(px, py, pc), device_id_type=MESH))
        own = pltpu.make_async_copy(in_ref, out_ref.at[me], local)
        own.start()
        for cp in cps:
            cp.start()
        for d in range(1, 8):
            px = 1 - x if d & 4 else x
            py = 1 - y if d & 2 else y
            pc = 1 - c if d & 1 else c
            src = 4 * px + 2 * py + pc
            pltpu.make_async_remote_copy(
                src_ref=in_ref, dst_ref=out_ref.at[src], send_sem=send.at[d - 1], recv_sem=recv.at[d - 1],
                device_id=(px, py, pc), device_id_type=MESH).wait_recv()
        for cp in cps:
            cp.wait_send()
        own.wait()

    return pl.pallas_call(
        body, name=name, in_specs=[ANY], out_specs=ANY,
        out_shape=jax.ShapeDtypeStruct((8, rows, cols), buf.dtype),
        scratch_shapes=[pltpu.SemaphoreType.DMA((7,)), pltpu.SemaphoreType.DMA((7,)), pltpu.SemaphoreType.DMA],
    )(buf)


def _pack(arrs):
    flat = []
    for a in arrs:
        v = a.reshape(-1).astype(f32)
        pad = (-v.shape[0]) % 128
        flat.append(jnp.pad(v, (0, pad)) if pad else v)
    v = jnp.concatenate(flat)
    rows = v.shape[0] // 128
    pad_rows = (-rows) % 8
    v = v.reshape(rows, 128)
    return jnp.pad(v, ((0, pad_rows), (0, 0))) if pad_rows else v


def _unpack(buf, shapes):
    out, row = [], 0
    for s in shapes:
        size = math.prod(s)
        rows = -(-size // 128)
        out.append(buf[row:row + rows].reshape(-1)[:size].reshape(s))
        row += rows
    return out


def _perm_in_cols(w):
    gates, z = w[..., 0:2048], w[..., 2048:4096]
    xbc = w[..., 4096:7168]
    dt, lx, ly = w[..., 7168:7200], w[..., 7200:8480], w[..., 8480:9760]
    parts = []
    for g in range(SSM_GROUPS):
        parts += [xbc[..., g * 512:(g + 1) * 512], xbc[..., 2048 + g * 128:2048 + (g + 1) * 128],
                  xbc[..., 2560 + g * 128:2560 + (g + 1) * 128]]
    pad = jnp.zeros(w.shape[:-1] + (PROJ_W - IN_PROJ_DIM,), w.dtype)
    return jnp.concatenate(parts + [gates, z, lx, ly, dt, pad], axis=-1)


def _perm_xbc_cols(w):
    parts = []
    for g in range(SSM_GROUPS):
        parts += [w[..., g * 512:(g + 1) * 512], w[..., 2048 + g * 128:2048 + (g + 1) * 128],
                  w[..., 2560 + g * 128:2560 + (g + 1) * 128]]
    return jnp.concatenate(parts, axis=-1)


def _unperm_xbc_cols(w):
    xs = [w[..., g * XBC_GROUP_W:g * XBC_GROUP_W + 512] for g in range(SSM_GROUPS)]
    bs = [w[..., g * XBC_GROUP_W + 512:g * XBC_GROUP_W + 640] for g in range(SSM_GROUPS)]
    cs = [w[..., g * XBC_GROUP_W + 640:(g + 1) * XBC_GROUP_W] for g in range(SSM_GROUPS)]
    return jnp.concatenate(xs + bs + cs, axis=-1)


def _unperm_in_cols(w):
    xbc = _unperm_xbc_cols(w[..., OFF_XBC:OFF_XBC + 3072])
    return jnp.concatenate([w[..., OFF_GATES:OFF_GATES + 2048], w[..., OFF_Z:OFF_Z + 2048], xbc,
                            w[..., OFF_DT:OFF_DT + 32], w[..., OFF_LX:OFF_LX + 1280], w[..., OFF_LY:OFF_LY + 1280]], axis=-1)


def _col_shards(w, n=N_CHIPS):
    r, c = w.shape
    return jnp.transpose(w.reshape(r, n, c // n), (1, 0, 2))


def _from_col_shards(w):
    n, r, c = w.shape
    return jnp.transpose(w, (1, 0, 2)).reshape(r, n * c)


def kernel(x, norm1_w, w_in, b_branch_gate, ssm_conv_w, ssm_conv_b, ssm_dt_bias, ssm_a_log, ssm_d, ssm_norm_w, w_out_ssm, lru_conv_w, lru_conv_b, lru_w_r, lru_b_r, lru_w_i, lru_b_i, lru_lambda, w_out_lru, w_out, norm2_w, w_ffn_in, w_ffn_out, norm_f_w, loss_target, m_norm1_w, m_w_in, m_b_branch_gate, m_ssm_conv_w, m_ssm_conv_b, m_ssm_dt_bias, m_ssm_a_log, m_ssm_d, m_ssm_norm_w, m_w_out_ssm, m_lru_conv_w, m_lru_conv_b, m_lru_w_r, m_lru_b_r, m_lru_w_i, m_lru_b_i, m_lru_lambda, m_w_out_lru, m_w_out, m_norm2_w, m_w_ffn_in, m_w_ffn_out, m_norm_f_w, v_norm1_w, v_w_in, v_b_branch_gate, v_ssm_conv_w, v_ssm_conv_b, v_ssm_dt_bias, v_ssm_a_log, v_ssm_d, v_ssm_norm_w, v_w_out_ssm, v_lru_conv_w, v_lru_conv_b, v_lru_w_r, v_lru_b_r, v_lru_w_i, v_lru_b_i, v_lru_lambda, v_w_out_lru, v_w_out, v_norm2_w, v_w_ffn_in, v_w_ffn_out, v_norm_f_w):
    xi, yi, ci = lax.axis_index("x"), lax.axis_index("y"), lax.axis_index("c")
    me = 2 * xi + yi
    me_idx = jnp.reshape(me, (1,)).astype(jnp.int32)
    c_idx = jnp.reshape(ci, (1,)).astype(jnp.int32)
    x2 = x[0]
    tgt = loss_target[0]

    big_names = ["w_in", "w_out_ssm", "w_out_lru", "w_out", "w_ffn_in", "w_ffn_out"]
    big_w = dict(w_in=w_in[0], w_out_ssm=w_out_ssm[0], w_out_lru=w_out_lru[0], w_out=w_out[0], w_ffn_in=w_ffn_in[0],
                 w_ffn_out=w_ffn_out[0])
    big_m = dict(w_in=m_w_in[0], w_out_ssm=m_w_out_ssm[0], w_out_lru=m_w_out_lru[0], w_out=m_w_out[0],
                 w_ffn_in=m_w_ffn_in[0], w_ffn_out=m_w_ffn_out[0])
    big_v = dict(w_in=v_w_in[0], w_out_ssm=v_w_out_ssm[0], w_out_lru=v_w_out_lru[0], w_out=v_w_out[0],
                 w_ffn_in=v_w_ffn_in[0], w_ffn_out=v_w_ffn_out[0])
    conv_pad = jnp.zeros((16, 768), f32).at[0:4, :].set(ssm_conv_w[0]).at[8:12, 0:320].set(lru_conv_w[0])
    gathered = gather_weights([big_w[k].astype(bf16) for k in big_names] + [conv_pad], name="gather_weights")
    g_in, g_out_ssm, g_out_lru, g_out, g_ffn_in, g_ffn_out, g_conv = gathered
    w_in_p = _perm_in_cols(_from_col_shards(g_in))
    w_out_ssm_f = g_out_ssm.reshape(SSM_D_INNER, D_MODEL)
    w_out_lru_f = g_out_lru.reshape(LRU_WIDTH, D_MODEL)
    w_out_f = g_out.reshape(D_MODEL, D_MODEL)
    w_ffn_in_f = _from_col_shards(g_ffn_in)
    w_ffn_out_f = g_ffn_out.reshape(FFN_HIDDEN, D_MODEL)
    ssm_cw_full = _from_col_shards(g_conv[:, 0:4, :])
    lru_cw_full = _from_col_shards(g_conv[:, 8:12, 0:320])
    ssm_cw_p = _perm_xbc_cols(ssm_cw_full)
    ssm_cb_p = _perm_xbc_cols(ssm_conv_b)

    par = jnp.stack([ssm_dt_bias[0], ssm_a_log[0], ssm_d[0]], axis=0).reshape(3, SSM_GROUPS, SSM_HPG)
    par_row = jnp.zeros((SSM_GROUPS, 8, 8), f32).at[:, 0:3, :].set(jnp.transpose(par, (1, 0, 2)))
    par_col = jnp.transpose(par_row, (0, 2, 1))

    hn1 = rms_fwd(x2, norm1_w, name="rms1_fwd")
    proj = mm(hn1, w_in_p, "nn", name="in_proj")
    t = x2.shape[0]
    dtr = jnp.transpose(proj[:, OFF_DT:OFF_DT + 32].reshape(t, SSM_GROUPS, SSM_HPG), (1, 0, 2))
    dtr_t = jnp.transpose(dtr, (0, 2, 1))
    xbc_pre, xbc_post = conv_fwd(proj, OFF_XBC, SSM_CONV_DIM, ssm_cw_p, ssm_cb_p, silu=True, name="ssm_conv_fwd")
    y_ssd, s_in = ssd_fwd(xbc_post, dtr, dtr_t, par_row, par_col, name="ssd_fwd")
    yn = gnorm_fwd(y_ssd, proj, ssm_norm_w, name="gnorm_fwd")
    y_ssm = mm(yn, w_out_ssm_f, "nn", name="out_ssm")
    (u_lru,) = conv_fwd(proj, OFF_LX, LRU_WIDTH, lru_cw_full, lru_conv_b, silu=False, name="lru_conv_fwd")
    h_lru, o_lru = lru_fwd(u_lru, proj, lru_w_r[0], lru_b_r, lru_w_i[0], lru_b_i, lru_lambda, name="lru_fwd")
    y_lru = mm(o_lru, w_out_lru_f, "nn", name="out_lru")
    mix = merge_fwd(proj, b_branch_gate, y_ssm, y_lru, name="merge_fwd")
    h1 = mm(mix, w_out_f, "nn", add=x2, name="out_proj")
    hn2 = rms_fwd(h1, norm2_w, name="rms2_fwd")
    ff = mm(hn2, w_ffn_in_f, "nn", name="ffn_in")
    act = swiglu_fwd(ff, name="swiglu_fwd")
    h2 = mm(act, w_ffn_out_f, "nn", add=h1, name="ffn_out")
    loss_tile, dh2, d_norm_f = loss_head(h2, norm_f_w.reshape(1, D_MODEL), tgt, name="loss_head")
    loss = lax.psum(loss_tile[0, 0], ("x", "y", "c"))

    d_w_ffn_out = mm(act, dh2, "tn", name="d_w_ffn_out")
    dact = mm(dh2, w_ffn_out_f, "nt", name="d_act")
    dff = swiglu_bwd(ff, dact, name="swiglu_bwd")
    d_w_ffn_in = mm(hn2, dff, "tn", name="d_w_ffn_in")
    dhn2 = mm(dff, w_ffn_in_f, "nt", name="d_hn2")
    dh1, d_norm2 = rms_bwd(h1, norm2_w, dhn2, dh2, name="rms2_bwd")
    d_w_out = mm(mix, dh1, "tn", name="d_w_out")
    dmix = mm(dh1, w_out_f, "nt", name="d_mix")
    dgates, dy_ssm, dy_lru, d_bg = merge_bwd(proj, b_branch_gate, y_ssm, y_lru, dmix, name="merge_bwd")
    d_w_out_ssm = mm(yn, dy_ssm, "tn", name="d_w_out_ssm")
    dyn = mm(dy_ssm, w_out_ssm_f, "nt", name="d_yn")
    dy_ssd, dz, d_ssm_norm = gnorm_bwd(y_ssd, proj, ssm_norm_w, dyn, name="gnorm_bwd")
    dxbc_post, ddtr, dpar = ssd_bwd(xbc_post, dtr, dtr_t, par_row, par_col, s_in, dy_ssd, name="ssd_bwd")
    dxbc, d_ssm_cw_p, d_ssm_cb_p = conv_bwd(dxbc_post, xbc_pre, proj, OFF_XBC, ssm_cw_p, name="ssm_conv_bwd")
    d_w_out_lru = mm(o_lru, dy_lru, "tn", name="d_w_out_lru")
    do_lru = mm(dy_lru, w_out_lru_f, "nt", name="d_o_lru")
    du_lru, dly, d_w_r, d_w_i, d_b_r, d_b_i, d_lam = lru_bwd(u_lru, proj, h_lru, do_lru, lru_w_r[0], lru_b_r, lru_w_i[0],
                                                             lru_b_i, lru_lambda, name="lru_bwd")
    dlx, d_lru_cw, d_lru_cb = conv_bwd(du_lru, None, proj, OFF_LX, lru_cw_full, name="lru_conv_bwd")
    ddt_cols = jnp.transpose(ddtr, (1, 0, 2)).reshape(t, SSM_HEADS).astype(bf16)
    dproj = jnp.concatenate([dxbc, dgates, dz, dlx, dly, ddt_cols, jnp.zeros((t, PROJ_W - OFF_DT - 32), bf16)], axis=1)
    d_w_in_p = mm(hn1, dproj, "tn", name="d_w_in")
    dhn1 = mm(dproj, w_in_p, "nt", name="d_hn1")
    grad_x, d_norm1 = rms_bwd(x2, norm1_w, dhn1, dh1, name="rms1_bwd")

    big_g = [_col_shards(_unperm_in_cols(d_w_in_p)), d_w_out_ssm.reshape(N_CHIPS, 512, D_MODEL),
             d_w_out_lru.reshape(N_CHIPS, 320, D_MODEL), d_w_out.reshape(N_CHIPS, 256, D_MODEL),
             _col_shards(d_w_ffn_in), d_w_ffn_out.reshape(N_CHIPS, 704, D_MODEL)]
    from_sibling = pair_exchange(big_g, name="pair_exchange")
    pair_sums = [pair_add(g, rb, c_idx, name="pair_add_" + k) for g, rb, k in zip(big_g, from_sibling, big_names)]
    from_chips = chip_exchange(pair_sums, name="chip_exchange")
    reduced_half = [chip_sum(p, rb, me_idx, name="chip_sum_" + k) for p, rb, k in zip(pair_sums, from_chips, big_names)]
    reduced = pair_gather(reduced_half, name="pair_gather")
    big_out = {}
    for k, g in zip(big_names, reduced):
        big_out[k] = (g,) + tuple(adamw(big_w[k], g, big_m[k], big_v[k], name="adamw_" + k))

    d_ssm_cw = _unperm_xbc_cols(d_ssm_cw_p)
    d_ssm_cb = _unperm_xbc_cols(d_ssm_cb_p)
    dpar_h = jnp.transpose(dpar[:, 0:3, :], (1, 0, 2)).reshape(3, SSM_HEADS)
    small_names = ["norm1_w", "b_branch_gate", "ssm_conv_b", "ssm_dt_bias", "ssm_a_log", "ssm_d", "ssm_norm_w",
                   "lru_conv_b", "lru_w_r", "lru_b_r", "lru_w_i", "lru_b_i", "lru_lambda", "norm2_w", "norm_f_w"]
    small_g = dict(norm1_w=d_norm1, b_branch_gate=d_bg, ssm_conv_b=d_ssm_cb, ssm_dt_bias=dpar_h[0:1], ssm_a_log=dpar_h[1:2],
                   ssm_d=dpar_h[2:3], ssm_norm_w=d_ssm_norm, lru_conv_b=d_lru_cb, lru_w_r=d_w_r[None], lru_b_r=d_b_r,
                   lru_w_i=d_w_i[None], lru_b_i=d_b_i, lru_lambda=d_lam, norm2_w=d_norm2, norm_f_w=d_norm_f.reshape(D_MODEL))
    small_w = dict(norm1_w=norm1_w, b_branch_gate=b_branch_gate, ssm_conv_b=ssm_conv_b, ssm_dt_bias=ssm_dt_bias,
                   ssm_a_log=ssm_a_log, ssm_d=ssm_d, ssm_norm_w=ssm_norm_w, lru_conv_b=lru_conv_b, lru_w_r=lru_w_r,
                   lru_b_r=lru_b_r, lru_w_i=lru_w_i, lru_b_i=lru_b_i, lru_lambda=lru_lambda, norm2_w=norm2_w, norm_f_w=norm_f_w)
    small_m = dict(norm1_w=m_norm1_w, b_branch_gate=m_b_branch_gate, ssm_conv_b=m_ssm_conv_b, ssm_dt_bias=m_ssm_dt_bias,
                   ssm_a_log=m_ssm_a_log, ssm_d=m_ssm_d, ssm_norm_w=m_ssm_norm_w, lru_conv_b=m_lru_conv_b, lru_w_r=m_lru_w_r,
                   lru_b_r=m_lru_b_r, lru_w_i=m_lru_w_i, lru_b_i=m_lru_b_i, lru_lambda=m_lru_lambda, norm2_w=m_norm2_w,
                   norm_f_w=m_norm_f_w)
    small_v = dict(norm1_w=v_norm1_w, b_branch_gate=v_b_branch_gate, ssm_conv_b=v_ssm_conv_b, ssm_dt_bias=v_ssm_dt_bias,
                   ssm_a_log=v_ssm_a_log, ssm_d=v_ssm_d, ssm_norm_w=v_ssm_norm_w, lru_conv_b=v_lru_conv_b, lru_w_r=v_lru_w_r,
                   lru_b_r=v_lru_b_r, lru_w_i=v_lru_w_i, lru_b_i=v_lru_b_i, lru_lambda=v_lru_lambda, norm2_w=v_norm2_w,
                   norm_f_w=v_norm_f_w)
    shapes = [small_w[k].shape for k in small_names]
    conv_shapes = [(4, SSM_CONV_DIM), (4, LRU_WIDTH)]
    g_pack = _pack([small_g[k] for k in small_names] + [d_ssm_cw, d_lru_cw])
    g_sum = sum8(all_exchange(g_pack, name="all_exchange"), name="sum8")
    g_ssm_cw_full, g_lru_cw_full = _unpack(g_sum, shapes + conv_shapes)[len(shapes):]
    g_ssm_cw = lax.dynamic_slice_in_dim(g_ssm_cw_full, me * 768, 768, axis=1)
    g_lru_cw = lax.dynamic_slice_in_dim(g_lru_cw_full, me * 320, 320, axis=1)
    loc_shapes = [(4, 768), (4, 320)]
    g_loc = _pack(_unpack(g_sum, shapes)[:len(shapes)] + [g_ssm_cw, g_lru_cw])
    w_loc = _pack([small_w[k] for k in small_names] + [ssm_conv_w[0], lru_conv_w[0]])
    m_loc = _pack([small_m[k] for k in small_names] + [m_ssm_conv_w[0], m_lru_conv_w[0]])
    v_loc = _pack([small_v[k] for k in small_names] + [v_ssm_conv_w[0], v_lru_conv_w[0]])
    d_loc, nm_loc, nv_loc = adamw(w_loc, g_loc, m_loc, v_loc, name="adamw_small")
    small_out = {}
    unp = [_unpack(b, shapes + loc_shapes) for b in (g_loc, d_loc, nm_loc, nv_loc)]
    for i, k in enumerate(small_names + ["ssm_conv_w", "lru_conv_w"]):
        small_out[k] = tuple(u[i] for u in unp)

    order = ["norm1_w", "w_in", "b_branch_gate", "ssm_conv_w", "ssm_conv_b", "ssm_dt_bias", "ssm_a_log", "ssm_d", "ssm_norm_w",
             "w_out_ssm", "lru_conv_w", "lru_conv_b", "lru_w_r", "lru_b_r", "lru_w_i", "lru_b_i", "lru_lambda", "w_out_lru",
             "w_out", "norm2_w", "w_ffn_in", "w_ffn_out", "norm_f_w"]
    outs = [loss, grad_x[None]]
    for which in range(4):
        for k in order:
            if k in big_out:
                outs.append(big_out[k][which][None])
            elif k in ("ssm_conv_w", "lru_conv_w"):
                outs.append(small_out[k][which][None])
            else:
                outs.append(small_out[k][which])
    return tuple(outs)
```

```python
import functools
import math

import jax
import jax.numpy as jnp
from jax import lax
from jax.experimental import pallas as pl
from jax.experimental.pallas import tpu as pltpu

f32 = jnp.float32
bf16 = jnp.bfloat16

D_MODEL = 1024
SSM_D_INNER = 2048
SSM_HEADS = 32
SSM_HEAD_DIM = 64
SSM_GROUPS = 4
SSM_HPG = 8
SSM_D_STATE = 128
SSM_CHUNK = 128
SSM_GROUP_W = 512
SSM_CONV_DIM = 3072
XBC_GROUP_W = 768
LRU_WIDTH = 1280
LRU_BLOCKS = 10
LRU_BLOCK = 128
LRU_C = 8.0
FFN_HIDDEN = 2816
RMS_EPS = 1e-6
IN_PROJ_DIM = 9760
N_CHIPS = 4

OFF_GATES = 0
OFF_Z = 2048
OFF_LX = 4096
OFF_LY = 5376
OFF_DT = 6656
DT_PAD_W = 256
OFF_XBC = 6912
PROJ_W = 9984

ADAM_LR = 0.001
ADAM_B1 = 0.9
ADAM_B2 = 0.999
ADAM_EPS = 1e-08
ADAM_WD = 0.01
ADAM_STEP = 10

MESH = pl.DeviceIdType.MESH
ANY = pl.BlockSpec(memory_space=pl.ANY)
HI = lax.Precision.HIGHEST

NN = (((1,), (0,)), ((), ()))
NT = (((1,), (1,)), ((), ()))
TN = (((0,), (0,)), ((), ()))


def _pick(n, cap, mult=128):
    best = None
    for t in range(mult, min(n, cap) + 1, mult):
        if n % t == 0:
            best = t
    return best if best is not None else n


def _sigmoid(x):
    return 1.0 / (1.0 + jnp.exp(-x))


def _softplus(x):
    return jnp.maximum(x, 0.0) + jnp.log(1.0 + jnp.exp(-jnp.abs(x)))


def _silu(x):
    return x * _sigmoid(x)


def _dsilu(x):
    s = _sigmoid(x)
    return s * (1.0 + x * (1.0 - s))


_GELU_K = math.sqrt(2.0 / math.pi)


def _gelu(x):
    return 0.5 * x * (1.0 + jnp.tanh(_GELU_K * (x + 0.044715 * x * x * x)))


def _dgelu(x):
    t = jnp.tanh(_GELU_K * (x + 0.044715 * x * x * x))
    return 0.5 * (1.0 + t) + 0.5 * x * (1.0 - t * t) * _GELU_K * (1.0 + 3.0 * 0.044715 * x * x)


def _expm1(x):
    poly = x * (1.0 + x * (0.5 + x * (1.0 / 6.0 + x * (1.0 / 24.0 + x * (1.0 / 120.0 + x * (1.0 / 720.0))))))
    return jnp.where(jnp.abs(x) < 0.1, poly, jnp.exp(x) - 1.0)


def _dot(a, b, dn):
    return lax.dot_general(a.astype(bf16), b.astype(bf16), dn, preferred_element_type=f32)


def _dot_hi(a, b, dn):
    return lax.dot_general(a, b, dn, preferred_element_type=f32, precision=HI)


def mm(a, b, mode, *, name, add=None, out_dtype=f32):
    if mode == "nn":
        (m, k), (k2, n) = a.shape, b.shape
    elif mode == "nt":
        (m, k), (n, k2) = a.shape, b.shape
    else:
        (k, m), (k2, n) = a.shape, b.shape
    assert k == k2, (a.shape, b.shape, mode)
    tm, tn, tk = _pick(m, 1024), _pick(n, 1024), _pick(k, 1024)
    nk = k // tk
    dn = {"nn": NN, "nt": NT, "tn": TN}[mode]
    a_spec = pl.BlockSpec((tk, tm), lambda i, j, kk: (kk, i)) if mode == "tn" else pl.BlockSpec((tm, tk), lambda i, j, kk: (i, kk))
    b_spec = pl.BlockSpec((tn, tk), lambda i, j, kk: (j, kk)) if mode == "nt" else pl.BlockSpec((tk, tn), lambda i, j, kk: (kk, j))
    o_spec = pl.BlockSpec((tm, tn), lambda i, j, kk: (i, j))
    has_add = add is not None

    def body(a_ref, b_ref, *rest):
        add_ref = rest[0] if has_add else None
        o_ref = rest[1] if has_add else rest[0]

        def finish(r):
            if has_add:
                r = r + add_ref[...]
            o_ref[...] = r.astype(out_dtype)

        if nk == 1:
            finish(_dot(a_ref[...], b_ref[...], dn))
            return
        acc = rest[-1]
        kk = pl.program_id(2)

        @pl.when(kk == 0)
        def _():
            acc[...] = jnp.zeros_like(acc)

        acc[...] += _dot(a_ref[...], b_ref[...], dn)

        @pl.when(kk == nk - 1)
        def _():
            finish(acc[...])

    ins = [a, b] + ([add] if has_add else [])
    in_specs = [a_spec, b_spec] + ([o_spec] if has_add else [])
    return pl.pallas_call(
        body, name=name, grid=(m // tm, n // tn, nk), in_specs=in_specs, out_specs=o_spec,
        out_shape=jax.ShapeDtypeStruct((m, n), out_dtype),
        scratch_shapes=[pltpu.VMEM((tm, tn), f32)] if nk > 1 else [],
        compiler_params=pltpu.CompilerParams(dimension_semantics=("parallel", "parallel", "arbitrary")),
    )(*ins)


def rms_fwd(x, w, *, name):
    t, d = x.shape
    tr = _pick(t, 256, 8)

    def body(x_ref, w_ref, o_ref):
        xv = x_ref[...]
        r = lax.rsqrt(jnp.mean(xv * xv, axis=-1, keepdims=True) + RMS_EPS)
        o_ref[...] = (xv * r * w_ref[...]).astype(bf16)

    return pl.pallas_call(
        body, name=name, grid=(t // tr,),
        in_specs=[pl.BlockSpec((tr, d), lambda i: (i, 0)), pl.BlockSpec((1, d), lambda i: (0, 0))],
        out_specs=pl.BlockSpec((tr, d), lambda i: (i, 0)), out_shape=jax.ShapeDtypeStruct((t, d), bf16),
    )(x, w)


def _rms_bwd_math(xv, wv, dy):
    r = lax.rsqrt(jnp.mean(xv * xv, axis=-1, keepdims=True) + RMS_EPS)
    g = dy * wv
    dx = r * g - xv * (r * r * r) * jnp.mean(g * xv, axis=-1, keepdims=True)
    dw = jnp.sum(dy * xv * r, axis=0, keepdims=True)
    return dx, dw


def rms_bwd(x, w, dy, res, *, name):
    t, d = x.shape
    tr = _pick(t, 256, 8)

    def body(x_ref, w_ref, dy_ref, res_ref, dx_ref, dw_ref):
        dx, dw = _rms_bwd_math(x_ref[...], w_ref[...], dy_ref[...])
        dx_ref[...] = dx + res_ref[...]

        @pl.when(pl.program_id(0) == 0)
        def _():
            dw_ref[...] = jnp.zeros_like(dw_ref)

        dw_ref[...] += dw

    row = pl.BlockSpec((tr, d), lambda i: (i, 0))
    vec = pl.BlockSpec((1, d), lambda i: (0, 0))
    return pl.pallas_call(
        body, name=name, grid=(t // tr,), in_specs=[row, vec, row, row], out_specs=[row, vec],
        out_shape=[jax.ShapeDtypeStruct((t, d), f32), jax.ShapeDtypeStruct((1, d), f32)],
        compiler_params=pltpu.CompilerParams(dimension_semantics=("arbitrary",)),
    )(x, w, dy, res)


def loss_head(h, w, target, *, name):
    t, d = h.shape
    tr = _pick(t, 256, 8)

    def body(h_ref, w_ref, t_ref, loss_ref, dh_ref, dw_ref):
        xv, wv = h_ref[...], w_ref[...]
        r = lax.rsqrt(jnp.mean(xv * xv, axis=-1, keepdims=True) + RMS_EPS)
        err = xv * r * wv - t_ref[...]
        part = 0.5 * jnp.sum(jnp.mean(err * err, axis=-1, keepdims=True), axis=0, keepdims=True)
        dx, dw = _rms_bwd_math(xv, wv, err * (1.0 / d))
        dh_ref[...] = dx

        @pl.when(pl.program_id(0) == 0)
        def _():
            dw_ref[...] = jnp.zeros_like(dw_ref)
            loss_ref[...] = jnp.zeros_like(loss_ref)

        dw_ref[...] += dw
        loss_ref[...] += part

    row = pl.BlockSpec((tr, d), lambda i: (i, 0))
    vec = pl.BlockSpec((1, d), lambda i: (0, 0))
    return pl.pallas_call(
        body, name=name, grid=(t // tr,), in_specs=[row, vec, row],
        out_specs=[pl.BlockSpec((8, 128), lambda i: (0, 0)), row, vec],
        out_shape=[jax.ShapeDtypeStruct((8, 128), f32), jax.ShapeDtypeStruct((t, d), f32), jax.ShapeDtypeStruct((1, d), f32)],
        compiler_params=pltpu.CompilerParams(dimension_semantics=("arbitrary",)),
    )(h, w, target)


CONV_ROWS = 256


def conv_fwd(src, col0, width, w, b, *, silu, name):
    t = src.shape[0]
    tc = _pick(math.gcd(width, col0), 768)
    assert col0 % tc == 0
    cb = col0 // tc
    r = CONV_ROWS

    def body(u_ref, w_ref, b_ref, *rest):
        ext = rest[-1]
        j = pl.program_id(1)

        @pl.when(j == 0)
        def _():
            ext[0:8, :] = jnp.zeros((8, tc), f32)

        @pl.when(j > 0)
        def _():
            ext[0:8, :] = ext[r:r + 8, :]

        ext[8:r + 8, :] = u_ref[...]
        v = ext[...]
        wv = w_ref[...]
        acc = b_ref[...] + wv[3:4, :] * v
        for s in (1, 2, 3):
            acc = acc + wv[3 - s:4 - s, :] * pltpu.roll(v, s, 0)
        pre = acc[8:, :]
        rest[0][...] = pre
        if silu:
            rest[1][...] = _silu(pre)

    tile = pl.BlockSpec((r, tc), lambda c, j: (j, c))
    n_out = 2 if silu else 1
    return pl.pallas_call(
        body, name=name, grid=(width // tc, t // r),
        in_specs=[pl.BlockSpec((r, tc), lambda c, j: (j, cb + c)), pl.BlockSpec((4, tc), lambda c, j: (0, c)),
                  pl.BlockSpec((1, tc), lambda c, j: (0, c))],
        out_specs=[tile] * n_out, out_shape=[jax.ShapeDtypeStruct((t, width), f32)] * n_out,
        scratch_shapes=[pltpu.VMEM((r + 8, tc), f32)],
        compiler_params=pltpu.CompilerParams(dimension_semantics=("parallel", "arbitrary")),
    )(src, w, b)


def conv_bwd(dpost, pre, src, col0, w, dst, *, name):
    t, width = dpost.shape
    tc = _pick(math.gcd(width, col0), 768)
    assert col0 % tc == 0
    cb = col0 // tc
    r = CONV_ROWS
    nt = t // r
    has_pre = pre is not None

    def body(*refs):
        refs = refs[1:]
        if has_pre:
            d_ref, p_ref, u_ref, w_ref, du_ref, dw_ref, db_ref, ext = refs
        else:
            d_ref, u_ref, w_ref, du_ref, dw_ref, db_ref, ext = refs
        j = pl.program_id(1)

        @pl.when(j == 0)
        def _():
            ext[r:r + 8, :] = jnp.zeros((8, tc), f32)
            dw_ref[...] = jnp.zeros_like(dw_ref)
            db_ref[...] = jnp.zeros_like(db_ref)

        @pl.when(j > 0)
        def _():
            ext[r:r + 8, :] = ext[0:8, :]

        dpre = d_ref[...]
        if has_pre:
            dpre = dpre * _dsilu(p_ref[...])
        ext[0:r, :] = dpre
        v = ext[...]
        wv = w_ref[...]
        uv = u_ref[...]
        du = wv[3:4, :] * dpre
        dw_ref[3:4, :] += jnp.sum(dpre * uv, axis=0, keepdims=True)
        for s in (1, 2, 3):
            sh = pltpu.roll(v, r + 8 - s, 0)[0:r, :]
            du = du + wv[3 - s:4 - s, :] * sh
            dw_ref[3 - s:4 - s, :] += jnp.sum(sh * uv, axis=0, keepdims=True)
        db_ref[...] += jnp.sum(dpre, axis=0, keepdims=True)
        du_ref[...] = du.astype(bf16)

    rev = pl.BlockSpec((r, tc), lambda c, j: (nt - 1 - j, c))
    win = pl.BlockSpec((r, tc), lambda c, j: (nt - 1 - j, cb + c))
    in_specs = [ANY, rev] + ([rev] if has_pre else []) + [win, pl.BlockSpec((4, tc), lambda c, j: (0, c))]
    ins = [dst, dpost] + ([pre] if has_pre else []) + [src, w]
    return pl.pallas_call(
        body, name=name, grid=(width // tc, nt), in_specs=in_specs,
        out_specs=[win, pl.BlockSpec((4, tc), lambda c, j: (0, c)), pl.BlockSpec((1, tc), lambda c, j: (0, c))],
        out_shape=[jax.ShapeDtypeStruct(dst.shape, bf16), jax.ShapeDtypeStruct((4, width), f32),
                   jax.ShapeDtypeStruct((1, width), f32)],
        input_output_aliases={0: 0},
        scratch_shapes=[pltpu.VMEM((r + 8, tc), f32)],
        compiler_params=pltpu.CompilerParams(dimension_semantics=("parallel", "arbitrary")),
    )(*ins)


def _ssd_common(xbc_ref, dtr_ref, dtrT_ref, par_row_ref, par_col_ref):
    l = SSM_CHUNK
    x = xbc_ref[:, 0:SSM_GROUP_W]
    bm = xbc_ref[:, SSM_GROUP_W:SSM_GROUP_W + SSM_D_STATE]
    cm = xbc_ref[:, SSM_GROUP_W + SSM_D_STATE:XBC_GROUP_W]
    par_row = par_row_ref[0]
    par_col = par_col_ref[0]
    bias_row, alog_row, d_row = par_row[0:1, :], par_row[1:2, :], par_row[2:3, :]
    bias_col, alog_col = par_col[:, 0:1], par_col[:, 1:2]
    dtr = dtr_ref[0]
    dt = _softplus(dtr + bias_row)
    dt_t = _softplus(dtrT_ref[0] + bias_col)
    a_row = -jnp.exp(alog_row)
    a_col = -jnp.exp(alog_col)
    li = lax.broadcasted_iota(jnp.int32, (l, l), 0)
    si = lax.broadcasted_iota(jnp.int32, (l, l), 1)
    tri = (li >= si).astype(f32)
    cs = _dot_hi(tri, dt * a_row, NN)
    cs_t = _dot_hi(dt_t * a_col, tri, NT)
    g = _dot(cm, bm, NT)
    return x, bm, cm, dtr, dt, a_row, d_row, bias_row, tri, li, si, cs, cs_t, g


def ssd_fwd(xbc, dtr, dtr_t, par_row, par_col, *, name):
    t = xbc.shape[0]
    nc = t // SSM_CHUNK
    l, p = SSM_CHUNK, SSM_HEAD_DIM

    def body(xbc_ref, dtr_ref, dtrT_ref, prow_ref, pcol_ref, y_ref, sin_ref, state):
        @pl.when(pl.program_id(1) == 0)
        def _():
            state[...] = jnp.zeros_like(state)

        x, bm, cm, _, dt, _, d_row, _, tri, li, si, cs, cs_t, g = _ssd_common(xbc_ref, dtr_ref, dtrT_ref, prow_ref, pcol_ref)
        s_all = state[...]
        sin_ref[0] = s_all
        for r in range(SSM_HPG):
            sl = slice(r * p, (r + 1) * p)
            xh = x[:, sl]
            xd = xh * dt[:, r:r + 1]
            csc, csr = cs[:, r:r + 1], cs_t[r:r + 1, :]
            lm = jnp.where(li >= si, jnp.exp(jnp.minimum(csc - csr, 0.0)), 0.0)
            yd = _dot(g * lm, xd, NN)
            sh = s_all[sl, :]
            yo = jnp.exp(csc) * _dot(cm, sh, NT)
            y_ref[:, sl] = yd + yo + d_row[:, r:r + 1] * xh
            cl = cs[l - 1:l, r:r + 1]
            dec = jnp.exp(cl - csc)
            state[sl, :] = jnp.exp(cl) * sh + _dot(xd * dec, bm, TN)

    return pl.pallas_call(
        body, name=name, grid=(SSM_GROUPS, nc),
        in_specs=[pl.BlockSpec((l, XBC_GROUP_W), lambda g, c: (c, g)),
                  pl.BlockSpec((1, l, SSM_HPG), lambda g, c: (g, c, 0)),
                  pl.BlockSpec((1, SSM_HPG, l), lambda g, c: (g, 0, c)),
                  pl.BlockSpec((1, 8, 8), lambda g, c: (g, 0, 0)),
                  pl.BlockSpec((1, 8, 8), lambda g, c: (g, 0, 0))],
        out_specs=[pl.BlockSpec((l, SSM_GROUP_W), lambda g, c: (c, g)),
                   pl.BlockSpec((1, SSM_GROUP_W, SSM_D_STATE), lambda g, c: (c, g, 0))],
        out_shape=[jax.ShapeDtypeStruct((t, SSM_D_INNER), f32),
                   jax.ShapeDtypeStruct((nc, SSM_D_INNER, SSM_D_STATE), f32)],
        scratch_shapes=[pltpu.VMEM((SSM_GROUP_W, SSM_D_STATE), f32)],
        compiler_params=pltpu.CompilerParams(dimension_semantics=("parallel", "arbitrary")),
    )(xbc, dtr, dtr_t, par_row, par_col)


def ssd_bwd(xbc, dtr, dtr_t, par_row, par_col, s_in, dy, *, name):
    t = xbc.shape[0]
    nc = t // SSM_CHUNK
    l, p = SSM_CHUNK, SSM_HEAD_DIM

    def body(xbc_ref, dtr_ref, dtrT_ref, prow_ref, pcol_ref, sin_ref, dy_ref, dxbc_ref, ddtr_ref, dpar_ref,
             dstate, dz_buf, xd_buf):
        @pl.when(pl.program_id(1) == 0)
        def _():
            dstate[...] = jnp.zeros_like(dstate)
            dpar_ref[...] = jnp.zeros_like(dpar_ref)

        x, bm, cm, dtr, dt, a_row, d_row, bias_row, tri, li, si, cs, cs_t, g = _ssd_common(
            xbc_ref, dtr_ref, dtrT_ref, prow_ref, pcol_ref)
        s_all = sin_ref[0]
        ds_all = dstate[...]
        dyv = dy_ref[...]
        lane8 = lax.broadcasted_iota(jnp.int32, (l, SSM_HPG), 1)
        row8 = lax.broadcasted_iota(jnp.int32, (l, SSM_HPG), 0)
        lane1 = lax.broadcasted_iota(jnp.int32, (1, SSM_HPG), 1)
        ones8 = jnp.ones((l, SSM_HPG), f32)
        dg = jnp.zeros((l, l), f32)
        dcs = jnp.zeros((l, SSM_HPG), f32)
        ddt = jnp.zeros((l, SSM_HPG), f32)
        dd = jnp.zeros((1, SSM_HPG), f32)
        for r in range(SSM_HPG):
            sl = slice(r * p, (r + 1) * p)
            xh = x[:, sl]
            dtc = dt[:, r:r + 1]
            xd = xh * dtc
            csc, csr = cs[:, r:r + 1], cs_t[r:r + 1, :]
            lm = jnp.where(li >= si, jnp.exp(jnp.minimum(csc - csr, 0.0)), 0.0)
            m = g * lm
            dyh = dyv[:, sl]
            sh = s_all[sl, :]
            dsh = ds_all[sl, :]
            dm = _dot(dyh, xd, NT)
            dxd = _dot(m, dyh, TN)
            pm = dm * m
            dcs_r = jnp.sum(pm, axis=1, keepdims=True) - _dot_hi(pm, ones8, TN)[:, 0:1]
            dg = dg + dm * lm
            e = jnp.exp(csc)
            z = _dot(cm, sh, NT)
            dcs_r = dcs_r + jnp.sum(dyh * z, axis=1, keepdims=True) * e
            dz = e * dyh
            dz_buf[:, sl] = dz
            ds_in = _dot(dz, cm, TN)
            cl = cs[l - 1:l, r:r + 1]
            el = jnp.exp(cl)
            dec = jnp.exp(cl - csc)
            ds_in = ds_in + el * dsh
            dcl = el * jnp.sum(jnp.sum(dsh * sh, axis=1, keepdims=True), axis=0, keepdims=True)
            wv = _dot(bm, dsh, NT)
            dxd = dxd + dec * wv
            ddec = jnp.sum(xd * wv, axis=1, keepdims=True) * dec
            xd_buf[:, sl] = xd * dec
            dcl = dcl + jnp.sum(ddec, axis=0, keepdims=True)
            dcs_r = dcs_r - ddec
            dstate[sl, :] = ds_in
            dcs = jnp.where(lane8 == r, dcs_r + jnp.where(row8 == l - 1, dcl, 0.0), dcs)
            ddt = jnp.where(lane8 == r, jnp.sum(dxd * xh, axis=1, keepdims=True), ddt)
            dd = jnp.where(lane1 == r, jnp.sum(jnp.sum(dyh * xh, axis=1, keepdims=True), axis=0, keepdims=True), dd)
            dxbc_ref[:, sl] = dxd * dtc + d_row[:, r:r + 1] * dyh
        dda = _dot_hi(tri, dcs, TN)
        ddt = ddt + dda * a_row
        dalog = jnp.sum(dda * dt, axis=0, keepdims=True) * a_row
        ddtr = ddt * _sigmoid(dtr + bias_row)
        ddtr_ref[0] = ddtr
        dpar_ref[0, 0:1, :] += jnp.sum(ddtr, axis=0, keepdims=True)
        dpar_ref[0, 1:2, :] += dalog
        dpar_ref[0, 2:3, :] += dd
        dxbc_ref[:, SSM_GROUP_W:SSM_GROUP_W + SSM_D_STATE] = _dot(dg, cm, TN) + _dot(xd_buf[...], ds_all, NN)
        dxbc_ref[:, SSM_GROUP_W + SSM_D_STATE:XBC_GROUP_W] = _dot(dg, bm, NN) + _dot(dz_buf[...], s_all, NN)

    rc = lambda c: nc - 1 - c
    return pl.pallas_call(
        body, name=name, grid=(SSM_GROUPS, nc),
        in_specs=[pl.BlockSpec((l, XBC_GROUP_W), lambda g, c: (rc(c), g)),
                  pl.BlockSpec((1, l, SSM_HPG), lambda g, c: (g, rc(c), 0)),
                  pl.BlockSpec((1, SSM_HPG, l), lambda g, c: (g, 0, rc(c))),
                  pl.BlockSpec((1, 8, 8), lambda g, c: (g, 0, 0)),
                  pl.BlockSpec((1, 8, 8), lambda g, c: (g, 0, 0)),
                  pl.BlockSpec((1, SSM_GROUP_W, SSM_D_STATE), lambda g, c: (rc(c), g, 0)),
                  pl.BlockSpec((l, SSM_GROUP_W), lambda g, c: (rc(c), g))],
        out_specs=[pl.BlockSpec((l, XBC_GROUP_W), lambda g, c: (rc(c), g)),
                   pl.BlockSpec((1, l, SSM_HPG), lambda g, c: (g, rc(c), 0)),
                   pl.BlockSpec((1, 8, 8), lambda g, c: (g, 0, 0))],
        out_shape=[jax.ShapeDtypeStruct((t, SSM_CONV_DIM), f32),
                   jax.ShapeDtypeStruct((SSM_GROUPS, t, SSM_HPG), f32),
                   jax.ShapeDtypeStruct((SSM_GROUPS, 8, 8), f32)],
        scratch_shapes=[pltpu.VMEM((SSM_GROUP_W, SSM_D_STATE), f32), pltpu.VMEM((l, SSM_GROUP_W), f32),
                        pltpu.VMEM((l, SSM_GROUP_W), f32)],
        compiler_params=pltpu.CompilerParams(dimension_semantics=("parallel", "arbitrary")),
    )(xbc, dtr, dtr_t, par_row, par_col, s_in, dy)


def gnorm_fwd(y, proj, w, *, name):
    t = y.shape[0]
    tr = _pick(t, 512, 8)
    gw = SSM_GROUP_W
    zb = OFF_Z // gw

    def body(y_ref, z_ref, w_ref, o_ref):
        y2 = y_ref[...] * _silu(z_ref[...])
        r = lax.rsqrt(jnp.mean(y2 * y2, axis=-1, keepdims=True) + RMS_EPS)
        o_ref[...] = (y2 * r * w_ref[...]).astype(bf16)

    return pl.pallas_call(
        body, name=name, grid=(SSM_GROUPS, t // tr),
        in_specs=[pl.BlockSpec((tr, gw), lambda g, i: (i, g)), pl.BlockSpec((tr, gw), lambda g, i: (i, zb + g)),
                  pl.BlockSpec((1, gw), lambda g, i: (0, g))],
        out_specs=pl.BlockSpec((tr, gw), lambda g, i: (i, g)), out_shape=jax.ShapeDtypeStruct((t, SSM_D_INNER), bf16),
    )(y, proj, w)


def gnorm_bwd(y, proj, w, dout, dst, *, name):
    t = y.shape[0]
    tr = _pick(t, 512, 8)
    gw = SSM_GROUP_W
    zb = OFF_Z // gw

    def body(_, y_ref, z_ref, w_ref, do_ref, dy_ref, dz_ref, dw_ref):
        yv, zv = y_ref[...], z_ref[...]
        sz = _silu(zv)
        y2 = yv * sz
        dy2, dw = _rms_bwd_math(y2, w_ref[...], do_ref[...])
        dy_ref[...] = dy2 * sz
        dz_ref[...] = (dy2 * yv * _dsilu(zv)).astype(bf16)

        @pl.when(pl.program_id(1) == 0)
        def _():
            dw_ref[...] = jnp.zeros_like(dw_ref)

        dw_ref[...] += dw

    tile = pl.BlockSpec((tr, gw), lambda g, i: (i, g))
    vec = pl.BlockSpec((1, gw), lambda g, i: (0, g))
    return pl.pallas_call(
        body, name=name, grid=(SSM_GROUPS, t // tr),
        in_specs=[ANY, tile, pl.BlockSpec((tr, gw), lambda g, i: (i, zb + g)), vec, tile],
        out_specs=[tile, pl.BlockSpec((tr, gw), lambda g, i: (i, zb + g)), vec],
        out_shape=[jax.ShapeDtypeStruct((t, SSM_D_INNER), f32), jax.ShapeDtypeStruct(dst.shape, bf16),
                   jax.ShapeDtypeStruct((1, SSM_D_INNER), f32)],
        input_output_aliases={0: 1},
        compiler_params=pltpu.CompilerParams(dimension_semantics=("parallel", "arbitrary")),
    )(dst, y, proj, w, dout)


LRU_ROWS = 256


def _lru_gates(uv, wr_ref, wi_ref, br_ref, bi_ref, lam_ref):
    rg = _sigmoid(_dot(uv, wr_ref[0], NN) + br_ref[...])
    ig = _sigmoid(_dot(uv, wi_ref[0], NN) + bi_ref[...])
    sp = _softplus(-lam_ref[...])
    la = -LRU_C * rg * sp
    a = jnp.exp(la)
    s = jnp.sqrt(jnp.maximum(-_expm1(2.0 * la), 0.0))
    return rg, ig, sp, la, a, s


def lru_fwd(u, proj, w_r, b_r, w_i, b_i, lam, *, name):
    t = u.shape[0]
    r = LRU_ROWS
    lb = LRU_BLOCK
    yb = OFF_LY // lb

    def body(u_ref, y_ref, wr_ref, br_ref, wi_ref, bi_ref, lam_ref, h_ref, o_ref, carry):
        @pl.when(pl.program_id(1) == 0)
        def _():
            carry[...] = jnp.zeros_like(carry)

        uv = u_ref[...]
        _, ig, _, _, a, s = _lru_gates(uv, wr_ref, wi_ref, br_ref, bi_ref, lam_ref)
        b = s * ig * uv
        row = lax.broadcasted_iota(jnp.int32, (r, lb), 0)
        d = 1
        while d < r:
            keep = row >= d
            b = b + a * jnp.where(keep, pltpu.roll(b, d, 0), 0.0)
            a = a * jnp.where(keep, pltpu.roll(a, d, 0), 1.0)
            d *= 2
        h = b + a * carry[0:1, :]
        carry[0:1, :] = h[r - 1:r, :]
        h_ref[...] = h
        o_ref[...] = (h * _gelu(y_ref[...])).astype(bf16)

    tile = pl.BlockSpec((r, lb), lambda hb, j: (j, hb))
    vec = pl.BlockSpec((1, lb), lambda hb, j: (0, hb))
    wsp = pl.BlockSpec((1, lb, lb), lambda hb, j: (hb, 0, 0))
    return pl.pallas_call(
        body, name=name, grid=(LRU_BLOCKS, t // r),
        in_specs=[tile, pl.BlockSpec((r, lb), lambda hb, j: (j, yb + hb)), wsp, vec, wsp, vec, vec],
        out_specs=[tile, tile],
        out_shape=[jax.ShapeDtypeStruct((t, LRU_WIDTH), f32), jax.ShapeDtypeStruct((t, LRU_WIDTH), bf16)],
        scratch_shapes=[pltpu.VMEM((8, lb), f32)],
        compiler_params=pltpu.CompilerParams(dimension_semantics=("parallel", "arbitrary")),
    )(u, proj, w_r, b_r, w_i, b_i, lam)


def lru_bwd(u, proj, hseq, dout, w_r, b_r, w_i, b_i, lam, dst, *, name):
    t = u.shape[0]
    r = LRU_ROWS
    nt = t // r
    lb = LRU_BLOCK
    yb = OFF_LY // lb

    def body(_, u_ref, y_ref, h_ref, hp_ref, do_ref, wr_ref, br_ref, wi_ref, bi_ref, lam_ref,
             du_ref, dy_ref, dwr_ref, dwi_ref, dbr_ref, dbi_ref, dlam_ref, carry_dh, carry_a):
        j = pl.program_id(1)

        @pl.when(j == 0)
        def _():
            carry_dh[...] = jnp.zeros_like(carry_dh)
            carry_a[...] = jnp.zeros_like(carry_a)
            dwr_ref[...] = jnp.zeros_like(dwr_ref)
            dwi_ref[...] = jnp.zeros_like(dwi_ref)
            dbr_ref[...] = jnp.zeros_like(dbr_ref)
            dbi_ref[...] = jnp.zeros_like(dbi_ref)
            dlam_ref[...] = jnp.zeros_like(dlam_ref)

        uv = u_ref[...]
        yv = y_ref[...]
        hv = h_ref[...]
        dov = do_ref[...]
        rg, ig, sp, la, a, s = _lru_gates(uv, wr_ref, wi_ref, br_ref, bi_ref, lam_ref)
        dy_ref[...] = (dov * hv * _dgelu(yv)).astype(bf16)
        gq = dov * _gelu(yv)
        row = lax.broadcasted_iota(jnp.int32, (r, lb), 0)
        an = jnp.where(row < r - 1, pltpu.roll(a, r - 1, 0), carry_a[0:1, :])
        d = 1
        while d < r:
            keep = row < r - d
            gq = gq + an * jnp.where(keep, pltpu.roll(gq, r - d, 0), 0.0)
            an = an * jnp.where(keep, pltpu.roll(an, r - d, 0), 1.0)
            d *= 2
        dh = gq + an * carry_dh[0:1, :]
        carry_dh[0:1, :] = dh[0:1, :]
        carry_a[0:1, :] = a[0:1, :]
        first = jnp.where(j == nt - 1, 0.0, 1.0) * hp_ref[7:8, :]
        hprev = jnp.where(row >= 1, pltpu.roll(hv, 1, 0), first)
        da = dh * hprev
        iu = ig * uv
        e2 = jnp.exp(2.0 * la)
        dla = da * a - dh * iu * e2 / jnp.maximum(s, 1e-30)
        drp = dla * (-LRU_C * sp) * rg * (1.0 - rg)
        dip = dh * s * uv * ig * (1.0 - ig)
        dlam_ref[...] += jnp.sum(dla * (LRU_C * rg) * _sigmoid(-lam_ref[...]), axis=0, keepdims=True)
        du_ref[...] = dh * s * ig + _dot(drp, wr_ref[0], NT) + _dot(dip, wi_ref[0], NT)
        dwr_ref[0] += _dot(uv, drp, TN)
        dwi_ref[0] += _dot(uv, dip, TN)
        dbr_ref[...] += jnp.sum(drp, axis=0, keepdims=True)
        dbi_ref[...] += jnp.sum(dip, axis=0, keepdims=True)

    rj = lambda j: nt - 1 - j
    tile = pl.BlockSpec((r, lb), lambda hb, j: (rj(j), hb))
    vec = pl.BlockSpec((1, lb), lambda hb, j: (0, hb))
    wsp = pl.BlockSpec((1, lb, lb), lambda hb, j: (hb, 0, 0))
    hprev_spec = pl.BlockSpec((8, lb), lambda hb, j: (jnp.maximum(rj(j) * (r // 8) - 1, 0), hb))
    ywin = pl.BlockSpec((r, lb), lambda hb, j: (rj(j), yb + hb))
    return pl.pallas_call(
        body, name=name, grid=(LRU_BLOCKS, nt),
        in_specs=[ANY, tile, ywin, tile, hprev_spec, tile, wsp, vec, wsp, vec, vec],
        out_specs=[tile, ywin, wsp, wsp, vec, vec, vec],
        out_shape=[jax.ShapeDtypeStruct((t, LRU_WIDTH), f32), jax.ShapeDtypeStruct(dst.shape, bf16),
                   jax.ShapeDtypeStruct((LRU_BLOCKS, lb, lb), f32), jax.ShapeDtypeStruct((LRU_BLOCKS, lb, lb), f32),
                   jax.ShapeDtypeStruct((1, LRU_WIDTH), f32), jax.ShapeDtypeStruct((1, LRU_WIDTH), f32),
                   jax.ShapeDtypeStruct((1, LRU_WIDTH), f32)],
        input_output_aliases={0: 1},
        scratch_shapes=[pltpu.VMEM((8, lb), f32), pltpu.VMEM((8, lb), f32)],
        compiler_params=pltpu.CompilerParams(dimension_semantics=("parallel", "arbitrary")),
    )(dst, u, proj, hseq, hseq, dout, w_r, b_r, w_i, b_i, lam)


def merge_fwd(proj, bg, y_ssm, y_lru, *, name):
    t, d = y_ssm.shape
    tr = _pick(t, 256, 8)
    gb = OFF_GATES // d

    def body(gs_ref, gl_ref, bs_ref, bl_ref, ys_ref, yl_ref, o_ref):
        gs = _sigmoid(gs_ref[...] + bs_ref[...])
        gl = _sigmoid(gl_ref[...] + bl_ref[...])
        o_ref[...] = (gs * ys_ref[...] + gl * yl_ref[...]).astype(bf16)

    row = pl.BlockSpec((tr, d), lambda i: (i, 0))
    return pl.pallas_call(
        body, name=name, grid=(t // tr,),
        in_specs=[pl.BlockSpec((tr, d), lambda i: (i, gb)), pl.BlockSpec((tr, d), lambda i: (i, gb + 1)),
                  pl.BlockSpec((1, d), lambda i: (0, 0)), pl.BlockSpec((1, d), lambda i: (0, 1)), row, row],
        out_specs=row, out_shape=jax.ShapeDtypeStruct((t, d), bf16),
    )(proj, proj, bg, bg, y_ssm, y_lru)


def merge_bwd(proj, bg, y_ssm, y_lru, dmix, *, name):
    t, d = y_ssm.shape
    tr = _pick(t, 256, 8)
    gb = OFF_GATES // d

    def body(gs_ref, gl_ref, bs_ref, bl_ref, ys_ref, yl_ref, dm_ref, dg_ref, dys_ref, dyl_ref, dbg_ref):
        gs = _sigmoid(gs_ref[...] + bs_ref[...])
        gl = _sigmoid(gl_ref[...] + bl_ref[...])
        dm = dm_ref[...]
        dys_ref[...] = (dm * gs).astype(bf16)
        dyl_ref[...] = (dm * gl).astype(bf16)
        dgs = dm * ys_ref[...] * gs * (1.0 - gs)
        dgl = dm * yl_ref[...] * gl * (1.0 - gl)
        dg_ref[:, 0:d] = dgs.astype(bf16)
        dg_ref[:, d:2 * d] = dgl.astype(bf16)

        @pl.when(pl.program_id(0) == 0)
        def _():
            dbg_ref[...] = jnp.zeros_like(dbg_ref)

        dbg_ref[:, 0:d] += jnp.sum(dgs, axis=0, keepdims=True)
        dbg_ref[:, d:2 * d] += jnp.sum(dgl, axis=0, keepdims=True)

    row = pl.BlockSpec((tr, d), lambda i: (i, 0))
    return pl.pallas_call(
        body, name=name, grid=(t // tr,),
        in_specs=[pl.BlockSpec((tr, d), lambda i: (i, gb)), pl.BlockSpec((tr, d), lambda i: (i, gb + 1)),
                  pl.BlockSpec((1, d), lambda i: (0, 0)), pl.BlockSpec((1, d), lambda i: (0, 1)), row, row, row],
        out_specs=[pl.BlockSpec((tr, 2 * d), lambda i: (i, OFF_GATES // (2 * d))), row, row,
                   pl.BlockSpec((1, 2 * d), lambda i: (0, 0))],
        out_shape=[jax.ShapeDtypeStruct((t, PROJ_W), bf16), jax.ShapeDtypeStruct((t, d), bf16),
                   jax.ShapeDtypeStruct((t, d), bf16), jax.ShapeDtypeStruct((1, 2 * d), f32)],
        compiler_params=pltpu.CompilerParams(dimension_semantics=("arbitrary",)),
    )(proj, proj, bg, bg, y_ssm, y_lru, dmix)


def swiglu_fwd(ff, *, name):
    t = ff.shape[0]
    hd = FFN_HIDDEN
    tr = _pick(t, 128, 8)

    def body(f_ref, o_ref):
        o_ref[...] = (_silu(f_ref[:, 0:hd]) * f_ref[:, hd:2 * hd]).astype(bf16)

    return pl.pallas_call(
        body, name=name, grid=(t // tr,), in_specs=[pl.BlockSpec((tr, 2 * hd), lambda i: (i, 0))],
        out_specs=pl.BlockSpec((tr, hd), lambda i: (i, 0)), out_shape=jax.ShapeDtypeStruct((t, hd), bf16),
    )(ff)


def swiglu_bwd(ff, dact, *, name):
    t = ff.shape[0]
    hd = FFN_HIDDEN
    tr = _pick(t, 128, 8)

    def body(f_ref, d_ref, o_ref):
        gate, up, dv = f_ref[:, 0:hd], f_ref[:, hd:2 * hd], d_ref[...]
        o_ref[:, 0:hd] = (dv * up * _dsilu(gate)).astype(bf16)
        o_ref[:, hd:2 * hd] = (dv * _silu(gate)).astype(bf16)

    return pl.pallas_call(
        body, name=name, grid=(t // tr,),
        in_specs=[pl.BlockSpec((tr, 2 * hd), lambda i: (i, 0)), pl.BlockSpec((tr, hd), lambda i: (i, 0))],
        out_specs=pl.BlockSpec((tr, 2 * hd), lambda i: (i, 0)), out_shape=jax.ShapeDtypeStruct((t, 2 * hd), bf16),
    )(ff, dact)


def _adam_math(w, g, m, v):
    m = ADAM_B1 * m + (1.0 - ADAM_B1) * g
    v = ADAM_B2 * v + (1.0 - ADAM_B2) * (g * g)
    m_hat = m / (1.0 - ADAM_B1 ** ADAM_STEP)
    v_hat = v / (1.0 - ADAM_B2 ** ADAM_STEP)
    delta = -ADAM_LR * (m_hat / (jnp.sqrt(v_hat) + ADAM_EPS) + ADAM_WD * w)
    return delta, m, v


def _row_tile(rows, cols):
    cap = max(8, (1 << 18) // cols)
    return _pick(rows, cap, 8) if rows % 8 == 0 else rows


def adamw(w, g, m, v, *, name):
    rows, cols = w.shape
    tr = _row_tile(rows, cols)

    def body(w_ref, g_ref, m_ref, v_ref, d_ref, nm_ref, nv_ref):
        d, nm, nv = _adam_math(w_ref[...], g_ref[...], m_ref[...], v_ref[...])
        d_ref[...] = d
        nm_ref[...] = nm
        nv_ref[...] = nv

    tile = pl.BlockSpec((tr, cols), lambda i: (i, 0))
    return pl.pallas_call(
        body, name=name, grid=(rows // tr,), in_specs=[tile] * 4, out_specs=[tile] * 3,
        out_shape=[jax.ShapeDtypeStruct((rows, cols), f32)] * 3,
    )(w, g, m, v)


def pair_add(dw, rbuf, idx, *, name):
    n, rows, cols = dw.shape
    hr = rows // 2
    tr = _row_tile(hr, cols)
    nrt = hr // tr

    def body(idx_ref, a_ref, b_ref, o_ref, own_ref):
        s = a_ref[...] + b_ref[...]
        o_ref[...] = s.astype(bf16)

        @pl.when(pl.program_id(1) == idx_ref[0])
        def _():
            own_ref[...] = s[0]

    return pl.pallas_call(
        body, name=name,
        grid_spec=pltpu.PrefetchScalarGridSpec(
            num_scalar_prefetch=1, grid=(nrt, n),
            in_specs=[pl.BlockSpec((1, tr, cols), lambda i, k, idx: (k, idx[1] * nrt + i, 0)),
                      pl.BlockSpec((1, tr, cols), lambda i, k, idx: (k, i, 0))],
            out_specs=[pl.BlockSpec((1, tr, cols), lambda i, k, idx: (k, i, 0)),
                       pl.BlockSpec((tr, cols), lambda i, k, idx: (i, 0))]),
        out_shape=[jax.ShapeDtypeStruct((n, hr, cols), bf16), jax.ShapeDtypeStruct((hr, cols), f32)],
    )(idx, dw, rbuf)


def chip_sum(own, rbuf, idx, *, name):
    hr, cols = own.shape
    tr = _row_tile(hr, cols)
    nrt = hr // tr

    def body(idx_ref, a_ref, b_ref, o_ref):
        o_ref[...] = ((a_ref[...] + b_ref[0].astype(f32)) + b_ref[1].astype(f32)) + b_ref[2].astype(f32)

    return pl.pallas_call(
        body, name=name,
        grid_spec=pltpu.PrefetchScalarGridSpec(
            num_scalar_prefetch=1, grid=(nrt,),
            in_specs=[pl.BlockSpec((tr, cols), lambda i, idx: (i, 0)),
                      pl.BlockSpec((3, tr, cols), lambda i, idx: (0, i, 0))],
            out_specs=pl.BlockSpec((tr, cols), lambda i, idx: (idx[1] * nrt + i, 0))),
        out_shape=jax.ShapeDtypeStruct((2 * hr, cols), f32),
    )(idx, own, rbuf)


def sum8(rbuf, *, name):
    n, rows, cols = rbuf.shape
    tr = _row_tile(rows, cols * n)

    def body(a_ref, o_ref):
        acc = a_ref[0]
        for k in range(1, n):
            acc = acc + a_ref[k]
        o_ref[...] = acc

    return pl.pallas_call(
        body, name=name, grid=(rows // tr,), in_specs=[pl.BlockSpec((n, tr, cols), lambda i: (0, i, 0))],
        out_specs=pl.BlockSpec((tr, cols), lambda i: (i, 0)), out_shape=jax.ShapeDtypeStruct((rows, cols), f32),
    )(rbuf)


def _coords():
    return lax.axis_index("x"), lax.axis_index("y"), lax.axis_index("c")


def _other_chips(x, y):
    return [(1 - x, y), (x, 1 - y), (1 - x, 1 - y)]


def gather_weights(shards, *, name):
    n = len(shards)
    halves = [s.shape[0] // 2 for s in shards]

    def body(*refs):
        ins, outs = refs[:n], refs[n:2 * n]
        send1, recv1, send2, recv2 = refs[2 * n:]
        x, y, c = _coords()
        me = 2 * x + y
        chips = _other_chips(x, y)
        sibling = (x, y, 1 - c)

        def half(i, k, hc):
            return outs[i].at[k, pl.ds(hc * halves[i], halves[i]), :]

        def ici(i, j):
            return pltpu.make_async_remote_copy(
                src_ref=ins[i].at[pl.ds(c * halves[i], halves[i]), :], dst_ref=half(i, me, c),
                send_sem=send1.at[i, j], recv_sem=recv1.at[i, j], device_id=(*chips[j], c), device_id_type=MESH)

        def landed(i, j):
            kj = 2 * chips[j][0] + chips[j][1]
            return pltpu.make_async_remote_copy(
                src_ref=half(i, kj, c), dst_ref=half(i, kj, c),
                send_sem=send2.at[i, j], recv_sem=recv1.at[i, j], device_id=sibling, device_id_type=MESH)

        def from_sibling(i, j):
            kj = 2 * chips[j][0] + chips[j][1]
            return pltpu.make_async_remote_copy(
                src_ref=half(i, kj, 1 - c), dst_ref=half(i, kj, 1 - c),
                send_sem=send2.at[i, j], recv_sem=recv2.at[i, j], device_id=sibling, device_id_type=MESH)

        def d2d(i, j):
            kj = 2 * chips[j][0] + chips[j][1]
            return pltpu.make_async_remote_copy(
                src_ref=half(i, kj, c), dst_ref=half(i, kj, c),
                send_sem=send2.at[i, j], recv_sem=recv2.at[i, j], device_id=sibling, device_id_type=MESH)

        for j in range(3):
            for i in range(n):
                ici(i, j).start()
        for j in range(3):
            for i in range(n):
                landed(i, j).wait_recv()
                d2d(i, j).start()
        for j in range(3):
            for i in range(n):
                from_sibling(i, j).wait_recv()
        for j in range(3):
            for i in range(n):
                ici(i, j).wait_send()
                d2d(i, j).wait_send()

    return pl.pallas_call(
        body, name=name, in_specs=[ANY] * n, out_specs=[ANY] * n,
        out_shape=[jax.ShapeDtypeStruct((N_CHIPS,) + s.shape, s.dtype) for s in shards],
        scratch_shapes=[pltpu.SemaphoreType.DMA((n, 3))] * 4,
    )(*shards)


def pair_exchange(grads, *, name):
    n = len(grads)
    halves = [g.shape[1] // 2 for g in grads]

    def body(*refs):
        ins, outs = refs[:n], refs[n:2 * n]
        send, recv = refs[2 * n:]
        x, y, c = _coords()
        cps = [pltpu.make_async_remote_copy(
            src_ref=ins[i].at[:, pl.ds((1 - c) * halves[i], halves[i]), :], dst_ref=outs[i],
            send_sem=send.at[i], recv_sem=recv.at[i], device_id=(x, y, 1 - c), device_id_type=MESH) for i in range(n)]
        for cp in cps:
            cp.start()
        for cp in cps:
            cp.wait()

    return pl.pallas_call(
        body, name=name, in_specs=[ANY] * n, out_specs=[ANY] * n,
        out_shape=[jax.ShapeDtypeStruct((N_CHIPS, g.shape[1] // 2, g.shape[2]), g.dtype) for g in grads],
        scratch_shapes=[pltpu.SemaphoreType.DMA((n,))] * 2,
    )(*grads)


def chip_exchange(psums, *, name):
    n = len(psums)

    def body(*refs):
        ins, outs = refs[:n], refs[n:2 * n]
        send, recv = refs[2 * n:]
        x, y, c = _coords()
        chips = _other_chips(x, y)
        cps = []
        for j in range(3):
            kj = 2 * chips[j][0] + chips[j][1]
            for i in range(n):
                cps.append(pltpu.make_async_remote_copy(
                    src_ref=ins[i].at[kj], dst_ref=outs[i].at[j], send_sem=send.at[i, j], recv_sem=recv.at[i, j],
                    device_id=(*chips[j], c), device_id_type=MESH))
        for cp in cps:
            cp.start()
        for cp in cps:
            cp.wait()

    return pl.pallas_call(
        body, name=name, in_specs=[ANY] * n, out_specs=[ANY] * n,
        out_shape=[jax.ShapeDtypeStruct((3,) + p.shape[1:], p.dtype) for p in psums],
        scratch_shapes=[pltpu.SemaphoreType.DMA((n, 3))] * 2,
    )(*psums)


def pair_gather(bufs, *, name):
    n = len(bufs)

    def body(*refs):
        ins, outs = refs[:n], refs[n:2 * n]
        send, recv = refs[2 * n:]
        x, y, c = _coords()
        cps = []
        for i in range(n):
            hr = ins[i].shape[0] // 2
            cps.append(pltpu.make_async_remote_copy(
                src_ref=ins[i].at[pl.ds(c * hr, hr), :], dst_ref=outs[i].at[pl.ds(c * hr, hr), :],
                send_sem=send.at[i], recv_sem=recv.at[i], device_id=(x, y, 1 - c), device_id_type=MESH))
        for cp in cps:
            cp.start()
        for i in range(n):
            hr = ins[i].shape[0] // 2
            pltpu.make_async_remote_copy(
                src_ref=ins[i].at[pl.ds((1 - c) * hr, hr), :], dst_ref=outs[i].at[pl.ds((1 - c) * hr, hr), :],
                send_sem=send.at[i], recv_sem=recv.at[i], device_id=(x, y, 1 - c), device_id_type=MESH).wait_recv()
        for cp in cps:
            cp.wait_send()

    return pl.pallas_call(
        body, name=name, in_specs=[ANY] * n, out_specs=[ANY] * n,
        out_shape=[jax.ShapeDtypeStruct(b.shape, b.dtype) for b in bufs],
        input_output_aliases={i: i for i in range(n)},
        scratch_shapes=[pltpu.SemaphoreType.DMA((n,))] * 2,
    )(*bufs)


def all_exchange(buf, *, name):
    rows, cols = buf.shape

    def body(in_ref, out_ref, send, recv):
        x, y, c = _coords()
        me = 4 * x + 2 * y + c
        cps = []
        for d in range(1, 8):
            px = 1 - x if d & 4 else x
            py = 1 - y if d & 2 else y
            pc = 1 - c if d & 1 else c
            cps.append(pltpu.make_async_remote_copy(
                src_ref=in_ref, dst_ref=out_ref.at[me], send_sem=send.at[d - 1], recv_sem=recv.at[d - 1],
                device_id=(px, py, pc), device_id_type=MESH))
        for cp in cps:
            cp.start()
        for d in range(1, 8):
            px = 1 - x if d & 4 else x
            py = 1 - y if d & 2 else y
            pc = 1 - c if d & 1 else c
            src = 4 * px + 2 * py + pc
            pltpu.make_async_remote_copy(
                src_ref=in_ref, dst_ref=out_ref.at[src], send_sem=send.at[d - 1], recv_sem=recv.at[d - 1],
                device_id=(px, py, pc), device_id_type=MESH).wait_recv()
        for cp in cps:
            cp.wait_send()

    return pl.pallas_call(
        body, name=name, in_specs=[ANY], out_specs=ANY,
        out_shape=jax.ShapeDtypeStruct((8, rows, cols), buf.dtype),
        scratch_shapes=[pltpu.SemaphoreType.DMA((7,)), pltpu.SemaphoreType.DMA((7,))],
    )(buf)


def _pack(arrs):
    flat = []
    for a in arrs:
        v = a.reshape(-1).astype(f32)
        pad = (-v.shape[0]) % 128
        flat.append(jnp.pad(v, (0, pad)) if pad else v)
    v = jnp.concatenate(flat)
    rows = v.shape[0] // 128
    pad_rows = (-rows) % 256
    v = v.reshape(rows, 128)
    return jnp.pad(v, ((0, pad_rows), (0, 0))) if pad_rows else v


def _unpack(buf, shapes):
    out, row = [], 0
    for s in shapes:
        size = math.prod(s)
        rows = -(-size // 128)
        out.append(buf[row:row + rows].reshape(-1)[:size].reshape(s))
        row += rows
    return out


def _perm_in_cols(w):
    gates, z = w[..., 0:2048], w[..., 2048:4096]
    xbc = w[..., 4096:7168]
    dt, lx, ly = w[..., 7168:7200], w[..., 7200:8480], w[..., 8480:9760]
    pad = jnp.zeros(w.shape[:-1] + (DT_PAD_W - SSM_HEADS,), w.dtype)
    return jnp.concatenate([gates, z, lx, ly, dt, pad, _perm_xbc_cols(xbc)], axis=-1)


def _perm_xbc_cols(w):
    parts = []
    for g in range(SSM_GROUPS):
        parts += [w[..., g * 512:(g + 1) * 512], w[..., 2048 + g * 128:2048 + (g + 1) * 128],
                  w[..., 2560 + g * 128:2560 + (g + 1) * 128]]
    return jnp.concatenate(parts, axis=-1)


def _unperm_xbc_cols(w):
    xs = [w[..., g * XBC_GROUP_W:g * XBC_GROUP_W + 512] for g in range(SSM_GROUPS)]
    bs = [w[..., g * XBC_GROUP_W + 512:g * XBC_GROUP_W + 640] for g in range(SSM_GROUPS)]
    cs = [w[..., g * XBC_GROUP_W + 640:(g + 1) * XBC_GROUP_W] for g in range(SSM_GROUPS)]
    return jnp.concatenate(xs + bs + cs, axis=-1)


def _unperm_in_cols(w):
    xbc = _unperm_xbc_cols(w[..., OFF_XBC:OFF_XBC + 3072])
    return jnp.concatenate([w[..., OFF_GATES:OFF_GATES + 2048], w[..., OFF_Z:OFF_Z + 2048], xbc,
                            w[..., OFF_DT:OFF_DT + 32], w[..., OFF_LX:OFF_LX + 1280], w[..., OFF_LY:OFF_LY + 1280]], axis=-1)


def _col_shards(w, n=N_CHIPS):
    r, c = w.shape
    return jnp.transpose(w.reshape(r, n, c // n), (1, 0, 2))


def _from_col_shards(w):
    n, r, c = w.shape
    return jnp.transpose(w, (1, 0, 2)).reshape(r, n * c)


def kernel(x, norm1_w, w_in, b_branch_gate, ssm_conv_w, ssm_conv_b, ssm_dt_bias, ssm_a_log, ssm_d, ssm_norm_w, w_out_ssm, lru_conv_w, lru_conv_b, lru_w_r, lru_b_r, lru_w_i, lru_b_i, lru_lambda, w_out_lru, w_out, norm2_w, w_ffn_in, w_ffn_out, norm_f_w, loss_target, m_norm1_w, m_w_in, m_b_branch_gate, m_ssm_conv_w, m_ssm_conv_b, m_ssm_dt_bias, m_ssm_a_log, m_ssm_d, m_ssm_norm_w, m_w_out_ssm, m_lru_conv_w, m_lru_conv_b, m_lru_w_r, m_lru_b_r, m_lru_w_i, m_lru_b_i, m_lru_lambda, m_w_out_lru, m_w_out, m_norm2_w, m_w_ffn_in, m_w_ffn_out, m_norm_f_w, v_norm1_w, v_w_in, v_b_branch_gate, v_ssm_conv_w, v_ssm_conv_b, v_ssm_dt_bias, v_ssm_a_log, v_ssm_d, v_ssm_norm_w, v_w_out_ssm, v_lru_conv_w, v_lru_conv_b, v_lru_w_r, v_lru_b_r, v_lru_w_i, v_lru_b_i, v_lru_lambda, v_w_out_lru, v_w_out, v_norm2_w, v_w_ffn_in, v_w_ffn_out, v_norm_f_w):
    xi, yi, ci = lax.axis_index("x"), lax.axis_index("y"), lax.axis_index("c")
    me = 2 * xi + yi
    idx = jnp.stack([me, ci]).astype(jnp.int32)
    x2 = x[0]
    tgt = loss_target[0]

    big_names = ["w_in", "w_out_ssm", "w_out_lru", "w_out", "w_ffn_in", "w_ffn_out"]
    big_w = dict(w_in=w_in[0], w_out_ssm=w_out_ssm[0], w_out_lru=w_out_lru[0], w_out=w_out[0], w_ffn_in=w_ffn_in[0],
                 w_ffn_out=w_ffn_out[0])
    big_m = dict(w_in=m_w_in[0], w_out_ssm=m_w_out_ssm[0], w_out_lru=m_w_out_lru[0], w_out=m_w_out[0],
                 w_ffn_in=m_w_ffn_in[0], w_ffn_out=m_w_ffn_out[0])
    big_v = dict(w_in=v_w_in[0], w_out_ssm=v_w_out_ssm[0], w_out_lru=v_w_out_lru[0], w_out=v_w_out[0],
                 w_ffn_in=v_w_ffn_in[0], w_ffn_out=v_w_ffn_out[0])
    conv_pad = jnp.zeros((16, 768), f32).at[0:4, :].set(ssm_conv_w[0]).at[8:12, 0:320].set(lru_conv_w[0])
    mine = [big_w[k].astype(bf16) for k in big_names] + [conv_pad]
    gathered = gather_weights(mine, name="gather_weights")
    gathered = [lax.dynamic_update_index_in_dim(g, s, me, 0) for g, s in zip(gathered, mine)]
    g_in, g_out_ssm, g_out_lru, g_out, g_ffn_in, g_ffn_out, g_conv = gathered
    w_in_p = _perm_in_cols(_from_col_shards(g_in))
    w_out_ssm_f = g_out_ssm.reshape(SSM_D_INNER, D_MODEL)
    w_out_lru_f = g_out_lru.reshape(LRU_WIDTH, D_MODEL)
    w_out_f = g_out.reshape(D_MODEL, D_MODEL)
    w_ffn_in_f = _from_col_shards(g_ffn_in)
    w_ffn_out_f = g_ffn_out.reshape(FFN_HIDDEN, D_MODEL)
    ssm_cw_full = _from_col_shards(g_conv[:, 0:4, :])
    lru_cw_full = _from_col_shards(g_conv[:, 8:12, 0:320])
    ssm_cw_p = _perm_xbc_cols(ssm_cw_full)
    ssm_cb_p = _perm_xbc_cols(ssm_conv_b)

    par = jnp.stack([ssm_dt_bias[0], ssm_a_log[0], ssm_d[0]], axis=0).reshape(3, SSM_GROUPS, SSM_HPG)
    par_row = jnp.zeros((SSM_GROUPS, 8, 8), f32).at[:, 0:3, :].set(jnp.transpose(par, (1, 0, 2)))
    par_col = jnp.transpose(par_row, (0, 2, 1))

    hn1 = rms_fwd(x2, norm1_w, name="rms1_fwd")
    proj = mm(hn1, w_in_p, "nn", name="in_proj")
    t = x2.shape[0]
    dtr = jnp.transpose(proj[:, OFF_DT:OFF_DT + 32].reshape(t, SSM_GROUPS, SSM_HPG), (1, 0, 2))
    dtr_t = jnp.transpose(dtr, (0, 2, 1))
    xbc_pre, xbc_post = conv_fwd(proj, OFF_XBC, SSM_CONV_DIM, ssm_cw_p, ssm_cb_p, silu=True, name="ssm_conv_fwd")
    y_ssd, s_in = ssd_fwd(xbc_post, dtr, dtr_t, par_row, par_col, name="ssd_fwd")
    yn = gnorm_fwd(y_ssd, proj, ssm_norm_w, name="gnorm_fwd")
    y_ssm = mm(yn, w_out_ssm_f, "nn", name="out_ssm")
    (u_lru,) = conv_fwd(proj, OFF_LX, LRU_WIDTH, lru_cw_full, lru_conv_b, silu=False, name="lru_conv_fwd")
    h_lru, o_lru = lru_fwd(u_lru, proj, lru_w_r[0], lru_b_r, lru_w_i[0], lru_b_i, lru_lambda, name="lru_fwd")
    y_lru = mm(o_lru, w_out_lru_f, "nn", name="out_lru")
    mix = merge_fwd(proj, b_branch_gate, y_ssm, y_lru, name="merge_fwd")
    h1 = mm(mix, w_out_f, "nn", add=x2, name="out_proj")
    hn2 = rms_fwd(h1, norm2_w, name="rms2_fwd")
    ff = mm(hn2, w_ffn_in_f, "nn", name="ffn_in")
    act = swiglu_fwd(ff, name="swiglu_fwd")
    h2 = mm(act, w_ffn_out_f, "nn", add=h1, name="ffn_out")
    loss_tile, dh2, d_norm_f = loss_head(h2, norm_f_w.reshape(1, D_MODEL), tgt, name="loss_head")
    loss = lax.psum(loss_tile[0, 0], ("x", "y", "c"))

    d_w_ffn_out = mm(act, dh2, "tn", name="d_w_ffn_out")
    dact = mm(dh2, w_ffn_out_f, "nt", name="d_act")
    dff = swiglu_bwd(ff, dact, name="swiglu_bwd")
    d_w_ffn_in = mm(hn2, dff, "tn", name="d_w_ffn_in")
    dhn2 = mm(dff, w_ffn_in_f, "nt", name="d_hn2")
    dh1, d_norm2 = rms_bwd(h1, norm2_w, dhn2, dh2, name="rms2_bwd")
    d_w_out = mm(mix, dh1, "tn", name="d_w_out")
    dmix = mm(dh1, w_out_f, "nt", name="d_mix")
    dproj, dy_ssm, dy_lru, d_bg = merge_bwd(proj, b_branch_gate, y_ssm, y_lru, dmix, name="merge_bwd")
    d_w_out_ssm = mm(yn, dy_ssm, "tn", name="d_w_out_ssm")
    dyn = mm(dy_ssm, w_out_ssm_f, "nt", name="d_yn")
    dy_ssd, dproj, d_ssm_norm = gnorm_bwd(y_ssd, proj, ssm_norm_w, dyn, dproj, name="gnorm_bwd")
    dxbc_post, ddtr, dpar = ssd_bwd(xbc_post, dtr, dtr_t, par_row, par_col, s_in, dy_ssd, name="ssd_bwd")
    dproj, d_ssm_cw_p, d_ssm_cb_p = conv_bwd(dxbc_post, xbc_pre, proj, OFF_XBC, ssm_cw_p, dproj, name="ssm_conv_bwd")
    d_w_out_lru = mm(o_lru, dy_lru, "tn", name="d_w_out_lru")
    do_lru = mm(dy_lru, w_out_lru_f, "nt", name="d_o_lru")
    du_lru, dproj, d_w_r, d_w_i, d_b_r, d_b_i, d_lam = lru_bwd(u_lru, proj, h_lru, do_lru, lru_w_r[0], lru_b_r, lru_w_i[0],
                                                               lru_b_i, lru_lambda, dproj, name="lru_bwd")
    dproj, d_lru_cw, d_lru_cb = conv_bwd(du_lru, None, proj, OFF_LX, lru_cw_full, dproj, name="lru_conv_bwd")
    ddt_cols = jnp.transpose(ddtr, (1, 0, 2)).reshape(t, SSM_HEADS).astype(bf16)
    ddt_cols = jnp.pad(ddt_cols, ((0, 0), (0, DT_PAD_W - SSM_HEADS)))
    dproj = lax.dynamic_update_slice(dproj, ddt_cols, (0, OFF_DT))
    d_w_in_p = mm(hn1, dproj, "tn", name="d_w_in")
    dhn1 = mm(dproj, w_in_p, "nt", name="d_hn1")
    grad_x, d_norm1 = rms_bwd(x2, norm1_w, dhn1, dh1, name="rms1_bwd")

    big_g = [_col_shards(_unperm_in_cols(d_w_in_p)), d_w_out_ssm.reshape(N_CHIPS, 512, D_MODEL),
             d_w_out_lru.reshape(N_CHIPS, 320, D_MODEL), d_w_out.reshape(N_CHIPS, 256, D_MODEL),
             _col_shards(d_w_ffn_in), d_w_ffn_out.reshape(N_CHIPS, 704, D_MODEL)]
    from_sibling = pair_exchange(big_g, name="pair_exchange")
    pair_sums = [pair_add(g, rb, idx, name="pair_add_" + k) for g, rb, k in zip(big_g, from_sibling, big_names)]
    from_chips = chip_exchange([p[0] for p in pair_sums], name="chip_exchange")
    reduced_half = [chip_sum(p[1], rb, idx, name="chip_sum_" + k) for p, rb, k in zip(pair_sums, from_chips, big_names)]
    reduced = pair_gather(reduced_half, name="pair_gather")
    big_out = {}
    for k, g in zip(big_names, reduced):
        big_out[k] = (g,) + tuple(adamw(big_w[k], g, big_m[k], big_v[k], name="adamw_" + k))

    d_ssm_cw = _unperm_xbc_cols(d_ssm_cw_p)
    d_ssm_cb = _unperm_xbc_cols(d_ssm_cb_p)
    dpar_h = jnp.transpose(dpar[:, 0:3, :], (1, 0, 2)).reshape(3, SSM_HEADS)
    small_names = ["norm1_w", "b_branch_gate", "ssm_conv_b", "ssm_dt_bias", "ssm_a_log", "ssm_d", "ssm_norm_w",
                   "lru_conv_b", "lru_w_r", "lru_b_r", "lru_w_i", "lru_b_i", "lru_lambda", "norm2_w", "norm_f_w"]
    small_g = dict(norm1_w=d_norm1, b_branch_gate=d_bg, ssm_conv_b=d_ssm_cb, ssm_dt_bias=dpar_h[0:1], ssm_a_log=dpar_h[1:2],
                   ssm_d=dpar_h[2:3], ssm_norm_w=d_ssm_norm, lru_conv_b=d_lru_cb, lru_w_r=d_w_r[None], lru_b_r=d_b_r,
                   lru_w_i=d_w_i[None], lru_b_i=d_b_i, lru_lambda=d_lam, norm2_w=d_norm2, norm_f_w=d_norm_f.reshape(D_MODEL))
    small_w = dict(norm1_w=norm1_w, b_branch_gate=b_branch_gate, ssm_conv_b=ssm_conv_b, ssm_dt_bias=ssm_dt_bias,
                   ssm_a_log=ssm_a_log, ssm_d=ssm_d, ssm_norm_w=ssm_norm_w, lru_conv_b=lru_conv_b, lru_w_r=lru_w_r,
                   lru_b_r=lru_b_r, lru_w_i=lru_w_i, lru_b_i=lru_b_i, lru_lambda=lru_lambda, norm2_w=norm2_w, norm_f_w=norm_f_w)
    small_m = dict(norm1_w=m_norm1_w, b_branch_gate=m_b_branch_gate, ssm_conv_b=m_ssm_conv_b, ssm_dt_bias=m_ssm_dt_bias,
                   ssm_a_log=m_ssm_a_log, ssm_d=m_ssm_d, ssm_norm_w=m_ssm_norm_w, lru_conv_b=m_lru_conv_b, lru_w_r=m_lru_w_r,
                   lru_b_r=m_lru_b_r, lru_w_i=m_lru_w_i, lru_b_i=m_lru_b_i, lru_lambda=m_lru_lambda, norm2_w=m_norm2_w,
                   norm_f_w=m_norm_f_w)
    small_v = dict(norm1_w=v_norm1_w, b_branch_gate=v_b_branch_gate, ssm_conv_b=v_ssm_conv_b, ssm_dt_bias=v_ssm_dt_bias,
                   ssm_a_log=v_ssm_a_log, ssm_d=v_ssm_d, ssm_norm_w=v_ssm_norm_w, lru_conv_b=v_lru_conv_b, lru_w_r=v_lru_w_r,
                   lru_b_r=v_lru_b_r, lru_w_i=v_lru_w_i, lru_b_i=v_lru_b_i, lru_lambda=v_lru_lambda, norm2_w=v_norm2_w,
                   norm_f_w=v_norm_f_w)
    shapes = [small_w[k].shape for k in small_names]
    conv_shapes = [(4, SSM_CONV_DIM), (4, LRU_WIDTH)]
    g_pack = _pack([small_g[k] for k in small_names] + [d_ssm_cw, d_lru_cw])
    g_all = lax.dynamic_update_index_in_dim(all_exchange(g_pack, name="all_exchange"), g_pack, 2 * me + ci, 0)
    g_sum = sum8(g_all, name="sum8")
    g_ssm_cw_full, g_lru_cw_full = _unpack(g_sum, shapes + conv_shapes)[len(shapes):]
    g_ssm_cw = lax.dynamic_slice_in_dim(g_ssm_cw_full, me * 768, 768, axis=1)
    g_lru_cw = lax.dynamic_slice_in_dim(g_lru_cw_full, me * 320, 320, axis=1)
    loc_shapes = [(4, 768), (4, 320)]
    g_loc = _pack(_unpack(g_sum, shapes)[:len(shapes)] + [g_ssm_cw, g_lru_cw])
    w_loc = _pack([small_w[k] for k in small_names] + [ssm_conv_w[0], lru_conv_w[0]])
    m_loc = _pack([small_m[k] for k in small_names] + [m_ssm_conv_w[0], m_lru_conv_w[0]])
    v_loc = _pack([small_v[k] for k in small_names] + [v_ssm_conv_w[0], v_lru_conv_w[0]])
    d_loc, nm_loc, nv_loc = adamw(w_loc, g_loc, m_loc, v_loc, name="adamw_small")
    small_out = {}
    unp = [_unpack(b, shapes + loc_shapes) for b in (g_loc, d_loc, nm_loc, nv_loc)]
    for i, k in enumerate(small_names + ["ssm_conv_w", "lru_conv_w"]):
        small_out[k] = tuple(u[i] for u in unp)

    order = ["norm1_w", "w_in", "b_branch_gate", "ssm_conv_w", "ssm_conv_b", "ssm_dt_bias", "ssm_a_log", "ssm_d", "ssm_norm_w",
             "w_out_ssm", "lru_conv_w", "lru_conv_b", "lru_w_r", "lru_b_r", "lru_w_i", "lru_b_i", "lru_lambda", "w_out_lru",
             "w_out", "norm2_w", "w_ffn_in", "w_ffn_out", "norm_f_w"]
    outs = [loss, grad_x[None]]
    for which in range(4):
        for k in order:
            if k in big_out:
                outs.append(big_out[k][which][None])
            elif k in ("ssm_conv_w", "lru_conv_w"):
                outs.append(small_out[k][which][None])
            else:
                outs.append(small_out[k][which])
    return tuple(outs)
```

```python
import functools
import math

import jax
import jax.numpy as jnp
from jax import lax
from jax.experimental import pallas as pl
from jax.experimental.pallas import tpu as pltpu

f32 = jnp.float32
bf16 = jnp.bfloat16

D_MODEL = 1024
SSM_D_INNER = 2048
SSM_HEADS = 32
SSM_HEAD_DIM = 64
SSM_GROUPS = 4
SSM_HPG = 8
SSM_D_STATE = 128
SSM_CHUNK = 128
SSM_GROUP_W = 512
SSM_CONV_DIM = 3072
XBC_GROUP_W = 768
LRU_WIDTH = 1280
LRU_BLOCKS = 10
LRU_BLOCK = 128
LRU_C = 8.0
FFN_HIDDEN = 2816
RMS_EPS = 1e-6
IN_PROJ_DIM = 9760
N_CHIPS = 4

OFF_GATES = 0
OFF_Z = 2048
OFF_LX = 4096
OFF_LY = 5376
OFF_DT = 6656
DT_PAD_W = 256
OFF_XBC = 6912
PROJ_W = 9984

ADAM_LR = 0.001
ADAM_B1 = 0.9
ADAM_B2 = 0.999
ADAM_EPS = 1e-08
ADAM_WD = 0.01
ADAM_STEP = 10

MESH = pl.DeviceIdType.MESH
ANY = pl.BlockSpec(memory_space=pl.ANY)

NN = (((1,), (0,)), ((), ()))
NT = (((1,), (1,)), ((), ()))
TN = (((0,), (0,)), ((), ()))


def _pick(n, cap, mult=128):
    best = None
    for t in range(mult, min(n, cap) + 1, mult):
        if n % t == 0:
            best = t
    return best if best is not None else n


def _sigmoid(x):
    return 1.0 / (1.0 + jnp.exp(-x))


def _softplus(x):
    return jnp.maximum(x, 0.0) + jnp.log(1.0 + jnp.exp(-jnp.abs(x)))


def _silu(x):
    return x * _sigmoid(x)


def _dsilu(x):
    s = _sigmoid(x)
    return s * (1.0 + x * (1.0 - s))


_GELU_K = math.sqrt(2.0 / math.pi)


def _gelu(x):
    return 0.5 * x * (1.0 + jnp.tanh(_GELU_K * (x + 0.044715 * x * x * x)))


def _dgelu(x):
    t = jnp.tanh(_GELU_K * (x + 0.044715 * x * x * x))
    return 0.5 * (1.0 + t) + 0.5 * x * (1.0 - t * t) * _GELU_K * (1.0 + 3.0 * 0.044715 * x * x)


def _expm1(x):
    poly = x * (1.0 + x * (0.5 + x * (1.0 / 6.0 + x * (1.0 / 24.0 + x * (1.0 / 120.0 + x * (1.0 / 720.0))))))
    return jnp.where(jnp.abs(x) < 0.1, poly, jnp.exp(x) - 1.0)


def _dot(a, b, dn):
    return lax.dot_general(a.astype(bf16), b.astype(bf16), dn, preferred_element_type=f32)


def _dot_01(a, b, dn, split, terms):
    r = a if split == 0 else b
    out = None
    for _ in range(terms):
        h = r.astype(bf16)
        r = r - h.astype(f32)
        d = lax.dot_general(h if split == 0 else a.astype(bf16), b.astype(bf16) if split == 0 else h, dn,
                            preferred_element_type=f32)
        out = d if out is None else out + d
    return out


def mm(a, b, mode, *, name, add=None, out_dtype=f32):
    if mode == "nn":
        (m, k), (k2, n) = a.shape, b.shape
    elif mode == "nt":
        (m, k), (n, k2) = a.shape, b.shape
    else:
        (k, m), (k2, n) = a.shape, b.shape
    assert k == k2, (a.shape, b.shape, mode)
    tm, tn, tk = _pick(m, 1024), _pick(n, 1024), _pick(k, 1024)
    nk = k // tk
    dn = {"nn": NN, "nt": NT, "tn": TN}[mode]
    a_spec = pl.BlockSpec((tk, tm), lambda i, j, kk: (kk, i)) if mode == "tn" else pl.BlockSpec((tm, tk), lambda i, j, kk: (i, kk))
    b_spec = pl.BlockSpec((tn, tk), lambda i, j, kk: (j, kk)) if mode == "nt" else pl.BlockSpec((tk, tn), lambda i, j, kk: (kk, j))
    o_spec = pl.BlockSpec((tm, tn), lambda i, j, kk: (i, j))
    has_add = add is not None

    def body(a_ref, b_ref, *rest):
        add_ref = rest[0] if has_add else None
        o_ref = rest[1] if has_add else rest[0]

        def finish(r):
            if has_add:
                r = r + add_ref[...]
            o_ref[...] = r.astype(out_dtype)

        if nk == 1:
            finish(_dot(a_ref[...], b_ref[...], dn))
            return
        acc = rest[-1]
        kk = pl.program_id(2)

        @pl.when(kk == 0)
        def _():
            acc[...] = jnp.zeros_like(acc)

        acc[...] += _dot(a_ref[...], b_ref[...], dn)

        @pl.when(kk == nk - 1)
        def _():
            finish(acc[...])

    ins = [a, b] + ([add] if has_add else [])
    in_specs = [a_spec, b_spec] + ([o_spec] if has_add else [])
    return pl.pallas_call(
        body, name=name, grid=(m // tm, n // tn, nk), in_specs=in_specs, out_specs=o_spec,
        out_shape=jax.ShapeDtypeStruct((m, n), out_dtype),
        scratch_shapes=[pltpu.VMEM((tm, tn), f32)] if nk > 1 else [],
        compiler_params=pltpu.CompilerParams(dimension_semantics=("parallel", "parallel", "arbitrary")),
    )(*ins)


def rms_fwd(x, w, *, name):
    t, d = x.shape
    tr = _pick(t, 256, 8)

    def body(x_ref, w_ref, o_ref):
        xv = x_ref[...]
        r = lax.rsqrt(jnp.mean(xv * xv, axis=-1, keepdims=True) + RMS_EPS)
        o_ref[...] = (xv * r * w_ref[...]).astype(bf16)

    return pl.pallas_call(
        body, name=name, grid=(t // tr,),
        in_specs=[pl.BlockSpec((tr, d), lambda i: (i, 0)), pl.BlockSpec((1, d), lambda i: (0, 0))],
        out_specs=pl.BlockSpec((tr, d), lambda i: (i, 0)), out_shape=jax.ShapeDtypeStruct((t, d), bf16),
    )(x, w)


def _rms_bwd_math(xv, wv, dy):
    r = lax.rsqrt(jnp.mean(xv * xv, axis=-1, keepdims=True) + RMS_EPS)
    g = dy * wv
    dx = r * g - xv * (r * r * r) * jnp.mean(g * xv, axis=-1, keepdims=True)
    dw = jnp.sum(dy * xv * r, axis=0, keepdims=True)
    return dx, dw


def rms_bwd(x, w, dy, res, *, name):
    t, d = x.shape
    tr = _pick(t, 256, 8)

    def body(x_ref, w_ref, dy_ref, res_ref, dx_ref, dw_ref):
        dx, dw = _rms_bwd_math(x_ref[...], w_ref[...], dy_ref[...])
        dx_ref[...] = dx + res_ref[...]

        @pl.when(pl.program_id(0) == 0)
        def _():
            dw_ref[...] = jnp.zeros_like(dw_ref)

        dw_ref[...] += dw

    row = pl.BlockSpec((tr, d), lambda i: (i, 0))
    vec = pl.BlockSpec((1, d), lambda i: (0, 0))
    return pl.pallas_call(
        body, name=name, grid=(t // tr,), in_specs=[row, vec, row, row], out_specs=[row, vec],
        out_shape=[jax.ShapeDtypeStruct((t, d), f32), jax.ShapeDtypeStruct((1, d), f32)],
        compiler_params=pltpu.CompilerParams(dimension_semantics=("arbitrary",)),
    )(x, w, dy, res)


def loss_head(h, w, target, *, name):
    t, d = h.shape
    tr = _pick(t, 256, 8)

    def body(h_ref, w_ref, t_ref, loss_ref, dh_ref, dw_ref):
        xv, wv = h_ref[...], w_ref[...]
        r = lax.rsqrt(jnp.mean(xv * xv, axis=-1, keepdims=True) + RMS_EPS)
        err = xv * r * wv - t_ref[...]
        part = 0.5 * jnp.sum(jnp.mean(err * err, axis=-1, keepdims=True), axis=0, keepdims=True)
        dx, dw = _rms_bwd_math(xv, wv, err * (1.0 / d))
        dh_ref[...] = dx

        @pl.when(pl.program_id(0) == 0)
        def _():
            dw_ref[...] = jnp.zeros_like(dw_ref)
            loss_ref[...] = jnp.zeros_like(loss_ref)

        dw_ref[...] += dw
        loss_ref[...] += part

    row = pl.BlockSpec((tr, d), lambda i: (i, 0))
    vec = pl.BlockSpec((1, d), lambda i: (0, 0))
    return pl.pallas_call(
        body, name=name, grid=(t // tr,), in_specs=[row, vec, row],
        out_specs=[pl.BlockSpec((8, 128), lambda i: (0, 0)), row, vec],
        out_shape=[jax.ShapeDtypeStruct((8, 128), f32), jax.ShapeDtypeStruct((t, d), f32), jax.ShapeDtypeStruct((1, d), f32)],
        compiler_params=pltpu.CompilerParams(dimension_semantics=("arbitrary",)),
    )(h, w, target)


CONV_ROWS = 256


def conv_fwd(src, col0, width, w, b, *, silu, name):
    t = src.shape[0]
    tc = _pick(math.gcd(width, col0), 768)
    assert col0 % tc == 0
    cb = col0 // tc
    r = CONV_ROWS

    def body(u_ref, w_ref, b_ref, *rest):
        ext = rest[-1]
        j = pl.program_id(1)

        @pl.when(j == 0)
        def _():
            ext[0:8, :] = jnp.zeros((8, tc), f32)

        @pl.when(j > 0)
        def _():
            ext[0:8, :] = ext[r:r + 8, :]

        ext[8:r + 8, :] = u_ref[...]
        v = ext[...]
        wv = w_ref[...]
        acc = b_ref[...] + wv[3:4, :] * v
        for s in (1, 2, 3):
            acc = acc + wv[3 - s:4 - s, :] * pltpu.roll(v, s, 0)
        pre = acc[8:, :]
        rest[0][...] = pre
        if silu:
            rest[1][...] = _silu(pre)

    tile = pl.BlockSpec((r, tc), lambda c, j: (j, c))
    n_out = 2 if silu else 1
    return pl.pallas_call(
        body, name=name, grid=(width // tc, t // r),
        in_specs=[pl.BlockSpec((r, tc), lambda c, j: (j, cb + c)), pl.BlockSpec((4, tc), lambda c, j: (0, c)),
                  pl.BlockSpec((1, tc), lambda c, j: (0, c))],
        out_specs=[tile] * n_out, out_shape=[jax.ShapeDtypeStruct((t, width), f32)] * n_out,
        scratch_shapes=[pltpu.VMEM((r + 8, tc), f32)],
        compiler_params=pltpu.CompilerParams(dimension_semantics=("parallel", "arbitrary")),
    )(src, w, b)


def conv_bwd(dpost, pre, src, col0, w, dst, *, name):
    t, width = dpost.shape
    tc = _pick(math.gcd(width, col0), 768)
    assert col0 % tc == 0
    cb = col0 // tc
    r = CONV_ROWS
    nt = t // r
    has_pre = pre is not None

    def body(*refs):
        refs = refs[1:]
        if has_pre:
            d_ref, p_ref, u_ref, w_ref, du_ref, dw_ref, db_ref, ext = refs
        else:
            d_ref, u_ref, w_ref, du_ref, dw_ref, db_ref, ext = refs
        j = pl.program_id(1)

        @pl.when(j == 0)
        def _():
            ext[r:r + 8, :] = jnp.zeros((8, tc), f32)
            dw_ref[...] = jnp.zeros_like(dw_ref)
            db_ref[...] = jnp.zeros_like(db_ref)

        @pl.when(j > 0)
        def _():
            ext[r:r + 8, :] = ext[0:8, :]

        dpre = d_ref[...]
        if has_pre:
            dpre = dpre * _dsilu(p_ref[...])
        ext[0:r, :] = dpre
        v = ext[...]
        wv = w_ref[...]
        uv = u_ref[...]
        du = wv[3:4, :] * dpre
        dw_ref[3:4, :] += jnp.sum(dpre * uv, axis=0, keepdims=True)
        for s in (1, 2, 3):
            sh = pltpu.roll(v, r + 8 - s, 0)[0:r, :]
            du = du + wv[3 - s:4 - s, :] * sh
            dw_ref[3 - s:4 - s, :] += jnp.sum(sh * uv, axis=0, keepdims=True)
        db_ref[...] += jnp.sum(dpre, axis=0, keepdims=True)
        du_ref[...] = du.astype(bf16)

    rev = pl.BlockSpec((r, tc), lambda c, j: (nt - 1 - j, c))
    win = pl.BlockSpec((r, tc), lambda c, j: (nt - 1 - j, cb + c))
    in_specs = [ANY, rev] + ([rev] if has_pre else []) + [win, pl.BlockSpec((4, tc), lambda c, j: (0, c))]
    ins = [dst, dpost] + ([pre] if has_pre else []) + [src, w]
    return pl.pallas_call(
        body, name=name, grid=(width // tc, nt), in_specs=in_specs,
        out_specs=[win, pl.BlockSpec((4, tc), lambda c, j: (0, c)), pl.BlockSpec((1, tc), lambda c, j: (0, c))],
        out_shape=[jax.ShapeDtypeStruct(dst.shape, bf16), jax.ShapeDtypeStruct((4, width), f32),
                   jax.ShapeDtypeStruct((1, width), f32)],
        input_output_aliases={0: 0},
        scratch_shapes=[pltpu.VMEM((r + 8, tc), f32)],
        compiler_params=pltpu.CompilerParams(dimension_semantics=("parallel", "arbitrary")),
    )(*ins)


def _ssd_common(xbc_ref, dtr_ref, dtrT_ref, par_row_ref, par_col_ref):
    l = SSM_CHUNK
    x = xbc_ref[:, 0:SSM_GROUP_W]
    bm = xbc_ref[:, SSM_GROUP_W:SSM_GROUP_W + SSM_D_STATE]
    cm = xbc_ref[:, SSM_GROUP_W + SSM_D_STATE:XBC_GROUP_W]
    par_row = par_row_ref[0]
    par_col = par_col_ref[0]
    bias_row, alog_row, d_row = par_row[0:1, :], par_row[1:2, :], par_row[2:3, :]
    bias_col, alog_col = par_col[:, 0:1], par_col[:, 1:2]
    dtr = dtr_ref[0]
    dt = _softplus(dtr + bias_row)
    dt_t = _softplus(dtrT_ref[0] + bias_col)
    a_row = -jnp.exp(alog_row)
    a_col = -jnp.exp(alog_col)
    li = lax.broadcasted_iota(jnp.int32, (l, l), 0)
    si = lax.broadcasted_iota(jnp.int32, (l, l), 1)
    tri = (li >= si).astype(f32)
    cs = _dot_01(tri, dt * a_row, NN, 1, 3)
    cs_t = _dot_01(dt_t * a_col, tri, NT, 0, 3)
    off = lax.broadcasted_iota(jnp.int32, (SSM_HPG, SSM_GROUP_W), 1) - SSM_HEAD_DIM * lax.broadcasted_iota(
        jnp.int32, (SSM_HPG, SSM_GROUP_W), 0)
    ex = ((off >= 0) & (off < SSM_HEAD_DIM)).astype(f32)
    cs_x = _dot_01(cs, ex, NN, 0, 3)
    cl_x = cs_x[l - 1:l, :]
    return dict(x=x, bm=bm, cm=cm, dtr=dtr, dt=dt, a_row=a_row, bias_row=bias_row, tri=tri, li=li, si=si, cs=cs,
                cs_t=cs_t, ex=ex, dt_x=_dot_01(dt, ex, NN, 0, 2), d_x=_dot_01(par_row, ex, NN, 0, 2)[2:3, :], e_x=jnp.exp(cs_x),
                el_x=jnp.exp(cl_x), dec_x=jnp.exp(cl_x - cs_x))


def ssd_fwd(xbc, dtr, dtr_t, par_row, par_col, *, name):
    t = xbc.shape[0]
    nc = t // SSM_CHUNK
    l, p = SSM_CHUNK, SSM_HEAD_DIM

    def body(xbc_ref, dtr_ref, dtrT_ref, prow_ref, pcol_ref, y_ref, sin_ref, state):
        @pl.when(pl.program_id(1) == 0)
        def _():
            state[...] = jnp.zeros_like(state)

        q = _ssd_common(xbc_ref, dtr_ref, dtrT_ref, prow_ref, pcol_ref)
        st = state[...]
        sin_ref[0] = st
        xd = q["x"] * q["dt_x"]
        g = _dot(q["cm"], q["bm"], NT)
        for r in range(SSM_HPG):
            sl = slice(r * p, (r + 1) * p)
            diff = q["cs"][:, r:r + 1] - q["cs_t"][r:r + 1, :]
            lm = jnp.where(q["li"] >= q["si"], jnp.exp(jnp.minimum(diff, 0.0)), 0.0)
            y_ref[:, sl] = _dot(g * lm, xd[:, sl], NN)
        y_ref[...] += q["e_x"] * _dot(q["cm"], st, NN) + q["d_x"] * q["x"]
        state[...] = q["el_x"] * st + _dot(q["bm"].T, xd * q["dec_x"], NN)

    return pl.pallas_call(
        body, name=name, grid=(SSM_GROUPS, nc),
        in_specs=[pl.BlockSpec((l, XBC_GROUP_W), lambda g, c: (c, g)),
                  pl.BlockSpec((1, l, SSM_HPG), lambda g, c: (g, c, 0)),
                  pl.BlockSpec((1, SSM_HPG, l), lambda g, c: (g, 0, c)),
                  pl.BlockSpec((1, 8, 8), lambda g, c: (g, 0, 0)),
                  pl.BlockSpec((1, 8, 8), lambda g, c: (g, 0, 0))],
        out_specs=[pl.BlockSpec((l, SSM_GROUP_W), lambda g, c: (c, g)),
                   pl.BlockSpec((1, SSM_D_STATE, SSM_GROUP_W), lambda g, c: (c, 0, g))],
        out_shape=[jax.ShapeDtypeStruct((t, SSM_D_INNER), f32),
                   jax.ShapeDtypeStruct((nc, SSM_D_STATE, SSM_D_INNER), f32)],
        scratch_shapes=[pltpu.VMEM((SSM_D_STATE, SSM_GROUP_W), f32)],
        compiler_params=pltpu.CompilerParams(dimension_semantics=("parallel", "arbitrary")),
    )(xbc, dtr, dtr_t, par_row, par_col)


def ssd_bwd(xbc, dtr, dtr_t, par_row, par_col, s_in, dy, *, name):
    t = xbc.shape[0]
    nc = t // SSM_CHUNK
    l, p = SSM_CHUNK, SSM_HEAD_DIM

    def body(xbc_ref, dtr_ref, dtrT_ref, prow_ref, pcol_ref, sin_ref, dy_ref, dxbc_ref, ddtr_ref, dpar_ref,
             dstate, yd_buf, dxd_buf):
        @pl.when(pl.program_id(1) == 0)
        def _():
            dstate[...] = jnp.zeros_like(dstate)
            dpar_ref[...] = jnp.zeros_like(dpar_ref)

        q = _ssd_common(xbc_ref, dtr_ref, dtrT_ref, prow_ref, pcol_ref)
        x, bm, cm, ex, li, si = q["x"], q["bm"], q["cm"], q["ex"], q["li"], q["si"]
        e_x, el_x, dec_x = q["e_x"], q["el_x"], q["dec_x"]
        st = sin_ref[0]
        dst = dstate[...]
        dy = dy_ref[...]
        xd = x * q["dt_x"]
        g = _dot(cm, bm, NT)
        dg = jnp.zeros((l, l), f32)
        for r in range(SSM_HPG):
            sl = slice(r * p, (r + 1) * p)
            diff = q["cs"][:, r:r + 1] - q["cs_t"][r:r + 1, :]
            lm = jnp.where(li >= si, jnp.exp(jnp.minimum(diff, 0.0)), 0.0)
            m = (g * lm).astype(bf16)
            xdh, dyh = xd[:, sl].astype(bf16), dy[:, sl].astype(bf16)
            yd_buf[:, sl] = _dot(m, xdh, NN)
            dxd_buf[:, sl] = _dot(m, dyh, TN)
            dg = dg + _dot(dyh, xdh, NT) * lm
        yd, dxd_diag = yd_buf[...], dxd_buf[...]
        yo = e_x * _dot(cm, st, NN)
        dz = e_x * dy
        wv = _dot(bm, dst, NN)
        xw = xd * wv * dec_x
        row8 = lax.broadcasted_iota(jnp.int32, (l, SSM_HPG), 0)
        dy_b, xd_b = dy.astype(bf16).astype(f32), xd.astype(bf16).astype(f32)
        dcs = _dot_01(dy_b * yd - xd_b * dxd_diag + dy * yo - xw, ex, NT, 0, 3)
        tail = jnp.sum(xw, axis=0, keepdims=True) + el_x * jnp.sum(dst * st, axis=0, keepdims=True)
        dcl = _dot_01(jnp.broadcast_to(tail, (SSM_HPG, SSM_GROUP_W)), ex, NT, 0, 3)[0:1, :]
        dcs = dcs + jnp.where(row8 == l - 1, dcl, 0.0)
        dda = _dot_01(q["tri"], dcs, TN, 1, 3)
        dxd = dxd_diag + dec_x * wv
        ddt = _dot_01(dxd * x, ex, NT, 0, 3) + dda * q["a_row"]
        ddtr = ddt * _sigmoid(q["dtr"] + q["bias_row"])
        ddtr_ref[0] = ddtr
        dd = _dot_01(jnp.broadcast_to(jnp.sum(dy * x, axis=0, keepdims=True), (SSM_HPG, SSM_GROUP_W)), ex, NT, 0, 2)[0:1, :]
        dpar_ref[0, 0:1, :] += jnp.sum(ddtr, axis=0, keepdims=True)
        dpar_ref[0, 1:2, :] += jnp.sum(dda * q["dt"], axis=0, keepdims=True) * q["a_row"]
        dpar_ref[0, 2:3, :] += dd
        dxbc_ref[:, 0:SSM_GROUP_W] = dxd * q["dt_x"] + q["d_x"] * dy
        dxbc_ref[:, SSM_GROUP_W:SSM_GROUP_W + SSM_D_STATE] = _dot(dg, cm, TN) + _dot(xd * dec_x, dst, NT)
        dxbc_ref[:, SSM_GROUP_W + SSM_D_STATE:XBC_GROUP_W] = _dot(dg, bm, NN) + _dot(dz, st, NT)
        dstate[...] = _dot(cm.T, dz, NN) + el_x * dst

    rc = lambda c: nc - 1 - c
    return pl.pallas_call(
        body, name=name, grid=(SSM_GROUPS, nc),
        in_specs=[pl.BlockSpec((l, XBC_GROUP_W), lambda g, c: (rc(c), g)),
                  pl.BlockSpec((1, l, SSM_HPG), lambda g, c: (g, rc(c), 0)),
                  pl.BlockSpec((1, SSM_HPG, l), lambda g, c: (g, 0, rc(c))),
                  pl.BlockSpec((1, 8, 8), lambda g, c: (g, 0, 0)),
                  pl.BlockSpec((1, 8, 8), lambda g, c: (g, 0, 0)),
                  pl.BlockSpec((1, SSM_D_STATE, SSM_GROUP_W), lambda g, c: (rc(c), 0, g)),
                  pl.BlockSpec((l, SSM_GROUP_W), lambda g, c: (rc(c), g))],
        out_specs=[pl.BlockSpec((l, XBC_GROUP_W), lambda g, c: (rc(c), g)),
                   pl.BlockSpec((1, l, SSM_HPG), lambda g, c: (g, rc(c), 0)),
                   pl.BlockSpec((1, 8, 8), lambda g, c: (g, 0, 0))],
        out_shape=[jax.ShapeDtypeStruct((t, SSM_CONV_DIM), f32),
                   jax.ShapeDtypeStruct((SSM_GROUPS, t, SSM_HPG), f32),
                   jax.ShapeDtypeStruct((SSM_GROUPS, 8, 8), f32)],
        scratch_shapes=[pltpu.VMEM((SSM_D_STATE, SSM_GROUP_W), f32), pltpu.VMEM((l, SSM_GROUP_W), f32),
                        pltpu.VMEM((l, SSM_GROUP_W), f32)],
        compiler_params=pltpu.CompilerParams(dimension_semantics=("parallel", "arbitrary")),
    )(xbc, dtr, dtr_t, par_row, par_col, s_in, dy)


def gnorm_fwd(y, proj, w, *, name):
    t = y.shape[0]
    tr = _pick(t, 512, 8)
    gw = SSM_GROUP_W
    zb = OFF_Z // gw

    def body(y_ref, z_ref, w_ref, o_ref):
        y2 = y_ref[...] * _silu(z_ref[...])
        r = lax.rsqrt(jnp.mean(y2 * y2, axis=-1, keepdims=True) + RMS_EPS)
        o_ref[...] = (y2 * r * w_ref[...]).astype(bf16)

    return pl.pallas_call(
        body, name=name, grid=(SSM_GROUPS, t // tr),
        in_specs=[pl.BlockSpec((tr, gw), lambda g, i: (i, g)), pl.BlockSpec((tr, gw), lambda g, i: (i, zb + g)),
                  pl.BlockSpec((1, gw), lambda g, i: (0, g))],
        out_specs=pl.BlockSpec((tr, gw), lambda g, i: (i, g)), out_shape=jax.ShapeDtypeStruct((t, SSM_D_INNER), bf16),
    )(y, proj, w)


def gnorm_bwd(y, proj, w, dout, dst, *, name):
    t = y.shape[0]
    tr = _pick(t, 512, 8)
    gw = SSM_GROUP_W
    zb = OFF_Z // gw

    def body(_, y_ref, z_ref, w_ref, do_ref, dy_ref, dz_ref, dw_ref):
        yv, zv = y_ref[...], z_ref[...]
        sz = _silu(zv)
        y2 = yv * sz
        dy2, dw = _rms_bwd_math(y2, w_ref[...], do_ref[...])
        dy_ref[...] = dy2 * sz
        dz_ref[...] = (dy2 * yv * _dsilu(zv)).astype(bf16)

        @pl.when(pl.program_id(1) == 0)
        def _():
            dw_ref[...] = jnp.zeros_like(dw_ref)

        dw_ref[...] += dw

    tile = pl.BlockSpec((tr, gw), lambda g, i: (i, g))
    vec = pl.BlockSpec((1, gw), lambda g, i: (0, g))
    return pl.pallas_call(
        body, name=name, grid=(SSM_GROUPS, t // tr),
        in_specs=[ANY, tile, pl.BlockSpec((tr, gw), lambda g, i: (i, zb + g)), vec, tile],
        out_specs=[tile, pl.BlockSpec((tr, gw), lambda g, i: (i, zb + g)), vec],
        out_shape=[jax.ShapeDtypeStruct((t, SSM_D_INNER), f32), jax.ShapeDtypeStruct(dst.shape, bf16),
                   jax.ShapeDtypeStruct((1, SSM_D_INNER), f32)],
        input_output_aliases={0: 1},
        compiler_params=pltpu.CompilerParams(dimension_semantics=("parallel", "arbitrary")),
    )(dst, y, proj, w, dout)


LRU_ROWS = 256


def _lru_gates(uv, wr_ref, wi_ref, br_ref, bi_ref, lam_ref):
    rg = _sigmoid(_dot(uv, wr_ref[0], NN) + br_ref[...])
    ig = _sigmoid(_dot(uv, wi_ref[0], NN) + bi_ref[...])
    sp = _softplus(-lam_ref[...])
    la = -LRU_C * rg * sp
    a = jnp.exp(la)
    s = jnp.sqrt(jnp.maximum(-_expm1(2.0 * la), 0.0))
    return rg, ig, sp, la, a, s


def lru_fwd(u, proj, w_r, b_r, w_i, b_i, lam, *, name):
    t = u.shape[0]
    r = LRU_ROWS
    lb = LRU_BLOCK
    yb = OFF_LY // lb

    def body(u_ref, y_ref, wr_ref, br_ref, wi_ref, bi_ref, lam_ref, h_ref, o_ref, carry):
        @pl.when(pl.program_id(1) == 0)
        def _():
            carry[...] = jnp.zeros_like(carry)

        uv = u_ref[...]
        _, ig, _, _, a, s = _lru_gates(uv, wr_ref, wi_ref, br_ref, bi_ref, lam_ref)
        b = s * ig * uv
        row = lax.broadcasted_iota(jnp.int32, (r, lb), 0)
        d = 1
        while d < r:
            keep = row >= d
            b = b + a * jnp.where(keep, pltpu.roll(b, d, 0), 0.0)
            a = a * jnp.where(keep, pltpu.roll(a, d, 0), 1.0)
            d *= 2
        h = b + a * carry[0:1, :]
        carry[0:1, :] = h[r - 1:r, :]
        h_ref[...] = h
        o_ref[...] = (h * _gelu(y_ref[...])).astype(bf16)

    tile = pl.BlockSpec((r, lb), lambda hb, j: (j, hb))
    vec = pl.BlockSpec((1, lb), lambda hb, j: (0, hb))
    wsp = pl.BlockSpec((1, lb, lb), lambda hb, j: (hb, 0, 0))
    return pl.pallas_call(
        body, name=name, grid=(LRU_BLOCKS, t // r),
        in_specs=[tile, pl.BlockSpec((r, lb), lambda hb, j: (j, yb + hb)), wsp, vec, wsp, vec, vec],
        out_specs=[tile, tile],
        out_shape=[jax.ShapeDtypeStruct((t, LRU_WIDTH), f32), jax.ShapeDtypeStruct((t, LRU_WIDTH), bf16)],
        scratch_shapes=[pltpu.VMEM((8, lb), f32)],
        compiler_params=pltpu.CompilerParams(dimension_semantics=("parallel", "arbitrary")),
    )(u, proj, w_r, b_r, w_i, b_i, lam)


def lru_bwd(u, proj, hseq, dout, w_r, b_r, w_i, b_i, lam, dst, *, name):
    t = u.shape[0]
    r = LRU_ROWS
    nt = t // r
    lb = LRU_BLOCK
    yb = OFF_LY // lb

    def body(_, u_ref, y_ref, h_ref, hp_ref, do_ref, wr_ref, br_ref, wi_ref, bi_ref, lam_ref,
             du_ref, dy_ref, dwr_ref, dwi_ref, dbr_ref, dbi_ref, dlam_ref, carry_dh, carry_a):
        j = pl.program_id(1)

        @pl.when(j == 0)
        def _():
            carry_dh[...] = jnp.zeros_like(carry_dh)
            carry_a[...] = jnp.zeros_like(carry_a)
            dwr_ref[...] = jnp.zeros_like(dwr_ref)
            dwi_ref[...] = jnp.zeros_like(dwi_ref)
            dbr_ref[...] = jnp.zeros_like(dbr_ref)
            dbi_ref[...] = jnp.zeros_like(dbi_ref)
            dlam_ref[...] = jnp.zeros_like(dlam_ref)

        uv = u_ref[...]
        yv = y_ref[...]
        hv = h_ref[...]
        dov = do_ref[...]
        rg, ig, sp, la, a, s = _lru_gates(uv, wr_ref, wi_ref, br_ref, bi_ref, lam_ref)
        dy_ref[...] = (dov * hv * _dgelu(yv)).astype(bf16)
        gq = dov * _gelu(yv)
        row = lax.broadcasted_iota(jnp.int32, (r, lb), 0)
        an = jnp.where(row < r - 1, pltpu.roll(a, r - 1, 0), carry_a[0:1, :])
        d = 1
        while d < r:
            keep = row < r - d
            gq = gq + an * jnp.where(keep, pltpu.roll(gq, r - d, 0), 0.0)
            an = an * jnp.where(keep, pltpu.roll(an, r - d, 0), 1.0)
            d *= 2
        dh = gq + an * carry_dh[0:1, :]
        carry_dh[0:1, :] = dh[0:1, :]
        carry_a[0:1, :] = a[0:1, :]
        first = jnp.where(j == nt - 1, 0.0, 1.0) * hp_ref[7:8, :]
        hprev = jnp.where(row >= 1, pltpu.roll(hv, 1, 0), first)
        da = dh * hprev
        iu = ig * uv
        e2 = jnp.exp(2.0 * la)
        dla = da * a - dh * iu * e2 / jnp.maximum(s, 1e-30)
        drp = dla * (-LRU_C * sp) * rg * (1.0 - rg)
        dip = dh * s * uv * ig * (1.0 - ig)
        dlam_ref[...] += jnp.sum(dla * (LRU_C * rg) * _sigmoid(-lam_ref[...]), axis=0, keepdims=True)
        du_ref[...] = dh * s * ig + _dot(drp, wr_ref[0], NT) + _dot(dip, wi_ref[0], NT)
        dwr_ref[0] += _dot(uv, drp, TN)
        dwi_ref[0] += _dot(uv, dip, TN)
        dbr_ref[...] += jnp.sum(drp, axis=0, keepdims=True)
        dbi_ref[...] += jnp.sum(dip, axis=0, keepdims=True)

    rj = lambda j: nt - 1 - j
    tile = pl.BlockSpec((r, lb), lambda hb, j: (rj(j), hb))
    vec = pl.BlockSpec((1, lb), lambda hb, j: (0, hb))
    wsp = pl.BlockSpec((1, lb, lb), lambda hb, j: (hb, 0, 0))
    hprev_spec = pl.BlockSpec((8, lb), lambda hb, j: (jnp.maximum(rj(j) * (r // 8) - 1, 0), hb))
    ywin = pl.BlockSpec((r, lb), lambda hb, j: (rj(j), yb + hb))
    return pl.pallas_call(
        body, name=name, grid=(LRU_BLOCKS, nt),
        in_specs=[ANY, tile, ywin, tile, hprev_spec, tile, wsp, vec, wsp, vec, vec],
        out_specs=[tile, ywin, wsp, wsp, vec, vec, vec],
        out_shape=[jax.ShapeDtypeStruct((t, LRU_WIDTH), f32), jax.ShapeDtypeStruct(dst.shape, bf16),
                   jax.ShapeDtypeStruct((LRU_BLOCKS, lb, lb), f32), jax.ShapeDtypeStruct((LRU_BLOCKS, lb, lb), f32),
                   jax.ShapeDtypeStruct((1, LRU_WIDTH), f32), jax.ShapeDtypeStruct((1, LRU_WIDTH), f32),
                   jax.ShapeDtypeStruct((1, LRU_WIDTH), f32)],
        input_output_aliases={0: 1},
        scratch_shapes=[pltpu.VMEM((8, lb), f32), pltpu.VMEM((8, lb), f32)],
        compiler_params=pltpu.CompilerParams(dimension_semantics=("parallel", "arbitrary")),
    )(dst, u, proj, hseq, hseq, dout, w_r, b_r, w_i, b_i, lam)


def merge_fwd(proj, bg, y_ssm, y_lru, *, name):
    t, d = y_ssm.shape
    tr = _pick(t, 256, 8)
    gb = OFF_GATES // d

    def body(gs_ref, gl_ref, bs_ref, bl_ref, ys_ref, yl_ref, o_ref):
        gs = _sigmoid(gs_ref[...] + bs_ref[...])
        gl = _sigmoid(gl_ref[...] + bl_ref[...])
        o_ref[...] = (gs * ys_ref[...] + gl * yl_ref[...]).astype(bf16)

    row = pl.BlockSpec((tr, d), lambda i: (i, 0))
    return pl.pallas_call(
        body, name=name, grid=(t // tr,),
        in_specs=[pl.BlockSpec((tr, d), lambda i: (i, gb)), pl.BlockSpec((tr, d), lambda i: (i, gb + 1)),
                  pl.BlockSpec((1, d), lambda i: (0, 0)), pl.BlockSpec((1, d), lambda i: (0, 1)), row, row],
        out_specs=row, out_shape=jax.ShapeDtypeStruct((t, d), bf16),
    )(proj, proj, bg, bg, y_ssm, y_lru)


def merge_bwd(proj, bg, y_ssm, y_lru, dmix, *, name):
    t, d = y_ssm.shape
    tr = _pick(t, 256, 8)
    gb = OFF_GATES // d

    def body(gs_ref, gl_ref, bs_ref, bl_ref, ys_ref, yl_ref, dm_ref, dg_ref, dys_ref, dyl_ref, dbg_ref):
        gs = _sigmoid(gs_ref[...] + bs_ref[...])
        gl = _sigmoid(gl_ref[...] + bl_ref[...])
        dm = dm_ref[...]
        dys_ref[...] = (dm * gs).astype(bf16)
        dyl_ref[...] = (dm * gl).astype(bf16)
        dgs = dm * ys_ref[...] * gs * (1.0 - gs)
        dgl = dm * yl_ref[...] * gl * (1.0 - gl)
        dg_ref[:, 0:d] = dgs.astype(bf16)
        dg_ref[:, d:2 * d] = dgl.astype(bf16)

        @pl.when(pl.program_id(0) == 0)
        def _():
            dbg_ref[...] = jnp.zeros_like(dbg_ref)

        dbg_ref[:, 0:d] += jnp.sum(dgs, axis=0, keepdims=True)
        dbg_ref[:, d:2 * d] += jnp.sum(dgl, axis=0, keepdims=True)

    row = pl.BlockSpec((tr, d), lambda i: (i, 0))
    return pl.pallas_call(
        body, name=name, grid=(t // tr,),
        in_specs=[pl.BlockSpec((tr, d), lambda i: (i, gb)), pl.BlockSpec((tr, d), lambda i: (i, gb + 1)),
                  pl.BlockSpec((1, d), lambda i: (0, 0)), pl.BlockSpec((1, d), lambda i: (0, 1)), row, row, row],
        out_specs=[pl.BlockSpec((tr, 2 * d), lambda i: (i, OFF_GATES // (2 * d))), row, row,
                   pl.BlockSpec((1, 2 * d), lambda i: (0, 0))],
        out_shape=[jax.ShapeDtypeStruct((t, PROJ_W), bf16), jax.ShapeDtypeStruct((t, d), bf16),
                   jax.ShapeDtypeStruct((t, d), bf16), jax.ShapeDtypeStruct((1, 2 * d), f32)],
        compiler_params=pltpu.CompilerParams(dimension_semantics=("arbitrary",)),
    )(proj, proj, bg, bg, y_ssm, y_lru, dmix)


def swiglu_fwd(ff, *, name):
    t = ff.shape[0]
    hd = FFN_HIDDEN
    tr = _pick(t, 128, 8)

    def body(f_ref, o_ref):
        o_ref[...] = (_silu(f_ref[:, 0:hd]) * f_ref[:, hd:2 * hd]).astype(bf16)

    return pl.pallas_call(
        body, name=name, grid=(t // tr,), in_specs=[pl.BlockSpec((tr, 2 * hd), lambda i: (i, 0))],
        out_specs=pl.BlockSpec((tr, hd), lambda i: (i, 0)), out_shape=jax.ShapeDtypeStruct((t, hd), bf16),
    )(ff)


def swiglu_bwd(ff, dact, *, name):
    t = ff.shape[0]
    hd = FFN_HIDDEN
    tr = _pick(t, 128, 8)

    def body(f_ref, d_ref, o_ref):
        gate, up, dv = f_ref[:, 0:hd], f_ref[:, hd:2 * hd], d_ref[...]
        o_ref[:, 0:hd] = (dv * up * _dsilu(gate)).astype(bf16)
        o_ref[:, hd:2 * hd] = (dv * _silu(gate)).astype(bf16)

    return pl.pallas_call(
        body, name=name, grid=(t // tr,),
        in_specs=[pl.BlockSpec((tr, 2 * hd), lambda i: (i, 0)), pl.BlockSpec((tr, hd), lambda i: (i, 0))],
        out_specs=pl.BlockSpec((tr, 2 * hd), lambda i: (i, 0)), out_shape=jax.ShapeDtypeStruct((t, 2 * hd), bf16),
    )(ff, dact)


def _adam_math(w, g, m, v):
    m = ADAM_B1 * m + (1.0 - ADAM_B1) * g
    v = ADAM_B2 * v + (1.0 - ADAM_B2) * (g * g)
    m_hat = m / (1.0 - ADAM_B1 ** ADAM_STEP)
    v_hat = v / (1.0 - ADAM_B2 ** ADAM_STEP)
    delta = -ADAM_LR * (m_hat / (jnp.sqrt(v_hat) + ADAM_EPS) + ADAM_WD * w)
    return delta, m, v


def _row_tile(rows, cols):
    cap = max(8, (1 << 18) // cols)
    return _pick(rows, cap, 8) if rows % 8 == 0 else rows


def adamw(w, g, m, v, *, name):
    rows, cols = w.shape
    tr = _row_tile(rows, cols)

    def body(w_ref, g_ref, m_ref, v_ref, d_ref, nm_ref, nv_ref):
        d, nm, nv = _adam_math(w_ref[...], g_ref[...], m_ref[...], v_ref[...])
        d_ref[...] = d
        nm_ref[...] = nm
        nv_ref[...] = nv

    tile = pl.BlockSpec((tr, cols), lambda i: (i, 0))
    return pl.pallas_call(
        body, name=name, grid=(rows // tr,), in_specs=[tile] * 4, out_specs=[tile] * 3,
        out_shape=[jax.ShapeDtypeStruct((rows, cols), f32)] * 3,
    )(w, g, m, v)


def pair_add(dw, rbuf, idx, *, name):
    n, rows, cols = dw.shape
    hr = rows // 2
    tr = _row_tile(hr, cols)
    nrt = hr // tr

    def body(idx_ref, a_ref, b_ref, o_ref, own_ref):
        s = a_ref[...] + b_ref[...]
        o_ref[...] = s.astype(bf16)

        @pl.when(pl.program_id(1) == idx_ref[0])
        def _():
            own_ref[...] = s[0]

    return pl.pallas_call(
        body, name=name,
        grid_spec=pltpu.PrefetchScalarGridSpec(
            num_scalar_prefetch=1, grid=(nrt, n),
            in_specs=[pl.BlockSpec((1, tr, cols), lambda i, k, idx: (k, idx[1] * nrt + i, 0)),
                      pl.BlockSpec((1, tr, cols), lambda i, k, idx: (k, i, 0))],
            out_specs=[pl.BlockSpec((1, tr, cols), lambda i, k, idx: (k, i, 0)),
                       pl.BlockSpec((tr, cols), lambda i, k, idx: (i, 0))]),
        out_shape=[jax.ShapeDtypeStruct((n, hr, cols), bf16), jax.ShapeDtypeStruct((hr, cols), f32)],
    )(idx, dw, rbuf)


def chip_sum(own, rbuf, idx, *, name):
    hr, cols = own.shape
    tr = _row_tile(hr, cols)
    nrt = hr // tr

    def body(idx_ref, a_ref, b_ref, o_ref):
        o_ref[...] = ((a_ref[...] + b_ref[0].astype(f32)) + b_ref[1].astype(f32)) + b_ref[2].astype(f32)

    return pl.pallas_call(
        body, name=name,
        grid_spec=pltpu.PrefetchScalarGridSpec(
            num_scalar_prefetch=1, grid=(nrt,),
            in_specs=[pl.BlockSpec((tr, cols), lambda i, idx: (i, 0)),
                      pl.BlockSpec((3, tr, cols), lambda i, idx: (0, i, 0))],
            out_specs=pl.BlockSpec((tr, cols), lambda i, idx: (idx[1] * nrt + i, 0))),
        out_shape=jax.ShapeDtypeStruct((2 * hr, cols), f32),
    )(idx, own, rbuf)


def sum8(rbuf, *, name):
    n, rows, cols = rbuf.shape
    tr = _row_tile(rows, cols * n)

    def body(a_ref, o_ref):
        acc = a_ref[0]
        for k in range(1, n):
            acc = acc + a_ref[k]
        o_ref[...] = acc

    return pl.pallas_call(
        body, name=name, grid=(rows // tr,), in_specs=[pl.BlockSpec((n, tr, cols), lambda i: (0, i, 0))],
        out_specs=pl.BlockSpec((tr, cols), lambda i: (i, 0)), out_shape=jax.ShapeDtypeStruct((rows, cols), f32),
    )(rbuf)


def _coords():
    return lax.axis_index("x"), lax.axis_index("y"), lax.axis_index("c")


def _other_chips(x, y):
    return [(1 - x, y), (x, 1 - y), (1 - x, 1 - y)]


def gather_weights(shards, *, name):
    n = len(shards)
    halves = [s.shape[0] // 2 for s in shards]

    def body(*refs):
        ins, outs = refs[:n], refs[n:2 * n]
        send1, recv1, send2, recv2 = refs[2 * n:]
        x, y, c = _coords()
        me = 2 * x + y
        chips = _other_chips(x, y)
        sibling = (x, y, 1 - c)

        def half(i, k, hc):
            return outs[i].at[k, pl.ds(hc * halves[i], halves[i]), :]

        def ici(i, j):
            return pltpu.make_async_remote_copy(
                src_ref=ins[i].at[pl.ds(c * halves[i], halves[i]), :], dst_ref=half(i, me, c),
                send_sem=send1.at[i, j], recv_sem=recv1.at[i, j], device_id=(*chips[j], c), device_id_type=MESH)

        def landed(i, j):
            kj = 2 * chips[j][0] + chips[j][1]
            return pltpu.make_async_remote_copy(
                src_ref=half(i, kj, c), dst_ref=half(i, kj, c),
                send_sem=send2.at[i, j], recv_sem=recv1.at[i, j], device_id=sibling, device_id_type=MESH)

        def from_sibling(i, j):
            kj = 2 * chips[j][0] + chips[j][1]
            return pltpu.make_async_remote_copy(
                src_ref=half(i, kj, 1 - c), dst_ref=half(i, kj, 1 - c),
                send_sem=send2.at[i, j], recv_sem=recv2.at[i, j], device_id=sibling, device_id_type=MESH)

        def d2d(i, j):
            kj = 2 * chips[j][0] + chips[j][1]
            return pltpu.make_async_remote_copy(
                src_ref=half(i, kj, c), dst_ref=half(i, kj, c),
                send_sem=send2.at[i, j], recv_sem=recv2.at[i, j], device_id=sibling, device_id_type=MESH)

        for j in range(3):
            for i in range(n):
                ici(i, j).start()
        for j in range(3):
            for i in range(n):
                landed(i, j).wait_recv()
                d2d(i, j).start()
        for j in range(3):
            for i in range(n):
                from_sibling(i, j).wait_recv()
        for j in range(3):
            for i in range(n):
                ici(i, j).wait_send()
                d2d(i, j).wait_send()

    return pl.pallas_call(
        body, name=name, in_specs=[ANY] * n, out_specs=[ANY] * n,
        out_shape=[jax.ShapeDtypeStruct((N_CHIPS,) + s.shape, s.dtype) for s in shards],
        scratch_shapes=[pltpu.SemaphoreType.DMA((n, 3))] * 4,
    )(*shards)


def pair_exchange(grads, *, name):
    n = len(grads)
    halves = [g.shape[1] // 2 for g in grads]

    def body(*refs):
        ins, outs = refs[:n], refs[n:2 * n]
        send, recv = refs[2 * n:]
        x, y, c = _coords()
        cps = [pltpu.make_async_remote_copy(
            src_ref=ins[i].at[:, pl.ds((1 - c) * halves[i], halves[i]), :], dst_ref=outs[i],
            send_sem=send.at[i], recv_sem=recv.at[i], device_id=(x, y, 1 - c), device_id_type=MESH) for i in range(n)]
        for cp in cps:
            cp.start()
        for cp in cps:
            cp.wait()

    return pl.pallas_call(
        body, name=name, in_specs=[ANY] * n, out_specs=[ANY] * n,
        out_shape=[jax.ShapeDtypeStruct((N_CHIPS, g.shape[1] // 2, g.shape[2]), g.dtype) for g in grads],
        scratch_shapes=[pltpu.SemaphoreType.DMA((n,))] * 2,
    )(*grads)


def chip_exchange(psums, *, name):
    n = len(psums)

    def body(*refs):
        ins, outs = refs[:n], refs[n:2 * n]
        send, recv = refs[2 * n:]
        x, y, c = _coords()
        chips = _other_chips(x, y)
        cps = []
        for j in range(3):
            kj = 2 * chips[j][0] + chips[j][1]
            for i in range(n):
                cps.append(pltpu.make_async_remote_copy(
                    src_ref=ins[i].at[kj], dst_ref=outs[i].at[j], send_sem=send.at[i, j], recv_sem=recv.at[i, j],
                    device_id=(*chips[j], c), device_id_type=MESH))
        for cp in cps:
            cp.start()
        for cp in cps:
            cp.wait()

    return pl.pallas_call(
        body, name=name, in_specs=[ANY] * n, out_specs=[ANY] * n,
        out_shape=[jax.ShapeDtypeStruct((3,) + p.shape[1:], p.dtype) for p in psums],
        scratch_shapes=[pltpu.SemaphoreType.DMA((n, 3))] * 2,
    )(*psums)


def pair_gather(bufs, *, name):
    n = len(bufs)

    def body(*refs):
        ins, outs = refs[:n], refs[n:2 * n]
        send, recv = refs[2 * n:]
        x, y, c = _coords()
        cps = []
        for i in range(n):
            hr = ins[i].shape[0] // 2
            cps.append(pltpu.make_async_remote_copy(
                src_ref=ins[i].at[pl.ds(c * hr, hr), :], dst_ref=outs[i].at[pl.ds(c * hr, hr), :],
                send_sem=send.at[i], recv_sem=recv.at[i], device_id=(x, y, 1 - c), device_id_type=MESH))
        for cp in cps:
            cp.start()
        for i in range(n):
            hr = ins[i].shape[0] // 2
            pltpu.make_async_remote_copy(
                src_ref=ins[i].at[pl.ds((1 - c) * hr, hr), :], dst_ref=outs[i].at[pl.ds((1 - c) * hr, hr), :],
                send_sem=send.at[i], recv_sem=recv.at[i], device_id=(x, y, 1 - c), device_id_type=MESH).wait_recv()
        for cp in cps:
            cp.wait_send()

    return pl.pallas_call(
        body, name=name, in_specs=[ANY] * n, out_specs=[ANY] * n,
        out_shape=[jax.ShapeDtypeStruct(b.shape, b.dtype) for b in bufs],
        input_output_aliases={i: i for i in range(n)},
        scratch_shapes=[pltpu.SemaphoreType.DMA((n,))] * 2,
    )(*bufs)


def all_exchange(buf, *, name):
    rows, cols = buf.shape

    def body(in_ref, out_ref, send, recv):
        x, y, c = _coords()
        me = 4 * x + 2 * y + c
        cps = []
        for d in range(1, 8):
            px = 1 - x if d & 4 else x
            py = 1 - y if d & 2 else y
            pc = 1 - c if d & 1 else c
            cps.append(pltpu.make_async_remote_copy(
                src_ref=in_ref, dst_ref=out_ref.at[me], send_sem=send.at[d - 1], recv_sem=recv.at[d - 1],
                device_id=(px, py, pc), device_id_type=MESH))
        for cp in cps:
            cp.start()
        for d in range(1, 8):
            px = 1 - x if d & 4 else x
            py = 1 - y if d & 2 else y
            pc = 1 - c if d & 1 else c
            src = 4 * px + 2 * py + pc
            pltpu.make_async_remote_copy(
                src_ref=in_ref, dst_ref=out_ref.at[src], send_sem=send.at[d - 1], recv_sem=recv.at[d - 1],
                device_id=(px, py, pc), device_id_type=MESH).wait_recv()
        for cp in cps:
            cp.wait_send()

    return pl.pallas_call(
        body, name=name, in_specs=[ANY], out_specs=ANY,
        out_shape=jax.ShapeDtypeStruct((8, rows, cols), buf.dtype),
        scratch_shapes=[pltpu.SemaphoreType.DMA((7,)), pltpu.SemaphoreType.DMA((7,))],
    )(buf)


def _pack(arrs):
    flat = []
    for a in arrs:
        v = a.reshape(-1).astype(f32)
        pad = (-v.shape[0]) % 128
        flat.append(jnp.pad(v, (0, pad)) if pad else v)
    v = jnp.concatenate(flat)
    rows = v.shape[0] // 128
    pad_rows = (-rows) % 256
    v = v.reshape(rows, 128)
    return jnp.pad(v, ((0, pad_rows), (0, 0))) if pad_rows else v


def _unpack(buf, shapes):
    out, row = [], 0
    for s in shapes:
        size = math.prod(s)
        rows = -(-size // 128)
        out.append(buf[row:row + rows].reshape(-1)[:size].reshape(s))
        row += rows
    return out


def _perm_in_cols(w):
    gates, z = w[..., 0:2048], w[..., 2048:4096]
    xbc = w[..., 4096:7168]
    dt, lx, ly = w[..., 7168:7200], w[..., 7200:8480], w[..., 8480:9760]
    pad = jnp.zeros(w.shape[:-1] + (DT_PAD_W - SSM_HEADS,), w.dtype)
    return jnp.concatenate([gates, z, lx, ly, dt, pad, _perm_xbc_cols(xbc)], axis=-1)


def _perm_xbc_cols(w):
    parts = []
    for g in range(SSM_GROUPS):
        parts += [w[..., g * 512:(g + 1) * 512], w[..., 2048 + g * 128:2048 + (g + 1) * 128],
                  w[..., 2560 + g * 128:2560 + (g + 1) * 128]]
    return jnp.concatenate(parts, axis=-1)


def _unperm_xbc_cols(w):
    xs = [w[..., g * XBC_GROUP_W:g * XBC_GROUP_W + 512] for g in range(SSM_GROUPS)]
    bs = [w[..., g * XBC_GROUP_W + 512:g * XBC_GROUP_W + 640] for g in range(SSM_GROUPS)]
    cs = [w[..., g * XBC_GROUP_W + 640:(g + 1) * XBC_GROUP_W] for g in range(SSM_GROUPS)]
    return jnp.concatenate(xs + bs + cs, axis=-1)


def _unperm_in_cols(w):
    xbc = _unperm_xbc_cols(w[..., OFF_XBC:OFF_XBC + 3072])
    return jnp.concatenate([w[..., OFF_GATES:OFF_GATES + 2048], w[..., OFF_Z:OFF_Z + 2048], xbc,
                            w[..., OFF_DT:OFF_DT + 32], w[..., OFF_LX:OFF_LX + 1280], w[..., OFF_LY:OFF_LY + 1280]], axis=-1)


def _col_shards(w, n=N_CHIPS):
    r, c = w.shape
    return jnp.transpose(w.reshape(r, n, c // n), (1, 0, 2))


def _from_col_shards(w):
    n, r, c = w.shape
    return jnp.transpose(w, (1, 0, 2)).reshape(r, n * c)


def kernel(x, norm1_w, w_in, b_branch_gate, ssm_conv_w, ssm_conv_b, ssm_dt_bias, ssm_a_log, ssm_d, ssm_norm_w, w_out_ssm, lru_conv_w, lru_conv_b, lru_w_r, lru_b_r, lru_w_i, lru_b_i, lru_lambda, w_out_lru, w_out, norm2_w, w_ffn_in, w_ffn_out, norm_f_w, loss_target, m_norm1_w, m_w_in, m_b_branch_gate, m_ssm_conv_w, m_ssm_conv_b, m_ssm_dt_bias, m_ssm_a_log, m_ssm_d, m_ssm_norm_w, m_w_out_ssm, m_lru_conv_w, m_lru_conv_b, m_lru_w_r, m_lru_b_r, m_lru_w_i, m_lru_b_i, m_lru_lambda, m_w_out_lru, m_w_out, m_norm2_w, m_w_ffn_in, m_w_ffn_out, m_norm_f_w, v_norm1_w, v_w_in, v_b_branch_gate, v_ssm_conv_w, v_ssm_conv_b, v_ssm_dt_bias, v_ssm_a_log, v_ssm_d, v_ssm_norm_w, v_w_out_ssm, v_lru_conv_w, v_lru_conv_b, v_lru_w_r, v_lru_b_r, v_lru_w_i, v_lru_b_i, v_lru_lambda, v_w_out_lru, v_w_out, v_norm2_w, v_w_ffn_in, v_w_ffn_out, v_norm_f_w):
    xi, yi, ci = lax.axis_index("x"), lax.axis_index("y"), lax.axis_index("c")
    me = 2 * xi + yi
    idx = jnp.stack([me, ci]).astype(jnp.int32)
    x2 = x[0]
    tgt = loss_target[0]

    big_names = ["w_in", "w_out_ssm", "w_out_lru", "w_out", "w_ffn_in", "w_ffn_out"]
    big_w = dict(w_in=w_in[0], w_out_ssm=w_out_ssm[0], w_out_lru=w_out_lru[0], w_out=w_out[0], w_ffn_in=w_ffn_in[0],
                 w_ffn_out=w_ffn_out[0])
    big_m = dict(w_in=m_w_in[0], w_out_ssm=m_w_out_ssm[0], w_out_lru=m_w_out_lru[0], w_out=m_w_out[0],
                 w_ffn_in=m_w_ffn_in[0], w_ffn_out=m_w_ffn_out[0])
    big_v = dict(w_in=v_w_in[0], w_out_ssm=v_w_out_ssm[0], w_out_lru=v_w_out_lru[0], w_out=v_w_out[0],
                 w_ffn_in=v_w_ffn_in[0], w_ffn_out=v_w_ffn_out[0])
    conv_pad = jnp.zeros((16, 768), f32).at[0:4, :].set(ssm_conv_w[0]).at[8:12, 0:320].set(lru_conv_w[0])
    mine = [big_w[k].astype(bf16) for k in big_names] + [conv_pad]
    gathered = gather_weights(mine, name="gather_weights")
    gathered = [lax.dynamic_update_index_in_dim(g, s, me, 0) for g, s in zip(gathered, mine)]
    g_in, g_out_ssm, g_out_lru, g_out, g_ffn_in, g_ffn_out, g_conv = gathered
    w_in_p = _perm_in_cols(_from_col_shards(g_in))
    w_out_ssm_f = g_out_ssm.reshape(SSM_D_INNER, D_MODEL)
    w_out_lru_f = g_out_lru.reshape(LRU_WIDTH, D_MODEL)
    w_out_f = g_out.reshape(D_MODEL, D_MODEL)
    w_ffn_in_f = _from_col_shards(g_ffn_in)
    w_ffn_out_f = g_ffn_out.reshape(FFN_HIDDEN, D_MODEL)
    ssm_cw_full = _from_col_shards(g_conv[:, 0:4, :])
    lru_cw_full = _from_col_shards(g_conv[:, 8:12, 0:320])
    ssm_cw_p = _perm_xbc_cols(ssm_cw_full)
    ssm_cb_p = _perm_xbc_cols(ssm_conv_b)

    par = jnp.stack([ssm_dt_bias[0], ssm_a_log[0], ssm_d[0]], axis=0).reshape(3, SSM_GROUPS, SSM_HPG)
    par_row = jnp.zeros((SSM_GROUPS, 8, 8), f32).at[:, 0:3, :].set(jnp.transpose(par, (1, 0, 2)))
    par_col = jnp.transpose(par_row, (0, 2, 1))

    hn1 = rms_fwd(x2, norm1_w, name="rms1_fwd")
    proj = mm(hn1, w_in_p, "nn", name="in_proj")
    t = x2.shape[0]
    dtr = jnp.transpose(proj[:, OFF_DT:OFF_DT + 32].reshape(t, SSM_GROUPS, SSM_HPG), (1, 0, 2))
    dtr_t = jnp.transpose(dtr, (0, 2, 1))
    xbc_pre, xbc_post = conv_fwd(proj, OFF_XBC, SSM_CONV_DIM, ssm_cw_p, ssm_cb_p, silu=True, name="ssm_conv_fwd")
    y_ssd, s_in = ssd_fwd(xbc_post, dtr, dtr_t, par_row, par_col, name="ssd_fwd")
    yn = gnorm_fwd(y_ssd, proj, ssm_norm_w, name="gnorm_fwd")
    y_ssm = mm(yn, w_out_ssm_f, "nn", name="out_ssm")
    (u_lru,) = conv_fwd(proj, OFF_LX, LRU_WIDTH, lru_cw_full, lru_conv_b, silu=False, name="lru_conv_fwd")
    h_lru, o_lru = lru_fwd(u_lru, proj, lru_w_r[0], lru_b_r, lru_w_i[0], lru_b_i, lru_lambda, name="lru_fwd")
    y_lru = mm(o_lru, w_out_lru_f, "nn", name="out_lru")
    mix = merge_fwd(proj, b_branch_gate, y_ssm, y_lru, name="merge_fwd")
    h1 = mm(mix, w_out_f, "nn", add=x2, name="out_proj")
    hn2 = rms_fwd(h1, norm2_w, name="rms2_fwd")
    ff = mm(hn2, w_ffn_in_f, "nn", name="ffn_in")
    act = swiglu_fwd(ff, name="swiglu_fwd")
    h2 = mm(act, w_ffn_out_f, "nn", add=h1, name="ffn_out")
    loss_tile, dh2, d_norm_f = loss_head(h2, norm_f_w.reshape(1, D_MODEL), tgt, name="loss_head")
    loss = lax.psum(loss_tile[0, 0], ("x", "y", "c"))

    d_w_ffn_out = mm(act, dh2, "tn", name="d_w_ffn_out")
    dact = mm(dh2, w_ffn_out_f, "nt", name="d_act")
    dff = swiglu_bwd(ff, dact, name="swiglu_bwd")
    d_w_ffn_in = mm(hn2, dff, "tn", name="d_w_ffn_in")
    dhn2 = mm(dff, w_ffn_in_f, "nt", name="d_hn2")
    dh1, d_norm2 = rms_bwd(h1, norm2_w, dhn2, dh2, name="rms2_bwd")
    d_w_out = mm(mix, dh1, "tn", name="d_w_out")
    dmix = mm(dh1, w_out_f, "nt", name="d_mix")
    dproj, dy_ssm, dy_lru, d_bg = merge_bwd(proj, b_branch_gate, y_ssm, y_lru, dmix, name="merge_bwd")
    d_w_out_ssm = mm(yn, dy_ssm, "tn", name="d_w_out_ssm")
    dyn = mm(dy_ssm, w_out_ssm_f, "nt", name="d_yn")
    dy_ssd, dproj, d_ssm_norm = gnorm_bwd(y_ssd, proj, ssm_norm_w, dyn, dproj, name="gnorm_bwd")
    dxbc_post, ddtr, dpar = ssd_bwd(xbc_post, dtr, dtr_t, par_row, par_col, s_in, dy_ssd, name="ssd_bwd")
    dproj, d_ssm_cw_p, d_ssm_cb_p = conv_bwd(dxbc_post, xbc_pre, proj, OFF_XBC, ssm_cw_p, dproj, name="ssm_conv_bwd")
    d_w_out_lru = mm(o_lru, dy_lru, "tn", name="d_w_out_lru")
    do_lru = mm(dy_lru, w_out_lru_f, "nt", name="d_o_lru")
    du_lru, dproj, d_w_r, d_w_i, d_b_r, d_b_i, d_lam = lru_bwd(u_lru, proj, h_lru, do_lru, lru_w_r[0], lru_b_r, lru_w_i[0],
                                                               lru_b_i, lru_lambda, dproj, name="lru_bwd")
    dproj, d_lru_cw, d_lru_cb = conv_bwd(du_lru, None, proj, OFF_LX, lru_cw_full, dproj, name="lru_conv_bwd")
    ddt_cols = jnp.transpose(ddtr, (1, 0, 2)).reshape(t, SSM_HEADS).astype(bf16)
    ddt_cols = jnp.pad(ddt_cols, ((0, 0), (0, DT_PAD_W - SSM_HEADS)))
    dproj = lax.dynamic_update_slice(dproj, ddt_cols, (0, OFF_DT))
    d_w_in_p = mm(hn1, dproj, "tn", name="d_w_in")
    dhn1 = mm(dproj, w_in_p, "nt", name="d_hn1")
    grad_x, d_norm1 = rms_bwd(x2, norm1_w, dhn1, dh1, name="rms1_bwd")

    big_g = [_col_shards(_unperm_in_cols(d_w_in_p)), d_w_out_ssm.reshape(N_CHIPS, 512, D_MODEL),
             d_w_out_lru.reshape(N_CHIPS, 320, D_MODEL), d_w_out.reshape(N_CHIPS, 256, D_MODEL),
             _col_shards(d_w_ffn_in), d_w_ffn_out.reshape(N_CHIPS, 704, D_MODEL)]
    from_sibling = pair_exchange(big_g, name="pair_exchange")
    pair_sums = [pair_add(g, rb, idx, name="pair_add_" + k) for g, rb, k in zip(big_g, from_sibling, big_names)]
    from_chips = chip_exchange([p[0] for p in pair_sums], name="chip_exchange")
    reduced_half = [chip_sum(p[1], rb, idx, name="chip_sum_" + k) for p, rb, k in zip(pair_sums, from_chips, big_names)]
    reduced = pair_gather(reduced_half, name="pair_gather")
    big_out = {}
    for k, g in zip(big_names, reduced):
        big_out[k] = (g,) + tuple(adamw(big_w[k], g, big_m[k], big_v[k], name="adamw_" + k))

    d_ssm_cw = _unperm_xbc_cols(d_ssm_cw_p)
    d_ssm_cb = _unperm_xbc_cols(d_ssm_cb_p)
    dpar_h = jnp.transpose(dpar[:, 0:3, :], (1, 0, 2)).reshape(3, SSM_HEADS)
    small_names = ["norm1_w", "b_branch_gate", "ssm_conv_b", "ssm_dt_bias", "ssm_a_log", "ssm_d", "ssm_norm_w",
                   "lru_conv_b", "lru_w_r", "lru_b_r", "lru_w_i", "lru_b_i", "lru_lambda", "norm2_w", "norm_f_w"]
    small_g = dict(norm1_w=d_norm1, b_branch_gate=d_bg, ssm_conv_b=d_ssm_cb, ssm_dt_bias=dpar_h[0:1], ssm_a_log=dpar_h[1:2],
                   ssm_d=dpar_h[2:3], ssm_norm_w=d_ssm_norm, lru_conv_b=d_lru_cb, lru_w_r=d_w_r[None], lru_b_r=d_b_r,
                   lru_w_i=d_w_i[None], lru_b_i=d_b_i, lru_lambda=d_lam, norm2_w=d_norm2, norm_f_w=d_norm_f.reshape(D_MODEL))
    small_w = dict(norm1_w=norm1_w, b_branch_gate=b_branch_gate, ssm_conv_b=ssm_conv_b, ssm_dt_bias=ssm_dt_bias,
                   ssm_a_log=ssm_a_log, ssm_d=ssm_d, ssm_norm_w=ssm_norm_w, lru_conv_b=lru_conv_b, lru_w_r=lru_w_r,
                   lru_b_r=lru_b_r, lru_w_i=lru_w_i, lru_b_i=lru_b_i, lru_lambda=lru_lambda, norm2_w=norm2_w, norm_f_w=norm_f_w)
    small_m = dict(norm1_w=m_norm1_w, b_branch_gate=m_b_branch_gate, ssm_conv_b=m_ssm_conv_b, ssm_dt_bias=m_ssm_dt_bias,
                   ssm_a_log=m_ssm_a_log, ssm_d=m_ssm_d, ssm_norm_w=m_ssm_norm_w, lru_conv_b=m_lru_conv_b, lru_w_r=m_lru_w_r,
                   lru_b_r=m_lru_b_r, lru_w_i=m_lru_w_i, lru_b_i=m_lru_b_i, lru_lambda=m_lru_lambda, norm2_w=m_norm2_w,
                   norm_f_w=m_norm_f_w)
    small_v = dict(norm1_w=v_norm1_w, b_branch_gate=v_b_branch_gate, ssm_conv_b=v_ssm_conv_b, ssm_dt_bias=v_ssm_dt_bias,
                   ssm_a_log=v_ssm_a_log, ssm_d=v_ssm_d, ssm_norm_w=v_ssm_norm_w, lru_conv_b=v_lru_conv_b, lru_w_r=v_lru_w_r,
                   lru_b_r=v_lru_b_r, lru_w_i=v_lru_w_i, lru_b_i=v_lru_b_i, lru_lambda=v_lru_lambda, norm2_w=v_norm2_w,
                   norm_f_w=v_norm_f_w)
    shapes = [small_w[k].shape for k in small_names]
    conv_shapes = [(4, SSM_CONV_DIM), (4, LRU_WIDTH)]
    g_pack = _pack([small_g[k] for k in small_names] + [d_ssm_cw, d_lru_cw])
    g_all = lax.dynamic_update_index_in_dim(all_exchange(g_pack, name="all_exchange"), g_pack, 2 * me + ci, 0)
    g_sum = sum8(g_all, name="sum8")
    g_ssm_cw_full, g_lru_cw_full = _unpack(g_sum, shapes + conv_shapes)[len(shapes):]
    g_ssm_cw = lax.dynamic_slice_in_dim(g_ssm_cw_full, me * 768, 768, axis=1)
    g_lru_cw = lax.dynamic_slice_in_dim(g_lru_cw_full, me * 320, 320, axis=1)
    loc_shapes = [(4, 768), (4, 320)]
    g_loc = _pack(_unpack(g_sum, shapes)[:len(shapes)] + [g_ssm_cw, g_lru_cw])
    w_loc = _pack([small_w[k] for k in small_names] + [ssm_conv_w[0], lru_conv_w[0]])
    m_loc = _pack([small_m[k] for k in small_names] + [m_ssm_conv_w[0], m_lru_conv_w[0]])
    v_loc = _pack([small_v[k] for k in small_names] + [v_ssm_conv_w[0], v_lru_conv_w[0]])
    d_loc, nm_loc, nv_loc = adamw(w_loc, g_loc, m_loc, v_loc, name="adamw_small")
    small_out = {}
    unp = [_unpack(b, shapes + loc_shapes) for b in (g_loc, d_loc, nm_loc, nv_loc)]
    for i, k in enumerate(small_names + ["ssm_conv_w", "lru_conv_w"]):
        small_out[k] = tuple(u[i] for u in unp)

    order = ["norm1_w", "w_in", "b_branch_gate", "ssm_conv_w", "ssm_conv_b", "ssm_dt_bias", "ssm_a_log", "ssm_d", "ssm_norm_w",
             "w_out_ssm", "lru_conv_w", "lru_conv_b", "lru_w_r", "lru_b_r", "lru_w_i", "lru_b_i", "lru_lambda", "w_out_lru",
             "w_out", "norm2_w", "w_ffn_in", "w_ffn_out", "norm_f_w"]
    outs = [loss, grad_x[None]]
    for which in range(4):
        for k in order:
            if k in big_out:
                outs.append(big_out[k][which][None])
            elif k in ("ssm_conv_w", "lru_conv_w"):
                outs.append(small_out[k][which][None])
            else:
                outs.append(small_out[k][which])
    return tuple(outs)
```

```python
import functools
import math

import jax
import jax.numpy as jnp
from jax import lax
from jax.experimental import pallas as pl
from jax.experimental.pallas import tpu as pltpu

f32 = jnp.float32
bf16 = jnp.bfloat16

D_MODEL = 1024
SSM_D_INNER = 2048
SSM_HEADS = 32
SSM_HEAD_DIM = 64
SSM_GROUPS = 4
SSM_HPG = 8
SSM_D_STATE = 128
SSM_CHUNK = 128
SSM_GROUP_W = 512
SSM_CONV_DIM = 3072
XBC_GROUP_W = 768
LRU_WIDTH = 1280
LRU_BLOCKS = 10
LRU_BLOCK = 128
LRU_C = 8.0
FFN_HIDDEN = 2816
RMS_EPS = 1e-6
IN_PROJ_DIM = 9760
N_CHIPS = 4

OFF_GATES = 0
OFF_Z = 2048
OFF_LX = 4096
OFF_LY = 5376
OFF_DT = 6656
DT_PAD_W = 256
OFF_XBC = 6912
PROJ_W = 9984

ADAM_LR = 0.001
ADAM_B1 = 0.9
ADAM_B2 = 0.999
ADAM_EPS = 1e-08
ADAM_WD = 0.01
ADAM_STEP = 10

MESH = pl.DeviceIdType.MESH
ANY = pl.BlockSpec(memory_space=pl.ANY)

NN = (((1,), (0,)), ((), ()))
NT = (((1,), (1,)), ((), ()))
TN = (((0,), (0,)), ((), ()))


def _pick(n, cap, mult=128):
    best = None
    for t in range(mult, min(n, cap) + 1, mult):
        if n % t == 0:
            best = t
    return best if best is not None else n


def _sigmoid(x):
    return 1.0 / (1.0 + jnp.exp(-x))


def _softplus(x):
    return jnp.maximum(x, 0.0) + jnp.log(1.0 + jnp.exp(-jnp.abs(x)))


def _silu(x):
    return x * _sigmoid(x)


def _dsilu(x):
    s = _sigmoid(x)
    return s * (1.0 + x * (1.0 - s))


_GELU_K = math.sqrt(2.0 / math.pi)


def _gelu(x):
    return 0.5 * x * (1.0 + jnp.tanh(_GELU_K * (x + 0.044715 * x * x * x)))


def _dgelu(x):
    t = jnp.tanh(_GELU_K * (x + 0.044715 * x * x * x))
    return 0.5 * (1.0 + t) + 0.5 * x * (1.0 - t * t) * _GELU_K * (1.0 + 3.0 * 0.044715 * x * x)


def _expm1(x):
    poly = x * (1.0 + x * (0.5 + x * (1.0 / 6.0 + x * (1.0 / 24.0 + x * (1.0 / 120.0 + x * (1.0 / 720.0))))))
    return jnp.where(jnp.abs(x) < 0.1, poly, jnp.exp(x) - 1.0)


def _dot(a, b, dn):
    return lax.dot_general(a.astype(bf16), b.astype(bf16), dn, preferred_element_type=f32)


def _dot_01(a, b, dn, split, terms):
    r = a if split == 0 else b
    out = None
    for _ in range(terms):
        h = r.astype(bf16)
        r = r - h.astype(f32)
        d = lax.dot_general(h if split == 0 else a.astype(bf16), b.astype(bf16) if split == 0 else h, dn,
                            preferred_element_type=f32)
        out = d if out is None else out + d
    return out


def mm(a, b, mode, *, name, add=None, out_dtype=f32):
    if mode == "nn":
        (m, k), (k2, n) = a.shape, b.shape
    elif mode == "nt":
        (m, k), (n, k2) = a.shape, b.shape
    else:
        (k, m), (k2, n) = a.shape, b.shape
    assert k == k2, (a.shape, b.shape, mode)
    tm, tn, tk = _pick(m, 1024), _pick(n, 1024), _pick(k, 1024)
    nk = k // tk
    dn = {"nn": NN, "nt": NT, "tn": TN}[mode]
    a_spec = pl.BlockSpec((tk, tm), lambda i, j, kk: (kk, i)) if mode == "tn" else pl.BlockSpec((tm, tk), lambda i, j, kk: (i, kk))
    b_spec = pl.BlockSpec((tn, tk), lambda i, j, kk: (j, kk)) if mode == "nt" else pl.BlockSpec((tk, tn), lambda i, j, kk: (kk, j))
    o_spec = pl.BlockSpec((tm, tn), lambda i, j, kk: (i, j))
    has_add = add is not None

    def body(a_ref, b_ref, *rest):
        add_ref = rest[0] if has_add else None
        o_ref = rest[1] if has_add else rest[0]

        def finish(r):
            if has_add:
                r = r + add_ref[...]
            o_ref[...] = r.astype(out_dtype)

        if nk == 1:
            finish(_dot(a_ref[...], b_ref[...], dn))
            return
        acc = rest[-1]
        kk = pl.program_id(2)

        @pl.when(kk == 0)
        def _():
            acc[...] = jnp.zeros_like(acc)

        acc[...] += _dot(a_ref[...], b_ref[...], dn)

        @pl.when(kk == nk - 1)
        def _():
            finish(acc[...])

    ins = [a, b] + ([add] if has_add else [])
    in_specs = [a_spec, b_spec] + ([o_spec] if has_add else [])
    return pl.pallas_call(
        body, name=name, grid=(m // tm, n // tn, nk), in_specs=in_specs, out_specs=o_spec,
        out_shape=jax.ShapeDtypeStruct((m, n), out_dtype),
        scratch_shapes=[pltpu.VMEM((tm, tn), f32)] if nk > 1 else [],
        compiler_params=pltpu.CompilerParams(dimension_semantics=("parallel", "parallel", "arbitrary")),
    )(*ins)


def rms_fwd(x, w, *, name):
    t, d = x.shape
    tr = _pick(t, 256, 8)

    def body(x_ref, w_ref, o_ref):
        xv = x_ref[...]
        r = lax.rsqrt(jnp.mean(xv * xv, axis=-1, keepdims=True) + RMS_EPS)
        o_ref[...] = (xv * r * w_ref[...]).astype(bf16)

    return pl.pallas_call(
        body, name=name, grid=(t // tr,),
        in_specs=[pl.BlockSpec((tr, d), lambda i: (i, 0)), pl.BlockSpec((1, d), lambda i: (0, 0))],
        out_specs=pl.BlockSpec((tr, d), lambda i: (i, 0)), out_shape=jax.ShapeDtypeStruct((t, d), bf16),
    )(x, w)


def _rms_bwd_math(xv, wv, dy):
    r = lax.rsqrt(jnp.mean(xv * xv, axis=-1, keepdims=True) + RMS_EPS)
    g = dy * wv
    dx = r * g - xv * (r * r * r) * jnp.mean(g * xv, axis=-1, keepdims=True)
    dw = jnp.sum(dy * xv * r, axis=0, keepdims=True)
    return dx, dw


def rms_bwd(x, w, dy, res, *, name):
    t, d = x.shape
    tr = _pick(t, 256, 8)

    def body(x_ref, w_ref, dy_ref, res_ref, dx_ref, dw_ref):
        dx, dw = _rms_bwd_math(x_ref[...], w_ref[...], dy_ref[...])
        dx_ref[...] = dx + res_ref[...]

        @pl.when(pl.program_id(0) == 0)
        def _():
            dw_ref[...] = jnp.zeros_like(dw_ref)

        dw_ref[...] += dw

    row = pl.BlockSpec((tr, d), lambda i: (i, 0))
    vec = pl.BlockSpec((1, d), lambda i: (0, 0))
    return pl.pallas_call(
        body, name=name, grid=(t // tr,), in_specs=[row, vec, row, row], out_specs=[row, vec],
        out_shape=[jax.ShapeDtypeStruct((t, d), f32), jax.ShapeDtypeStruct((1, d), f32)],
        compiler_params=pltpu.CompilerParams(dimension_semantics=("arbitrary",)),
    )(x, w, dy, res)


def loss_head(h, w, target, *, name):
    t, d = h.shape
    tr = _pick(t, 256, 8)

    def body(h_ref, w_ref, t_ref, loss_ref, dh_ref, dw_ref):
        xv, wv = h_ref[...], w_ref[...]
        r = lax.rsqrt(jnp.mean(xv * xv, axis=-1, keepdims=True) + RMS_EPS)
        err = xv * r * wv - t_ref[...]
        part = 0.5 * jnp.sum(jnp.mean(err * err, axis=-1, keepdims=True), axis=0, keepdims=True)
        dx, dw = _rms_bwd_math(xv, wv, err * (1.0 / d))
        dh_ref[...] = dx

        @pl.when(pl.program_id(0) == 0)
        def _():
            dw_ref[...] = jnp.zeros_like(dw_ref)
            loss_ref[...] = jnp.zeros_like(loss_ref)

        dw_ref[...] += dw
        loss_ref[...] += part

    row = pl.BlockSpec((tr, d), lambda i: (i, 0))
    vec = pl.BlockSpec((1, d), lambda i: (0, 0))
    return pl.pallas_call(
        body, name=name, grid=(t // tr,), in_specs=[row, vec, row],
        out_specs=[pl.BlockSpec((8, 128), lambda i: (0, 0)), row, vec],
        out_shape=[jax.ShapeDtypeStruct((8, 128), f32), jax.ShapeDtypeStruct((t, d), f32), jax.ShapeDtypeStruct((1, d), f32)],
        compiler_params=pltpu.CompilerParams(dimension_semantics=("arbitrary",)),
    )(h, w, target)


CONV_ROWS = 256


def conv_fwd(src, col0, width, w, b, *, silu, name):
    t = src.shape[0]
    tc = _pick(math.gcd(width, col0), 768)
    assert col0 % tc == 0
    cb = col0 // tc
    r = CONV_ROWS

    def body(u_ref, w_ref, b_ref, *rest):
        ext = rest[-1]
        j = pl.program_id(1)

        @pl.when(j == 0)
        def _():
            ext[0:8, :] = jnp.zeros((8, tc), f32)

        @pl.when(j > 0)
        def _():
            ext[0:8, :] = ext[r:r + 8, :]

        ext[8:r + 8, :] = u_ref[...]
        v = ext[...]
        wv = w_ref[...]
        acc = b_ref[...] + wv[3:4, :] * v
        for s in (1, 2, 3):
            acc = acc + wv[3 - s:4 - s, :] * pltpu.roll(v, s, 0)
        pre = acc[8:, :]
        rest[0][...] = pre
        if silu:
            rest[1][...] = _silu(pre)

    tile = pl.BlockSpec((r, tc), lambda c, j: (j, c))
    n_out = 2 if silu else 1
    return pl.pallas_call(
        body, name=name, grid=(width // tc, t // r),
        in_specs=[pl.BlockSpec((r, tc), lambda c, j: (j, cb + c)), pl.BlockSpec((4, tc), lambda c, j: (0, c)),
                  pl.BlockSpec((1, tc), lambda c, j: (0, c))],
        out_specs=[tile] * n_out, out_shape=[jax.ShapeDtypeStruct((t, width), f32)] * n_out,
        scratch_shapes=[pltpu.VMEM((r + 8, tc), f32)],
        compiler_params=pltpu.CompilerParams(dimension_semantics=("parallel", "arbitrary")),
    )(src, w, b)


def conv_bwd(dpost, pre, src, col0, w, dst, *, name):
    t, width = dpost.shape
    tc = _pick(math.gcd(width, col0), 768)
    assert col0 % tc == 0
    cb = col0 // tc
    r = CONV_ROWS
    nt = t // r
    has_pre = pre is not None

    def body(*refs):
        refs = refs[1:]
        if has_pre:
            d_ref, p_ref, u_ref, w_ref, du_ref, dw_ref, db_ref, ext = refs
        else:
            d_ref, u_ref, w_ref, du_ref, dw_ref, db_ref, ext = refs
        j = pl.program_id(1)

        @pl.when(j == 0)
        def _():
            ext[r:r + 8, :] = jnp.zeros((8, tc), f32)
            dw_ref[...] = jnp.zeros_like(dw_ref)
            db_ref[...] = jnp.zeros_like(db_ref)

        @pl.when(j > 0)
        def _():
            ext[r:r + 8, :] = ext[0:8, :]

        dpre = d_ref[...]
        if has_pre:
            dpre = dpre * _dsilu(p_ref[...])
        ext[0:r, :] = dpre
        v = ext[...]
        wv = w_ref[...]
        uv = u_ref[...]
        du = wv[3:4, :] * dpre
        dw_ref[3:4, :] += jnp.sum(dpre * uv, axis=0, keepdims=True)
        for s in (1, 2, 3):
            sh = pltpu.roll(v, r + 8 - s, 0)[0:r, :]
            du = du + wv[3 - s:4 - s, :] * sh
            dw_ref[3 - s:4 - s, :] += jnp.sum(sh * uv, axis=0, keepdims=True)
        db_ref[...] += jnp.sum(dpre, axis=0, keepdims=True)
        du_ref[...] = du.astype(bf16)

    rev = pl.BlockSpec((r, tc), lambda c, j: (nt - 1 - j, c))
    win = pl.BlockSpec((r, tc), lambda c, j: (nt - 1 - j, cb + c))
    in_specs = [ANY, rev] + ([rev] if has_pre else []) + [win, pl.BlockSpec((4, tc), lambda c, j: (0, c))]
    ins = [dst, dpost] + ([pre] if has_pre else []) + [src, w]
    return pl.pallas_call(
        body, name=name, grid=(width // tc, nt), in_specs=in_specs,
        out_specs=[win, pl.BlockSpec((4, tc), lambda c, j: (0, c)), pl.BlockSpec((1, tc), lambda c, j: (0, c))],
        out_shape=[jax.ShapeDtypeStruct(dst.shape, bf16), jax.ShapeDtypeStruct((4, width), f32),
                   jax.ShapeDtypeStruct((1, width), f32)],
        input_output_aliases={0: 0},
        scratch_shapes=[pltpu.VMEM((r + 8, tc), f32)],
        compiler_params=pltpu.CompilerParams(dimension_semantics=("parallel", "arbitrary")),
    )(*ins)


def _ssd_common(xbc_ref, dtr_ref, dtrT_ref, par_row_ref, par_col_ref):
    l = SSM_CHUNK
    x = xbc_ref[:, 0:SSM_GROUP_W]
    bm = xbc_ref[:, SSM_GROUP_W:SSM_GROUP_W + SSM_D_STATE]
    cm = xbc_ref[:, SSM_GROUP_W + SSM_D_STATE:XBC_GROUP_W]
    par_row = par_row_ref[0]
    par_col = par_col_ref[0]
    bias_row, alog_row, d_row = par_row[0:1, :], par_row[1:2, :], par_row[2:3, :]
    bias_col, alog_col = par_col[:, 0:1], par_col[:, 1:2]
    dtr = dtr_ref[0]
    dt = _softplus(dtr + bias_row)
    dt_t = _softplus(dtrT_ref[0] + bias_col)
    a_row = -jnp.exp(alog_row)
    a_col = -jnp.exp(alog_col)
    li = lax.broadcasted_iota(jnp.int32, (l, l), 0)
    si = lax.broadcasted_iota(jnp.int32, (l, l), 1)
    tri = (li >= si).astype(f32)
    cs = _dot_01(tri, dt * a_row, NN, 1, 3)
    cs_t = _dot_01(dt_t * a_col, tri, NT, 0, 3)
    off = lax.broadcasted_iota(jnp.int32, (SSM_HPG, SSM_GROUP_W), 1) - SSM_HEAD_DIM * lax.broadcasted_iota(
        jnp.int32, (SSM_HPG, SSM_GROUP_W), 0)
    ex = ((off >= 0) & (off < SSM_HEAD_DIM)).astype(f32)
    cs_x = _dot_01(cs, ex, NN, 0, 3)
    cl_x = cs_x[l - 1:l, :]
    return dict(x=x, bm=bm, cm=cm, dtr=dtr, dt=dt, a_row=a_row, bias_row=bias_row, tri=tri, li=li, si=si, cs=cs,
                cs_t=cs_t, ex=ex, dt_x=_dot_01(dt, ex, NN, 0, 2), d_x=_dot_01(par_row, ex, NN, 0, 2)[2:3, :], e_x=jnp.exp(cs_x),
                el_x=jnp.exp(cl_x), dec_x=jnp.exp(cl_x - cs_x))


def ssd_fwd(xbc, dtr, dtr_t, par_row, par_col, *, name):
    t = xbc.shape[0]
    nc = t // SSM_CHUNK
    l, p = SSM_CHUNK, SSM_HEAD_DIM

    def body(xbc_ref, dtr_ref, dtrT_ref, prow_ref, pcol_ref, y_ref, sin_ref, state):
        @pl.when(pl.program_id(1) == 0)
        def _():
            state[...] = jnp.zeros_like(state)

        q = _ssd_common(xbc_ref, dtr_ref, dtrT_ref, prow_ref, pcol_ref)
        st = state[...]
        sin_ref[0] = st
        xd = q["x"] * q["dt_x"]
        g = _dot(q["cm"], q["bm"], NT)
        for r in range(SSM_HPG):
            sl = slice(r * p, (r + 1) * p)
            diff = q["cs"][:, r:r + 1] - q["cs_t"][r:r + 1, :]
            lm = jnp.where(q["li"] >= q["si"], jnp.exp(jnp.minimum(diff, 0.0)), 0.0)
            y_ref[:, sl] = _dot(g * lm, xd[:, sl], NN)
        y_ref[...] += q["e_x"] * _dot(q["cm"], st, NN) + q["d_x"] * q["x"]
        state[...] = q["el_x"] * st + _dot(q["bm"].T, xd * q["dec_x"], NN)

    return pl.pallas_call(
        body, name=name, grid=(SSM_GROUPS, nc),
        in_specs=[pl.BlockSpec((l, XBC_GROUP_W), lambda g, c: (c, g)),
                  pl.BlockSpec((1, l, SSM_HPG), lambda g, c: (g, c, 0)),
                  pl.BlockSpec((1, SSM_HPG, l), lambda g, c: (g, 0, c)),
                  pl.BlockSpec((1, 8, 8), lambda g, c: (g, 0, 0)),
                  pl.BlockSpec((1, 8, 8), lambda g, c: (g, 0, 0))],
        out_specs=[pl.BlockSpec((l, SSM_GROUP_W), lambda g, c: (c, g)),
                   pl.BlockSpec((1, SSM_D_STATE, SSM_GROUP_W), lambda g, c: (c, 0, g))],
        out_shape=[jax.ShapeDtypeStruct((t, SSM_D_INNER), f32),
                   jax.ShapeDtypeStruct((nc, SSM_D_STATE, SSM_D_INNER), f32)],
        scratch_shapes=[pltpu.VMEM((SSM_D_STATE, SSM_GROUP_W), f32)],
        compiler_params=pltpu.CompilerParams(dimension_semantics=("parallel", "arbitrary")),
    )(xbc, dtr, dtr_t, par_row, par_col)


def ssd_bwd(xbc, dtr, dtr_t, par_row, par_col, s_in, dy, *, name):
    t = xbc.shape[0]
    nc = t // SSM_CHUNK
    l, p = SSM_CHUNK, SSM_HEAD_DIM

    def body(xbc_ref, dtr_ref, dtrT_ref, prow_ref, pcol_ref, sin_ref, dy_ref, dxbc_ref, ddtr_ref, dpar_ref,
             dstate, yd_buf, dxd_buf):
        @pl.when(pl.program_id(1) == 0)
        def _():
            dstate[...] = jnp.zeros_like(dstate)
            dpar_ref[...] = jnp.zeros_like(dpar_ref)

        q = _ssd_common(xbc_ref, dtr_ref, dtrT_ref, prow_ref, pcol_ref)
        x, bm, cm, ex, li, si = q["x"], q["bm"], q["cm"], q["ex"], q["li"], q["si"]
        e_x, el_x, dec_x = q["e_x"], q["el_x"], q["dec_x"]
        st = sin_ref[0]
        dst = dstate[...]
        dy = dy_ref[...]
        xd = x * q["dt_x"]
        g = _dot(cm, bm, NT)
        dg = jnp.zeros((l, l), f32)
        for r in range(SSM_HPG):
            sl = slice(r * p, (r + 1) * p)
            diff = q["cs"][:, r:r + 1] - q["cs_t"][r:r + 1, :]
            lm = jnp.where(li >= si, jnp.exp(jnp.minimum(diff, 0.0)), 0.0)
            m = (g * lm).astype(bf16)
            xdh, dyh = xd[:, sl].astype(bf16), dy[:, sl].astype(bf16)
            yd_buf[:, sl] = _dot(m, xdh, NN)
            dxd_buf[:, sl] = _dot(m, dyh, TN)
            dg = dg + _dot(dyh, xdh, NT) * lm
        yd, dxd_diag = yd_buf[...], dxd_buf[...]
        yo = e_x * _dot(cm, st, NN)
        dz = e_x * dy
        wv = _dot(bm, dst, NN)
        xw = xd * wv * dec_x
        row8 = lax.broadcasted_iota(jnp.int32, (l, SSM_HPG), 0)
        dy_b, xd_b = dy.astype(bf16).astype(f32), xd.astype(bf16).astype(f32)
        dcs = _dot_01(dy_b * yd - xd_b * dxd_diag + dy * yo - xw, ex, NT, 0, 3)
        tail = jnp.sum(xw, axis=0, keepdims=True) + el_x * jnp.sum(dst * st, axis=0, keepdims=True)
        dcl = _dot_01(jnp.broadcast_to(tail, (SSM_HPG, SSM_GROUP_W)), ex, NT, 0, 3)[0:1, :]
        dcs = dcs + jnp.where(row8 == l - 1, dcl, 0.0)
        dda = _dot_01(q["tri"], dcs, TN, 1, 3)
        dxd = dxd_diag + dec_x * wv
        ddt = _dot_01(dxd * x, ex, NT, 0, 3) + dda * q["a_row"]
        ddtr = ddt * _sigmoid(q["dtr"] + q["bias_row"])
        ddtr_ref[0] = ddtr
        dd = _dot_01(jnp.broadcast_to(jnp.sum(dy * x, axis=0, keepdims=True), (SSM_HPG, SSM_GROUP_W)), ex, NT, 0, 2)[0:1, :]
        dpar_ref[0, 0:1, :] += jnp.sum(ddtr, axis=0, keepdims=True)
        dpar_ref[0, 1:2, :] += jnp.sum(dda * q["dt"], axis=0, keepdims=True) * q["a_row"]
        dpar_ref[0, 2:3, :] += dd
        dxbc_ref[:, 0:SSM_GROUP_W] = dxd * q["dt_x"] + q["d_x"] * dy
        dxbc_ref[:, SSM_GROUP_W:SSM_GROUP_W + SSM_D_STATE] = _dot(dg, cm, TN) + _dot(xd * dec_x, dst, NT)
        dxbc_ref[:, SSM_GROUP_W + SSM_D_STATE:XBC_GROUP_W] = _dot(dg, bm, NN) + _dot(dz, st, NT)
        dstate[...] = _dot(cm.T, dz, NN) + el_x * dst

    rc = lambda c: nc - 1 - c
    return pl.pallas_call(
        body, name=name, grid=(SSM_GROUPS, nc),
        in_specs=[pl.BlockSpec((l, XBC_GROUP_W), lambda g, c: (rc(c), g)),
                  pl.BlockSpec((1, l, SSM_HPG), lambda g, c: (g, rc(c), 0)),
                  pl.BlockSpec((1, SSM_HPG, l), lambda g, c: (g, 0, rc(c))),
                  pl.BlockSpec((1, 8, 8), lambda g, c: (g, 0, 0)),
                  pl.BlockSpec((1, 8, 8), lambda g, c: (g, 0, 0)),
                  pl.BlockSpec((1, SSM_D_STATE, SSM_GROUP_W), lambda g, c: (rc(c), 0, g)),
                  pl.BlockSpec((l, SSM_GROUP_W), lambda g, c: (rc(c), g))],
        out_specs=[pl.BlockSpec((l, XBC_GROUP_W), lambda g, c: (rc(c), g)),
                   pl.BlockSpec((1, l, SSM_HPG), lambda g, c: (g, rc(c), 0)),
                   pl.BlockSpec((1, 8, 8), lambda g, c: (g, 0, 0))],
        out_shape=[jax.ShapeDtypeStruct((t, SSM_CONV_DIM), f32),
                   jax.ShapeDtypeStruct((SSM_GROUPS, t, SSM_HPG), f32),
                   jax.ShapeDtypeStruct((SSM_GROUPS, 8, 8), f32)],
        scratch_shapes=[pltpu.VMEM((SSM_D_STATE, SSM_GROUP_W), f32), pltpu.VMEM((l, SSM_GROUP_W), f32),
                        pltpu.VMEM((l, SSM_GROUP_W), f32)],
        compiler_params=pltpu.CompilerParams(dimension_semantics=("parallel", "arbitrary")),
    )(xbc, dtr, dtr_t, par_row, par_col, s_in, dy)


def gnorm_fwd(y, proj, w, *, name):
    t = y.shape[0]
    tr = _pick(t, 512, 8)
    gw = SSM_GROUP_W
    zb = OFF_Z // gw

    def body(y_ref, z_ref, w_ref, o_ref):
        y2 = y_ref[...] * _silu(z_ref[...])
        r = lax.rsqrt(jnp.mean(y2 * y2, axis=-1, keepdims=True) + RMS_EPS)
        o_ref[...] = (y2 * r * w_ref[...]).astype(bf16)

    return pl.pallas_call(
        body, name=name, grid=(SSM_GROUPS, t // tr),
        in_specs=[pl.BlockSpec((tr, gw), lambda g, i: (i, g)), pl.BlockSpec((tr, gw), lambda g, i: (i, zb + g)),
                  pl.BlockSpec((1, gw), lambda g, i: (0, g))],
        out_specs=pl.BlockSpec((tr, gw), lambda g, i: (i, g)), out_shape=jax.ShapeDtypeStruct((t, SSM_D_INNER), bf16),
    )(y, proj, w)


def gnorm_bwd(y, proj, w, dout, dst, *, name):
    t = y.shape[0]
    tr = _pick(t, 512, 8)
    gw = SSM_GROUP_W
    zb = OFF_Z // gw

    def body(_, y_ref, z_ref, w_ref, do_ref, dy_ref, dz_ref, dw_ref):
        yv, zv = y_ref[...], z_ref[...]
        sz = _silu(zv)
        y2 = yv * sz
        dy2, dw = _rms_bwd_math(y2, w_ref[...], do_ref[...])
        dy_ref[...] = dy2 * sz
        dz_ref[...] = (dy2 * yv * _dsilu(zv)).astype(bf16)

        @pl.when(pl.program_id(1) == 0)
        def _():
            dw_ref[...] = jnp.zeros_like(dw_ref)

        dw_ref[...] += dw

    tile = pl.BlockSpec((tr, gw), lambda g, i: (i, g))
    vec = pl.BlockSpec((1, gw), lambda g, i: (0, g))
    return pl.pallas_call(
        body, name=name, grid=(SSM_GROUPS, t // tr),
        in_specs=[ANY, tile, pl.BlockSpec((tr, gw), lambda g, i: (i, zb + g)), vec, tile],
        out_specs=[tile, pl.BlockSpec((tr, gw), lambda g, i: (i, zb + g)), vec],
        out_shape=[jax.ShapeDtypeStruct((t, SSM_D_INNER), f32), jax.ShapeDtypeStruct(dst.shape, bf16),
                   jax.ShapeDtypeStruct((1, SSM_D_INNER), f32)],
        input_output_aliases={0: 1},
        compiler_params=pltpu.CompilerParams(dimension_semantics=("parallel", "arbitrary")),
    )(dst, y, proj, w, dout)


LRU_ROWS = 256


def _lru_gates(uv, wr_ref, wi_ref, br_ref, bi_ref, lam_ref):
    rg = _sigmoid(_dot(uv, wr_ref[0], NN) + br_ref[...])
    ig = _sigmoid(_dot(uv, wi_ref[0], NN) + bi_ref[...])
    sp = _softplus(-lam_ref[...])
    la = -LRU_C * rg * sp
    a = jnp.exp(la)
    s = jnp.sqrt(jnp.maximum(-_expm1(2.0 * la), 0.0))
    return rg, ig, sp, la, a, s


def lru_fwd(u, proj, w_r, b_r, w_i, b_i, lam, *, name):
    t = u.shape[0]
    r = LRU_ROWS
    lb = LRU_BLOCK
    yb = OFF_LY // lb

    def body(u_ref, y_ref, wr_ref, br_ref, wi_ref, bi_ref, lam_ref, h_ref, o_ref, carry):
        @pl.when(pl.program_id(1) == 0)
        def _():
            carry[...] = jnp.zeros_like(carry)

        uv = u_ref[...]
        _, ig, _, _, a, s = _lru_gates(uv, wr_ref, wi_ref, br_ref, bi_ref, lam_ref)
        b = s * ig * uv
        row = lax.broadcasted_iota(jnp.int32, (r, lb), 0)
        d = 1
        while d < r:
            keep = row >= d
            b = b + a * jnp.where(keep, pltpu.roll(b, d, 0), 0.0)
            a = a * jnp.where(keep, pltpu.roll(a, d, 0), 1.0)
            d *= 2
        h = b + a * carry[0:1, :]
        carry[0:1, :] = h[r - 1:r, :]
        h_ref[...] = h
        o_ref[...] = (h * _gelu(y_ref[...])).astype(bf16)

    tile = pl.BlockSpec((r, lb), lambda hb, j: (j, hb))
    vec = pl.BlockSpec((1, lb), lambda hb, j: (0, hb))
    wsp = pl.BlockSpec((1, lb, lb), lambda hb, j: (hb, 0, 0))
    return pl.pallas_call(
        body, name=name, grid=(LRU_BLOCKS, t // r),
        in_specs=[tile, pl.BlockSpec((r, lb), lambda hb, j: (j, yb + hb)), wsp, vec, wsp, vec, vec],
        out_specs=[tile, tile],
        out_shape=[jax.ShapeDtypeStruct((t, LRU_WIDTH), f32), jax.ShapeDtypeStruct((t, LRU_WIDTH), bf16)],
        scratch_shapes=[pltpu.VMEM((8, lb), f32)],
        compiler_params=pltpu.CompilerParams(dimension_semantics=("parallel", "arbitrary")),
    )(u, proj, w_r, b_r, w_i, b_i, lam)


def lru_bwd(u, proj, hseq, dout, w_r, b_r, w_i, b_i, lam, dst, *, name):
    t = u.shape[0]
    r = LRU_ROWS
    nt = t // r
    lb = LRU_BLOCK
    yb = OFF_LY // lb

    def body(_, u_ref, y_ref, h_ref, hp_ref, do_ref, wr_ref, br_ref, wi_ref, bi_ref, lam_ref,
             du_ref, dy_ref, dwr_ref, dwi_ref, dbr_ref, dbi_ref, dlam_ref, carry_dh, carry_a):
        j = pl.program_id(1)

        @pl.when(j == 0)
        def _():
            carry_dh[...] = jnp.zeros_like(carry_dh)
            carry_a[...] = jnp.zeros_like(carry_a)
            dwr_ref[...] = jnp.zeros_like(dwr_ref)
            dwi_ref[...] = jnp.zeros_like(dwi_ref)
            dbr_ref[...] = jnp.zeros_like(dbr_ref)
            dbi_ref[...] = jnp.zeros_like(dbi_ref)
            dlam_ref[...] = jnp.zeros_like(dlam_ref)

        uv = u_ref[...]
        yv = y_ref[...]
        hv = h_ref[...]
        dov = do_ref[...]
        rg, ig, sp, la, a, s = _lru_gates(uv, wr_ref, wi_ref, br_ref, bi_ref, lam_ref)
        dy_ref[...] = (dov * hv * _dgelu(yv)).astype(bf16)
        gq = dov * _gelu(yv)
        row = lax.broadcasted_iota(jnp.int32, (r, lb), 0)
        an = jnp.where(row < r - 1, pltpu.roll(a, r - 1, 0), carry_a[0:1, :])
        d = 1
        while d < r:
            keep = row < r - d
            gq = gq + an * jnp.where(keep, pltpu.roll(gq, r - d, 0), 0.0)
            an = an * jnp.where(keep, pltpu.roll(an, r - d, 0), 1.0)
            d *= 2
        dh = gq + an * carry_dh[0:1, :]
        carry_dh[0:1, :] = dh[0:1, :]
        carry_a[0:1, :] = a[0:1, :]
        first = jnp.where(j == nt - 1, 0.0, 1.0) * hp_ref[7:8, :]
        hprev = jnp.where(row >= 1, pltpu.roll(hv, 1, 0), first)
        da = dh * hprev
        iu = ig * uv
        e2 = jnp.exp(2.0 * la)
        dla = da * a - dh * iu * e2 / jnp.maximum(s, 1e-30)
        drp = dla * (-LRU_C * sp) * rg * (1.0 - rg)
        dip = dh * s * uv * ig * (1.0 - ig)
        dlam_ref[...] += jnp.sum(dla * (LRU_C * rg) * _sigmoid(-lam_ref[...]), axis=0, keepdims=True)
        du_ref[...] = dh * s * ig + _dot(drp, wr_ref[0], NT) + _dot(dip, wi_ref[0], NT)
        dwr_ref[0] += _dot(uv, drp, TN)
        dwi_ref[0] += _dot(uv, dip, TN)
        dbr_ref[...] += jnp.sum(drp, axis=0, keepdims=True)
        dbi_ref[...] += jnp.sum(dip, axis=0, keepdims=True)

    rj = lambda j: nt - 1 - j
    tile = pl.BlockSpec((r, lb), lambda hb, j: (rj(j), hb))
    vec = pl.BlockSpec((1, lb), lambda hb, j: (0, hb))
    wsp = pl.BlockSpec((1, lb, lb), lambda hb, j: (hb, 0, 0))
    hprev_spec = pl.BlockSpec((8, lb), lambda hb, j: (jnp.maximum(rj(j) * (r // 8) - 1, 0), hb))
    ywin = pl.BlockSpec((r, lb), lambda hb, j: (rj(j), yb + hb))
    return pl.pallas_call(
        body, name=name, grid=(LRU_BLOCKS, nt),
        in_specs=[ANY, tile, ywin, tile, hprev_spec, tile, wsp, vec, wsp, vec, vec],
        out_specs=[tile, ywin, wsp, wsp, vec, vec, vec],
        out_shape=[jax.ShapeDtypeStruct((t, LRU_WIDTH), f32), jax.ShapeDtypeStruct(dst.shape, bf16),
                   jax.ShapeDtypeStruct((LRU_BLOCKS, lb, lb), f32), jax.ShapeDtypeStruct((LRU_BLOCKS, lb, lb), f32),
                   jax.ShapeDtypeStruct((1, LRU_WIDTH), f32), jax.ShapeDtypeStruct((1, LRU_WIDTH), f32),
                   jax.ShapeDtypeStruct((1, LRU_WIDTH), f32)],
        input_output_aliases={0: 1},
        scratch_shapes=[pltpu.VMEM((8, lb), f32), pltpu.VMEM((8, lb), f32)],
        compiler_params=pltpu.CompilerParams(dimension_semantics=("parallel", "arbitrary")),
    )(dst, u, proj, hseq, hseq, dout, w_r, b_r, w_i, b_i, lam)


def merge_fwd(proj, bg, y_ssm, y_lru, *, name):
    t, d = y_ssm.shape
    tr = _pick(t, 256, 8)
    gb = OFF_GATES // d

    def body(gs_ref, gl_ref, bs_ref, bl_ref, ys_ref, yl_ref, o_ref):
        gs = _sigmoid(gs_ref[...] + bs_ref[...])
        gl = _sigmoid(gl_ref[...] + bl_ref[...])
        o_ref[...] = (gs * ys_ref[...] + gl * yl_ref[...]).astype(bf16)

    row = pl.BlockSpec((tr, d), lambda i: (i, 0))
    return pl.pallas_call(
        body, name=name, grid=(t // tr,),
        in_specs=[pl.BlockSpec((tr, d), lambda i: (i, gb)), pl.BlockSpec((tr, d), lambda i: (i, gb + 1)),
                  pl.BlockSpec((1, d), lambda i: (0, 0)), pl.BlockSpec((1, d), lambda i: (0, 1)), row, row],
        out_specs=row, out_shape=jax.ShapeDtypeStruct((t, d), bf16),
    )(proj, proj, bg, bg, y_ssm, y_lru)


def merge_bwd(proj, bg, y_ssm, y_lru, dmix, *, name):
    t, d = y_ssm.shape
    tr = _pick(t, 256, 8)
    gb = OFF_GATES // d

    def body(gs_ref, gl_ref, bs_ref, bl_ref, ys_ref, yl_ref, dm_ref, dg_ref, dys_ref, dyl_ref, dbg_ref):
        gs = _sigmoid(gs_ref[...] + bs_ref[...])
        gl = _sigmoid(gl_ref[...] + bl_ref[...])
        dm = dm_ref[...]
        dys_ref[...] = (dm * gs).astype(bf16)
        dyl_ref[...] = (dm * gl).astype(bf16)
        dgs = dm * ys_ref[...] * gs * (1.0 - gs)
        dgl = dm * yl_ref[...] * gl * (1.0 - gl)
        dg_ref[:, 0:d] = dgs.astype(bf16)
        dg_ref[:, d:2 * d] = dgl.astype(bf16)

        @pl.when(pl.program_id(0) == 0)
        def _():
            dbg_ref[...] = jnp.zeros_like(dbg_ref)

        dbg_ref[:, 0:d] += jnp.sum(dgs, axis=0, keepdims=True)
        dbg_ref[:, d:2 * d] += jnp.sum(dgl, axis=0, keepdims=True)

    row = pl.BlockSpec((tr, d), lambda i: (i, 0))
    return pl.pallas_call(
        body, name=name, grid=(t // tr,),
        in_specs=[pl.BlockSpec((tr, d), lambda i: (i, gb)), pl.BlockSpec((tr, d), lambda i: (i, gb + 1)),
                  pl.BlockSpec((1, d), lambda i: (0, 0)), pl.BlockSpec((1, d), lambda i: (0, 1)), row, row, row],
        out_specs=[pl.BlockSpec((tr, 2 * d), lambda i: (i, OFF_GATES // (2 * d))), row, row,
                   pl.BlockSpec((1, 2 * d), lambda i: (0, 0))],
        out_shape=[jax.ShapeDtypeStruct((t, PROJ_W), bf16), jax.ShapeDtypeStruct((t, d), bf16),
                   jax.ShapeDtypeStruct((t, d), bf16), jax.ShapeDtypeStruct((1, 2 * d), f32)],
        compiler_params=pltpu.CompilerParams(dimension_semantics=("arbitrary",)),
    )(proj, proj, bg, bg, y_ssm, y_lru, dmix)


def swiglu_fwd(ff, *, name):
    t = ff.shape[0]
    hd = FFN_HIDDEN
    tr = _pick(t, 128, 8)

    def body(f_ref, o_ref):
        o_ref[...] = (_silu(f_ref[:, 0:hd]) * f_ref[:, hd:2 * hd]).astype(bf16)

    return pl.pallas_call(
        body, name=name, grid=(t // tr,), in_specs=[pl.BlockSpec((tr, 2 * hd), lambda i: (i, 0))],
        out_specs=pl.BlockSpec((tr, hd), lambda i: (i, 0)), out_shape=jax.ShapeDtypeStruct((t, hd), bf16),
    )(ff)


def swiglu_bwd(ff, dact, *, name):
    t = ff.shape[0]
    hd = FFN_HIDDEN
    tr = _pick(t, 128, 8)

    def body(f_ref, d_ref, o_ref):
        gate, up, dv = f_ref[:, 0:hd], f_ref[:, hd:2 * hd], d_ref[...]
        o_ref[:, 0:hd] = (dv * up * _dsilu(gate)).astype(bf16)
        o_ref[:, hd:2 * hd] = (dv * _silu(gate)).astype(bf16)

    return pl.pallas_call(
        body, name=name, grid=(t // tr,),
        in_specs=[pl.BlockSpec((tr, 2 * hd), lambda i: (i, 0)), pl.BlockSpec((tr, hd), lambda i: (i, 0))],
        out_specs=pl.BlockSpec((tr, 2 * hd), lambda i: (i, 0)), out_shape=jax.ShapeDtypeStruct((t, 2 * hd), bf16),
    )(ff, dact)


def _adam_math(w, g, m, v):
    m = ADAM_B1 * m + (1.0 - ADAM_B1) * g
    v = ADAM_B2 * v + (1.0 - ADAM_B2) * (g * g)
    m_hat = m / (1.0 - ADAM_B1 ** ADAM_STEP)
    v_hat = v / (1.0 - ADAM_B2 ** ADAM_STEP)
    delta = -ADAM_LR * (m_hat / (jnp.sqrt(v_hat) + ADAM_EPS) + ADAM_WD * w)
    return delta, m, v


def _row_tile(rows, cols):
    cap = max(8, (1 << 18) // cols)
    return _pick(rows, cap, 8) if rows % 8 == 0 else rows


def adamw(w, g, m, v, *, name):
    rows, cols = w.shape
    tr = _row_tile(rows, cols)

    def body(w_ref, g_ref, m_ref, v_ref, d_ref, nm_ref, nv_ref):
        d, nm, nv = _adam_math(w_ref[...], g_ref[...], m_ref[...], v_ref[...])
        d_ref[...] = d
        nm_ref[...] = nm
        nv_ref[...] = nv

    tile = pl.BlockSpec((tr, cols), lambda i: (i, 0))
    return pl.pallas_call(
        body, name=name, grid=(rows // tr,), in_specs=[tile] * 4, out_specs=[tile] * 3,
        out_shape=[jax.ShapeDtypeStruct((rows, cols), f32)] * 3,
    )(w, g, m, v)


def pair_add(dw, rbuf, idx, *, name):
    n, rows, cols = dw.shape
    hr = rows // 2
    tr = _row_tile(hr, cols)
    nrt = hr // tr

    def body(idx_ref, a_ref, b_ref, o_ref, own_ref):
        s = a_ref[...] + b_ref[...]
        o_ref[...] = s.astype(bf16)

        @pl.when(pl.program_id(1) == idx_ref[0])
        def _():
            own_ref[...] = s[0]

    return pl.pallas_call(
        body, name=name,
        grid_spec=pltpu.PrefetchScalarGridSpec(
            num_scalar_prefetch=1, grid=(nrt, n),
            in_specs=[pl.BlockSpec((1, tr, cols), lambda i, k, idx: (k, idx[1] * nrt + i, 0)),
                      pl.BlockSpec((1, tr, cols), lambda i, k, idx: (k, i, 0))],
            out_specs=[pl.BlockSpec((1, tr, cols), lambda i, k, idx: (k, i, 0)),
                       pl.BlockSpec((tr, cols), lambda i, k, idx: (i, 0))]),
        out_shape=[jax.ShapeDtypeStruct((n, hr, cols), bf16), jax.ShapeDtypeStruct((hr, cols), f32)],
    )(idx, dw, rbuf)


def chip_sum(own, rbuf, idx, *, name):
    hr, cols = own.shape
    tr = _row_tile(hr, cols)
    nrt = hr // tr

    def body(idx_ref, a_ref, b_ref, o_ref):
        o_ref[...] = ((a_ref[...] + b_ref[0].astype(f32)) + b_ref[1].astype(f32)) + b_ref[2].astype(f32)

    return pl.pallas_call(
        body, name=name,
        grid_spec=pltpu.PrefetchScalarGridSpec(
            num_scalar_prefetch=1, grid=(nrt,),
            in_specs=[pl.BlockSpec((tr, cols), lambda i, idx: (i, 0)),
                      pl.BlockSpec((3, tr, cols), lambda i, idx: (0, i, 0))],
            out_specs=pl.BlockSpec((tr, cols), lambda i, idx: (idx[1] * nrt + i, 0))),
        out_shape=jax.ShapeDtypeStruct((2 * hr, cols), f32),
    )(idx, own, rbuf)


def sum8(rbuf, *, name):
    n, rows, cols = rbuf.shape
    tr = _row_tile(rows, cols * n)

    def body(a_ref, o_ref):
        acc = a_ref[0]
        for k in range(1, n):
            acc = acc + a_ref[k]
        o_ref[...] = acc

    return pl.pallas_call(
        body, name=name, grid=(rows // tr,), in_specs=[pl.BlockSpec((n, tr, cols), lambda i: (0, i, 0))],
        out_specs=pl.BlockSpec((tr, cols), lambda i: (i, 0)), out_shape=jax.ShapeDtypeStruct((rows, cols), f32),
    )(rbuf)


def _coords():
    return lax.axis_index("x"), lax.axis_index("y"), lax.axis_index("c")


def _other_chips(x, y):
    return [(1 - x, y), (x, 1 - y), (1 - x, 1 - y)]


def gather_weights(shards, *, name):
    n = len(shards)
    halves = [s.shape[0] // 2 for s in shards]

    def body(*refs):
        ins, outs = refs[:n], refs[n:2 * n]
        send1, recv1, send2, recv2 = refs[2 * n:]
        x, y, c = _coords()
        me = 2 * x + y
        chips = _other_chips(x, y)
        sibling = (x, y, 1 - c)

        def half(i, k, hc):
            return outs[i].at[k, pl.ds(hc * halves[i], halves[i]), :]

        def ici(i, j):
            return pltpu.make_async_remote_copy(
                src_ref=ins[i].at[pl.ds(c * halves[i], halves[i]), :], dst_ref=half(i, me, c),
                send_sem=send1.at[i, j], recv_sem=recv1.at[i, j], device_id=(*chips[j], c), device_id_type=MESH)

        def landed(i, j):
            kj = 2 * chips[j][0] + chips[j][1]
            return pltpu.make_async_remote_copy(
                src_ref=half(i, kj, c), dst_ref=half(i, kj, c),
                send_sem=send2.at[i, j], recv_sem=recv1.at[i, j], device_id=sibling, device_id_type=MESH)

        def from_sibling(i, j):
            kj = 2 * chips[j][0] + chips[j][1]
            return pltpu.make_async_remote_copy(
                src_ref=half(i, kj, 1 - c), dst_ref=half(i, kj, 1 - c),
                send_sem=send2.at[i, j], recv_sem=recv2.at[i, j], device_id=sibling, device_id_type=MESH)

        def d2d(i, j):
            kj = 2 * chips[j][0] + chips[j][1]
            return pltpu.make_async_remote_copy(
                src_ref=half(i, kj, c), dst_ref=half(i, kj, c),
                send_sem=send2.at[i, j], recv_sem=recv2.at[i, j], device_id=sibling, device_id_type=MESH)

        for j in range(3):
            for i in range(n):
                ici(i, j).start()
        for j in range(3):
            for i in range(n):
                landed(i, j).wait_recv()
                d2d(i, j).start()
        for j in range(3):
            for i in range(n):
                from_sibling(i, j).wait_recv()
        for j in range(3):
            for i in range(n):
                ici(i, j).wait_send()
                d2d(i, j).wait_send()

    return pl.pallas_call(
        body, name=name, in_specs=[ANY] * n, out_specs=[ANY] * n,
        out_shape=[jax.ShapeDtypeStruct((N_CHIPS,) + s.shape, s.dtype) for s in shards],
        scratch_shapes=[pltpu.SemaphoreType.DMA((n, 3))] * 4,
    )(*shards)


def pair_exchange(grads, *, name):
    n = len(grads)
    halves = [g.shape[1] // 2 for g in grads]

    def body(*refs):
        ins, outs = refs[:n], refs[n:2 * n]
        send, recv = refs[2 * n:]
        x, y, c = _coords()
        cps = [pltpu.make_async_remote_copy(
            src_ref=ins[i].at[:, pl.ds((1 - c) * halves[i], halves[i]), :], dst_ref=outs[i],
            send_sem=send.at[i], recv_sem=recv.at[i], device_id=(x, y, 1 - c), device_id_type=MESH) for i in range(n)]
        for cp in cps:
            cp.start()
        for cp in cps:
            cp.wait()

    return pl.pallas_call(
        body, name=name, in_specs=[ANY] * n, out_specs=[ANY] * n,
        out_shape=[jax.ShapeDtypeStruct((N_CHIPS, g.shape[1] // 2, g.shape[2]), g.dtype) for g in grads],
        scratch_shapes=[pltpu.SemaphoreType.DMA((n,))] * 2,
    )(*grads)


def pair_gather(bufs, *, name):
    n = len(bufs)

    def body(*refs):
        ins, outs = refs[:n], refs[n:2 * n]
        send, recv = refs[2 * n:]
        x, y, c = _coords()
        cps = []
        for i in range(n):
            hr = ins[i].shape[0] // 2
            cps.append(pltpu.make_async_remote_copy(
                src_ref=ins[i].at[pl.ds(c * hr, hr), :], dst_ref=outs[i].at[pl.ds(c * hr, hr), :],
                send_sem=send.at[i], recv_sem=recv.at[i], device_id=(x, y, 1 - c), device_id_type=MESH))
        for cp in cps:
            cp.start()
        for i in range(n):
            hr = ins[i].shape[0] // 2
            pltpu.make_async_remote_copy(
                src_ref=ins[i].at[pl.ds((1 - c) * hr, hr), :], dst_ref=outs[i].at[pl.ds((1 - c) * hr, hr), :],
                send_sem=send.at[i], recv_sem=recv.at[i], device_id=(x, y, 1 - c), device_id_type=MESH).wait_recv()
        for cp in cps:
            cp.wait_send()

    return pl.pallas_call(
        body, name=name, in_specs=[ANY] * n, out_specs=[ANY] * n,
        out_shape=[jax.ShapeDtypeStruct(b.shape, b.dtype) for b in bufs],
        input_output_aliases={i: i for i in range(n)},
        scratch_shapes=[pltpu.SemaphoreType.DMA((n,))] * 2,
    )(*bufs)


def all_exchange(buf, *, name):
    rows, cols = buf.shape

    def body(in_ref, out_ref, send, recv):
        x, y, c = _coords()
        me = 4 * x + 2 * y + c
        cps = []
        for d in range(1, 8):
            px = 1 - x if d & 4 else x
            py = 1 - y if d & 2 else y
            pc = 1 - c if d & 1 else c
            cps.append(pltpu.make_async_remote_copy(
                src_ref=in_ref, dst_ref=out_ref.at[me], send_sem=send.at[d - 1], recv_sem=recv.at[d - 1],
                device_id=(px, py, pc), device_id_type=MESH))
        for cp in cps:
            cp.start()
        for d in range(1, 8):
            px = 1 - x if d & 4 else x
            py = 1 - y if d & 2 else y
            pc = 1 - c if d & 1 else c
            src = 4 * px + 2 * py + pc
            pltpu.make_async_remote_copy(
                src_ref=in_ref, dst_ref=out_ref.at[src], send_sem=send.at[d - 1], recv_sem=recv.at[d - 1],
                device_id=(px, py, pc), device_id_type=MESH).wait_recv()
        for cp in cps:
            cp.wait_send()

    return pl.pallas_call(
        body, name=name, in_specs=[ANY], out_specs=ANY,
        out_shape=jax.ShapeDtypeStruct((8, rows, cols), buf.dtype),
        scratch_shapes=[pltpu.SemaphoreType.DMA((7,)), pltpu.SemaphoreType.DMA((7,))],
    )(buf)


HBM = pl.BlockSpec(memory_space=pltpu.HBM)
SEM = pl.BlockSpec(memory_space=pltpu.SEMAPHORE)
EFFECT = pltpu.SideEffectType.DATAFLOW_SIDE_EFFECTING


def split_start(arrays, after, copies, sem_shape, *, name):
    na = len(arrays)

    def body(*refs):
        for cp in copies(refs[:na], refs[na + 1], refs[na + 2]):
            cp.start()
        refs[-1][...] = jnp.zeros((8, 128), f32)

    outs = pl.pallas_call(
        body, name=name,
        out_shape=(pltpu.SemaphoreType.DMA(sem_shape), pltpu.SemaphoreType.DMA(sem_shape),
                   *[pltpu.HBM(a.shape, a.dtype) for a in arrays], jax.ShapeDtypeStruct((8, 128), f32)),
        in_specs=[HBM] * na + [ANY], out_specs=(SEM, SEM, *[HBM] * na, pl.BlockSpec(memory_space=pltpu.VMEM)),
        input_output_aliases={i: 2 + i for i in range(na)},
        compiler_params=pltpu.CompilerParams(has_side_effects=EFFECT),
    )(*[pltpu.with_memory_space_constraint(a, pltpu.HBM) for a in arrays], after)
    return outs[0], outs[1], list(outs[2:2 + na]), outs[-1]


def split_wait(send, recv, arrays, after, copies, *, name):
    na = len(arrays)

    def body(*refs):
        for cp in copies(refs[:na], refs[na], refs[na + 1]):
            cp.wait_send()
            cp.wait_recv()

    outs = pl.pallas_call(
        body, name=name, out_shape=tuple(pltpu.HBM(a.shape, a.dtype) for a in arrays),
        in_specs=[HBM] * na + [SEM, SEM, ANY], out_specs=tuple([HBM] * na),
        input_output_aliases={i: i for i in range(na)},
        compiler_params=pltpu.CompilerParams(has_side_effects=EFFECT),
    )(*arrays, send, recv, after)
    return list(outs)


def gather_copies(n):
    def copies(refs, send, recv):
        x, y, c = _coords()
        me = 2 * x + y
        chips = _other_chips(x, y)
        return [pltpu.make_async_remote_copy(
            src_ref=refs[i], dst_ref=refs[n + i].at[me], send_sem=send.at[3 * i + j], recv_sem=recv.at[3 * i + j],
            device_id=(*chips[j], c), device_id_type=MESH) for j in range(3) for i in range(n)]
    return copies


def reduce_copies(n):
    def copies(refs, send, recv):
        x, y, c = _coords()
        chips = _other_chips(x, y)
        return [pltpu.make_async_remote_copy(
            src_ref=refs[i].at[2 * chips[j][0] + chips[j][1]], dst_ref=refs[n + i].at[j],
            send_sem=send.at[3 * i + j], recv_sem=recv.at[3 * i + j], device_id=(*chips[j], c), device_id_type=MESH)
            for j in range(3) for i in range(n)]
    return copies


def _pack(arrs):
    flat = []
    for a in arrs:
        v = a.reshape(-1).astype(f32)
        pad = (-v.shape[0]) % 128
        flat.append(jnp.pad(v, (0, pad)) if pad else v)
    v = jnp.concatenate(flat)
    rows = v.shape[0] // 128
    pad_rows = (-rows) % 256
    v = v.reshape(rows, 128)
    return jnp.pad(v, ((0, pad_rows), (0, 0))) if pad_rows else v


def _unpack(buf, shapes):
    out, row = [], 0
    for s in shapes:
        size = math.prod(s)
        rows = -(-size // 128)
        out.append(buf[row:row + rows].reshape(-1)[:size].reshape(s))
        row += rows
    return out


def _perm_in_cols(w):
    gates, z = w[..., 0:2048], w[..., 2048:4096]
    xbc = w[..., 4096:7168]
    dt, lx, ly = w[..., 7168:7200], w[..., 7200:8480], w[..., 8480:9760]
    pad = jnp.zeros(w.shape[:-1] + (DT_PAD_W - SSM_HEADS,), w.dtype)
    return jnp.concatenate([gates, z, lx, ly, dt, pad, _perm_xbc_cols(xbc)], axis=-1)


def _perm_xbc_cols(w):
    parts = []
    for g in range(SSM_GROUPS):
        parts += [w[..., g * 512:(g + 1) * 512], w[..., 2048 + g * 128:2048 + (g + 1) * 128],
                  w[..., 2560 + g * 128:2560 + (g + 1) * 128]]
    return jnp.concatenate(parts, axis=-1)


def _unperm_xbc_cols(w):
    xs = [w[..., g * XBC_GROUP_W:g * XBC_GROUP_W + 512] for g in range(SSM_GROUPS)]
    bs = [w[..., g * XBC_GROUP_W + 512:g * XBC_GROUP_W + 640] for g in range(SSM_GROUPS)]
    cs = [w[..., g * XBC_GROUP_W + 640:(g + 1) * XBC_GROUP_W] for g in range(SSM_GROUPS)]
    return jnp.concatenate(xs + bs + cs, axis=-1)


def _unperm_in_cols(w):
    xbc = _unperm_xbc_cols(w[..., OFF_XBC:OFF_XBC + 3072])
    return jnp.concatenate([w[..., OFF_GATES:OFF_GATES + 2048], w[..., OFF_Z:OFF_Z + 2048], xbc,
                            w[..., OFF_DT:OFF_DT + 32], w[..., OFF_LX:OFF_LX + 1280], w[..., OFF_LY:OFF_LY + 1280]], axis=-1)


def _col_shards(w, n=N_CHIPS):
    r, c = w.shape
    return jnp.transpose(w.reshape(r, n, c // n), (1, 0, 2))


def _from_col_shards(w):
    n, r, c = w.shape
    return jnp.transpose(w, (1, 0, 2)).reshape(r, n * c)


def kernel(x, norm1_w, w_in, b_branch_gate, ssm_conv_w, ssm_conv_b, ssm_dt_bias, ssm_a_log, ssm_d, ssm_norm_w, w_out_ssm, lru_conv_w, lru_conv_b, lru_w_r, lru_b_r, lru_w_i, lru_b_i, lru_lambda, w_out_lru, w_out, norm2_w, w_ffn_in, w_ffn_out, norm_f_w, loss_target, m_norm1_w, m_w_in, m_b_branch_gate, m_ssm_conv_w, m_ssm_conv_b, m_ssm_dt_bias, m_ssm_a_log, m_ssm_d, m_ssm_norm_w, m_w_out_ssm, m_lru_conv_w, m_lru_conv_b, m_lru_w_r, m_lru_b_r, m_lru_w_i, m_lru_b_i, m_lru_lambda, m_w_out_lru, m_w_out, m_norm2_w, m_w_ffn_in, m_w_ffn_out, m_norm_f_w, v_norm1_w, v_w_in, v_b_branch_gate, v_ssm_conv_w, v_ssm_conv_b, v_ssm_dt_bias, v_ssm_a_log, v_ssm_d, v_ssm_norm_w, v_w_out_ssm, v_lru_conv_w, v_lru_conv_b, v_lru_w_r, v_lru_b_r, v_lru_w_i, v_lru_b_i, v_lru_lambda, v_w_out_lru, v_w_out, v_norm2_w, v_w_ffn_in, v_w_ffn_out, v_norm_f_w):
    xi, yi, ci = lax.axis_index("x"), lax.axis_index("y"), lax.axis_index("c")
    me = 2 * xi + yi
    idx = jnp.stack([me, ci]).astype(jnp.int32)
    x2 = x[0]
    tgt = loss_target[0]

    big_names = ["w_in", "w_out_ssm", "w_out_lru", "w_out", "w_ffn_in", "w_ffn_out"]
    big_w = dict(w_in=w_in[0], w_out_ssm=w_out_ssm[0], w_out_lru=w_out_lru[0], w_out=w_out[0], w_ffn_in=w_ffn_in[0],
                 w_ffn_out=w_ffn_out[0])
    big_m = dict(w_in=m_w_in[0], w_out_ssm=m_w_out_ssm[0], w_out_lru=m_w_out_lru[0], w_out=m_w_out[0],
                 w_ffn_in=m_w_ffn_in[0], w_ffn_out=m_w_ffn_out[0])
    big_v = dict(w_in=v_w_in[0], w_out_ssm=v_w_out_ssm[0], w_out_lru=v_w_out_lru[0], w_out=v_w_out[0],
                 w_ffn_in=v_w_ffn_in[0], w_ffn_out=v_w_ffn_out[0])
    conv_pad = jnp.zeros((16, 768), f32).at[0:4, :].set(ssm_conv_w[0]).at[8:12, 0:320].set(lru_conv_w[0])
    mine = [big_w["w_in"].astype(bf16), conv_pad]
    gathered = gather_weights(mine, name="gather_weights")
    g_in, g_conv = [lax.dynamic_update_index_in_dim(g, s, me, 0) for g, s in zip(gathered, mine)]
    w_in_p = _perm_in_cols(_from_col_shards(g_in))
    late_names = big_names[1:]
    late = [big_w[k].astype(bf16) for k in late_names]
    late_lands = [lax.empty((N_CHIPS,) + s.shape, bf16) for s in late]
    g_send, g_recv, g_arrays, g_token = split_start(late + late_lands, g_conv, gather_copies(5), (15,),
                                                    name="gather_late_start")
    ssm_cw_full = _from_col_shards(g_conv[:, 0:4, :])
    lru_cw_full = _from_col_shards(g_conv[:, 8:12, 0:320])
    ssm_cw_p = _perm_xbc_cols(ssm_cw_full)
    ssm_cb_p = _perm_xbc_cols(ssm_conv_b)

    par = jnp.stack([ssm_dt_bias[0], ssm_a_log[0], ssm_d[0]], axis=0).reshape(3, SSM_GROUPS, SSM_HPG)
    par_row = jnp.zeros((SSM_GROUPS, 8, 8), f32).at[:, 0:3, :].set(jnp.transpose(par, (1, 0, 2)))
    par_col = jnp.transpose(par_row, (0, 2, 1))

    hn1 = rms_fwd(x2, norm1_w + g_token[0:1, 0:1], name="rms1_fwd")
    proj = mm(hn1, w_in_p, "nn", name="in_proj")
    t = x2.shape[0]
    dtr = jnp.transpose(proj[:, OFF_DT:OFF_DT + 32].reshape(t, SSM_GROUPS, SSM_HPG), (1, 0, 2))
    dtr_t = jnp.transpose(dtr, (0, 2, 1))
    xbc_pre, xbc_post = conv_fwd(proj, OFF_XBC, SSM_CONV_DIM, ssm_cw_p, ssm_cb_p, silu=True, name="ssm_conv_fwd")
    y_ssd, s_in = ssd_fwd(xbc_post, dtr, dtr_t, par_row, par_col, name="ssd_fwd")
    yn = gnorm_fwd(y_ssd, proj, ssm_norm_w, name="gnorm_fwd")
    g_arrays = split_wait(g_send, g_recv, g_arrays, yn, gather_copies(5), name="gather_late_wait")
    g_out_ssm, g_out_lru, g_out, g_ffn_in, g_ffn_out = [
        lax.dynamic_update_index_in_dim(g, s, me, 0) for g, s in zip(g_arrays[5:], late)]
    w_out_ssm_f = g_out_ssm.reshape(SSM_D_INNER, D_MODEL)
    w_out_lru_f = g_out_lru.reshape(LRU_WIDTH, D_MODEL)
    w_out_f = g_out.reshape(D_MODEL, D_MODEL)
    w_ffn_in_f = _from_col_shards(g_ffn_in)
    w_ffn_out_f = g_ffn_out.reshape(FFN_HIDDEN, D_MODEL)
    y_ssm = mm(yn, w_out_ssm_f, "nn", name="out_ssm")
    (u_lru,) = conv_fwd(proj, OFF_LX, LRU_WIDTH, lru_cw_full, lru_conv_b, silu=False, name="lru_conv_fwd")
    h_lru, o_lru = lru_fwd(u_lru, proj, lru_w_r[0], lru_b_r, lru_w_i[0], lru_b_i, lru_lambda, name="lru_fwd")
    y_lru = mm(o_lru, w_out_lru_f, "nn", name="out_lru")
    mix = merge_fwd(proj, b_branch_gate, y_ssm, y_lru, name="merge_fwd")
    h1 = mm(mix, w_out_f, "nn", add=x2, name="out_proj")
    hn2 = rms_fwd(h1, norm2_w, name="rms2_fwd")
    ff = mm(hn2, w_ffn_in_f, "nn", name="ffn_in")
    act = swiglu_fwd(ff, name="swiglu_fwd")
    h2 = mm(act, w_ffn_out_f, "nn", add=h1, name="ffn_out")
    loss_tile, dh2, d_norm_f = loss_head(h2, norm_f_w.reshape(1, D_MODEL), tgt, name="loss_head")
    loss = lax.psum(loss_tile[0, 0], ("x", "y", "c"))

    d_w_ffn_out = mm(act, dh2, "tn", name="d_w_ffn_out")
    dact = mm(dh2, w_ffn_out_f, "nt", name="d_act")
    dff = swiglu_bwd(ff, dact, name="swiglu_bwd")
    d_w_ffn_in = mm(hn2, dff, "tn", name="d_w_ffn_in")
    dhn2 = mm(dff, w_ffn_in_f, "nt", name="d_hn2")
    dh1, d_norm2 = rms_bwd(h1, norm2_w, dhn2, dh2, name="rms2_bwd")
    d_w_out = mm(mix, dh1, "tn", name="d_w_out")
    dmix = mm(dh1, w_out_f, "nt", name="d_mix")
    dproj, dy_ssm, dy_lru, d_bg = merge_bwd(proj, b_branch_gate, y_ssm, y_lru, dmix, name="merge_bwd")
    d_w_out_ssm = mm(yn, dy_ssm, "tn", name="d_w_out_ssm")
    d_w_out_lru = mm(o_lru, dy_lru, "tn", name="d_w_out_lru")
    early_g = [d_w_out_ssm.reshape(N_CHIPS, 512, D_MODEL), d_w_out_lru.reshape(N_CHIPS, 320, D_MODEL),
               d_w_out.reshape(N_CHIPS, 256, D_MODEL), _col_shards(d_w_ffn_in), d_w_ffn_out.reshape(N_CHIPS, 704, D_MODEL)]
    e_sib = pair_exchange(early_g, name="pair_exchange_early")
    e_pairs = [pair_add(g, rb, idx, name="pair_add_" + k) for g, rb, k in zip(early_g, e_sib, late_names)]
    e_lands = [lax.empty((3,) + p[0].shape[1:], bf16) for p in e_pairs]
    e_send, e_recv, e_arrays, e_token = split_start([p[0] for p in e_pairs] + e_lands, e_pairs[0][1], reduce_copies(5),
                                                    (15,), name="reduce_early_start")
    dyn = mm(dy_ssm, w_out_ssm_f, "nt", name="d_yn")
    dy_ssd, dproj, d_ssm_norm = gnorm_bwd(y_ssd, proj, ssm_norm_w + e_token[0:1, 0:1], dyn, dproj, name="gnorm_bwd")
    dxbc_post, ddtr, dpar = ssd_bwd(xbc_post, dtr, dtr_t, par_row, par_col, s_in, dy_ssd, name="ssd_bwd")
    dproj, d_ssm_cw_p, d_ssm_cb_p = conv_bwd(dxbc_post, xbc_pre, proj, OFF_XBC, ssm_cw_p, dproj, name="ssm_conv_bwd")
    do_lru = mm(dy_lru, w_out_lru_f, "nt", name="d_o_lru")
    du_lru, dproj, d_w_r, d_w_i, d_b_r, d_b_i, d_lam = lru_bwd(u_lru, proj, h_lru, do_lru, lru_w_r[0], lru_b_r, lru_w_i[0],
                                                               lru_b_i, lru_lambda, dproj, name="lru_bwd")
    dproj, d_lru_cw, d_lru_cb = conv_bwd(du_lru, None, proj, OFF_LX, lru_cw_full, dproj, name="lru_conv_bwd")
    ddt_cols = jnp.transpose(ddtr, (1, 0, 2)).reshape(t, SSM_HEADS).astype(bf16)
    ddt_cols = jnp.pad(ddt_cols, ((0, 0), (0, DT_PAD_W - SSM_HEADS)))
    dproj = lax.dynamic_update_slice(dproj, ddt_cols, (0, OFF_DT))
    d_w_in_p = mm(hn1, dproj, "tn", name="d_w_in")
    dhn1 = mm(dproj, w_in_p, "nt", name="d_hn1")
    grad_x, d_norm1 = rms_bwd(x2, norm1_w, dhn1, dh1, name="rms1_bwd")

    d_w_in_s = _col_shards(_unperm_in_cols(d_w_in_p))
    (l_sib,) = pair_exchange([d_w_in_s], name="pair_exchange_late")
    l_pair = pair_add(d_w_in_s, l_sib, idx, name="pair_add_w_in")
    l_land = lax.empty((3,) + l_pair[0].shape[1:], bf16)
    l_send, l_recv, l_arrays, l_token = split_start([l_pair[0], l_land], l_pair[1], reduce_copies(1), (3,),
                                                    name="reduce_late_start")
    e_arrays = split_wait(e_send, e_recv, e_arrays, l_token, reduce_copies(5), name="reduce_early_wait")
    e_half = [chip_sum(p[1], rb, idx, name="chip_sum_" + k) for p, rb, k in zip(e_pairs, e_arrays[5:], late_names)]
    big_out = {}
    for k, g in zip(late_names, pair_gather(e_half, name="pair_gather_early")):
        big_out[k] = (g,) + tuple(adamw(big_w[k], g, big_m[k], big_v[k], name="adamw_" + k))

    d_ssm_cw = _unperm_xbc_cols(d_ssm_cw_p)
    d_ssm_cb = _unperm_xbc_cols(d_ssm_cb_p)
    dpar_h = jnp.transpose(dpar[:, 0:3, :], (1, 0, 2)).reshape(3, SSM_HEADS)
    small_names = ["norm1_w", "b_branch_gate", "ssm_conv_b", "ssm_dt_bias", "ssm_a_log", "ssm_d", "ssm_norm_w",
                   "lru_conv_b", "lru_w_r", "lru_b_r", "lru_w_i", "lru_b_i", "lru_lambda", "norm2_w", "norm_f_w"]
    small_g = dict(norm1_w=d_norm1, b_branch_gate=d_bg, ssm_conv_b=d_ssm_cb, ssm_dt_bias=dpar_h[0:1], ssm_a_log=dpar_h[1:2],
                   ssm_d=dpar_h[2:3], ssm_norm_w=d_ssm_norm, lru_conv_b=d_lru_cb, lru_w_r=d_w_r[None], lru_b_r=d_b_r,
                   lru_w_i=d_w_i[None], lru_b_i=d_b_i, lru_lambda=d_lam, norm2_w=d_norm2, norm_f_w=d_norm_f.reshape(D_MODEL))
    small_w = dict(norm1_w=norm1_w, b_branch_gate=b_branch_gate, ssm_conv_b=ssm_conv_b, ssm_dt_bias=ssm_dt_bias,
                   ssm_a_log=ssm_a_log, ssm_d=ssm_d, ssm_norm_w=ssm_norm_w, lru_conv_b=lru_conv_b, lru_w_r=lru_w_r,
                   lru_b_r=lru_b_r, lru_w_i=lru_w_i, lru_b_i=lru_b_i, lru_lambda=lru_lambda, norm2_w=norm2_w, norm_f_w=norm_f_w)
    small_m = dict(norm1_w=m_norm1_w, b_branch_gate=m_b_branch_gate, ssm_conv_b=m_ssm_conv_b, ssm_dt_bias=m_ssm_dt_bias,
                   ssm_a_log=m_ssm_a_log, ssm_d=m_ssm_d, ssm_norm_w=m_ssm_norm_w, lru_conv_b=m_lru_conv_b, lru_w_r=m_lru_w_r,
                   lru_b_r=m_lru_b_r, lru_w_i=m_lru_w_i, lru_b_i=m_lru_b_i, lru_lambda=m_lru_lambda, norm2_w=m_norm2_w,
                   norm_f_w=m_norm_f_w)
    small_v = dict(norm1_w=v_norm1_w, b_branch_gate=v_b_branch_gate, ssm_conv_b=v_ssm_conv_b, ssm_dt_bias=v_ssm_dt_bias,
                   ssm_a_log=v_ssm_a_log, ssm_d=v_ssm_d, ssm_norm_w=v_ssm_norm_w, lru_conv_b=v_lru_conv_b, lru_w_r=v_lru_w_r,
                   lru_b_r=v_lru_b_r, lru_w_i=v_lru_w_i, lru_b_i=v_lru_b_i, lru_lambda=v_lru_lambda, norm2_w=v_norm2_w,
                   norm_f_w=v_norm_f_w)
    shapes = [small_w[k].shape for k in small_names]
    conv_shapes = [(4, SSM_CONV_DIM), (4, LRU_WIDTH)]
    g_pack = _pack([small_g[k] for k in small_names] + [d_ssm_cw, d_lru_cw])
    g_all = lax.dynamic_update_index_in_dim(all_exchange(g_pack, name="all_exchange"), g_pack, 2 * me + ci, 0)
    g_sum = sum8(g_all, name="sum8")
    g_ssm_cw_full, g_lru_cw_full = _unpack(g_sum, shapes + conv_shapes)[len(shapes):]
    g_ssm_cw = lax.dynamic_slice_in_dim(g_ssm_cw_full, me * 768, 768, axis=1)
    g_lru_cw = lax.dynamic_slice_in_dim(g_lru_cw_full, me * 320, 320, axis=1)
    loc_shapes = [(4, 768), (4, 320)]
    g_loc = _pack(_unpack(g_sum, shapes)[:len(shapes)] + [g_ssm_cw, g_lru_cw])
    w_loc = _pack([small_w[k] for k in small_names] + [ssm_conv_w[0], lru_conv_w[0]])
    m_loc = _pack([small_m[k] for k in small_names] + [m_ssm_conv_w[0], m_lru_conv_w[0]])
    v_loc = _pack([small_v[k] for k in small_names] + [v_ssm_conv_w[0], v_lru_conv_w[0]])
    d_loc, nm_loc, nv_loc = adamw(w_loc, g_loc, m_loc, v_loc, name="adamw_small")
    l_arrays = split_wait(l_send, l_recv, l_arrays, d_loc, reduce_copies(1), name="reduce_late_wait")
    l_half = chip_sum(l_pair[1], l_arrays[1], idx, name="chip_sum_w_in")
    (g_w_in,) = pair_gather([l_half], name="pair_gather_late")
    big_out["w_in"] = (g_w_in,) + tuple(adamw(big_w["w_in"], g_w_in, big_m["w_in"], big_v["w_in"], name="adamw_w_in"))
    small_out = {}
    unp = [_unpack(b, shapes + loc_shapes) for b in (g_loc, d_loc, nm_loc, nv_loc)]
    for i, k in enumerate(small_names + ["ssm_conv_w", "lru_conv_w"]):
        small_out[k] = tuple(u[i] for u in unp)

    order = ["norm1_w", "w_in", "b_branch_gate", "ssm_conv_w", "ssm_conv_b", "ssm_dt_bias", "ssm_a_log", "ssm_d", "ssm_norm_w",
             "w_out_ssm", "lru_conv_w", "lru_conv_b", "lru_w_r", "lru_b_r", "lru_w_i", "lru_b_i", "lru_lambda", "w_out_lru",
             "w_out", "norm2_w", "w_ffn_in", "w_ffn_out", "norm_f_w"]
    outs = [loss, grad_x[None]]
    for which in range(4):
        for k in order:
            if k in big_out:
                outs.append(big_out[k][which][None])
            elif k in ("ssm_conv_w", "lru_conv_w"):
                outs.append(small_out[k][which][None])
            else:
                outs.append(small_out[k][which])
    return tuple(outs)
```

```python
import functools
import math

import jax
import jax.numpy as jnp
from jax import lax
from jax.experimental import pallas as pl
from jax.experimental.pallas import tpu as pltpu

f32 = jnp.float32
bf16 = jnp.bfloat16

D_MODEL = 1024
SSM_D_INNER = 2048
SSM_HEADS = 32
SSM_HEAD_DIM = 64
SSM_GROUPS = 4
SSM_HPG = 8
SSM_D_STATE = 128
SSM_CHUNK = 128
SSM_GROUP_W = 512
SSM_CONV_DIM = 3072
XBC_GROUP_W = 768
LRU_WIDTH = 1280
LRU_BLOCKS = 10
LRU_BLOCK = 128
LRU_C = 8.0
FFN_HIDDEN = 2816
RMS_EPS = 1e-6
IN_PROJ_DIM = 9760
N_CHIPS = 4

OFF_GATES = 0
OFF_Z = 2048
OFF_LX = 4096
OFF_LY = 5376
OFF_DT = 6656
DT_PAD_W = 256
OFF_XBC = 6912
PROJ_W = 9984

ADAM_LR = 0.001
ADAM_B1 = 0.9
ADAM_B2 = 0.999
ADAM_EPS = 1e-08
ADAM_WD = 0.01
ADAM_STEP = 10

MESH = pl.DeviceIdType.MESH
ANY = pl.BlockSpec(memory_space=pl.ANY)

NN = (((1,), (0,)), ((), ()))
NT = (((1,), (1,)), ((), ()))
TN = (((0,), (0,)), ((), ()))


def _pick(n, cap, mult=128):
    best = None
    for t in range(mult, min(n, cap) + 1, mult):
        if n % t == 0:
            best = t
    return best if best is not None else n


def _sigmoid(x):
    return 1.0 / (1.0 + jnp.exp(-x))


def _softplus(x):
    return jnp.maximum(x, 0.0) + jnp.log(1.0 + jnp.exp(-jnp.abs(x)))


def _silu(x):
    return x * _sigmoid(x)


def _dsilu(x):
    s = _sigmoid(x)
    return s * (1.0 + x * (1.0 - s))


_GELU_K = math.sqrt(2.0 / math.pi)


def _gelu(x):
    return 0.5 * x * (1.0 + jnp.tanh(_GELU_K * (x + 0.044715 * x * x * x)))


def _dgelu(x):
    t = jnp.tanh(_GELU_K * (x + 0.044715 * x * x * x))
    return 0.5 * (1.0 + t) + 0.5 * x * (1.0 - t * t) * _GELU_K * (1.0 + 3.0 * 0.044715 * x * x)


def _expm1(x):
    poly = x * (1.0 + x * (0.5 + x * (1.0 / 6.0 + x * (1.0 / 24.0 + x * (1.0 / 120.0 + x * (1.0 / 720.0))))))
    return jnp.where(jnp.abs(x) < 0.1, poly, jnp.exp(x) - 1.0)


def _dot(a, b, dn):
    return lax.dot_general(a.astype(bf16), b.astype(bf16), dn, preferred_element_type=f32)


def _dot_01(a, b, dn, split, terms):
    r = a if split == 0 else b
    out = None
    for _ in range(terms):
        h = r.astype(bf16)
        r = r - h.astype(f32)
        d = lax.dot_general(h if split == 0 else a.astype(bf16), b.astype(bf16) if split == 0 else h, dn,
                            preferred_element_type=f32)
        out = d if out is None else out + d
    return out


def mm(a, b, mode, *, name, add=None, after=None, out_dtype=f32):
    if mode == "nn":
        (m, k), (k2, n) = a.shape, b.shape
    elif mode == "nt":
        (m, k), (n, k2) = a.shape, b.shape
    else:
        (k, m), (k2, n) = a.shape, b.shape
    assert k == k2, (a.shape, b.shape, mode)
    tm, tn, tk = _pick(m, 1024), _pick(n, 1024), _pick(k, 1024)
    nk = k // tk
    dn = {"nn": NN, "nt": NT, "tn": TN}[mode]
    a_spec = pl.BlockSpec((tk, tm), lambda i, j, kk: (kk, i)) if mode == "tn" else pl.BlockSpec((tm, tk), lambda i, j, kk: (i, kk))
    b_spec = pl.BlockSpec((tn, tk), lambda i, j, kk: (j, kk)) if mode == "nt" else pl.BlockSpec((tk, tn), lambda i, j, kk: (kk, j))
    o_spec = pl.BlockSpec((tm, tn), lambda i, j, kk: (i, j))
    has_add = add is not None

    n_extra = int(has_add) + int(after is not None)

    def body(a_ref, b_ref, *rest):
        add_ref = rest[0] if has_add else None
        o_ref = rest[n_extra]

        def finish(r):
            if has_add:
                r = r + add_ref[...]
            o_ref[...] = r.astype(out_dtype)

        if nk == 1:
            finish(_dot(a_ref[...], b_ref[...], dn))
            return
        acc = rest[-1]
        kk = pl.program_id(2)

        @pl.when(kk == 0)
        def _():
            acc[...] = jnp.zeros_like(acc)

        acc[...] += _dot(a_ref[...], b_ref[...], dn)

        @pl.when(kk == nk - 1)
        def _():
            finish(acc[...])

    ins = [a, b] + ([add] if has_add else []) + ([after] if after is not None else [])
    in_specs = [a_spec, b_spec] + ([o_spec] if has_add else []) + ([ANY] if after is not None else [])
    return pl.pallas_call(
        body, name=name, grid=(m // tm, n // tn, nk), in_specs=in_specs, out_specs=o_spec,
        out_shape=jax.ShapeDtypeStruct((m, n), out_dtype),
        scratch_shapes=[pltpu.VMEM((tm, tn), f32)] if nk > 1 else [],
        compiler_params=pltpu.CompilerParams(dimension_semantics=("parallel", "parallel", "arbitrary")),
    )(*ins)


def rms_fwd(x, w, *, name):
    t, d = x.shape
    tr = _pick(t, 256, 8)

    def body(x_ref, w_ref, o_ref):
        xv = x_ref[...]
        r = lax.rsqrt(jnp.mean(xv * xv, axis=-1, keepdims=True) + RMS_EPS)
        o_ref[...] = (xv * r * w_ref[...]).astype(bf16)

    return pl.pallas_call(
        body, name=name, grid=(t // tr,),
        in_specs=[pl.BlockSpec((tr, d), lambda i: (i, 0)), pl.BlockSpec((1, d), lambda i: (0, 0))],
        out_specs=pl.BlockSpec((tr, d), lambda i: (i, 0)), out_shape=jax.ShapeDtypeStruct((t, d), bf16),
    )(x, w)


def _rms_bwd_math(xv, wv, dy):
    r = lax.rsqrt(jnp.mean(xv * xv, axis=-1, keepdims=True) + RMS_EPS)
    g = dy * wv
    dx = r * g - xv * (r * r * r) * jnp.mean(g * xv, axis=-1, keepdims=True)
    dw = jnp.sum(dy * xv * r, axis=0, keepdims=True)
    return dx, dw


def rms_bwd(x, w, dy, res, *, name):
    t, d = x.shape
    tr = _pick(t, 256, 8)

    def body(x_ref, w_ref, dy_ref, res_ref, dx_ref, dw_ref):
        dx, dw = _rms_bwd_math(x_ref[...], w_ref[...], dy_ref[...])
        dx_ref[...] = dx + res_ref[...]

        @pl.when(pl.program_id(0) == 0)
        def _():
            dw_ref[...] = jnp.zeros_like(dw_ref)

        dw_ref[...] += dw

    row = pl.BlockSpec((tr, d), lambda i: (i, 0))
    vec = pl.BlockSpec((1, d), lambda i: (0, 0))
    return pl.pallas_call(
        body, name=name, grid=(t // tr,), in_specs=[row, vec, row, row], out_specs=[row, vec],
        out_shape=[jax.ShapeDtypeStruct((t, d), f32), jax.ShapeDtypeStruct((1, d), f32)],
        compiler_params=pltpu.CompilerParams(dimension_semantics=("arbitrary",)),
    )(x, w, dy, res)


def loss_head(h, w, target, *, name):
    t, d = h.shape
    tr = _pick(t, 256, 8)

    def body(h_ref, w_ref, t_ref, loss_ref, dh_ref, dw_ref):
        xv, wv = h_ref[...], w_ref[...]
        r = lax.rsqrt(jnp.mean(xv * xv, axis=-1, keepdims=True) + RMS_EPS)
        err = xv * r * wv - t_ref[...]
        part = 0.5 * jnp.sum(jnp.mean(err * err, axis=-1, keepdims=True), axis=0, keepdims=True)
        dx, dw = _rms_bwd_math(xv, wv, err * (1.0 / d))
        dh_ref[...] = dx

        @pl.when(pl.program_id(0) == 0)
        def _():
            dw_ref[...] = jnp.zeros_like(dw_ref)
            loss_ref[...] = jnp.zeros_like(loss_ref)

        dw_ref[...] += dw
        loss_ref[...] += part

    row = pl.BlockSpec((tr, d), lambda i: (i, 0))
    vec = pl.BlockSpec((1, d), lambda i: (0, 0))
    return pl.pallas_call(
        body, name=name, grid=(t // tr,), in_specs=[row, vec, row],
        out_specs=[pl.BlockSpec((8, 128), lambda i: (0, 0)), row, vec],
        out_shape=[jax.ShapeDtypeStruct((8, 128), f32), jax.ShapeDtypeStruct((t, d), f32), jax.ShapeDtypeStruct((1, d), f32)],
        compiler_params=pltpu.CompilerParams(dimension_semantics=("arbitrary",)),
    )(h, w, target)


CONV_ROWS = 256


def conv_fwd(src, col0, width, w, b, *, silu, name):
    t = src.shape[0]
    tc = _pick(math.gcd(width, col0), 768)
    assert col0 % tc == 0
    cb = col0 // tc
    r = CONV_ROWS

    def body(u_ref, w_ref, b_ref, *rest):
        ext = rest[-1]
        j = pl.program_id(1)

        @pl.when(j == 0)
        def _():
            ext[0:8, :] = jnp.zeros((8, tc), f32)

        @pl.when(j > 0)
        def _():
            ext[0:8, :] = ext[r:r + 8, :]

        ext[8:r + 8, :] = u_ref[...]
        v = ext[...]
        wv = w_ref[...]
        acc = b_ref[...] + wv[3:4, :] * v
        for s in (1, 2, 3):
            acc = acc + wv[3 - s:4 - s, :] * pltpu.roll(v, s, 0)
        pre = acc[8:, :]
        rest[0][...] = pre
        if silu:
            rest[1][...] = _silu(pre)

    tile = pl.BlockSpec((r, tc), lambda c, j: (j, c))
    n_out = 2 if silu else 1
    return pl.pallas_call(
        body, name=name, grid=(width // tc, t // r),
        in_specs=[pl.BlockSpec((r, tc), lambda c, j: (j, cb + c)), pl.BlockSpec((4, tc), lambda c, j: (0, c)),
                  pl.BlockSpec((1, tc), lambda c, j: (0, c))],
        out_specs=[tile] * n_out, out_shape=[jax.ShapeDtypeStruct((t, width), f32)] * n_out,
        scratch_shapes=[pltpu.VMEM((r + 8, tc), f32)],
        compiler_params=pltpu.CompilerParams(dimension_semantics=("parallel", "arbitrary")),
    )(src, w, b)


def conv_bwd(dpost, pre, src, col0, w, dst, *, name):
    t, width = dpost.shape
    tc = _pick(math.gcd(width, col0), 768)
    assert col0 % tc == 0
    cb = col0 // tc
    r = CONV_ROWS
    nt = t // r
    has_pre = pre is not None

    def body(*refs):
        refs = refs[1:]
        if has_pre:
            d_ref, p_ref, u_ref, w_ref, du_ref, dw_ref, db_ref, ext = refs
        else:
            d_ref, u_ref, w_ref, du_ref, dw_ref, db_ref, ext = refs
        j = pl.program_id(1)

        @pl.when(j == 0)
        def _():
            ext[r:r + 8, :] = jnp.zeros((8, tc), f32)
            dw_ref[...] = jnp.zeros_like(dw_ref)
            db_ref[...] = jnp.zeros_like(db_ref)

        @pl.when(j > 0)
        def _():
            ext[r:r + 8, :] = ext[0:8, :]

        dpre = d_ref[...]
        if has_pre:
            dpre = dpre * _dsilu(p_ref[...])
        ext[0:r, :] = dpre
        v = ext[...]
        wv = w_ref[...]
        uv = u_ref[...]
        du = wv[3:4, :] * dpre
        dw_ref[3:4, :] += jnp.sum(dpre * uv, axis=0, keepdims=True)
        for s in (1, 2, 3):
            sh = pltpu.roll(v, r + 8 - s, 0)[0:r, :]
            du = du + wv[3 - s:4 - s, :] * sh
            dw_ref[3 - s:4 - s, :] += jnp.sum(sh * uv, axis=0, keepdims=True)
        db_ref[...] += jnp.sum(dpre, axis=0, keepdims=True)
        du_ref[...] = du.astype(bf16)

    rev = pl.BlockSpec((r, tc), lambda c, j: (nt - 1 - j, c))
    win = pl.BlockSpec((r, tc), lambda c, j: (nt - 1 - j, cb + c))
    in_specs = [ANY, rev] + ([rev] if has_pre else []) + [win, pl.BlockSpec((4, tc), lambda c, j: (0, c))]
    ins = [dst, dpost] + ([pre] if has_pre else []) + [src, w]
    return pl.pallas_call(
        body, name=name, grid=(width // tc, nt), in_specs=in_specs,
        out_specs=[win, pl.BlockSpec((4, tc), lambda c, j: (0, c)), pl.BlockSpec((1, tc), lambda c, j: (0, c))],
        out_shape=[jax.ShapeDtypeStruct(dst.shape, bf16), jax.ShapeDtypeStruct((4, width), f32),
                   jax.ShapeDtypeStruct((1, width), f32)],
        input_output_aliases={0: 0},
        scratch_shapes=[pltpu.VMEM((r + 8, tc), f32)],
        compiler_params=pltpu.CompilerParams(dimension_semantics=("parallel", "arbitrary")),
    )(*ins)


def _ssd_common(xbc_ref, dtr_ref, dtrT_ref, par_row_ref, par_col_ref):
    l = SSM_CHUNK
    x = xbc_ref[:, 0:SSM_GROUP_W]
    bm = xbc_ref[:, SSM_GROUP_W:SSM_GROUP_W + SSM_D_STATE]
    cm = xbc_ref[:, SSM_GROUP_W + SSM_D_STATE:XBC_GROUP_W]
    par_row = par_row_ref[0]
    par_col = par_col_ref[0]
    bias_row, alog_row, d_row = par_row[0:1, :], par_row[1:2, :], par_row[2:3, :]
    bias_col, alog_col = par_col[:, 0:1], par_col[:, 1:2]
    dtr = dtr_ref[0]
    dt = _softplus(dtr + bias_row)
    dt_t = _softplus(dtrT_ref[0] + bias_col)
    a_row = -jnp.exp(alog_row)
    a_col = -jnp.exp(alog_col)
    li = lax.broadcasted_iota(jnp.int32, (l, l), 0)
    si = lax.broadcasted_iota(jnp.int32, (l, l), 1)
    tri = (li >= si).astype(f32)
    cs = _dot_01(tri, dt * a_row, NN, 1, 3)
    cs_t = _dot_01(dt_t * a_col, tri, NT, 0, 3)
    off = lax.broadcasted_iota(jnp.int32, (SSM_HPG, SSM_GROUP_W), 1) - SSM_HEAD_DIM * lax.broadcasted_iota(
        jnp.int32, (SSM_HPG, SSM_GROUP_W), 0)
    ex = ((off >= 0) & (off < SSM_HEAD_DIM)).astype(f32)
    cs_x = _dot_01(cs, ex, NN, 0, 3)
    cl_x = cs_x[l - 1:l, :]
    return dict(x=x, bm=bm, cm=cm, dtr=dtr, dt=dt, a_row=a_row, bias_row=bias_row, tri=tri, li=li, si=si, cs=cs,
                cs_t=cs_t, ex=ex, dt_x=_dot_01(dt, ex, NN, 0, 2), d_x=_dot_01(par_row, ex, NN, 0, 2)[2:3, :], e_x=jnp.exp(cs_x),
                el_x=jnp.exp(cl_x), dec_x=jnp.exp(cl_x - cs_x))


def ssd_fwd(xbc, dtr, dtr_t, par_row, par_col, *, name):
    t = xbc.shape[0]
    nc = t // SSM_CHUNK
    l, p = SSM_CHUNK, SSM_HEAD_DIM

    def body(xbc_ref, dtr_ref, dtrT_ref, prow_ref, pcol_ref, y_ref, sin_ref, state):
        @pl.when(pl.program_id(1) == 0)
        def _():
            state[...] = jnp.zeros_like(state)

        q = _ssd_common(xbc_ref, dtr_ref, dtrT_ref, prow_ref, pcol_ref)
        st = state[...]
        sin_ref[0] = st
        xd = q["x"] * q["dt_x"]
        g = _dot(q["cm"], q["bm"], NT)
        for r in range(SSM_HPG):
            sl = slice(r * p, (r + 1) * p)
            diff = q["cs"][:, r:r + 1] - q["cs_t"][r:r + 1, :]
            lm = jnp.where(q["li"] >= q["si"], jnp.exp(jnp.minimum(diff, 0.0)), 0.0)
            y_ref[:, sl] = _dot(g * lm, xd[:, sl], NN)
        y_ref[...] += q["e_x"] * _dot(q["cm"], st, NN) + q["d_x"] * q["x"]
        state[...] = q["el_x"] * st + _dot(q["bm"].T, xd * q["dec_x"], NN)

    return pl.pallas_call(
        body, name=name, grid=(SSM_GROUPS, nc),
        in_specs=[pl.BlockSpec((l, XBC_GROUP_W), lambda g, c: (c, g)),
                  pl.BlockSpec((1, l, SSM_HPG), lambda g, c: (g, c, 0)),
                  pl.BlockSpec((1, SSM_HPG, l), lambda g, c: (g, 0, c)),
                  pl.BlockSpec((1, 8, 8), lambda g, c: (g, 0, 0)),
                  pl.BlockSpec((1, 8, 8), lambda g, c: (g, 0, 0))],
        out_specs=[pl.BlockSpec((l, SSM_GROUP_W), lambda g, c: (c, g)),
                   pl.BlockSpec((1, SSM_D_STATE, SSM_GROUP_W), lambda g, c: (c, 0, g))],
        out_shape=[jax.ShapeDtypeStruct((t, SSM_D_INNER), f32),
                   jax.ShapeDtypeStruct((nc, SSM_D_STATE, SSM_D_INNER), f32)],
        scratch_shapes=[pltpu.VMEM((SSM_D_STATE, SSM_GROUP_W), f32)],
        compiler_params=pltpu.CompilerParams(dimension_semantics=("parallel", "arbitrary")),
    )(xbc, dtr, dtr_t, par_row, par_col)


def ssd_bwd(xbc, dtr, dtr_t, par_row, par_col, s_in, dy, *, name):
    t = xbc.shape[0]
    nc = t // SSM_CHUNK
    l, p = SSM_CHUNK, SSM_HEAD_DIM

    def body(xbc_ref, dtr_ref, dtrT_ref, prow_ref, pcol_ref, sin_ref, dy_ref, dxbc_ref, ddtr_ref, dpar_ref,
             dstate, yd_buf, dxd_buf):
        @pl.when(pl.program_id(1) == 0)
        def _():
            dstate[...] = jnp.zeros_like(dstate)
            dpar_ref[...] = jnp.zeros_like(dpar_ref)

        q = _ssd_common(xbc_ref, dtr_ref, dtrT_ref, prow_ref, pcol_ref)
        x, bm, cm, ex, li, si = q["x"], q["bm"], q["cm"], q["ex"], q["li"], q["si"]
        e_x, el_x, dec_x = q["e_x"], q["el_x"], q["dec_x"]
        st = sin_ref[0]
        dst = dstate[...]
        dy = dy_ref[...]
        xd = x * q["dt_x"]
        g = _dot(cm, bm, NT)
        dg = jnp.zeros((l, l), f32)
        for r in range(SSM_HPG):
            sl = slice(r * p, (r + 1) * p)
            diff = q["cs"][:, r:r + 1] - q["cs_t"][r:r + 1, :]
            lm = jnp.where(li >= si, jnp.exp(jnp.minimum(diff, 0.0)), 0.0)
            m = (g * lm).astype(bf16)
            xdh, dyh = xd[:, sl].astype(bf16), dy[:, sl].astype(bf16)
            yd_buf[:, sl] = _dot(m, xdh, NN)
            dxd_buf[:, sl] = _dot(m, dyh, TN)
            dg = dg + _dot(dyh, xdh, NT) * lm
        yd, dxd_diag = yd_buf[...], dxd_buf[...]
        yo = e_x * _dot(cm, st, NN)
        dz = e_x * dy
        wv = _dot(bm, dst, NN)
        xw = xd * wv * dec_x
        row8 = lax.broadcasted_iota(jnp.int32, (l, SSM_HPG), 0)
        dy_b, xd_b = dy.astype(bf16).astype(f32), xd.astype(bf16).astype(f32)
        dcs = _dot_01(dy_b * yd - xd_b * dxd_diag + dy * yo - xw, ex, NT, 0, 3)
        tail = jnp.sum(xw, axis=0, keepdims=True) + el_x * jnp.sum(dst * st, axis=0, keepdims=True)
        dcl = _dot_01(jnp.broadcast_to(tail, (SSM_HPG, SSM_GROUP_W)), ex, NT, 0, 3)[0:1, :]
        dcs = dcs + jnp.where(row8 == l - 1, dcl, 0.0)
        dda = _dot_01(q["tri"], dcs, TN, 1, 3)
        dxd = dxd_diag + dec_x * wv
        ddt = _dot_01(dxd * x, ex, NT, 0, 3) + dda * q["a_row"]
        ddtr = ddt * _sigmoid(q["dtr"] + q["bias_row"])
        ddtr_ref[0] = ddtr
        dd = _dot_01(jnp.broadcast_to(jnp.sum(dy * x, axis=0, keepdims=True), (SSM_HPG, SSM_GROUP_W)), ex, NT, 0, 2)[0:1, :]
        dpar_ref[0, 0:1, :] += jnp.sum(ddtr, axis=0, keepdims=True)
        dpar_ref[0, 1:2, :] += jnp.sum(dda * q["dt"], axis=0, keepdims=True) * q["a_row"]
        dpar_ref[0, 2:3, :] += dd
        dxbc_ref[:, 0:SSM_GROUP_W] = dxd * q["dt_x"] + q["d_x"] * dy
        dxbc_ref[:, SSM_GROUP_W:SSM_GROUP_W + SSM_D_STATE] = _dot(dg, cm, TN) + _dot(xd * dec_x, dst, NT)
        dxbc_ref[:, SSM_GROUP_W + SSM_D_STATE:XBC_GROUP_W] = _dot(dg, bm, NN) + _dot(dz, st, NT)
        dstate[...] = _dot(cm.T, dz, NN) + el_x * dst

    rc = lambda c: nc - 1 - c
    return pl.pallas_call(
        body, name=name, grid=(SSM_GROUPS, nc),
        in_specs=[pl.BlockSpec((l, XBC_GROUP_W), lambda g, c: (rc(c), g)),
                  pl.BlockSpec((1, l, SSM_HPG), lambda g, c: (g, rc(c), 0)),
                  pl.BlockSpec((1, SSM_HPG, l), lambda g, c: (g, 0, rc(c))),
                  pl.BlockSpec((1, 8, 8), lambda g, c: (g, 0, 0)),
                  pl.BlockSpec((1, 8, 8), lambda g, c: (g, 0, 0)),
                  pl.BlockSpec((1, SSM_D_STATE, SSM_GROUP_W), lambda g, c: (rc(c), 0, g)),
                  pl.BlockSpec((l, SSM_GROUP_W), lambda g, c: (rc(c), g))],
        out_specs=[pl.BlockSpec((l, XBC_GROUP_W), lambda g, c: (rc(c), g)),
                   pl.BlockSpec((1, l, SSM_HPG), lambda g, c: (g, rc(c), 0)),
                   pl.BlockSpec((1, 8, 8), lambda g, c: (g, 0, 0))],
        out_shape=[jax.ShapeDtypeStruct((t, SSM_CONV_DIM), f32),
                   jax.ShapeDtypeStruct((SSM_GROUPS, t, SSM_HPG), f32),
                   jax.ShapeDtypeStruct((SSM_GROUPS, 8, 8), f32)],
        scratch_shapes=[pltpu.VMEM((SSM_D_STATE, SSM_GROUP_W), f32), pltpu.VMEM((l, SSM_GROUP_W), f32),
                        pltpu.VMEM((l, SSM_GROUP_W), f32)],
        compiler_params=pltpu.CompilerParams(dimension_semantics=("parallel", "arbitrary")),
    )(xbc, dtr, dtr_t, par_row, par_col, s_in, dy)


def gnorm_fwd(y, proj, w, *, name):
    t = y.shape[0]
    tr = _pick(t, 512, 8)
    gw = SSM_GROUP_W
    zb = OFF_Z // gw

    def body(y_ref, z_ref, w_ref, o_ref):
        y2 = y_ref[...] * _silu(z_ref[...])
        r = lax.rsqrt(jnp.mean(y2 * y2, axis=-1, keepdims=True) + RMS_EPS)
        o_ref[...] = (y2 * r * w_ref[...]).astype(bf16)

    return pl.pallas_call(
        body, name=name, grid=(SSM_GROUPS, t // tr),
        in_specs=[pl.BlockSpec((tr, gw), lambda g, i: (i, g)), pl.BlockSpec((tr, gw), lambda g, i: (i, zb + g)),
                  pl.BlockSpec((1, gw), lambda g, i: (0, g))],
        out_specs=pl.BlockSpec((tr, gw), lambda g, i: (i, g)), out_shape=jax.ShapeDtypeStruct((t, SSM_D_INNER), bf16),
    )(y, proj, w)


def gnorm_bwd(y, proj, w, dout, dst, *, name):
    t = y.shape[0]
    tr = _pick(t, 512, 8)
    gw = SSM_GROUP_W
    zb = OFF_Z // gw

    def body(_, y_ref, z_ref, w_ref, do_ref, dy_ref, dz_ref, dw_ref):
        yv, zv = y_ref[...], z_ref[...]
        sz = _silu(zv)
        y2 = yv * sz
        dy2, dw = _rms_bwd_math(y2, w_ref[...], do_ref[...])
        dy_ref[...] = dy2 * sz
        dz_ref[...] = (dy2 * yv * _dsilu(zv)).astype(bf16)

        @pl.when(pl.program_id(1) == 0)
        def _():
            dw_ref[...] = jnp.zeros_like(dw_ref)

        dw_ref[...] += dw

    tile = pl.BlockSpec((tr, gw), lambda g, i: (i, g))
    vec = pl.BlockSpec((1, gw), lambda g, i: (0, g))
    return pl.pallas_call(
        body, name=name, grid=(SSM_GROUPS, t // tr),
        in_specs=[ANY, tile, pl.BlockSpec((tr, gw), lambda g, i: (i, zb + g)), vec, tile],
        out_specs=[tile, pl.BlockSpec((tr, gw), lambda g, i: (i, zb + g)), vec],
        out_shape=[jax.ShapeDtypeStruct((t, SSM_D_INNER), f32), jax.ShapeDtypeStruct(dst.shape, bf16),
                   jax.ShapeDtypeStruct((1, SSM_D_INNER), f32)],
        input_output_aliases={0: 1},
        compiler_params=pltpu.CompilerParams(dimension_semantics=("parallel", "arbitrary")),
    )(dst, y, proj, w, dout)


LRU_ROWS = 256


def _lru_gates(uv, wr_ref, wi_ref, br_ref, bi_ref, lam_ref):
    rg = _sigmoid(_dot(uv, wr_ref[0], NN) + br_ref[...])
    ig = _sigmoid(_dot(uv, wi_ref[0], NN) + bi_ref[...])
    sp = _softplus(-lam_ref[...])
    la = -LRU_C * rg * sp
    a = jnp.exp(la)
    s = jnp.sqrt(jnp.maximum(-_expm1(2.0 * la), 0.0))
    return rg, ig, sp, la, a, s


def lru_fwd(u, proj, w_r, b_r, w_i, b_i, lam, *, name):
    t = u.shape[0]
    r = LRU_ROWS
    lb = LRU_BLOCK
    yb = OFF_LY // lb

    def body(u_ref, y_ref, wr_ref, br_ref, wi_ref, bi_ref, lam_ref, h_ref, o_ref, carry):
        @pl.when(pl.program_id(1) == 0)
        def _():
            carry[...] = jnp.zeros_like(carry)

        uv = u_ref[...]
        _, ig, _, _, a, s = _lru_gates(uv, wr_ref, wi_ref, br_ref, bi_ref, lam_ref)
        b = s * ig * uv
        row = lax.broadcasted_iota(jnp.int32, (r, lb), 0)
        d = 1
        while d < r:
            keep = row >= d
            b = b + a * jnp.where(keep, pltpu.roll(b, d, 0), 0.0)
            a = a * jnp.where(keep, pltpu.roll(a, d, 0), 1.0)
            d *= 2
        h = b + a * carry[0:1, :]
        carry[0:1, :] = h[r - 1:r, :]
        h_ref[...] = h
        o_ref[...] = (h * _gelu(y_ref[...])).astype(bf16)

    tile = pl.BlockSpec((r, lb), lambda hb, j: (j, hb))
    vec = pl.BlockSpec((1, lb), lambda hb, j: (0, hb))
    wsp = pl.BlockSpec((1, lb, lb), lambda hb, j: (hb, 0, 0))
    return pl.pallas_call(
        body, name=name, grid=(LRU_BLOCKS, t // r),
        in_specs=[tile, pl.BlockSpec((r, lb), lambda hb, j: (j, yb + hb)), wsp, vec, wsp, vec, vec],
        out_specs=[tile, tile],
        out_shape=[jax.ShapeDtypeStruct((t, LRU_WIDTH), f32), jax.ShapeDtypeStruct((t, LRU_WIDTH), bf16)],
        scratch_shapes=[pltpu.VMEM((8, lb), f32)],
        compiler_params=pltpu.CompilerParams(dimension_semantics=("parallel", "arbitrary")),
    )(u, proj, w_r, b_r, w_i, b_i, lam)


def lru_bwd(u, proj, hseq, dout, w_r, b_r, w_i, b_i, lam, dst, *, name):
    t = u.shape[0]
    r = LRU_ROWS
    nt = t // r
    lb = LRU_BLOCK
    yb = OFF_LY // lb

    def body(_, u_ref, y_ref, h_ref, hp_ref, do_ref, wr_ref, br_ref, wi_ref, bi_ref, lam_ref,
             du_ref, dy_ref, dwr_ref, dwi_ref, dbr_ref, dbi_ref, dlam_ref, carry_dh, carry_a):
        j = pl.program_id(1)

        @pl.when(j == 0)
        def _():
            carry_dh[...] = jnp.zeros_like(carry_dh)
            carry_a[...] = jnp.zeros_like(carry_a)
            dwr_ref[...] = jnp.zeros_like(dwr_ref)
            dwi_ref[...] = jnp.zeros_like(dwi_ref)
            dbr_ref[...] = jnp.zeros_like(dbr_ref)
            dbi_ref[...] = jnp.zeros_like(dbi_ref)
            dlam_ref[...] = jnp.zeros_like(dlam_ref)

        uv = u_ref[...]
        yv = y_ref[...]
        hv = h_ref[...]
        dov = do_ref[...]
        rg, ig, sp, la, a, s = _lru_gates(uv, wr_ref, wi_ref, br_ref, bi_ref, lam_ref)
        dy_ref[...] = (dov * hv * _dgelu(yv)).astype(bf16)
        gq = dov * _gelu(yv)
        row = lax.broadcasted_iota(jnp.int32, (r, lb), 0)
        an = jnp.where(row < r - 1, pltpu.roll(a, r - 1, 0), carry_a[0:1, :])
        d = 1
        while d < r:
            keep = row < r - d
            gq = gq + an * jnp.where(keep, pltpu.roll(gq, r - d, 0), 0.0)
            an = an * jnp.where(keep, pltpu.roll(an, r - d, 0), 1.0)
            d *= 2
        dh = gq + an * carry_dh[0:1, :]
        carry_dh[0:1, :] = dh[0:1, :]
        carry_a[0:1, :] = a[0:1, :]
        first = jnp.where(j == nt - 1, 0.0, 1.0) * hp_ref[7:8, :]
        hprev = jnp.where(row >= 1, pltpu.roll(hv, 1, 0), first)
        da = dh * hprev
        iu = ig * uv
        e2 = jnp.exp(2.0 * la)
        dla = da * a - dh * iu * e2 / jnp.maximum(s, 1e-30)
        drp = dla * (-LRU_C * sp) * rg * (1.0 - rg)
        dip = dh * s * uv * ig * (1.0 - ig)
        dlam_ref[...] += jnp.sum(dla * (LRU_C * rg) * _sigmoid(-lam_ref[...]), axis=0, keepdims=True)
        du_ref[...] = dh * s * ig + _dot(drp, wr_ref[0], NT) + _dot(dip, wi_ref[0], NT)
        dwr_ref[0] += _dot(uv, drp, TN)
        dwi_ref[0] += _dot(uv, dip, TN)
        dbr_ref[...] += jnp.sum(drp, axis=0, keepdims=True)
        dbi_ref[...] += jnp.sum(dip, axis=0, keepdims=True)

    rj = lambda j: nt - 1 - j
    tile = pl.BlockSpec((r, lb), lambda hb, j: (rj(j), hb))
    vec = pl.BlockSpec((1, lb), lambda hb, j: (0, hb))
    wsp = pl.BlockSpec((1, lb, lb), lambda hb, j: (hb, 0, 0))
    hprev_spec = pl.BlockSpec((8, lb), lambda hb, j: (jnp.maximum(rj(j) * (r // 8) - 1, 0), hb))
    ywin = pl.BlockSpec((r, lb), lambda hb, j: (rj(j), yb + hb))
    return pl.pallas_call(
        body, name=name, grid=(LRU_BLOCKS, nt),
        in_specs=[ANY, tile, ywin, tile, hprev_spec, tile, wsp, vec, wsp, vec, vec],
        out_specs=[tile, ywin, wsp, wsp, vec, vec, vec],
        out_shape=[jax.ShapeDtypeStruct((t, LRU_WIDTH), f32), jax.ShapeDtypeStruct(dst.shape, bf16),
                   jax.ShapeDtypeStruct((LRU_BLOCKS, lb, lb), f32), jax.ShapeDtypeStruct((LRU_BLOCKS, lb, lb), f32),
                   jax.ShapeDtypeStruct((1, LRU_WIDTH), f32), jax.ShapeDtypeStruct((1, LRU_WIDTH), f32),
                   jax.ShapeDtypeStruct((1, LRU_WIDTH), f32)],
        input_output_aliases={0: 1},
        scratch_shapes=[pltpu.VMEM((8, lb), f32), pltpu.VMEM((8, lb), f32)],
        compiler_params=pltpu.CompilerParams(dimension_semantics=("parallel", "arbitrary")),
    )(dst, u, proj, hseq, hseq, dout, w_r, b_r, w_i, b_i, lam)


def merge_fwd(proj, bg, y_ssm, y_lru, *, name):
    t, d = y_ssm.shape
    tr = _pick(t, 256, 8)
    gb = OFF_GATES // d

    def body(gs_ref, gl_ref, bs_ref, bl_ref, ys_ref, yl_ref, o_ref):
        gs = _sigmoid(gs_ref[...] + bs_ref[...])
        gl = _sigmoid(gl_ref[...] + bl_ref[...])
        o_ref[...] = (gs * ys_ref[...] + gl * yl_ref[...]).astype(bf16)

    row = pl.BlockSpec((tr, d), lambda i: (i, 0))
    return pl.pallas_call(
        body, name=name, grid=(t // tr,),
        in_specs=[pl.BlockSpec((tr, d), lambda i: (i, gb)), pl.BlockSpec((tr, d), lambda i: (i, gb + 1)),
                  pl.BlockSpec((1, d), lambda i: (0, 0)), pl.BlockSpec((1, d), lambda i: (0, 1)), row, row],
        out_specs=row, out_shape=jax.ShapeDtypeStruct((t, d), bf16),
    )(proj, proj, bg, bg, y_ssm, y_lru)


def merge_bwd(proj, bg, y_ssm, y_lru, dmix, *, name):
    t, d = y_ssm.shape
    tr = _pick(t, 256, 8)
    gb = OFF_GATES // d

    def body(gs_ref, gl_ref, bs_ref, bl_ref, ys_ref, yl_ref, dm_ref, dg_ref, dys_ref, dyl_ref, dbg_ref):
        gs = _sigmoid(gs_ref[...] + bs_ref[...])
        gl = _sigmoid(gl_ref[...] + bl_ref[...])
        dm = dm_ref[...]
        dys_ref[...] = (dm * gs).astype(bf16)
        dyl_ref[...] = (dm * gl).astype(bf16)
        dgs = dm * ys_ref[...] * gs * (1.0 - gs)
        dgl = dm * yl_ref[...] * gl * (1.0 - gl)
        dg_ref[:, 0:d] = dgs.astype(bf16)
        dg_ref[:, d:2 * d] = dgl.astype(bf16)

        @pl.when(pl.program_id(0) == 0)
        def _():
            dbg_ref[...] = jnp.zeros_like(dbg_ref)

        dbg_ref[:, 0:d] += jnp.sum(dgs, axis=0, keepdims=True)
        dbg_ref[:, d:2 * d] += jnp.sum(dgl, axis=0, keepdims=True)

    row = pl.BlockSpec((tr, d), lambda i: (i, 0))
    return pl.pallas_call(
        body, name=name, grid=(t // tr,),
        in_specs=[pl.BlockSpec((tr, d), lambda i: (i, gb)), pl.BlockSpec((tr, d), lambda i: (i, gb + 1)),
                  pl.BlockSpec((1, d), lambda i: (0, 0)), pl.BlockSpec((1, d), lambda i: (0, 1)), row, row, row],
        out_specs=[pl.BlockSpec((tr, 2 * d), lambda i: (i, OFF_GATES // (2 * d))), row, row,
                   pl.BlockSpec((1, 2 * d), lambda i: (0, 0))],
        out_shape=[jax.ShapeDtypeStruct((t, PROJ_W), bf16), jax.ShapeDtypeStruct((t, d), bf16),
                   jax.ShapeDtypeStruct((t, d), bf16), jax.ShapeDtypeStruct((1, 2 * d), f32)],
        compiler_params=pltpu.CompilerParams(dimension_semantics=("arbitrary",)),
    )(proj, proj, bg, bg, y_ssm, y_lru, dmix)


def swiglu_fwd(ff, *, name):
    t = ff.shape[0]
    hd = FFN_HIDDEN
    tr = _pick(t, 128, 8)

    def body(f_ref, o_ref):
        o_ref[...] = (_silu(f_ref[:, 0:hd]) * f_ref[:, hd:2 * hd]).astype(bf16)

    return pl.pallas_call(
        body, name=name, grid=(t // tr,), in_specs=[pl.BlockSpec((tr, 2 * hd), lambda i: (i, 0))],
        out_specs=pl.BlockSpec((tr, hd), lambda i: (i, 0)), out_shape=jax.ShapeDtypeStruct((t, hd), bf16),
    )(ff)


def swiglu_bwd(ff, dact, *, name):
    t = ff.shape[0]
    hd = FFN_HIDDEN
    tr = _pick(t, 128, 8)

    def body(f_ref, d_ref, o_ref):
        gate, up, dv = f_ref[:, 0:hd], f_ref[:, hd:2 * hd], d_ref[...]
        o_ref[:, 0:hd] = (dv * up * _dsilu(gate)).astype(bf16)
        o_ref[:, hd:2 * hd] = (dv * _silu(gate)).astype(bf16)

    return pl.pallas_call(
        body, name=name, grid=(t // tr,),
        in_specs=[pl.BlockSpec((tr, 2 * hd), lambda i: (i, 0)), pl.BlockSpec((tr, hd), lambda i: (i, 0))],
        out_specs=pl.BlockSpec((tr, 2 * hd), lambda i: (i, 0)), out_shape=jax.ShapeDtypeStruct((t, 2 * hd), bf16),
    )(ff, dact)


def _adam_math(w, g, m, v):
    m = ADAM_B1 * m + (1.0 - ADAM_B1) * g
    v = ADAM_B2 * v + (1.0 - ADAM_B2) * (g * g)
    m_hat = m / (1.0 - ADAM_B1 ** ADAM_STEP)
    v_hat = v / (1.0 - ADAM_B2 ** ADAM_STEP)
    delta = -ADAM_LR * (m_hat / (jnp.sqrt(v_hat) + ADAM_EPS) + ADAM_WD * w)
    return delta, m, v


def _row_tile(rows, cols):
    cap = max(8, (1 << 18) // cols)
    return _pick(rows, cap, 8) if rows % 8 == 0 else rows


def adamw(w, g, m, v, *, name):
    rows, cols = w.shape
    tr = _row_tile(rows, cols)

    def body(w_ref, g_ref, m_ref, v_ref, d_ref, nm_ref, nv_ref):
        d, nm, nv = _adam_math(w_ref[...], g_ref[...], m_ref[...], v_ref[...])
        d_ref[...] = d
        nm_ref[...] = nm
        nv_ref[...] = nv

    tile = pl.BlockSpec((tr, cols), lambda i: (i, 0))
    return pl.pallas_call(
        body, name=name, grid=(rows // tr,), in_specs=[tile] * 4, out_specs=[tile] * 3,
        out_shape=[jax.ShapeDtypeStruct((rows, cols), f32)] * 3,
    )(w, g, m, v)


def pair_add(dw, rbuf, idx, *, name):
    n, rows, cols = dw.shape
    hr = rows // 2
    tr = _row_tile(hr, cols)
    nrt = hr // tr

    def body(idx_ref, a_ref, b_ref, o_ref, own_ref):
        s = a_ref[...] + b_ref[...]
        o_ref[...] = s.astype(bf16)

        @pl.when(pl.program_id(1) == idx_ref[0])
        def _():
            own_ref[...] = s[0]

    return pl.pallas_call(
        body, name=name,
        grid_spec=pltpu.PrefetchScalarGridSpec(
            num_scalar_prefetch=1, grid=(nrt, n),
            in_specs=[pl.BlockSpec((1, tr, cols), lambda i, k, idx: (k, idx[1] * nrt + i, 0)),
                      pl.BlockSpec((1, tr, cols), lambda i, k, idx: (k, i, 0))],
            out_specs=[pl.BlockSpec((1, tr, cols), lambda i, k, idx: (k, i, 0)),
                       pl.BlockSpec((tr, cols), lambda i, k, idx: (i, 0))]),
        out_shape=[jax.ShapeDtypeStruct((n, hr, cols), bf16), jax.ShapeDtypeStruct((hr, cols), f32)],
    )(idx, dw, rbuf)


def chip_sum(own, rbuf, idx, *, name):
    hr, cols = own.shape
    tr = _row_tile(hr, cols)
    nrt = hr // tr

    def body(idx_ref, a_ref, b_ref, o_ref):
        o_ref[...] = ((a_ref[...] + b_ref[0].astype(f32)) + b_ref[1].astype(f32)) + b_ref[2].astype(f32)

    return pl.pallas_call(
        body, name=name,
        grid_spec=pltpu.PrefetchScalarGridSpec(
            num_scalar_prefetch=1, grid=(nrt,),
            in_specs=[pl.BlockSpec((tr, cols), lambda i, idx: (i, 0)),
                      pl.BlockSpec((3, tr, cols), lambda i, idx: (0, i, 0))],
            out_specs=pl.BlockSpec((tr, cols), lambda i, idx: (idx[1] * nrt + i, 0))),
        out_shape=jax.ShapeDtypeStruct((2 * hr, cols), f32),
    )(idx, own, rbuf)


def sum8(rbuf, *, name):
    n, rows, cols = rbuf.shape
    tr = _row_tile(rows, cols * n)

    def body(a_ref, o_ref):
        acc = a_ref[0]
        for k in range(1, n):
            acc = acc + a_ref[k]
        o_ref[...] = acc

    return pl.pallas_call(
        body, name=name, grid=(rows // tr,), in_specs=[pl.BlockSpec((n, tr, cols), lambda i: (0, i, 0))],
        out_specs=pl.BlockSpec((tr, cols), lambda i: (i, 0)), out_shape=jax.ShapeDtypeStruct((rows, cols), f32),
    )(rbuf)


def _coords():
    return lax.axis_index("x"), lax.axis_index("y"), lax.axis_index("c")


def _other_chips(x, y):
    return [(1 - x, y), (x, 1 - y), (1 - x, 1 - y)]


def gather_weights(shards, *, name):
    n = len(shards)
    halves = [s.shape[0] // 2 for s in shards]

    def body(*refs):
        ins, outs = refs[:n], refs[n:2 * n]
        send1, recv1, send2, recv2 = refs[2 * n:]
        x, y, c = _coords()
        me = 2 * x + y
        chips = _other_chips(x, y)
        sibling = (x, y, 1 - c)

        def half(i, k, hc):
            return outs[i].at[k, pl.ds(hc * halves[i], halves[i]), :]

        def ici(i, j):
            return pltpu.make_async_remote_copy(
                src_ref=ins[i].at[pl.ds(c * halves[i], halves[i]), :], dst_ref=half(i, me, c),
                send_sem=send1.at[i, j], recv_sem=recv1.at[i, j], device_id=(*chips[j], c), device_id_type=MESH)

        def landed(i, j):
            kj = 2 * chips[j][0] + chips[j][1]
            return pltpu.make_async_remote_copy(
                src_ref=half(i, kj, c), dst_ref=half(i, kj, c),
                send_sem=send2.at[i, j], recv_sem=recv1.at[i, j], device_id=sibling, device_id_type=MESH)

        def from_sibling(i, j):
            kj = 2 * chips[j][0] + chips[j][1]
            return pltpu.make_async_remote_copy(
                src_ref=half(i, kj, 1 - c), dst_ref=half(i, kj, 1 - c),
                send_sem=send2.at[i, j], recv_sem=recv2.at[i, j], device_id=sibling, device_id_type=MESH)

        def d2d(i, j):
            kj = 2 * chips[j][0] + chips[j][1]
            return pltpu.make_async_remote_copy(
                src_ref=half(i, kj, c), dst_ref=half(i, kj, c),
                send_sem=send2.at[i, j], recv_sem=recv2.at[i, j], device_id=sibling, device_id_type=MESH)

        for j in range(3):
            for i in range(n):
                ici(i, j).start()
        for j in range(3):
            for i in range(n):
                landed(i, j).wait_recv()
                d2d(i, j).start()
        for j in range(3):
            for i in range(n):
                from_sibling(i, j).wait_recv()
        for j in range(3):
            for i in range(n):
                ici(i, j).wait_send()
                d2d(i, j).wait_send()

    return pl.pallas_call(
        body, name=name, in_specs=[ANY] * n, out_specs=[ANY] * n,
        out_shape=[jax.ShapeDtypeStruct((N_CHIPS,) + s.shape, s.dtype) for s in shards],
        scratch_shapes=[pltpu.SemaphoreType.DMA((n, 3))] * 4,
    )(*shards)


def pair_exchange(grads, *, name):
    n = len(grads)
    halves = [g.shape[1] // 2 for g in grads]

    def body(*refs):
        ins, outs = refs[:n], refs[n:2 * n]
        send, recv = refs[2 * n:]
        x, y, c = _coords()
        cps = [pltpu.make_async_remote_copy(
            src_ref=ins[i].at[:, pl.ds((1 - c) * halves[i], halves[i]), :], dst_ref=outs[i],
            send_sem=send.at[i], recv_sem=recv.at[i], device_id=(x, y, 1 - c), device_id_type=MESH) for i in range(n)]
        for cp in cps:
            cp.start()
        for cp in cps:
            cp.wait()

    return pl.pallas_call(
        body, name=name, in_specs=[ANY] * n, out_specs=[ANY] * n,
        out_shape=[jax.ShapeDtypeStruct((N_CHIPS, g.shape[1] // 2, g.shape[2]), g.dtype) for g in grads],
        scratch_shapes=[pltpu.SemaphoreType.DMA((n,))] * 2,
    )(*grads)


def pair_gather(bufs, *, name):
    n = len(bufs)

    def body(*refs):
        ins, outs = refs[:n], refs[n:2 * n]
        send, recv = refs[2 * n:]
        x, y, c = _coords()
        cps = []
        for i in range(n):
            hr = ins[i].shape[0] // 2
            cps.append(pltpu.make_async_remote_copy(
                src_ref=ins[i].at[pl.ds(c * hr, hr), :], dst_ref=outs[i].at[pl.ds(c * hr, hr), :],
                send_sem=send.at[i], recv_sem=recv.at[i], device_id=(x, y, 1 - c), device_id_type=MESH))
        for cp in cps:
            cp.start()
        for i in range(n):
            hr = ins[i].shape[0] // 2
            pltpu.make_async_remote_copy(
                src_ref=ins[i].at[pl.ds((1 - c) * hr, hr), :], dst_ref=outs[i].at[pl.ds((1 - c) * hr, hr), :],
                send_sem=send.at[i], recv_sem=recv.at[i], device_id=(x, y, 1 - c), device_id_type=MESH).wait_recv()
        for cp in cps:
            cp.wait_send()

    return pl.pallas_call(
        body, name=name, in_specs=[ANY] * n, out_specs=[ANY] * n,
        out_shape=[jax.ShapeDtypeStruct(b.shape, b.dtype) for b in bufs],
        input_output_aliases={i: i for i in range(n)},
        scratch_shapes=[pltpu.SemaphoreType.DMA((n,))] * 2,
    )(*bufs)


def all_exchange(buf, *, name):
    rows, cols = buf.shape

    def body(in_ref, out_ref, send, recv):
        x, y, c = _coords()
        me = 4 * x + 2 * y + c
        cps = []
        for d in range(1, 8):
            px = 1 - x if d & 4 else x
            py = 1 - y if d & 2 else y
            pc = 1 - c if d & 1 else c
            cps.append(pltpu.make_async_remote_copy(
                src_ref=in_ref, dst_ref=out_ref.at[me], send_sem=send.at[d - 1], recv_sem=recv.at[d - 1],
                device_id=(px, py, pc), device_id_type=MESH))
        for cp in cps:
            cp.start()
        for d in range(1, 8):
            px = 1 - x if d & 4 else x
            py = 1 - y if d & 2 else y
            pc = 1 - c if d & 1 else c
            src = 4 * px + 2 * py + pc
            pltpu.make_async_remote_copy(
                src_ref=in_ref, dst_ref=out_ref.at[src], send_sem=send.at[d - 1], recv_sem=recv.at[d - 1],
                device_id=(px, py, pc), device_id_type=MESH).wait_recv()
        for cp in cps:
            cp.wait_send()

    return pl.pallas_call(
        body, name=name, in_specs=[ANY], out_specs=ANY,
        out_shape=jax.ShapeDtypeStruct((8, rows, cols), buf.dtype),
        scratch_shapes=[pltpu.SemaphoreType.DMA((7,)), pltpu.SemaphoreType.DMA((7,))],
    )(buf)


HBM = pl.BlockSpec(memory_space=pltpu.HBM)
SEM = pl.BlockSpec(memory_space=pltpu.SEMAPHORE)
EFFECT = pltpu.SideEffectType.DATAFLOW_SIDE_EFFECTING


def split_start(arrays, after, copies, sem_shape, *, name):
    na = len(arrays)

    def body(*refs):
        for cp in copies(refs[:na], refs[na + 1], refs[na + 2]):
            cp.start()
        refs[-1][...] = jnp.zeros((8, 128), f32)

    outs = pl.pallas_call(
        body, name=name,
        out_shape=(pltpu.SemaphoreType.DMA(sem_shape), pltpu.SemaphoreType.DMA(sem_shape),
                   *[pltpu.HBM(a.shape, a.dtype) for a in arrays], jax.ShapeDtypeStruct((8, 128), f32)),
        in_specs=[HBM] * na + [ANY], out_specs=(SEM, SEM, *[HBM] * na, pl.BlockSpec(memory_space=pltpu.VMEM)),
        input_output_aliases={i: 2 + i for i in range(na)},
        compiler_params=pltpu.CompilerParams(has_side_effects=EFFECT),
    )(*[pltpu.with_memory_space_constraint(a, pltpu.HBM) for a in arrays], after)
    return outs[0], outs[1], list(outs[2:2 + na]), outs[-1]


def split_wait(send, recv, arrays, after, copies, *, name):
    na = len(arrays)

    def body(*refs):
        for cp in copies(refs[:na], refs[na], refs[na + 1]):
            cp.wait_send()
            cp.wait_recv()

    outs = pl.pallas_call(
        body, name=name, out_shape=tuple(pltpu.HBM(a.shape, a.dtype) for a in arrays),
        in_specs=[HBM] * na + [SEM, SEM, ANY], out_specs=tuple([HBM] * na),
        input_output_aliases={i: i for i in range(na)},
        compiler_params=pltpu.CompilerParams(has_side_effects=EFFECT),
    )(*arrays, send, recv, after)
    return list(outs)


def gather_copies(n):
    def copies(refs, send, recv):
        x, y, c = _coords()
        me = 2 * x + y
        chips = _other_chips(x, y)
        return [pltpu.make_async_remote_copy(
            src_ref=refs[i], dst_ref=refs[n + i].at[me], send_sem=send.at[3 * i + j], recv_sem=recv.at[3 * i + j],
            device_id=(*chips[j], c), device_id_type=MESH) for j in range(3) for i in range(n)]
    return copies


def all_copies():
    def copies(refs, send, recv):
        x, y, c = _coords()
        me = 4 * x + 2 * y + c
        cps = []
        for d in range(1, 8):
            peer = (1 - x if d & 4 else x, 1 - y if d & 2 else y, 1 - c if d & 1 else c)
            cps.append(pltpu.make_async_remote_copy(
                src_ref=refs[0], dst_ref=refs[1].at[me], send_sem=send.at[d - 1], recv_sem=recv.at[d - 1],
                device_id=peer, device_id_type=MESH))
        return cps
    return copies


def reduce_copies(n):
    def copies(refs, send, recv):
        x, y, c = _coords()
        chips = _other_chips(x, y)
        return [pltpu.make_async_remote_copy(
            src_ref=refs[i].at[2 * chips[j][0] + chips[j][1]], dst_ref=refs[n + i].at[j],
            send_sem=send.at[3 * i + j], recv_sem=recv.at[3 * i + j], device_id=(*chips[j], c), device_id_type=MESH)
            for j in range(3) for i in range(n)]
    return copies


def _pack(arrs):
    flat = []
    for a in arrs:
        v = a.reshape(-1).astype(f32)
        pad = (-v.shape[0]) % 128
        flat.append(jnp.pad(v, (0, pad)) if pad else v)
    v = jnp.concatenate(flat)
    rows = v.shape[0] // 128
    pad_rows = (-rows) % 256
    v = v.reshape(rows, 128)
    return jnp.pad(v, ((0, pad_rows), (0, 0))) if pad_rows else v


def _unpack(buf, shapes):
    out, row = [], 0
    for s in shapes:
        size = math.prod(s)
        rows = -(-size // 128)
        out.append(buf[row:row + rows].reshape(-1)[:size].reshape(s))
        row += rows
    return out


def _perm_in_cols(w):
    gates, z = w[..., 0:2048], w[..., 2048:4096]
    xbc = w[..., 4096:7168]
    dt, lx, ly = w[..., 7168:7200], w[..., 7200:8480], w[..., 8480:9760]
    pad = jnp.zeros(w.shape[:-1] + (DT_PAD_W - SSM_HEADS,), w.dtype)
    return jnp.concatenate([gates, z, lx, ly, dt, pad, _perm_xbc_cols(xbc)], axis=-1)


def _perm_xbc_cols(w):
    parts = []
    for g in range(SSM_GROUPS):
        parts += [w[..., g * 512:(g + 1) * 512], w[..., 2048 + g * 128:2048 + (g + 1) * 128],
                  w[..., 2560 + g * 128:2560 + (g + 1) * 128]]
    return jnp.concatenate(parts, axis=-1)


def _unperm_xbc_cols(w):
    xs = [w[..., g * XBC_GROUP_W:g * XBC_GROUP_W + 512] for g in range(SSM_GROUPS)]
    bs = [w[..., g * XBC_GROUP_W + 512:g * XBC_GROUP_W + 640] for g in range(SSM_GROUPS)]
    cs = [w[..., g * XBC_GROUP_W + 640:(g + 1) * XBC_GROUP_W] for g in range(SSM_GROUPS)]
    return jnp.concatenate(xs + bs + cs, axis=-1)


def _unperm_in_cols(w):
    xbc = _unperm_xbc_cols(w[..., OFF_XBC:OFF_XBC + 3072])
    return jnp.concatenate([w[..., OFF_GATES:OFF_GATES + 2048], w[..., OFF_Z:OFF_Z + 2048], xbc,
                            w[..., OFF_DT:OFF_DT + 32], w[..., OFF_LX:OFF_LX + 1280], w[..., OFF_LY:OFF_LY + 1280]], axis=-1)


def _col_shards(w, n=N_CHIPS):
    r, c = w.shape
    return jnp.transpose(w.reshape(r, n, c // n), (1, 0, 2))


def _from_col_shards(w):
    n, r, c = w.shape
    return jnp.transpose(w, (1, 0, 2)).reshape(r, n * c)


def kernel(x, norm1_w, w_in, b_branch_gate, ssm_conv_w, ssm_conv_b, ssm_dt_bias, ssm_a_log, ssm_d, ssm_norm_w, w_out_ssm, lru_conv_w, lru_conv_b, lru_w_r, lru_b_r, lru_w_i, lru_b_i, lru_lambda, w_out_lru, w_out, norm2_w, w_ffn_in, w_ffn_out, norm_f_w, loss_target, m_norm1_w, m_w_in, m_b_branch_gate, m_ssm_conv_w, m_ssm_conv_b, m_ssm_dt_bias, m_ssm_a_log, m_ssm_d, m_ssm_norm_w, m_w_out_ssm, m_lru_conv_w, m_lru_conv_b, m_lru_w_r, m_lru_b_r, m_lru_w_i, m_lru_b_i, m_lru_lambda, m_w_out_lru, m_w_out, m_norm2_w, m_w_ffn_in, m_w_ffn_out, m_norm_f_w, v_norm1_w, v_w_in, v_b_branch_gate, v_ssm_conv_w, v_ssm_conv_b, v_ssm_dt_bias, v_ssm_a_log, v_ssm_d, v_ssm_norm_w, v_w_out_ssm, v_lru_conv_w, v_lru_conv_b, v_lru_w_r, v_lru_b_r, v_lru_w_i, v_lru_b_i, v_lru_lambda, v_w_out_lru, v_w_out, v_norm2_w, v_w_ffn_in, v_w_ffn_out, v_norm_f_w):
    xi, yi, ci = lax.axis_index("x"), lax.axis_index("y"), lax.axis_index("c")
    me = 2 * xi + yi
    idx = jnp.stack([me, ci]).astype(jnp.int32)
    x2 = x[0]
    tgt = loss_target[0]

    big_names = ["w_in", "w_out_ssm", "w_out_lru", "w_out", "w_ffn_in", "w_ffn_out"]
    big_w = dict(w_in=w_in[0], w_out_ssm=w_out_ssm[0], w_out_lru=w_out_lru[0], w_out=w_out[0], w_ffn_in=w_ffn_in[0],
                 w_ffn_out=w_ffn_out[0])
    big_m = dict(w_in=m_w_in[0], w_out_ssm=m_w_out_ssm[0], w_out_lru=m_w_out_lru[0], w_out=m_w_out[0],
                 w_ffn_in=m_w_ffn_in[0], w_ffn_out=m_w_ffn_out[0])
    big_v = dict(w_in=v_w_in[0], w_out_ssm=v_w_out_ssm[0], w_out_lru=v_w_out_lru[0], w_out=v_w_out[0],
                 w_ffn_in=v_w_ffn_in[0], w_ffn_out=v_w_ffn_out[0])
    conv_pad = jnp.zeros((16, 768), f32).at[0:4, :].set(ssm_conv_w[0]).at[8:12, 0:320].set(lru_conv_w[0])
    mine = [big_w["w_in"].astype(bf16), conv_pad]
    gathered = gather_weights(mine, name="gather_weights")
    g_in, g_conv = [lax.dynamic_update_index_in_dim(g, s, me, 0) for g, s in zip(gathered, mine)]
    w_in_p = _perm_in_cols(_from_col_shards(g_in))
    late_names = big_names[1:]
    late = [big_w[k].astype(bf16) for k in late_names]
    late_lands = [lax.empty((N_CHIPS,) + s.shape, bf16) for s in late]
    g_send, g_recv, g_arrays, g_token = split_start(late + late_lands, g_conv, gather_copies(5), (15,),
                                                    name="gather_late_start")
    ssm_cw_full = _from_col_shards(g_conv[:, 0:4, :])
    lru_cw_full = _from_col_shards(g_conv[:, 8:12, 0:320])
    ssm_cw_p = _perm_xbc_cols(ssm_cw_full)
    ssm_cb_p = _perm_xbc_cols(ssm_conv_b)

    par = jnp.stack([ssm_dt_bias[0], ssm_a_log[0], ssm_d[0]], axis=0).reshape(3, SSM_GROUPS, SSM_HPG)
    par_row = jnp.zeros((SSM_GROUPS, 8, 8), f32).at[:, 0:3, :].set(jnp.transpose(par, (1, 0, 2)))
    par_col = jnp.transpose(par_row, (0, 2, 1))

    hn1 = rms_fwd(x2, norm1_w + g_token[0:1, 0:1], name="rms1_fwd")
    proj = mm(hn1, w_in_p, "nn", name="in_proj")
    t = x2.shape[0]
    dtr = jnp.transpose(proj[:, OFF_DT:OFF_DT + 32].reshape(t, SSM_GROUPS, SSM_HPG), (1, 0, 2))
    dtr_t = jnp.transpose(dtr, (0, 2, 1))
    xbc_pre, xbc_post = conv_fwd(proj, OFF_XBC, SSM_CONV_DIM, ssm_cw_p, ssm_cb_p, silu=True, name="ssm_conv_fwd")
    y_ssd, s_in = ssd_fwd(xbc_post, dtr, dtr_t, par_row, par_col, name="ssd_fwd")
    yn = gnorm_fwd(y_ssd, proj, ssm_norm_w, name="gnorm_fwd")
    g_arrays = split_wait(g_send, g_recv, g_arrays, yn, gather_copies(5), name="gather_late_wait")
    g_out_ssm, g_out_lru, g_out, g_ffn_in, g_ffn_out = [
        lax.dynamic_update_index_in_dim(g, s, me, 0) for g, s in zip(g_arrays[5:], late)]
    w_out_ssm_f = g_out_ssm.reshape(SSM_D_INNER, D_MODEL)
    w_out_lru_f = g_out_lru.reshape(LRU_WIDTH, D_MODEL)
    w_out_f = g_out.reshape(D_MODEL, D_MODEL)
    w_ffn_in_f = _from_col_shards(g_ffn_in)
    w_ffn_out_f = g_ffn_out.reshape(FFN_HIDDEN, D_MODEL)
    y_ssm = mm(yn, w_out_ssm_f, "nn", name="out_ssm")
    (u_lru,) = conv_fwd(proj, OFF_LX, LRU_WIDTH, lru_cw_full, lru_conv_b, silu=False, name="lru_conv_fwd")
    h_lru, o_lru = lru_fwd(u_lru, proj, lru_w_r[0], lru_b_r, lru_w_i[0], lru_b_i, lru_lambda, name="lru_fwd")
    y_lru = mm(o_lru, w_out_lru_f, "nn", name="out_lru")
    mix = merge_fwd(proj, b_branch_gate, y_ssm, y_lru, name="merge_fwd")
    h1 = mm(mix, w_out_f, "nn", add=x2, name="out_proj")
    hn2 = rms_fwd(h1, norm2_w, name="rms2_fwd")
    ff = mm(hn2, w_ffn_in_f, "nn", name="ffn_in")
    act = swiglu_fwd(ff, name="swiglu_fwd")
    h2 = mm(act, w_ffn_out_f, "nn", add=h1, name="ffn_out")
    loss_tile, dh2, d_norm_f = loss_head(h2, norm_f_w.reshape(1, D_MODEL), tgt, name="loss_head")
    loss = lax.psum(loss_tile[0, 0], ("x", "y", "c"))

    d_w_ffn_out = mm(act, dh2, "tn", name="d_w_ffn_out")
    dact = mm(dh2, w_ffn_out_f, "nt", name="d_act")
    dff = swiglu_bwd(ff, dact, name="swiglu_bwd")
    d_w_ffn_in = mm(hn2, dff, "tn", name="d_w_ffn_in")
    dhn2 = mm(dff, w_ffn_in_f, "nt", name="d_hn2")
    dh1, d_norm2 = rms_bwd(h1, norm2_w, dhn2, dh2, name="rms2_bwd")
    d_w_out = mm(mix, dh1, "tn", name="d_w_out")
    dmix = mm(dh1, w_out_f, "nt", name="d_mix")
    dproj, dy_ssm, dy_lru, d_bg = merge_bwd(proj, b_branch_gate, y_ssm, y_lru, dmix, name="merge_bwd")
    d_w_out_ssm = mm(yn, dy_ssm, "tn", name="d_w_out_ssm")
    d_w_out_lru = mm(o_lru, dy_lru, "tn", name="d_w_out_lru")
    early_g = [d_w_out_ssm.reshape(N_CHIPS, 512, D_MODEL), d_w_out_lru.reshape(N_CHIPS, 320, D_MODEL),
               d_w_out.reshape(N_CHIPS, 256, D_MODEL), _col_shards(d_w_ffn_in), d_w_ffn_out.reshape(N_CHIPS, 704, D_MODEL)]
    e_sib = pair_exchange(early_g, name="pair_exchange_early")
    e_pairs = [pair_add(g, rb, idx, name="pair_add_" + k) for g, rb, k in zip(early_g, e_sib, late_names)]
    e_lands = [lax.empty((3,) + p[0].shape[1:], bf16) for p in e_pairs]
    e_send, e_recv, e_arrays, e_token = split_start([p[0] for p in e_pairs] + e_lands, e_pairs[0][1], reduce_copies(5),
                                                    (15,), name="reduce_early_start")
    dyn = mm(dy_ssm, w_out_ssm_f, "nt", name="d_yn")
    dy_ssd, dproj, d_ssm_norm = gnorm_bwd(y_ssd, proj, ssm_norm_w + e_token[0:1, 0:1], dyn, dproj, name="gnorm_bwd")
    dxbc_post, ddtr, dpar = ssd_bwd(xbc_post, dtr, dtr_t, par_row, par_col, s_in, dy_ssd, name="ssd_bwd")
    dproj, d_ssm_cw_p, d_ssm_cb_p = conv_bwd(dxbc_post, xbc_pre, proj, OFF_XBC, ssm_cw_p, dproj, name="ssm_conv_bwd")
    do_lru = mm(dy_lru, w_out_lru_f, "nt", name="d_o_lru")
    du_lru, dproj, d_w_r, d_w_i, d_b_r, d_b_i, d_lam = lru_bwd(u_lru, proj, h_lru, do_lru, lru_w_r[0], lru_b_r, lru_w_i[0],
                                                               lru_b_i, lru_lambda, dproj, name="lru_bwd")
    dproj, d_lru_cw, d_lru_cb = conv_bwd(du_lru, None, proj, OFF_LX, lru_cw_full, dproj, name="lru_conv_bwd")
    ddt_cols = jnp.transpose(ddtr, (1, 0, 2)).reshape(t, SSM_HEADS).astype(bf16)
    ddt_cols = jnp.pad(ddt_cols, ((0, 0), (0, DT_PAD_W - SSM_HEADS)))
    dproj = lax.dynamic_update_slice(dproj, ddt_cols, (0, OFF_DT))

    d_ssm_cw = _unperm_xbc_cols(d_ssm_cw_p)
    d_ssm_cb = _unperm_xbc_cols(d_ssm_cb_p)
    dpar_h = jnp.transpose(dpar[:, 0:3, :], (1, 0, 2)).reshape(3, SSM_HEADS)
    small_names = ["norm1_w", "b_branch_gate", "ssm_conv_b", "ssm_dt_bias", "ssm_a_log", "ssm_d", "ssm_norm_w",
                   "lru_conv_b", "lru_w_r", "lru_b_r", "lru_w_i", "lru_b_i", "lru_lambda", "norm2_w", "norm_f_w"]
    small_g = dict(norm1_w=jnp.zeros_like(norm1_w), b_branch_gate=d_bg, ssm_conv_b=d_ssm_cb, ssm_dt_bias=dpar_h[0:1], ssm_a_log=dpar_h[1:2],
                   ssm_d=dpar_h[2:3], ssm_norm_w=d_ssm_norm, lru_conv_b=d_lru_cb, lru_w_r=d_w_r[None], lru_b_r=d_b_r,
                   lru_w_i=d_w_i[None], lru_b_i=d_b_i, lru_lambda=d_lam, norm2_w=d_norm2, norm_f_w=d_norm_f.reshape(D_MODEL))
    small_w = dict(norm1_w=norm1_w, b_branch_gate=b_branch_gate, ssm_conv_b=ssm_conv_b, ssm_dt_bias=ssm_dt_bias,
                   ssm_a_log=ssm_a_log, ssm_d=ssm_d, ssm_norm_w=ssm_norm_w, lru_conv_b=lru_conv_b, lru_w_r=lru_w_r,
                   lru_b_r=lru_b_r, lru_w_i=lru_w_i, lru_b_i=lru_b_i, lru_lambda=lru_lambda, norm2_w=norm2_w, norm_f_w=norm_f_w)
    small_m = dict(norm1_w=m_norm1_w, b_branch_gate=m_b_branch_gate, ssm_conv_b=m_ssm_conv_b, ssm_dt_bias=m_ssm_dt_bias,
                   ssm_a_log=m_ssm_a_log, ssm_d=m_ssm_d, ssm_norm_w=m_ssm_norm_w, lru_conv_b=m_lru_conv_b, lru_w_r=m_lru_w_r,
                   lru_b_r=m_lru_b_r, lru_w_i=m_lru_w_i, lru_b_i=m_lru_b_i, lru_lambda=m_lru_lambda, norm2_w=m_norm2_w,
                   norm_f_w=m_norm_f_w)
    small_v = dict(norm1_w=v_norm1_w, b_branch_gate=v_b_branch_gate, ssm_conv_b=v_ssm_conv_b, ssm_dt_bias=v_ssm_dt_bias,
                   ssm_a_log=v_ssm_a_log, ssm_d=v_ssm_d, ssm_norm_w=v_ssm_norm_w, lru_conv_b=v_lru_conv_b, lru_w_r=v_lru_w_r,
                   lru_b_r=v_lru_b_r, lru_w_i=v_lru_w_i, lru_b_i=v_lru_b_i, lru_lambda=v_lru_lambda, norm2_w=v_norm2_w,
                   norm_f_w=v_norm_f_w)
    shapes = [small_w[k].shape for k in small_names]
    conv_shapes = [(4, SSM_CONV_DIM), (4, LRU_WIDTH)]
    g_pack = _pack([small_g[k] for k in small_names] + [d_ssm_cw, d_lru_cw])
    s_send, s_recv, s_arrays, s_token = split_start([g_pack, lax.empty((8,) + g_pack.shape, f32)], g_pack, all_copies(),
                                                    (7,), name="small_start")
    d_w_in_p = mm(hn1, dproj, "tn", after=s_token, name="d_w_in")

    d_w_in_s = _col_shards(_unperm_in_cols(d_w_in_p))
    (l_sib,) = pair_exchange([d_w_in_s], name="pair_exchange_late")
    l_pair = pair_add(d_w_in_s, l_sib, idx, name="pair_add_w_in")
    l_land = lax.empty((3,) + l_pair[0].shape[1:], bf16)
    l_send, l_recv, l_arrays, l_token = split_start([l_pair[0], l_land], l_pair[1], reduce_copies(1), (3,),
                                                    name="reduce_late_start")
    dhn1 = mm(dproj, w_in_p, "nt", after=l_token, name="d_hn1")
    grad_x, d_norm1 = rms_bwd(x2, norm1_w, dhn1, dh1, name="rms1_bwd")

    e_arrays = split_wait(e_send, e_recv, e_arrays, d_norm1, reduce_copies(5), name="reduce_early_wait")
    e_half = [chip_sum(p[1], rb, idx, name="chip_sum_" + k) for p, rb, k in zip(e_pairs, e_arrays[5:], late_names)]
    big_out = {}
    for k, g in zip(late_names, pair_gather(e_half, name="pair_gather_early")):
        big_out[k] = (g,) + tuple(adamw(big_w[k], g, big_m[k], big_v[k], name="adamw_" + k))

    s_arrays = split_wait(s_send, s_recv, s_arrays, d_norm1, all_copies(), name="small_wait")
    g_sum = sum8(lax.dynamic_update_index_in_dim(s_arrays[1], g_pack, 2 * me + ci, 0), name="sum8")
    n1 = d_norm1.reshape(8, 128)
    n1_sum = sum8(lax.dynamic_update_index_in_dim(all_exchange(n1, name="all_exchange_norm1"), n1, 2 * me + ci, 0),
                  name="sum8_norm1")
    g_sum = lax.dynamic_update_slice(g_sum, n1_sum, (0, 0))
    g_ssm_cw_full, g_lru_cw_full = _unpack(g_sum, shapes + conv_shapes)[len(shapes):]
    g_ssm_cw = lax.dynamic_slice_in_dim(g_ssm_cw_full, me * 768, 768, axis=1)
    g_lru_cw = lax.dynamic_slice_in_dim(g_lru_cw_full, me * 320, 320, axis=1)
    loc_shapes = [(4, 768), (4, 320)]
    g_loc = _pack(_unpack(g_sum, shapes)[:len(shapes)] + [g_ssm_cw, g_lru_cw])
    w_loc = _pack([small_w[k] for k in small_names] + [ssm_conv_w[0], lru_conv_w[0]])
    m_loc = _pack([small_m[k] for k in small_names] + [m_ssm_conv_w[0], m_lru_conv_w[0]])
    v_loc = _pack([small_v[k] for k in small_names] + [v_ssm_conv_w[0], v_lru_conv_w[0]])
    d_loc, nm_loc, nv_loc = adamw(w_loc, g_loc, m_loc, v_loc, name="adamw_small")
    l_arrays = split_wait(l_send, l_recv, l_arrays, d_loc, reduce_copies(1), name="reduce_late_wait")
    l_half = chip_sum(l_pair[1], l_arrays[1], idx, name="chip_sum_w_in")
    (g_w_in,) = pair_gather([l_half], name="pair_gather_late")
    big_out["w_in"] = (g_w_in,) + tuple(adamw(big_w["w_in"], g_w_in, big_m["w_in"], big_v["w_in"], name="adamw_w_in"))
    small_out = {}
    unp = [_unpack(b, shapes + loc_shapes) for b in (g_loc, d_loc, nm_loc, nv_loc)]
    for i, k in enumerate(small_names + ["ssm_conv_w", "lru_conv_w"]):
        small_out[k] = tuple(u[i] for u in unp)

    order = ["norm1_w", "w_in", "b_branch_gate", "ssm_conv_w", "ssm_conv_b", "ssm_dt_bias", "ssm_a_log", "ssm_d", "ssm_norm_w",
             "w_out_ssm", "lru_conv_w", "lru_conv_b", "lru_w_r", "lru_b_r", "lru_w_i", "lru_b_i", "lru_lambda", "w_out_lru",
             "w_out", "norm2_w", "w_ffn_in", "w_ffn_out", "norm_f_w"]
    outs = [loss, grad_x[None]]
    for which in range(4):
        for k in order:
            if k in big_out:
                outs.append(big_out[k][which][None])
            elif k in ("ssm_conv_w", "lru_conv_w"):
                outs.append(small_out[k][which][None])
            else:
                outs.append(small_out[k][which])
    return tuple(outs)
```

```python
import functools
import math

import jax
import jax.numpy as jnp
from jax import lax
from jax.experimental import pallas as pl
from jax.experimental.pallas import tpu as pltpu

f32 = jnp.float32
bf16 = jnp.bfloat16

D_MODEL = 1024
SSM_D_INNER = 2048
SSM_HEADS = 32
SSM_HEAD_DIM = 64
SSM_GROUPS = 4
SSM_HPG = 8
SSM_D_STATE = 128
SSM_CHUNK = 128
SSM_GROUP_W = 512
SSM_CONV_DIM = 3072
XBC_GROUP_W = 768
LRU_WIDTH = 1280
LRU_BLOCKS = 10
LRU_BLOCK = 128
LRU_C = 8.0
FFN_HIDDEN = 2816
RMS_EPS = 1e-6
IN_PROJ_DIM = 9760
N_CHIPS = 4

OFF_GATES = 0
OFF_Z = 2048
OFF_LX = 4096
OFF_LY = 5376
OFF_DT = 6656
DT_PAD_W = 256
OFF_XBC = 6912
PROJ_W = 9984

ADAM_LR = 0.001
ADAM_B1 = 0.9
ADAM_B2 = 0.999
ADAM_EPS = 1e-08
ADAM_WD = 0.01
ADAM_STEP = 10

MESH = pl.DeviceIdType.MESH
ANY = pl.BlockSpec(memory_space=pl.ANY)

NN = (((1,), (0,)), ((), ()))
NT = (((1,), (1,)), ((), ()))
TN = (((0,), (0,)), ((), ()))


def _pick(n, cap, mult=128):
    best = None
    for t in range(mult, min(n, cap) + 1, mult):
        if n % t == 0:
            best = t
    return best if best is not None else n


def _sigmoid(x):
    return 1.0 / (1.0 + jnp.exp(-x))


def _softplus(x):
    return jnp.maximum(x, 0.0) + jnp.log(1.0 + jnp.exp(-jnp.abs(x)))


def _silu(x):
    return x * _sigmoid(x)


def _dsilu(x):
    s = _sigmoid(x)
    return s * (1.0 + x * (1.0 - s))


_GELU_K = math.sqrt(2.0 / math.pi)


def _gelu(x):
    return 0.5 * x * (1.0 + jnp.tanh(_GELU_K * (x + 0.044715 * x * x * x)))


def _dgelu(x):
    t = jnp.tanh(_GELU_K * (x + 0.044715 * x * x * x))
    return 0.5 * (1.0 + t) + 0.5 * x * (1.0 - t * t) * _GELU_K * (1.0 + 3.0 * 0.044715 * x * x)


def _expm1(x):
    poly = x * (1.0 + x * (0.5 + x * (1.0 / 6.0 + x * (1.0 / 24.0 + x * (1.0 / 120.0 + x * (1.0 / 720.0))))))
    return jnp.where(jnp.abs(x) < 0.1, poly, jnp.exp(x) - 1.0)


def _dot(a, b, dn):
    return lax.dot_general(a.astype(bf16), b.astype(bf16), dn, preferred_element_type=f32)


def _dot_01(a, b, dn, split, terms):
    r = a if split == 0 else b
    out = None
    for _ in range(terms):
        h = r.astype(bf16)
        r = r - h.astype(f32)
        d = lax.dot_general(h if split == 0 else a.astype(bf16), b.astype(bf16) if split == 0 else h, dn,
                            preferred_element_type=f32)
        out = d if out is None else out + d
    return out


def mm(a, b, mode, *, name, add=None, after=None, out_dtype=f32):
    if mode == "nn":
        (m, k), (k2, n) = a.shape, b.shape
    elif mode == "nt":
        (m, k), (n, k2) = a.shape, b.shape
    else:
        (k, m), (k2, n) = a.shape, b.shape
    assert k == k2, (a.shape, b.shape, mode)
    tm, tn, tk = _pick(m, 1024), _pick(n, 1024), _pick(k, 1024)
    nk = k // tk
    dn = {"nn": NN, "nt": NT, "tn": TN}[mode]
    a_spec = pl.BlockSpec((tk, tm), lambda i, j, kk: (kk, i)) if mode == "tn" else pl.BlockSpec((tm, tk), lambda i, j, kk: (i, kk))
    b_spec = pl.BlockSpec((tn, tk), lambda i, j, kk: (j, kk)) if mode == "nt" else pl.BlockSpec((tk, tn), lambda i, j, kk: (kk, j))
    o_spec = pl.BlockSpec((tm, tn), lambda i, j, kk: (i, j))
    has_add = add is not None

    n_extra = int(has_add) + int(after is not None)

    def body(a_ref, b_ref, *rest):
        add_ref = rest[0] if has_add else None
        o_ref = rest[n_extra]

        def finish(r):
            if has_add:
                r = r + add_ref[...]
            o_ref[...] = r.astype(out_dtype)

        if nk == 1:
            finish(_dot(a_ref[...], b_ref[...], dn))
            return
        acc = rest[-1]
        kk = pl.program_id(2)

        @pl.when(kk == 0)
        def _():
            acc[...] = jnp.zeros_like(acc)

        acc[...] += _dot(a_ref[...], b_ref[...], dn)

        @pl.when(kk == nk - 1)
        def _():
            finish(acc[...])

    ins = [a, b] + ([add] if has_add else []) + ([after] if after is not None else [])
    in_specs = [a_spec, b_spec] + ([o_spec] if has_add else []) + ([ANY] if after is not None else [])
    return pl.pallas_call(
        body, name=name, grid=(m // tm, n // tn, nk), in_specs=in_specs, out_specs=o_spec,
        out_shape=jax.ShapeDtypeStruct((m, n), out_dtype),
        scratch_shapes=[pltpu.VMEM((tm, tn), f32)] if nk > 1 else [],
        compiler_params=pltpu.CompilerParams(dimension_semantics=("parallel", "parallel", "arbitrary")),
    )(*ins)


def rms_fwd(x, w, *, name):
    t, d = x.shape
    tr = _pick(t, 256, 8)

    def body(x_ref, w_ref, o_ref):
        xv = x_ref[...]
        r = lax.rsqrt(jnp.mean(xv * xv, axis=-1, keepdims=True) + RMS_EPS)
        o_ref[...] = (xv * r * w_ref[...]).astype(bf16)

    return pl.pallas_call(
        body, name=name, grid=(t // tr,),
        in_specs=[pl.BlockSpec((tr, d), lambda i: (i, 0)), pl.BlockSpec((1, d), lambda i: (0, 0))],
        out_specs=pl.BlockSpec((tr, d), lambda i: (i, 0)), out_shape=jax.ShapeDtypeStruct((t, d), bf16),
    )(x, w)


def _rms_bwd_math(xv, wv, dy):
    r = lax.rsqrt(jnp.mean(xv * xv, axis=-1, keepdims=True) + RMS_EPS)
    g = dy * wv
    dx = r * g - xv * (r * r * r) * jnp.mean(g * xv, axis=-1, keepdims=True)
    dw = jnp.sum(dy * xv * r, axis=0, keepdims=True)
    return dx, dw


def rms_bwd(x, w, dy, res, *, name):
    t, d = x.shape
    tr = _pick(t, 256, 8)

    def body(x_ref, w_ref, dy_ref, res_ref, dx_ref, dw_ref):
        dx, dw = _rms_bwd_math(x_ref[...], w_ref[...], dy_ref[...])
        dx_ref[...] = dx + res_ref[...]

        @pl.when(pl.program_id(0) == 0)
        def _():
            dw_ref[...] = jnp.zeros_like(dw_ref)

        dw_ref[...] += dw

    row = pl.BlockSpec((tr, d), lambda i: (i, 0))
    vec = pl.BlockSpec((1, d), lambda i: (0, 0))
    return pl.pallas_call(
        body, name=name, grid=(t // tr,), in_specs=[row, vec, row, row], out_specs=[row, vec],
        out_shape=[jax.ShapeDtypeStruct((t, d), f32), jax.ShapeDtypeStruct((1, d), f32)],
        compiler_params=pltpu.CompilerParams(dimension_semantics=("arbitrary",)),
    )(x, w, dy, res)


def loss_head(h, w, target, *, name):
    t, d = h.shape
    tr = _pick(t, 256, 8)

    def body(h_ref, w_ref, t_ref, loss_ref, dh_ref, dw_ref):
        xv, wv = h_ref[...], w_ref[...]
        r = lax.rsqrt(jnp.mean(xv * xv, axis=-1, keepdims=True) + RMS_EPS)
        err = xv * r * wv - t_ref[...]
        part = 0.5 * jnp.sum(jnp.mean(err * err, axis=-1, keepdims=True), axis=0, keepdims=True)
        dx, dw = _rms_bwd_math(xv, wv, err * (1.0 / d))
        dh_ref[...] = dx

        @pl.when(pl.program_id(0) == 0)
        def _():
            dw_ref[...] = jnp.zeros_like(dw_ref)
            loss_ref[...] = jnp.zeros_like(loss_ref)

        dw_ref[...] += dw
        loss_ref[...] += part

    row = pl.BlockSpec((tr, d), lambda i: (i, 0))
    vec = pl.BlockSpec((1, d), lambda i: (0, 0))
    return pl.pallas_call(
        body, name=name, grid=(t // tr,), in_specs=[row, vec, row],
        out_specs=[pl.BlockSpec((8, 128), lambda i: (0, 0)), row, vec],
        out_shape=[jax.ShapeDtypeStruct((8, 128), f32), jax.ShapeDtypeStruct((t, d), f32), jax.ShapeDtypeStruct((1, d), f32)],
        compiler_params=pltpu.CompilerParams(dimension_semantics=("arbitrary",)),
    )(h, w, target)


CONV_ROWS = 512


def conv_fwd(src, col0, width, w, b, *, silu, name):
    t = src.shape[0]
    tc = _pick(math.gcd(width, col0), 768)
    assert col0 % tc == 0
    cb = col0 // tc
    r = CONV_ROWS

    def body(u_ref, w_ref, b_ref, *rest):
        ext = rest[-1]
        j = pl.program_id(1)

        @pl.when(j == 0)
        def _():
            ext[0:8, :] = jnp.zeros((8, tc), f32)

        @pl.when(j > 0)
        def _():
            ext[0:8, :] = ext[r:r + 8, :]

        ext[8:r + 8, :] = u_ref[...]
        v = ext[...]
        wv = w_ref[...]
        acc = b_ref[...] + wv[3:4, :] * v
        for s in (1, 2, 3):
            acc = acc + wv[3 - s:4 - s, :] * pltpu.roll(v, s, 0)
        pre = acc[8:, :]
        rest[0][...] = pre
        if silu:
            rest[1][...] = _silu(pre)

    tile = pl.BlockSpec((r, tc), lambda c, j: (j, c))
    n_out = 2 if silu else 1
    return pl.pallas_call(
        body, name=name, grid=(width // tc, t // r),
        in_specs=[pl.BlockSpec((r, tc), lambda c, j: (j, cb + c)), pl.BlockSpec((4, tc), lambda c, j: (0, c)),
                  pl.BlockSpec((1, tc), lambda c, j: (0, c))],
        out_specs=[tile] * n_out, out_shape=[jax.ShapeDtypeStruct((t, width), f32)] * n_out,
        scratch_shapes=[pltpu.VMEM((r + 8, tc), f32)],
        compiler_params=pltpu.CompilerParams(dimension_semantics=("parallel", "arbitrary")),
    )(src, w, b)


def conv_bwd(dpost, pre, src, col0, w, dst, *, name):
    t, width = dpost.shape
    tc = _pick(math.gcd(width, col0), 768)
    assert col0 % tc == 0
    cb = col0 // tc
    r = CONV_ROWS
    nt = t // r
    has_pre = pre is not None

    def body(*refs):
        refs = refs[1:]
        if has_pre:
            d_ref, p_ref, u_ref, w_ref, du_ref, dw_ref, db_ref, ext = refs
        else:
            d_ref, u_ref, w_ref, du_ref, dw_ref, db_ref, ext = refs
        j = pl.program_id(1)

        @pl.when(j == 0)
        def _():
            ext[r:r + 8, :] = jnp.zeros((8, tc), f32)
            dw_ref[...] = jnp.zeros_like(dw_ref)
            db_ref[...] = jnp.zeros_like(db_ref)

        @pl.when(j > 0)
        def _():
            ext[r:r + 8, :] = ext[0:8, :]

        dpre = d_ref[...]
        if has_pre:
            dpre = dpre * _dsilu(p_ref[...])
        ext[0:r, :] = dpre
        v = ext[...]
        wv = w_ref[...]
        uv = u_ref[...]
        du = wv[3:4, :] * dpre
        dw_ref[3:4, :] += jnp.sum(dpre * uv, axis=0, keepdims=True)
        for s in (1, 2, 3):
            sh = pltpu.roll(v, r + 8 - s, 0)[0:r, :]
            du = du + wv[3 - s:4 - s, :] * sh
            dw_ref[3 - s:4 - s, :] += jnp.sum(sh * uv, axis=0, keepdims=True)
        db_ref[...] += jnp.sum(dpre, axis=0, keepdims=True)
        du_ref[...] = du.astype(bf16)

    rev = pl.BlockSpec((r, tc), lambda c, j: (nt - 1 - j, c))
    win = pl.BlockSpec((r, tc), lambda c, j: (nt - 1 - j, cb + c))
    in_specs = [ANY, rev] + ([rev] if has_pre else []) + [win, pl.BlockSpec((4, tc), lambda c, j: (0, c))]
    ins = [dst, dpost] + ([pre] if has_pre else []) + [src, w]
    return pl.pallas_call(
        body, name=name, grid=(width // tc, nt), in_specs=in_specs,
        out_specs=[win, pl.BlockSpec((4, tc), lambda c, j: (0, c)), pl.BlockSpec((1, tc), lambda c, j: (0, c))],
        out_shape=[jax.ShapeDtypeStruct(dst.shape, bf16), jax.ShapeDtypeStruct((4, width), f32),
                   jax.ShapeDtypeStruct((1, width), f32)],
        input_output_aliases={0: 0},
        scratch_shapes=[pltpu.VMEM((r + 8, tc), f32)],
        compiler_params=pltpu.CompilerParams(dimension_semantics=("parallel", "arbitrary")),
    )(*ins)


def _ssd_common(xbc_ref, dtr_ref, dtrT_ref, par_row_ref, par_col_ref):
    l = SSM_CHUNK
    x = xbc_ref[:, 0:SSM_GROUP_W]
    bm = xbc_ref[:, SSM_GROUP_W:SSM_GROUP_W + SSM_D_STATE]
    cm = xbc_ref[:, SSM_GROUP_W + SSM_D_STATE:XBC_GROUP_W]
    par_row = par_row_ref[0]
    par_col = par_col_ref[0]
    bias_row, alog_row, d_row = par_row[0:1, :], par_row[1:2, :], par_row[2:3, :]
    bias_col, alog_col = par_col[:, 0:1], par_col[:, 1:2]
    dtr = dtr_ref[0]
    dt = _softplus(dtr + bias_row)
    dt_t = _softplus(dtrT_ref[0] + bias_col)
    a_row = -jnp.exp(alog_row)
    a_col = -jnp.exp(alog_col)
    li = lax.broadcasted_iota(jnp.int32, (l, l), 0)
    si = lax.broadcasted_iota(jnp.int32, (l, l), 1)
    tri = (li >= si).astype(f32)
    cs = _dot_01(tri, dt * a_row, NN, 1, 3)
    cs_t = _dot_01(dt_t * a_col, tri, NT, 0, 3)
    off = lax.broadcasted_iota(jnp.int32, (SSM_HPG, SSM_GROUP_W), 1) - SSM_HEAD_DIM * lax.broadcasted_iota(
        jnp.int32, (SSM_HPG, SSM_GROUP_W), 0)
    ex = ((off >= 0) & (off < SSM_HEAD_DIM)).astype(f32)
    cs_x = _dot_01(cs, ex, NN, 0, 3)
    cl_x = cs_x[l - 1:l, :]
    return dict(x=x, bm=bm, cm=cm, dtr=dtr, dt=dt, a_row=a_row, bias_row=bias_row, tri=tri, li=li, si=si, cs=cs,
                cs_t=cs_t, ex=ex, dt_x=_dot_01(dt, ex, NN, 0, 2), d_x=_dot_01(par_row, ex, NN, 0, 2)[2:3, :], e_x=jnp.exp(cs_x),
                el_x=jnp.exp(cl_x), dec_x=jnp.exp(cl_x - cs_x))


def ssd_fwd(xbc, dtr, dtr_t, par_row, par_col, *, name):
    t = xbc.shape[0]
    nc = t // SSM_CHUNK
    l, p = SSM_CHUNK, SSM_HEAD_DIM

    def body(xbc_ref, dtr_ref, dtrT_ref, prow_ref, pcol_ref, y_ref, sin_ref, state):
        @pl.when(pl.program_id(1) == 0)
        def _():
            state[...] = jnp.zeros_like(state)

        q = _ssd_common(xbc_ref, dtr_ref, dtrT_ref, prow_ref, pcol_ref)
        st = state[...]
        sin_ref[0] = st
        xd = q["x"] * q["dt_x"]
        g = _dot(q["cm"], q["bm"], NT)
        for r in range(SSM_HPG):
            sl = slice(r * p, (r + 1) * p)
            diff = q["cs"][:, r:r + 1] - q["cs_t"][r:r + 1, :]
            lm = jnp.where(q["li"] >= q["si"], jnp.exp(jnp.minimum(diff, 0.0)), 0.0)
            y_ref[:, sl] = _dot(g * lm, xd[:, sl], NN)
        y_ref[...] += q["e_x"] * _dot(q["cm"], st, NN) + q["d_x"] * q["x"]
        state[...] = q["el_x"] * st + _dot(q["bm"].T, xd * q["dec_x"], NN)

    return pl.pallas_call(
        body, name=name, grid=(SSM_GROUPS, nc),
        in_specs=[pl.BlockSpec((l, XBC_GROUP_W), lambda g, c: (c, g)),
                  pl.BlockSpec((1, l, SSM_HPG), lambda g, c: (g, c, 0)),
                  pl.BlockSpec((1, SSM_HPG, l), lambda g, c: (g, 0, c)),
                  pl.BlockSpec((1, 8, 8), lambda g, c: (g, 0, 0)),
                  pl.BlockSpec((1, 8, 8), lambda g, c: (g, 0, 0))],
        out_specs=[pl.BlockSpec((l, SSM_GROUP_W), lambda g, c: (c, g)),
                   pl.BlockSpec((1, SSM_D_STATE, SSM_GROUP_W), lambda g, c: (c, 0, g))],
        out_shape=[jax.ShapeDtypeStruct((t, SSM_D_INNER), f32),
                   jax.ShapeDtypeStruct((nc, SSM_D_STATE, SSM_D_INNER), f32)],
        scratch_shapes=[pltpu.VMEM((SSM_D_STATE, SSM_GROUP_W), f32)],
        compiler_params=pltpu.CompilerParams(dimension_semantics=("parallel", "arbitrary")),
    )(xbc, dtr, dtr_t, par_row, par_col)


def ssd_bwd(xbc, dtr, dtr_t, par_row, par_col, s_in, dy, *, name):
    t = xbc.shape[0]
    nc = t // SSM_CHUNK
    l, p = SSM_CHUNK, SSM_HEAD_DIM

    def body(xbc_ref, dtr_ref, dtrT_ref, prow_ref, pcol_ref, sin_ref, dy_ref, dxbc_ref, ddtr_ref, dpar_ref,
             dstate, yd_buf, dxd_buf):
        @pl.when(pl.program_id(1) == 0)
        def _():
            dstate[...] = jnp.zeros_like(dstate)
            dpar_ref[...] = jnp.zeros_like(dpar_ref)

        q = _ssd_common(xbc_ref, dtr_ref, dtrT_ref, prow_ref, pcol_ref)
        x, bm, cm, ex, li, si = q["x"], q["bm"], q["cm"], q["ex"], q["li"], q["si"]
        e_x, el_x, dec_x = q["e_x"], q["el_x"], q["dec_x"]
        st = sin_ref[0]
        dst = dstate[...]
        dy = dy_ref[...]
        xd = x * q["dt_x"]
        g = _dot(cm, bm, NT)
        dg = jnp.zeros((l, l), f32)
        for r in range(SSM_HPG):
            sl = slice(r * p, (r + 1) * p)
            diff = q["cs"][:, r:r + 1] - q["cs_t"][r:r + 1, :]
            lm = jnp.where(li >= si, jnp.exp(jnp.minimum(diff, 0.0)), 0.0)
            m = (g * lm).astype(bf16)
            xdh, dyh = xd[:, sl].astype(bf16), dy[:, sl].astype(bf16)
            yd_buf[:, sl] = _dot(m, xdh, NN)
            dxd_buf[:, sl] = _dot(m, dyh, TN)
            dg = dg + _dot(dyh, xdh, NT) * lm
        yd, dxd_diag = yd_buf[...], dxd_buf[...]
        yo = e_x * _dot(cm, st, NN)
        dz = e_x * dy
        wv = _dot(bm, dst, NN)
        xw = xd * wv * dec_x
        row8 = lax.broadcasted_iota(jnp.int32, (l, SSM_HPG), 0)
        dy_b, xd_b = dy.astype(bf16).astype(f32), xd.astype(bf16).astype(f32)
        dcs = _dot_01(dy_b * yd - xd_b * dxd_diag + dy * yo - xw, ex, NT, 0, 3)
        tail = jnp.sum(xw, axis=0, keepdims=True) + el_x * jnp.sum(dst * st, axis=0, keepdims=True)
        dcl = _dot_01(jnp.broadcast_to(tail, (SSM_HPG, SSM_GROUP_W)), ex, NT, 0, 3)[0:1, :]
        dcs = dcs + jnp.where(row8 == l - 1, dcl, 0.0)
        dda = _dot_01(q["tri"], dcs, TN, 1, 3)
        dxd = dxd_diag + dec_x * wv
        ddt = _dot_01(dxd * x, ex, NT, 0, 3) + dda * q["a_row"]
        ddtr = ddt * _sigmoid(q["dtr"] + q["bias_row"])
        ddtr_ref[0] = ddtr
        dd = _dot_01(jnp.broadcast_to(jnp.sum(dy * x, axis=0, keepdims=True), (SSM_HPG, SSM_GROUP_W)), ex, NT, 0, 2)[0:1, :]
        dpar_ref[0, 0:1, :] += jnp.sum(ddtr, axis=0, keepdims=True)
        dpar_ref[0, 1:2, :] += jnp.sum(dda * q["dt"], axis=0, keepdims=True) * q["a_row"]
        dpar_ref[0, 2:3, :] += dd
        dxbc_ref[:, 0:SSM_GROUP_W] = dxd * q["dt_x"] + q["d_x"] * dy
        dxbc_ref[:, SSM_GROUP_W:SSM_GROUP_W + SSM_D_STATE] = _dot(dg, cm, TN) + _dot(xd * dec_x, dst, NT)
        dxbc_ref[:, SSM_GROUP_W + SSM_D_STATE:XBC_GROUP_W] = _dot(dg, bm, NN) + _dot(dz, st, NT)
        dstate[...] = _dot(cm.T, dz, NN) + el_x * dst

    rc = lambda c: nc - 1 - c
    return pl.pallas_call(
        body, name=name, grid=(SSM_GROUPS, nc),
        in_specs=[pl.BlockSpec((l, XBC_GROUP_W), lambda g, c: (rc(c), g)),
                  pl.BlockSpec((1, l, SSM_HPG), lambda g, c: (g, rc(c), 0)),
                  pl.BlockSpec((1, SSM_HPG, l), lambda g, c: (g, 0, rc(c))),
                  pl.BlockSpec((1, 8, 8), lambda g, c: (g, 0, 0)),
                  pl.BlockSpec((1, 8, 8), lambda g, c: (g, 0, 0)),
                  pl.BlockSpec((1, SSM_D_STATE, SSM_GROUP_W), lambda g, c: (rc(c), 0, g)),
                  pl.BlockSpec((l, SSM_GROUP_W), lambda g, c: (rc(c), g))],
        out_specs=[pl.BlockSpec((l, XBC_GROUP_W), lambda g, c: (rc(c), g)),
                   pl.BlockSpec((1, l, SSM_HPG), lambda g, c: (g, rc(c), 0)),
                   pl.BlockSpec((1, 8, 8), lambda g, c: (g, 0, 0))],
        out_shape=[jax.ShapeDtypeStruct((t, SSM_CONV_DIM), f32),
                   jax.ShapeDtypeStruct((SSM_GROUPS, t, SSM_HPG), f32),
                   jax.ShapeDtypeStruct((SSM_GROUPS, 8, 8), f32)],
        scratch_shapes=[pltpu.VMEM((SSM_D_STATE, SSM_GROUP_W), f32), pltpu.VMEM((l, SSM_GROUP_W), f32),
                        pltpu.VMEM((l, SSM_GROUP_W), f32)],
        compiler_params=pltpu.CompilerParams(dimension_semantics=("parallel", "arbitrary")),
    )(xbc, dtr, dtr_t, par_row, par_col, s_in, dy)


def gnorm_fwd(y, proj, w, *, name):
    t = y.shape[0]
    tr = _pick(t, 512, 8)
    gw = SSM_GROUP_W
    zb = OFF_Z // gw

    def body(y_ref, z_ref, w_ref, o_ref):
        y2 = y_ref[...] * _silu(z_ref[...])
        r = lax.rsqrt(jnp.mean(y2 * y2, axis=-1, keepdims=True) + RMS_EPS)
        o_ref[...] = (y2 * r * w_ref[...]).astype(bf16)

    return pl.pallas_call(
        body, name=name, grid=(SSM_GROUPS, t // tr),
        in_specs=[pl.BlockSpec((tr, gw), lambda g, i: (i, g)), pl.BlockSpec((tr, gw), lambda g, i: (i, zb + g)),
                  pl.BlockSpec((1, gw), lambda g, i: (0, g))],
        out_specs=pl.BlockSpec((tr, gw), lambda g, i: (i, g)), out_shape=jax.ShapeDtypeStruct((t, SSM_D_INNER), bf16),
    )(y, proj, w)


def gnorm_bwd(y, proj, w, dout, dst, *, name):
    t = y.shape[0]
    tr = _pick(t, 512, 8)
    gw = SSM_GROUP_W
    zb = OFF_Z // gw

    def body(_, y_ref, z_ref, w_ref, do_ref, dy_ref, dz_ref, dw_ref):
        yv, zv = y_ref[...], z_ref[...]
        sz = _silu(zv)
        y2 = yv * sz
        dy2, dw = _rms_bwd_math(y2, w_ref[...], do_ref[...])
        dy_ref[...] = dy2 * sz
        dz_ref[...] = (dy2 * yv * _dsilu(zv)).astype(bf16)

        @pl.when(pl.program_id(1) == 0)
        def _():
            dw_ref[...] = jnp.zeros_like(dw_ref)

        dw_ref[...] += dw

    tile = pl.BlockSpec((tr, gw), lambda g, i: (i, g))
    vec = pl.BlockSpec((1, gw), lambda g, i: (0, g))
    return pl.pallas_call(
        body, name=name, grid=(SSM_GROUPS, t // tr),
        in_specs=[ANY, tile, pl.BlockSpec((tr, gw), lambda g, i: (i, zb + g)), vec, tile],
        out_specs=[tile, pl.BlockSpec((tr, gw), lambda g, i: (i, zb + g)), vec],
        out_shape=[jax.ShapeDtypeStruct((t, SSM_D_INNER), f32), jax.ShapeDtypeStruct(dst.shape, bf16),
                   jax.ShapeDtypeStruct((1, SSM_D_INNER), f32)],
        input_output_aliases={0: 1},
        compiler_params=pltpu.CompilerParams(dimension_semantics=("parallel", "arbitrary")),
    )(dst, y, proj, w, dout)


LRU_ROWS = 256


def _lru_gates(uv, wr_ref, wi_ref, br_ref, bi_ref, lam_ref):
    rg = _sigmoid(_dot(uv, wr_ref[0], NN) + br_ref[...])
    ig = _sigmoid(_dot(uv, wi_ref[0], NN) + bi_ref[...])
    sp = _softplus(-lam_ref[...])
    la = -LRU_C * rg * sp
    a = jnp.exp(la)
    s = jnp.sqrt(jnp.maximum(-_expm1(2.0 * la), 0.0))
    return rg, ig, sp, la, a, s


def lru_fwd(u, proj, w_r, b_r, w_i, b_i, lam, *, name):
    t = u.shape[0]
    r = LRU_ROWS
    lb = LRU_BLOCK
    yb = OFF_LY // lb

    def body(u_ref, y_ref, wr_ref, br_ref, wi_ref, bi_ref, lam_ref, h_ref, o_ref, carry):
        @pl.when(pl.program_id(1) == 0)
        def _():
            carry[...] = jnp.zeros_like(carry)

        uv = u_ref[...]
        _, ig, _, _, a, s = _lru_gates(uv, wr_ref, wi_ref, br_ref, bi_ref, lam_ref)
        b = s * ig * uv
        row = lax.broadcasted_iota(jnp.int32, (r, lb), 0)
        d = 1
        while d < r:
            keep = row >= d
            b = b + a * jnp.where(keep, pltpu.roll(b, d, 0), 0.0)
            a = a * jnp.where(keep, pltpu.roll(a, d, 0), 1.0)
            d *= 2
        h = b + a * carry[0:1, :]
        carry[0:1, :] = h[r - 1:r, :]
        h_ref[...] = h
        o_ref[...] = (h * _gelu(y_ref[...])).astype(bf16)

    tile = pl.BlockSpec((r, lb), lambda hb, j: (j, hb))
    vec = pl.BlockSpec((1, lb), lambda hb, j: (0, hb))
    wsp = pl.BlockSpec((1, lb, lb), lambda hb, j: (hb, 0, 0))
    return pl.pallas_call(
        body, name=name, grid=(LRU_BLOCKS, t // r),
        in_specs=[tile, pl.BlockSpec((r, lb), lambda hb, j: (j, yb + hb)), wsp, vec, wsp, vec, vec],
        out_specs=[tile, tile],
        out_shape=[jax.ShapeDtypeStruct((t, LRU_WIDTH), f32), jax.ShapeDtypeStruct((t, LRU_WIDTH), bf16)],
        scratch_shapes=[pltpu.VMEM((8, lb), f32)],
        compiler_params=pltpu.CompilerParams(dimension_semantics=("parallel", "arbitrary")),
    )(u, proj, w_r, b_r, w_i, b_i, lam)


def lru_bwd(u, proj, hseq, dout, w_r, b_r, w_i, b_i, lam, dst, *, name):
    t = u.shape[0]
    r = LRU_ROWS
    nt = t // r
    lb = LRU_BLOCK
    yb = OFF_LY // lb

    def body(_, u_ref, y_ref, h_ref, hp_ref, do_ref, wr_ref, br_ref, wi_ref, bi_ref, lam_ref,
             du_ref, dy_ref, dwr_ref, dwi_ref, dbr_ref, dbi_ref, dlam_ref, carry_dh, carry_a):
        j = pl.program_id(1)

        @pl.when(j == 0)
        def _():
            carry_dh[...] = jnp.zeros_like(carry_dh)
            carry_a[...] = jnp.zeros_like(carry_a)
            dwr_ref[...] = jnp.zeros_like(dwr_ref)
            dwi_ref[...] = jnp.zeros_like(dwi_ref)
            dbr_ref[...] = jnp.zeros_like(dbr_ref)
            dbi_ref[...] = jnp.zeros_like(dbi_ref)
            dlam_ref[...] = jnp.zeros_like(dlam_ref)

        uv = u_ref[...]
        yv = y_ref[...]
        hv = h_ref[...]
        dov = do_ref[...]
        rg, ig, sp, la, a, s = _lru_gates(uv, wr_ref, wi_ref, br_ref, bi_ref, lam_ref)
        dy_ref[...] = (dov * hv * _dgelu(yv)).astype(bf16)
        gq = dov * _gelu(yv)
        row = lax.broadcasted_iota(jnp.int32, (r, lb), 0)
        an = jnp.where(row < r - 1, pltpu.roll(a, r - 1, 0), carry_a[0:1, :])
        d = 1
        while d < r:
            keep = row < r - d
            gq = gq + an * jnp.where(keep, pltpu.roll(gq, r - d, 0), 0.0)
            an = an * jnp.where(keep, pltpu.roll(an, r - d, 0), 1.0)
            d *= 2
        dh = gq + an * carry_dh[0:1, :]
        carry_dh[0:1, :] = dh[0:1, :]
        carry_a[0:1, :] = a[0:1, :]
        first = jnp.where(j == nt - 1, 0.0, 1.0) * hp_ref[7:8, :]
        hprev = jnp.where(row >= 1, pltpu.roll(hv, 1, 0), first)
        da = dh * hprev
        iu = ig * uv
        e2 = jnp.exp(2.0 * la)
        dla = da * a - dh * iu * e2 / jnp.maximum(s, 1e-30)
        drp = dla * (-LRU_C * sp) * rg * (1.0 - rg)
        dip = dh * s * uv * ig * (1.0 - ig)
        dlam_ref[...] += jnp.sum(dla * (LRU_C * rg) * _sigmoid(-lam_ref[...]), axis=0, keepdims=True)
        du_ref[...] = dh * s * ig + _dot(drp, wr_ref[0], NT) + _dot(dip, wi_ref[0], NT)
        dwr_ref[0] += _dot(uv, drp, TN)
        dwi_ref[0] += _dot(uv, dip, TN)
        dbr_ref[...] += jnp.sum(drp, axis=0, keepdims=True)
        dbi_ref[...] += jnp.sum(dip, axis=0, keepdims=True)

    rj = lambda j: nt - 1 - j
    tile = pl.BlockSpec((r, lb), lambda hb, j: (rj(j), hb))
    vec = pl.BlockSpec((1, lb), lambda hb, j: (0, hb))
    wsp = pl.BlockSpec((1, lb, lb), lambda hb, j: (hb, 0, 0))
    hprev_spec = pl.BlockSpec((8, lb), lambda hb, j: (jnp.maximum(rj(j) * (r // 8) - 1, 0), hb))
    ywin = pl.BlockSpec((r, lb), lambda hb, j: (rj(j), yb + hb))
    return pl.pallas_call(
        body, name=name, grid=(LRU_BLOCKS, nt),
        in_specs=[ANY, tile, ywin, tile, hprev_spec, tile, wsp, vec, wsp, vec, vec],
        out_specs=[tile, ywin, wsp, wsp, vec, vec, vec],
        out_shape=[jax.ShapeDtypeStruct((t, LRU_WIDTH), f32), jax.ShapeDtypeStruct(dst.shape, bf16),
                   jax.ShapeDtypeStruct((LRU_BLOCKS, lb, lb), f32), jax.ShapeDtypeStruct((LRU_BLOCKS, lb, lb), f32),
                   jax.ShapeDtypeStruct((1, LRU_WIDTH), f32), jax.ShapeDtypeStruct((1, LRU_WIDTH), f32),
                   jax.ShapeDtypeStruct((1, LRU_WIDTH), f32)],
        input_output_aliases={0: 1},
        scratch_shapes=[pltpu.VMEM((8, lb), f32), pltpu.VMEM((8, lb), f32)],
        compiler_params=pltpu.CompilerParams(dimension_semantics=("parallel", "arbitrary")),
    )(dst, u, proj, hseq, hseq, dout, w_r, b_r, w_i, b_i, lam)


def merge_fwd(proj, bg, y_ssm, y_lru, *, name):
    t, d = y_ssm.shape
    tr = _pick(t, 256, 8)
    gb = OFF_GATES // d

    def body(gs_ref, gl_ref, bs_ref, bl_ref, ys_ref, yl_ref, o_ref):
        gs = _sigmoid(gs_ref[...] + bs_ref[...])
        gl = _sigmoid(gl_ref[...] + bl_ref[...])
        o_ref[...] = (gs * ys_ref[...] + gl * yl_ref[...]).astype(bf16)

    row = pl.BlockSpec((tr, d), lambda i: (i, 0))
    return pl.pallas_call(
        body, name=name, grid=(t // tr,),
        in_specs=[pl.BlockSpec((tr, d), lambda i: (i, gb)), pl.BlockSpec((tr, d), lambda i: (i, gb + 1)),
                  pl.BlockSpec((1, d), lambda i: (0, 0)), pl.BlockSpec((1, d), lambda i: (0, 1)), row, row],
        out_specs=row, out_shape=jax.ShapeDtypeStruct((t, d), bf16),
    )(proj, proj, bg, bg, y_ssm, y_lru)


def merge_bwd(proj, bg, y_ssm, y_lru, dmix, *, name):
    t, d = y_ssm.shape
    tr = _pick(t, 256, 8)
    gb = OFF_GATES // d

    def body(gs_ref, gl_ref, bs_ref, bl_ref, ys_ref, yl_ref, dm_ref, dg_ref, dys_ref, dyl_ref, dbg_ref):
        gs = _sigmoid(gs_ref[...] + bs_ref[...])
        gl = _sigmoid(gl_ref[...] + bl_ref[...])
        dm = dm_ref[...]
        dys_ref[...] = (dm * gs).astype(bf16)
        dyl_ref[...] = (dm * gl).astype(bf16)
        dgs = dm * ys_ref[...] * gs * (1.0 - gs)
        dgl = dm * yl_ref[...] * gl * (1.0 - gl)
        dg_ref[:, 0:d] = dgs.astype(bf16)
        dg_ref[:, d:2 * d] = dgl.astype(bf16)

        @pl.when(pl.program_id(0) == 0)
        def _():
            dbg_ref[...] = jnp.zeros_like(dbg_ref)

        dbg_ref[:, 0:d] += jnp.sum(dgs, axis=0, keepdims=True)
        dbg_ref[:, d:2 * d] += jnp.sum(dgl, axis=0, keepdims=True)

    row = pl.BlockSpec((tr, d), lambda i: (i, 0))
    return pl.pallas_call(
        body, name=name, grid=(t // tr,),
        in_specs=[pl.BlockSpec((tr, d), lambda i: (i, gb)), pl.BlockSpec((tr, d), lambda i: (i, gb + 1)),
                  pl.BlockSpec((1, d), lambda i: (0, 0)), pl.BlockSpec((1, d), lambda i: (0, 1)), row, row, row],
        out_specs=[pl.BlockSpec((tr, 2 * d), lambda i: (i, OFF_GATES // (2 * d))), row, row,
                   pl.BlockSpec((1, 2 * d), lambda i: (0, 0))],
        out_shape=[jax.ShapeDtypeStruct((t, PROJ_W), bf16), jax.ShapeDtypeStruct((t, d), bf16),
                   jax.ShapeDtypeStruct((t, d), bf16), jax.ShapeDtypeStruct((1, 2 * d), f32)],
        compiler_params=pltpu.CompilerParams(dimension_semantics=("arbitrary",)),
    )(proj, proj, bg, bg, y_ssm, y_lru, dmix)


def swiglu_fwd(ff, *, name):
    t = ff.shape[0]
    hd = FFN_HIDDEN
    tr = _pick(t, 128, 8)

    def body(f_ref, o_ref):
        o_ref[...] = (_silu(f_ref[:, 0:hd]) * f_ref[:, hd:2 * hd]).astype(bf16)

    return pl.pallas_call(
        body, name=name, grid=(t // tr,), in_specs=[pl.BlockSpec((tr, 2 * hd), lambda i: (i, 0))],
        out_specs=pl.BlockSpec((tr, hd), lambda i: (i, 0)), out_shape=jax.ShapeDtypeStruct((t, hd), bf16),
    )(ff)


def swiglu_bwd(ff, dact, *, name):
    t = ff.shape[0]
    hd = FFN_HIDDEN
    tr = _pick(t, 128, 8)

    def body(f_ref, d_ref, o_ref):
        gate, up, dv = f_ref[:, 0:hd], f_ref[:, hd:2 * hd], d_ref[...]
        o_ref[:, 0:hd] = (dv * up * _dsilu(gate)).astype(bf16)
        o_ref[:, hd:2 * hd] = (dv * _silu(gate)).astype(bf16)

    return pl.pallas_call(
        body, name=name, grid=(t // tr,),
        in_specs=[pl.BlockSpec((tr, 2 * hd), lambda i: (i, 0)), pl.BlockSpec((tr, hd), lambda i: (i, 0))],
        out_specs=pl.BlockSpec((tr, 2 * hd), lambda i: (i, 0)), out_shape=jax.ShapeDtypeStruct((t, 2 * hd), bf16),
    )(ff, dact)


def _adam_math(w, g, m, v):
    m = ADAM_B1 * m + (1.0 - ADAM_B1) * g
    v = ADAM_B2 * v + (1.0 - ADAM_B2) * (g * g)
    m_hat = m / (1.0 - ADAM_B1 ** ADAM_STEP)
    v_hat = v / (1.0 - ADAM_B2 ** ADAM_STEP)
    delta = -ADAM_LR * (m_hat / (jnp.sqrt(v_hat) + ADAM_EPS) + ADAM_WD * w)
    return delta, m, v


def _row_tile(rows, cols):
    cap = max(8, (1 << 18) // cols)
    return _pick(rows, cap, 8) if rows % 8 == 0 else rows


def adamw(w, g, m, v, *, name):
    rows, cols = w.shape
    tr = _row_tile(rows, cols)

    def body(w_ref, g_ref, m_ref, v_ref, d_ref, nm_ref, nv_ref):
        d, nm, nv = _adam_math(w_ref[...], g_ref[...], m_ref[...], v_ref[...])
        d_ref[...] = d
        nm_ref[...] = nm
        nv_ref[...] = nv

    tile = pl.BlockSpec((tr, cols), lambda i: (i, 0))
    return pl.pallas_call(
        body, name=name, grid=(rows // tr,), in_specs=[tile] * 4, out_specs=[tile] * 3,
        out_shape=[jax.ShapeDtypeStruct((rows, cols), f32)] * 3,
    )(w, g, m, v)


def adamw_many(ws, gs, ms, vs, *, name):
    n = len(ws)

    def body(*refs):
        for i in range(n):
            d, nm, nv = _adam_math(refs[i][...], refs[n + i][...], refs[2 * n + i][...], refs[3 * n + i][...])
            refs[4 * n + 3 * i][...] = d
            refs[4 * n + 3 * i + 1][...] = nm
            refs[4 * n + 3 * i + 2][...] = nv

    outs = pl.pallas_call(
        body, name=name, out_shape=[jax.ShapeDtypeStruct(w.shape, f32) for w in ws for _ in range(3)],
    )(*ws, *gs, *ms, *vs)
    return [tuple(outs[3 * i:3 * i + 3]) for i in range(n)]


def pair_add(dw, rbuf, idx, *, name):
    n, rows, cols = dw.shape
    hr = rows // 2
    tr = _row_tile(hr, cols)
    nrt = hr // tr

    def body(idx_ref, a_ref, b_ref, o_ref, own_ref):
        s = a_ref[...] + b_ref[...]
        o_ref[...] = s.astype(bf16)

        @pl.when(pl.program_id(1) == idx_ref[0])
        def _():
            own_ref[...] = s[0]

    return pl.pallas_call(
        body, name=name,
        grid_spec=pltpu.PrefetchScalarGridSpec(
            num_scalar_prefetch=1, grid=(nrt, n),
            in_specs=[pl.BlockSpec((1, tr, cols), lambda i, k, idx: (k, idx[1] * nrt + i, 0)),
                      pl.BlockSpec((1, tr, cols), lambda i, k, idx: (k, i, 0))],
            out_specs=[pl.BlockSpec((1, tr, cols), lambda i, k, idx: (k, i, 0)),
                       pl.BlockSpec((tr, cols), lambda i, k, idx: (i, 0))]),
        out_shape=[jax.ShapeDtypeStruct((n, hr, cols), bf16), jax.ShapeDtypeStruct((hr, cols), f32)],
    )(idx, dw, rbuf)


def chip_sum(own, rbuf, idx, *, name):
    hr, cols = own.shape
    tr = _row_tile(hr, cols)
    nrt = hr // tr

    def body(idx_ref, a_ref, b_ref, o_ref):
        o_ref[...] = ((a_ref[...] + b_ref[0].astype(f32)) + b_ref[1].astype(f32)) + b_ref[2].astype(f32)

    return pl.pallas_call(
        body, name=name,
        grid_spec=pltpu.PrefetchScalarGridSpec(
            num_scalar_prefetch=1, grid=(nrt,),
            in_specs=[pl.BlockSpec((tr, cols), lambda i, idx: (i, 0)),
                      pl.BlockSpec((3, tr, cols), lambda i, idx: (0, i, 0))],
            out_specs=pl.BlockSpec((tr, cols), lambda i, idx: (idx[1] * nrt + i, 0))),
        out_shape=jax.ShapeDtypeStruct((2 * hr, cols), f32),
    )(idx, own, rbuf)


def sum8(rbuf, *, name):
    n, rows, cols = rbuf.shape
    tr = _row_tile(rows, cols * n)

    def body(a_ref, o_ref):
        acc = a_ref[0]
        for k in range(1, n):
            acc = acc + a_ref[k]
        o_ref[...] = acc

    return pl.pallas_call(
        body, name=name, grid=(rows // tr,), in_specs=[pl.BlockSpec((n, tr, cols), lambda i: (0, i, 0))],
        out_specs=pl.BlockSpec((tr, cols), lambda i: (i, 0)), out_shape=jax.ShapeDtypeStruct((rows, cols), f32),
    )(rbuf)


def _coords():
    return lax.axis_index("x"), lax.axis_index("y"), lax.axis_index("c")


def _other_chips(x, y):
    return [(1 - x, y), (x, 1 - y), (1 - x, 1 - y)]


def gather_weights(shards, *, name):
    n = len(shards)
    halves = [s.shape[0] // 2 for s in shards]

    def body(*refs):
        ins, outs = refs[:n], refs[n:2 * n]
        send1, recv1, send2, recv2 = refs[2 * n:]
        x, y, c = _coords()
        me = 2 * x + y
        chips = _other_chips(x, y)
        sibling = (x, y, 1 - c)

        def half(i, k, hc):
            return outs[i].at[k, pl.ds(hc * halves[i], halves[i]), :]

        def ici(i, j):
            return pltpu.make_async_remote_copy(
                src_ref=ins[i].at[pl.ds(c * halves[i], halves[i]), :], dst_ref=half(i, me, c),
                send_sem=send1.at[i, j], recv_sem=recv1.at[i, j], device_id=(*chips[j], c), device_id_type=MESH)

        def landed(i, j):
            kj = 2 * chips[j][0] + chips[j][1]
            return pltpu.make_async_remote_copy(
                src_ref=half(i, kj, c), dst_ref=half(i, kj, c),
                send_sem=send2.at[i, j], recv_sem=recv1.at[i, j], device_id=sibling, device_id_type=MESH)

        def from_sibling(i, j):
            kj = 2 * chips[j][0] + chips[j][1]
            return pltpu.make_async_remote_copy(
                src_ref=half(i, kj, 1 - c), dst_ref=half(i, kj, 1 - c),
                send_sem=send2.at[i, j], recv_sem=recv2.at[i, j], device_id=sibling, device_id_type=MESH)

        def d2d(i, j):
            kj = 2 * chips[j][0] + chips[j][1]
            return pltpu.make_async_remote_copy(
                src_ref=half(i, kj, c), dst_ref=half(i, kj, c),
                send_sem=send2.at[i, j], recv_sem=recv2.at[i, j], device_id=sibling, device_id_type=MESH)

        for j in range(3):
            for i in range(n):
                ici(i, j).start()
        for j in range(3):
            for i in range(n):
                landed(i, j).wait_recv()
                d2d(i, j).start()
        for j in range(3):
            for i in range(n):
                from_sibling(i, j).wait_recv()
        for j in range(3):
            for i in range(n):
                ici(i, j).wait_send()
                d2d(i, j).wait_send()

    return pl.pallas_call(
        body, name=name, in_specs=[ANY] * n, out_specs=[ANY] * n,
        out_shape=[jax.ShapeDtypeStruct((N_CHIPS,) + s.shape, s.dtype) for s in shards],
        scratch_shapes=[pltpu.SemaphoreType.DMA((n, 3))] * 4,
    )(*shards)


def pair_exchange(grads, *, name):
    n = len(grads)
    halves = [g.shape[1] // 2 for g in grads]

    def body(*refs):
        ins, outs = refs[:n], refs[n:2 * n]
        send, recv = refs[2 * n:]
        x, y, c = _coords()
        cps = [pltpu.make_async_remote_copy(
            src_ref=ins[i].at[:, pl.ds((1 - c) * halves[i], halves[i]), :], dst_ref=outs[i],
            send_sem=send.at[i], recv_sem=recv.at[i], device_id=(x, y, 1 - c), device_id_type=MESH) for i in range(n)]
        for cp in cps:
            cp.start()
        for cp in cps:
            cp.wait()

    return pl.pallas_call(
        body, name=name, in_specs=[ANY] * n, out_specs=[ANY] * n,
        out_shape=[jax.ShapeDtypeStruct((N_CHIPS, g.shape[1] // 2, g.shape[2]), g.dtype) for g in grads],
        scratch_shapes=[pltpu.SemaphoreType.DMA((n,))] * 2,
    )(*grads)


def pair_gather(bufs, *, name):
    n = len(bufs)

    def body(*refs):
        ins, outs = refs[:n], refs[n:2 * n]
        send, recv = refs[2 * n:]
        x, y, c = _coords()
        cps = []
        for i in range(n):
            hr = ins[i].shape[0] // 2
            cps.append(pltpu.make_async_remote_copy(
                src_ref=ins[i].at[pl.ds(c * hr, hr), :], dst_ref=outs[i].at[pl.ds(c * hr, hr), :],
                send_sem=send.at[i], recv_sem=recv.at[i], device_id=(x, y, 1 - c), device_id_type=MESH))
        for cp in cps:
            cp.start()
        for i in range(n):
            hr = ins[i].shape[0] // 2
            pltpu.make_async_remote_copy(
                src_ref=ins[i].at[pl.ds((1 - c) * hr, hr), :], dst_ref=outs[i].at[pl.ds((1 - c) * hr, hr), :],
                send_sem=send.at[i], recv_sem=recv.at[i], device_id=(x, y, 1 - c), device_id_type=MESH).wait_recv()
        for cp in cps:
            cp.wait_send()

    return pl.pallas_call(
        body, name=name, in_specs=[ANY] * n, out_specs=[ANY] * n,
        out_shape=[jax.ShapeDtypeStruct(b.shape, b.dtype) for b in bufs],
        input_output_aliases={i: i for i in range(n)},
        scratch_shapes=[pltpu.SemaphoreType.DMA((n,))] * 2,
    )(*bufs)


def all_exchange(buf, *, name):
    rows, cols = buf.shape

    def body(in_ref, out_ref, send, recv):
        x, y, c = _coords()
        me = 4 * x + 2 * y + c
        cps = []
        for d in range(1, 8):
            px = 1 - x if d & 4 else x
            py = 1 - y if d & 2 else y
            pc = 1 - c if d & 1 else c
            cps.append(pltpu.make_async_remote_copy(
                src_ref=in_ref, dst_ref=out_ref.at[me], send_sem=send.at[d - 1], recv_sem=recv.at[d - 1],
                device_id=(px, py, pc), device_id_type=MESH))
        for cp in cps:
            cp.start()
        for d in range(1, 8):
            px = 1 - x if d & 4 else x
            py = 1 - y if d & 2 else y
            pc = 1 - c if d & 1 else c
            src = 4 * px + 2 * py + pc
            pltpu.make_async_remote_copy(
                src_ref=in_ref, dst_ref=out_ref.at[src], send_sem=send.at[d - 1], recv_sem=recv.at[d - 1],
                device_id=(px, py, pc), device_id_type=MESH).wait_recv()
        for cp in cps:
            cp.wait_send()

    return pl.pallas_call(
        body, name=name, in_specs=[ANY], out_specs=ANY,
        out_shape=jax.ShapeDtypeStruct((8, rows, cols), buf.dtype),
        scratch_shapes=[pltpu.SemaphoreType.DMA((7,)), pltpu.SemaphoreType.DMA((7,))],
    )(buf)


HBM = pl.BlockSpec(memory_space=pltpu.HBM)
SEM = pl.BlockSpec(memory_space=pltpu.SEMAPHORE)
EFFECT = pltpu.SideEffectType.DATAFLOW_SIDE_EFFECTING


def split_start(arrays, after, copies, sem_shape, *, name):
    na = len(arrays)

    def body(*refs):
        for cp in copies(refs[:na], refs[na + 1], refs[na + 2]):
            cp.start()
        refs[-1][...] = jnp.zeros((8, 128), f32)

    outs = pl.pallas_call(
        body, name=name,
        out_shape=(pltpu.SemaphoreType.DMA(sem_shape), pltpu.SemaphoreType.DMA(sem_shape),
                   *[pltpu.HBM(a.shape, a.dtype) for a in arrays], jax.ShapeDtypeStruct((8, 128), f32)),
        in_specs=[HBM] * na + [ANY], out_specs=(SEM, SEM, *[HBM] * na, pl.BlockSpec(memory_space=pltpu.VMEM)),
        input_output_aliases={i: 2 + i for i in range(na)},
        compiler_params=pltpu.CompilerParams(has_side_effects=EFFECT),
    )(*[pltpu.with_memory_space_constraint(a, pltpu.HBM) for a in arrays], after)
    return outs[0], outs[1], list(outs[2:2 + na]), outs[-1]


def split_wait(send, recv, arrays, after, copies, *, name):
    na = len(arrays)

    def body(*refs):
        for cp in copies(refs[:na], refs[na], refs[na + 1]):
            cp.wait_send()
            cp.wait_recv()

    outs = pl.pallas_call(
        body, name=name, out_shape=tuple(pltpu.HBM(a.shape, a.dtype) for a in arrays),
        in_specs=[HBM] * na + [SEM, SEM, ANY], out_specs=tuple([HBM] * na),
        input_output_aliases={i: i for i in range(na)},
        compiler_params=pltpu.CompilerParams(has_side_effects=EFFECT),
    )(*arrays, send, recv, after)
    return list(outs)


def gather_copies(n):
    def copies(refs, send, recv):
        x, y, c = _coords()
        me = 2 * x + y
        chips = _other_chips(x, y)
        return [pltpu.make_async_remote_copy(
            src_ref=refs[i], dst_ref=refs[n + i].at[me], send_sem=send.at[3 * i + j], recv_sem=recv.at[3 * i + j],
            device_id=(*chips[j], c), device_id_type=MESH) for j in range(3) for i in range(n)]
    return copies


def all_copies():
    def copies(refs, send, recv):
        x, y, c = _coords()
        me = 4 * x + 2 * y + c
        cps = []
        for d in range(1, 8):
            peer = (1 - x if d & 4 else x, 1 - y if d & 2 else y, 1 - c if d & 1 else c)
            cps.append(pltpu.make_async_remote_copy(
                src_ref=refs[0], dst_ref=refs[1].at[me], send_sem=send.at[d - 1], recv_sem=recv.at[d - 1],
                device_id=peer, device_id_type=MESH))
        return cps
    return copies


def reduce_copies(n):
    def copies(refs, send, recv):
        x, y, c = _coords()
        chips = _other_chips(x, y)
        return [pltpu.make_async_remote_copy(
            src_ref=refs[i].at[2 * chips[j][0] + chips[j][1]], dst_ref=refs[n + i].at[j],
            send_sem=send.at[3 * i + j], recv_sem=recv.at[3 * i + j], device_id=(*chips[j], c), device_id_type=MESH)
            for j in range(3) for i in range(n)]
    return copies


def _pack(arrs):
    flat = []
    for a in arrs:
        v = a.reshape(-1).astype(f32)
        pad = (-v.shape[0]) % 128
        flat.append(jnp.pad(v, (0, pad)) if pad else v)
    v = jnp.concatenate(flat)
    rows = v.shape[0] // 128
    pad_rows = (-rows) % 256
    v = v.reshape(rows, 128)
    return jnp.pad(v, ((0, pad_rows), (0, 0))) if pad_rows else v


def _unpack(buf, shapes):
    out, row = [], 0
    for s in shapes:
        size = math.prod(s)
        rows = -(-size // 128)
        out.append(buf[row:row + rows].reshape(-1)[:size].reshape(s))
        row += rows
    return out


def _perm_in_cols(w):
    gates, z = w[..., 0:2048], w[..., 2048:4096]
    xbc = w[..., 4096:7168]
    dt, lx, ly = w[..., 7168:7200], w[..., 7200:8480], w[..., 8480:9760]
    pad = jnp.zeros(w.shape[:-1] + (DT_PAD_W - SSM_HEADS,), w.dtype)
    return jnp.concatenate([gates, z, lx, ly, dt, pad, _perm_xbc_cols(xbc)], axis=-1)


def _perm_xbc_cols(w):
    parts = []
    for g in range(SSM_GROUPS):
        parts += [w[..., g * 512:(g + 1) * 512], w[..., 2048 + g * 128:2048 + (g + 1) * 128],
                  w[..., 2560 + g * 128:2560 + (g + 1) * 128]]
    return jnp.concatenate(parts, axis=-1)


def _unperm_xbc_cols(w):
    xs = [w[..., g * XBC_GROUP_W:g * XBC_GROUP_W + 512] for g in range(SSM_GROUPS)]
    bs = [w[..., g * XBC_GROUP_W + 512:g * XBC_GROUP_W + 640] for g in range(SSM_GROUPS)]
    cs = [w[..., g * XBC_GROUP_W + 640:(g + 1) * XBC_GROUP_W] for g in range(SSM_GROUPS)]
    return jnp.concatenate(xs + bs + cs, axis=-1)


def _unperm_in_cols(w):
    xbc = _unperm_xbc_cols(w[..., OFF_XBC:OFF_XBC + 3072])
    return jnp.concatenate([w[..., OFF_GATES:OFF_GATES + 2048], w[..., OFF_Z:OFF_Z + 2048], xbc,
                            w[..., OFF_DT:OFF_DT + 32], w[..., OFF_LX:OFF_LX + 1280], w[..., OFF_LY:OFF_LY + 1280]], axis=-1)


def _col_shards(w, n=N_CHIPS):
    r, c = w.shape
    return jnp.transpose(w.reshape(r, n, c // n), (1, 0, 2))


def _from_col_shards(w):
    n, r, c = w.shape
    return jnp.transpose(w, (1, 0, 2)).reshape(r, n * c)


def kernel(x, norm1_w, w_in, b_branch_gate, ssm_conv_w, ssm_conv_b, ssm_dt_bias, ssm_a_log, ssm_d, ssm_norm_w, w_out_ssm, lru_conv_w, lru_conv_b, lru_w_r, lru_b_r, lru_w_i, lru_b_i, lru_lambda, w_out_lru, w_out, norm2_w, w_ffn_in, w_ffn_out, norm_f_w, loss_target, m_norm1_w, m_w_in, m_b_branch_gate, m_ssm_conv_w, m_ssm_conv_b, m_ssm_dt_bias, m_ssm_a_log, m_ssm_d, m_ssm_norm_w, m_w_out_ssm, m_lru_conv_w, m_lru_conv_b, m_lru_w_r, m_lru_b_r, m_lru_w_i, m_lru_b_i, m_lru_lambda, m_w_out_lru, m_w_out, m_norm2_w, m_w_ffn_in, m_w_ffn_out, m_norm_f_w, v_norm1_w, v_w_in, v_b_branch_gate, v_ssm_conv_w, v_ssm_conv_b, v_ssm_dt_bias, v_ssm_a_log, v_ssm_d, v_ssm_norm_w, v_w_out_ssm, v_lru_conv_w, v_lru_conv_b, v_lru_w_r, v_lru_b_r, v_lru_w_i, v_lru_b_i, v_lru_lambda, v_w_out_lru, v_w_out, v_norm2_w, v_w_ffn_in, v_w_ffn_out, v_norm_f_w):
    xi, yi, ci = lax.axis_index("x"), lax.axis_index("y"), lax.axis_index("c")
    me = 2 * xi + yi
    idx = jnp.stack([me, ci]).astype(jnp.int32)
    x2 = x[0]
    tgt = loss_target[0]

    big_names = ["w_in", "w_out_ssm", "w_out_lru", "w_out", "w_ffn_in", "w_ffn_out"]
    big_w = dict(w_in=w_in[0], w_out_ssm=w_out_ssm[0], w_out_lru=w_out_lru[0], w_out=w_out[0], w_ffn_in=w_ffn_in[0],
                 w_ffn_out=w_ffn_out[0])
    big_m = dict(w_in=m_w_in[0], w_out_ssm=m_w_out_ssm[0], w_out_lru=m_w_out_lru[0], w_out=m_w_out[0],
                 w_ffn_in=m_w_ffn_in[0], w_ffn_out=m_w_ffn_out[0])
    big_v = dict(w_in=v_w_in[0], w_out_ssm=v_w_out_ssm[0], w_out_lru=v_w_out_lru[0], w_out=v_w_out[0],
                 w_ffn_in=v_w_ffn_in[0], w_ffn_out=v_w_ffn_out[0])
    conv_pad = jnp.zeros((16, 768), f32).at[0:4, :].set(ssm_conv_w[0]).at[8:12, 0:320].set(lru_conv_w[0])
    mine = [big_w["w_in"].astype(bf16), conv_pad]
    gathered = gather_weights(mine, name="gather_weights")
    g_in, g_conv = [lax.dynamic_update_index_in_dim(g, s, me, 0) for g, s in zip(gathered, mine)]
    w_in_p = _perm_in_cols(_from_col_shards(g_in))
    late_names = big_names[1:]
    late = [big_w[k].astype(bf16) for k in late_names]
    late_lands = [lax.empty((N_CHIPS,) + s.shape, bf16) for s in late]
    g_send, g_recv, g_arrays, g_token = split_start(late + late_lands, g_conv, gather_copies(5), (15,),
                                                    name="gather_late_start")
    ssm_cw_full = _from_col_shards(g_conv[:, 0:4, :])
    lru_cw_full = _from_col_shards(g_conv[:, 8:12, 0:320])
    ssm_cw_p = _perm_xbc_cols(ssm_cw_full)
    ssm_cb_p = _perm_xbc_cols(ssm_conv_b)

    par = jnp.stack([ssm_dt_bias[0], ssm_a_log[0], ssm_d[0]], axis=0).reshape(3, SSM_GROUPS, SSM_HPG)
    par_row = jnp.zeros((SSM_GROUPS, 8, 8), f32).at[:, 0:3, :].set(jnp.transpose(par, (1, 0, 2)))
    par_col = jnp.transpose(par_row, (0, 2, 1))

    hn1 = rms_fwd(x2, norm1_w + g_token[0:1, 0:1], name="rms1_fwd")
    proj = mm(hn1, w_in_p, "nn", name="in_proj")
    t = x2.shape[0]
    dtr = jnp.transpose(proj[:, OFF_DT:OFF_DT + 32].reshape(t, SSM_GROUPS, SSM_HPG), (1, 0, 2))
    dtr_t = jnp.transpose(dtr, (0, 2, 1))
    xbc_pre, xbc_post = conv_fwd(proj, OFF_XBC, SSM_CONV_DIM, ssm_cw_p, ssm_cb_p, silu=True, name="ssm_conv_fwd")
    y_ssd, s_in = ssd_fwd(xbc_post, dtr, dtr_t, par_row, par_col, name="ssd_fwd")
    yn = gnorm_fwd(y_ssd, proj, ssm_norm_w, name="gnorm_fwd")
    g_arrays = split_wait(g_send, g_recv, g_arrays, yn, gather_copies(5), name="gather_late_wait")
    g_out_ssm, g_out_lru, g_out, g_ffn_in, g_ffn_out = [
        lax.dynamic_update_index_in_dim(g, s, me, 0) for g, s in zip(g_arrays[5:], late)]
    w_out_ssm_f = g_out_ssm.reshape(SSM_D_INNER, D_MODEL)
    w_out_lru_f = g_out_lru.reshape(LRU_WIDTH, D_MODEL)
    w_out_f = g_out.reshape(D_MODEL, D_MODEL)
    w_ffn_in_f = _from_col_shards(g_ffn_in)
    w_ffn_out_f = g_ffn_out.reshape(FFN_HIDDEN, D_MODEL)
    y_ssm = mm(yn, w_out_ssm_f, "nn", name="out_ssm")
    (u_lru,) = conv_fwd(proj, OFF_LX, LRU_WIDTH, lru_cw_full, lru_conv_b, silu=False, name="lru_conv_fwd")
    h_lru, o_lru = lru_fwd(u_lru, proj, lru_w_r[0], lru_b_r, lru_w_i[0], lru_b_i, lru_lambda, name="lru_fwd")
    y_lru = mm(o_lru, w_out_lru_f, "nn", name="out_lru")
    mix = merge_fwd(proj, b_branch_gate, y_ssm, y_lru, name="merge_fwd")
    h1 = mm(mix, w_out_f, "nn", add=x2, name="out_proj")
    hn2 = rms_fwd(h1, norm2_w, name="rms2_fwd")
    ff = mm(hn2, w_ffn_in_f, "nn", name="ffn_in")
    act = swiglu_fwd(ff, name="swiglu_fwd")
    h2 = mm(act, w_ffn_out_f, "nn", add=h1, name="ffn_out")
    loss_tile, dh2, d_norm_f = loss_head(h2, norm_f_w.reshape(1, D_MODEL), tgt, name="loss_head")
    loss = lax.psum(loss_tile[0, 0], ("x", "y", "c"))

    d_w_ffn_out = mm(act, dh2, "tn", name="d_w_ffn_out")
    dact = mm(dh2, w_ffn_out_f, "nt", name="d_act")
    dff = swiglu_bwd(ff, dact, name="swiglu_bwd")
    d_w_ffn_in = mm(hn2, dff, "tn", name="d_w_ffn_in")
    dhn2 = mm(dff, w_ffn_in_f, "nt", name="d_hn2")
    dh1, d_norm2 = rms_bwd(h1, norm2_w, dhn2, dh2, name="rms2_bwd")
    d_w_out = mm(mix, dh1, "tn", name="d_w_out")
    dmix = mm(dh1, w_out_f, "nt", name="d_mix")
    dproj, dy_ssm, dy_lru, d_bg = merge_bwd(proj, b_branch_gate, y_ssm, y_lru, dmix, name="merge_bwd")
    d_w_out_ssm = mm(yn, dy_ssm, "tn", name="d_w_out_ssm")
    d_w_out_lru = mm(o_lru, dy_lru, "tn", name="d_w_out_lru")
    early_g = [d_w_out_ssm.reshape(N_CHIPS, 512, D_MODEL), d_w_out_lru.reshape(N_CHIPS, 320, D_MODEL),
               d_w_out.reshape(N_CHIPS, 256, D_MODEL), _col_shards(d_w_ffn_in), d_w_ffn_out.reshape(N_CHIPS, 704, D_MODEL)]
    e_sib = pair_exchange(early_g, name="pair_exchange_early")
    e_pairs = [pair_add(g, rb, idx, name="pair_add_" + k) for g, rb, k in zip(early_g, e_sib, late_names)]
    e_lands = [lax.empty((3,) + p[0].shape[1:], bf16) for p in e_pairs]
    e_send, e_recv, e_arrays, e_token = split_start([p[0] for p in e_pairs] + e_lands, e_pairs[0][1], reduce_copies(5),
                                                    (15,), name="reduce_early_start")
    dyn = mm(dy_ssm, w_out_ssm_f, "nt", name="d_yn")
    dy_ssd, dproj, d_ssm_norm = gnorm_bwd(y_ssd, proj, ssm_norm_w + e_token[0:1, 0:1], dyn, dproj, name="gnorm_bwd")
    dxbc_post, ddtr, dpar = ssd_bwd(xbc_post, dtr, dtr_t, par_row, par_col, s_in, dy_ssd, name="ssd_bwd")
    dproj, d_ssm_cw_p, d_ssm_cb_p = conv_bwd(dxbc_post, xbc_pre, proj, OFF_XBC, ssm_cw_p, dproj, name="ssm_conv_bwd")
    do_lru = mm(dy_lru, w_out_lru_f, "nt", name="d_o_lru")
    du_lru, dproj, d_w_r, d_w_i, d_b_r, d_b_i, d_lam = lru_bwd(u_lru, proj, h_lru, do_lru, lru_w_r[0], lru_b_r, lru_w_i[0],
                                                               lru_b_i, lru_lambda, dproj, name="lru_bwd")
    dproj, d_lru_cw, d_lru_cb = conv_bwd(du_lru, None, proj, OFF_LX, lru_cw_full, dproj, name="lru_conv_bwd")
    ddt_cols = jnp.transpose(ddtr, (1, 0, 2)).reshape(t, SSM_HEADS).astype(bf16)
    ddt_cols = jnp.pad(ddt_cols, ((0, 0), (0, DT_PAD_W - SSM_HEADS)))
    dproj = lax.dynamic_update_slice(dproj, ddt_cols, (0, OFF_DT))

    d_ssm_cw = _unperm_xbc_cols(d_ssm_cw_p)
    d_ssm_cb = _unperm_xbc_cols(d_ssm_cb_p)
    dpar_h = jnp.transpose(dpar[:, 0:3, :], (1, 0, 2)).reshape(3, SSM_HEADS)
    small_names = ["norm1_w", "b_branch_gate", "ssm_conv_b", "ssm_dt_bias", "ssm_a_log", "ssm_d", "ssm_norm_w",
                   "lru_conv_b", "lru_w_r", "lru_b_r", "lru_w_i", "lru_b_i", "lru_lambda", "norm2_w", "norm_f_w"]
    small_g = dict(norm1_w=jnp.zeros_like(norm1_w), b_branch_gate=d_bg, ssm_conv_b=d_ssm_cb, ssm_dt_bias=dpar_h[0:1], ssm_a_log=dpar_h[1:2],
                   ssm_d=dpar_h[2:3], ssm_norm_w=d_ssm_norm, lru_conv_b=d_lru_cb, lru_w_r=d_w_r[None], lru_b_r=d_b_r,
                   lru_w_i=d_w_i[None], lru_b_i=d_b_i, lru_lambda=d_lam, norm2_w=d_norm2, norm_f_w=d_norm_f.reshape(D_MODEL))
    small_w = dict(norm1_w=norm1_w, b_branch_gate=b_branch_gate, ssm_conv_b=ssm_conv_b, ssm_dt_bias=ssm_dt_bias,
                   ssm_a_log=ssm_a_log, ssm_d=ssm_d, ssm_norm_w=ssm_norm_w, lru_conv_b=lru_conv_b, lru_w_r=lru_w_r,
                   lru_b_r=lru_b_r, lru_w_i=lru_w_i, lru_b_i=lru_b_i, lru_lambda=lru_lambda, norm2_w=norm2_w, norm_f_w=norm_f_w)
    small_m = dict(norm1_w=m_norm1_w, b_branch_gate=m_b_branch_gate, ssm_conv_b=m_ssm_conv_b, ssm_dt_bias=m_ssm_dt_bias,
                   ssm_a_log=m_ssm_a_log, ssm_d=m_ssm_d, ssm_norm_w=m_ssm_norm_w, lru_conv_b=m_lru_conv_b, lru_w_r=m_lru_w_r,
                   lru_b_r=m_lru_b_r, lru_w_i=m_lru_w_i, lru_b_i=m_lru_b_i, lru_lambda=m_lru_lambda, norm2_w=m_norm2_w,
                   norm_f_w=m_norm_f_w)
    small_v = dict(norm1_w=v_norm1_w, b_branch_gate=v_b_branch_gate, ssm_conv_b=v_ssm_conv_b, ssm_dt_bias=v_ssm_dt_bias,
                   ssm_a_log=v_ssm_a_log, ssm_d=v_ssm_d, ssm_norm_w=v_ssm_norm_w, lru_conv_b=v_lru_conv_b, lru_w_r=v_lru_w_r,
                   lru_b_r=v_lru_b_r, lru_w_i=v_lru_w_i, lru_b_i=v_lru_b_i, lru_lambda=v_lru_lambda, norm2_w=v_norm2_w,
                   norm_f_w=v_norm_f_w)
    shapes = [small_w[k].shape for k in small_names]
    conv_shapes = [(4, SSM_CONV_DIM), (4, LRU_WIDTH)]
    g_pack = _pack([small_g[k] for k in small_names] + [d_ssm_cw, d_lru_cw])
    s_send, s_recv, s_arrays, s_token = split_start([g_pack, lax.empty((8,) + g_pack.shape, f32)], g_pack, all_copies(),
                                                    (7,), name="small_start")
    d_w_in_p = mm(hn1, dproj, "tn", after=s_token, name="d_w_in")

    d_w_in_s = _col_shards(_unperm_in_cols(d_w_in_p))
    (l_sib,) = pair_exchange([d_w_in_s], name="pair_exchange_late")
    l_pair = pair_add(d_w_in_s, l_sib, idx, name="pair_add_w_in")
    l_land = lax.empty((3,) + l_pair[0].shape[1:], bf16)
    l_send, l_recv, l_arrays, l_token = split_start([l_pair[0], l_land], l_pair[1], reduce_copies(1), (3,),
                                                    name="reduce_late_start")
    dhn1 = mm(dproj, w_in_p, "nt", after=l_token, name="d_hn1")
    grad_x, d_norm1 = rms_bwd(x2, norm1_w, dhn1, dh1, name="rms1_bwd")

    e_arrays = split_wait(e_send, e_recv, e_arrays, d_norm1, reduce_copies(5), name="reduce_early_wait")
    e_half = [chip_sum(p[1], rb, idx, name="chip_sum_" + k) for p, rb, k in zip(e_pairs, e_arrays[5:], late_names)]
    big_out = {}
    for k, g in zip(late_names, pair_gather(e_half, name="pair_gather_early")):
        big_out[k] = (g,) + tuple(adamw(big_w[k], g, big_m[k], big_v[k], name="adamw_" + k))

    s_arrays = split_wait(s_send, s_recv, s_arrays, d_norm1, all_copies(), name="small_wait")
    g_sum = sum8(lax.dynamic_update_index_in_dim(s_arrays[1], g_pack, 2 * me + ci, 0), name="sum8")
    n1 = d_norm1.reshape(8, 128)
    n1_sum = sum8(lax.dynamic_update_index_in_dim(all_exchange(n1, name="all_exchange_norm1"), n1, 2 * me + ci, 0),
                  name="sum8_norm1")
    g_sum = lax.dynamic_update_slice(g_sum, n1_sum, (0, 0))
    g_small = _unpack(g_sum, shapes + conv_shapes)
    g_small[-2] = lax.dynamic_slice_in_dim(g_small[-2], me * 768, 768, axis=1)
    g_small[-1] = lax.dynamic_slice_in_dim(g_small[-1], me * 320, 320, axis=1)
    all_names = small_names + ["ssm_conv_w", "lru_conv_w"]
    small_w.update(ssm_conv_w=ssm_conv_w[0], lru_conv_w=lru_conv_w[0])
    small_m.update(ssm_conv_w=m_ssm_conv_w[0], lru_conv_w=m_lru_conv_w[0])
    small_v.update(ssm_conv_w=v_ssm_conv_w[0], lru_conv_w=v_lru_conv_w[0])
    as2d = lambda a: a.reshape(-1, a.shape[-1])
    upd = adamw_many([as2d(small_w[k]) for k in all_names], [as2d(g) for g in g_small],
                     [as2d(small_m[k]) for k in all_names], [as2d(small_v[k]) for k in all_names], name="adamw_small")
    small_out = {}
    for k, g, u in zip(all_names, g_small, upd):
        small_out[k] = (g,) + tuple(o.reshape(g.shape) for o in u)
    l_arrays = split_wait(l_send, l_recv, l_arrays, upd[0][0], reduce_copies(1), name="reduce_late_wait")
    l_half = chip_sum(l_pair[1], l_arrays[1], idx, name="chip_sum_w_in")
    (g_w_in,) = pair_gather([l_half], name="pair_gather_late")
    big_out["w_in"] = (g_w_in,) + tuple(adamw(big_w["w_in"], g_w_in, big_m["w_in"], big_v["w_in"], name="adamw_w_in"))

    order = ["norm1_w", "w_in", "b_branch_gate", "ssm_conv_w", "ssm_conv_b", "ssm_dt_bias", "ssm_a_log", "ssm_d", "ssm_norm_w",
             "w_out_ssm", "lru_conv_w", "lru_conv_b", "lru_w_r", "lru_b_r", "lru_w_i", "lru_b_i", "lru_lambda", "w_out_lru",
             "w_out", "norm2_w", "w_ffn_in", "w_ffn_out", "norm_f_w"]
    outs = [loss, grad_x[None]]
    for which in range(4):
        for k in order:
            if k in big_out:
                outs.append(big_out[k][which][None])
            elif k in ("ssm_conv_w", "lru_conv_w"):
                outs.append(small_out[k][which][None])
            else:
                outs.append(small_out[k][which])
    return tuple(outs)
```

```python
import functools
import math

import jax
import jax.numpy as jnp
from jax import lax
from jax.experimental import pallas as pl
from jax.experimental.pallas import tpu as pltpu

f32 = jnp.float32
bf16 = jnp.bfloat16

D_MODEL = 1024
SSM_D_INNER = 2048
SSM_HEADS = 32
SSM_HEAD_DIM = 64
SSM_GROUPS = 4
SSM_HPG = 8
SSM_D_STATE = 128
SSM_CHUNK = 128
SSM_GROUP_W = 512
SSM_CONV_DIM = 3072
XBC_GROUP_W = 768
LRU_WIDTH = 1280
LRU_BLOCKS = 10
LRU_BLOCK = 128
LRU_C = 8.0
FFN_HIDDEN = 2816
RMS_EPS = 1e-6
IN_PROJ_DIM = 9760
N_CHIPS = 4

OFF_GATES = 0
OFF_Z = 2048
OFF_LX = 4096
OFF_LY = 5376
OFF_DT = 6656
DT_PAD_W = 256
OFF_XBC = 6912
PROJ_W = 9984

ADAM_LR = 0.001
ADAM_B1 = 0.9
ADAM_B2 = 0.999
ADAM_EPS = 1e-08
ADAM_WD = 0.01
ADAM_STEP = 10

MESH = pl.DeviceIdType.MESH
ANY = pl.BlockSpec(memory_space=pl.ANY)

NN = (((1,), (0,)), ((), ()))
NT = (((1,), (1,)), ((), ()))
TN = (((0,), (0,)), ((), ()))


def _pick(n, cap, mult=128):
    best = None
    for t in range(mult, min(n, cap) + 1, mult):
        if n % t == 0:
            best = t
    return best if best is not None else n


def _sigmoid(x):
    return 1.0 / (1.0 + jnp.exp(-x))


def _softplus(x):
    return jnp.maximum(x, 0.0) + jnp.log(1.0 + jnp.exp(-jnp.abs(x)))


def _silu(x):
    return x * _sigmoid(x)


def _dsilu(x):
    s = _sigmoid(x)
    return s * (1.0 + x * (1.0 - s))


_GELU_K = math.sqrt(2.0 / math.pi)


def _gelu(x):
    return 0.5 * x * (1.0 + jnp.tanh(_GELU_K * (x + 0.044715 * x * x * x)))


def _dgelu(x):
    t = jnp.tanh(_GELU_K * (x + 0.044715 * x * x * x))
    return 0.5 * (1.0 + t) + 0.5 * x * (1.0 - t * t) * _GELU_K * (1.0 + 3.0 * 0.044715 * x * x)


def _expm1(x):
    poly = x * (1.0 + x * (0.5 + x * (1.0 / 6.0 + x * (1.0 / 24.0 + x * (1.0 / 120.0 + x * (1.0 / 720.0))))))
    return jnp.where(jnp.abs(x) < 0.1, poly, jnp.exp(x) - 1.0)


def _dot(a, b, dn):
    return lax.dot_general(a.astype(bf16), b.astype(bf16), dn, preferred_element_type=f32)


def _dot_01(a, b, dn, split, terms):
    r = a if split == 0 else b
    out = None
    for _ in range(terms):
        h = r.astype(bf16)
        r = r - h.astype(f32)
        d = lax.dot_general(h if split == 0 else a.astype(bf16), b.astype(bf16) if split == 0 else h, dn,
                            preferred_element_type=f32)
        out = d if out is None else out + d
    return out


MM_WHOLE_K = 2816
MM_VMEM_BUDGET = 40 * 2 ** 20

def mm(a, b, mode, *, name, add=None, after=None, out_dtype=f32):
    if mode == "nn":
        (m, k), (k2, n) = a.shape, b.shape
    elif mode == "nt":
        (m, k), (n, k2) = a.shape, b.shape
    else:
        (k, m), (k2, n) = a.shape, b.shape
    assert k == k2, (a.shape, b.shape, mode)
    tk = k if k <= MM_WHOLE_K else _pick(k, 1024 if mode == "tn" else 2048)
    tn = _pick(n, 1536)
    isz = lambda v: jnp.dtype(v.dtype).itemsize
    for cap in (1536, 1024, 512, 256):
        tm = _pick(m, cap)
        vmem = 2 * (tm * tk * isz(a) + tk * tn * isz(b) + tm * tn * (4 * int(add is not None) + jnp.dtype(out_dtype).itemsize))
        vmem += 4 * tm * tn * int(k > tk)
        if vmem <= MM_VMEM_BUDGET:
            break
    nk = k // tk
    dn = {"nn": NN, "nt": NT, "tn": TN}[mode]
    a_spec = pl.BlockSpec((tk, tm), lambda i, j, kk: (kk, i)) if mode == "tn" else pl.BlockSpec((tm, tk), lambda i, j, kk: (i, kk))
    b_spec = pl.BlockSpec((tn, tk), lambda i, j, kk: (j, kk)) if mode == "nt" else pl.BlockSpec((tk, tn), lambda i, j, kk: (kk, j))
    o_spec = pl.BlockSpec((tm, tn), lambda i, j, kk: (i, j))
    has_add = add is not None

    n_extra = int(has_add) + int(after is not None)

    def body(a_ref, b_ref, *rest):
        add_ref = rest[0] if has_add else None
        o_ref = rest[n_extra]

        def finish(r):
            if has_add:
                r = r + add_ref[...]
            o_ref[...] = r.astype(out_dtype)

        if nk == 1:
            finish(_dot(a_ref[...], b_ref[...], dn))
            return
        acc = rest[-1]
        kk = pl.program_id(2)

        @pl.when(kk == 0)
        def _():
            acc[...] = jnp.zeros_like(acc)

        acc[...] += _dot(a_ref[...], b_ref[...], dn)

        @pl.when(kk == nk - 1)
        def _():
            finish(acc[...])

    ins = [a, b] + ([add] if has_add else []) + ([after] if after is not None else [])
    in_specs = [a_spec, b_spec] + ([o_spec] if has_add else []) + ([ANY] if after is not None else [])
    return pl.pallas_call(
        body, name=name, grid=(m // tm, n // tn, nk), in_specs=in_specs, out_specs=o_spec,
        out_shape=jax.ShapeDtypeStruct((m, n), out_dtype),
        scratch_shapes=[pltpu.VMEM((tm, tn), f32)] if nk > 1 else [],
        compiler_params=pltpu.CompilerParams(dimension_semantics=("parallel", "parallel", "arbitrary")),
    )(*ins)


def rms_fwd(x, w, *, name):
    t, d = x.shape
    tr = _pick(t, 256, 8)

    def body(x_ref, w_ref, o_ref):
        xv = x_ref[...]
        r = lax.rsqrt(jnp.mean(xv * xv, axis=-1, keepdims=True) + RMS_EPS)
        o_ref[...] = (xv * r * w_ref[...]).astype(bf16)

    return pl.pallas_call(
        body, name=name, grid=(t // tr,),
        in_specs=[pl.BlockSpec((tr, d), lambda i: (i, 0)), pl.BlockSpec((1, d), lambda i: (0, 0))],
        out_specs=pl.BlockSpec((tr, d), lambda i: (i, 0)), out_shape=jax.ShapeDtypeStruct((t, d), bf16),
    )(x, w)


def _rms_bwd_math(xv, wv, dy):
    r = lax.rsqrt(jnp.mean(xv * xv, axis=-1, keepdims=True) + RMS_EPS)
    g = dy * wv
    dx = r * g - xv * (r * r * r) * jnp.mean(g * xv, axis=-1, keepdims=True)
    dw = jnp.sum(dy * xv * r, axis=0, keepdims=True)
    return dx, dw


def rms_bwd(x, w, dy, res, *, name):
    t, d = x.shape
    tr = _pick(t, 256, 8)

    def body(x_ref, w_ref, dy_ref, res_ref, dx_ref, dw_ref):
        dx, dw = _rms_bwd_math(x_ref[...], w_ref[...], dy_ref[...])
        dx_ref[...] = dx + res_ref[...]

        @pl.when(pl.program_id(0) == 0)
        def _():
            dw_ref[...] = jnp.zeros_like(dw_ref)

        dw_ref[...] += dw

    row = pl.BlockSpec((tr, d), lambda i: (i, 0))
    vec = pl.BlockSpec((1, d), lambda i: (0, 0))
    return pl.pallas_call(
        body, name=name, grid=(t // tr,), in_specs=[row, vec, row, row], out_specs=[row, vec],
        out_shape=[jax.ShapeDtypeStruct((t, d), f32), jax.ShapeDtypeStruct((1, d), f32)],
        compiler_params=pltpu.CompilerParams(dimension_semantics=("arbitrary",)),
    )(x, w, dy, res)


def loss_head(h, w, target, *, name):
    t, d = h.shape
    tr = _pick(t, 256, 8)

    def body(h_ref, w_ref, t_ref, loss_ref, dh_ref, dw_ref):
        xv, wv = h_ref[...], w_ref[...]
        r = lax.rsqrt(jnp.mean(xv * xv, axis=-1, keepdims=True) + RMS_EPS)
        err = xv * r * wv - t_ref[...]
        part = 0.5 * jnp.sum(jnp.mean(err * err, axis=-1, keepdims=True), axis=0, keepdims=True)
        dx, dw = _rms_bwd_math(xv, wv, err * (1.0 / d))
        dh_ref[...] = dx

        @pl.when(pl.program_id(0) == 0)
        def _():
            dw_ref[...] = jnp.zeros_like(dw_ref)
            loss_ref[...] = jnp.zeros_like(loss_ref)

        dw_ref[...] += dw
        loss_ref[...] += part

    row = pl.BlockSpec((tr, d), lambda i: (i, 0))
    vec = pl.BlockSpec((1, d), lambda i: (0, 0))
    return pl.pallas_call(
        body, name=name, grid=(t // tr,), in_specs=[row, vec, row],
        out_specs=[pl.BlockSpec((8, 128), lambda i: (0, 0)), row, vec],
        out_shape=[jax.ShapeDtypeStruct((8, 128), f32), jax.ShapeDtypeStruct((t, d), f32), jax.ShapeDtypeStruct((1, d), f32)],
        compiler_params=pltpu.CompilerParams(dimension_semantics=("arbitrary",)),
    )(h, w, target)


CONV_ROWS = 512


def conv_fwd(src, col0, width, w, b, *, silu, name):
    t = src.shape[0]
    tc = _pick(math.gcd(width, col0), 768)
    assert col0 % tc == 0
    cb = col0 // tc
    r = CONV_ROWS

    def body(u_ref, w_ref, b_ref, *rest):
        ext = rest[-1]
        j = pl.program_id(1)

        @pl.when(j == 0)
        def _():
            ext[0:8, :] = jnp.zeros((8, tc), f32)

        @pl.when(j > 0)
        def _():
            ext[0:8, :] = ext[r:r + 8, :]

        ext[8:r + 8, :] = u_ref[...]
        v = ext[...]
        wv = w_ref[...]
        acc = b_ref[...] + wv[3:4, :] * v
        for s in (1, 2, 3):
            acc = acc + wv[3 - s:4 - s, :] * pltpu.roll(v, s, 0)
        pre = acc[8:, :]
        rest[0][...] = pre
        if silu:
            rest[1][...] = _silu(pre)

    tile = pl.BlockSpec((r, tc), lambda c, j: (j, c))
    n_out = 2 if silu else 1
    return pl.pallas_call(
        body, name=name, grid=(width // tc, t // r),
        in_specs=[pl.BlockSpec((r, tc), lambda c, j: (j, cb + c)), pl.BlockSpec((4, tc), lambda c, j: (0, c)),
                  pl.BlockSpec((1, tc), lambda c, j: (0, c))],
        out_specs=[tile] * n_out, out_shape=[jax.ShapeDtypeStruct((t, width), f32)] * n_out,
        scratch_shapes=[pltpu.VMEM((r + 8, tc), f32)],
        compiler_params=pltpu.CompilerParams(dimension_semantics=("parallel", "arbitrary")),
    )(src, w, b)


def conv_bwd(dpost, pre, src, col0, w, dst, *, name):
    t, width = dpost.shape
    tc = _pick(math.gcd(width, col0), 768)
    assert col0 % tc == 0
    cb = col0 // tc
    r = CONV_ROWS
    nt = t // r
    has_pre = pre is not None

    def body(*refs):
        refs = refs[1:]
        if has_pre:
            d_ref, p_ref, u_ref, w_ref, du_ref, dw_ref, db_ref, ext = refs
        else:
            d_ref, u_ref, w_ref, du_ref, dw_ref, db_ref, ext = refs
        j = pl.program_id(1)

        @pl.when(j == 0)
        def _():
            ext[r:r + 8, :] = jnp.zeros((8, tc), f32)
            dw_ref[...] = jnp.zeros_like(dw_ref)
            db_ref[...] = jnp.zeros_like(db_ref)

        @pl.when(j > 0)
        def _():
            ext[r:r + 8, :] = ext[0:8, :]

        dpre = d_ref[...]
        if has_pre:
            dpre = dpre * _dsilu(p_ref[...])
        ext[0:r, :] = dpre
        v = ext[...]
        wv = w_ref[...]
        uv = u_ref[...]
        du = wv[3:4, :] * dpre
        dw_ref[3:4, :] += jnp.sum(dpre * uv, axis=0, keepdims=True)
        for s in (1, 2, 3):
            sh = pltpu.roll(v, r + 8 - s, 0)[0:r, :]
            du = du + wv[3 - s:4 - s, :] * sh
            dw_ref[3 - s:4 - s, :] += jnp.sum(sh * uv, axis=0, keepdims=True)
        db_ref[...] += jnp.sum(dpre, axis=0, keepdims=True)
        du_ref[...] = du.astype(bf16)

    rev = pl.BlockSpec((r, tc), lambda c, j: (nt - 1 - j, c))
    win = pl.BlockSpec((r, tc), lambda c, j: (nt - 1 - j, cb + c))
    in_specs = [ANY, rev] + ([rev] if has_pre else []) + [win, pl.BlockSpec((4, tc), lambda c, j: (0, c))]
    ins = [dst, dpost] + ([pre] if has_pre else []) + [src, w]
    return pl.pallas_call(
        body, name=name, grid=(width // tc, nt), in_specs=in_specs,
        out_specs=[win, pl.BlockSpec((4, tc), lambda c, j: (0, c)), pl.BlockSpec((1, tc), lambda c, j: (0, c))],
        out_shape=[jax.ShapeDtypeStruct(dst.shape, bf16), jax.ShapeDtypeStruct((4, width), f32),
                   jax.ShapeDtypeStruct((1, width), f32)],
        input_output_aliases={0: 0},
        scratch_shapes=[pltpu.VMEM((r + 8, tc), f32)],
        compiler_params=pltpu.CompilerParams(dimension_semantics=("parallel", "arbitrary")),
    )(*ins)


def _ssd_common(xbc_ref, dtr_ref, dtrT_ref, par_row_ref, par_col_ref):
    l = SSM_CHUNK
    x = xbc_ref[:, 0:SSM_GROUP_W]
    bm = xbc_ref[:, SSM_GROUP_W:SSM_GROUP_W + SSM_D_STATE]
    cm = xbc_ref[:, SSM_GROUP_W + SSM_D_STATE:XBC_GROUP_W]
    par_row = par_row_ref[0]
    par_col = par_col_ref[0]
    bias_row, alog_row, d_row = par_row[0:1, :], par_row[1:2, :], par_row[2:3, :]
    bias_col, alog_col = par_col[:, 0:1], par_col[:, 1:2]
    dtr = dtr_ref[0]
    dt = _softplus(dtr + bias_row)
    dt_t = _softplus(dtrT_ref[0] + bias_col)
    a_row = -jnp.exp(alog_row)
    a_col = -jnp.exp(alog_col)
    li = lax.broadcasted_iota(jnp.int32, (l, l), 0)
    si = lax.broadcasted_iota(jnp.int32, (l, l), 1)
    tri = (li >= si).astype(f32)
    cs = _dot_01(tri, dt * a_row, NN, 1, 3)
    cs_t = _dot_01(dt_t * a_col, tri, NT, 0, 3)
    off = lax.broadcasted_iota(jnp.int32, (SSM_HPG, SSM_GROUP_W), 1) - SSM_HEAD_DIM * lax.broadcasted_iota(
        jnp.int32, (SSM_HPG, SSM_GROUP_W), 0)
    ex = ((off >= 0) & (off < SSM_HEAD_DIM)).astype(f32)
    cs_x = _dot_01(cs, ex, NN, 0, 3)
    cl_x = cs_x[l - 1:l, :]
    return dict(x=x, bm=bm, cm=cm, dtr=dtr, dt=dt, a_row=a_row, bias_row=bias_row, tri=tri, li=li, si=si, cs=cs,
                cs_t=cs_t, ex=ex, dt_x=_dot_01(dt, ex, NN, 0, 2), d_x=_dot_01(par_row, ex, NN, 0, 2)[2:3, :], e_x=jnp.exp(cs_x),
                el_x=jnp.exp(cl_x), dec_x=jnp.exp(cl_x - cs_x))


def ssd_fwd(xbc, dtr, dtr_t, par_row, par_col, *, name):
    t = xbc.shape[0]
    nc = t // SSM_CHUNK
    l, p = SSM_CHUNK, SSM_HEAD_DIM

    def body(xbc_ref, dtr_ref, dtrT_ref, prow_ref, pcol_ref, y_ref, sin_ref, state):
        @pl.when(pl.program_id(1) == 0)
        def _():
            state[...] = jnp.zeros_like(state)

        q = _ssd_common(xbc_ref, dtr_ref, dtrT_ref, prow_ref, pcol_ref)
        st = state[...]
        sin_ref[0] = st
        xd = q["x"] * q["dt_x"]
        g = _dot(q["cm"], q["bm"], NT)
        for r in range(SSM_HPG):
            sl = slice(r * p, (r + 1) * p)
            diff = q["cs"][:, r:r + 1] - q["cs_t"][r:r + 1, :]
            lm = jnp.where(q["li"] >= q["si"], jnp.exp(jnp.minimum(diff, 0.0)), 0.0)
            y_ref[:, sl] = _dot(g * lm, xd[:, sl], NN)
        y_ref[...] += q["e_x"] * _dot(q["cm"], st, NN) + q["d_x"] * q["x"]
        state[...] = q["el_x"] * st + _dot(q["bm"].T, xd * q["dec_x"], NN)

    return pl.pallas_call(
        body, name=name, grid=(SSM_GROUPS, nc),
        in_specs=[pl.BlockSpec((l, XBC_GROUP_W), lambda g, c: (c, g)),
                  pl.BlockSpec((1, l, SSM_HPG), lambda g, c: (g, c, 0)),
                  pl.BlockSpec((1, SSM_HPG, l), lambda g, c: (g, 0, c)),
                  pl.BlockSpec((1, 8, 8), lambda g, c: (g, 0, 0)),
                  pl.BlockSpec((1, 8, 8), lambda g, c: (g, 0, 0))],
        out_specs=[pl.BlockSpec((l, SSM_GROUP_W), lambda g, c: (c, g)),
                   pl.BlockSpec((1, SSM_D_STATE, SSM_GROUP_W), lambda g, c: (c, 0, g))],
        out_shape=[jax.ShapeDtypeStruct((t, SSM_D_INNER), f32),
                   jax.ShapeDtypeStruct((nc, SSM_D_STATE, SSM_D_INNER), f32)],
        scratch_shapes=[pltpu.VMEM((SSM_D_STATE, SSM_GROUP_W), f32)],
        compiler_params=pltpu.CompilerParams(dimension_semantics=("parallel", "arbitrary")),
    )(xbc, dtr, dtr_t, par_row, par_col)


def ssd_bwd(xbc, dtr, dtr_t, par_row, par_col, s_in, dy, *, name):
    t = xbc.shape[0]
    nc = t // SSM_CHUNK
    l, p = SSM_CHUNK, SSM_HEAD_DIM

    def body(xbc_ref, dtr_ref, dtrT_ref, prow_ref, pcol_ref, sin_ref, dy_ref, dxbc_ref, ddtr_ref, dpar_ref,
             dstate, yd_buf, dxd_buf):
        @pl.when(pl.program_id(1) == 0)
        def _():
            dstate[...] = jnp.zeros_like(dstate)
            dpar_ref[...] = jnp.zeros_like(dpar_ref)

        q = _ssd_common(xbc_ref, dtr_ref, dtrT_ref, prow_ref, pcol_ref)
        x, bm, cm, ex, li, si = q["x"], q["bm"], q["cm"], q["ex"], q["li"], q["si"]
        e_x, el_x, dec_x = q["e_x"], q["el_x"], q["dec_x"]
        st = sin_ref[0]
        dst = dstate[...]
        dy = dy_ref[...]
        xd = x * q["dt_x"]
        g = _dot(cm, bm, NT)
        dg = jnp.zeros((l, l), f32)
        for r in range(SSM_HPG):
            sl = slice(r * p, (r + 1) * p)
            diff = q["cs"][:, r:r + 1] - q["cs_t"][r:r + 1, :]
            lm = jnp.where(li >= si, jnp.exp(jnp.minimum(diff, 0.0)), 0.0)
            m = (g * lm).astype(bf16)
            xdh, dyh = xd[:, sl].astype(bf16), dy[:, sl].astype(bf16)
            yd_buf[:, sl] = _dot(m, xdh, NN)
            dxd_buf[:, sl] = _dot(m, dyh, TN)
            dg = dg + _dot(dyh, xdh, NT) * lm
        yd, dxd_diag = yd_buf[...], dxd_buf[...]
        yo = e_x * _dot(cm, st, NN)
        dz = e_x * dy
        wv = _dot(bm, dst, NN)
        xw = xd * wv * dec_x
        row8 = lax.broadcasted_iota(jnp.int32, (l, SSM_HPG), 0)
        dy_b, xd_b = dy.astype(bf16).astype(f32), xd.astype(bf16).astype(f32)
        dcs = _dot_01(dy_b * yd - xd_b * dxd_diag + dy * yo - xw, ex, NT, 0, 3)
        tail = jnp.sum(xw, axis=0, keepdims=True) + el_x * jnp.sum(dst * st, axis=0, keepdims=True)
        dcl = _dot_01(jnp.broadcast_to(tail, (SSM_HPG, SSM_GROUP_W)), ex, NT, 0, 3)[0:1, :]
        dcs = dcs + jnp.where(row8 == l - 1, dcl, 0.0)
        dda = _dot_01(q["tri"], dcs, TN, 1, 3)
        dxd = dxd_diag + dec_x * wv
        ddt = _dot_01(dxd * x, ex, NT, 0, 3) + dda * q["a_row"]
        ddtr = ddt * _sigmoid(q["dtr"] + q["bias_row"])
        ddtr_ref[0] = ddtr
        dd = _dot_01(jnp.broadcast_to(jnp.sum(dy * x, axis=0, keepdims=True), (SSM_HPG, SSM_GROUP_W)), ex, NT, 0, 2)[0:1, :]
        dpar_ref[0, 0:1, :] += jnp.sum(ddtr, axis=0, keepdims=True)
        dpar_ref[0, 1:2, :] += jnp.sum(dda * q["dt"], axis=0, keepdims=True) * q["a_row"]
        dpar_ref[0, 2:3, :] += dd
        dxbc_ref[:, 0:SSM_GROUP_W] = dxd * q["dt_x"] + q["d_x"] * dy
        dxbc_ref[:, SSM_GROUP_W:SSM_GROUP_W + SSM_D_STATE] = _dot(dg, cm, TN) + _dot(xd * dec_x, dst, NT)
        dxbc_ref[:, SSM_GROUP_W + SSM_D_STATE:XBC_GROUP_W] = _dot(dg, bm, NN) + _dot(dz, st, NT)
        dstate[...] = _dot(cm.T, dz, NN) + el_x * dst

    rc = lambda c: nc - 1 - c
    return pl.pallas_call(
        body, name=name, grid=(SSM_GROUPS, nc),
        in_specs=[pl.BlockSpec((l, XBC_GROUP_W), lambda g, c: (rc(c), g)),
                  pl.BlockSpec((1, l, SSM_HPG), lambda g, c: (g, rc(c), 0)),
                  pl.BlockSpec((1, SSM_HPG, l), lambda g, c: (g, 0, rc(c))),
                  pl.BlockSpec((1, 8, 8), lambda g, c: (g, 0, 0)),
                  pl.BlockSpec((1, 8, 8), lambda g, c: (g, 0, 0)),
                  pl.BlockSpec((1, SSM_D_STATE, SSM_GROUP_W), lambda g, c: (rc(c), 0, g)),
                  pl.BlockSpec((l, SSM_GROUP_W), lambda g, c: (rc(c), g))],
        out_specs=[pl.BlockSpec((l, XBC_GROUP_W), lambda g, c: (rc(c), g)),
                   pl.BlockSpec((1, l, SSM_HPG), lambda g, c: (g, rc(c), 0)),
                   pl.BlockSpec((1, 8, 8), lambda g, c: (g, 0, 0))],
        out_shape=[jax.ShapeDtypeStruct((t, SSM_CONV_DIM), f32),
                   jax.ShapeDtypeStruct((SSM_GROUPS, t, SSM_HPG), f32),
                   jax.ShapeDtypeStruct((SSM_GROUPS, 8, 8), f32)],
        scratch_shapes=[pltpu.VMEM((SSM_D_STATE, SSM_GROUP_W), f32), pltpu.VMEM((l, SSM_GROUP_W), f32),
                        pltpu.VMEM((l, SSM_GROUP_W), f32)],
        compiler_params=pltpu.CompilerParams(dimension_semantics=("parallel", "arbitrary")),
    )(xbc, dtr, dtr_t, par_row, par_col, s_in, dy)


def gnorm_fwd(y, proj, w, *, name):
    t = y.shape[0]
    tr = _pick(t, 512, 8)
    gw = SSM_GROUP_W
    zb = OFF_Z // gw

    def body(y_ref, z_ref, w_ref, o_ref):
        y2 = y_ref[...] * _silu(z_ref[...])
        r = lax.rsqrt(jnp.mean(y2 * y2, axis=-1, keepdims=True) + RMS_EPS)
        o_ref[...] = (y2 * r * w_ref[...]).astype(bf16)

    return pl.pallas_call(
        body, name=name, grid=(SSM_GROUPS, t // tr),
        in_specs=[pl.BlockSpec((tr, gw), lambda g, i: (i, g)), pl.BlockSpec((tr, gw), lambda g, i: (i, zb + g)),
                  pl.BlockSpec((1, gw), lambda g, i: (0, g))],
        out_specs=pl.BlockSpec((tr, gw), lambda g, i: (i, g)), out_shape=jax.ShapeDtypeStruct((t, SSM_D_INNER), bf16),
    )(y, proj, w)


def gnorm_bwd(y, proj, w, dout, dst, *, name):
    t = y.shape[0]
    tr = _pick(t, 512, 8)
    gw = SSM_GROUP_W
    zb = OFF_Z // gw

    def body(_, y_ref, z_ref, w_ref, do_ref, dy_ref, dz_ref, dw_ref):
        yv, zv = y_ref[...], z_ref[...]
        sz = _silu(zv)
        y2 = yv * sz
        dy2, dw = _rms_bwd_math(y2, w_ref[...], do_ref[...])
        dy_ref[...] = dy2 * sz
        dz_ref[...] = (dy2 * yv * _dsilu(zv)).astype(bf16)

        @pl.when(pl.program_id(1) == 0)
        def _():
            dw_ref[...] = jnp.zeros_like(dw_ref)

        dw_ref[...] += dw

    tile = pl.BlockSpec((tr, gw), lambda g, i: (i, g))
    vec = pl.BlockSpec((1, gw), lambda g, i: (0, g))
    return pl.pallas_call(
        body, name=name, grid=(SSM_GROUPS, t // tr),
        in_specs=[ANY, tile, pl.BlockSpec((tr, gw), lambda g, i: (i, zb + g)), vec, tile],
        out_specs=[tile, pl.BlockSpec((tr, gw), lambda g, i: (i, zb + g)), vec],
        out_shape=[jax.ShapeDtypeStruct((t, SSM_D_INNER), f32), jax.ShapeDtypeStruct(dst.shape, bf16),
                   jax.ShapeDtypeStruct((1, SSM_D_INNER), f32)],
        input_output_aliases={0: 1},
        compiler_params=pltpu.CompilerParams(dimension_semantics=("parallel", "arbitrary")),
    )(dst, y, proj, w, dout)


LRU_ROWS = 256


def _lru_gates(uv, wr_ref, wi_ref, br_ref, bi_ref, lam_ref):
    rg = _sigmoid(_dot(uv, wr_ref[0], NN) + br_ref[...])
    ig = _sigmoid(_dot(uv, wi_ref[0], NN) + bi_ref[...])
    sp = _softplus(-lam_ref[...])
    la = -LRU_C * rg * sp
    a = jnp.exp(la)
    s = jnp.sqrt(jnp.maximum(-_expm1(2.0 * la), 0.0))
    return rg, ig, sp, la, a, s


def lru_fwd(u, proj, w_r, b_r, w_i, b_i, lam, *, name):
    t = u.shape[0]
    r = LRU_ROWS
    lb = LRU_BLOCK
    yb = OFF_LY // lb

    def body(u_ref, y_ref, wr_ref, br_ref, wi_ref, bi_ref, lam_ref, h_ref, o_ref, carry):
        @pl.when(pl.program_id(1) == 0)
        def _():
            carry[...] = jnp.zeros_like(carry)

        uv = u_ref[...]
        _, ig, _, _, a, s = _lru_gates(uv, wr_ref, wi_ref, br_ref, bi_ref, lam_ref)
        b = s * ig * uv
        row = lax.broadcasted_iota(jnp.int32, (r, lb), 0)
        d = 1
        while d < r:
            keep = row >= d
            b = b + a * jnp.where(keep, pltpu.roll(b, d, 0), 0.0)
            a = a * jnp.where(keep, pltpu.roll(a, d, 0), 1.0)
            d *= 2
        h = b + a * carry[0:1, :]
        carry[0:1, :] = h[r - 1:r, :]
        h_ref[...] = h
        o_ref[...] = (h * _gelu(y_ref[...])).astype(bf16)

    tile = pl.BlockSpec((r, lb), lambda hb, j: (j, hb))
    vec = pl.BlockSpec((1, lb), lambda hb, j: (0, hb))
    wsp = pl.BlockSpec((1, lb, lb), lambda hb, j: (hb, 0, 0))
    return pl.pallas_call(
        body, name=name, grid=(LRU_BLOCKS, t // r),
        in_specs=[tile, pl.BlockSpec((r, lb), lambda hb, j: (j, yb + hb)), wsp, vec, wsp, vec, vec],
        out_specs=[tile, tile],
        out_shape=[jax.ShapeDtypeStruct((t, LRU_WIDTH), f32), jax.ShapeDtypeStruct((t, LRU_WIDTH), bf16)],
        scratch_shapes=[pltpu.VMEM((8, lb), f32)],
        compiler_params=pltpu.CompilerParams(dimension_semantics=("parallel", "arbitrary")),
    )(u, proj, w_r, b_r, w_i, b_i, lam)


def lru_bwd(u, proj, hseq, dout, w_r, b_r, w_i, b_i, lam, dst, *, name):
    t = u.shape[0]
    r = LRU_ROWS
    nt = t // r
    lb = LRU_BLOCK
    yb = OFF_LY // lb

    def body(_, u_ref, y_ref, h_ref, hp_ref, do_ref, wr_ref, br_ref, wi_ref, bi_ref, lam_ref,
             du_ref, dy_ref, dwr_ref, dwi_ref, dbr_ref, dbi_ref, dlam_ref, carry_dh, carry_a):
        j = pl.program_id(1)

        @pl.when(j == 0)
        def _():
            carry_dh[...] = jnp.zeros_like(carry_dh)
            carry_a[...] = jnp.zeros_like(carry_a)
            dwr_ref[...] = jnp.zeros_like(dwr_ref)
            dwi_ref[...] = jnp.zeros_like(dwi_ref)
            dbr_ref[...] = jnp.zeros_like(dbr_ref)
            dbi_ref[...] = jnp.zeros_like(dbi_ref)
            dlam_ref[...] = jnp.zeros_like(dlam_ref)

        uv = u_ref[...]
        yv = y_ref[...]
        hv = h_ref[...]
        dov = do_ref[...]
        rg, ig, sp, la, a, s = _lru_gates(uv, wr_ref, wi_ref, br_ref, bi_ref, lam_ref)
        dy_ref[...] = (dov * hv * _dgelu(yv)).astype(bf16)
        gq = dov * _gelu(yv)
        row = lax.broadcasted_iota(jnp.int32, (r, lb), 0)
        an = jnp.where(row < r - 1, pltpu.roll(a, r - 1, 0), carry_a[0:1, :])
        d = 1
        while d < r:
            keep = row < r - d
            gq = gq + an * jnp.where(keep, pltpu.roll(gq, r - d, 0), 0.0)
            an = an * jnp.where(keep, pltpu.roll(an, r - d, 0), 1.0)
            d *= 2
        dh = gq + an * carry_dh[0:1, :]
        carry_dh[0:1, :] = dh[0:1, :]
        carry_a[0:1, :] = a[0:1, :]
        first = jnp.where(j == nt - 1, 0.0, 1.0) * hp_ref[7:8, :]
        hprev = jnp.where(row >= 1, pltpu.roll(hv, 1, 0), first)
        da = dh * hprev
        iu = ig * uv
        e2 = jnp.exp(2.0 * la)
        dla = da * a - dh * iu * e2 / jnp.maximum(s, 1e-30)
        drp = dla * (-LRU_C * sp) * rg * (1.0 - rg)
        dip = dh * s * uv * ig * (1.0 - ig)
        dlam_ref[...] += jnp.sum(dla * (LRU_C * rg) * _sigmoid(-lam_ref[...]), axis=0, keepdims=True)
        du_ref[...] = dh * s * ig + _dot(drp, wr_ref[0], NT) + _dot(dip, wi_ref[0], NT)
        dwr_ref[0] += _dot(uv, drp, TN)
        dwi_ref[0] += _dot(uv, dip, TN)
        dbr_ref[...] += jnp.sum(drp, axis=0, keepdims=True)
        dbi_ref[...] += jnp.sum(dip, axis=0, keepdims=True)

    rj = lambda j: nt - 1 - j
    tile = pl.BlockSpec((r, lb), lambda hb, j: (rj(j), hb))
    vec = pl.BlockSpec((1, lb), lambda hb, j: (0, hb))
    wsp = pl.BlockSpec((1, lb, lb), lambda hb, j: (hb, 0, 0))
    hprev_spec = pl.BlockSpec((8, lb), lambda hb, j: (jnp.maximum(rj(j) * (r // 8) - 1, 0), hb))
    ywin = pl.BlockSpec((r, lb), lambda hb, j: (rj(j), yb + hb))
    return pl.pallas_call(
        body, name=name, grid=(LRU_BLOCKS, nt),
        in_specs=[ANY, tile, ywin, tile, hprev_spec, tile, wsp, vec, wsp, vec, vec],
        out_specs=[tile, ywin, wsp, wsp, vec, vec, vec],
        out_shape=[jax.ShapeDtypeStruct((t, LRU_WIDTH), f32), jax.ShapeDtypeStruct(dst.shape, bf16),
                   jax.ShapeDtypeStruct((LRU_BLOCKS, lb, lb), f32), jax.ShapeDtypeStruct((LRU_BLOCKS, lb, lb), f32),
                   jax.ShapeDtypeStruct((1, LRU_WIDTH), f32), jax.ShapeDtypeStruct((1, LRU_WIDTH), f32),
                   jax.ShapeDtypeStruct((1, LRU_WIDTH), f32)],
        input_output_aliases={0: 1},
        scratch_shapes=[pltpu.VMEM((8, lb), f32), pltpu.VMEM((8, lb), f32)],
        compiler_params=pltpu.CompilerParams(dimension_semantics=("parallel", "arbitrary")),
    )(dst, u, proj, hseq, hseq, dout, w_r, b_r, w_i, b_i, lam)


def merge_fwd(proj, bg, y_ssm, y_lru, *, name):
    t, d = y_ssm.shape
    tr = _pick(t, 256, 8)
    gb = OFF_GATES // d

    def body(gs_ref, gl_ref, bs_ref, bl_ref, ys_ref, yl_ref, o_ref):
        gs = _sigmoid(gs_ref[...] + bs_ref[...])
        gl = _sigmoid(gl_ref[...] + bl_ref[...])
        o_ref[...] = (gs * ys_ref[...] + gl * yl_ref[...]).astype(bf16)

    row = pl.BlockSpec((tr, d), lambda i: (i, 0))
    return pl.pallas_call(
        body, name=name, grid=(t // tr,),
        in_specs=[pl.BlockSpec((tr, d), lambda i: (i, gb)), pl.BlockSpec((tr, d), lambda i: (i, gb + 1)),
                  pl.BlockSpec((1, d), lambda i: (0, 0)), pl.BlockSpec((1, d), lambda i: (0, 1)), row, row],
        out_specs=row, out_shape=jax.ShapeDtypeStruct((t, d), bf16),
    )(proj, proj, bg, bg, y_ssm, y_lru)


def merge_bwd(proj, bg, y_ssm, y_lru, dmix, *, name):
    t, d = y_ssm.shape
    tr = _pick(t, 256, 8)
    gb = OFF_GATES // d

    def body(gs_ref, gl_ref, bs_ref, bl_ref, ys_ref, yl_ref, dm_ref, dg_ref, dys_ref, dyl_ref, dbg_ref):
        gs = _sigmoid(gs_ref[...] + bs_ref[...])
        gl = _sigmoid(gl_ref[...] + bl_ref[...])
        dm = dm_ref[...]
        dys_ref[...] = (dm * gs).astype(bf16)
        dyl_ref[...] = (dm * gl).astype(bf16)
        dgs = dm * ys_ref[...] * gs * (1.0 - gs)
        dgl = dm * yl_ref[...] * gl * (1.0 - gl)
        dg_ref[:, 0:d] = dgs.astype(bf16)
        dg_ref[:, d:2 * d] = dgl.astype(bf16)

        @pl.when(pl.program_id(0) == 0)
        def _():
            dbg_ref[...] = jnp.zeros_like(dbg_ref)

        dbg_ref[:, 0:d] += jnp.sum(dgs, axis=0, keepdims=True)
        dbg_ref[:, d:2 * d] += jnp.sum(dgl, axis=0, keepdims=True)

    row = pl.BlockSpec((tr, d), lambda i: (i, 0))
    return pl.pallas_call(
        body, name=name, grid=(t // tr,),
        in_specs=[pl.BlockSpec((tr, d), lambda i: (i, gb)), pl.BlockSpec((tr, d), lambda i: (i, gb + 1)),
                  pl.BlockSpec((1, d), lambda i: (0, 0)), pl.BlockSpec((1, d), lambda i: (0, 1)), row, row, row],
        out_specs=[pl.BlockSpec((tr, 2 * d), lambda i: (i, OFF_GATES // (2 * d))), row, row,
                   pl.BlockSpec((1, 2 * d), lambda i: (0, 0))],
        out_shape=[jax.ShapeDtypeStruct((t, PROJ_W), bf16), jax.ShapeDtypeStruct((t, d), bf16),
                   jax.ShapeDtypeStruct((t, d), bf16), jax.ShapeDtypeStruct((1, 2 * d), f32)],
        compiler_params=pltpu.CompilerParams(dimension_semantics=("arbitrary",)),
    )(proj, proj, bg, bg, y_ssm, y_lru, dmix)


def swiglu_fwd(ff, *, name):
    t = ff.shape[0]
    hd = FFN_HIDDEN
    tr = _pick(t, 128, 8)

    def body(f_ref, o_ref):
        o_ref[...] = (_silu(f_ref[:, 0:hd]) * f_ref[:, hd:2 * hd]).astype(bf16)

    return pl.pallas_call(
        body, name=name, grid=(t // tr,), in_specs=[pl.BlockSpec((tr, 2 * hd), lambda i: (i, 0))],
        out_specs=pl.BlockSpec((tr, hd), lambda i: (i, 0)), out_shape=jax.ShapeDtypeStruct((t, hd), bf16),
    )(ff)


def swiglu_bwd(ff, dact, *, name):
    t = ff.shape[0]
    hd = FFN_HIDDEN
    tr = _pick(t, 128, 8)

    def body(f_ref, d_ref, o_ref):
        gate, up, dv = f_ref[:, 0:hd], f_ref[:, hd:2 * hd], d_ref[...]
        o_ref[:, 0:hd] = (dv * up * _dsilu(gate)).astype(bf16)
        o_ref[:, hd:2 * hd] = (dv * _silu(gate)).astype(bf16)

    return pl.pallas_call(
        body, name=name, grid=(t // tr,),
        in_specs=[pl.BlockSpec((tr, 2 * hd), lambda i: (i, 0)), pl.BlockSpec((tr, hd), lambda i: (i, 0))],
        out_specs=pl.BlockSpec((tr, 2 * hd), lambda i: (i, 0)), out_shape=jax.ShapeDtypeStruct((t, 2 * hd), bf16),
    )(ff, dact)


def _adam_math(w, g, m, v):
    m = ADAM_B1 * m + (1.0 - ADAM_B1) * g
    v = ADAM_B2 * v + (1.0 - ADAM_B2) * (g * g)
    m_hat = m / (1.0 - ADAM_B1 ** ADAM_STEP)
    v_hat = v / (1.0 - ADAM_B2 ** ADAM_STEP)
    delta = -ADAM_LR * (m_hat / (jnp.sqrt(v_hat) + ADAM_EPS) + ADAM_WD * w)
    return delta, m, v


def _row_tile(rows, cols):
    cap = max(8, (1 << 18) // cols)
    return _pick(rows, cap, 8) if rows % 8 == 0 else rows


def adamw(w, g, m, v, *, name):
    rows, cols = w.shape
    tr = _row_tile(rows, cols)

    def body(w_ref, g_ref, m_ref, v_ref, d_ref, nm_ref, nv_ref):
        d, nm, nv = _adam_math(w_ref[...], g_ref[...], m_ref[...], v_ref[...])
        d_ref[...] = d
        nm_ref[...] = nm
        nv_ref[...] = nv

    tile = pl.BlockSpec((tr, cols), lambda i: (i, 0))
    return pl.pallas_call(
        body, name=name, grid=(rows // tr,), in_specs=[tile] * 4, out_specs=[tile] * 3,
        out_shape=[jax.ShapeDtypeStruct((rows, cols), f32)] * 3,
    )(w, g, m, v)


def adamw_many(ws, gs, ms, vs, *, name):
    n = len(ws)

    def body(*refs):
        for i in range(n):
            d, nm, nv = _adam_math(refs[i][...], refs[n + i][...], refs[2 * n + i][...], refs[3 * n + i][...])
            refs[4 * n + 3 * i][...] = d
            refs[4 * n + 3 * i + 1][...] = nm
            refs[4 * n + 3 * i + 2][...] = nv

    outs = pl.pallas_call(
        body, name=name, out_shape=[jax.ShapeDtypeStruct(w.shape, f32) for w in ws for _ in range(3)],
    )(*ws, *gs, *ms, *vs)
    return [tuple(outs[3 * i:3 * i + 3]) for i in range(n)]


def pair_add(dw, rbuf, idx, *, name):
    n, rows, cols = dw.shape
    hr = rows // 2
    tr = _row_tile(hr, cols)
    nrt = hr // tr

    def body(idx_ref, a_ref, b_ref, o_ref, own_ref):
        s = a_ref[...] + b_ref[...]
        o_ref[...] = s.astype(bf16)

        @pl.when(pl.program_id(1) == idx_ref[0])
        def _():
            own_ref[...] = s[0]

    return pl.pallas_call(
        body, name=name,
        grid_spec=pltpu.PrefetchScalarGridSpec(
            num_scalar_prefetch=1, grid=(nrt, n),
            in_specs=[pl.BlockSpec((1, tr, cols), lambda i, k, idx: (k, idx[1] * nrt + i, 0)),
                      pl.BlockSpec((1, tr, cols), lambda i, k, idx: (k, i, 0))],
            out_specs=[pl.BlockSpec((1, tr, cols), lambda i, k, idx: (k, i, 0)),
                       pl.BlockSpec((tr, cols), lambda i, k, idx: (i, 0))]),
        out_shape=[jax.ShapeDtypeStruct((n, hr, cols), bf16), jax.ShapeDtypeStruct((hr, cols), f32)],
    )(idx, dw, rbuf)


def chip_sum(own, rbuf, idx, *, name):
    hr, cols = own.shape
    tr = _row_tile(hr, cols)
    nrt = hr // tr

    def body(idx_ref, a_ref, b_ref, o_ref):
        o_ref[...] = ((a_ref[...] + b_ref[0].astype(f32)) + b_ref[1].astype(f32)) + b_ref[2].astype(f32)

    return pl.pallas_call(
        body, name=name,
        grid_spec=pltpu.PrefetchScalarGridSpec(
            num_scalar_prefetch=1, grid=(nrt,),
            in_specs=[pl.BlockSpec((tr, cols), lambda i, idx: (i, 0)),
                      pl.BlockSpec((3, tr, cols), lambda i, idx: (0, i, 0))],
            out_specs=pl.BlockSpec((tr, cols), lambda i, idx: (idx[1] * nrt + i, 0))),
        out_shape=jax.ShapeDtypeStruct((2 * hr, cols), f32),
    )(idx, own, rbuf)


def sum8(rbuf, *, name):
    n, rows, cols = rbuf.shape
    tr = _row_tile(rows, cols * n)

    def body(a_ref, o_ref):
        acc = a_ref[0]
        for k in range(1, n):
            acc = acc + a_ref[k]
        o_ref[...] = acc

    return pl.pallas_call(
        body, name=name, grid=(rows // tr,), in_specs=[pl.BlockSpec((n, tr, cols), lambda i: (0, i, 0))],
        out_specs=pl.BlockSpec((tr, cols), lambda i: (i, 0)), out_shape=jax.ShapeDtypeStruct((rows, cols), f32),
    )(rbuf)


def _coords():
    return lax.axis_index("x"), lax.axis_index("y"), lax.axis_index("c")


def _other_chips(x, y):
    return [(1 - x, y), (x, 1 - y), (1 - x, 1 - y)]


def gather_weights(shards, *, name):
    n = len(shards)
    halves = [s.shape[0] // 2 for s in shards]

    def body(*refs):
        ins, outs = refs[:n], refs[n:2 * n]
        send1, recv1, send2, recv2 = refs[2 * n:]
        x, y, c = _coords()
        me = 2 * x + y
        chips = _other_chips(x, y)
        sibling = (x, y, 1 - c)

        def half(i, k, hc):
            return outs[i].at[k, pl.ds(hc * halves[i], halves[i]), :]

        def ici(i, j):
            return pltpu.make_async_remote_copy(
                src_ref=ins[i].at[pl.ds(c * halves[i], halves[i]), :], dst_ref=half(i, me, c),
                send_sem=send1.at[i, j], recv_sem=recv1.at[i, j], device_id=(*chips[j], c), device_id_type=MESH)

        def landed(i, j):
            kj = 2 * chips[j][0] + chips[j][1]
            return pltpu.make_async_remote_copy(
                src_ref=half(i, kj, c), dst_ref=half(i, kj, c),
                send_sem=send2.at[i, j], recv_sem=recv1.at[i, j], device_id=sibling, device_id_type=MESH)

        def from_sibling(i, j):
            kj = 2 * chips[j][0] + chips[j][1]
            return pltpu.make_async_remote_copy(
                src_ref=half(i, kj, 1 - c), dst_ref=half(i, kj, 1 - c),
                send_sem=send2.at[i, j], recv_sem=recv2.at[i, j], device_id=sibling, device_id_type=MESH)

        def d2d(i, j):
            kj = 2 * chips[j][0] + chips[j][1]
            return pltpu.make_async_remote_copy(
                src_ref=half(i, kj, c), dst_ref=half(i, kj, c),
                send_sem=send2.at[i, j], recv_sem=recv2.at[i, j], device_id=sibling, device_id_type=MESH)

        for j in range(3):
            for i in range(n):
                ici(i, j).start()
        for j in range(3):
            for i in range(n):
                landed(i, j).wait_recv()
                d2d(i, j).start()
        for j in range(3):
            for i in range(n):
                from_sibling(i, j).wait_recv()
        for j in range(3):
            for i in range(n):
                ici(i, j).wait_send()
                d2d(i, j).wait_send()

    return pl.pallas_call(
        body, name=name, in_specs=[ANY] * n, out_specs=[ANY] * n,
        out_shape=[jax.ShapeDtypeStruct((N_CHIPS,) + s.shape, s.dtype) for s in shards],
        scratch_shapes=[pltpu.SemaphoreType.DMA((n, 3))] * 4,
    )(*shards)


def pair_exchange(grads, *, name):
    n = len(grads)
    halves = [g.shape[1] // 2 for g in grads]

    def body(*refs):
        ins, outs = refs[:n], refs[n:2 * n]
        send, recv = refs[2 * n:]
        x, y, c = _coords()
        cps = [pltpu.make_async_remote_copy(
            src_ref=ins[i].at[:, pl.ds((1 - c) * halves[i], halves[i]), :], dst_ref=outs[i],
            send_sem=send.at[i], recv_sem=recv.at[i], device_id=(x, y, 1 - c), device_id_type=MESH) for i in range(n)]
        for cp in cps:
            cp.start()
        for cp in cps:
            cp.wait()

    return pl.pallas_call(
        body, name=name, in_specs=[ANY] * n, out_specs=[ANY] * n,
        out_shape=[jax.ShapeDtypeStruct((N_CHIPS, g.shape[1] // 2, g.shape[2]), g.dtype) for g in grads],
        scratch_shapes=[pltpu.SemaphoreType.DMA((n,))] * 2,
    )(*grads)


def pair_gather(bufs, *, name):
    n = len(bufs)

    def body(*refs):
        ins, outs = refs[:n], refs[n:2 * n]
        send, recv = refs[2 * n:]
        x, y, c = _coords()
        cps = []
        for i in range(n):
            hr = ins[i].shape[0] // 2
            cps.append(pltpu.make_async_remote_copy(
                src_ref=ins[i].at[pl.ds(c * hr, hr), :], dst_ref=outs[i].at[pl.ds(c * hr, hr), :],
                send_sem=send.at[i], recv_sem=recv.at[i], device_id=(x, y, 1 - c), device_id_type=MESH))
        for cp in cps:
            cp.start()
        for i in range(n):
            hr = ins[i].shape[0] // 2
            pltpu.make_async_remote_copy(
                src_ref=ins[i].at[pl.ds((1 - c) * hr, hr), :], dst_ref=outs[i].at[pl.ds((1 - c) * hr, hr), :],
                send_sem=send.at[i], recv_sem=recv.at[i], device_id=(x, y, 1 - c), device_id_type=MESH).wait_recv()
        for cp in cps:
            cp.wait_send()

    return pl.pallas_call(
        body, name=name, in_specs=[ANY] * n, out_specs=[ANY] * n,
        out_shape=[jax.ShapeDtypeStruct(b.shape, b.dtype) for b in bufs],
        input_output_aliases={i: i for i in range(n)},
        scratch_shapes=[pltpu.SemaphoreType.DMA((n,))] * 2,
    )(*bufs)


def all_exchange(buf, *, name):
    rows, cols = buf.shape

    def body(in_ref, out_ref, send, recv):
        x, y, c = _coords()
        me = 4 * x + 2 * y + c
        cps = []
        for d in range(1, 8):
            px = 1 - x if d & 4 else x
            py = 1 - y if d & 2 else y
            pc = 1 - c if d & 1 else c
            cps.append(pltpu.make_async_remote_copy(
                src_ref=in_ref, dst_ref=out_ref.at[me], send_sem=send.at[d - 1], recv_sem=recv.at[d - 1],
                device_id=(px, py, pc), device_id_type=MESH))
        for cp in cps:
            cp.start()
        for d in range(1, 8):
            px = 1 - x if d & 4 else x
            py = 1 - y if d & 2 else y
            pc = 1 - c if d & 1 else c
            src = 4 * px + 2 * py + pc
            pltpu.make_async_remote_copy(
                src_ref=in_ref, dst_ref=out_ref.at[src], send_sem=send.at[d - 1], recv_sem=recv.at[d - 1],
                device_id=(px, py, pc), device_id_type=MESH).wait_recv()
        for cp in cps:
            cp.wait_send()

    return pl.pallas_call(
        body, name=name, in_specs=[ANY], out_specs=ANY,
        out_shape=jax.ShapeDtypeStruct((8, rows, cols), buf.dtype),
        scratch_shapes=[pltpu.SemaphoreType.DMA((7,)), pltpu.SemaphoreType.DMA((7,))],
    )(buf)


HBM = pl.BlockSpec(memory_space=pltpu.HBM)
SEM = pl.BlockSpec(memory_space=pltpu.SEMAPHORE)
EFFECT = pltpu.SideEffectType.DATAFLOW_SIDE_EFFECTING


def split_start(arrays, after, copies, sem_shape, *, name):
    na = len(arrays)

    def body(*refs):
        for cp in copies(refs[:na], refs[na + 1], refs[na + 2]):
            cp.start()
        refs[-1][...] = jnp.zeros((8, 128), f32)

    outs = pl.pallas_call(
        body, name=name,
        out_shape=(pltpu.SemaphoreType.DMA(sem_shape), pltpu.SemaphoreType.DMA(sem_shape),
                   *[pltpu.HBM(a.shape, a.dtype) for a in arrays], jax.ShapeDtypeStruct((8, 128), f32)),
        in_specs=[HBM] * na + [ANY], out_specs=(SEM, SEM, *[HBM] * na, pl.BlockSpec(memory_space=pltpu.VMEM)),
        input_output_aliases={i: 2 + i for i in range(na)},
        compiler_params=pltpu.CompilerParams(has_side_effects=EFFECT),
    )(*[pltpu.with_memory_space_constraint(a, pltpu.HBM) for a in arrays], after)
    return outs[0], outs[1], list(outs[2:2 + na]), outs[-1]


def split_wait(send, recv, arrays, after, copies, *, name):
    na = len(arrays)

    def body(*refs):
        for cp in copies(refs[:na], refs[na], refs[na + 1]):
            cp.wait_send()
            cp.wait_recv()

    outs = pl.pallas_call(
        body, name=name, out_shape=tuple(pltpu.HBM(a.shape, a.dtype) for a in arrays),
        in_specs=[HBM] * na + [SEM, SEM, ANY], out_specs=tuple([HBM] * na),
        input_output_aliases={i: i for i in range(na)},
        compiler_params=pltpu.CompilerParams(has_side_effects=EFFECT),
    )(*arrays, send, recv, after)
    return list(outs)


def gather_copies(n):
    def copies(refs, send, recv):
        x, y, c = _coords()
        me = 2 * x + y
        chips = _other_chips(x, y)
        return [pltpu.make_async_remote_copy(
            src_ref=refs[i], dst_ref=refs[n + i].at[me], send_sem=send.at[3 * i + j], recv_sem=recv.at[3 * i + j],
            device_id=(*chips[j], c), device_id_type=MESH) for j in range(3) for i in range(n)]
    return copies


def all_copies():
    def copies(refs, send, recv):
        x, y, c = _coords()
        me = 4 * x + 2 * y + c
        cps = []
        for d in range(1, 8):
            peer = (1 - x if d & 4 else x, 1 - y if d & 2 else y, 1 - c if d & 1 else c)
            cps.append(pltpu.make_async_remote_copy(
                src_ref=refs[0], dst_ref=refs[1].at[me], send_sem=send.at[d - 1], recv_sem=recv.at[d - 1],
                device_id=peer, device_id_type=MESH))
        return cps
    return copies


def reduce_copies(n):
    def copies(refs, send, recv):
        x, y, c = _coords()
        chips = _other_chips(x, y)
        return [pltpu.make_async_remote_copy(
            src_ref=refs[i].at[2 * chips[j][0] + chips[j][1]], dst_ref=refs[n + i].at[j],
            send_sem=send.at[3 * i + j], recv_sem=recv.at[3 * i + j], device_id=(*chips[j], c), device_id_type=MESH)
            for j in range(3) for i in range(n)]
    return copies


def _pack(arrs):
    flat = []
    for a in arrs:
        v = a.reshape(-1).astype(f32)
        pad = (-v.shape[0]) % 128
        flat.append(jnp.pad(v, (0, pad)) if pad else v)
    v = jnp.concatenate(flat)
    rows = v.shape[0] // 128
    pad_rows = (-rows) % 256
    v = v.reshape(rows, 128)
    return jnp.pad(v, ((0, pad_rows), (0, 0))) if pad_rows else v


def _unpack(buf, shapes):
    out, row = [], 0
    for s in shapes:
        size = math.prod(s)
        rows = -(-size // 128)
        out.append(buf[row:row + rows].reshape(-1)[:size].reshape(s))
        row += rows
    return out


def _perm_in_cols(w):
    gates, z = w[..., 0:2048], w[..., 2048:4096]
    xbc = w[..., 4096:7168]
    dt, lx, ly = w[..., 7168:7200], w[..., 7200:8480], w[..., 8480:9760]
    pad = jnp.zeros(w.shape[:-1] + (DT_PAD_W - SSM_HEADS,), w.dtype)
    return jnp.concatenate([gates, z, lx, ly, dt, pad, _perm_xbc_cols(xbc)], axis=-1)


def _perm_xbc_cols(w):
    parts = []
    for g in range(SSM_GROUPS):
        parts += [w[..., g * 512:(g + 1) * 512], w[..., 2048 + g * 128:2048 + (g + 1) * 128],
                  w[..., 2560 + g * 128:2560 + (g + 1) * 128]]
    return jnp.concatenate(parts, axis=-1)


def _unperm_xbc_cols(w):
    xs = [w[..., g * XBC_GROUP_W:g * XBC_GROUP_W + 512] for g in range(SSM_GROUPS)]
    bs = [w[..., g * XBC_GROUP_W + 512:g * XBC_GROUP_W + 640] for g in range(SSM_GROUPS)]
    cs = [w[..., g * XBC_GROUP_W + 640:(g + 1) * XBC_GROUP_W] for g in range(SSM_GROUPS)]
    return jnp.concatenate(xs + bs + cs, axis=-1)


def _unperm_in_cols(w):
    xbc = _unperm_xbc_cols(w[..., OFF_XBC:OFF_XBC + 3072])
    return jnp.concatenate([w[..., OFF_GATES:OFF_GATES + 2048], w[..., OFF_Z:OFF_Z + 2048], xbc,
                            w[..., OFF_DT:OFF_DT + 32], w[..., OFF_LX:OFF_LX + 1280], w[..., OFF_LY:OFF_LY + 1280]], axis=-1)


def _col_shards(w, n=N_CHIPS):
    r, c = w.shape
    return jnp.transpose(w.reshape(r, n, c // n), (1, 0, 2))


def _from_col_shards(w):
    n, r, c = w.shape
    return jnp.transpose(w, (1, 0, 2)).reshape(r, n * c)


def kernel(x, norm1_w, w_in, b_branch_gate, ssm_conv_w, ssm_conv_b, ssm_dt_bias, ssm_a_log, ssm_d, ssm_norm_w, w_out_ssm, lru_conv_w, lru_conv_b, lru_w_r, lru_b_r, lru_w_i, lru_b_i, lru_lambda, w_out_lru, w_out, norm2_w, w_ffn_in, w_ffn_out, norm_f_w, loss_target, m_norm1_w, m_w_in, m_b_branch_gate, m_ssm_conv_w, m_ssm_conv_b, m_ssm_dt_bias, m_ssm_a_log, m_ssm_d, m_ssm_norm_w, m_w_out_ssm, m_lru_conv_w, m_lru_conv_b, m_lru_w_r, m_lru_b_r, m_lru_w_i, m_lru_b_i, m_lru_lambda, m_w_out_lru, m_w_out, m_norm2_w, m_w_ffn_in, m_w_ffn_out, m_norm_f_w, v_norm1_w, v_w_in, v_b_branch_gate, v_ssm_conv_w, v_ssm_conv_b, v_ssm_dt_bias, v_ssm_a_log, v_ssm_d, v_ssm_norm_w, v_w_out_ssm, v_lru_conv_w, v_lru_conv_b, v_lru_w_r, v_lru_b_r, v_lru_w_i, v_lru_b_i, v_lru_lambda, v_w_out_lru, v_w_out, v_norm2_w, v_w_ffn_in, v_w_ffn_out, v_norm_f_w):
    xi, yi, ci = lax.axis_index("x"), lax.axis_index("y"), lax.axis_index("c")
    me = 2 * xi + yi
    idx = jnp.stack([me, ci]).astype(jnp.int32)
    x2 = x[0]
    tgt = loss_target[0]

    big_names = ["w_in", "w_out_ssm", "w_out_lru", "w_out", "w_ffn_in", "w_ffn_out"]
    big_w = dict(w_in=w_in[0], w_out_ssm=w_out_ssm[0], w_out_lru=w_out_lru[0], w_out=w_out[0], w_ffn_in=w_ffn_in[0],
                 w_ffn_out=w_ffn_out[0])
    big_m = dict(w_in=m_w_in[0], w_out_ssm=m_w_out_ssm[0], w_out_lru=m_w_out_lru[0], w_out=m_w_out[0],
                 w_ffn_in=m_w_ffn_in[0], w_ffn_out=m_w_ffn_out[0])
    big_v = dict(w_in=v_w_in[0], w_out_ssm=v_w_out_ssm[0], w_out_lru=v_w_out_lru[0], w_out=v_w_out[0],
                 w_ffn_in=v_w_ffn_in[0], w_ffn_out=v_w_ffn_out[0])
    conv_pad = jnp.zeros((16, 768), f32).at[0:4, :].set(ssm_conv_w[0]).at[8:12, 0:320].set(lru_conv_w[0])
    mine = [big_w["w_in"].astype(bf16), conv_pad]
    gathered = gather_weights(mine, name="gather_weights")
    g_in, g_conv = [lax.dynamic_update_index_in_dim(g, s, me, 0) for g, s in zip(gathered, mine)]
    w_in_p = _perm_in_cols(_from_col_shards(g_in))
    late_names = big_names[1:]
    late = [big_w[k].astype(bf16) for k in late_names]
    late_lands = [lax.empty((N_CHIPS,) + s.shape, bf16) for s in late]
    g_send, g_recv, g_arrays, g_token = split_start(late + late_lands, g_conv, gather_copies(5), (15,),
                                                    name="gather_late_start")
    ssm_cw_full = _from_col_shards(g_conv[:, 0:4, :])
    lru_cw_full = _from_col_shards(g_conv[:, 8:12, 0:320])
    ssm_cw_p = _perm_xbc_cols(ssm_cw_full)
    ssm_cb_p = _perm_xbc_cols(ssm_conv_b)

    par = jnp.stack([ssm_dt_bias[0], ssm_a_log[0], ssm_d[0]], axis=0).reshape(3, SSM_GROUPS, SSM_HPG)
    par_row = jnp.zeros((SSM_GROUPS, 8, 8), f32).at[:, 0:3, :].set(jnp.transpose(par, (1, 0, 2)))
    par_col = jnp.transpose(par_row, (0, 2, 1))

    hn1 = rms_fwd(x2, norm1_w + g_token[0:1, 0:1], name="rms1_fwd")
    proj = mm(hn1, w_in_p, "nn", name="in_proj")
    t = x2.shape[0]
    dtr = jnp.transpose(proj[:, OFF_DT:OFF_DT + 32].reshape(t, SSM_GROUPS, SSM_HPG), (1, 0, 2))
    dtr_t = jnp.transpose(dtr, (0, 2, 1))
    xbc_pre, xbc_post = conv_fwd(proj, OFF_XBC, SSM_CONV_DIM, ssm_cw_p, ssm_cb_p, silu=True, name="ssm_conv_fwd")
    y_ssd, s_in = ssd_fwd(xbc_post, dtr, dtr_t, par_row, par_col, name="ssd_fwd")
    yn = gnorm_fwd(y_ssd, proj, ssm_norm_w, name="gnorm_fwd")
    g_arrays = split_wait(g_send, g_recv, g_arrays, yn, gather_copies(5), name="gather_late_wait")
    g_out_ssm, g_out_lru, g_out, g_ffn_in, g_ffn_out = [
        lax.dynamic_update_index_in_dim(g, s, me, 0) for g, s in zip(g_arrays[5:], late)]
    w_out_ssm_f = g_out_ssm.reshape(SSM_D_INNER, D_MODEL)
    w_out_lru_f = g_out_lru.reshape(LRU_WIDTH, D_MODEL)
    w_out_f = g_out.reshape(D_MODEL, D_MODEL)
    w_ffn_in_f = _from_col_shards(g_ffn_in)
    w_ffn_out_f = g_ffn_out.reshape(FFN_HIDDEN, D_MODEL)
    y_ssm = mm(yn, w_out_ssm_f, "nn", name="out_ssm")
    (u_lru,) = conv_fwd(proj, OFF_LX, LRU_WIDTH, lru_cw_full, lru_conv_b, silu=False, name="lru_conv_fwd")
    h_lru, o_lru = lru_fwd(u_lru, proj, lru_w_r[0], lru_b_r, lru_w_i[0], lru_b_i, lru_lambda, name="lru_fwd")
    y_lru = mm(o_lru, w_out_lru_f, "nn", name="out_lru")
    mix = merge_fwd(proj, b_branch_gate, y_ssm, y_lru, name="merge_fwd")
    h1 = mm(mix, w_out_f, "nn", add=x2, name="out_proj")
    hn2 = rms_fwd(h1, norm2_w, name="rms2_fwd")
    ff = mm(hn2, w_ffn_in_f, "nn", name="ffn_in")
    act = swiglu_fwd(ff, name="swiglu_fwd")
    h2 = mm(act, w_ffn_out_f, "nn", add=h1, name="ffn_out")
    loss_tile, dh2, d_norm_f = loss_head(h2, norm_f_w.reshape(1, D_MODEL), tgt, name="loss_head")
    loss = lax.psum(loss_tile[0, 0], ("x", "y", "c"))

    d_w_ffn_out = mm(act, dh2, "tn", name="d_w_ffn_out")
    dact = mm(dh2, w_ffn_out_f, "nt", name="d_act")
    dff = swiglu_bwd(ff, dact, name="swiglu_bwd")
    d_w_ffn_in = mm(hn2, dff, "tn", name="d_w_ffn_in")
    dhn2 = mm(dff, w_ffn_in_f, "nt", name="d_hn2")
    dh1, d_norm2 = rms_bwd(h1, norm2_w, dhn2, dh2, name="rms2_bwd")
    d_w_out = mm(mix, dh1, "tn", name="d_w_out")
    dmix = mm(dh1, w_out_f, "nt", name="d_mix")
    dproj, dy_ssm, dy_lru, d_bg = merge_bwd(proj, b_branch_gate, y_ssm, y_lru, dmix, name="merge_bwd")
    d_w_out_ssm = mm(yn, dy_ssm, "tn", name="d_w_out_ssm")
    d_w_out_lru = mm(o_lru, dy_lru, "tn", name="d_w_out_lru")
    early_g = [d_w_out_ssm.reshape(N_CHIPS, 512, D_MODEL), d_w_out_lru.reshape(N_CHIPS, 320, D_MODEL),
               d_w_out.reshape(N_CHIPS, 256, D_MODEL), _col_shards(d_w_ffn_in), d_w_ffn_out.reshape(N_CHIPS, 704, D_MODEL)]
    e_sib = pair_exchange(early_g, name="pair_exchange_early")
    e_pairs = [pair_add(g, rb, idx, name="pair_add_" + k) for g, rb, k in zip(early_g, e_sib, late_names)]
    e_lands = [lax.empty((3,) + p[0].shape[1:], bf16) for p in e_pairs]
    e_send, e_recv, e_arrays, e_token = split_start([p[0] for p in e_pairs] + e_lands, e_pairs[0][1], reduce_copies(5),
                                                    (15,), name="reduce_early_start")
    dyn = mm(dy_ssm, w_out_ssm_f, "nt", name="d_yn")
    dy_ssd, dproj, d_ssm_norm = gnorm_bwd(y_ssd, proj, ssm_norm_w + e_token[0:1, 0:1], dyn, dproj, name="gnorm_bwd")
    dxbc_post, ddtr, dpar = ssd_bwd(xbc_post, dtr, dtr_t, par_row, par_col, s_in, dy_ssd, name="ssd_bwd")
    dproj, d_ssm_cw_p, d_ssm_cb_p = conv_bwd(dxbc_post, xbc_pre, proj, OFF_XBC, ssm_cw_p, dproj, name="ssm_conv_bwd")
    do_lru = mm(dy_lru, w_out_lru_f, "nt", name="d_o_lru")
    du_lru, dproj, d_w_r, d_w_i, d_b_r, d_b_i, d_lam = lru_bwd(u_lru, proj, h_lru, do_lru, lru_w_r[0], lru_b_r, lru_w_i[0],
                                                               lru_b_i, lru_lambda, dproj, name="lru_bwd")
    dproj, d_lru_cw, d_lru_cb = conv_bwd(du_lru, None, proj, OFF_LX, lru_cw_full, dproj, name="lru_conv_bwd")
    ddt_cols = jnp.transpose(ddtr, (1, 0, 2)).reshape(t, SSM_HEADS).astype(bf16)
    ddt_cols = jnp.pad(ddt_cols, ((0, 0), (0, DT_PAD_W - SSM_HEADS)))
    dproj = lax.dynamic_update_slice(dproj, ddt_cols, (0, OFF_DT))

    d_ssm_cw = _unperm_xbc_cols(d_ssm_cw_p)
    d_ssm_cb = _unperm_xbc_cols(d_ssm_cb_p)
    dpar_h = jnp.transpose(dpar[:, 0:3, :], (1, 0, 2)).reshape(3, SSM_HEADS)
    small_names = ["norm1_w", "b_branch_gate", "ssm_conv_b", "ssm_dt_bias", "ssm_a_log", "ssm_d", "ssm_norm_w",
                   "lru_conv_b", "lru_w_r", "lru_b_r", "lru_w_i", "lru_b_i", "lru_lambda", "norm2_w", "norm_f_w"]
    small_g = dict(norm1_w=jnp.zeros_like(norm1_w), b_branch_gate=d_bg, ssm_conv_b=d_ssm_cb, ssm_dt_bias=dpar_h[0:1], ssm_a_log=dpar_h[1:2],
                   ssm_d=dpar_h[2:3], ssm_norm_w=d_ssm_norm, lru_conv_b=d_lru_cb, lru_w_r=d_w_r[None], lru_b_r=d_b_r,
                   lru_w_i=d_w_i[None], lru_b_i=d_b_i, lru_lambda=d_lam, norm2_w=d_norm2, norm_f_w=d_norm_f.reshape(D_MODEL))
    small_w = dict(norm1_w=norm1_w, b_branch_gate=b_branch_gate, ssm_conv_b=ssm_conv_b, ssm_dt_bias=ssm_dt_bias,
                   ssm_a_log=ssm_a_log, ssm_d=ssm_d, ssm_norm_w=ssm_norm_w, lru_conv_b=lru_conv_b, lru_w_r=lru_w_r,
                   lru_b_r=lru_b_r, lru_w_i=lru_w_i, lru_b_i=lru_b_i, lru_lambda=lru_lambda, norm2_w=norm2_w, norm_f_w=norm_f_w)
    small_m = dict(norm1_w=m_norm1_w, b_branch_gate=m_b_branch_gate, ssm_conv_b=m_ssm_conv_b, ssm_dt_bias=m_ssm_dt_bias,
                   ssm_a_log=m_ssm_a_log, ssm_d=m_ssm_d, ssm_norm_w=m_ssm_norm_w, lru_conv_b=m_lru_conv_b, lru_w_r=m_lru_w_r,
                   lru_b_r=m_lru_b_r, lru_w_i=m_lru_w_i, lru_b_i=m_lru_b_i, lru_lambda=m_lru_lambda, norm2_w=m_norm2_w,
                   norm_f_w=m_norm_f_w)
    small_v = dict(norm1_w=v_norm1_w, b_branch_gate=v_b_branch_gate, ssm_conv_b=v_ssm_conv_b, ssm_dt_bias=v_ssm_dt_bias,
                   ssm_a_log=v_ssm_a_log, ssm_d=v_ssm_d, ssm_norm_w=v_ssm_norm_w, lru_conv_b=v_lru_conv_b, lru_w_r=v_lru_w_r,
                   lru_b_r=v_lru_b_r, lru_w_i=v_lru_w_i, lru_b_i=v_lru_b_i, lru_lambda=v_lru_lambda, norm2_w=v_norm2_w,
                   norm_f_w=v_norm_f_w)
    shapes = [small_w[k].shape for k in small_names]
    conv_shapes = [(4, SSM_CONV_DIM), (4, LRU_WIDTH)]
    g_pack = _pack([small_g[k] for k in small_names] + [d_ssm_cw, d_lru_cw])
    s_send, s_recv, s_arrays, s_token = split_start([g_pack, lax.empty((8,) + g_pack.shape, f32)], g_pack, all_copies(),
                                                    (7,), name="small_start")
    d_w_in_p = mm(hn1, dproj, "tn", after=s_token, name="d_w_in")

    d_w_in_s = _col_shards(_unperm_in_cols(d_w_in_p))
    (l_sib,) = pair_exchange([d_w_in_s], name="pair_exchange_late")
    l_pair = pair_add(d_w_in_s, l_sib, idx, name="pair_add_w_in")
    l_land = lax.empty((3,) + l_pair[0].shape[1:], bf16)
    l_send, l_recv, l_arrays, l_token = split_start([l_pair[0], l_land], l_pair[1], reduce_copies(1), (3,),
                                                    name="reduce_late_start")
    dhn1 = mm(dproj, w_in_p, "nt", after=l_token, name="d_hn1")
    grad_x, d_norm1 = rms_bwd(x2, norm1_w, dhn1, dh1, name="rms1_bwd")

    e_arrays = split_wait(e_send, e_recv, e_arrays, d_norm1, reduce_copies(5), name="reduce_early_wait")
    e_half = [chip_sum(p[1], rb, idx, name="chip_sum_" + k) for p, rb, k in zip(e_pairs, e_arrays[5:], late_names)]
    big_out = {}
    for k, g in zip(late_names, pair_gather(e_half, name="pair_gather_early")):
        big_out[k] = (g,) + tuple(adamw(big_w[k], g, big_m[k], big_v[k], name="adamw_" + k))

    s_arrays = split_wait(s_send, s_recv, s_arrays, d_norm1, all_copies(), name="small_wait")
    g_sum = sum8(lax.dynamic_update_index_in_dim(s_arrays[1], g_pack, 2 * me + ci, 0), name="sum8")
    n1 = d_norm1.reshape(8, 128)
    n1_sum = sum8(lax.dynamic_update_index_in_dim(all_exchange(n1, name="all_exchange_norm1"), n1, 2 * me + ci, 0),
                  name="sum8_norm1")
    g_sum = lax.dynamic_update_slice(g_sum, n1_sum, (0, 0))
    g_small = _unpack(g_sum, shapes + conv_shapes)
    g_small[-2] = lax.dynamic_slice_in_dim(g_small[-2], me * 768, 768, axis=1)
    g_small[-1] = lax.dynamic_slice_in_dim(g_small[-1], me * 320, 320, axis=1)
    all_names = small_names + ["ssm_conv_w", "lru_conv_w"]
    small_w.update(ssm_conv_w=ssm_conv_w[0], lru_conv_w=lru_conv_w[0])
    small_m.update(ssm_conv_w=m_ssm_conv_w[0], lru_conv_w=m_lru_conv_w[0])
    small_v.update(ssm_conv_w=v_ssm_conv_w[0], lru_conv_w=v_lru_conv_w[0])
    as2d = lambda a: a.reshape(-1, a.shape[-1])
    upd = adamw_many([as2d(small_w[k]) for k in all_names], [as2d(g) for g in g_small],
                     [as2d(small_m[k]) for k in all_names], [as2d(small_v[k]) for k in all_names], name="adamw_small")
    small_out = {}
    for k, g, u in zip(all_names, g_small, upd):
        small_out[k] = (g,) + tuple(o.reshape(g.shape) for o in u)
    l_arrays = split_wait(l_send, l_recv, l_arrays, upd[0][0], reduce_copies(1), name="reduce_late_wait")
    l_half = chip_sum(l_pair[1], l_arrays[1], idx, name="chip_sum_w_in")
    (g_w_in,) = pair_gather([l_half], name="pair_gather_late")
    big_out["w_in"] = (g_w_in,) + tuple(adamw(big_w["w_in"], g_w_in, big_m["w_in"], big_v["w_in"], name="adamw_w_in"))

    order = ["norm1_w", "w_in", "b_branch_gate", "ssm_conv_w", "ssm_conv_b", "ssm_dt_bias", "ssm_a_log", "ssm_d", "ssm_norm_w",
             "w_out_ssm", "lru_conv_w", "lru_conv_b", "lru_w_r", "lru_b_r", "lru_w_i", "lru_b_i", "lru_lambda", "w_out_lru",
             "w_out", "norm2_w", "w_ffn_in", "w_ffn_out", "norm_f_w"]
    outs = [loss, grad_x[None]]
    for which in range(4):
        for k in order:
            if k in big_out:
                outs.append(big_out[k][which][None])
            elif k in ("ssm_conv_w", "lru_conv_w"):
                outs.append(small_out[k][which][None])
            else:
                outs.append(small_out[k][which])
    return tuple(outs)
```

```python
import functools
import math

import jax
import jax.numpy as jnp
from jax import lax
from jax.experimental import pallas as pl
from jax.experimental.pallas import tpu as pltpu

f32 = jnp.float32
bf16 = jnp.bfloat16

D_MODEL = 1024
SSM_D_INNER = 2048
SSM_HEADS = 32
SSM_HEAD_DIM = 64
SSM_GROUPS = 4
SSM_HPG = 8
SSM_D_STATE = 128
SSM_CHUNK = 128
SSM_GROUP_W = 512
SSM_CONV_DIM = 3072
XBC_GROUP_W = 768
LRU_WIDTH = 1280
LRU_BLOCKS = 10
LRU_BLOCK = 128
LRU_C = 8.0
FFN_HIDDEN = 2816
RMS_EPS = 1e-6
IN_PROJ_DIM = 9760
N_CHIPS = 4

OFF_GATES = 0
OFF_Z = 2048
OFF_LX = 4096
OFF_LY = 5376
OFF_DT = 6656
DT_PAD_W = 256
OFF_XBC = 6912
PROJ_W = 9984

ADAM_LR = 0.001
ADAM_B1 = 0.9
ADAM_B2 = 0.999
ADAM_EPS = 1e-08
ADAM_WD = 0.01
ADAM_STEP = 10

MESH = pl.DeviceIdType.MESH
ANY = pl.BlockSpec(memory_space=pl.ANY)

NN = (((1,), (0,)), ((), ()))
NT = (((1,), (1,)), ((), ()))
TN = (((0,), (0,)), ((), ()))


def _pick(n, cap, mult=128):
    best = None
    for t in range(mult, min(n, cap) + 1, mult):
        if n % t == 0:
            best = t
    return best if best is not None else n


def _sigmoid(x):
    return 1.0 / (1.0 + jnp.exp(-x))


def _softplus(x):
    return jnp.maximum(x, 0.0) + jnp.log(1.0 + jnp.exp(-jnp.abs(x)))


def _silu(x):
    return x * _sigmoid(x)


def _dsilu(x):
    s = _sigmoid(x)
    return s * (1.0 + x * (1.0 - s))


_GELU_K = math.sqrt(2.0 / math.pi)


def _gelu(x):
    return 0.5 * x * (1.0 + jnp.tanh(_GELU_K * (x + 0.044715 * x * x * x)))


def _dgelu(x):
    t = jnp.tanh(_GELU_K * (x + 0.044715 * x * x * x))
    return 0.5 * (1.0 + t) + 0.5 * x * (1.0 - t * t) * _GELU_K * (1.0 + 3.0 * 0.044715 * x * x)


def _expm1(x):
    poly = x * (1.0 + x * (0.5 + x * (1.0 / 6.0 + x * (1.0 / 24.0 + x * (1.0 / 120.0 + x * (1.0 / 720.0))))))
    return jnp.where(jnp.abs(x) < 0.1, poly, jnp.exp(x) - 1.0)


def _dot(a, b, dn):
    return lax.dot_general(a.astype(bf16), b.astype(bf16), dn, preferred_element_type=f32)


def _dot_01(a, b, dn, split, terms):
    r = a if split == 0 else b
    out = None
    for _ in range(terms):
        h = r.astype(bf16)
        r = r - h.astype(f32)
        d = lax.dot_general(h if split == 0 else a.astype(bf16), b.astype(bf16) if split == 0 else h, dn,
                            preferred_element_type=f32)
        out = d if out is None else out + d
    return out


MM_WHOLE_K = 2816
MM_VMEM_BUDGET = 40 * 2 ** 20

def mm(a, b, mode, *, name, add=None, after=None, out_dtype=f32):
    if mode == "nn":
        (m, k), (k2, n) = a.shape, b.shape
    elif mode == "nt":
        (m, k), (n, k2) = a.shape, b.shape
    else:
        (k, m), (k2, n) = a.shape, b.shape
    assert k == k2, (a.shape, b.shape, mode)
    tk = k if k <= MM_WHOLE_K else _pick(k, 1024 if mode == "tn" else 2048)
    tn = _pick(n, 1536)
    isz = lambda v: jnp.dtype(v.dtype).itemsize
    for cap in (1536, 1024, 512, 256):
        tm = _pick(m, cap)
        vmem = 2 * (tm * tk * isz(a) + tk * tn * isz(b) + tm * tn * (4 * int(add is not None) + jnp.dtype(out_dtype).itemsize))
        vmem += 4 * tm * tn * int(k > tk)
        if vmem <= MM_VMEM_BUDGET:
            break
    nk = k // tk
    dn = {"nn": NN, "nt": NT, "tn": TN}[mode]
    a_spec = pl.BlockSpec((tk, tm), lambda i, j, kk: (kk, i)) if mode == "tn" else pl.BlockSpec((tm, tk), lambda i, j, kk: (i, kk))
    b_spec = pl.BlockSpec((tn, tk), lambda i, j, kk: (j, kk)) if mode == "nt" else pl.BlockSpec((tk, tn), lambda i, j, kk: (kk, j))
    o_spec = pl.BlockSpec((tm, tn), lambda i, j, kk: (i, j))
    has_add = add is not None

    n_extra = int(has_add) + int(after is not None)

    def body(a_ref, b_ref, *rest):
        add_ref = rest[0] if has_add else None
        o_ref = rest[n_extra]

        def finish(r):
            if has_add:
                r = r + add_ref[...]
            o_ref[...] = r.astype(out_dtype)

        if nk == 1:
            finish(_dot(a_ref[...], b_ref[...], dn))
            return
        acc = rest[-1]
        kk = pl.program_id(2)

        @pl.when(kk == 0)
        def _():
            acc[...] = jnp.zeros_like(acc)

        acc[...] += _dot(a_ref[...], b_ref[...], dn)

        @pl.when(kk == nk - 1)
        def _():
            finish(acc[...])

    ins = [a, b] + ([add] if has_add else []) + ([after] if after is not None else [])
    in_specs = [a_spec, b_spec] + ([o_spec] if has_add else []) + ([ANY] if after is not None else [])
    return pl.pallas_call(
        body, name=name, grid=(m // tm, n // tn, nk), in_specs=in_specs, out_specs=o_spec,
        out_shape=jax.ShapeDtypeStruct((m, n), out_dtype),
        scratch_shapes=[pltpu.VMEM((tm, tn), f32)] if nk > 1 else [],
        compiler_params=pltpu.CompilerParams(dimension_semantics=("parallel", "parallel", "arbitrary")),
    )(*ins)


def rms_fwd(x, w, *, name):
    t, d = x.shape
    tr = _pick(t, 256, 8)

    def body(x_ref, w_ref, o_ref):
        xv = x_ref[...]
        r = lax.rsqrt(jnp.mean(xv * xv, axis=-1, keepdims=True) + RMS_EPS)
        o_ref[...] = (xv * r * w_ref[...]).astype(bf16)

    return pl.pallas_call(
        body, name=name, grid=(t // tr,),
        in_specs=[pl.BlockSpec((tr, d), lambda i: (i, 0)), pl.BlockSpec((1, d), lambda i: (0, 0))],
        out_specs=pl.BlockSpec((tr, d), lambda i: (i, 0)), out_shape=jax.ShapeDtypeStruct((t, d), bf16),
    )(x, w)


def _rms_bwd_math(xv, wv, dy):
    r = lax.rsqrt(jnp.mean(xv * xv, axis=-1, keepdims=True) + RMS_EPS)
    g = dy * wv
    dx = r * g - xv * (r * r * r) * jnp.mean(g * xv, axis=-1, keepdims=True)
    dw = jnp.sum(dy * xv * r, axis=0, keepdims=True)
    return dx, dw


def rms_bwd(x, w, dy, res, *, name):
    t, d = x.shape
    tr = _pick(t, 256, 8)

    def body(x_ref, w_ref, dy_ref, res_ref, dx_ref, dw_ref):
        dx, dw = _rms_bwd_math(x_ref[...], w_ref[...], dy_ref[...])
        dx_ref[...] = dx + res_ref[...]

        @pl.when(pl.program_id(0) == 0)
        def _():
            dw_ref[...] = jnp.zeros_like(dw_ref)

        dw_ref[...] += dw

    row = pl.BlockSpec((tr, d), lambda i: (i, 0))
    vec = pl.BlockSpec((1, d), lambda i: (0, 0))
    return pl.pallas_call(
        body, name=name, grid=(t // tr,), in_specs=[row, vec, row, row], out_specs=[row, vec],
        out_shape=[jax.ShapeDtypeStruct((t, d), f32), jax.ShapeDtypeStruct((1, d), f32)],
        compiler_params=pltpu.CompilerParams(dimension_semantics=("arbitrary",)),
    )(x, w, dy, res)


def loss_head(h, w, target, *, name):
    t, d = h.shape
    tr = _pick(t, 256, 8)

    def body(h_ref, w_ref, t_ref, loss_ref, dh_ref, dw_ref):
        xv, wv = h_ref[...], w_ref[...]
        r = lax.rsqrt(jnp.mean(xv * xv, axis=-1, keepdims=True) + RMS_EPS)
        err = xv * r * wv - t_ref[...]
        part = 0.5 * jnp.sum(jnp.mean(err * err, axis=-1, keepdims=True), axis=0, keepdims=True)
        dx, dw = _rms_bwd_math(xv, wv, err * (1.0 / d))
        dh_ref[...] = dx

        @pl.when(pl.program_id(0) == 0)
        def _():
            dw_ref[...] = jnp.zeros_like(dw_ref)
            loss_ref[...] = jnp.zeros_like(loss_ref)

        dw_ref[...] += dw
        loss_ref[...] += part

    row = pl.BlockSpec((tr, d), lambda i: (i, 0))
    vec = pl.BlockSpec((1, d), lambda i: (0, 0))
    return pl.pallas_call(
        body, name=name, grid=(t // tr,), in_specs=[row, vec, row],
        out_specs=[pl.BlockSpec((8, 128), lambda i: (0, 0)), row, vec],
        out_shape=[jax.ShapeDtypeStruct((8, 128), f32), jax.ShapeDtypeStruct((t, d), f32), jax.ShapeDtypeStruct((1, d), f32)],
        compiler_params=pltpu.CompilerParams(dimension_semantics=("arbitrary",)),
    )(h, w, target)


CONV_ROWS = 512


def conv_fwd(src, col0, width, w, b, *, silu, name):
    t = src.shape[0]
    tc = _pick(math.gcd(width, col0), 768)
    assert col0 % tc == 0
    cb = col0 // tc
    r = CONV_ROWS

    def body(u_ref, w_ref, b_ref, *rest):
        ext = rest[-1]
        j = pl.program_id(1)

        @pl.when(j == 0)
        def _():
            ext[0:8, :] = jnp.zeros((8, tc), f32)

        @pl.when(j > 0)
        def _():
            ext[0:8, :] = ext[r:r + 8, :]

        ext[8:r + 8, :] = u_ref[...]
        v = ext[...]
        wv = w_ref[...]
        acc = b_ref[...] + wv[3:4, :] * v
        for s in (1, 2, 3):
            acc = acc + wv[3 - s:4 - s, :] * pltpu.roll(v, s, 0)
        pre = acc[8:, :]
        rest[0][...] = pre
        if silu:
            rest[1][...] = _silu(pre)

    tile = pl.BlockSpec((r, tc), lambda c, j: (j, c))
    n_out = 2 if silu else 1
    return pl.pallas_call(
        body, name=name, grid=(width // tc, t // r),
        in_specs=[pl.BlockSpec((r, tc), lambda c, j: (j, cb + c)), pl.BlockSpec((4, tc), lambda c, j: (0, c)),
                  pl.BlockSpec((1, tc), lambda c, j: (0, c))],
        out_specs=[tile] * n_out, out_shape=[jax.ShapeDtypeStruct((t, width), f32)] * n_out,
        scratch_shapes=[pltpu.VMEM((r + 8, tc), f32)],
        compiler_params=pltpu.CompilerParams(dimension_semantics=("parallel", "arbitrary")),
    )(src, w, b)


def conv_bwd(dpost, pre, src, col0, w, dst, *, name):
    t, width = dpost.shape
    tc = _pick(math.gcd(width, col0), 768)
    assert col0 % tc == 0
    cb = col0 // tc
    r = CONV_ROWS
    nt = t // r
    has_pre = pre is not None

    def body(*refs):
        refs = refs[1:]
        if has_pre:
            d_ref, p_ref, u_ref, w_ref, du_ref, dw_ref, db_ref, ext = refs
        else:
            d_ref, u_ref, w_ref, du_ref, dw_ref, db_ref, ext = refs
        j = pl.program_id(1)

        @pl.when(j == 0)
        def _():
            ext[r:r + 8, :] = jnp.zeros((8, tc), f32)
            dw_ref[...] = jnp.zeros_like(dw_ref)
            db_ref[...] = jnp.zeros_like(db_ref)

        @pl.when(j > 0)
        def _():
            ext[r:r + 8, :] = ext[0:8, :]

        dpre = d_ref[...]
        if has_pre:
            dpre = dpre * _dsilu(p_ref[...])
        ext[0:r, :] = dpre
        v = ext[...]
        wv = w_ref[...]
        uv = u_ref[...]
        du = wv[3:4, :] * dpre
        dw_ref[3:4, :] += jnp.sum(dpre * uv, axis=0, keepdims=True)
        for s in (1, 2, 3):
            sh = pltpu.roll(v, r + 8 - s, 0)[0:r, :]
            du = du + wv[3 - s:4 - s, :] * sh
            dw_ref[3 - s:4 - s, :] += jnp.sum(sh * uv, axis=0, keepdims=True)
        db_ref[...] += jnp.sum(dpre, axis=0, keepdims=True)
        du_ref[...] = du.astype(bf16)

    rev = pl.BlockSpec((r, tc), lambda c, j: (nt - 1 - j, c))
    win = pl.BlockSpec((r, tc), lambda c, j: (nt - 1 - j, cb + c))
    in_specs = [ANY, rev] + ([rev] if has_pre else []) + [win, pl.BlockSpec((4, tc), lambda c, j: (0, c))]
    ins = [dst, dpost] + ([pre] if has_pre else []) + [src, w]
    return pl.pallas_call(
        body, name=name, grid=(width // tc, nt), in_specs=in_specs,
        out_specs=[win, pl.BlockSpec((4, tc), lambda c, j: (0, c)), pl.BlockSpec((1, tc), lambda c, j: (0, c))],
        out_shape=[jax.ShapeDtypeStruct(dst.shape, bf16), jax.ShapeDtypeStruct((4, width), f32),
                   jax.ShapeDtypeStruct((1, width), f32)],
        input_output_aliases={0: 0},
        scratch_shapes=[pltpu.VMEM((r + 8, tc), f32)],
        compiler_params=pltpu.CompilerParams(dimension_semantics=("parallel", "arbitrary")),
    )(*ins)


def _ssd_common(xbc_ref, dtr_ref, dtrT_ref, par_row_ref, par_col_ref):
    l = SSM_CHUNK
    x = xbc_ref[:, 0:SSM_GROUP_W]
    bm = xbc_ref[:, SSM_GROUP_W:SSM_GROUP_W + SSM_D_STATE]
    cm = xbc_ref[:, SSM_GROUP_W + SSM_D_STATE:XBC_GROUP_W]
    par_row = par_row_ref[0]
    par_col = par_col_ref[0]
    bias_row, alog_row, d_row = par_row[0:1, :], par_row[1:2, :], par_row[2:3, :]
    bias_col, alog_col = par_col[:, 0:1], par_col[:, 1:2]
    dtr = dtr_ref[0]
    dt = _softplus(dtr + bias_row)
    dt_t = _softplus(dtrT_ref[0] + bias_col)
    a_row = -jnp.exp(alog_row)
    a_col = -jnp.exp(alog_col)
    li = lax.broadcasted_iota(jnp.int32, (l, l), 0)
    si = lax.broadcasted_iota(jnp.int32, (l, l), 1)
    tri = (li >= si).astype(f32)
    cs = _dot_01(tri, dt * a_row, NN, 1, 3)
    cs_t = _dot_01(dt_t * a_col, tri, NT, 0, 3)
    off = lax.broadcasted_iota(jnp.int32, (SSM_HPG, SSM_GROUP_W), 1) - SSM_HEAD_DIM * lax.broadcasted_iota(
        jnp.int32, (SSM_HPG, SSM_GROUP_W), 0)
    ex = ((off >= 0) & (off < SSM_HEAD_DIM)).astype(f32)
    cs_x = _dot_01(cs, ex, NN, 0, 3)
    cl_x = cs_x[l - 1:l, :]
    return dict(x=x, bm=bm, cm=cm, dtr=dtr, dt=dt, a_row=a_row, bias_row=bias_row, tri=tri, li=li, si=si, cs=cs,
                cs_t=cs_t, ex=ex, dt_x=_dot_01(dt, ex, NN, 0, 2), d_x=_dot_01(par_row, ex, NN, 0, 2)[2:3, :], e_x=jnp.exp(cs_x),
                el_x=jnp.exp(cl_x), dec_x=jnp.exp(cl_x - cs_x))


def ssd_fwd(xbc, dtr, dtr_t, par_row, par_col, *, name):
    t = xbc.shape[0]
    nc = t // SSM_CHUNK
    l, p = SSM_CHUNK, SSM_HEAD_DIM

    def body(xbc_ref, dtr_ref, dtrT_ref, prow_ref, pcol_ref, y_ref, sin_ref, state):
        @pl.when(pl.program_id(1) == 0)
        def _():
            state[...] = jnp.zeros_like(state)

        q = _ssd_common(xbc_ref, dtr_ref, dtrT_ref, prow_ref, pcol_ref)
        st = state[...]
        sin_ref[0] = st
        xd = q["x"] * q["dt_x"]
        g = _dot(q["cm"], q["bm"], NT)
        for r in range(SSM_HPG):
            sl = slice(r * p, (r + 1) * p)
            diff = q["cs"][:, r:r + 1] - q["cs_t"][r:r + 1, :]
            lm = jnp.where(q["li"] >= q["si"], jnp.exp(jnp.minimum(diff, 0.0)), 0.0)
            y_ref[:, sl] = _dot(g * lm, xd[:, sl], NN)
        y_ref[...] += q["e_x"] * _dot(q["cm"], st, NN) + q["d_x"] * q["x"]
        state[...] = q["el_x"] * st + _dot(q["bm"].T, xd * q["dec_x"], NN)

    return pl.pallas_call(
        body, name=name, grid=(SSM_GROUPS, nc),
        in_specs=[pl.BlockSpec((l, XBC_GROUP_W), lambda g, c: (c, g)),
                  pl.BlockSpec((1, l, SSM_HPG), lambda g, c: (g, c, 0)),
                  pl.BlockSpec((1, SSM_HPG, l), lambda g, c: (g, 0, c)),
                  pl.BlockSpec((1, 8, 8), lambda g, c: (g, 0, 0)),
                  pl.BlockSpec((1, 8, 8), lambda g, c: (g, 0, 0))],
        out_specs=[pl.BlockSpec((l, SSM_GROUP_W), lambda g, c: (c, g)),
                   pl.BlockSpec((1, SSM_D_STATE, SSM_GROUP_W), lambda g, c: (c, 0, g))],
        out_shape=[jax.ShapeDtypeStruct((t, SSM_D_INNER), f32),
                   jax.ShapeDtypeStruct((nc, SSM_D_STATE, SSM_D_INNER), f32)],
        scratch_shapes=[pltpu.VMEM((SSM_D_STATE, SSM_GROUP_W), f32)],
        compiler_params=pltpu.CompilerParams(dimension_semantics=("parallel", "arbitrary")),
    )(xbc, dtr, dtr_t, par_row, par_col)


def ssd_bwd(xbc, dtr, dtr_t, par_row, par_col, s_in, dy, *, name):
    t = xbc.shape[0]
    nc = t // SSM_CHUNK
    l, p = SSM_CHUNK, SSM_HEAD_DIM

    def body(xbc_ref, dtr_ref, dtrT_ref, prow_ref, pcol_ref, sin_ref, dy_ref, dxbc_ref, ddtr_ref, dpar_ref,
             dstate, yd_buf, dxd_buf):
        @pl.when(pl.program_id(1) == 0)
        def _():
            dstate[...] = jnp.zeros_like(dstate)
            dpar_ref[...] = jnp.zeros_like(dpar_ref)

        q = _ssd_common(xbc_ref, dtr_ref, dtrT_ref, prow_ref, pcol_ref)
        x, bm, cm, ex, li, si = q["x"], q["bm"], q["cm"], q["ex"], q["li"], q["si"]
        e_x, el_x, dec_x = q["e_x"], q["el_x"], q["dec_x"]
        st = sin_ref[0]
        dst = dstate[...]
        dy = dy_ref[...]
        xd = x * q["dt_x"]
        g = _dot(cm, bm, NT)
        dg = jnp.zeros((l, l), f32)
        for r in range(SSM_HPG):
            sl = slice(r * p, (r + 1) * p)
            diff = q["cs"][:, r:r + 1] - q["cs_t"][r:r + 1, :]
            lm = jnp.where(li >= si, jnp.exp(jnp.minimum(diff, 0.0)), 0.0)
            m = (g * lm).astype(bf16)
            xdh, dyh = xd[:, sl].astype(bf16), dy[:, sl].astype(bf16)
            yd_buf[:, sl] = _dot(m, xdh, NN)
            dxd_buf[:, sl] = _dot(m, dyh, TN)
            dg = dg + _dot(dyh, xdh, NT) * lm
        yd, dxd_diag = yd_buf[...], dxd_buf[...]
        yo = e_x * _dot(cm, st, NN)
        dz = e_x * dy
        wv = _dot(bm, dst, NN)
        xw = xd * wv * dec_x
        row8 = lax.broadcasted_iota(jnp.int32, (l, SSM_HPG), 0)
        dy_b, xd_b = dy.astype(bf16).astype(f32), xd.astype(bf16).astype(f32)
        dcs = _dot_01(dy_b * yd - xd_b * dxd_diag + dy * yo - xw, ex, NT, 0, 3)
        tail = jnp.sum(xw, axis=0, keepdims=True) + el_x * jnp.sum(dst * st, axis=0, keepdims=True)
        dcl = _dot_01(jnp.broadcast_to(tail, (SSM_HPG, SSM_GROUP_W)), ex, NT, 0, 3)[0:1, :]
        dcs = dcs + jnp.where(row8 == l - 1, dcl, 0.0)
        dda = _dot_01(q["tri"], dcs, TN, 1, 3)
        dxd = dxd_diag + dec_x * wv
        ddt = _dot_01(dxd * x, ex, NT, 0, 3) + dda * q["a_row"]
        ddtr = ddt * _sigmoid(q["dtr"] + q["bias_row"])
        ddtr_ref[0] = ddtr
        dd = _dot_01(jnp.broadcast_to(jnp.sum(dy * x, axis=0, keepdims=True), (SSM_HPG, SSM_GROUP_W)), ex, NT, 0, 2)[0:1, :]
        dpar_ref[0, 0:1, :] += jnp.sum(ddtr, axis=0, keepdims=True)
        dpar_ref[0, 1:2, :] += jnp.sum(dda * q["dt"], axis=0, keepdims=True) * q["a_row"]
        dpar_ref[0, 2:3, :] += dd
        dxbc_ref[:, 0:SSM_GROUP_W] = dxd * q["dt_x"] + q["d_x"] * dy
        dxbc_ref[:, SSM_GROUP_W:SSM_GROUP_W + SSM_D_STATE] = _dot(dg, cm, TN) + _dot(xd * dec_x, dst, NT)
        dxbc_ref[:, SSM_GROUP_W + SSM_D_STATE:XBC_GROUP_W] = _dot(dg, bm, NN) + _dot(dz, st, NT)
        dstate[...] = _dot(cm.T, dz, NN) + el_x * dst

    rc = lambda c: nc - 1 - c
    return pl.pallas_call(
        body, name=name, grid=(SSM_GROUPS, nc),
        in_specs=[pl.BlockSpec((l, XBC_GROUP_W), lambda g, c: (rc(c), g)),
                  pl.BlockSpec((1, l, SSM_HPG), lambda g, c: (g, rc(c), 0)),
                  pl.BlockSpec((1, SSM_HPG, l), lambda g, c: (g, 0, rc(c))),
                  pl.BlockSpec((1, 8, 8), lambda g, c: (g, 0, 0)),
                  pl.BlockSpec((1, 8, 8), lambda g, c: (g, 0, 0)),
                  pl.BlockSpec((1, SSM_D_STATE, SSM_GROUP_W), lambda g, c: (rc(c), 0, g)),
                  pl.BlockSpec((l, SSM_GROUP_W), lambda g, c: (rc(c), g))],
        out_specs=[pl.BlockSpec((l, XBC_GROUP_W), lambda g, c: (rc(c), g)),
                   pl.BlockSpec((1, l, SSM_HPG), lambda g, c: (g, rc(c), 0)),
                   pl.BlockSpec((1, 8, 8), lambda g, c: (g, 0, 0))],
        out_shape=[jax.ShapeDtypeStruct((t, SSM_CONV_DIM), f32),
                   jax.ShapeDtypeStruct((SSM_GROUPS, t, SSM_HPG), f32),
                   jax.ShapeDtypeStruct((SSM_GROUPS, 8, 8), f32)],
        scratch_shapes=[pltpu.VMEM((SSM_D_STATE, SSM_GROUP_W), f32), pltpu.VMEM((l, SSM_GROUP_W), f32),
                        pltpu.VMEM((l, SSM_GROUP_W), f32)],
        compiler_params=pltpu.CompilerParams(dimension_semantics=("parallel", "arbitrary")),
    )(xbc, dtr, dtr_t, par_row, par_col, s_in, dy)


def gnorm_fwd(y, proj, w, *, name):
    t = y.shape[0]
    tr = _pick(t, 512, 8)
    gw = SSM_GROUP_W
    zb = OFF_Z // gw

    def body(y_ref, z_ref, w_ref, o_ref):
        y2 = y_ref[...] * _silu(z_ref[...])
        r = lax.rsqrt(jnp.mean(y2 * y2, axis=-1, keepdims=True) + RMS_EPS)
        o_ref[...] = (y2 * r * w_ref[...]).astype(bf16)

    return pl.pallas_call(
        body, name=name, grid=(SSM_GROUPS, t // tr),
        in_specs=[pl.BlockSpec((tr, gw), lambda g, i: (i, g)), pl.BlockSpec((tr, gw), lambda g, i: (i, zb + g)),
                  pl.BlockSpec((1, gw), lambda g, i: (0, g))],
        out_specs=pl.BlockSpec((tr, gw), lambda g, i: (i, g)), out_shape=jax.ShapeDtypeStruct((t, SSM_D_INNER), bf16),
    )(y, proj, w)


def gnorm_bwd(y, proj, w, dout, dst, *, name):
    t = y.shape[0]
    tr = _pick(t, 512, 8)
    gw = SSM_GROUP_W
    zb = OFF_Z // gw

    def body(_, y_ref, z_ref, w_ref, do_ref, dy_ref, dz_ref, dw_ref):
        yv, zv = y_ref[...], z_ref[...]
        sz = _silu(zv)
        y2 = yv * sz
        dy2, dw = _rms_bwd_math(y2, w_ref[...], do_ref[...].astype(f32))
        dy_ref[...] = dy2 * sz
        dz_ref[...] = (dy2 * yv * _dsilu(zv)).astype(bf16)

        @pl.when(pl.program_id(1) == 0)
        def _():
            dw_ref[...] = jnp.zeros_like(dw_ref)

        dw_ref[...] += dw

    tile = pl.BlockSpec((tr, gw), lambda g, i: (i, g))
    vec = pl.BlockSpec((1, gw), lambda g, i: (0, g))
    return pl.pallas_call(
        body, name=name, grid=(SSM_GROUPS, t // tr),
        in_specs=[ANY, tile, pl.BlockSpec((tr, gw), lambda g, i: (i, zb + g)), vec, tile],
        out_specs=[tile, pl.BlockSpec((tr, gw), lambda g, i: (i, zb + g)), vec],
        out_shape=[jax.ShapeDtypeStruct((t, SSM_D_INNER), f32), jax.ShapeDtypeStruct(dst.shape, bf16),
                   jax.ShapeDtypeStruct((1, SSM_D_INNER), f32)],
        input_output_aliases={0: 1},
        compiler_params=pltpu.CompilerParams(dimension_semantics=("parallel", "arbitrary")),
    )(dst, y, proj, w, dout)


LRU_ROWS = 256


def _lru_gates(uv, wr_ref, wi_ref, br_ref, bi_ref, lam_ref):
    rg = _sigmoid(_dot(uv, wr_ref[0], NN) + br_ref[...])
    ig = _sigmoid(_dot(uv, wi_ref[0], NN) + bi_ref[...])
    sp = _softplus(-lam_ref[...])
    la = -LRU_C * rg * sp
    a = jnp.exp(la)
    s = jnp.sqrt(jnp.maximum(-_expm1(2.0 * la), 0.0))
    return rg, ig, sp, la, a, s


def lru_fwd(u, proj, w_r, b_r, w_i, b_i, lam, *, name):
    t = u.shape[0]
    r = LRU_ROWS
    lb = LRU_BLOCK
    yb = OFF_LY // lb

    def body(u_ref, y_ref, wr_ref, br_ref, wi_ref, bi_ref, lam_ref, h_ref, o_ref, carry):
        @pl.when(pl.program_id(1) == 0)
        def _():
            carry[...] = jnp.zeros_like(carry)

        uv = u_ref[...]
        _, ig, _, _, a, s = _lru_gates(uv, wr_ref, wi_ref, br_ref, bi_ref, lam_ref)
        b = s * ig * uv
        row = lax.broadcasted_iota(jnp.int32, (r, lb), 0)
        d = 1
        while d < r:
            keep = row >= d
            b = b + a * jnp.where(keep, pltpu.roll(b, d, 0), 0.0)
            a = a * jnp.where(keep, pltpu.roll(a, d, 0), 1.0)
            d *= 2
        h = b + a * carry[0:1, :]
        carry[0:1, :] = h[r - 1:r, :]
        h_ref[...] = h
        o_ref[...] = (h * _gelu(y_ref[...])).astype(bf16)

    tile = pl.BlockSpec((r, lb), lambda hb, j: (j, hb))
    vec = pl.BlockSpec((1, lb), lambda hb, j: (0, hb))
    wsp = pl.BlockSpec((1, lb, lb), lambda hb, j: (hb, 0, 0))
    return pl.pallas_call(
        body, name=name, grid=(LRU_BLOCKS, t // r),
        in_specs=[tile, pl.BlockSpec((r, lb), lambda hb, j: (j, yb + hb)), wsp, vec, wsp, vec, vec],
        out_specs=[tile, tile],
        out_shape=[jax.ShapeDtypeStruct((t, LRU_WIDTH), f32), jax.ShapeDtypeStruct((t, LRU_WIDTH), bf16)],
        scratch_shapes=[pltpu.VMEM((8, lb), f32)],
        compiler_params=pltpu.CompilerParams(dimension_semantics=("parallel", "arbitrary")),
    )(u, proj, w_r, b_r, w_i, b_i, lam)


def lru_bwd(u, proj, hseq, dout, w_r, b_r, w_i, b_i, lam, dst, *, name):
    t = u.shape[0]
    r = LRU_ROWS
    nt = t // r
    lb = LRU_BLOCK
    yb = OFF_LY // lb

    def body(_, u_ref, y_ref, h_ref, hp_ref, do_ref, wr_ref, br_ref, wi_ref, bi_ref, lam_ref,
             du_ref, dy_ref, dwr_ref, dwi_ref, dbr_ref, dbi_ref, dlam_ref, carry_dh, carry_a):
        j = pl.program_id(1)

        @pl.when(j == 0)
        def _():
            carry_dh[...] = jnp.zeros_like(carry_dh)
            carry_a[...] = jnp.zeros_like(carry_a)
            dwr_ref[...] = jnp.zeros_like(dwr_ref)
            dwi_ref[...] = jnp.zeros_like(dwi_ref)
            dbr_ref[...] = jnp.zeros_like(dbr_ref)
            dbi_ref[...] = jnp.zeros_like(dbi_ref)
            dlam_ref[...] = jnp.zeros_like(dlam_ref)

        uv = u_ref[...]
        yv = y_ref[...]
        hv = h_ref[...]
        dov = do_ref[...]
        rg, ig, sp, la, a, s = _lru_gates(uv, wr_ref, wi_ref, br_ref, bi_ref, lam_ref)
        dy_ref[...] = (dov * hv * _dgelu(yv)).astype(bf16)
        gq = dov * _gelu(yv)
        row = lax.broadcasted_iota(jnp.int32, (r, lb), 0)
        an = jnp.where(row < r - 1, pltpu.roll(a, r - 1, 0), carry_a[0:1, :])
        d = 1
        while d < r:
            keep = row < r - d
            gq = gq + an * jnp.where(keep, pltpu.roll(gq, r - d, 0), 0.0)
            an = an * jnp.where(keep, pltpu.roll(an, r - d, 0), 1.0)
            d *= 2
        dh = gq + an * carry_dh[0:1, :]
        carry_dh[0:1, :] = dh[0:1, :]
        carry_a[0:1, :] = a[0:1, :]
        first = jnp.where(j == nt - 1, 0.0, 1.0) * hp_ref[7:8, :]
        hprev = jnp.where(row >= 1, pltpu.roll(hv, 1, 0), first)
        da = dh * hprev
        iu = ig * uv
        e2 = jnp.exp(2.0 * la)
        dla = da * a - dh * iu * e2 / jnp.maximum(s, 1e-30)
        drp = dla * (-LRU_C * sp) * rg * (1.0 - rg)
        dip = dh * s * uv * ig * (1.0 - ig)
        dlam_ref[...] += jnp.sum(dla * (LRU_C * rg) * _sigmoid(-lam_ref[...]), axis=0, keepdims=True)
        du_ref[...] = dh * s * ig + _dot(drp, wr_ref[0], NT) + _dot(dip, wi_ref[0], NT)
        dwr_ref[0] += _dot(uv, drp, TN)
        dwi_ref[0] += _dot(uv, dip, TN)
        dbr_ref[...] += jnp.sum(drp, axis=0, keepdims=True)
        dbi_ref[...] += jnp.sum(dip, axis=0, keepdims=True)

    rj = lambda j: nt - 1 - j
    tile = pl.BlockSpec((r, lb), lambda hb, j: (rj(j), hb))
    vec = pl.BlockSpec((1, lb), lambda hb, j: (0, hb))
    wsp = pl.BlockSpec((1, lb, lb), lambda hb, j: (hb, 0, 0))
    hprev_spec = pl.BlockSpec((8, lb), lambda hb, j: (jnp.maximum(rj(j) * (r // 8) - 1, 0), hb))
    ywin = pl.BlockSpec((r, lb), lambda hb, j: (rj(j), yb + hb))
    return pl.pallas_call(
        body, name=name, grid=(LRU_BLOCKS, nt),
        in_specs=[ANY, tile, ywin, tile, hprev_spec, tile, wsp, vec, wsp, vec, vec],
        out_specs=[tile, ywin, wsp, wsp, vec, vec, vec],
        out_shape=[jax.ShapeDtypeStruct((t, LRU_WIDTH), f32), jax.ShapeDtypeStruct(dst.shape, bf16),
                   jax.ShapeDtypeStruct((LRU_BLOCKS, lb, lb), f32), jax.ShapeDtypeStruct((LRU_BLOCKS, lb, lb), f32),
                   jax.ShapeDtypeStruct((1, LRU_WIDTH), f32), jax.ShapeDtypeStruct((1, LRU_WIDTH), f32),
                   jax.ShapeDtypeStruct((1, LRU_WIDTH), f32)],
        input_output_aliases={0: 1},
        scratch_shapes=[pltpu.VMEM((8, lb), f32), pltpu.VMEM((8, lb), f32)],
        compiler_params=pltpu.CompilerParams(dimension_semantics=("parallel", "arbitrary")),
    )(dst, u, proj, hseq, hseq, dout, w_r, b_r, w_i, b_i, lam)


def merge_fwd(proj, bg, y_ssm, y_lru, *, name):
    t, d = y_ssm.shape
    tr = _pick(t, 256, 8)
    gb = OFF_GATES // d

    def body(gs_ref, gl_ref, bs_ref, bl_ref, ys_ref, yl_ref, o_ref):
        gs = _sigmoid(gs_ref[...] + bs_ref[...])
        gl = _sigmoid(gl_ref[...] + bl_ref[...])
        o_ref[...] = (gs * ys_ref[...].astype(f32) + gl * yl_ref[...].astype(f32)).astype(bf16)

    row = pl.BlockSpec((tr, d), lambda i: (i, 0))
    return pl.pallas_call(
        body, name=name, grid=(t // tr,),
        in_specs=[pl.BlockSpec((tr, d), lambda i: (i, gb)), pl.BlockSpec((tr, d), lambda i: (i, gb + 1)),
                  pl.BlockSpec((1, d), lambda i: (0, 0)), pl.BlockSpec((1, d), lambda i: (0, 1)), row, row],
        out_specs=row, out_shape=jax.ShapeDtypeStruct((t, d), bf16),
    )(proj, proj, bg, bg, y_ssm, y_lru)


def merge_bwd(proj, bg, y_ssm, y_lru, dmix, *, name):
    t, d = y_ssm.shape
    tr = _pick(t, 256, 8)
    gb = OFF_GATES // d

    def body(gs_ref, gl_ref, bs_ref, bl_ref, ys_ref, yl_ref, dm_ref, dg_ref, dys_ref, dyl_ref, dbg_ref):
        gs = _sigmoid(gs_ref[...] + bs_ref[...])
        gl = _sigmoid(gl_ref[...] + bl_ref[...])
        dm = dm_ref[...].astype(f32)
        dys_ref[...] = (dm * gs).astype(bf16)
        dyl_ref[...] = (dm * gl).astype(bf16)
        dgs = dm * ys_ref[...].astype(f32) * gs * (1.0 - gs)
        dgl = dm * yl_ref[...].astype(f32) * gl * (1.0 - gl)
        dg_ref[:, 0:d] = dgs.astype(bf16)
        dg_ref[:, d:2 * d] = dgl.astype(bf16)

        @pl.when(pl.program_id(0) == 0)
        def _():
            dbg_ref[...] = jnp.zeros_like(dbg_ref)

        dbg_ref[:, 0:d] += jnp.sum(dgs, axis=0, keepdims=True)
        dbg_ref[:, d:2 * d] += jnp.sum(dgl, axis=0, keepdims=True)

    row = pl.BlockSpec((tr, d), lambda i: (i, 0))
    return pl.pallas_call(
        body, name=name, grid=(t // tr,),
        in_specs=[pl.BlockSpec((tr, d), lambda i: (i, gb)), pl.BlockSpec((tr, d), lambda i: (i, gb + 1)),
                  pl.BlockSpec((1, d), lambda i: (0, 0)), pl.BlockSpec((1, d), lambda i: (0, 1)), row, row, row],
        out_specs=[pl.BlockSpec((tr, 2 * d), lambda i: (i, OFF_GATES // (2 * d))), row, row,
                   pl.BlockSpec((1, 2 * d), lambda i: (0, 0))],
        out_shape=[jax.ShapeDtypeStruct((t, PROJ_W), bf16), jax.ShapeDtypeStruct((t, d), bf16),
                   jax.ShapeDtypeStruct((t, d), bf16), jax.ShapeDtypeStruct((1, 2 * d), f32)],
        compiler_params=pltpu.CompilerParams(dimension_semantics=("arbitrary",)),
    )(proj, proj, bg, bg, y_ssm, y_lru, dmix)


def swiglu_fwd(ff, *, name):
    t = ff.shape[0]
    hd = FFN_HIDDEN
    tr = _pick(t, 128, 8)

    def body(f_ref, o_ref):
        o_ref[...] = (_silu(f_ref[:, 0:hd].astype(f32)) * f_ref[:, hd:2 * hd].astype(f32)).astype(bf16)

    return pl.pallas_call(
        body, name=name, grid=(t // tr,), in_specs=[pl.BlockSpec((tr, 2 * hd), lambda i: (i, 0))],
        out_specs=pl.BlockSpec((tr, hd), lambda i: (i, 0)), out_shape=jax.ShapeDtypeStruct((t, hd), bf16),
    )(ff)


def swiglu_bwd(ff, dact, *, name):
    t = ff.shape[0]
    hd = FFN_HIDDEN
    tr = _pick(t, 128, 8)

    def body(f_ref, d_ref, o_ref):
        gate, up, dv = f_ref[:, 0:hd].astype(f32), f_ref[:, hd:2 * hd].astype(f32), d_ref[...].astype(f32)
        o_ref[:, 0:hd] = (dv * up * _dsilu(gate)).astype(bf16)
        o_ref[:, hd:2 * hd] = (dv * _silu(gate)).astype(bf16)

    return pl.pallas_call(
        body, name=name, grid=(t // tr,),
        in_specs=[pl.BlockSpec((tr, 2 * hd), lambda i: (i, 0)), pl.BlockSpec((tr, hd), lambda i: (i, 0))],
        out_specs=pl.BlockSpec((tr, 2 * hd), lambda i: (i, 0)), out_shape=jax.ShapeDtypeStruct((t, 2 * hd), bf16),
    )(ff, dact)


def _adam_math(w, g, m, v):
    m = ADAM_B1 * m + (1.0 - ADAM_B1) * g
    v = ADAM_B2 * v + (1.0 - ADAM_B2) * (g * g)
    m_hat = m / (1.0 - ADAM_B1 ** ADAM_STEP)
    v_hat = v / (1.0 - ADAM_B2 ** ADAM_STEP)
    delta = -ADAM_LR * (m_hat / (jnp.sqrt(v_hat) + ADAM_EPS) + ADAM_WD * w)
    return delta, m, v


def _row_tile(rows, cols):
    cap = max(8, (1 << 18) // cols)
    return _pick(rows, cap, 8) if rows % 8 == 0 else rows


def adamw(w, g, m, v, *, name):
    rows, cols = w.shape
    tr = _row_tile(rows, cols)

    def body(w_ref, g_ref, m_ref, v_ref, d_ref, nm_ref, nv_ref):
        d, nm, nv = _adam_math(w_ref[...], g_ref[...], m_ref[...], v_ref[...])
        d_ref[...] = d
        nm_ref[...] = nm
        nv_ref[...] = nv

    tile = pl.BlockSpec((tr, cols), lambda i: (i, 0))
    return pl.pallas_call(
        body, name=name, grid=(rows // tr,), in_specs=[tile] * 4, out_specs=[tile] * 3,
        out_shape=[jax.ShapeDtypeStruct((rows, cols), f32)] * 3,
    )(w, g, m, v)


def adamw_many(ws, gs, ms, vs, *, name):
    n = len(ws)

    def body(*refs):
        for i in range(n):
            d, nm, nv = _adam_math(refs[i][...], refs[n + i][...], refs[2 * n + i][...], refs[3 * n + i][...])
            refs[4 * n + 3 * i][...] = d
            refs[4 * n + 3 * i + 1][...] = nm
            refs[4 * n + 3 * i + 2][...] = nv

    outs = pl.pallas_call(
        body, name=name, out_shape=[jax.ShapeDtypeStruct(w.shape, f32) for w in ws for _ in range(3)],
    )(*ws, *gs, *ms, *vs)
    return [tuple(outs[3 * i:3 * i + 3]) for i in range(n)]


def pair_add(dw, rbuf, idx, *, name):
    n, rows, cols = dw.shape
    hr = rows // 2
    tr = _row_tile(hr, cols)
    nrt = hr // tr

    def body(idx_ref, a_ref, b_ref, o_ref, own_ref):
        s = a_ref[...] + b_ref[...]
        o_ref[...] = s.astype(bf16)

        @pl.when(pl.program_id(1) == idx_ref[0])
        def _():
            own_ref[...] = s[0]

    return pl.pallas_call(
        body, name=name,
        grid_spec=pltpu.PrefetchScalarGridSpec(
            num_scalar_prefetch=1, grid=(nrt, n),
            in_specs=[pl.BlockSpec((1, tr, cols), lambda i, k, idx: (k, idx[1] * nrt + i, 0)),
                      pl.BlockSpec((1, tr, cols), lambda i, k, idx: (k, i, 0))],
            out_specs=[pl.BlockSpec((1, tr, cols), lambda i, k, idx: (k, i, 0)),
                       pl.BlockSpec((tr, cols), lambda i, k, idx: (i, 0))]),
        out_shape=[jax.ShapeDtypeStruct((n, hr, cols), bf16), jax.ShapeDtypeStruct((hr, cols), f32)],
    )(idx, dw, rbuf)


def chip_sum(own, rbuf, idx, *, name):
    hr, cols = own.shape
    tr = _row_tile(hr, cols)
    nrt = hr // tr

    def body(idx_ref, a_ref, b_ref, o_ref):
        o_ref[...] = ((a_ref[...] + b_ref[0].astype(f32)) + b_ref[1].astype(f32)) + b_ref[2].astype(f32)

    return pl.pallas_call(
        body, name=name,
        grid_spec=pltpu.PrefetchScalarGridSpec(
            num_scalar_prefetch=1, grid=(nrt,),
            in_specs=[pl.BlockSpec((tr, cols), lambda i, idx: (i, 0)),
                      pl.BlockSpec((3, tr, cols), lambda i, idx: (0, i, 0))],
            out_specs=pl.BlockSpec((tr, cols), lambda i, idx: (idx[1] * nrt + i, 0))),
        out_shape=jax.ShapeDtypeStruct((2 * hr, cols), f32),
    )(idx, own, rbuf)


def sum8(rbuf, *, name):
    n, rows, cols = rbuf.shape
    tr = _row_tile(rows, cols * n)

    def body(a_ref, o_ref):
        acc = a_ref[0]
        for k in range(1, n):
            acc = acc + a_ref[k]
        o_ref[...] = acc

    return pl.pallas_call(
        body, name=name, grid=(rows // tr,), in_specs=[pl.BlockSpec((n, tr, cols), lambda i: (0, i, 0))],
        out_specs=pl.BlockSpec((tr, cols), lambda i: (i, 0)), out_shape=jax.ShapeDtypeStruct((rows, cols), f32),
    )(rbuf)


def _coords():
    return lax.axis_index("x"), lax.axis_index("y"), lax.axis_index("c")


def _other_chips(x, y):
    return [(1 - x, y), (x, 1 - y), (1 - x, 1 - y)]


def gather_weights(shards, *, name):
    n = len(shards)
    halves = [s.shape[0] // 2 for s in shards]

    def body(*refs):
        ins, outs = refs[:n], refs[n:2 * n]
        send1, recv1, send2, recv2 = refs[2 * n:]
        x, y, c = _coords()
        me = 2 * x + y
        chips = _other_chips(x, y)
        sibling = (x, y, 1 - c)

        def half(i, k, hc):
            return outs[i].at[k, pl.ds(hc * halves[i], halves[i]), :]

        def ici(i, j):
            return pltpu.make_async_remote_copy(
                src_ref=ins[i].at[pl.ds(c * halves[i], halves[i]), :], dst_ref=half(i, me, c),
                send_sem=send1.at[i, j], recv_sem=recv1.at[i, j], device_id=(*chips[j], c), device_id_type=MESH)

        def landed(i, j):
            kj = 2 * chips[j][0] + chips[j][1]
            return pltpu.make_async_remote_copy(
                src_ref=half(i, kj, c), dst_ref=half(i, kj, c),
                send_sem=send2.at[i, j], recv_sem=recv1.at[i, j], device_id=sibling, device_id_type=MESH)

        def from_sibling(i, j):
            kj = 2 * chips[j][0] + chips[j][1]
            return pltpu.make_async_remote_copy(
                src_ref=half(i, kj, 1 - c), dst_ref=half(i, kj, 1 - c),
                send_sem=send2.at[i, j], recv_sem=recv2.at[i, j], device_id=sibling, device_id_type=MESH)

        def d2d(i, j):
            kj = 2 * chips[j][0] + chips[j][1]
            return pltpu.make_async_remote_copy(
                src_ref=half(i, kj, c), dst_ref=half(i, kj, c),
                send_sem=send2.at[i, j], recv_sem=recv2.at[i, j], device_id=sibling, device_id_type=MESH)

        for j in range(3):
            for i in range(n):
                ici(i, j).start()
        for j in range(3):
            for i in range(n):
                landed(i, j).wait_recv()
                d2d(i, j).start()
        for j in range(3):
            for i in range(n):
                from_sibling(i, j).wait_recv()
        for j in range(3):
            for i in range(n):
                ici(i, j).wait_send()
                d2d(i, j).wait_send()

    return pl.pallas_call(
        body, name=name, in_specs=[ANY] * n, out_specs=[ANY] * n,
        out_shape=[jax.ShapeDtypeStruct((N_CHIPS,) + s.shape, s.dtype) for s in shards],
        scratch_shapes=[pltpu.SemaphoreType.DMA((n, 3))] * 4,
    )(*shards)


def pair_exchange(grads, *, name):
    n = len(grads)
    halves = [g.shape[1] // 2 for g in grads]

    def body(*refs):
        ins, outs = refs[:n], refs[n:2 * n]
        send, recv = refs[2 * n:]
        x, y, c = _coords()
        cps = [pltpu.make_async_remote_copy(
            src_ref=ins[i].at[:, pl.ds((1 - c) * halves[i], halves[i]), :], dst_ref=outs[i],
            send_sem=send.at[i], recv_sem=recv.at[i], device_id=(x, y, 1 - c), device_id_type=MESH) for i in range(n)]
        for cp in cps:
            cp.start()
        for cp in cps:
            cp.wait()

    return pl.pallas_call(
        body, name=name, in_specs=[ANY] * n, out_specs=[ANY] * n,
        out_shape=[jax.ShapeDtypeStruct((N_CHIPS, g.shape[1] // 2, g.shape[2]), g.dtype) for g in grads],
        scratch_shapes=[pltpu.SemaphoreType.DMA((n,))] * 2,
    )(*grads)


def pair_gather(bufs, *, name):
    n = len(bufs)

    def body(*refs):
        ins, outs = refs[:n], refs[n:2 * n]
        send, recv = refs[2 * n:]
        x, y, c = _coords()
        cps = []
        for i in range(n):
            hr = ins[i].shape[0] // 2
            cps.append(pltpu.make_async_remote_copy(
                src_ref=ins[i].at[pl.ds(c * hr, hr), :], dst_ref=outs[i].at[pl.ds(c * hr, hr), :],
                send_sem=send.at[i], recv_sem=recv.at[i], device_id=(x, y, 1 - c), device_id_type=MESH))
        for cp in cps:
            cp.start()
        for i in range(n):
            hr = ins[i].shape[0] // 2
            pltpu.make_async_remote_copy(
                src_ref=ins[i].at[pl.ds((1 - c) * hr, hr), :], dst_ref=outs[i].at[pl.ds((1 - c) * hr, hr), :],
                send_sem=send.at[i], recv_sem=recv.at[i], device_id=(x, y, 1 - c), device_id_type=MESH).wait_recv()
        for cp in cps:
            cp.wait_send()

    return pl.pallas_call(
        body, name=name, in_specs=[ANY] * n, out_specs=[ANY] * n,
        out_shape=[jax.ShapeDtypeStruct(b.shape, b.dtype) for b in bufs],
        input_output_aliases={i: i for i in range(n)},
        scratch_shapes=[pltpu.SemaphoreType.DMA((n,))] * 2,
    )(*bufs)


def all_exchange(buf, *, name):
    rows, cols = buf.shape

    def body(in_ref, out_ref, send, recv):
        x, y, c = _coords()
        me = 4 * x + 2 * y + c
        cps = []
        for d in range(1, 8):
            px = 1 - x if d & 4 else x
            py = 1 - y if d & 2 else y
            pc = 1 - c if d & 1 else c
            cps.append(pltpu.make_async_remote_copy(
                src_ref=in_ref, dst_ref=out_ref.at[me], send_sem=send.at[d - 1], recv_sem=recv.at[d - 1],
                device_id=(px, py, pc), device_id_type=MESH))
        for cp in cps:
            cp.start()
        for d in range(1, 8):
            px = 1 - x if d & 4 else x
            py = 1 - y if d & 2 else y
            pc = 1 - c if d & 1 else c
            src = 4 * px + 2 * py + pc
            pltpu.make_async_remote_copy(
                src_ref=in_ref, dst_ref=out_ref.at[src], send_sem=send.at[d - 1], recv_sem=recv.at[d - 1],
                device_id=(px, py, pc), device_id_type=MESH).wait_recv()
        for cp in cps:
            cp.wait_send()

    return pl.pallas_call(
        body, name=name, in_specs=[ANY], out_specs=ANY,
        out_shape=jax.ShapeDtypeStruct((8, rows, cols), buf.dtype),
        scratch_shapes=[pltpu.SemaphoreType.DMA((7,)), pltpu.SemaphoreType.DMA((7,))],
    )(buf)


HBM = pl.BlockSpec(memory_space=pltpu.HBM)
SEM = pl.BlockSpec(memory_space=pltpu.SEMAPHORE)
EFFECT = pltpu.SideEffectType.DATAFLOW_SIDE_EFFECTING


def split_start(arrays, after, copies, sem_shape, *, name):
    na = len(arrays)

    def body(*refs):
        for cp in copies(refs[:na], refs[na + 1], refs[na + 2]):
            cp.start()
        refs[-1][...] = jnp.zeros((8, 128), f32)

    outs = pl.pallas_call(
        body, name=name,
        out_shape=(pltpu.SemaphoreType.DMA(sem_shape), pltpu.SemaphoreType.DMA(sem_shape),
                   *[pltpu.HBM(a.shape, a.dtype) for a in arrays], jax.ShapeDtypeStruct((8, 128), f32)),
        in_specs=[HBM] * na + [ANY], out_specs=(SEM, SEM, *[HBM] * na, pl.BlockSpec(memory_space=pltpu.VMEM)),
        input_output_aliases={i: 2 + i for i in range(na)},
        compiler_params=pltpu.CompilerParams(has_side_effects=EFFECT),
    )(*[pltpu.with_memory_space_constraint(a, pltpu.HBM) for a in arrays], after)
    return outs[0], outs[1], list(outs[2:2 + na]), outs[-1]


def split_wait(send, recv, arrays, after, copies, *, name):
    na = len(arrays)

    def body(*refs):
        for cp in copies(refs[:na], refs[na], refs[na + 1]):
            cp.wait_send()
            cp.wait_recv()

    outs = pl.pallas_call(
        body, name=name, out_shape=tuple(pltpu.HBM(a.shape, a.dtype) for a in arrays),
        in_specs=[HBM] * na + [SEM, SEM, ANY], out_specs=tuple([HBM] * na),
        input_output_aliases={i: i for i in range(na)},
        compiler_params=pltpu.CompilerParams(has_side_effects=EFFECT),
    )(*arrays, send, recv, after)
    return list(outs)


def gather_copies(n):
    def copies(refs, send, recv):
        x, y, c = _coords()
        me = 2 * x + y
        chips = _other_chips(x, y)
        return [pltpu.make_async_remote_copy(
            src_ref=refs[i], dst_ref=refs[n + i].at[me], send_sem=send.at[3 * i + j], recv_sem=recv.at[3 * i + j],
            device_id=(*chips[j], c), device_id_type=MESH) for j in range(3) for i in range(n)]
    return copies


def pair_copies(n):
    def copies(refs, send, recv):
        x, y, c = _coords()
        cps = []
        for i in range(n):
            hr = refs[i].shape[1] // 2
            cps.append(pltpu.make_async_remote_copy(
                src_ref=refs[i].at[:, pl.ds((1 - c) * hr, hr), :], dst_ref=refs[n + i], send_sem=send.at[i],
                recv_sem=recv.at[i], device_id=(x, y, 1 - c), device_id_type=MESH))
        return cps
    return copies


def all_copies():
    def copies(refs, send, recv):
        x, y, c = _coords()
        me = 4 * x + 2 * y + c
        cps = []
        for d in range(1, 8):
            peer = (1 - x if d & 4 else x, 1 - y if d & 2 else y, 1 - c if d & 1 else c)
            cps.append(pltpu.make_async_remote_copy(
                src_ref=refs[0], dst_ref=refs[1].at[me], send_sem=send.at[d - 1], recv_sem=recv.at[d - 1],
                device_id=peer, device_id_type=MESH))
        return cps
    return copies


def reduce_copies(n):
    def copies(refs, send, recv):
        x, y, c = _coords()
        chips = _other_chips(x, y)
        return [pltpu.make_async_remote_copy(
            src_ref=refs[i].at[2 * chips[j][0] + chips[j][1]], dst_ref=refs[n + i].at[j],
            send_sem=send.at[3 * i + j], recv_sem=recv.at[3 * i + j], device_id=(*chips[j], c), device_id_type=MESH)
            for j in range(3) for i in range(n)]
    return copies


def _pack(arrs):
    flat = []
    for a in arrs:
        v = a.reshape(-1).astype(f32)
        pad = (-v.shape[0]) % 128
        flat.append(jnp.pad(v, (0, pad)) if pad else v)
    v = jnp.concatenate(flat)
    rows = v.shape[0] // 128
    pad_rows = (-rows) % 256
    v = v.reshape(rows, 128)
    return jnp.pad(v, ((0, pad_rows), (0, 0))) if pad_rows else v


def _unpack(buf, shapes):
    out, row = [], 0
    for s in shapes:
        size = math.prod(s)
        rows = -(-size // 128)
        out.append(buf[row:row + rows].reshape(-1)[:size].reshape(s))
        row += rows
    return out


def _perm_in_cols(w):
    gates, z = w[..., 0:2048], w[..., 2048:4096]
    xbc = w[..., 4096:7168]
    dt, lx, ly = w[..., 7168:7200], w[..., 7200:8480], w[..., 8480:9760]
    pad = jnp.zeros(w.shape[:-1] + (DT_PAD_W - SSM_HEADS,), w.dtype)
    return jnp.concatenate([gates, z, lx, ly, dt, pad, _perm_xbc_cols(xbc)], axis=-1)


def _perm_xbc_cols(w):
    parts = []
    for g in range(SSM_GROUPS):
        parts += [w[..., g * 512:(g + 1) * 512], w[..., 2048 + g * 128:2048 + (g + 1) * 128],
                  w[..., 2560 + g * 128:2560 + (g + 1) * 128]]
    return jnp.concatenate(parts, axis=-1)


def _unperm_xbc_cols(w):
    xs = [w[..., g * XBC_GROUP_W:g * XBC_GROUP_W + 512] for g in range(SSM_GROUPS)]
    bs = [w[..., g * XBC_GROUP_W + 512:g * XBC_GROUP_W + 640] for g in range(SSM_GROUPS)]
    cs = [w[..., g * XBC_GROUP_W + 640:(g + 1) * XBC_GROUP_W] for g in range(SSM_GROUPS)]
    return jnp.concatenate(xs + bs + cs, axis=-1)


def _unperm_in_cols(w):
    xbc = _unperm_xbc_cols(w[..., OFF_XBC:OFF_XBC + 3072])
    return jnp.concatenate([w[..., OFF_GATES:OFF_GATES + 2048], w[..., OFF_Z:OFF_Z + 2048], xbc,
                            w[..., OFF_DT:OFF_DT + 32], w[..., OFF_LX:OFF_LX + 1280], w[..., OFF_LY:OFF_LY + 1280]], axis=-1)


def _col_shards(w, n=N_CHIPS):
    r, c = w.shape
    return jnp.transpose(w.reshape(r, n, c // n), (1, 0, 2))


def _from_col_shards(w):
    n, r, c = w.shape
    return jnp.transpose(w, (1, 0, 2)).reshape(r, n * c)


def kernel(x, norm1_w, w_in, b_branch_gate, ssm_conv_w, ssm_conv_b, ssm_dt_bias, ssm_a_log, ssm_d, ssm_norm_w, w_out_ssm, lru_conv_w, lru_conv_b, lru_w_r, lru_b_r, lru_w_i, lru_b_i, lru_lambda, w_out_lru, w_out, norm2_w, w_ffn_in, w_ffn_out, norm_f_w, loss_target, m_norm1_w, m_w_in, m_b_branch_gate, m_ssm_conv_w, m_ssm_conv_b, m_ssm_dt_bias, m_ssm_a_log, m_ssm_d, m_ssm_norm_w, m_w_out_ssm, m_lru_conv_w, m_lru_conv_b, m_lru_w_r, m_lru_b_r, m_lru_w_i, m_lru_b_i, m_lru_lambda, m_w_out_lru, m_w_out, m_norm2_w, m_w_ffn_in, m_w_ffn_out, m_norm_f_w, v_norm1_w, v_w_in, v_b_branch_gate, v_ssm_conv_w, v_ssm_conv_b, v_ssm_dt_bias, v_ssm_a_log, v_ssm_d, v_ssm_norm_w, v_w_out_ssm, v_lru_conv_w, v_lru_conv_b, v_lru_w_r, v_lru_b_r, v_lru_w_i, v_lru_b_i, v_lru_lambda, v_w_out_lru, v_w_out, v_norm2_w, v_w_ffn_in, v_w_ffn_out, v_norm_f_w):
    xi, yi, ci = lax.axis_index("x"), lax.axis_index("y"), lax.axis_index("c")
    me = 2 * xi + yi
    idx = jnp.stack([me, ci]).astype(jnp.int32)
    x2 = x[0]
    tgt = loss_target[0]

    big_names = ["w_in", "w_out_ssm", "w_out_lru", "w_out", "w_ffn_in", "w_ffn_out"]
    big_w = dict(w_in=w_in[0], w_out_ssm=w_out_ssm[0], w_out_lru=w_out_lru[0], w_out=w_out[0], w_ffn_in=w_ffn_in[0],
                 w_ffn_out=w_ffn_out[0])
    big_m = dict(w_in=m_w_in[0], w_out_ssm=m_w_out_ssm[0], w_out_lru=m_w_out_lru[0], w_out=m_w_out[0],
                 w_ffn_in=m_w_ffn_in[0], w_ffn_out=m_w_ffn_out[0])
    big_v = dict(w_in=v_w_in[0], w_out_ssm=v_w_out_ssm[0], w_out_lru=v_w_out_lru[0], w_out=v_w_out[0],
                 w_ffn_in=v_w_ffn_in[0], w_ffn_out=v_w_ffn_out[0])
    conv_pad = jnp.zeros((16, 768), f32).at[0:4, :].set(ssm_conv_w[0]).at[8:12, 0:320].set(lru_conv_w[0])
    mine = [big_w["w_in"].astype(bf16), conv_pad]
    gathered = gather_weights(mine, name="gather_weights")
    g_in, g_conv = [lax.dynamic_update_index_in_dim(g, s, me, 0) for g, s in zip(gathered, mine)]
    w_in_p = _perm_in_cols(_from_col_shards(g_in))
    late_names = big_names[1:]
    late = [big_w[k].astype(bf16) for k in late_names]
    late_lands = [lax.empty((N_CHIPS,) + s.shape, bf16) for s in late]
    g_send, g_recv, g_arrays, g_token = split_start(late + late_lands, g_conv, gather_copies(5), (15,),
                                                    name="gather_late_start")
    ssm_cw_full = _from_col_shards(g_conv[:, 0:4, :])
    lru_cw_full = _from_col_shards(g_conv[:, 8:12, 0:320])
    ssm_cw_p = _perm_xbc_cols(ssm_cw_full)
    ssm_cb_p = _perm_xbc_cols(ssm_conv_b)

    par = jnp.stack([ssm_dt_bias[0], ssm_a_log[0], ssm_d[0]], axis=0).reshape(3, SSM_GROUPS, SSM_HPG)
    par_row = jnp.zeros((SSM_GROUPS, 8, 8), f32).at[:, 0:3, :].set(jnp.transpose(par, (1, 0, 2)))
    par_col = jnp.transpose(par_row, (0, 2, 1))

    hn1 = rms_fwd(x2, norm1_w + g_token[0:1, 0:1], name="rms1_fwd")
    proj = mm(hn1, w_in_p, "nn", name="in_proj")
    t = x2.shape[0]
    dtr = jnp.transpose(proj[:, OFF_DT:OFF_DT + 32].reshape(t, SSM_GROUPS, SSM_HPG), (1, 0, 2))
    dtr_t = jnp.transpose(dtr, (0, 2, 1))
    xbc_pre, xbc_post = conv_fwd(proj, OFF_XBC, SSM_CONV_DIM, ssm_cw_p, ssm_cb_p, silu=True, name="ssm_conv_fwd")
    y_ssd, s_in = ssd_fwd(xbc_post, dtr, dtr_t, par_row, par_col, name="ssd_fwd")
    yn = gnorm_fwd(y_ssd, proj, ssm_norm_w, name="gnorm_fwd")
    g_arrays = split_wait(g_send, g_recv, g_arrays, yn, gather_copies(5), name="gather_late_wait")
    g_out_ssm, g_out_lru, g_out, g_ffn_in, g_ffn_out = [
        lax.dynamic_update_index_in_dim(g, s, me, 0) for g, s in zip(g_arrays[5:], late)]
    w_out_ssm_f = g_out_ssm.reshape(SSM_D_INNER, D_MODEL)
    w_out_lru_f = g_out_lru.reshape(LRU_WIDTH, D_MODEL)
    w_out_f = g_out.reshape(D_MODEL, D_MODEL)
    w_ffn_in_f = _from_col_shards(g_ffn_in)
    w_ffn_out_f = g_ffn_out.reshape(FFN_HIDDEN, D_MODEL)
    y_ssm = mm(yn, w_out_ssm_f, "nn", out_dtype=bf16, name="out_ssm")
    (u_lru,) = conv_fwd(proj, OFF_LX, LRU_WIDTH, lru_cw_full, lru_conv_b, silu=False, name="lru_conv_fwd")
    h_lru, o_lru = lru_fwd(u_lru, proj, lru_w_r[0], lru_b_r, lru_w_i[0], lru_b_i, lru_lambda, name="lru_fwd")
    y_lru = mm(o_lru, w_out_lru_f, "nn", out_dtype=bf16, name="out_lru")
    mix = merge_fwd(proj, b_branch_gate, y_ssm, y_lru, name="merge_fwd")
    h1 = mm(mix, w_out_f, "nn", add=x2, name="out_proj")
    hn2 = rms_fwd(h1, norm2_w, name="rms2_fwd")
    ff = mm(hn2, w_ffn_in_f, "nn", out_dtype=bf16, name="ffn_in")
    act = swiglu_fwd(ff, name="swiglu_fwd")
    h2 = mm(act, w_ffn_out_f, "nn", add=h1, name="ffn_out")
    loss_tile, dh2, d_norm_f = loss_head(h2, norm_f_w.reshape(1, D_MODEL), tgt, name="loss_head")
    loss = lax.psum(loss_tile[0, 0], ("x", "y", "c"))

    d_w_ffn_out = mm(act, dh2, "tn", name="d_w_ffn_out")
    dact = mm(dh2, w_ffn_out_f, "nt", out_dtype=bf16, name="d_act")
    dff = swiglu_bwd(ff, dact, name="swiglu_bwd")
    d_w_ffn_in = mm(hn2, dff, "tn", name="d_w_ffn_in")
    dhn2 = mm(dff, w_ffn_in_f, "nt", name="d_hn2")
    dh1, d_norm2 = rms_bwd(h1, norm2_w, dhn2, dh2, name="rms2_bwd")
    d_w_out = mm(mix, dh1, "tn", name="d_w_out")
    dmix = mm(dh1, w_out_f, "nt", out_dtype=bf16, name="d_mix")
    dproj, dy_ssm, dy_lru, d_bg = merge_bwd(proj, b_branch_gate, y_ssm, y_lru, dmix, name="merge_bwd")
    d_w_out_ssm = mm(yn, dy_ssm, "tn", name="d_w_out_ssm")
    d_w_out_lru = mm(o_lru, dy_lru, "tn", name="d_w_out_lru")
    early_g = [d_w_out_ssm.reshape(N_CHIPS, 512, D_MODEL), d_w_out_lru.reshape(N_CHIPS, 320, D_MODEL),
               d_w_out.reshape(N_CHIPS, 256, D_MODEL), _col_shards(d_w_ffn_in), d_w_ffn_out.reshape(N_CHIPS, 704, D_MODEL)]
    p_lands = [lax.empty((N_CHIPS, g.shape[1] // 2, g.shape[2]), f32) for g in early_g]
    p_send, p_recv, p_arrays, p_token = split_start(early_g + p_lands, early_g[0], pair_copies(5), (5,),
                                                    name="pair_early_start")
    dyn = mm(dy_ssm, w_out_ssm_f, "nt", out_dtype=bf16, after=p_token, name="d_yn")
    dy_ssd, dproj, d_ssm_norm = gnorm_bwd(y_ssd, proj, ssm_norm_w, dyn, dproj, name="gnorm_bwd")
    p_arrays = split_wait(p_send, p_recv, p_arrays, dy_ssd, pair_copies(5), name="pair_early_wait")
    e_pairs = [pair_add(g, rb, idx, name="pair_add_" + k) for g, rb, k in zip(p_arrays[:5], p_arrays[5:], late_names)]
    e_lands = [lax.empty((3,) + p[0].shape[1:], bf16) for p in e_pairs]
    e_send, e_recv, e_arrays, e_token = split_start([p[0] for p in e_pairs] + e_lands, e_pairs[0][1], reduce_copies(5),
                                                    (15,), name="reduce_early_start")
    dxbc_post, ddtr, dpar = ssd_bwd(xbc_post, dtr, dtr_t, par_row + e_token[0:1, 0:1], par_col, s_in, dy_ssd,
                                    name="ssd_bwd")
    dproj, d_ssm_cw_p, d_ssm_cb_p = conv_bwd(dxbc_post, xbc_pre, proj, OFF_XBC, ssm_cw_p, dproj, name="ssm_conv_bwd")
    do_lru = mm(dy_lru, w_out_lru_f, "nt", name="d_o_lru")
    du_lru, dproj, d_w_r, d_w_i, d_b_r, d_b_i, d_lam = lru_bwd(u_lru, proj, h_lru, do_lru, lru_w_r[0], lru_b_r, lru_w_i[0],
                                                               lru_b_i, lru_lambda, dproj, name="lru_bwd")
    dproj, d_lru_cw, d_lru_cb = conv_bwd(du_lru, None, proj, OFF_LX, lru_cw_full, dproj, name="lru_conv_bwd")
    ddt_cols = jnp.transpose(ddtr, (1, 0, 2)).reshape(t, SSM_HEADS).astype(bf16)
    ddt_cols = jnp.pad(ddt_cols, ((0, 0), (0, DT_PAD_W - SSM_HEADS)))
    dproj = lax.dynamic_update_slice(dproj, ddt_cols, (0, OFF_DT))

    d_ssm_cw = _unperm_xbc_cols(d_ssm_cw_p)
    d_ssm_cb = _unperm_xbc_cols(d_ssm_cb_p)
    dpar_h = jnp.transpose(dpar[:, 0:3, :], (1, 0, 2)).reshape(3, SSM_HEADS)
    small_names = ["norm1_w", "b_branch_gate", "ssm_conv_b", "ssm_dt_bias", "ssm_a_log", "ssm_d", "ssm_norm_w",
                   "lru_conv_b", "lru_w_r", "lru_b_r", "lru_w_i", "lru_b_i", "lru_lambda", "norm2_w", "norm_f_w"]
    small_g = dict(norm1_w=jnp.zeros_like(norm1_w), b_branch_gate=d_bg, ssm_conv_b=d_ssm_cb, ssm_dt_bias=dpar_h[0:1], ssm_a_log=dpar_h[1:2],
                   ssm_d=dpar_h[2:3], ssm_norm_w=d_ssm_norm, lru_conv_b=d_lru_cb, lru_w_r=d_w_r[None], lru_b_r=d_b_r,
                   lru_w_i=d_w_i[None], lru_b_i=d_b_i, lru_lambda=d_lam, norm2_w=d_norm2, norm_f_w=d_norm_f.reshape(D_MODEL))
    small_w = dict(norm1_w=norm1_w, b_branch_gate=b_branch_gate, ssm_conv_b=ssm_conv_b, ssm_dt_bias=ssm_dt_bias,
                   ssm_a_log=ssm_a_log, ssm_d=ssm_d, ssm_norm_w=ssm_norm_w, lru_conv_b=lru_conv_b, lru_w_r=lru_w_r,
                   lru_b_r=lru_b_r, lru_w_i=lru_w_i, lru_b_i=lru_b_i, lru_lambda=lru_lambda, norm2_w=norm2_w, norm_f_w=norm_f_w)
    small_m = dict(norm1_w=m_norm1_w, b_branch_gate=m_b_branch_gate, ssm_conv_b=m_ssm_conv_b, ssm_dt_bias=m_ssm_dt_bias,
                   ssm_a_log=m_ssm_a_log, ssm_d=m_ssm_d, ssm_norm_w=m_ssm_norm_w, lru_conv_b=m_lru_conv_b, lru_w_r=m_lru_w_r,
                   lru_b_r=m_lru_b_r, lru_w_i=m_lru_w_i, lru_b_i=m_lru_b_i, lru_lambda=m_lru_lambda, norm2_w=m_norm2_w,
                   norm_f_w=m_norm_f_w)
    small_v = dict(norm1_w=v_norm1_w, b_branch_gate=v_b_branch_gate, ssm_conv_b=v_ssm_conv_b, ssm_dt_bias=v_ssm_dt_bias,
                   ssm_a_log=v_ssm_a_log, ssm_d=v_ssm_d, ssm_norm_w=v_ssm_norm_w, lru_conv_b=v_lru_conv_b, lru_w_r=v_lru_w_r,
                   lru_b_r=v_lru_b_r, lru_w_i=v_lru_w_i, lru_b_i=v_lru_b_i, lru_lambda=v_lru_lambda, norm2_w=v_norm2_w,
                   norm_f_w=v_norm_f_w)
    shapes = [small_w[k].shape for k in small_names]
    conv_shapes = [(4, SSM_CONV_DIM), (4, LRU_WIDTH)]
    g_pack = _pack([small_g[k] for k in small_names] + [d_ssm_cw, d_lru_cw])
    s_send, s_recv, s_arrays, s_token = split_start([g_pack, lax.empty((8,) + g_pack.shape, f32)], g_pack, all_copies(),
                                                    (7,), name="small_start")
    d_w_in_p = mm(hn1, dproj, "tn", after=s_token, name="d_w_in")

    d_w_in_s = _col_shards(_unperm_in_cols(d_w_in_p))
    (l_sib,) = pair_exchange([d_w_in_s], name="pair_exchange_late")
    l_pair = pair_add(d_w_in_s, l_sib, idx, name="pair_add_w_in")
    l_land = lax.empty((3,) + l_pair[0].shape[1:], bf16)
    l_send, l_recv, l_arrays, l_token = split_start([l_pair[0], l_land], l_pair[1], reduce_copies(1), (3,),
                                                    name="reduce_late_start")
    dhn1 = mm(dproj, w_in_p, "nt", after=l_token, name="d_hn1")
    grad_x, d_norm1 = rms_bwd(x2, norm1_w, dhn1, dh1, name="rms1_bwd")

    e_arrays = split_wait(e_send, e_recv, e_arrays, d_norm1, reduce_copies(5), name="reduce_early_wait")
    e_half = [chip_sum(p[1], rb, idx, name="chip_sum_" + k) for p, rb, k in zip(e_pairs, e_arrays[5:], late_names)]
    big_out = {}
    for k, g in zip(late_names, pair_gather(e_half, name="pair_gather_early")):
        big_out[k] = (g,) + tuple(adamw(big_w[k], g, big_m[k], big_v[k], name="adamw_" + k))

    s_arrays = split_wait(s_send, s_recv, s_arrays, d_norm1, all_copies(), name="small_wait")
    g_sum = sum8(lax.dynamic_update_index_in_dim(s_arrays[1], g_pack, 2 * me + ci, 0), name="sum8")
    n1 = d_norm1.reshape(8, 128)
    n1_sum = sum8(lax.dynamic_update_index_in_dim(all_exchange(n1, name="all_exchange_norm1"), n1, 2 * me + ci, 0),
                  name="sum8_norm1")
    g_sum = lax.dynamic_update_slice(g_sum, n1_sum, (0, 0))
    g_small = _unpack(g_sum, shapes + conv_shapes)
    g_small[-2] = lax.dynamic_slice_in_dim(g_small[-2], me * 768, 768, axis=1)
    g_small[-1] = lax.dynamic_slice_in_dim(g_small[-1], me * 320, 320, axis=1)
    all_names = small_names + ["ssm_conv_w", "lru_conv_w"]
    small_w.update(ssm_conv_w=ssm_conv_w[0], lru_conv_w=lru_conv_w[0])
    small_m.update(ssm_conv_w=m_ssm_conv_w[0], lru_conv_w=m_lru_conv_w[0])
    small_v.update(ssm_conv_w=v_ssm_conv_w[0], lru_conv_w=v_lru_conv_w[0])
    as2d = lambda a: a.reshape(-1, a.shape[-1])
    upd = adamw_many([as2d(small_w[k]) for k in all_names], [as2d(g) for g in g_small],
                     [as2d(small_m[k]) for k in all_names], [as2d(small_v[k]) for k in all_names], name="adamw_small")
    small_out = {}
    for k, g, u in zip(all_names, g_small, upd):
        small_out[k] = (g,) + tuple(o.reshape(g.shape) for o in u)
    l_arrays = split_wait(l_send, l_recv, l_arrays, upd[0][0], reduce_copies(1), name="reduce_late_wait")
    l_half = chip_sum(l_pair[1], l_arrays[1], idx, name="chip_sum_w_in")
    (g_w_in,) = pair_gather([l_half], name="pair_gather_late")
    big_out["w_in"] = (g_w_in,) + tuple(adamw(big_w["w_in"], g_w_in, big_m["w_in"], big_v["w_in"], name="adamw_w_in"))

    order = ["norm1_w", "w_in", "b_branch_gate", "ssm_conv_w", "ssm_conv_b", "ssm_dt_bias", "ssm_a_log", "ssm_d", "ssm_norm_w",
             "w_out_ssm", "lru_conv_w", "lru_conv_b", "lru_w_r", "lru_b_r", "lru_w_i", "lru_b_i", "lru_lambda", "w_out_lru",
             "w_out", "norm2_w", "w_ffn_in", "w_ffn_out", "norm_f_w"]
    outs = [loss, grad_x[None]]
    for which in range(4):
        for k in order:
            if k in big_out:
                outs.append(big_out[k][which][None])
            elif k in ("ssm_conv_w", "lru_conv_w"):
                outs.append(small_out[k][which][None])
            else:
                outs.append(small_out[k][which])
    return tuple(outs)
```

```python
import functools
import math

import jax
import jax.numpy as jnp
import numpy as np
from jax import lax
from jax.experimental import pallas as pl
from jax.experimental.pallas import tpu as pltpu

f32 = jnp.float32
bf16 = jnp.bfloat16

D_MODEL = 1024
SSM_D_INNER = 2048
SSM_HEADS = 32
SSM_HEAD_DIM = 64
SSM_GROUPS = 4
SSM_HPG = 8
SSM_D_STATE = 128
SSM_CHUNK = 128
SSM_GROUP_W = 512
SSM_CONV_DIM = 3072
XBC_GROUP_W = 768
LRU_WIDTH = 1280
LRU_BLOCKS = 10
LRU_BLOCK = 128
LRU_C = 8.0
FFN_HIDDEN = 2816
RMS_EPS = 1e-6
IN_PROJ_DIM = 9760
N_CHIPS = 4

OFF_GATES = 0
OFF_Z = 2048
OFF_LX = 4096
OFF_LY = 5376
OFF_DT = 6656
DT_PAD_W = 256
OFF_XBC = 6912
PROJ_W = 9984

ADAM_LR = 0.001
ADAM_B1 = 0.9
ADAM_B2 = 0.999
ADAM_EPS = 1e-08
ADAM_WD = 0.01
ADAM_STEP = 10

MESH = pl.DeviceIdType.MESH
ANY = pl.BlockSpec(memory_space=pl.ANY)

NN = (((1,), (0,)), ((), ()))
NT = (((1,), (1,)), ((), ()))
TN = (((0,), (0,)), ((), ()))


def _pick(n, cap, mult=128):
    best = None
    for t in range(mult, min(n, cap) + 1, mult):
        if n % t == 0:
            best = t
    return best if best is not None else n


def _sigmoid(x):
    return 1.0 / (1.0 + jnp.exp(-x))


def _softplus(x):
    return jnp.maximum(x, 0.0) + jnp.log(1.0 + jnp.exp(-jnp.abs(x)))


def _silu(x):
    return x * _sigmoid(x)


def _dsilu(x):
    s = _sigmoid(x)
    return s * (1.0 + x * (1.0 - s))


_GELU_K = math.sqrt(2.0 / math.pi)


def _gelu(x):
    return 0.5 * x * (1.0 + jnp.tanh(_GELU_K * (x + 0.044715 * x * x * x)))


def _dgelu(x):
    t = jnp.tanh(_GELU_K * (x + 0.044715 * x * x * x))
    return 0.5 * (1.0 + t) + 0.5 * x * (1.0 - t * t) * _GELU_K * (1.0 + 3.0 * 0.044715 * x * x)


def _expm1(x):
    poly = x * (1.0 + x * (0.5 + x * (1.0 / 6.0 + x * (1.0 / 24.0 + x * (1.0 / 120.0 + x * (1.0 / 720.0))))))
    return jnp.where(jnp.abs(x) < 0.1, poly, jnp.exp(x) - 1.0)


def _dot(a, b, dn):
    return lax.dot_general(a.astype(bf16), b.astype(bf16), dn, preferred_element_type=f32)


def _dot_01(a, b, dn, split, terms):
    r = a if split == 0 else b
    out = None
    for _ in range(terms):
        h = r.astype(bf16)
        r = r - h.astype(f32)
        d = lax.dot_general(h if split == 0 else a.astype(bf16), b.astype(bf16) if split == 0 else h, dn,
                            preferred_element_type=f32)
        out = d if out is None else out + d
    return out


MM_WHOLE_K = 2816
MM_VMEM_BUDGET = 40 * 2 ** 20

def mm(a, b, mode, *, name, add=None, after=None, out_dtype=f32, b_shards=False, out_shards=0):
    bs = b.shape[1:] if b_shards else b.shape
    shard_w = b.shape[2] if b_shards else None
    bcols = bs[1] * (b.shape[0] if b_shards else 1)
    if mode == "nn":
        (m, k), (k2, n) = a.shape, (bs[0], bcols)
    elif mode == "nt":
        (m, k), (n, k2) = a.shape, (bs[0], bcols)
    else:
        (k, m), (k2, n) = a.shape, b.shape
    assert k == k2, (a.shape, b.shape, mode)
    tk = k if k <= MM_WHOLE_K else _pick(k, 1024 if mode == "tn" else 2048)
    tn = _pick(n, 1536)
    if b_shards and mode == "nn":
        tn = shard_w
    if b_shards and mode == "nt":
        tk = shard_w
    if out_shards:
        tn = n // out_shards
    isz = lambda v: jnp.dtype(v.dtype).itemsize
    for cap in (1536, 1024, 512, 256):
        tm = _pick(m, cap)
        vmem = 2 * (tm * tk * isz(a) + tk * tn * isz(b) + tm * tn * (4 * int(add is not None) + jnp.dtype(out_dtype).itemsize))
        vmem += 4 * tm * tn * int(k > tk)
        if vmem <= MM_VMEM_BUDGET:
            break
    nk = k // tk
    dn = {"nn": NN, "nt": NT, "tn": TN}[mode]
    a_spec = pl.BlockSpec((tk, tm), lambda i, j, kk: (kk, i)) if mode == "tn" else pl.BlockSpec((tm, tk), lambda i, j, kk: (i, kk))
    b_spec = pl.BlockSpec((tn, tk), lambda i, j, kk: (j, kk)) if mode == "nt" else pl.BlockSpec((tk, tn), lambda i, j, kk: (kk, j))
    if b_shards:
        b_spec = (pl.BlockSpec((None, tn, tk), lambda i, j, kk: (kk, j, 0)) if mode == "nt"
                  else pl.BlockSpec((None, tk, tn), lambda i, j, kk: (j, kk, 0)))
    o_spec = pl.BlockSpec((tm, tn), lambda i, j, kk: (i, j))
    out_shape = jax.ShapeDtypeStruct((m, n), out_dtype)
    if out_shards:
        assert add is None
        o_spec = pl.BlockSpec((None, tm, tn), lambda i, j, kk: (j, i, 0))
        out_shape = jax.ShapeDtypeStruct((out_shards, m, tn), out_dtype)
    has_add = add is not None

    n_extra = int(has_add) + int(after is not None)

    def body(a_ref, b_ref, *rest):
        add_ref = rest[0] if has_add else None
        o_ref = rest[n_extra]

        def finish(r):
            if has_add:
                r = r + add_ref[...]
            o_ref[...] = r.astype(out_dtype)

        if nk == 1:
            finish(_dot(a_ref[...], b_ref[...], dn))
            return
        acc = rest[-1]
        kk = pl.program_id(2)

        @pl.when(kk == 0)
        def _():
            acc[...] = jnp.zeros_like(acc)

        acc[...] += _dot(a_ref[...], b_ref[...], dn)

        @pl.when(kk == nk - 1)
        def _():
            finish(acc[...])

    ins = [a, b] + ([add] if has_add else []) + ([after] if after is not None else [])
    in_specs = [a_spec, b_spec] + ([o_spec] if has_add else []) + ([ANY] if after is not None else [])
    return pl.pallas_call(
        body, name=name, grid=(m // tm, n // tn, nk), in_specs=in_specs, out_specs=o_spec, out_shape=out_shape,
        scratch_shapes=[pltpu.VMEM((tm, tn), f32)] if nk > 1 else [],
        compiler_params=pltpu.CompilerParams(dimension_semantics=("parallel", "parallel", "arbitrary")),
    )(*ins)


def rms_fwd(x, w, *, name):
    t, d = x.shape
    tr = _pick(t, 256, 8)

    def body(x_ref, w_ref, o_ref):
        xv = x_ref[...]
        r = lax.rsqrt(jnp.mean(xv * xv, axis=-1, keepdims=True) + RMS_EPS)
        o_ref[...] = (xv * r * w_ref[...]).astype(bf16)

    return pl.pallas_call(
        body, name=name, grid=(t // tr,),
        in_specs=[pl.BlockSpec((tr, d), lambda i: (i, 0)), pl.BlockSpec((1, d), lambda i: (0, 0))],
        out_specs=pl.BlockSpec((tr, d), lambda i: (i, 0)), out_shape=jax.ShapeDtypeStruct((t, d), bf16),
    )(x, w)


def _rms_bwd_math(xv, wv, dy):
    r = lax.rsqrt(jnp.mean(xv * xv, axis=-1, keepdims=True) + RMS_EPS)
    g = dy * wv
    dx = r * g - xv * (r * r * r) * jnp.mean(g * xv, axis=-1, keepdims=True)
    dw = jnp.sum(dy * xv * r, axis=0, keepdims=True)
    return dx, dw


def rms_bwd(x, w, dy, res, *, name):
    t, d = x.shape
    tr = _pick(t, 256, 8)

    def body(x_ref, w_ref, dy_ref, res_ref, dx_ref, dw_ref):
        dx, dw = _rms_bwd_math(x_ref[...], w_ref[...], dy_ref[...])
        dx_ref[...] = dx + res_ref[...]

        @pl.when(pl.program_id(0) == 0)
        def _():
            dw_ref[...] = jnp.zeros_like(dw_ref)

        dw_ref[...] += dw

    row = pl.BlockSpec((tr, d), lambda i: (i, 0))
    vec = pl.BlockSpec((1, d), lambda i: (0, 0))
    return pl.pallas_call(
        body, name=name, grid=(t // tr,), in_specs=[row, vec, row, row], out_specs=[row, vec],
        out_shape=[jax.ShapeDtypeStruct((t, d), f32), jax.ShapeDtypeStruct((1, d), f32)],
        compiler_params=pltpu.CompilerParams(dimension_semantics=("arbitrary",)),
    )(x, w, dy, res)


def loss_head(h, w, target, *, name):
    t, d = h.shape
    tr = _pick(t, 256, 8)

    def body(h_ref, w_ref, t_ref, loss_ref, dh_ref, dw_ref):
        xv, wv = h_ref[...], w_ref[...]
        r = lax.rsqrt(jnp.mean(xv * xv, axis=-1, keepdims=True) + RMS_EPS)
        err = xv * r * wv - t_ref[...]
        part = 0.5 * jnp.sum(jnp.mean(err * err, axis=-1, keepdims=True), axis=0, keepdims=True)
        dx, dw = _rms_bwd_math(xv, wv, err * (1.0 / d))
        dh_ref[...] = dx

        @pl.when(pl.program_id(0) == 0)
        def _():
            dw_ref[...] = jnp.zeros_like(dw_ref)
            loss_ref[...] = jnp.zeros_like(loss_ref)

        dw_ref[...] += dw
        loss_ref[...] += part

    row = pl.BlockSpec((tr, d), lambda i: (i, 0))
    vec = pl.BlockSpec((1, d), lambda i: (0, 0))
    return pl.pallas_call(
        body, name=name, grid=(t // tr,), in_specs=[row, vec, row],
        out_specs=[pl.BlockSpec((8, 128), lambda i: (0, 0)), row, vec],
        out_shape=[jax.ShapeDtypeStruct((8, 128), f32), jax.ShapeDtypeStruct((t, d), f32), jax.ShapeDtypeStruct((1, d), f32)],
        compiler_params=pltpu.CompilerParams(dimension_semantics=("arbitrary",)),
    )(h, w, target)


CONV_ROWS = 512


def conv_fwd(src, col0, width, w, b, *, silu, name):
    t = src.shape[0]
    tc = _pick(math.gcd(width, col0), 768)
    assert col0 % tc == 0
    cb = col0 // tc
    r = CONV_ROWS

    def body(u_ref, w_ref, b_ref, *rest):
        ext = rest[-1]
        j = pl.program_id(1)

        @pl.when(j == 0)
        def _():
            ext[0:8, :] = jnp.zeros((8, tc), f32)

        @pl.when(j > 0)
        def _():
            ext[0:8, :] = ext[r:r + 8, :]

        ext[8:r + 8, :] = u_ref[...]
        v = ext[...]
        wv = w_ref[...]
        acc = b_ref[...] + wv[3:4, :] * v
        for s in (1, 2, 3):
            acc = acc + wv[3 - s:4 - s, :] * pltpu.roll(v, s, 0)
        pre = acc[8:, :]
        rest[0][...] = pre
        if silu:
            rest[1][...] = _silu(pre)

    tile = pl.BlockSpec((r, tc), lambda c, j: (j, c))
    n_out = 2 if silu else 1
    return pl.pallas_call(
        body, name=name, grid=(width // tc, t // r),
        in_specs=[pl.BlockSpec((r, tc), lambda c, j: (j, cb + c)), pl.BlockSpec((4, tc), lambda c, j: (0, c)),
                  pl.BlockSpec((1, tc), lambda c, j: (0, c))],
        out_specs=[tile] * n_out, out_shape=[jax.ShapeDtypeStruct((t, width), f32)] * n_out,
        scratch_shapes=[pltpu.VMEM((r + 8, tc), f32)],
        compiler_params=pltpu.CompilerParams(dimension_semantics=("parallel", "arbitrary")),
    )(src, w, b)


def conv_bwd(dpost, pre, src, col0, w, dst, *, name):
    t, width = dpost.shape
    tc = _pick(math.gcd(width, col0), 768)
    assert col0 % tc == 0
    cb = col0 // tc
    r = CONV_ROWS
    nt = t // r
    has_pre = pre is not None

    def body(*refs):
        refs = refs[1:]
        if has_pre:
            d_ref, p_ref, u_ref, w_ref, du_ref, dw_ref, db_ref, ext = refs
        else:
            d_ref, u_ref, w_ref, du_ref, dw_ref, db_ref, ext = refs
        j = pl.program_id(1)

        @pl.when(j == 0)
        def _():
            ext[r:r + 8, :] = jnp.zeros((8, tc), f32)
            dw_ref[...] = jnp.zeros_like(dw_ref)
            db_ref[...] = jnp.zeros_like(db_ref)

        @pl.when(j > 0)
        def _():
            ext[r:r + 8, :] = ext[0:8, :]

        dpre = d_ref[...]
        if has_pre:
            dpre = dpre * _dsilu(p_ref[...])
        ext[0:r, :] = dpre
        v = ext[...]
        wv = w_ref[...]
        uv = u_ref[...]
        du = wv[3:4, :] * dpre
        dw_ref[3:4, :] += jnp.sum(dpre * uv, axis=0, keepdims=True)
        for s in (1, 2, 3):
            sh = pltpu.roll(v, r + 8 - s, 0)[0:r, :]
            du = du + wv[3 - s:4 - s, :] * sh
            dw_ref[3 - s:4 - s, :] += jnp.sum(sh * uv, axis=0, keepdims=True)
        db_ref[...] += jnp.sum(dpre, axis=0, keepdims=True)
        du_ref[...] = du.astype(bf16)

    rev = pl.BlockSpec((r, tc), lambda c, j: (nt - 1 - j, c))
    win = pl.BlockSpec((r, tc), lambda c, j: (nt - 1 - j, cb + c))
    in_specs = [ANY, rev] + ([rev] if has_pre else []) + [win, pl.BlockSpec((4, tc), lambda c, j: (0, c))]
    ins = [dst, dpost] + ([pre] if has_pre else []) + [src, w]
    return pl.pallas_call(
        body, name=name, grid=(width // tc, nt), in_specs=in_specs,
        out_specs=[win, pl.BlockSpec((4, tc), lambda c, j: (0, c)), pl.BlockSpec((1, tc), lambda c, j: (0, c))],
        out_shape=[jax.ShapeDtypeStruct(dst.shape, bf16), jax.ShapeDtypeStruct((4, width), f32),
                   jax.ShapeDtypeStruct((1, width), f32)],
        input_output_aliases={0: 0},
        scratch_shapes=[pltpu.VMEM((r + 8, tc), f32)],
        compiler_params=pltpu.CompilerParams(dimension_semantics=("parallel", "arbitrary")),
    )(*ins)


def _ssd_common(xbc_ref, dtr_ref, dtrT_ref, par_row_ref, par_col_ref):
    l = SSM_CHUNK
    x = xbc_ref[:, 0:SSM_GROUP_W]
    bm = xbc_ref[:, SSM_GROUP_W:SSM_GROUP_W + SSM_D_STATE]
    cm = xbc_ref[:, SSM_GROUP_W + SSM_D_STATE:XBC_GROUP_W]
    par_row = par_row_ref[0]
    par_col = par_col_ref[0]
    bias_row, alog_row, d_row = par_row[0:1, :], par_row[1:2, :], par_row[2:3, :]
    bias_col, alog_col = par_col[:, 0:1], par_col[:, 1:2]
    dtr = dtr_ref[0]
    dt = _softplus(dtr + bias_row)
    dt_t = _softplus(dtrT_ref[0] + bias_col)
    a_row = -jnp.exp(alog_row)
    a_col = -jnp.exp(alog_col)
    li = lax.broadcasted_iota(jnp.int32, (l, l), 0)
    si = lax.broadcasted_iota(jnp.int32, (l, l), 1)
    tri = (li >= si).astype(f32)
    cs = _dot_01(tri, dt * a_row, NN, 1, 3)
    cs_t = _dot_01(dt_t * a_col, tri, NT, 0, 3)
    off = lax.broadcasted_iota(jnp.int32, (SSM_HPG, SSM_GROUP_W), 1) - SSM_HEAD_DIM * lax.broadcasted_iota(
        jnp.int32, (SSM_HPG, SSM_GROUP_W), 0)
    ex = ((off >= 0) & (off < SSM_HEAD_DIM)).astype(f32)
    cs_x = _dot_01(cs, ex, NN, 0, 3)
    cl_x = cs_x[l - 1:l, :]
    return dict(x=x, bm=bm, cm=cm, dtr=dtr, dt=dt, a_row=a_row, bias_row=bias_row, tri=tri, li=li, si=si, cs=cs,
                cs_t=cs_t, ex=ex, dt_x=_dot_01(dt, ex, NN, 0, 2), d_x=_dot_01(par_row, ex, NN, 0, 2)[2:3, :], e_x=jnp.exp(cs_x),
                el_x=jnp.exp(cl_x), dec_x=jnp.exp(cl_x - cs_x))


def ssd_fwd(xbc, dtr, dtr_t, par_row, par_col, *, name):
    t = xbc.shape[0]
    nc = t // SSM_CHUNK
    l, p = SSM_CHUNK, SSM_HEAD_DIM

    def body(xbc_ref, dtr_ref, dtrT_ref, prow_ref, pcol_ref, y_ref, sin_ref, state):
        @pl.when(pl.program_id(1) == 0)
        def _():
            state[...] = jnp.zeros_like(state)

        q = _ssd_common(xbc_ref, dtr_ref, dtrT_ref, prow_ref, pcol_ref)
        st = state[...]
        sin_ref[0] = st
        xd = q["x"] * q["dt_x"]
        g = _dot(q["cm"], q["bm"], NT)
        for r in range(SSM_HPG):
            sl = slice(r * p, (r + 1) * p)
            diff = q["cs"][:, r:r + 1] - q["cs_t"][r:r + 1, :]
            lm = jnp.where(q["li"] >= q["si"], jnp.exp(jnp.minimum(diff, 0.0)), 0.0)
            y_ref[:, sl] = _dot(g * lm, xd[:, sl], NN)
        y_ref[...] += q["e_x"] * _dot(q["cm"], st, NN) + q["d_x"] * q["x"]
        state[...] = q["el_x"] * st + _dot(q["bm"].T, xd * q["dec_x"], NN)

    return pl.pallas_call(
        body, name=name, grid=(SSM_GROUPS, nc),
        in_specs=[pl.BlockSpec((l, XBC_GROUP_W), lambda g, c: (c, g)),
                  pl.BlockSpec((1, l, SSM_HPG), lambda g, c: (g, c, 0)),
                  pl.BlockSpec((1, SSM_HPG, l), lambda g, c: (g, 0, c)),
                  pl.BlockSpec((1, 8, 8), lambda g, c: (g, 0, 0)),
                  pl.BlockSpec((1, 8, 8), lambda g, c: (g, 0, 0))],
        out_specs=[pl.BlockSpec((l, SSM_GROUP_W), lambda g, c: (c, g)),
                   pl.BlockSpec((1, SSM_D_STATE, SSM_GROUP_W), lambda g, c: (c, 0, g))],
        out_shape=[jax.ShapeDtypeStruct((t, SSM_D_INNER), f32),
                   jax.ShapeDtypeStruct((nc, SSM_D_STATE, SSM_D_INNER), f32)],
        scratch_shapes=[pltpu.VMEM((SSM_D_STATE, SSM_GROUP_W), f32)],
        compiler_params=pltpu.CompilerParams(dimension_semantics=("parallel", "arbitrary")),
    )(xbc, dtr, dtr_t, par_row, par_col)


def ssd_bwd(xbc, dtr, dtr_t, par_row, par_col, s_in, dy, *, name):
    t = xbc.shape[0]
    nc = t // SSM_CHUNK
    l, p = SSM_CHUNK, SSM_HEAD_DIM

    def body(xbc_ref, dtr_ref, dtrT_ref, prow_ref, pcol_ref, sin_ref, dy_ref, dxbc_ref, ddtr_ref, dpar_ref,
             dstate, yd_buf, dxd_buf):
        @pl.when(pl.program_id(1) == 0)
        def _():
            dstate[...] = jnp.zeros_like(dstate)
            dpar_ref[...] = jnp.zeros_like(dpar_ref)

        q = _ssd_common(xbc_ref, dtr_ref, dtrT_ref, prow_ref, pcol_ref)
        x, bm, cm, ex, li, si = q["x"], q["bm"], q["cm"], q["ex"], q["li"], q["si"]
        e_x, el_x, dec_x = q["e_x"], q["el_x"], q["dec_x"]
        st = sin_ref[0]
        dst = dstate[...]
        dy = dy_ref[...]
        xd = x * q["dt_x"]
        g = _dot(cm, bm, NT)
        dg = jnp.zeros((l, l), f32)
        for r in range(SSM_HPG):
            sl = slice(r * p, (r + 1) * p)
            diff = q["cs"][:, r:r + 1] - q["cs_t"][r:r + 1, :]
            lm = jnp.where(li >= si, jnp.exp(jnp.minimum(diff, 0.0)), 0.0)
            m = (g * lm).astype(bf16)
            xdh, dyh = xd[:, sl].astype(bf16), dy[:, sl].astype(bf16)
            yd_buf[:, sl] = _dot(m, xdh, NN)
            dxd_buf[:, sl] = _dot(m, dyh, TN)
            dg = dg + _dot(dyh, xdh, NT) * lm
        yd, dxd_diag = yd_buf[...], dxd_buf[...]
        yo = e_x * _dot(cm, st, NN)
        dz = e_x * dy
        wv = _dot(bm, dst, NN)
        xw = xd * wv * dec_x
        row8 = lax.broadcasted_iota(jnp.int32, (l, SSM_HPG), 0)
        dy_b, xd_b = dy.astype(bf16).astype(f32), xd.astype(bf16).astype(f32)
        dcs = _dot_01(dy_b * yd - xd_b * dxd_diag + dy * yo - xw, ex, NT, 0, 3)
        tail = jnp.sum(xw, axis=0, keepdims=True) + el_x * jnp.sum(dst * st, axis=0, keepdims=True)
        dcl = _dot_01(jnp.broadcast_to(tail, (SSM_HPG, SSM_GROUP_W)), ex, NT, 0, 3)[0:1, :]
        dcs = dcs + jnp.where(row8 == l - 1, dcl, 0.0)
        dda = _dot_01(q["tri"], dcs, TN, 1, 3)
        dxd = dxd_diag + dec_x * wv
        ddt = _dot_01(dxd * x, ex, NT, 0, 3) + dda * q["a_row"]
        ddtr = ddt * _sigmoid(q["dtr"] + q["bias_row"])
        ddtr_ref[0] = ddtr
        dd = _dot_01(jnp.broadcast_to(jnp.sum(dy * x, axis=0, keepdims=True), (SSM_HPG, SSM_GROUP_W)), ex, NT, 0, 2)[0:1, :]
        dpar_ref[0, 0:1, :] += jnp.sum(ddtr, axis=0, keepdims=True)
        dpar_ref[0, 1:2, :] += jnp.sum(dda * q["dt"], axis=0, keepdims=True) * q["a_row"]
        dpar_ref[0, 2:3, :] += dd
        dxbc_ref[:, 0:SSM_GROUP_W] = dxd * q["dt_x"] + q["d_x"] * dy
        dxbc_ref[:, SSM_GROUP_W:SSM_GROUP_W + SSM_D_STATE] = _dot(dg, cm, TN) + _dot(xd * dec_x, dst, NT)
        dxbc_ref[:, SSM_GROUP_W + SSM_D_STATE:XBC_GROUP_W] = _dot(dg, bm, NN) + _dot(dz, st, NT)
        dstate[...] = _dot(cm.T, dz, NN) + el_x * dst

    rc = lambda c: nc - 1 - c
    return pl.pallas_call(
        body, name=name, grid=(SSM_GROUPS, nc),
        in_specs=[pl.BlockSpec((l, XBC_GROUP_W), lambda g, c: (rc(c), g)),
                  pl.BlockSpec((1, l, SSM_HPG), lambda g, c: (g, rc(c), 0)),
                  pl.BlockSpec((1, SSM_HPG, l), lambda g, c: (g, 0, rc(c))),
                  pl.BlockSpec((1, 8, 8), lambda g, c: (g, 0, 0)),
                  pl.BlockSpec((1, 8, 8), lambda g, c: (g, 0, 0)),
                  pl.BlockSpec((1, SSM_D_STATE, SSM_GROUP_W), lambda g, c: (rc(c), 0, g)),
                  pl.BlockSpec((l, SSM_GROUP_W), lambda g, c: (rc(c), g))],
        out_specs=[pl.BlockSpec((l, XBC_GROUP_W), lambda g, c: (rc(c), g)),
                   pl.BlockSpec((1, l, SSM_HPG), lambda g, c: (g, rc(c), 0)),
                   pl.BlockSpec((1, 8, 8), lambda g, c: (g, 0, 0))],
        out_shape=[jax.ShapeDtypeStruct((t, SSM_CONV_DIM), f32),
                   jax.ShapeDtypeStruct((SSM_GROUPS, t, SSM_HPG), f32),
                   jax.ShapeDtypeStruct((SSM_GROUPS, 8, 8), f32)],
        scratch_shapes=[pltpu.VMEM((SSM_D_STATE, SSM_GROUP_W), f32), pltpu.VMEM((l, SSM_GROUP_W), f32),
                        pltpu.VMEM((l, SSM_GROUP_W), f32)],
        compiler_params=pltpu.CompilerParams(dimension_semantics=("parallel", "arbitrary")),
    )(xbc, dtr, dtr_t, par_row, par_col, s_in, dy)


def gnorm_fwd(y, proj, w, *, name):
    t = y.shape[0]
    tr = _pick(t, 512, 8)
    gw = SSM_GROUP_W
    zb = OFF_Z // gw

    def body(y_ref, z_ref, w_ref, o_ref):
        y2 = y_ref[...] * _silu(z_ref[...])
        r = lax.rsqrt(jnp.mean(y2 * y2, axis=-1, keepdims=True) + RMS_EPS)
        o_ref[...] = (y2 * r * w_ref[...]).astype(bf16)

    return pl.pallas_call(
        body, name=name, grid=(SSM_GROUPS, t // tr),
        in_specs=[pl.BlockSpec((tr, gw), lambda g, i: (i, g)), pl.BlockSpec((tr, gw), lambda g, i: (i, zb + g)),
                  pl.BlockSpec((1, gw), lambda g, i: (0, g))],
        out_specs=pl.BlockSpec((tr, gw), lambda g, i: (i, g)), out_shape=jax.ShapeDtypeStruct((t, SSM_D_INNER), bf16),
    )(y, proj, w)


def gnorm_bwd(y, proj, w, dout, dst, *, name):
    t = y.shape[0]
    tr = _pick(t, 512, 8)
    gw = SSM_GROUP_W
    zb = OFF_Z // gw

    def body(_, y_ref, z_ref, w_ref, do_ref, dy_ref, dz_ref, dw_ref):
        yv, zv = y_ref[...], z_ref[...]
        sz = _silu(zv)
        y2 = yv * sz
        dy2, dw = _rms_bwd_math(y2, w_ref[...], do_ref[...].astype(f32))
        dy_ref[...] = dy2 * sz
        dz_ref[...] = (dy2 * yv * _dsilu(zv)).astype(bf16)

        @pl.when(pl.program_id(1) == 0)
        def _():
            dw_ref[...] = jnp.zeros_like(dw_ref)

        dw_ref[...] += dw

    tile = pl.BlockSpec((tr, gw), lambda g, i: (i, g))
    vec = pl.BlockSpec((1, gw), lambda g, i: (0, g))
    return pl.pallas_call(
        body, name=name, grid=(SSM_GROUPS, t // tr),
        in_specs=[ANY, tile, pl.BlockSpec((tr, gw), lambda g, i: (i, zb + g)), vec, tile],
        out_specs=[tile, pl.BlockSpec((tr, gw), lambda g, i: (i, zb + g)), vec],
        out_shape=[jax.ShapeDtypeStruct((t, SSM_D_INNER), f32), jax.ShapeDtypeStruct(dst.shape, bf16),
                   jax.ShapeDtypeStruct((1, SSM_D_INNER), f32)],
        input_output_aliases={0: 1},
        compiler_params=pltpu.CompilerParams(dimension_semantics=("parallel", "arbitrary")),
    )(dst, y, proj, w, dout)


LRU_ROWS = 256


def _lru_gates(uv, wr_ref, wi_ref, br_ref, bi_ref, lam_ref):
    rg = _sigmoid(_dot(uv, wr_ref[0], NN) + br_ref[...])
    ig = _sigmoid(_dot(uv, wi_ref[0], NN) + bi_ref[...])
    sp = _softplus(-lam_ref[...])
    la = -LRU_C * rg * sp
    a = jnp.exp(la)
    s = jnp.sqrt(jnp.maximum(-_expm1(2.0 * la), 0.0))
    return rg, ig, sp, la, a, s


def lru_fwd(u, proj, w_r, b_r, w_i, b_i, lam, *, name):
    t = u.shape[0]
    r = LRU_ROWS
    lb = LRU_BLOCK
    yb = OFF_LY // lb

    def body(u_ref, y_ref, wr_ref, br_ref, wi_ref, bi_ref, lam_ref, h_ref, o_ref, carry):
        @pl.when(pl.program_id(1) == 0)
        def _():
            carry[...] = jnp.zeros_like(carry)

        uv = u_ref[...]
        _, ig, _, _, a, s = _lru_gates(uv, wr_ref, wi_ref, br_ref, bi_ref, lam_ref)
        b = s * ig * uv
        row = lax.broadcasted_iota(jnp.int32, (r, lb), 0)
        d = 1
        while d < r:
            keep = row >= d
            b = b + a * jnp.where(keep, pltpu.roll(b, d, 0), 0.0)
            a = a * jnp.where(keep, pltpu.roll(a, d, 0), 1.0)
            d *= 2
        h = b + a * carry[0:1, :]
        carry[0:1, :] = h[r - 1:r, :]
        h_ref[...] = h
        o_ref[...] = (h * _gelu(y_ref[...])).astype(bf16)

    tile = pl.BlockSpec((r, lb), lambda hb, j: (j, hb))
    vec = pl.BlockSpec((1, lb), lambda hb, j: (0, hb))
    wsp = pl.BlockSpec((1, lb, lb), lambda hb, j: (hb, 0, 0))
    return pl.pallas_call(
        body, name=name, grid=(LRU_BLOCKS, t // r),
        in_specs=[tile, pl.BlockSpec((r, lb), lambda hb, j: (j, yb + hb)), wsp, vec, wsp, vec, vec],
        out_specs=[tile, tile],
        out_shape=[jax.ShapeDtypeStruct((t, LRU_WIDTH), f32), jax.ShapeDtypeStruct((t, LRU_WIDTH), bf16)],
        scratch_shapes=[pltpu.VMEM((8, lb), f32)],
        compiler_params=pltpu.CompilerParams(dimension_semantics=("parallel", "arbitrary")),
    )(u, proj, w_r, b_r, w_i, b_i, lam)


def lru_bwd(u, proj, hseq, dout, w_r, b_r, w_i, b_i, lam, dst, *, name):
    t = u.shape[0]
    r = LRU_ROWS
    nt = t // r
    lb = LRU_BLOCK
    yb = OFF_LY // lb

    def body(_, u_ref, y_ref, h_ref, hp_ref, do_ref, wr_ref, br_ref, wi_ref, bi_ref, lam_ref,
             du_ref, dy_ref, dwr_ref, dwi_ref, dbr_ref, dbi_ref, dlam_ref, carry_dh, carry_a):
        j = pl.program_id(1)

        @pl.when(j == 0)
        def _():
            carry_dh[...] = jnp.zeros_like(carry_dh)
            carry_a[...] = jnp.zeros_like(carry_a)
            dwr_ref[...] = jnp.zeros_like(dwr_ref)
            dwi_ref[...] = jnp.zeros_like(dwi_ref)
            dbr_ref[...] = jnp.zeros_like(dbr_ref)
            dbi_ref[...] = jnp.zeros_like(dbi_ref)
            dlam_ref[...] = jnp.zeros_like(dlam_ref)

        uv = u_ref[...]
        yv = y_ref[...]
        hv = h_ref[...]
        dov = do_ref[...]
        rg, ig, sp, la, a, s = _lru_gates(uv, wr_ref, wi_ref, br_ref, bi_ref, lam_ref)
        dy_ref[...] = (dov * hv * _dgelu(yv)).astype(bf16)
        gq = dov * _gelu(yv)
        row = lax.broadcasted_iota(jnp.int32, (r, lb), 0)
        an = jnp.where(row < r - 1, pltpu.roll(a, r - 1, 0), carry_a[0:1, :])
        d = 1
        while d < r:
            keep = row < r - d
            gq = gq + an * jnp.where(keep, pltpu.roll(gq, r - d, 0), 0.0)
            an = an * jnp.where(keep, pltpu.roll(an, r - d, 0), 1.0)
            d *= 2
        dh = gq + an * carry_dh[0:1, :]
        carry_dh[0:1, :] = dh[0:1, :]
        carry_a[0:1, :] = a[0:1, :]
        first = jnp.where(j == nt - 1, 0.0, 1.0) * hp_ref[7:8, :]
        hprev = jnp.where(row >= 1, pltpu.roll(hv, 1, 0), first)
        da = dh * hprev
        iu = ig * uv
        e2 = jnp.exp(2.0 * la)
        dla = da * a - dh * iu * e2 / jnp.maximum(s, 1e-30)
        drp = dla * (-LRU_C * sp) * rg * (1.0 - rg)
        dip = dh * s * uv * ig * (1.0 - ig)
        dlam_ref[...] += jnp.sum(dla * (LRU_C * rg) * _sigmoid(-lam_ref[...]), axis=0, keepdims=True)
        du_ref[...] = dh * s * ig + _dot(drp, wr_ref[0], NT) + _dot(dip, wi_ref[0], NT)
        dwr_ref[0] += _dot(uv, drp, TN)
        dwi_ref[0] += _dot(uv, dip, TN)
        dbr_ref[...] += jnp.sum(drp, axis=0, keepdims=True)
        dbi_ref[...] += jnp.sum(dip, axis=0, keepdims=True)

    rj = lambda j: nt - 1 - j
    tile = pl.BlockSpec((r, lb), lambda hb, j: (rj(j), hb))
    vec = pl.BlockSpec((1, lb), lambda hb, j: (0, hb))
    wsp = pl.BlockSpec((1, lb, lb), lambda hb, j: (hb, 0, 0))
    hprev_spec = pl.BlockSpec((8, lb), lambda hb, j: (jnp.maximum(rj(j) * (r // 8) - 1, 0), hb))
    ywin = pl.BlockSpec((r, lb), lambda hb, j: (rj(j), yb + hb))
    return pl.pallas_call(
        body, name=name, grid=(LRU_BLOCKS, nt),
        in_specs=[ANY, tile, ywin, tile, hprev_spec, tile, wsp, vec, wsp, vec, vec],
        out_specs=[tile, ywin, wsp, wsp, vec, vec, vec],
        out_shape=[jax.ShapeDtypeStruct((t, LRU_WIDTH), f32), jax.ShapeDtypeStruct(dst.shape, bf16),
                   jax.ShapeDtypeStruct((LRU_BLOCKS, lb, lb), f32), jax.ShapeDtypeStruct((LRU_BLOCKS, lb, lb), f32),
                   jax.ShapeDtypeStruct((1, LRU_WIDTH), f32), jax.ShapeDtypeStruct((1, LRU_WIDTH), f32),
                   jax.ShapeDtypeStruct((1, LRU_WIDTH), f32)],
        input_output_aliases={0: 1},
        scratch_shapes=[pltpu.VMEM((8, lb), f32), pltpu.VMEM((8, lb), f32)],
        compiler_params=pltpu.CompilerParams(dimension_semantics=("parallel", "arbitrary")),
    )(dst, u, proj, hseq, hseq, dout, w_r, b_r, w_i, b_i, lam)


def merge_fwd(proj, bg, y_ssm, y_lru, *, name):
    t, d = y_ssm.shape
    tr = _pick(t, 256, 8)
    gb = OFF_GATES // d

    def body(gs_ref, gl_ref, bs_ref, bl_ref, ys_ref, yl_ref, o_ref):
        gs = _sigmoid(gs_ref[...] + bs_ref[...])
        gl = _sigmoid(gl_ref[...] + bl_ref[...])
        o_ref[...] = (gs * ys_ref[...].astype(f32) + gl * yl_ref[...].astype(f32)).astype(bf16)

    row = pl.BlockSpec((tr, d), lambda i: (i, 0))
    return pl.pallas_call(
        body, name=name, grid=(t // tr,),
        in_specs=[pl.BlockSpec((tr, d), lambda i: (i, gb)), pl.BlockSpec((tr, d), lambda i: (i, gb + 1)),
                  pl.BlockSpec((1, d), lambda i: (0, 0)), pl.BlockSpec((1, d), lambda i: (0, 1)), row, row],
        out_specs=row, out_shape=jax.ShapeDtypeStruct((t, d), bf16),
    )(proj, proj, bg, bg, y_ssm, y_lru)


def merge_bwd(proj, bg, y_ssm, y_lru, dmix, *, name):
    t, d = y_ssm.shape
    tr = _pick(t, 256, 8)
    gb = OFF_GATES // d

    def body(gs_ref, gl_ref, bs_ref, bl_ref, ys_ref, yl_ref, dm_ref, dg_ref, dys_ref, dyl_ref, dbg_ref):
        gs = _sigmoid(gs_ref[...] + bs_ref[...])
        gl = _sigmoid(gl_ref[...] + bl_ref[...])
        dm = dm_ref[...].astype(f32)
        dys_ref[...] = (dm * gs).astype(bf16)
        dyl_ref[...] = (dm * gl).astype(bf16)
        dgs = dm * ys_ref[...].astype(f32) * gs * (1.0 - gs)
        dgl = dm * yl_ref[...].astype(f32) * gl * (1.0 - gl)
        dg_ref[:, 0:d] = dgs.astype(bf16)
        dg_ref[:, d:2 * d] = dgl.astype(bf16)

        @pl.when(pl.program_id(0) == 0)
        def _():
            dbg_ref[...] = jnp.zeros_like(dbg_ref)

        dbg_ref[:, 0:d] += jnp.sum(dgs, axis=0, keepdims=True)
        dbg_ref[:, d:2 * d] += jnp.sum(dgl, axis=0, keepdims=True)

    row = pl.BlockSpec((tr, d), lambda i: (i, 0))
    return pl.pallas_call(
        body, name=name, grid=(t // tr,),
        in_specs=[pl.BlockSpec((tr, d), lambda i: (i, gb)), pl.BlockSpec((tr, d), lambda i: (i, gb + 1)),
                  pl.BlockSpec((1, d), lambda i: (0, 0)), pl.BlockSpec((1, d), lambda i: (0, 1)), row, row, row],
        out_specs=[pl.BlockSpec((tr, 2 * d), lambda i: (i, OFF_GATES // (2 * d))), row, row,
                   pl.BlockSpec((1, 2 * d), lambda i: (0, 0))],
        out_shape=[jax.ShapeDtypeStruct((t, PROJ_W), bf16), jax.ShapeDtypeStruct((t, d), bf16),
                   jax.ShapeDtypeStruct((t, d), bf16), jax.ShapeDtypeStruct((1, 2 * d), f32)],
        compiler_params=pltpu.CompilerParams(dimension_semantics=("arbitrary",)),
    )(proj, proj, bg, bg, y_ssm, y_lru, dmix)


def swiglu_fwd(ff, *, name):
    t = ff.shape[0]
    hd = FFN_HIDDEN
    tr = _pick(t, 128, 8)

    def body(f_ref, o_ref):
        o_ref[...] = (_silu(f_ref[:, 0:hd].astype(f32)) * f_ref[:, hd:2 * hd].astype(f32)).astype(bf16)

    return pl.pallas_call(
        body, name=name, grid=(t // tr,), in_specs=[pl.BlockSpec((tr, 2 * hd), lambda i: (i, 0))],
        out_specs=pl.BlockSpec((tr, hd), lambda i: (i, 0)), out_shape=jax.ShapeDtypeStruct((t, hd), bf16),
    )(ff)


def swiglu_bwd(ff, dact, *, name):
    t = ff.shape[0]
    hd = FFN_HIDDEN
    tr = _pick(t, 128, 8)

    def body(f_ref, d_ref, o_ref):
        gate, up, dv = f_ref[:, 0:hd].astype(f32), f_ref[:, hd:2 * hd].astype(f32), d_ref[...].astype(f32)
        o_ref[:, 0:hd] = (dv * up * _dsilu(gate)).astype(bf16)
        o_ref[:, hd:2 * hd] = (dv * _silu(gate)).astype(bf16)

    return pl.pallas_call(
        body, name=name, grid=(t // tr,),
        in_specs=[pl.BlockSpec((tr, 2 * hd), lambda i: (i, 0)), pl.BlockSpec((tr, hd), lambda i: (i, 0))],
        out_specs=pl.BlockSpec((tr, 2 * hd), lambda i: (i, 0)), out_shape=jax.ShapeDtypeStruct((t, 2 * hd), bf16),
    )(ff, dact)


def _adam_math(w, g, m, v):
    m = ADAM_B1 * m + (1.0 - ADAM_B1) * g
    v = ADAM_B2 * v + (1.0 - ADAM_B2) * (g * g)
    m_hat = m / (1.0 - ADAM_B1 ** ADAM_STEP)
    v_hat = v / (1.0 - ADAM_B2 ** ADAM_STEP)
    delta = -ADAM_LR * (m_hat / (jnp.sqrt(v_hat) + ADAM_EPS) + ADAM_WD * w)
    return delta, m, v


def _row_tile(rows, cols):
    cap = max(8, (1 << 18) // cols)
    return _pick(rows, cap, 8) if rows % 8 == 0 else rows


def adamw(w, g, m, v, *, name):
    rows, cols = w.shape
    tr = _row_tile(rows, cols)

    def body(w_ref, g_ref, m_ref, v_ref, d_ref, nm_ref, nv_ref):
        d, nm, nv = _adam_math(w_ref[...], g_ref[...], m_ref[...], v_ref[...])
        d_ref[...] = d
        nm_ref[...] = nm
        nv_ref[...] = nv

    tile = pl.BlockSpec((tr, cols), lambda i: (i, 0))
    return pl.pallas_call(
        body, name=name, grid=(rows // tr,), in_specs=[tile] * 4, out_specs=[tile] * 3,
        out_shape=[jax.ShapeDtypeStruct((rows, cols), f32)] * 3,
    )(w, g, m, v)


def adamw_many(ws, gs, ms, vs, *, name):
    n = len(ws)

    def body(*refs):
        for i in range(n):
            d, nm, nv = _adam_math(refs[i][...], refs[n + i][...], refs[2 * n + i][...], refs[3 * n + i][...])
            refs[4 * n + 3 * i][...] = d
            refs[4 * n + 3 * i + 1][...] = nm
            refs[4 * n + 3 * i + 2][...] = nv

    outs = pl.pallas_call(
        body, name=name, out_shape=[jax.ShapeDtypeStruct(w.shape, f32) for w in ws for _ in range(3)],
    )(*ws, *gs, *ms, *vs)
    return [tuple(outs[3 * i:3 * i + 3]) for i in range(n)]


def pair_add(dw, rbuf, idx, *, name):
    n, rows, cols = dw.shape
    hr = rows // 2
    tr = _row_tile(hr, cols)
    nrt = hr // tr

    def body(idx_ref, a_ref, b_ref, o_ref, own_ref):
        s = a_ref[...] + b_ref[...]
        o_ref[...] = s.astype(bf16)

        @pl.when(pl.program_id(1) == idx_ref[0])
        def _():
            own_ref[...] = s[0]

    return pl.pallas_call(
        body, name=name,
        grid_spec=pltpu.PrefetchScalarGridSpec(
            num_scalar_prefetch=1, grid=(nrt, n),
            in_specs=[pl.BlockSpec((1, tr, cols), lambda i, k, idx: (k, idx[1] * nrt + i, 0)),
                      pl.BlockSpec((1, tr, cols), lambda i, k, idx: (k, i, 0))],
            out_specs=[pl.BlockSpec((1, tr, cols), lambda i, k, idx: (k, i, 0)),
                       pl.BlockSpec((tr, cols), lambda i, k, idx: (i, 0))]),
        out_shape=[jax.ShapeDtypeStruct((n, hr, cols), bf16), jax.ShapeDtypeStruct((hr, cols), f32)],
    )(idx, dw, rbuf)


def chip_sum(own, rbuf, idx, *, name):
    hr, cols = own.shape
    tr = _row_tile(hr, cols)
    nrt = hr // tr

    def body(idx_ref, a_ref, b_ref, o_ref):
        o_ref[...] = ((a_ref[...] + b_ref[0].astype(f32)) + b_ref[1].astype(f32)) + b_ref[2].astype(f32)

    return pl.pallas_call(
        body, name=name,
        grid_spec=pltpu.PrefetchScalarGridSpec(
            num_scalar_prefetch=1, grid=(nrt,),
            in_specs=[pl.BlockSpec((tr, cols), lambda i, idx: (i, 0)),
                      pl.BlockSpec((3, tr, cols), lambda i, idx: (0, i, 0))],
            out_specs=pl.BlockSpec((tr, cols), lambda i, idx: (idx[1] * nrt + i, 0))),
        out_shape=jax.ShapeDtypeStruct((2 * hr, cols), f32),
    )(idx, own, rbuf)


def sum8(rbuf, *, name):
    n, rows, cols = rbuf.shape
    tr = _row_tile(rows, cols * n)

    def body(a_ref, o_ref):
        acc = a_ref[0]
        for k in range(1, n):
            acc = acc + a_ref[k]
        o_ref[...] = acc

    return pl.pallas_call(
        body, name=name, grid=(rows // tr,), in_specs=[pl.BlockSpec((n, tr, cols), lambda i: (0, i, 0))],
        out_specs=pl.BlockSpec((tr, cols), lambda i: (i, 0)), out_shape=jax.ShapeDtypeStruct((rows, cols), f32),
    )(rbuf)


def _coords():
    return lax.axis_index("x"), lax.axis_index("y"), lax.axis_index("c")


def _other_chips(x, y):
    return [(1 - x, y), (x, 1 - y), (1 - x, 1 - y)]


def gather_weights(shards, *, name):
    n = len(shards)
    halves = [s.shape[0] // 2 for s in shards]

    def body(*refs):
        ins, outs = refs[:n], refs[n:2 * n]
        send1, recv1, send2, recv2 = refs[2 * n:]
        x, y, c = _coords()
        me = 2 * x + y
        chips = _other_chips(x, y)
        sibling = (x, y, 1 - c)

        def half(i, k, hc):
            return outs[i].at[k, pl.ds(hc * halves[i], halves[i]), :]

        def ici(i, j):
            return pltpu.make_async_remote_copy(
                src_ref=ins[i].at[pl.ds(c * halves[i], halves[i]), :], dst_ref=half(i, me, c),
                send_sem=send1.at[i, j], recv_sem=recv1.at[i, j], device_id=(*chips[j], c), device_id_type=MESH)

        def landed(i, j):
            kj = 2 * chips[j][0] + chips[j][1]
            return pltpu.make_async_remote_copy(
                src_ref=half(i, kj, c), dst_ref=half(i, kj, c),
                send_sem=send2.at[i, j], recv_sem=recv1.at[i, j], device_id=sibling, device_id_type=MESH)

        def from_sibling(i, j):
            kj = 2 * chips[j][0] + chips[j][1]
            return pltpu.make_async_remote_copy(
                src_ref=half(i, kj, 1 - c), dst_ref=half(i, kj, 1 - c),
                send_sem=send2.at[i, j], recv_sem=recv2.at[i, j], device_id=sibling, device_id_type=MESH)

        def d2d(i, j):
            kj = 2 * chips[j][0] + chips[j][1]
            return pltpu.make_async_remote_copy(
                src_ref=half(i, kj, c), dst_ref=half(i, kj, c),
                send_sem=send2.at[i, j], recv_sem=recv2.at[i, j], device_id=sibling, device_id_type=MESH)

        for j in range(3):
            for i in range(n):
                ici(i, j).start()
        for j in range(3):
            for i in range(n):
                landed(i, j).wait_recv()
                d2d(i, j).start()
        for j in range(3):
            for i in range(n):
                from_sibling(i, j).wait_recv()
        for j in range(3):
            for i in range(n):
                ici(i, j).wait_send()
                d2d(i, j).wait_send()

    return pl.pallas_call(
        body, name=name, in_specs=[ANY] * n, out_specs=[ANY] * n,
        out_shape=[jax.ShapeDtypeStruct((N_CHIPS,) + s.shape, s.dtype) for s in shards],
        scratch_shapes=[pltpu.SemaphoreType.DMA((n, 3))] * 4,
    )(*shards)


def pair_exchange(grads, *, name):
    n = len(grads)
    halves = [g.shape[1] // 2 for g in grads]

    def body(*refs):
        ins, outs = refs[:n], refs[n:2 * n]
        send, recv = refs[2 * n:]
        x, y, c = _coords()
        cps = [pltpu.make_async_remote_copy(
            src_ref=ins[i].at[:, pl.ds((1 - c) * halves[i], halves[i]), :], dst_ref=outs[i],
            send_sem=send.at[i], recv_sem=recv.at[i], device_id=(x, y, 1 - c), device_id_type=MESH) for i in range(n)]
        for cp in cps:
            cp.start()
        for cp in cps:
            cp.wait()

    return pl.pallas_call(
        body, name=name, in_specs=[ANY] * n, out_specs=[ANY] * n,
        out_shape=[jax.ShapeDtypeStruct((N_CHIPS, g.shape[1] // 2, g.shape[2]), g.dtype) for g in grads],
        scratch_shapes=[pltpu.SemaphoreType.DMA((n,))] * 2,
    )(*grads)


def pair_gather(bufs, *, name):
    n = len(bufs)

    def body(*refs):
        ins, outs = refs[:n], refs[n:2 * n]
        send, recv = refs[2 * n:]
        x, y, c = _coords()
        cps = []
        for i in range(n):
            hr = ins[i].shape[0] // 2
            cps.append(pltpu.make_async_remote_copy(
                src_ref=ins[i].at[pl.ds(c * hr, hr), :], dst_ref=outs[i].at[pl.ds(c * hr, hr), :],
                send_sem=send.at[i], recv_sem=recv.at[i], device_id=(x, y, 1 - c), device_id_type=MESH))
        for cp in cps:
            cp.start()
        for i in range(n):
            hr = ins[i].shape[0] // 2
            pltpu.make_async_remote_copy(
                src_ref=ins[i].at[pl.ds((1 - c) * hr, hr), :], dst_ref=outs[i].at[pl.ds((1 - c) * hr, hr), :],
                send_sem=send.at[i], recv_sem=recv.at[i], device_id=(x, y, 1 - c), device_id_type=MESH).wait_recv()
        for cp in cps:
            cp.wait_send()

    return pl.pallas_call(
        body, name=name, in_specs=[ANY] * n, out_specs=[ANY] * n,
        out_shape=[jax.ShapeDtypeStruct(b.shape, b.dtype) for b in bufs],
        input_output_aliases={i: i for i in range(n)},
        scratch_shapes=[pltpu.SemaphoreType.DMA((n,))] * 2,
    )(*bufs)


def all_exchange(buf, *, name):
    rows, cols = buf.shape

    def body(in_ref, out_ref, send, recv):
        x, y, c = _coords()
        me = 4 * x + 2 * y + c
        cps = []
        for d in range(1, 8):
            px = 1 - x if d & 4 else x
            py = 1 - y if d & 2 else y
            pc = 1 - c if d & 1 else c
            cps.append(pltpu.make_async_remote_copy(
                src_ref=in_ref, dst_ref=out_ref.at[me], send_sem=send.at[d - 1], recv_sem=recv.at[d - 1],
                device_id=(px, py, pc), device_id_type=MESH))
        for cp in cps:
            cp.start()
        for d in range(1, 8):
            px = 1 - x if d & 4 else x
            py = 1 - y if d & 2 else y
            pc = 1 - c if d & 1 else c
            src = 4 * px + 2 * py + pc
            pltpu.make_async_remote_copy(
                src_ref=in_ref, dst_ref=out_ref.at[src], send_sem=send.at[d - 1], recv_sem=recv.at[d - 1],
                device_id=(px, py, pc), device_id_type=MESH).wait_recv()
        for cp in cps:
            cp.wait_send()

    return pl.pallas_call(
        body, name=name, in_specs=[ANY], out_specs=ANY,
        out_shape=jax.ShapeDtypeStruct((8, rows, cols), buf.dtype),
        scratch_shapes=[pltpu.SemaphoreType.DMA((7,)), pltpu.SemaphoreType.DMA((7,))],
    )(buf)


HBM = pl.BlockSpec(memory_space=pltpu.HBM)
SEM = pl.BlockSpec(memory_space=pltpu.SEMAPHORE)
EFFECT = pltpu.SideEffectType.DATAFLOW_SIDE_EFFECTING


def split_start(arrays, after, copies, sem_shape, *, name):
    na = len(arrays)

    def body(*refs):
        for cp in copies(refs[:na], refs[na + 1], refs[na + 2]):
            cp.start()
        refs[-1][...] = jnp.zeros((8, 128), f32)

    outs = pl.pallas_call(
        body, name=name,
        out_shape=(pltpu.SemaphoreType.DMA(sem_shape), pltpu.SemaphoreType.DMA(sem_shape),
                   *[pltpu.HBM(a.shape, a.dtype) for a in arrays], jax.ShapeDtypeStruct((8, 128), f32)),
        in_specs=[HBM] * na + [ANY], out_specs=(SEM, SEM, *[HBM] * na, pl.BlockSpec(memory_space=pltpu.VMEM)),
        input_output_aliases={i: 2 + i for i in range(na)},
        compiler_params=pltpu.CompilerParams(has_side_effects=EFFECT),
    )(*[pltpu.with_memory_space_constraint(a, pltpu.HBM) for a in arrays], after)
    return outs[0], outs[1], list(outs[2:2 + na]), outs[-1]


def split_wait(send, recv, arrays, after, copies, *, name):
    na = len(arrays)

    def body(*refs):
        for cp in copies(refs[:na], refs[na], refs[na + 1]):
            cp.wait_send()
            cp.wait_recv()

    outs = pl.pallas_call(
        body, name=name, out_shape=tuple(pltpu.HBM(a.shape, a.dtype) for a in arrays),
        in_specs=[HBM] * na + [SEM, SEM, ANY], out_specs=tuple([HBM] * na),
        input_output_aliases={i: i for i in range(na)},
        compiler_params=pltpu.CompilerParams(has_side_effects=EFFECT),
    )(*arrays, send, recv, after)
    return list(outs)


def gather_copies(n):
    def copies(refs, send, recv):
        x, y, c = _coords()
        me = 2 * x + y
        chips = _other_chips(x, y)
        return [pltpu.make_async_remote_copy(
            src_ref=refs[i], dst_ref=refs[n + i].at[me], send_sem=send.at[3 * i + j], recv_sem=recv.at[3 * i + j],
            device_id=(*chips[j], c), device_id_type=MESH) for j in range(3) for i in range(n)]
    return copies


def pair_copies(n):
    def copies(refs, send, recv):
        x, y, c = _coords()
        cps = []
        for i in range(n):
            hr = refs[i].shape[1] // 2
            cps.append(pltpu.make_async_remote_copy(
                src_ref=refs[i].at[:, pl.ds((1 - c) * hr, hr), :], dst_ref=refs[n + i], send_sem=send.at[i],
                recv_sem=recv.at[i], device_id=(x, y, 1 - c), device_id_type=MESH))
        return cps
    return copies


def all_copies():
    def copies(refs, send, recv):
        x, y, c = _coords()
        me = 4 * x + 2 * y + c
        cps = []
        for d in range(1, 8):
            peer = (1 - x if d & 4 else x, 1 - y if d & 2 else y, 1 - c if d & 1 else c)
            cps.append(pltpu.make_async_remote_copy(
                src_ref=refs[0], dst_ref=refs[1].at[me], send_sem=send.at[d - 1], recv_sem=recv.at[d - 1],
                device_id=peer, device_id_type=MESH))
        return cps
    return copies


def reduce_copies(n):
    def copies(refs, send, recv):
        x, y, c = _coords()
        chips = _other_chips(x, y)
        return [pltpu.make_async_remote_copy(
            src_ref=refs[i].at[2 * chips[j][0] + chips[j][1]], dst_ref=refs[n + i].at[j],
            send_sem=send.at[3 * i + j], recv_sem=recv.at[3 * i + j], device_id=(*chips[j], c), device_id_type=MESH)
            for j in range(3) for i in range(n)]
    return copies


def _pack(arrs):
    flat = []
    for a in arrs:
        v = a.reshape(-1).astype(f32)
        pad = (-v.shape[0]) % 128
        flat.append(jnp.pad(v, (0, pad)) if pad else v)
    v = jnp.concatenate(flat)
    rows = v.shape[0] // 128
    pad_rows = (-rows) % 256
    v = v.reshape(rows, 128)
    return jnp.pad(v, ((0, pad_rows), (0, 0))) if pad_rows else v


def _unpack(buf, shapes):
    out, row = [], 0
    for s in shapes:
        size = math.prod(s)
        rows = -(-size // 128)
        out.append(buf[row:row + rows].reshape(-1)[:size].reshape(s))
        row += rows
    return out


def _ref_of_perm():
    ref = np.arange(IN_PROJ_DIM)
    xbc = ref[4096:7168]
    xbc_p = [np.concatenate([xbc[g * 512:(g + 1) * 512], xbc[2048 + g * 128:2048 + (g + 1) * 128],
                             xbc[2560 + g * 128:2560 + (g + 1) * 128]]) for g in range(SSM_GROUPS)]
    return np.concatenate([ref[0:2048], ref[2048:4096], ref[7200:8480], ref[8480:9760], ref[7168:7200],
                           -np.ones(DT_PAD_W - SSM_HEADS, np.int64)] + xbc_p)


def _runs(vals):
    out, start = [], 0
    for i in range(1, len(vals) + 1):
        if i == len(vals) or not (vals[i] == vals[i - 1] + 1 or (vals[i] < 0 and vals[i - 1] < 0)):
            out.append((start, int(vals[start]), i - start))
            start = i
    return out


def _perm_in_from_shards(g):
    ref_of_perm = _ref_of_perm()
    sw = IN_PROJ_DIM // N_CHIPS
    parts = []
    for _, first, length in _runs(ref_of_perm):
        if first < 0:
            parts.append(jnp.zeros((g.shape[1], length), g.dtype))
            continue
        lo = first
        while lo < first + length:
            k = lo // sw
            hi = min(first + length, (k + 1) * sw)
            parts.append(g[k, :, lo - k * sw:hi - k * sw])
            lo = hi
    return jnp.concatenate(parts, axis=-1)


def _unperm_in_to_shards(w):
    ref_of_perm = _ref_of_perm()
    perm_of_ref = np.zeros(IN_PROJ_DIM, np.int64)
    perm_of_ref[ref_of_perm[ref_of_perm >= 0]] = np.nonzero(ref_of_perm >= 0)[0]
    sw = IN_PROJ_DIM // N_CHIPS
    shards = []
    for k in range(N_CHIPS):
        runs = _runs(perm_of_ref[k * sw:(k + 1) * sw])
        shards.append(jnp.concatenate([w[:, first:first + length] for _, first, length in runs], axis=-1))
    return jnp.stack(shards)


def _perm_xbc_cols(w):
    parts = []
    for g in range(SSM_GROUPS):
        parts += [w[..., g * 512:(g + 1) * 512], w[..., 2048 + g * 128:2048 + (g + 1) * 128],
                  w[..., 2560 + g * 128:2560 + (g + 1) * 128]]
    return jnp.concatenate(parts, axis=-1)


def _unperm_xbc_cols(w):
    xs = [w[..., g * XBC_GROUP_W:g * XBC_GROUP_W + 512] for g in range(SSM_GROUPS)]
    bs = [w[..., g * XBC_GROUP_W + 512:g * XBC_GROUP_W + 640] for g in range(SSM_GROUPS)]
    cs = [w[..., g * XBC_GROUP_W + 640:(g + 1) * XBC_GROUP_W] for g in range(SSM_GROUPS)]
    return jnp.concatenate(xs + bs + cs, axis=-1)


def _from_col_shards(w):
    n, r, c = w.shape
    return jnp.transpose(w, (1, 0, 2)).reshape(r, n * c)


def kernel(x, norm1_w, w_in, b_branch_gate, ssm_conv_w, ssm_conv_b, ssm_dt_bias, ssm_a_log, ssm_d, ssm_norm_w, w_out_ssm, lru_conv_w, lru_conv_b, lru_w_r, lru_b_r, lru_w_i, lru_b_i, lru_lambda, w_out_lru, w_out, norm2_w, w_ffn_in, w_ffn_out, norm_f_w, loss_target, m_norm1_w, m_w_in, m_b_branch_gate, m_ssm_conv_w, m_ssm_conv_b, m_ssm_dt_bias, m_ssm_a_log, m_ssm_d, m_ssm_norm_w, m_w_out_ssm, m_lru_conv_w, m_lru_conv_b, m_lru_w_r, m_lru_b_r, m_lru_w_i, m_lru_b_i, m_lru_lambda, m_w_out_lru, m_w_out, m_norm2_w, m_w_ffn_in, m_w_ffn_out, m_norm_f_w, v_norm1_w, v_w_in, v_b_branch_gate, v_ssm_conv_w, v_ssm_conv_b, v_ssm_dt_bias, v_ssm_a_log, v_ssm_d, v_ssm_norm_w, v_w_out_ssm, v_lru_conv_w, v_lru_conv_b, v_lru_w_r, v_lru_b_r, v_lru_w_i, v_lru_b_i, v_lru_lambda, v_w_out_lru, v_w_out, v_norm2_w, v_w_ffn_in, v_w_ffn_out, v_norm_f_w):
    xi, yi, ci = lax.axis_index("x"), lax.axis_index("y"), lax.axis_index("c")
    me = 2 * xi + yi
    idx = jnp.stack([me, ci]).astype(jnp.int32)
    x2 = x[0]
    tgt = loss_target[0]

    big_names = ["w_in", "w_out_ssm", "w_out_lru", "w_out", "w_ffn_in", "w_ffn_out"]
    big_w = dict(w_in=w_in[0], w_out_ssm=w_out_ssm[0], w_out_lru=w_out_lru[0], w_out=w_out[0], w_ffn_in=w_ffn_in[0],
                 w_ffn_out=w_ffn_out[0])
    big_m = dict(w_in=m_w_in[0], w_out_ssm=m_w_out_ssm[0], w_out_lru=m_w_out_lru[0], w_out=m_w_out[0],
                 w_ffn_in=m_w_ffn_in[0], w_ffn_out=m_w_ffn_out[0])
    big_v = dict(w_in=v_w_in[0], w_out_ssm=v_w_out_ssm[0], w_out_lru=v_w_out_lru[0], w_out=v_w_out[0],
                 w_ffn_in=v_w_ffn_in[0], w_ffn_out=v_w_ffn_out[0])
    conv_pad = jnp.zeros((16, 768), f32).at[0:4, :].set(ssm_conv_w[0]).at[8:12, 0:320].set(lru_conv_w[0])
    mine = [big_w["w_in"].astype(bf16), conv_pad]
    gathered = gather_weights(mine, name="gather_weights")
    g_in, g_conv = [lax.dynamic_update_index_in_dim(g, s, me, 0) for g, s in zip(gathered, mine)]
    w_in_p = _perm_in_from_shards(g_in)
    late_names = big_names[1:]
    late = [big_w[k].astype(bf16) for k in late_names]
    late_lands = [lax.empty((N_CHIPS,) + s.shape, bf16) for s in late]
    g_send, g_recv, g_arrays, g_token = split_start(late + late_lands, g_conv, gather_copies(5), (15,),
                                                    name="gather_late_start")
    ssm_cw_full = _from_col_shards(g_conv[:, 0:4, :])
    lru_cw_full = _from_col_shards(g_conv[:, 8:12, 0:320])
    ssm_cw_p = _perm_xbc_cols(ssm_cw_full)
    ssm_cb_p = _perm_xbc_cols(ssm_conv_b)

    par = jnp.stack([ssm_dt_bias[0], ssm_a_log[0], ssm_d[0]], axis=0).reshape(3, SSM_GROUPS, SSM_HPG)
    par_row = jnp.zeros((SSM_GROUPS, 8, 8), f32).at[:, 0:3, :].set(jnp.transpose(par, (1, 0, 2)))
    par_col = jnp.transpose(par_row, (0, 2, 1))

    hn1 = rms_fwd(x2, norm1_w + g_token[0:1, 0:1], name="rms1_fwd")
    proj = mm(hn1, w_in_p, "nn", name="in_proj")
    t = x2.shape[0]
    dtr = jnp.transpose(proj[:, OFF_DT:OFF_DT + 32].reshape(t, SSM_GROUPS, SSM_HPG), (1, 0, 2))
    dtr_t = jnp.transpose(dtr, (0, 2, 1))
    xbc_pre, xbc_post = conv_fwd(proj, OFF_XBC, SSM_CONV_DIM, ssm_cw_p, ssm_cb_p, silu=True, name="ssm_conv_fwd")
    y_ssd, s_in = ssd_fwd(xbc_post, dtr, dtr_t, par_row, par_col, name="ssd_fwd")
    yn = gnorm_fwd(y_ssd, proj, ssm_norm_w, name="gnorm_fwd")
    g_arrays = split_wait(g_send, g_recv, g_arrays, yn, gather_copies(5), name="gather_late_wait")
    g_out_ssm, g_out_lru, g_out, g_ffn_in, g_ffn_out = [
        lax.dynamic_update_index_in_dim(g, s, me, 0) for g, s in zip(g_arrays[5:], late)]
    w_out_ssm_f = g_out_ssm.reshape(SSM_D_INNER, D_MODEL)
    w_out_lru_f = g_out_lru.reshape(LRU_WIDTH, D_MODEL)
    w_out_f = g_out.reshape(D_MODEL, D_MODEL)
    w_ffn_out_f = g_ffn_out.reshape(FFN_HIDDEN, D_MODEL)
    y_ssm = mm(yn, w_out_ssm_f, "nn", out_dtype=bf16, name="out_ssm")
    (u_lru,) = conv_fwd(proj, OFF_LX, LRU_WIDTH, lru_cw_full, lru_conv_b, silu=False, name="lru_conv_fwd")
    h_lru, o_lru = lru_fwd(u_lru, proj, lru_w_r[0], lru_b_r, lru_w_i[0], lru_b_i, lru_lambda, name="lru_fwd")
    y_lru = mm(o_lru, w_out_lru_f, "nn", out_dtype=bf16, name="out_lru")
    mix = merge_fwd(proj, b_branch_gate, y_ssm, y_lru, name="merge_fwd")
    h1 = mm(mix, w_out_f, "nn", add=x2, name="out_proj")
    hn2 = rms_fwd(h1, norm2_w, name="rms2_fwd")
    ff = mm(hn2, g_ffn_in, "nn", b_shards=True, out_dtype=bf16, name="ffn_in")
    act = swiglu_fwd(ff, name="swiglu_fwd")
    h2 = mm(act, w_ffn_out_f, "nn", add=h1, name="ffn_out")
    loss_tile, dh2, d_norm_f = loss_head(h2, norm_f_w.reshape(1, D_MODEL), tgt, name="loss_head")
    loss = lax.psum(loss_tile[0, 0], ("x", "y", "c"))

    d_w_ffn_out = mm(act, dh2, "tn", name="d_w_ffn_out")
    dact = mm(dh2, w_ffn_out_f, "nt", out_dtype=bf16, name="d_act")
    dff = swiglu_bwd(ff, dact, name="swiglu_bwd")
    d_w_ffn_in = mm(hn2, dff, "tn", out_shards=N_CHIPS, name="d_w_ffn_in")
    dhn2 = mm(dff, g_ffn_in, "nt", b_shards=True, name="d_hn2")
    dh1, d_norm2 = rms_bwd(h1, norm2_w, dhn2, dh2, name="rms2_bwd")
    d_w_out = mm(mix, dh1, "tn", name="d_w_out")
    dmix = mm(dh1, w_out_f, "nt", out_dtype=bf16, name="d_mix")
    dproj, dy_ssm, dy_lru, d_bg = merge_bwd(proj, b_branch_gate, y_ssm, y_lru, dmix, name="merge_bwd")
    d_w_out_ssm = mm(yn, dy_ssm, "tn", name="d_w_out_ssm")
    d_w_out_lru = mm(o_lru, dy_lru, "tn", name="d_w_out_lru")
    early_g = [d_w_out_ssm.reshape(N_CHIPS, 512, D_MODEL), d_w_out_lru.reshape(N_CHIPS, 320, D_MODEL),
               d_w_out.reshape(N_CHIPS, 256, D_MODEL), d_w_ffn_in, d_w_ffn_out.reshape(N_CHIPS, 704, D_MODEL)]
    p_lands = [lax.empty((N_CHIPS, g.shape[1] // 2, g.shape[2]), f32) for g in early_g]
    p_send, p_recv, p_arrays, p_token = split_start(early_g + p_lands, early_g[0], pair_copies(5), (5,),
                                                    name="pair_early_start")
    dyn = mm(dy_ssm, w_out_ssm_f, "nt", out_dtype=bf16, after=p_token, name="d_yn")
    dy_ssd, dproj, d_ssm_norm = gnorm_bwd(y_ssd, proj, ssm_norm_w, dyn, dproj, name="gnorm_bwd")
    p_arrays = split_wait(p_send, p_recv, p_arrays, dy_ssd, pair_copies(5), name="pair_early_wait")
    e_pairs = [pair_add(g, rb, idx, name="pair_add_" + k) for g, rb, k in zip(p_arrays[:5], p_arrays[5:], late_names)]
    e_lands = [lax.empty((3,) + p[0].shape[1:], bf16) for p in e_pairs]
    e_send, e_recv, e_arrays, e_token = split_start([p[0] for p in e_pairs] + e_lands, e_pairs[0][1], reduce_copies(5),
                                                    (15,), name="reduce_early_start")
    dxbc_post, ddtr, dpar = ssd_bwd(xbc_post, dtr, dtr_t, par_row + e_token[0:1, 0:1], par_col, s_in, dy_ssd,
                                    name="ssd_bwd")
    dproj, d_ssm_cw_p, d_ssm_cb_p = conv_bwd(dxbc_post, xbc_pre, proj, OFF_XBC, ssm_cw_p, dproj, name="ssm_conv_bwd")
    do_lru = mm(dy_lru, w_out_lru_f, "nt", name="d_o_lru")
    du_lru, dproj, d_w_r, d_w_i, d_b_r, d_b_i, d_lam = lru_bwd(u_lru, proj, h_lru, do_lru, lru_w_r[0], lru_b_r, lru_w_i[0],
                                                               lru_b_i, lru_lambda, dproj, name="lru_bwd")
    dproj, d_lru_cw, d_lru_cb = conv_bwd(du_lru, None, proj, OFF_LX, lru_cw_full, dproj, name="lru_conv_bwd")
    ddt_cols = jnp.transpose(ddtr, (1, 0, 2)).reshape(t, SSM_HEADS).astype(bf16)
    ddt_cols = jnp.pad(ddt_cols, ((0, 0), (0, DT_PAD_W - SSM_HEADS)))
    dproj = lax.dynamic_update_slice(dproj, ddt_cols, (0, OFF_DT))

    d_ssm_cw = _unperm_xbc_cols(d_ssm_cw_p)
    d_ssm_cb = _unperm_xbc_cols(d_ssm_cb_p)
    dpar_h = jnp.transpose(dpar[:, 0:3, :], (1, 0, 2)).reshape(3, SSM_HEADS)
    small_names = ["norm1_w", "b_branch_gate", "ssm_conv_b", "ssm_dt_bias", "ssm_a_log", "ssm_d", "ssm_norm_w",
                   "lru_conv_b", "lru_w_r", "lru_b_r", "lru_w_i", "lru_b_i", "lru_lambda", "norm2_w", "norm_f_w"]
    small_g = dict(norm1_w=jnp.zeros_like(norm1_w), b_branch_gate=d_bg, ssm_conv_b=d_ssm_cb, ssm_dt_bias=dpar_h[0:1], ssm_a_log=dpar_h[1:2],
                   ssm_d=dpar_h[2:3], ssm_norm_w=d_ssm_norm, lru_conv_b=d_lru_cb, lru_w_r=d_w_r[None], lru_b_r=d_b_r,
                   lru_w_i=d_w_i[None], lru_b_i=d_b_i, lru_lambda=d_lam, norm2_w=d_norm2, norm_f_w=d_norm_f.reshape(D_MODEL))
    small_w = dict(norm1_w=norm1_w, b_branch_gate=b_branch_gate, ssm_conv_b=ssm_conv_b, ssm_dt_bias=ssm_dt_bias,
                   ssm_a_log=ssm_a_log, ssm_d=ssm_d, ssm_norm_w=ssm_norm_w, lru_conv_b=lru_conv_b, lru_w_r=lru_w_r,
                   lru_b_r=lru_b_r, lru_w_i=lru_w_i, lru_b_i=lru_b_i, lru_lambda=lru_lambda, norm2_w=norm2_w, norm_f_w=norm_f_w)
    small_m = dict(norm1_w=m_norm1_w, b_branch_gate=m_b_branch_gate, ssm_conv_b=m_ssm_conv_b, ssm_dt_bias=m_ssm_dt_bias,
                   ssm_a_log=m_ssm_a_log, ssm_d=m_ssm_d, ssm_norm_w=m_ssm_norm_w, lru_conv_b=m_lru_conv_b, lru_w_r=m_lru_w_r,
                   lru_b_r=m_lru_b_r, lru_w_i=m_lru_w_i, lru_b_i=m_lru_b_i, lru_lambda=m_lru_lambda, norm2_w=m_norm2_w,
                   norm_f_w=m_norm_f_w)
    small_v = dict(norm1_w=v_norm1_w, b_branch_gate=v_b_branch_gate, ssm_conv_b=v_ssm_conv_b, ssm_dt_bias=v_ssm_dt_bias,
                   ssm_a_log=v_ssm_a_log, ssm_d=v_ssm_d, ssm_norm_w=v_ssm_norm_w, lru_conv_b=v_lru_conv_b, lru_w_r=v_lru_w_r,
                   lru_b_r=v_lru_b_r, lru_w_i=v_lru_w_i, lru_b_i=v_lru_b_i, lru_lambda=v_lru_lambda, norm2_w=v_norm2_w,
                   norm_f_w=v_norm_f_w)
    shapes = [small_w[k].shape for k in small_names]
    conv_shapes = [(4, SSM_CONV_DIM), (4, LRU_WIDTH)]
    g_pack = _pack([small_g[k] for k in small_names] + [d_ssm_cw, d_lru_cw])
    s_send, s_recv, s_arrays, s_token = split_start([g_pack, lax.empty((8,) + g_pack.shape, f32)], g_pack, all_copies(),
                                                    (7,), name="small_start")
    d_w_in_p = mm(hn1, dproj, "tn", after=s_token, name="d_w_in")

    d_w_in_s = _unperm_in_to_shards(d_w_in_p)
    (l_sib,) = pair_exchange([d_w_in_s], name="pair_exchange_late")
    l_pair = pair_add(d_w_in_s, l_sib, idx, name="pair_add_w_in")
    l_land = lax.empty((3,) + l_pair[0].shape[1:], bf16)
    l_send, l_recv, l_arrays, l_token = split_start([l_pair[0], l_land], l_pair[1], reduce_copies(1), (3,),
                                                    name="reduce_late_start")
    dhn1 = mm(dproj, w_in_p, "nt", after=l_token, name="d_hn1")
    grad_x, d_norm1 = rms_bwd(x2, norm1_w, dhn1, dh1, name="rms1_bwd")

    e_arrays = split_wait(e_send, e_recv, e_arrays, d_norm1, reduce_copies(5), name="reduce_early_wait")
    e_half = [chip_sum(p[1], rb, idx, name="chip_sum_" + k) for p, rb, k in zip(e_pairs, e_arrays[5:], late_names)]
    big_out = {}
    for k, g in zip(late_names, pair_gather(e_half, name="pair_gather_early")):
        big_out[k] = (g,) + tuple(adamw(big_w[k], g, big_m[k], big_v[k], name="adamw_" + k))

    s_arrays = split_wait(s_send, s_recv, s_arrays, d_norm1, all_copies(), name="small_wait")
    g_sum = sum8(lax.dynamic_update_index_in_dim(s_arrays[1], g_pack, 2 * me + ci, 0), name="sum8")
    n1 = d_norm1.reshape(8, 128)
    n1_sum = sum8(lax.dynamic_update_index_in_dim(all_exchange(n1, name="all_exchange_norm1"), n1, 2 * me + ci, 0),
                  name="sum8_norm1")
    g_sum = lax.dynamic_update_slice(g_sum, n1_sum, (0, 0))
    g_small = _unpack(g_sum, shapes + conv_shapes)
    g_small[-2] = lax.dynamic_slice_in_dim(g_small[-2], me * 768, 768, axis=1)
    g_small[-1] = lax.dynamic_slice_in_dim(g_small[-1], me * 320, 320, axis=1)
    all_names = small_names + ["ssm_conv_w", "lru_conv_w"]
    small_w.update(ssm_conv_w=ssm_conv_w[0], lru_conv_w=lru_conv_w[0])
    small_m.update(ssm_conv_w=m_ssm_conv_w[0], lru_conv_w=m_lru_conv_w[0])
    small_v.update(ssm_conv_w=v_ssm_conv_w[0], lru_conv_w=v_lru_conv_w[0])
    as2d = lambda a: a.reshape(-1, a.shape[-1])
    upd = adamw_many([as2d(small_w[k]) for k in all_names], [as2d(g) for g in g_small],
                     [as2d(small_m[k]) for k in all_names], [as2d(small_v[k]) for k in all_names], name="adamw_small")
    small_out = {}
    for k, g, u in zip(all_names, g_small, upd):
        small_out[k] = (g,) + tuple(o.reshape(g.shape) for o in u)
    l_arrays = split_wait(l_send, l_recv, l_arrays, upd[0][0], reduce_copies(1), name="reduce_late_wait")
    l_half = chip_sum(l_pair[1], l_arrays[1], idx, name="chip_sum_w_in")
    (g_w_in,) = pair_gather([l_half], name="pair_gather_late")
    big_out["w_in"] = (g_w_in,) + tuple(adamw(big_w["w_in"], g_w_in, big_m["w_in"], big_v["w_in"], name="adamw_w_in"))

    order = ["norm1_w", "w_in", "b_branch_gate", "ssm_conv_w", "ssm_conv_b", "ssm_dt_bias", "ssm_a_log", "ssm_d", "ssm_norm_w",
             "w_out_ssm", "lru_conv_w", "lru_conv_b", "lru_w_r", "lru_b_r", "lru_w_i", "lru_b_i", "lru_lambda", "w_out_lru",
             "w_out", "norm2_w", "w_ffn_in", "w_ffn_out", "norm_f_w"]
    outs = [loss, grad_x[None]]
    for which in range(4):
        for k in order:
            if k in big_out:
                outs.append(big_out[k][which][None])
            elif k in ("ssm_conv_w", "lru_conv_w"):
                outs.append(small_out[k][which][None])
            else:
                outs.append(small_out[k][which])
    return tuple(outs)
```

```python
import functools
import math

import jax
import jax.numpy as jnp
import numpy as np
from jax import lax
from jax.experimental import pallas as pl
from jax.experimental.pallas import tpu as pltpu

f32 = jnp.float32
bf16 = jnp.bfloat16

D_MODEL = 1024
SSM_D_INNER = 2048
SSM_HEADS = 32
SSM_HEAD_DIM = 64
SSM_GROUPS = 4
SSM_HPG = 8
SSM_D_STATE = 128
SSM_CHUNK = 128
SSM_GROUP_W = 512
SSM_CONV_DIM = 3072
XBC_GROUP_W = 768
LRU_WIDTH = 1280
LRU_BLOCKS = 10
LRU_BLOCK = 128
LRU_C = 8.0
FFN_HIDDEN = 2816
RMS_EPS = 1e-6
IN_PROJ_DIM = 9760
N_CHIPS = 4

OFF_GATES = 0
OFF_Z = 2048
OFF_LX = 4096
OFF_LY = 5376
OFF_DT = 6656
DT_PAD_W = 256
OFF_XBC = 6912
PROJ_W = 9984

ADAM_LR = 0.001
ADAM_B1 = 0.9
ADAM_B2 = 0.999
ADAM_EPS = 1e-08
ADAM_WD = 0.01
ADAM_STEP = 10

MESH = pl.DeviceIdType.MESH
ANY = pl.BlockSpec(memory_space=pl.ANY)

NN = (((1,), (0,)), ((), ()))
NT = (((1,), (1,)), ((), ()))
TN = (((0,), (0,)), ((), ()))


def _pick(n, cap, mult=128):
    best = None
    for t in range(mult, min(n, cap) + 1, mult):
        if n % t == 0:
            best = t
    return best if best is not None else n


def _sigmoid(x):
    return 0.5 * jnp.tanh(0.5 * x) + 0.5


def _softplus(x):
    return jnp.maximum(x, 0.0) + jnp.log(1.0 + jnp.exp(-jnp.abs(x)))


def _silu(x):
    return x * _sigmoid(x)


def _dsilu(x):
    s = _sigmoid(x)
    return s * (1.0 + x * (1.0 - s))


_GELU_K = math.sqrt(2.0 / math.pi)


def _gelu(x):
    return 0.5 * x * (1.0 + jnp.tanh(_GELU_K * (x + 0.044715 * x * x * x)))


def _dgelu(x):
    t = jnp.tanh(_GELU_K * (x + 0.044715 * x * x * x))
    return 0.5 * (1.0 + t) + 0.5 * x * (1.0 - t * t) * _GELU_K * (1.0 + 3.0 * 0.044715 * x * x)


def _expm1(x):
    poly = x * (1.0 + x * (0.5 + x * (1.0 / 6.0 + x * (1.0 / 24.0 + x * (1.0 / 120.0 + x * (1.0 / 720.0))))))
    return jnp.where(jnp.abs(x) < 0.1, poly, jnp.exp(x) - 1.0)


def _dot(a, b, dn):
    return lax.dot_general(a.astype(bf16), b.astype(bf16), dn, preferred_element_type=f32)


def _dot_01(a, b, dn, split, terms):
    r = a if split == 0 else b
    out = None
    for _ in range(terms):
        h = r.astype(bf16)
        r = r - h.astype(f32)
        d = lax.dot_general(h if split == 0 else a.astype(bf16), b.astype(bf16) if split == 0 else h, dn,
                            preferred_element_type=f32)
        out = d if out is None else out + d
    return out


MM_WHOLE_K = 2816
MM_VMEM_BUDGET = 40 * 2 ** 20

def mm(a, b, mode, *, name, add=None, after=None, out_dtype=f32, b_shards=False, out_shards=0):
    bs = b.shape[1:] if b_shards else b.shape
    shard_w = b.shape[2] if b_shards else None
    bcols = bs[1] * (b.shape[0] if b_shards else 1)
    if mode == "nn":
        (m, k), (k2, n) = a.shape, (bs[0], bcols)
    elif mode == "nt":
        (m, k), (n, k2) = a.shape, (bs[0], bcols)
    else:
        (k, m), (k2, n) = a.shape, b.shape
    assert k == k2, (a.shape, b.shape, mode)
    tk = k if k <= MM_WHOLE_K else _pick(k, 1024 if mode == "tn" else 2048)
    tn = _pick(n, 1536)
    if b_shards and mode == "nn":
        tn = shard_w
    if b_shards and mode == "nt":
        tk = shard_w
    if out_shards:
        tn = n // out_shards
    isz = lambda v: jnp.dtype(v.dtype).itemsize
    for cap in (1536, 1024, 512, 256):
        tm = _pick(m, cap)
        vmem = 2 * (tm * tk * isz(a) + tk * tn * isz(b) + tm * tn * (4 * int(add is not None) + jnp.dtype(out_dtype).itemsize))
        vmem += 4 * tm * tn * int(k > tk)
        if vmem <= MM_VMEM_BUDGET:
            break
    nk = k // tk
    dn = {"nn": NN, "nt": NT, "tn": TN}[mode]
    a_spec = pl.BlockSpec((tk, tm), lambda i, j, kk: (kk, i)) if mode == "tn" else pl.BlockSpec((tm, tk), lambda i, j, kk: (i, kk))
    b_spec = pl.BlockSpec((tn, tk), lambda i, j, kk: (j, kk)) if mode == "nt" else pl.BlockSpec((tk, tn), lambda i, j, kk: (kk, j))
    if b_shards:
        b_spec = (pl.BlockSpec((None, tn, tk), lambda i, j, kk: (kk, j, 0)) if mode == "nt"
                  else pl.BlockSpec((None, tk, tn), lambda i, j, kk: (j, kk, 0)))
    o_spec = pl.BlockSpec((tm, tn), lambda i, j, kk: (i, j))
    out_shape = jax.ShapeDtypeStruct((m, n), out_dtype)
    if out_shards:
        assert add is None
        o_spec = pl.BlockSpec((None, tm, tn), lambda i, j, kk: (j, i, 0))
        out_shape = jax.ShapeDtypeStruct((out_shards, m, tn), out_dtype)
    has_add = add is not None

    n_extra = int(has_add) + int(after is not None)

    def body(a_ref, b_ref, *rest):
        add_ref = rest[0] if has_add else None
        o_ref = rest[n_extra]

        def finish(r):
            if has_add:
                r = r + add_ref[...]
            o_ref[...] = r.astype(out_dtype)

        if nk == 1:
            finish(_dot(a_ref[...], b_ref[...], dn))
            return
        acc = rest[-1]
        kk = pl.program_id(2)

        @pl.when(kk == 0)
        def _():
            acc[...] = jnp.zeros_like(acc)

        acc[...] += _dot(a_ref[...], b_ref[...], dn)

        @pl.when(kk == nk - 1)
        def _():
            finish(acc[...])

    ins = [a, b] + ([add] if has_add else []) + ([after] if after is not None else [])
    in_specs = [a_spec, b_spec] + ([o_spec] if has_add else []) + ([ANY] if after is not None else [])
    return pl.pallas_call(
        body, name=name, grid=(m // tm, n // tn, nk), in_specs=in_specs, out_specs=o_spec, out_shape=out_shape,
        scratch_shapes=[pltpu.VMEM((tm, tn), f32)] if nk > 1 else [],
        compiler_params=pltpu.CompilerParams(dimension_semantics=("parallel", "parallel", "arbitrary")),
    )(*ins)


def rms_fwd(x, w, *, name):
    t, d = x.shape
    tr = _pick(t, 256, 8)

    def body(x_ref, w_ref, o_ref):
        xv = x_ref[...]
        r = lax.rsqrt(jnp.mean(xv * xv, axis=-1, keepdims=True) + RMS_EPS)
        o_ref[...] = (xv * r * w_ref[...]).astype(bf16)

    return pl.pallas_call(
        body, name=name, grid=(t // tr,),
        in_specs=[pl.BlockSpec((tr, d), lambda i: (i, 0)), pl.BlockSpec((1, d), lambda i: (0, 0))],
        out_specs=pl.BlockSpec((tr, d), lambda i: (i, 0)), out_shape=jax.ShapeDtypeStruct((t, d), bf16),
    )(x, w)


def _rms_bwd_math(xv, wv, dy):
    r = lax.rsqrt(jnp.mean(xv * xv, axis=-1, keepdims=True) + RMS_EPS)
    g = dy * wv
    dx = r * g - xv * (r * r * r) * jnp.mean(g * xv, axis=-1, keepdims=True)
    dw = jnp.sum(dy * xv * r, axis=0, keepdims=True)
    return dx, dw


def rms_bwd(x, w, dy, res, *, name):
    t, d = x.shape
    tr = _pick(t, 256, 8)

    def body(x_ref, w_ref, dy_ref, res_ref, dx_ref, dw_ref):
        dx, dw = _rms_bwd_math(x_ref[...], w_ref[...], dy_ref[...])
        dx_ref[...] = dx + res_ref[...]

        @pl.when(pl.program_id(0) == 0)
        def _():
            dw_ref[...] = jnp.zeros_like(dw_ref)

        dw_ref[...] += dw

    row = pl.BlockSpec((tr, d), lambda i: (i, 0))
    vec = pl.BlockSpec((1, d), lambda i: (0, 0))
    return pl.pallas_call(
        body, name=name, grid=(t // tr,), in_specs=[row, vec, row, row], out_specs=[row, vec],
        out_shape=[jax.ShapeDtypeStruct((t, d), f32), jax.ShapeDtypeStruct((1, d), f32)],
        compiler_params=pltpu.CompilerParams(dimension_semantics=("arbitrary",)),
    )(x, w, dy, res)


def loss_head(h, w, target, *, name):
    t, d = h.shape
    tr = _pick(t, 256, 8)

    def body(h_ref, w_ref, t_ref, loss_ref, dh_ref, dw_ref):
        xv, wv = h_ref[...], w_ref[...]
        r = lax.rsqrt(jnp.mean(xv * xv, axis=-1, keepdims=True) + RMS_EPS)
        err = xv * r * wv - t_ref[...]
        part = 0.5 * jnp.sum(jnp.mean(err * err, axis=-1, keepdims=True), axis=0, keepdims=True)
        dx, dw = _rms_bwd_math(xv, wv, err * (1.0 / d))
        dh_ref[...] = dx

        @pl.when(pl.program_id(0) == 0)
        def _():
            dw_ref[...] = jnp.zeros_like(dw_ref)
            loss_ref[...] = jnp.zeros_like(loss_ref)

        dw_ref[...] += dw
        loss_ref[...] += part

    row = pl.BlockSpec((tr, d), lambda i: (i, 0))
    vec = pl.BlockSpec((1, d), lambda i: (0, 0))
    return pl.pallas_call(
        body, name=name, grid=(t // tr,), in_specs=[row, vec, row],
        out_specs=[pl.BlockSpec((8, 128), lambda i: (0, 0)), row, vec],
        out_shape=[jax.ShapeDtypeStruct((8, 128), f32), jax.ShapeDtypeStruct((t, d), f32), jax.ShapeDtypeStruct((1, d), f32)],
        compiler_params=pltpu.CompilerParams(dimension_semantics=("arbitrary",)),
    )(h, w, target)


CONV_ROWS = 512
VREG_ELEMS = 8 * 128


def _conv_chunk(tc):
    return 16 if (16 + 8) * tc * 3 > 48 * VREG_ELEMS else 32


def conv_fwd(src, col0, width, w, b, *, silu, name):
    t = src.shape[0]
    tc = _pick(math.gcd(width, col0), 768)
    assert col0 % tc == 0
    cb = col0 // tc
    r = CONV_ROWS
    ch = _conv_chunk(tc)

    def body(u_ref, w_ref, b_ref, *rest):
        ext = rest[-1]
        j = pl.program_id(1)

        @pl.when(j == 0)
        def _():
            ext[0:8, :] = jnp.zeros((8, tc), f32)

        @pl.when(j > 0)
        def _():
            ext[0:8, :] = ext[r:r + 8, :]

        ext[8:r + 8, :] = u_ref[...]
        wv = w_ref[...]
        bv = b_ref[...]

        def chunk(c, carry):
            r0 = pl.multiple_of(c * ch, ch)
            v = ext[pl.ds(r0, ch + 8), :]
            acc = bv + wv[3:4, :] * v[8:, :]
            for s in (1, 2, 3):
                acc = acc + wv[3 - s:4 - s, :] * pltpu.roll(v, s, 0)[8:, :]
            rest[0][pl.ds(r0, ch), :] = acc
            if silu:
                rest[1][pl.ds(r0, ch), :] = _silu(acc)
            return carry

        lax.fori_loop(0, r // ch, chunk, 0)

    tile = pl.BlockSpec((r, tc), lambda c, j: (j, c))
    n_out = 2 if silu else 1
    return pl.pallas_call(
        body, name=name, grid=(width // tc, t // r),
        in_specs=[pl.BlockSpec((r, tc), lambda c, j: (j, cb + c)), pl.BlockSpec((4, tc), lambda c, j: (0, c)),
                  pl.BlockSpec((1, tc), lambda c, j: (0, c))],
        out_specs=[tile] * n_out, out_shape=[jax.ShapeDtypeStruct((t, width), f32)] * n_out,
        scratch_shapes=[pltpu.VMEM((r + 8, tc), f32)],
        compiler_params=pltpu.CompilerParams(dimension_semantics=("parallel", "arbitrary")),
    )(src, w, b)


def conv_bwd(dpost, pre, src, col0, w, dst, *, name):
    t, width = dpost.shape
    tc = _pick(math.gcd(width, col0), 768)
    assert col0 % tc == 0
    cb = col0 // tc
    r = CONV_ROWS
    ch = _conv_chunk(tc)
    nt = t // r
    has_pre = pre is not None

    def body(*refs):
        refs = refs[1:]
        if has_pre:
            d_ref, p_ref, u_ref, w_ref, du_ref, dw_ref, db_ref, ext = refs
        else:
            d_ref, u_ref, w_ref, du_ref, dw_ref, db_ref, ext = refs
        j = pl.program_id(1)

        @pl.when(j == 0)
        def _():
            ext[r:r + 8, :] = jnp.zeros((8, tc), f32)
            dw_ref[...] = jnp.zeros_like(dw_ref)
            db_ref[...] = jnp.zeros_like(db_ref)

        @pl.when(j > 0)
        def _():
            ext[r:r + 8, :] = ext[0:8, :]

        dpre = d_ref[...]
        if has_pre:
            dpre = dpre * _dsilu(p_ref[...])
        ext[0:r, :] = dpre
        wv = w_ref[...]

        def chunk(c, sums):
            r0 = pl.multiple_of(c * ch, ch)
            v = ext[pl.ds(r0, ch + 8), :]
            uv = u_ref[pl.ds(r0, ch), :]
            d0 = v[0:ch, :]
            du = wv[3:4, :] * d0
            new = [None] * 5
            new[3] = sums[3] + jnp.sum(d0 * uv, axis=0, keepdims=True)
            for s in (1, 2, 3):
                sh = pltpu.roll(v, ch + 8 - s, 0)[0:ch, :]
                du = du + wv[3 - s:4 - s, :] * sh
                new[3 - s] = sums[3 - s] + jnp.sum(sh * uv, axis=0, keepdims=True)
            new[4] = sums[4] + jnp.sum(d0, axis=0, keepdims=True)
            du_ref[pl.ds(r0, ch), :] = du.astype(bf16)
            return tuple(new)

        sums = lax.fori_loop(0, r // ch, chunk, tuple(jnp.zeros((1, tc), f32) for _ in range(5)))
        for k in range(4):
            dw_ref[k:k + 1, :] += sums[k]
        db_ref[...] += sums[4]

    rev = pl.BlockSpec((r, tc), lambda c, j: (nt - 1 - j, c))
    win = pl.BlockSpec((r, tc), lambda c, j: (nt - 1 - j, cb + c))
    in_specs = [ANY, rev] + ([rev] if has_pre else []) + [win, pl.BlockSpec((4, tc), lambda c, j: (0, c))]
    ins = [dst, dpost] + ([pre] if has_pre else []) + [src, w]
    return pl.pallas_call(
        body, name=name, grid=(width // tc, nt), in_specs=in_specs,
        out_specs=[win, pl.BlockSpec((4, tc), lambda c, j: (0, c)), pl.BlockSpec((1, tc), lambda c, j: (0, c))],
        out_shape=[jax.ShapeDtypeStruct(dst.shape, bf16), jax.ShapeDtypeStruct((4, width), f32),
                   jax.ShapeDtypeStruct((1, width), f32)],
        input_output_aliases={0: 0},
        scratch_shapes=[pltpu.VMEM((r + 8, tc), f32)],
        compiler_params=pltpu.CompilerParams(dimension_semantics=("parallel", "arbitrary")),
    )(*ins)


def _ssd_common(xbc_ref, dtr_ref, dtrT_ref, par_row_ref, par_col_ref):
    l = SSM_CHUNK
    x = xbc_ref[:, 0:SSM_GROUP_W]
    bm = xbc_ref[:, SSM_GROUP_W:SSM_GROUP_W + SSM_D_STATE]
    cm = xbc_ref[:, SSM_GROUP_W + SSM_D_STATE:XBC_GROUP_W]
    par_row = par_row_ref[0]
    par_col = par_col_ref[0]
    bias_row, alog_row, d_row = par_row[0:1, :], par_row[1:2, :], par_row[2:3, :]
    bias_col, alog_col = par_col[:, 0:1], par_col[:, 1:2]
    dtr = dtr_ref[0]
    dt = _softplus(dtr + bias_row)
    dt_t = _softplus(dtrT_ref[0] + bias_col)
    a_row = -jnp.exp(alog_row)
    a_col = -jnp.exp(alog_col)
    li = lax.broadcasted_iota(jnp.int32, (l, l), 0)
    si = lax.broadcasted_iota(jnp.int32, (l, l), 1)
    tri = (li >= si).astype(f32)
    cs = _dot_01(tri, dt * a_row, NN, 1, 3)
    cs_t = _dot_01(dt_t * a_col, tri, NT, 0, 3)
    off = lax.broadcasted_iota(jnp.int32, (SSM_HPG, SSM_GROUP_W), 1) - SSM_HEAD_DIM * lax.broadcasted_iota(
        jnp.int32, (SSM_HPG, SSM_GROUP_W), 0)
    ex = ((off >= 0) & (off < SSM_HEAD_DIM)).astype(f32)
    cs_x = _dot_01(cs, ex, NN, 0, 3)
    cl_x = cs_x[l - 1:l, :]
    return dict(x=x, bm=bm, cm=cm, dtr=dtr, dt=dt, a_row=a_row, bias_row=bias_row, tri=tri, li=li, si=si, cs=cs,
                cs_t=cs_t, ex=ex, dt_x=_dot_01(dt, ex, NN, 0, 2), d_x=_dot_01(par_row, ex, NN, 0, 2)[2:3, :], e_x=jnp.exp(cs_x),
                el_x=jnp.exp(cl_x), dec_x=jnp.exp(cl_x - cs_x))


def ssd_fwd(xbc, dtr, dtr_t, par_row, par_col, *, name):
    t = xbc.shape[0]
    nc = t // SSM_CHUNK
    l, p = SSM_CHUNK, SSM_HEAD_DIM

    def body(xbc_ref, dtr_ref, dtrT_ref, prow_ref, pcol_ref, y_ref, sin_ref, state):
        @pl.when(pl.program_id(1) == 0)
        def _():
            state[...] = jnp.zeros_like(state)

        q = _ssd_common(xbc_ref, dtr_ref, dtrT_ref, prow_ref, pcol_ref)
        st = state[...]
        sin_ref[0] = st
        xd = q["x"] * q["dt_x"]
        g = _dot(q["cm"], q["bm"], NT)
        for r in range(SSM_HPG):
            sl = slice(r * p, (r + 1) * p)
            diff = q["cs"][:, r:r + 1] - q["cs_t"][r:r + 1, :]
            lm = jnp.where(q["li"] >= q["si"], jnp.exp(jnp.minimum(diff, 0.0)), 0.0)
            y_ref[:, sl] = _dot(g * lm, xd[:, sl], NN)
        y_ref[...] += q["e_x"] * _dot(q["cm"], st, NN) + q["d_x"] * q["x"]
        state[...] = q["el_x"] * st + _dot(q["bm"].T, xd * q["dec_x"], NN)

    return pl.pallas_call(
        body, name=name, grid=(SSM_GROUPS, nc),
        in_specs=[pl.BlockSpec((l, XBC_GROUP_W), lambda g, c: (c, g)),
                  pl.BlockSpec((1, l, SSM_HPG), lambda g, c: (g, c, 0)),
                  pl.BlockSpec((1, SSM_HPG, l), lambda g, c: (g, 0, c)),
                  pl.BlockSpec((1, 8, 8), lambda g, c: (g, 0, 0)),
                  pl.BlockSpec((1, 8, 8), lambda g, c: (g, 0, 0))],
        out_specs=[pl.BlockSpec((l, SSM_GROUP_W), lambda g, c: (c, g)),
                   pl.BlockSpec((1, SSM_D_STATE, SSM_GROUP_W), lambda g, c: (c, 0, g))],
        out_shape=[jax.ShapeDtypeStruct((t, SSM_D_INNER), f32),
                   jax.ShapeDtypeStruct((nc, SSM_D_STATE, SSM_D_INNER), f32)],
        scratch_shapes=[pltpu.VMEM((SSM_D_STATE, SSM_GROUP_W), f32)],
        compiler_params=pltpu.CompilerParams(dimension_semantics=("parallel", "arbitrary")),
    )(xbc, dtr, dtr_t, par_row, par_col)


def ssd_bwd(xbc, dtr, dtr_t, par_row, par_col, s_in, dy, *, name):
    t = xbc.shape[0]
    nc = t // SSM_CHUNK
    l, p = SSM_CHUNK, SSM_HEAD_DIM

    def body(xbc_ref, dtr_ref, dtrT_ref, prow_ref, pcol_ref, sin_ref, dy_ref, dxbc_ref, ddtr_ref, dpar_ref,
             dstate, yd_buf, dxd_buf):
        @pl.when(pl.program_id(1) == 0)
        def _():
            dstate[...] = jnp.zeros_like(dstate)
            dpar_ref[...] = jnp.zeros_like(dpar_ref)

        q = _ssd_common(xbc_ref, dtr_ref, dtrT_ref, prow_ref, pcol_ref)
        x, bm, cm, ex, li, si = q["x"], q["bm"], q["cm"], q["ex"], q["li"], q["si"]
        e_x, el_x, dec_x = q["e_x"], q["el_x"], q["dec_x"]
        st = sin_ref[0]
        dst = dstate[...]
        dy = dy_ref[...]
        xd = x * q["dt_x"]
        g = _dot(cm, bm, NT)
        dg = jnp.zeros((l, l), f32)
        for r in range(SSM_HPG):
            sl = slice(r * p, (r + 1) * p)
            diff = q["cs"][:, r:r + 1] - q["cs_t"][r:r + 1, :]
            lm = jnp.where(li >= si, jnp.exp(jnp.minimum(diff, 0.0)), 0.0)
            m = (g * lm).astype(bf16)
            xdh, dyh = xd[:, sl].astype(bf16), dy[:, sl].astype(bf16)
            yd_buf[:, sl] = _dot(m, xdh, NN)
            dxd_buf[:, sl] = _dot(m, dyh, TN)
            dg = dg + _dot(dyh, xdh, NT) * lm
        yd, dxd_diag = yd_buf[...], dxd_buf[...]
        yo = e_x * _dot(cm, st, NN)
        dz = e_x * dy
        wv = _dot(bm, dst, NN)
        xw = xd * wv * dec_x
        row8 = lax.broadcasted_iota(jnp.int32, (l, SSM_HPG), 0)
        dy_b, xd_b = dy.astype(bf16).astype(f32), xd.astype(bf16).astype(f32)
        dcs = _dot_01(dy_b * yd - xd_b * dxd_diag + dy * yo - xw, ex, NT, 0, 3)
        tail = jnp.sum(xw, axis=0, keepdims=True) + el_x * jnp.sum(dst * st, axis=0, keepdims=True)
        dcl = _dot_01(jnp.broadcast_to(tail, (SSM_HPG, SSM_GROUP_W)), ex, NT, 0, 3)[0:1, :]
        dcs = dcs + jnp.where(row8 == l - 1, dcl, 0.0)
        dda = _dot_01(q["tri"], dcs, TN, 1, 3)
        dxd = dxd_diag + dec_x * wv
        ddt = _dot_01(dxd * x, ex, NT, 0, 3) + dda * q["a_row"]
        ddtr = ddt * _sigmoid(q["dtr"] + q["bias_row"])
        ddtr_ref[0] = ddtr
        dd = _dot_01(jnp.broadcast_to(jnp.sum(dy * x, axis=0, keepdims=True), (SSM_HPG, SSM_GROUP_W)), ex, NT, 0, 2)[0:1, :]
        dpar_ref[0, 0:1, :] += jnp.sum(ddtr, axis=0, keepdims=True)
        dpar_ref[0, 1:2, :] += jnp.sum(dda * q["dt"], axis=0, keepdims=True) * q["a_row"]
        dpar_ref[0, 2:3, :] += dd
        dxbc_ref[:, 0:SSM_GROUP_W] = dxd * q["dt_x"] + q["d_x"] * dy
        dxbc_ref[:, SSM_GROUP_W:SSM_GROUP_W + SSM_D_STATE] = _dot(dg, cm, TN) + _dot(xd * dec_x, dst, NT)
        dxbc_ref[:, SSM_GROUP_W + SSM_D_STATE:XBC_GROUP_W] = _dot(dg, bm, NN) + _dot(dz, st, NT)
        dstate[...] = _dot(cm.T, dz, NN) + el_x * dst

    rc = lambda c: nc - 1 - c
    return pl.pallas_call(
        body, name=name, grid=(SSM_GROUPS, nc),
        in_specs=[pl.BlockSpec((l, XBC_GROUP_W), lambda g, c: (rc(c), g)),
                  pl.BlockSpec((1, l, SSM_HPG), lambda g, c: (g, rc(c), 0)),
                  pl.BlockSpec((1, SSM_HPG, l), lambda g, c: (g, 0, rc(c))),
                  pl.BlockSpec((1, 8, 8), lambda g, c: (g, 0, 0)),
                  pl.BlockSpec((1, 8, 8), lambda g, c: (g, 0, 0)),
                  pl.BlockSpec((1, SSM_D_STATE, SSM_GROUP_W), lambda g, c: (rc(c), 0, g)),
                  pl.BlockSpec((l, SSM_GROUP_W), lambda g, c: (rc(c), g))],
        out_specs=[pl.BlockSpec((l, XBC_GROUP_W), lambda g, c: (rc(c), g)),
                   pl.BlockSpec((1, l, SSM_HPG), lambda g, c: (g, rc(c), 0)),
                   pl.BlockSpec((1, 8, 8), lambda g, c: (g, 0, 0))],
        out_shape=[jax.ShapeDtypeStruct((t, SSM_CONV_DIM), f32),
                   jax.ShapeDtypeStruct((SSM_GROUPS, t, SSM_HPG), f32),
                   jax.ShapeDtypeStruct((SSM_GROUPS, 8, 8), f32)],
        scratch_shapes=[pltpu.VMEM((SSM_D_STATE, SSM_GROUP_W), f32), pltpu.VMEM((l, SSM_GROUP_W), f32),
                        pltpu.VMEM((l, SSM_GROUP_W), f32)],
        compiler_params=pltpu.CompilerParams(dimension_semantics=("parallel", "arbitrary")),
    )(xbc, dtr, dtr_t, par_row, par_col, s_in, dy)


def gnorm_fwd(y, proj, w, *, name):
    t = y.shape[0]
    tr = _pick(t, 512, 8)
    gw = SSM_GROUP_W
    zb = OFF_Z // gw

    def body(y_ref, z_ref, w_ref, o_ref):
        y2 = y_ref[...] * _silu(z_ref[...])
        r = lax.rsqrt(jnp.mean(y2 * y2, axis=-1, keepdims=True) + RMS_EPS)
        o_ref[...] = (y2 * r * w_ref[...]).astype(bf16)

    return pl.pallas_call(
        body, name=name, grid=(SSM_GROUPS, t // tr),
        in_specs=[pl.BlockSpec((tr, gw), lambda g, i: (i, g)), pl.BlockSpec((tr, gw), lambda g, i: (i, zb + g)),
                  pl.BlockSpec((1, gw), lambda g, i: (0, g))],
        out_specs=pl.BlockSpec((tr, gw), lambda g, i: (i, g)), out_shape=jax.ShapeDtypeStruct((t, SSM_D_INNER), bf16),
    )(y, proj, w)


def gnorm_bwd(y, proj, w, dout, dst, *, name):
    t = y.shape[0]
    tr = _pick(t, 512, 8)
    gw = SSM_GROUP_W
    zb = OFF_Z // gw

    def body(_, y_ref, z_ref, w_ref, do_ref, dy_ref, dz_ref, dw_ref):
        yv, zv = y_ref[...], z_ref[...]
        sz = _silu(zv)
        y2 = yv * sz
        dy2, dw = _rms_bwd_math(y2, w_ref[...], do_ref[...].astype(f32))
        dy_ref[...] = dy2 * sz
        dz_ref[...] = (dy2 * yv * _dsilu(zv)).astype(bf16)

        @pl.when(pl.program_id(1) == 0)
        def _():
            dw_ref[...] = jnp.zeros_like(dw_ref)

        dw_ref[...] += dw

    tile = pl.BlockSpec((tr, gw), lambda g, i: (i, g))
    vec = pl.BlockSpec((1, gw), lambda g, i: (0, g))
    return pl.pallas_call(
        body, name=name, grid=(SSM_GROUPS, t // tr),
        in_specs=[ANY, tile, pl.BlockSpec((tr, gw), lambda g, i: (i, zb + g)), vec, tile],
        out_specs=[tile, pl.BlockSpec((tr, gw), lambda g, i: (i, zb + g)), vec],
        out_shape=[jax.ShapeDtypeStruct((t, SSM_D_INNER), f32), jax.ShapeDtypeStruct(dst.shape, bf16),
                   jax.ShapeDtypeStruct((1, SSM_D_INNER), f32)],
        input_output_aliases={0: 1},
        compiler_params=pltpu.CompilerParams(dimension_semantics=("parallel", "arbitrary")),
    )(dst, y, proj, w, dout)


LRU_ROWS = 256


def _lru_gates(uv, wr_ref, wi_ref, br_ref, bi_ref, lam_ref):
    rg = _sigmoid(_dot(uv, wr_ref[0], NN) + br_ref[...])
    ig = _sigmoid(_dot(uv, wi_ref[0], NN) + bi_ref[...])
    sp = _softplus(-lam_ref[...])
    la = -LRU_C * rg * sp
    a = jnp.exp(la)
    s = jnp.sqrt(jnp.maximum(-_expm1(2.0 * la), 0.0))
    return rg, ig, sp, la, a, s


def lru_fwd(u, proj, w_r, b_r, w_i, b_i, lam, *, name):
    t = u.shape[0]
    r = LRU_ROWS
    lb = LRU_BLOCK
    yb = OFF_LY // lb

    def body(u_ref, y_ref, wr_ref, br_ref, wi_ref, bi_ref, lam_ref, h_ref, o_ref, carry):
        @pl.when(pl.program_id(1) == 0)
        def _():
            carry[...] = jnp.zeros_like(carry)

        uv = u_ref[...]
        _, ig, _, _, a, s = _lru_gates(uv, wr_ref, wi_ref, br_ref, bi_ref, lam_ref)
        b = s * ig * uv
        row = lax.broadcasted_iota(jnp.int32, (r, lb), 0)
        d = 1
        while d < r:
            keep = row >= d
            b = b + a * jnp.where(keep, pltpu.roll(b, d, 0), 0.0)
            a = a * jnp.where(keep, pltpu.roll(a, d, 0), 1.0)
            d *= 2
        h = b + a * carry[0:1, :]
        carry[0:1, :] = h[r - 1:r, :]
        h_ref[...] = h
        o_ref[...] = (h * _gelu(y_ref[...])).astype(bf16)

    tile = pl.BlockSpec((r, lb), lambda hb, j: (j, hb))
    vec = pl.BlockSpec((1, lb), lambda hb, j: (0, hb))
    wsp = pl.BlockSpec((1, lb, lb), lambda hb, j: (hb, 0, 0))
    return pl.pallas_call(
        body, name=name, grid=(LRU_BLOCKS, t // r),
        in_specs=[tile, pl.BlockSpec((r, lb), lambda hb, j: (j, yb + hb)), wsp, vec, wsp, vec, vec],
        out_specs=[tile, tile],
        out_shape=[jax.ShapeDtypeStruct((t, LRU_WIDTH), f32), jax.ShapeDtypeStruct((t, LRU_WIDTH), bf16)],
        scratch_shapes=[pltpu.VMEM((8, lb), f32)],
        compiler_params=pltpu.CompilerParams(dimension_semantics=("parallel", "arbitrary")),
    )(u, proj, w_r, b_r, w_i, b_i, lam)


def lru_bwd(u, proj, hseq, dout, w_r, b_r, w_i, b_i, lam, dst, *, name):
    t = u.shape[0]
    r = LRU_ROWS
    nt = t // r
    lb = LRU_BLOCK
    yb = OFF_LY // lb

    def body(_, u_ref, y_ref, h_ref, hp_ref, do_ref, wr_ref, br_ref, wi_ref, bi_ref, lam_ref,
             du_ref, dy_ref, dwr_ref, dwi_ref, dbr_ref, dbi_ref, dlam_ref, carry_dh, carry_a):
        j = pl.program_id(1)

        @pl.when(j == 0)
        def _():
            carry_dh[...] = jnp.zeros_like(carry_dh)
            carry_a[...] = jnp.zeros_like(carry_a)
            dwr_ref[...] = jnp.zeros_like(dwr_ref)
            dwi_ref[...] = jnp.zeros_like(dwi_ref)
            dbr_ref[...] = jnp.zeros_like(dbr_ref)
            dbi_ref[...] = jnp.zeros_like(dbi_ref)
            dlam_ref[...] = jnp.zeros_like(dlam_ref)

        uv = u_ref[...]
        yv = y_ref[...]
        hv = h_ref[...]
        dov = do_ref[...]
        rg, ig, sp, la, a, s = _lru_gates(uv, wr_ref, wi_ref, br_ref, bi_ref, lam_ref)
        dy_ref[...] = (dov * hv * _dgelu(yv)).astype(bf16)
        gq = dov * _gelu(yv)
        row = lax.broadcasted_iota(jnp.int32, (r, lb), 0)
        an = jnp.where(row < r - 1, pltpu.roll(a, r - 1, 0), carry_a[0:1, :])
        d = 1
        while d < r:
            keep = row < r - d
            gq = gq + an * jnp.where(keep, pltpu.roll(gq, r - d, 0), 0.0)
            an = an * jnp.where(keep, pltpu.roll(an, r - d, 0), 1.0)
            d *= 2
        dh = gq + an * carry_dh[0:1, :]
        carry_dh[0:1, :] = dh[0:1, :]
        carry_a[0:1, :] = a[0:1, :]
        first = jnp.where(j == nt - 1, 0.0, 1.0) * hp_ref[7:8, :]
        hprev = jnp.where(row >= 1, pltpu.roll(hv, 1, 0), first)
        da = dh * hprev
        iu = ig * uv
        e2 = jnp.exp(2.0 * la)
        dla = da * a - dh * iu * e2 / jnp.maximum(s, 1e-30)
        drp = dla * (-LRU_C * sp) * rg * (1.0 - rg)
        dip = dh * s * uv * ig * (1.0 - ig)
        dlam_ref[...] += jnp.sum(dla * (LRU_C * rg) * _sigmoid(-lam_ref[...]), axis=0, keepdims=True)
        du_ref[...] = dh * s * ig + _dot(drp, wr_ref[0], NT) + _dot(dip, wi_ref[0], NT)
        dwr_ref[0] += _dot(uv, drp, TN)
        dwi_ref[0] += _dot(uv, dip, TN)
        dbr_ref[...] += jnp.sum(drp, axis=0, keepdims=True)
        dbi_ref[...] += jnp.sum(dip, axis=0, keepdims=True)

    rj = lambda j: nt - 1 - j
    tile = pl.BlockSpec((r, lb), lambda hb, j: (rj(j), hb))
    vec = pl.BlockSpec((1, lb), lambda hb, j: (0, hb))
    wsp = pl.BlockSpec((1, lb, lb), lambda hb, j: (hb, 0, 0))
    hprev_spec = pl.BlockSpec((8, lb), lambda hb, j: (jnp.maximum(rj(j) * (r // 8) - 1, 0), hb))
    ywin = pl.BlockSpec((r, lb), lambda hb, j: (rj(j), yb + hb))
    return pl.pallas_call(
        body, name=name, grid=(LRU_BLOCKS, nt),
        in_specs=[ANY, tile, ywin, tile, hprev_spec, tile, wsp, vec, wsp, vec, vec],
        out_specs=[tile, ywin, wsp, wsp, vec, vec, vec],
        out_shape=[jax.ShapeDtypeStruct((t, LRU_WIDTH), f32), jax.ShapeDtypeStruct(dst.shape, bf16),
                   jax.ShapeDtypeStruct((LRU_BLOCKS, lb, lb), f32), jax.ShapeDtypeStruct((LRU_BLOCKS, lb, lb), f32),
                   jax.ShapeDtypeStruct((1, LRU_WIDTH), f32), jax.ShapeDtypeStruct((1, LRU_WIDTH), f32),
                   jax.ShapeDtypeStruct((1, LRU_WIDTH), f32)],
        input_output_aliases={0: 1},
        scratch_shapes=[pltpu.VMEM((8, lb), f32), pltpu.VMEM((8, lb), f32)],
        compiler_params=pltpu.CompilerParams(dimension_semantics=("parallel", "arbitrary")),
    )(dst, u, proj, hseq, hseq, dout, w_r, b_r, w_i, b_i, lam)


def merge_fwd(proj, bg, y_ssm, y_lru, *, name):
    t, d = y_ssm.shape
    tr = _pick(t, 256, 8)
    gb = OFF_GATES // d

    def body(gs_ref, gl_ref, bs_ref, bl_ref, ys_ref, yl_ref, o_ref):
        gs = _sigmoid(gs_ref[...] + bs_ref[...])
        gl = _sigmoid(gl_ref[...] + bl_ref[...])
        o_ref[...] = (gs * ys_ref[...].astype(f32) + gl * yl_ref[...].astype(f32)).astype(bf16)

    row = pl.BlockSpec((tr, d), lambda i: (i, 0))
    return pl.pallas_call(
        body, name=name, grid=(t // tr,),
        in_specs=[pl.BlockSpec((tr, d), lambda i: (i, gb)), pl.BlockSpec((tr, d), lambda i: (i, gb + 1)),
                  pl.BlockSpec((1, d), lambda i: (0, 0)), pl.BlockSpec((1, d), lambda i: (0, 1)), row, row],
        out_specs=row, out_shape=jax.ShapeDtypeStruct((t, d), bf16),
    )(proj, proj, bg, bg, y_ssm, y_lru)


def merge_bwd(proj, bg, y_ssm, y_lru, dmix, *, name):
    t, d = y_ssm.shape
    tr = _pick(t, 256, 8)
    gb = OFF_GATES // d

    def body(gs_ref, gl_ref, bs_ref, bl_ref, ys_ref, yl_ref, dm_ref, dg_ref, dys_ref, dyl_ref, dbg_ref):
        gs = _sigmoid(gs_ref[...] + bs_ref[...])
        gl = _sigmoid(gl_ref[...] + bl_ref[...])
        dm = dm_ref[...].astype(f32)
        dys_ref[...] = (dm * gs).astype(bf16)
        dyl_ref[...] = (dm * gl).astype(bf16)
        dgs = dm * ys_ref[...].astype(f32) * gs * (1.0 - gs)
        dgl = dm * yl_ref[...].astype(f32) * gl * (1.0 - gl)
        dg_ref[:, 0:d] = dgs.astype(bf16)
        dg_ref[:, d:2 * d] = dgl.astype(bf16)

        @pl.when(pl.program_id(0) == 0)
        def _():
            dbg_ref[...] = jnp.zeros_like(dbg_ref)

        dbg_ref[:, 0:d] += jnp.sum(dgs, axis=0, keepdims=True)
        dbg_ref[:, d:2 * d] += jnp.sum(dgl, axis=0, keepdims=True)

    row = pl.BlockSpec((tr, d), lambda i: (i, 0))
    return pl.pallas_call(
        body, name=name, grid=(t // tr,),
        in_specs=[pl.BlockSpec((tr, d), lambda i: (i, gb)), pl.BlockSpec((tr, d), lambda i: (i, gb + 1)),
                  pl.BlockSpec((1, d), lambda i: (0, 0)), pl.BlockSpec((1, d), lambda i: (0, 1)), row, row, row],
        out_specs=[pl.BlockSpec((tr, 2 * d), lambda i: (i, OFF_GATES // (2 * d))), row, row,
                   pl.BlockSpec((1, 2 * d), lambda i: (0, 0))],
        out_shape=[jax.ShapeDtypeStruct((t, PROJ_W), bf16), jax.ShapeDtypeStruct((t, d), bf16),
                   jax.ShapeDtypeStruct((t, d), bf16), jax.ShapeDtypeStruct((1, 2 * d), f32)],
        compiler_params=pltpu.CompilerParams(dimension_semantics=("arbitrary",)),
    )(proj, proj, bg, bg, y_ssm, y_lru, dmix)


def swiglu_fwd(ff, *, name):
    t = ff.shape[0]
    hd = FFN_HIDDEN
    tr = _pick(t, 128, 8)

    def body(f_ref, o_ref):
        o_ref[...] = (_silu(f_ref[:, 0:hd].astype(f32)) * f_ref[:, hd:2 * hd].astype(f32)).astype(bf16)

    return pl.pallas_call(
        body, name=name, grid=(t // tr,), in_specs=[pl.BlockSpec((tr, 2 * hd), lambda i: (i, 0))],
        out_specs=pl.BlockSpec((tr, hd), lambda i: (i, 0)), out_shape=jax.ShapeDtypeStruct((t, hd), bf16),
    )(ff)


def swiglu_bwd(ff, dact, *, name):
    t = ff.shape[0]
    hd = FFN_HIDDEN
    tr = _pick(t, 128, 8)

    def body(f_ref, d_ref, o_ref):
        gate, up, dv = f_ref[:, 0:hd].astype(f32), f_ref[:, hd:2 * hd].astype(f32), d_ref[...].astype(f32)
        o_ref[:, 0:hd] = (dv * up * _dsilu(gate)).astype(bf16)
        o_ref[:, hd:2 * hd] = (dv * _silu(gate)).astype(bf16)

    return pl.pallas_call(
        body, name=name, grid=(t // tr,),
        in_specs=[pl.BlockSpec((tr, 2 * hd), lambda i: (i, 0)), pl.BlockSpec((tr, hd), lambda i: (i, 0))],
        out_specs=pl.BlockSpec((tr, 2 * hd), lambda i: (i, 0)), out_shape=jax.ShapeDtypeStruct((t, 2 * hd), bf16),
    )(ff, dact)


def _adam_math(w, g, m, v):
    m = ADAM_B1 * m + (1.0 - ADAM_B1) * g
    v = ADAM_B2 * v + (1.0 - ADAM_B2) * (g * g)
    m_hat = m / (1.0 - ADAM_B1 ** ADAM_STEP)
    v_hat = v / (1.0 - ADAM_B2 ** ADAM_STEP)
    delta = -ADAM_LR * (m_hat / (jnp.sqrt(v_hat) + ADAM_EPS) + ADAM_WD * w)
    return delta, m, v


def _row_tile(rows, cols):
    cap = max(8, (1 << 18) // cols)
    return _pick(rows, cap, 8) if rows % 8 == 0 else rows


def adamw(w, g, m, v, *, name):
    rows, cols = w.shape
    tr = _row_tile(rows, cols)

    def body(w_ref, g_ref, m_ref, v_ref, d_ref, nm_ref, nv_ref):
        d, nm, nv = _adam_math(w_ref[...], g_ref[...], m_ref[...], v_ref[...])
        d_ref[...] = d
        nm_ref[...] = nm
        nv_ref[...] = nv

    tile = pl.BlockSpec((tr, cols), lambda i: (i, 0))
    return pl.pallas_call(
        body, name=name, grid=(rows // tr,), in_specs=[tile] * 4, out_specs=[tile] * 3,
        out_shape=[jax.ShapeDtypeStruct((rows, cols), f32)] * 3,
    )(w, g, m, v)


def adamw_many(ws, gs, ms, vs, *, name):
    n = len(ws)

    def body(*refs):
        for i in range(n):
            d, nm, nv = _adam_math(refs[i][...], refs[n + i][...], refs[2 * n + i][...], refs[3 * n + i][...])
            refs[4 * n + 3 * i][...] = d
            refs[4 * n + 3 * i + 1][...] = nm
            refs[4 * n + 3 * i + 2][...] = nv

    outs = pl.pallas_call(
        body, name=name, out_shape=[jax.ShapeDtypeStruct(w.shape, f32) for w in ws for _ in range(3)],
    )(*ws, *gs, *ms, *vs)
    return [tuple(outs[3 * i:3 * i + 3]) for i in range(n)]


def pair_add(dw, rbuf, idx, *, name):
    n, rows, cols = dw.shape
    hr = rows // 2
    tr = _row_tile(hr, cols)
    nrt = hr // tr

    def body(idx_ref, a_ref, b_ref, o_ref, own_ref):
        s = a_ref[...] + b_ref[...]
        o_ref[...] = s.astype(bf16)

        @pl.when(pl.program_id(1) == idx_ref[0])
        def _():
            own_ref[...] = s[0]

    return pl.pallas_call(
        body, name=name,
        grid_spec=pltpu.PrefetchScalarGridSpec(
            num_scalar_prefetch=1, grid=(nrt, n),
            in_specs=[pl.BlockSpec((1, tr, cols), lambda i, k, idx: (k, idx[1] * nrt + i, 0)),
                      pl.BlockSpec((1, tr, cols), lambda i, k, idx: (k, i, 0))],
            out_specs=[pl.BlockSpec((1, tr, cols), lambda i, k, idx: (k, i, 0)),
                       pl.BlockSpec((tr, cols), lambda i, k, idx: (i, 0))]),
        out_shape=[jax.ShapeDtypeStruct((n, hr, cols), bf16), jax.ShapeDtypeStruct((hr, cols), f32)],
    )(idx, dw, rbuf)


def chip_sum(own, rbuf, idx, *, name):
    hr, cols = own.shape
    tr = _row_tile(hr, cols)
    nrt = hr // tr

    def body(idx_ref, a_ref, b_ref, o_ref):
        o_ref[...] = ((a_ref[...] + b_ref[0].astype(f32)) + b_ref[1].astype(f32)) + b_ref[2].astype(f32)

    return pl.pallas_call(
        body, name=name,
        grid_spec=pltpu.PrefetchScalarGridSpec(
            num_scalar_prefetch=1, grid=(nrt,),
            in_specs=[pl.BlockSpec((tr, cols), lambda i, idx: (i, 0)),
                      pl.BlockSpec((3, tr, cols), lambda i, idx: (0, i, 0))],
            out_specs=pl.BlockSpec((tr, cols), lambda i, idx: (idx[1] * nrt + i, 0))),
        out_shape=jax.ShapeDtypeStruct((2 * hr, cols), f32),
    )(idx, own, rbuf)


def sum8(rbuf, *, name):
    n, rows, cols = rbuf.shape
    tr = _row_tile(rows, cols * n)

    def body(a_ref, o_ref):
        acc = a_ref[0]
        for k in range(1, n):
            acc = acc + a_ref[k]
        o_ref[...] = acc

    return pl.pallas_call(
        body, name=name, grid=(rows // tr,), in_specs=[pl.BlockSpec((n, tr, cols), lambda i: (0, i, 0))],
        out_specs=pl.BlockSpec((tr, cols), lambda i: (i, 0)), out_shape=jax.ShapeDtypeStruct((rows, cols), f32),
    )(rbuf)


def _coords():
    return lax.axis_index("x"), lax.axis_index("y"), lax.axis_index("c")


def _other_chips(x, y):
    return [(1 - x, y), (x, 1 - y), (1 - x, 1 - y)]


def gather_weights(shards, *, name):
    n = len(shards)
    halves = [s.shape[0] // 2 for s in shards]

    def body(*refs):
        ins, outs = refs[:n], refs[n:2 * n]
        send1, recv1, send2, recv2 = refs[2 * n:]
        x, y, c = _coords()
        me = 2 * x + y
        chips = _other_chips(x, y)
        sibling = (x, y, 1 - c)

        def half(i, k, hc):
            return outs[i].at[k, pl.ds(hc * halves[i], halves[i]), :]

        def ici(i, j):
            return pltpu.make_async_remote_copy(
                src_ref=ins[i].at[pl.ds(c * halves[i], halves[i]), :], dst_ref=half(i, me, c),
                send_sem=send1.at[i, j], recv_sem=recv1.at[i, j], device_id=(*chips[j], c), device_id_type=MESH)

        def landed(i, j):
            kj = 2 * chips[j][0] + chips[j][1]
            return pltpu.make_async_remote_copy(
                src_ref=half(i, kj, c), dst_ref=half(i, kj, c),
                send_sem=send2.at[i, j], recv_sem=recv1.at[i, j], device_id=sibling, device_id_type=MESH)

        def from_sibling(i, j):
            kj = 2 * chips[j][0] + chips[j][1]
            return pltpu.make_async_remote_copy(
                src_ref=half(i, kj, 1 - c), dst_ref=half(i, kj, 1 - c),
                send_sem=send2.at[i, j], recv_sem=recv2.at[i, j], device_id=sibling, device_id_type=MESH)

        def d2d(i, j):
            kj = 2 * chips[j][0] + chips[j][1]
            return pltpu.make_async_remote_copy(
                src_ref=half(i, kj, c), dst_ref=half(i, kj, c),
                send_sem=send2.at[i, j], recv_sem=recv2.at[i, j], device_id=sibling, device_id_type=MESH)

        for j in range(3):
            for i in range(n):
                ici(i, j).start()
        for j in range(3):
            for i in range(n):
                landed(i, j).wait_recv()
                d2d(i, j).start()
        for j in range(3):
            for i in range(n):
                from_sibling(i, j).wait_recv()
        for j in range(3):
            for i in range(n):
                ici(i, j).wait_send()
                d2d(i, j).wait_send()

    return pl.pallas_call(
        body, name=name, in_specs=[ANY] * n, out_specs=[ANY] * n,
        out_shape=[jax.ShapeDtypeStruct((N_CHIPS,) + s.shape, s.dtype) for s in shards],
        scratch_shapes=[pltpu.SemaphoreType.DMA((n, 3))] * 4,
    )(*shards)


def pair_exchange(grads, *, name):
    n = len(grads)
    halves = [g.shape[1] // 2 for g in grads]

    def body(*refs):
        ins, outs = refs[:n], refs[n:2 * n]
        send, recv = refs[2 * n:]
        x, y, c = _coords()
        cps = [pltpu.make_async_remote_copy(
            src_ref=ins[i].at[:, pl.ds((1 - c) * halves[i], halves[i]), :], dst_ref=outs[i],
            send_sem=send.at[i], recv_sem=recv.at[i], device_id=(x, y, 1 - c), device_id_type=MESH) for i in range(n)]
        for cp in cps:
            cp.start()
        for cp in cps:
            cp.wait()

    return pl.pallas_call(
        body, name=name, in_specs=[ANY] * n, out_specs=[ANY] * n,
        out_shape=[jax.ShapeDtypeStruct((N_CHIPS, g.shape[1] // 2, g.shape[2]), g.dtype) for g in grads],
        scratch_shapes=[pltpu.SemaphoreType.DMA((n,))] * 2,
    )(*grads)


def pair_gather(bufs, *, name):
    n = len(bufs)

    def body(*refs):
        ins, outs = refs[:n], refs[n:2 * n]
        send, recv = refs[2 * n:]
        x, y, c = _coords()
        cps = []
        for i in range(n):
            hr = ins[i].shape[0] // 2
            cps.append(pltpu.make_async_remote_copy(
                src_ref=ins[i].at[pl.ds(c * hr, hr), :], dst_ref=outs[i].at[pl.ds(c * hr, hr), :],
                send_sem=send.at[i], recv_sem=recv.at[i], device_id=(x, y, 1 - c), device_id_type=MESH))
        for cp in cps:
            cp.start()
        for i in range(n):
            hr = ins[i].shape[0] // 2
            pltpu.make_async_remote_copy(
                src_ref=ins[i].at[pl.ds((1 - c) * hr, hr), :], dst_ref=outs[i].at[pl.ds((1 - c) * hr, hr), :],
                send_sem=send.at[i], recv_sem=recv.at[i], device_id=(x, y, 1 - c), device_id_type=MESH).wait_recv()
        for cp in cps:
            cp.wait_send()

    return pl.pallas_call(
        body, name=name, in_specs=[ANY] * n, out_specs=[ANY] * n,
        out_shape=[jax.ShapeDtypeStruct(b.shape, b.dtype) for b in bufs],
        input_output_aliases={i: i for i in range(n)},
        scratch_shapes=[pltpu.SemaphoreType.DMA((n,))] * 2,
    )(*bufs)


def all_exchange(buf, *, name):
    rows, cols = buf.shape

    def body(in_ref, out_ref, send, recv):
        x, y, c = _coords()
        me = 4 * x + 2 * y + c
        cps = []
        for d in range(1, 8):
            px = 1 - x if d & 4 else x
            py = 1 - y if d & 2 else y
            pc = 1 - c if d & 1 else c
            cps.append(pltpu.make_async_remote_copy(
                src_ref=in_ref, dst_ref=out_ref.at[me], send_sem=send.at[d - 1], recv_sem=recv.at[d - 1],
                device_id=(px, py, pc), device_id_type=MESH))
        for cp in cps:
            cp.start()
        for d in range(1, 8):
            px = 1 - x if d & 4 else x
            py = 1 - y if d & 2 else y
            pc = 1 - c if d & 1 else c
            src = 4 * px + 2 * py + pc
            pltpu.make_async_remote_copy(
                src_ref=in_ref, dst_ref=out_ref.at[src], send_sem=send.at[d - 1], recv_sem=recv.at[d - 1],
                device_id=(px, py, pc), device_id_type=MESH).wait_recv()
        for cp in cps:
            cp.wait_send()

    return pl.pallas_call(
        body, name=name, in_specs=[ANY], out_specs=ANY,
        out_shape=jax.ShapeDtypeStruct((8, rows, cols), buf.dtype),
        scratch_shapes=[pltpu.SemaphoreType.DMA((7,)), pltpu.SemaphoreType.DMA((7,))],
    )(buf)


HBM = pl.BlockSpec(memory_space=pltpu.HBM)
SEM = pl.BlockSpec(memory_space=pltpu.SEMAPHORE)
EFFECT = pltpu.SideEffectType.DATAFLOW_SIDE_EFFECTING


def split_start(arrays, after, copies, sem_shape, *, name):
    na = len(arrays)

    def body(*refs):
        for cp in copies(refs[:na], refs[na + 1], refs[na + 2]):
            cp.start()
        refs[-1][...] = jnp.zeros((8, 128), f32)

    outs = pl.pallas_call(
        body, name=name,
        out_shape=(pltpu.SemaphoreType.DMA(sem_shape), pltpu.SemaphoreType.DMA(sem_shape),
                   *[pltpu.HBM(a.shape, a.dtype) for a in arrays], jax.ShapeDtypeStruct((8, 128), f32)),
        in_specs=[HBM] * na + [ANY], out_specs=(SEM, SEM, *[HBM] * na, pl.BlockSpec(memory_space=pltpu.VMEM)),
        input_output_aliases={i: 2 + i for i in range(na)},
        compiler_params=pltpu.CompilerParams(has_side_effects=EFFECT),
    )(*[pltpu.with_memory_space_constraint(a, pltpu.HBM) for a in arrays], after)
    return outs[0], outs[1], list(outs[2:2 + na]), outs[-1]


def split_wait(send, recv, arrays, after, copies, *, name):
    na = len(arrays)

    def body(*refs):
        for cp in copies(refs[:na], refs[na], refs[na + 1]):
            cp.wait_send()
            cp.wait_recv()

    outs = pl.pallas_call(
        body, name=name, out_shape=tuple(pltpu.HBM(a.shape, a.dtype) for a in arrays),
        in_specs=[HBM] * na + [SEM, SEM, ANY], out_specs=tuple([HBM] * na),
        input_output_aliases={i: i for i in range(na)},
        compiler_params=pltpu.CompilerParams(has_side_effects=EFFECT),
    )(*arrays, send, recv, after)
    return list(outs)


def gather_copies(n):
    def copies(refs, send, recv):
        x, y, c = _coords()
        me = 2 * x + y
        chips = _other_chips(x, y)
        return [pltpu.make_async_remote_copy(
            src_ref=refs[i], dst_ref=refs[n + i].at[me], send_sem=send.at[3 * i + j], recv_sem=recv.at[3 * i + j],
            device_id=(*chips[j], c), device_id_type=MESH) for j in range(3) for i in range(n)]
    return copies


def pair_copies(n):
    def copies(refs, send, recv):
        x, y, c = _coords()
        cps = []
        for i in range(n):
            hr = refs[i].shape[1] // 2
            cps.append(pltpu.make_async_remote_copy(
                src_ref=refs[i].at[:, pl.ds((1 - c) * hr, hr), :], dst_ref=refs[n + i], send_sem=send.at[i],
                recv_sem=recv.at[i], device_id=(x, y, 1 - c), device_id_type=MESH))
        return cps
    return copies


def all_copies():
    def copies(refs, send, recv):
        x, y, c = _coords()
        me = 4 * x + 2 * y + c
        cps = []
        for d in range(1, 8):
            peer = (1 - x if d & 4 else x, 1 - y if d & 2 else y, 1 - c if d & 1 else c)
            cps.append(pltpu.make_async_remote_copy(
                src_ref=refs[0], dst_ref=refs[1].at[me], send_sem=send.at[d - 1], recv_sem=recv.at[d - 1],
                device_id=peer, device_id_type=MESH))
        return cps
    return copies


def reduce_copies(n):
    def copies(refs, send, recv):
        x, y, c = _coords()
        chips = _other_chips(x, y)
        return [pltpu.make_async_remote_copy(
            src_ref=refs[i].at[2 * chips[j][0] + chips[j][1]], dst_ref=refs[n + i].at[j],
            send_sem=send.at[3 * i + j], recv_sem=recv.at[3 * i + j], device_id=(*chips[j], c), device_id_type=MESH)
            for j in range(3) for i in range(n)]
    return copies


def _pack(arrs):
    flat = []
    for a in arrs:
        v = a.reshape(-1).astype(f32)
        pad = (-v.shape[0]) % 128
        flat.append(jnp.pad(v, (0, pad)) if pad else v)
    v = jnp.concatenate(flat)
    rows = v.shape[0] // 128
    pad_rows = (-rows) % 256
    v = v.reshape(rows, 128)
    return jnp.pad(v, ((0, pad_rows), (0, 0))) if pad_rows else v


def _unpack(buf, shapes):
    out, row = [], 0
    for s in shapes:
        size = math.prod(s)
        rows = -(-size // 128)
        out.append(buf[row:row + rows].reshape(-1)[:size].reshape(s))
        row += rows
    return out


def _ref_of_perm():
    ref = np.arange(IN_PROJ_DIM)
    xbc = ref[4096:7168]
    xbc_p = [np.concatenate([xbc[g * 512:(g + 1) * 512], xbc[2048 + g * 128:2048 + (g + 1) * 128],
                             xbc[2560 + g * 128:2560 + (g + 1) * 128]]) for g in range(SSM_GROUPS)]
    return np.concatenate([ref[0:2048], ref[2048:4096], ref[7200:8480], ref[8480:9760], ref[7168:7200],
                           -np.ones(DT_PAD_W - SSM_HEADS, np.int64)] + xbc_p)


def _runs(vals):
    out, start = [], 0
    for i in range(1, len(vals) + 1):
        if i == len(vals) or not (vals[i] == vals[i - 1] + 1 or (vals[i] < 0 and vals[i - 1] < 0)):
            out.append((start, int(vals[start]), i - start))
            start = i
    return out


def _perm_in_from_shards(g):
    ref_of_perm = _ref_of_perm()
    sw = IN_PROJ_DIM // N_CHIPS
    parts = []
    for _, first, length in _runs(ref_of_perm):
        if first < 0:
            parts.append(jnp.zeros((g.shape[1], length), g.dtype))
            continue
        lo = first
        while lo < first + length:
            k = lo // sw
            hi = min(first + length, (k + 1) * sw)
            parts.append(g[k, :, lo - k * sw:hi - k * sw])
            lo = hi
    return jnp.concatenate(parts, axis=-1)


def _unperm_in_to_shards(w):
    ref_of_perm = _ref_of_perm()
    perm_of_ref = np.zeros(IN_PROJ_DIM, np.int64)
    perm_of_ref[ref_of_perm[ref_of_perm >= 0]] = np.nonzero(ref_of_perm >= 0)[0]
    sw = IN_PROJ_DIM // N_CHIPS
    shards = []
    for k in range(N_CHIPS):
        runs = _runs(perm_of_ref[k * sw:(k + 1) * sw])
        shards.append(jnp.concatenate([w[:, first:first + length] for _, first, length in runs], axis=-1))
    return jnp.stack(shards)


def _perm_xbc_cols(w):
    parts = []
    for g in range(SSM_GROUPS):
        parts += [w[..., g * 512:(g + 1) * 512], w[..., 2048 + g * 128:2048 + (g + 1) * 128],
                  w[..., 2560 + g * 128:2560 + (g + 1) * 128]]
    return jnp.concatenate(parts, axis=-1)


def _unperm_xbc_cols(w):
    xs = [w[..., g * XBC_GROUP_W:g * XBC_GROUP_W + 512] for g in range(SSM_GROUPS)]
    bs = [w[..., g * XBC_GROUP_W + 512:g * XBC_GROUP_W + 640] for g in range(SSM_GROUPS)]
    cs = [w[..., g * XBC_GROUP_W + 640:(g + 1) * XBC_GROUP_W] for g in range(SSM_GROUPS)]
    return jnp.concatenate(xs + bs + cs, axis=-1)


def _from_col_shards(w):
    n, r, c = w.shape
    return jnp.transpose(w, (1, 0, 2)).reshape(r, n * c)


def kernel(x, norm1_w, w_in, b_branch_gate, ssm_conv_w, ssm_conv_b, ssm_dt_bias, ssm_a_log, ssm_d, ssm_norm_w, w_out_ssm, lru_conv_w, lru_conv_b, lru_w_r, lru_b_r, lru_w_i, lru_b_i, lru_lambda, w_out_lru, w_out, norm2_w, w_ffn_in, w_ffn_out, norm_f_w, loss_target, m_norm1_w, m_w_in, m_b_branch_gate, m_ssm_conv_w, m_ssm_conv_b, m_ssm_dt_bias, m_ssm_a_log, m_ssm_d, m_ssm_norm_w, m_w_out_ssm, m_lru_conv_w, m_lru_conv_b, m_lru_w_r, m_lru_b_r, m_lru_w_i, m_lru_b_i, m_lru_lambda, m_w_out_lru, m_w_out, m_norm2_w, m_w_ffn_in, m_w_ffn_out, m_norm_f_w, v_norm1_w, v_w_in, v_b_branch_gate, v_ssm_conv_w, v_ssm_conv_b, v_ssm_dt_bias, v_ssm_a_log, v_ssm_d, v_ssm_norm_w, v_w_out_ssm, v_lru_conv_w, v_lru_conv_b, v_lru_w_r, v_lru_b_r, v_lru_w_i, v_lru_b_i, v_lru_lambda, v_w_out_lru, v_w_out, v_norm2_w, v_w_ffn_in, v_w_ffn_out, v_norm_f_w):
    xi, yi, ci = lax.axis_index("x"), lax.axis_index("y"), lax.axis_index("c")
    me = 2 * xi + yi
    idx = jnp.stack([me, ci]).astype(jnp.int32)
    x2 = x[0]
    tgt = loss_target[0]

    big_names = ["w_in", "w_out_ssm", "w_out_lru", "w_out", "w_ffn_in", "w_ffn_out"]
    big_w = dict(w_in=w_in[0], w_out_ssm=w_out_ssm[0], w_out_lru=w_out_lru[0], w_out=w_out[0], w_ffn_in=w_ffn_in[0],
                 w_ffn_out=w_ffn_out[0])
    big_m = dict(w_in=m_w_in[0], w_out_ssm=m_w_out_ssm[0], w_out_lru=m_w_out_lru[0], w_out=m_w_out[0],
                 w_ffn_in=m_w_ffn_in[0], w_ffn_out=m_w_ffn_out[0])
    big_v = dict(w_in=v_w_in[0], w_out_ssm=v_w_out_ssm[0], w_out_lru=v_w_out_lru[0], w_out=v_w_out[0],
                 w_ffn_in=v_w_ffn_in[0], w_ffn_out=v_w_ffn_out[0])
    conv_pad = jnp.zeros((16, 768), f32).at[0:4, :].set(ssm_conv_w[0]).at[8:12, 0:320].set(lru_conv_w[0])
    mine = [big_w["w_in"].astype(bf16), conv_pad]
    gathered = gather_weights(mine, name="gather_weights")
    g_in, g_conv = [lax.dynamic_update_index_in_dim(g, s, me, 0) for g, s in zip(gathered, mine)]
    w_in_p = _perm_in_from_shards(g_in)
    late_names = big_names[1:]
    late = [big_w[k].astype(bf16) for k in late_names]
    late_lands = [lax.empty((N_CHIPS,) + s.shape, bf16) for s in late]
    g_send, g_recv, g_arrays, g_token = split_start(late + late_lands, g_conv, gather_copies(5), (15,),
                                                    name="gather_late_start")
    ssm_cw_full = _from_col_shards(g_conv[:, 0:4, :])
    lru_cw_full = _from_col_shards(g_conv[:, 8:12, 0:320])
    ssm_cw_p = _perm_xbc_cols(ssm_cw_full)
    ssm_cb_p = _perm_xbc_cols(ssm_conv_b)

    par = jnp.stack([ssm_dt_bias[0], ssm_a_log[0], ssm_d[0]], axis=0).reshape(3, SSM_GROUPS, SSM_HPG)
    par_row = jnp.zeros((SSM_GROUPS, 8, 8), f32).at[:, 0:3, :].set(jnp.transpose(par, (1, 0, 2)))
    par_col = jnp.transpose(par_row, (0, 2, 1))

    hn1 = rms_fwd(x2, norm1_w + g_token[0:1, 0:1], name="rms1_fwd")
    proj = mm(hn1, w_in_p, "nn", name="in_proj")
    t = x2.shape[0]
    dtr = jnp.transpose(proj[:, OFF_DT:OFF_DT + 32].reshape(t, SSM_GROUPS, SSM_HPG), (1, 0, 2))
    dtr_t = jnp.transpose(dtr, (0, 2, 1))
    xbc_pre, xbc_post = conv_fwd(proj, OFF_XBC, SSM_CONV_DIM, ssm_cw_p, ssm_cb_p, silu=True, name="ssm_conv_fwd")
    y_ssd, s_in = ssd_fwd(xbc_post, dtr, dtr_t, par_row, par_col, name="ssd_fwd")
    yn = gnorm_fwd(y_ssd, proj, ssm_norm_w, name="gnorm_fwd")
    g_arrays = split_wait(g_send, g_recv, g_arrays, yn, gather_copies(5), name="gather_late_wait")
    g_out_ssm, g_out_lru, g_out, g_ffn_in, g_ffn_out = [
        lax.dynamic_update_index_in_dim(g, s, me, 0) for g, s in zip(g_arrays[5:], late)]
    w_out_ssm_f = g_out_ssm.reshape(SSM_D_INNER, D_MODEL)
    w_out_lru_f = g_out_lru.reshape(LRU_WIDTH, D_MODEL)
    w_out_f = g_out.reshape(D_MODEL, D_MODEL)
    w_ffn_out_f = g_ffn_out.reshape(FFN_HIDDEN, D_MODEL)
    y_ssm = mm(yn, w_out_ssm_f, "nn", out_dtype=bf16, name="out_ssm")
    (u_lru,) = conv_fwd(proj, OFF_LX, LRU_WIDTH, lru_cw_full, lru_conv_b, silu=False, name="lru_conv_fwd")
    h_lru, o_lru = lru_fwd(u_lru, proj, lru_w_r[0], lru_b_r, lru_w_i[0], lru_b_i, lru_lambda, name="lru_fwd")
    y_lru = mm(o_lru, w_out_lru_f, "nn", out_dtype=bf16, name="out_lru")
    mix = merge_fwd(proj, b_branch_gate, y_ssm, y_lru, name="merge_fwd")
    h1 = mm(mix, w_out_f, "nn", add=x2, name="out_proj")
    hn2 = rms_fwd(h1, norm2_w, name="rms2_fwd")
    ff = mm(hn2, g_ffn_in, "nn", b_shards=True, out_dtype=bf16, name="ffn_in")
    act = swiglu_fwd(ff, name="swiglu_fwd")
    h2 = mm(act, w_ffn_out_f, "nn", add=h1, name="ffn_out")
    loss_tile, dh2, d_norm_f = loss_head(h2, norm_f_w.reshape(1, D_MODEL), tgt, name="loss_head")
    loss = lax.psum(loss_tile[0, 0], ("x", "y", "c"))

    d_w_ffn_out = mm(act, dh2, "tn", name="d_w_ffn_out")
    dact = mm(dh2, w_ffn_out_f, "nt", out_dtype=bf16, name="d_act")
    dff = swiglu_bwd(ff, dact, name="swiglu_bwd")
    d_w_ffn_in = mm(hn2, dff, "tn", out_shards=N_CHIPS, name="d_w_ffn_in")
    dhn2 = mm(dff, g_ffn_in, "nt", b_shards=True, name="d_hn2")
    dh1, d_norm2 = rms_bwd(h1, norm2_w, dhn2, dh2, name="rms2_bwd")
    d_w_out = mm(mix, dh1, "tn", name="d_w_out")
    dmix = mm(dh1, w_out_f, "nt", out_dtype=bf16, name="d_mix")
    dproj, dy_ssm, dy_lru, d_bg = merge_bwd(proj, b_branch_gate, y_ssm, y_lru, dmix, name="merge_bwd")
    d_w_out_ssm = mm(yn, dy_ssm, "tn", name="d_w_out_ssm")
    d_w_out_lru = mm(o_lru, dy_lru, "tn", name="d_w_out_lru")
    early_g = [d_w_out_ssm.reshape(N_CHIPS, 512, D_MODEL), d_w_out_lru.reshape(N_CHIPS, 320, D_MODEL),
               d_w_out.reshape(N_CHIPS, 256, D_MODEL), d_w_ffn_in, d_w_ffn_out.reshape(N_CHIPS, 704, D_MODEL)]
    p_lands = [lax.empty((N_CHIPS, g.shape[1] // 2, g.shape[2]), f32) for g in early_g]
    p_send, p_recv, p_arrays, p_token = split_start(early_g + p_lands, early_g[0], pair_copies(5), (5,),
                                                    name="pair_early_start")
    dyn = mm(dy_ssm, w_out_ssm_f, "nt", out_dtype=bf16, after=p_token, name="d_yn")
    dy_ssd, dproj, d_ssm_norm = gnorm_bwd(y_ssd, proj, ssm_norm_w, dyn, dproj, name="gnorm_bwd")
    p_arrays = split_wait(p_send, p_recv, p_arrays, dy_ssd, pair_copies(5), name="pair_early_wait")
    e_pairs = [pair_add(g, rb, idx, name="pair_add_" + k) for g, rb, k in zip(p_arrays[:5], p_arrays[5:], late_names)]
    e_lands = [lax.empty((3,) + p[0].shape[1:], bf16) for p in e_pairs]
    e_send, e_recv, e_arrays, e_token = split_start([p[0] for p in e_pairs] + e_lands, e_pairs[0][1], reduce_copies(5),
                                                    (15,), name="reduce_early_start")
    dxbc_post, ddtr, dpar = ssd_bwd(xbc_post, dtr, dtr_t, par_row + e_token[0:1, 0:1], par_col, s_in, dy_ssd,
                                    name="ssd_bwd")
    dproj, d_ssm_cw_p, d_ssm_cb_p = conv_bwd(dxbc_post, xbc_pre, proj, OFF_XBC, ssm_cw_p, dproj, name="ssm_conv_bwd")
    do_lru = mm(dy_lru, w_out_lru_f, "nt", name="d_o_lru")
    du_lru, dproj, d_w_r, d_w_i, d_b_r, d_b_i, d_lam = lru_bwd(u_lru, proj, h_lru, do_lru, lru_w_r[0], lru_b_r, lru_w_i[0],
                                                               lru_b_i, lru_lambda, dproj, name="lru_bwd")
    dproj, d_lru_cw, d_lru_cb = conv_bwd(du_lru, None, proj, OFF_LX, lru_cw_full, dproj, name="lru_conv_bwd")
    ddt_cols = jnp.transpose(ddtr, (1, 0, 2)).reshape(t, SSM_HEADS).astype(bf16)
    ddt_cols = jnp.pad(ddt_cols, ((0, 0), (0, DT_PAD_W - SSM_HEADS)))
    dproj = lax.dynamic_update_slice(dproj, ddt_cols, (0, OFF_DT))

    d_ssm_cw = _unperm_xbc_cols(d_ssm_cw_p)
    d_ssm_cb = _unperm_xbc_cols(d_ssm_cb_p)
    dpar_h = jnp.transpose(dpar[:, 0:3, :], (1, 0, 2)).reshape(3, SSM_HEADS)
    small_names = ["norm1_w", "b_branch_gate", "ssm_conv_b", "ssm_dt_bias", "ssm_a_log", "ssm_d", "ssm_norm_w",
                   "lru_conv_b", "lru_w_r", "lru_b_r", "lru_w_i", "lru_b_i", "lru_lambda", "norm2_w", "norm_f_w"]
    small_g = dict(norm1_w=jnp.zeros_like(norm1_w), b_branch_gate=d_bg, ssm_conv_b=d_ssm_cb, ssm_dt_bias=dpar_h[0:1], ssm_a_log=dpar_h[1:2],
                   ssm_d=dpar_h[2:3], ssm_norm_w=d_ssm_norm, lru_conv_b=d_lru_cb, lru_w_r=d_w_r[None], lru_b_r=d_b_r,
                   lru_w_i=d_w_i[None], lru_b_i=d_b_i, lru_lambda=d_lam, norm2_w=d_norm2, norm_f_w=d_norm_f.reshape(D_MODEL))
    small_w = dict(norm1_w=norm1_w, b_branch_gate=b_branch_gate, ssm_conv_b=ssm_conv_b, ssm_dt_bias=ssm_dt_bias,
                   ssm_a_log=ssm_a_log, ssm_d=ssm_d, ssm_norm_w=ssm_norm_w, lru_conv_b=lru_conv_b, lru_w_r=lru_w_r,
                   lru_b_r=lru_b_r, lru_w_i=lru_w_i, lru_b_i=lru_b_i, lru_lambda=lru_lambda, norm2_w=norm2_w, norm_f_w=norm_f_w)
    small_m = dict(norm1_w=m_norm1_w, b_branch_gate=m_b_branch_gate, ssm_conv_b=m_ssm_conv_b, ssm_dt_bias=m_ssm_dt_bias,
                   ssm_a_log=m_ssm_a_log, ssm_d=m_ssm_d, ssm_norm_w=m_ssm_norm_w, lru_conv_b=m_lru_conv_b, lru_w_r=m_lru_w_r,
                   lru_b_r=m_lru_b_r, lru_w_i=m_lru_w_i, lru_b_i=m_lru_b_i, lru_lambda=m_lru_lambda, norm2_w=m_norm2_w,
                   norm_f_w=m_norm_f_w)
    small_v = dict(norm1_w=v_norm1_w, b_branch_gate=v_b_branch_gate, ssm_conv_b=v_ssm_conv_b, ssm_dt_bias=v_ssm_dt_bias,
                   ssm_a_log=v_ssm_a_log, ssm_d=v_ssm_d, ssm_norm_w=v_ssm_norm_w, lru_conv_b=v_lru_conv_b, lru_w_r=v_lru_w_r,
                   lru_b_r=v_lru_b_r, lru_w_i=v_lru_w_i, lru_b_i=v_lru_b_i, lru_lambda=v_lru_lambda, norm2_w=v_norm2_w,
                   norm_f_w=v_norm_f_w)
    shapes = [small_w[k].shape for k in small_names]
    conv_shapes = [(4, SSM_CONV_DIM), (4, LRU_WIDTH)]
    g_pack = _pack([small_g[k] for k in small_names] + [d_ssm_cw, d_lru_cw])
    s_send, s_recv, s_arrays, s_token = split_start([g_pack, lax.empty((8,) + g_pack.shape, f32)], g_pack, all_copies(),
                                                    (7,), name="small_start")
    d_w_in_p = mm(hn1, dproj, "tn", after=s_token, name="d_w_in")

    d_w_in_s = _unperm_in_to_shards(d_w_in_p)
    (l_sib,) = pair_exchange([d_w_in_s], name="pair_exchange_late")
    l_pair = pair_add(d_w_in_s, l_sib, idx, name="pair_add_w_in")
    l_land = lax.empty((3,) + l_pair[0].shape[1:], bf16)
    l_send, l_recv, l_arrays, l_token = split_start([l_pair[0], l_land], l_pair[1], reduce_copies(1), (3,),
                                                    name="reduce_late_start")
    dhn1 = mm(dproj, w_in_p, "nt", after=l_token, name="d_hn1")
    grad_x, d_norm1 = rms_bwd(x2, norm1_w, dhn1, dh1, name="rms1_bwd")

    e_arrays = split_wait(e_send, e_recv, e_arrays, d_norm1, reduce_copies(5), name="reduce_early_wait")
    e_half = [chip_sum(p[1], rb, idx, name="chip_sum_" + k) for p, rb, k in zip(e_pairs, e_arrays[5:], late_names)]
    big_out = {}
    for k, g in zip(late_names, pair_gather(e_half, name="pair_gather_early")):
        big_out[k] = (g,) + tuple(adamw(big_w[k], g, big_m[k], big_v[k], name="adamw_" + k))

    s_arrays = split_wait(s_send, s_recv, s_arrays, d_norm1, all_copies(), name="small_wait")
    g_sum = sum8(lax.dynamic_update_index_in_dim(s_arrays[1], g_pack, 2 * me + ci, 0), name="sum8")
    n1 = d_norm1.reshape(8, 128)
    n1_sum = sum8(lax.dynamic_update_index_in_dim(all_exchange(n1, name="all_exchange_norm1"), n1, 2 * me + ci, 0),
                  name="sum8_norm1")
    g_sum = lax.dynamic_update_slice(g_sum, n1_sum, (0, 0))
    g_small = _unpack(g_sum, shapes + conv_shapes)
    g_small[-2] = lax.dynamic_slice_in_dim(g_small[-2], me * 768, 768, axis=1)
    g_small[-1] = lax.dynamic_slice_in_dim(g_small[-1], me * 320, 320, axis=1)
    all_names = small_names + ["ssm_conv_w", "lru_conv_w"]
    small_w.update(ssm_conv_w=ssm_conv_w[0], lru_conv_w=lru_conv_w[0])
    small_m.update(ssm_conv_w=m_ssm_conv_w[0], lru_conv_w=m_lru_conv_w[0])
    small_v.update(ssm_conv_w=v_ssm_conv_w[0], lru_conv_w=v_lru_conv_w[0])
    as2d = lambda a: a.reshape(-1, a.shape[-1])
    upd = adamw_many([as2d(small_w[k]) for k in all_names], [as2d(g) for g in g_small],
                     [as2d(small_m[k]) for k in all_names], [as2d(small_v[k]) for k in all_names], name="adamw_small")
    small_out = {}
    for k, g, u in zip(all_names, g_small, upd):
        small_out[k] = (g,) + tuple(o.reshape(g.shape) for o in u)
    l_arrays = split_wait(l_send, l_recv, l_arrays, upd[0][0], reduce_copies(1), name="reduce_late_wait")
    l_half = chip_sum(l_pair[1], l_arrays[1], idx, name="chip_sum_w_in")
    (g_w_in,) = pair_gather([l_half], name="pair_gather_late")
    big_out["w_in"] = (g_w_in,) + tuple(adamw(big_w["w_in"], g_w_in, big_m["w_in"], big_v["w_in"], name="adamw_w_in"))

    order = ["norm1_w", "w_in", "b_branch_gate", "ssm_conv_w", "ssm_conv_b", "ssm_dt_bias", "ssm_a_log", "ssm_d", "ssm_norm_w",
             "w_out_ssm", "lru_conv_w", "lru_conv_b", "lru_w_r", "lru_b_r", "lru_w_i", "lru_b_i", "lru_lambda", "w_out_lru",
             "w_out", "norm2_w", "w_ffn_in", "w_ffn_out", "norm_f_w"]
    outs = [loss, grad_x[None]]
    for which in range(4):
        for k in order:
            if k in big_out:
                outs.append(big_out[k][which][None])
            elif k in ("ssm_conv_w", "lru_conv_w"):
                outs.append(small_out[k][which][None])
            else:
                outs.append(small_out[k][which])
    return tuple(outs)
```

```python
import functools
import math

import jax
import jax.numpy as jnp
import numpy as np
from jax import lax
from jax.experimental import pallas as pl
from jax.experimental.pallas import tpu as pltpu

f32 = jnp.float32
bf16 = jnp.bfloat16

D_MODEL = 1024
SSM_D_INNER = 2048
SSM_HEADS = 32
SSM_HEAD_DIM = 64
SSM_GROUPS = 4
SSM_HPG = 8
SSM_D_STATE = 128
SSM_CHUNK = 128
SSM_GROUP_W = 512
SSM_CONV_DIM = 3072
XBC_GROUP_W = 768
LRU_WIDTH = 1280
LRU_BLOCKS = 10
LRU_BLOCK = 128
LRU_C = 8.0
FFN_HIDDEN = 2816
RMS_EPS = 1e-6
IN_PROJ_DIM = 9760
N_CHIPS = 4

OFF_GATES = 0
OFF_Z = 2048
OFF_LX = 4096
OFF_LY = 5376
OFF_DT = 6656
DT_PAD_W = 256
OFF_XBC = 6912
PROJ_W = 9984

ADAM_LR = 0.001
ADAM_B1 = 0.9
ADAM_B2 = 0.999
ADAM_EPS = 1e-08
ADAM_WD = 0.01
ADAM_STEP = 10

MESH = pl.DeviceIdType.MESH
ANY = pl.BlockSpec(memory_space=pl.ANY)

NN = (((1,), (0,)), ((), ()))
NT = (((1,), (1,)), ((), ()))
TN = (((0,), (0,)), ((), ()))


def _pick(n, cap, mult=128):
    best = None
    for t in range(mult, min(n, cap) + 1, mult):
        if n % t == 0:
            best = t
    return best if best is not None else n


def _sigmoid(x):
    return 0.5 * jnp.tanh(0.5 * x) + 0.5


def _softplus(x):
    return jnp.maximum(x, 0.0) + jnp.log(1.0 + jnp.exp(-jnp.abs(x)))


def _silu(x):
    return x * _sigmoid(x)


def _dsilu(x):
    s = _sigmoid(x)
    return s * (1.0 + x * (1.0 - s))


_GELU_K = math.sqrt(2.0 / math.pi)


def _gelu(x):
    return 0.5 * x * (1.0 + jnp.tanh(_GELU_K * (x + 0.044715 * x * x * x)))


def _dgelu(x):
    t = jnp.tanh(_GELU_K * (x + 0.044715 * x * x * x))
    return 0.5 * (1.0 + t) + 0.5 * x * (1.0 - t * t) * _GELU_K * (1.0 + 3.0 * 0.044715 * x * x)


def _expm1(x):
    poly = x * (1.0 + x * (0.5 + x * (1.0 / 6.0 + x * (1.0 / 24.0 + x * (1.0 / 120.0 + x * (1.0 / 720.0))))))
    return jnp.where(jnp.abs(x) < 0.1, poly, jnp.exp(x) - 1.0)


def _dot(a, b, dn):
    return lax.dot_general(a.astype(bf16), b.astype(bf16), dn, preferred_element_type=f32)


def _dot_01(a, b, dn, split, terms):
    r = a if split == 0 else b
    out = None
    for _ in range(terms):
        h = r.astype(bf16)
        r = r - h.astype(f32)
        d = lax.dot_general(h if split == 0 else a.astype(bf16), b.astype(bf16) if split == 0 else h, dn,
                            preferred_element_type=f32)
        out = d if out is None else out + d
    return out


MM_VMEM_BUDGET = 40 * 2 ** 20

def mm(a, b, mode, *, name, add=None, after=None, out_dtype=f32, b_shards=False, out_shards=0):
    bs = b.shape[1:] if b_shards else b.shape
    shard_w = b.shape[2] if b_shards else None
    bcols = bs[1] * (b.shape[0] if b_shards else 1)
    if mode == "nn":
        (m, k), (k2, n) = a.shape, (bs[0], bcols)
    elif mode == "nt":
        (m, k), (n, k2) = a.shape, (bs[0], bcols)
    else:
        (k, m), (k2, n) = a.shape, b.shape
    assert k == k2, (a.shape, b.shape, mode)
    tn = _pick(n, 1536)
    if b_shards and mode == "nn":
        tn = shard_w
    if out_shards:
        tn = n // out_shards
    tks = [shard_w] if (b_shards and mode == "nt") else sorted({k, _pick(k, 3328), _pick(k, 2048), _pick(k, 1024)}, reverse=True)
    isz = lambda v: jnp.dtype(v.dtype).itemsize

    def vmem_of(tm, tk):
        blocks = tm * tk * isz(a) + tk * tn * isz(b) + tm * tn * (4 * int(add is not None) + jnp.dtype(out_dtype).itemsize)
        return 2 * blocks + 4 * tm * tn * int(k > tk)

    fits = [(tk, tm) for tk in tks for tm in (_pick(m, 1536), _pick(m, 1024), _pick(m, 512)) if vmem_of(tm, tk) <= MM_VMEM_BUDGET]
    tk, tm = fits[0] if fits else (tks[-1], _pick(m, 256))
    nk = k // tk
    dn = {"nn": NN, "nt": NT, "tn": TN}[mode]
    a_spec = pl.BlockSpec((tk, tm), lambda i, j, kk: (kk, i)) if mode == "tn" else pl.BlockSpec((tm, tk), lambda i, j, kk: (i, kk))
    b_spec = pl.BlockSpec((tn, tk), lambda i, j, kk: (j, kk)) if mode == "nt" else pl.BlockSpec((tk, tn), lambda i, j, kk: (kk, j))
    if b_shards:
        b_spec = (pl.BlockSpec((None, tn, tk), lambda i, j, kk: (kk, j, 0)) if mode == "nt"
                  else pl.BlockSpec((None, tk, tn), lambda i, j, kk: (j, kk, 0)))
    o_spec = pl.BlockSpec((tm, tn), lambda i, j, kk: (i, j))
    out_shape = jax.ShapeDtypeStruct((m, n), out_dtype)
    if out_shards:
        assert add is None
        o_spec = pl.BlockSpec((None, tm, tn), lambda i, j, kk: (j, i, 0))
        out_shape = jax.ShapeDtypeStruct((out_shards, m, tn), out_dtype)
    has_add = add is not None

    n_extra = int(has_add) + int(after is not None)

    def body(a_ref, b_ref, *rest):
        add_ref = rest[0] if has_add else None
        o_ref = rest[n_extra]

        def finish(r):
            if has_add:
                r = r + add_ref[...]
            o_ref[...] = r.astype(out_dtype)

        if nk == 1:
            finish(_dot(a_ref[...], b_ref[...], dn))
            return
        acc = rest[-1]
        kk = pl.program_id(2)

        @pl.when(kk == 0)
        def _():
            acc[...] = jnp.zeros_like(acc)

        acc[...] += _dot(a_ref[...], b_ref[...], dn)

        @pl.when(kk == nk - 1)
        def _():
            finish(acc[...])

    ins = [a, b] + ([add] if has_add else []) + ([after] if after is not None else [])
    in_specs = [a_spec, b_spec] + ([o_spec] if has_add else []) + ([ANY] if after is not None else [])
    return pl.pallas_call(
        body, name=name, grid=(m // tm, n // tn, nk), in_specs=in_specs, out_specs=o_spec, out_shape=out_shape,
        scratch_shapes=[pltpu.VMEM((tm, tn), f32)] if nk > 1 else [],
        compiler_params=pltpu.CompilerParams(dimension_semantics=("parallel", "parallel", "arbitrary")),
    )(*ins)


def rms_fwd(x, w, *, name):
    t, d = x.shape
    tr = _pick(t, 256, 8)

    def body(x_ref, w_ref, o_ref):
        xv = x_ref[...]
        r = lax.rsqrt(jnp.mean(xv * xv, axis=-1, keepdims=True) + RMS_EPS)
        o_ref[...] = (xv * r * w_ref[...]).astype(bf16)

    return pl.pallas_call(
        body, name=name, grid=(t // tr,),
        in_specs=[pl.BlockSpec((tr, d), lambda i: (i, 0)), pl.BlockSpec((1, d), lambda i: (0, 0))],
        out_specs=pl.BlockSpec((tr, d), lambda i: (i, 0)), out_shape=jax.ShapeDtypeStruct((t, d), bf16),
    )(x, w)


def _rms_bwd_math(xv, wv, dy):
    r = lax.rsqrt(jnp.mean(xv * xv, axis=-1, keepdims=True) + RMS_EPS)
    g = dy * wv
    dx = r * g - xv * (r * r * r) * jnp.mean(g * xv, axis=-1, keepdims=True)
    dw = jnp.sum(dy * xv * r, axis=0, keepdims=True)
    return dx, dw


def rms_bwd(x, w, dy, res, *, name, with_bf16=False):
    t, d = x.shape
    tr = _pick(t, 256, 8)

    def body(x_ref, w_ref, dy_ref, res_ref, dx_ref, dw_ref, *more):
        dx, dw = _rms_bwd_math(x_ref[...], w_ref[...], dy_ref[...])
        dx = dx + res_ref[...]
        dx_ref[...] = dx
        if with_bf16:
            more[0][...] = dx.astype(bf16)

        @pl.when(pl.program_id(0) == 0)
        def _():
            dw_ref[...] = jnp.zeros_like(dw_ref)

        dw_ref[...] += dw

    row = pl.BlockSpec((tr, d), lambda i: (i, 0))
    vec = pl.BlockSpec((1, d), lambda i: (0, 0))
    return pl.pallas_call(
        body, name=name, grid=(t // tr,), in_specs=[row, vec, row, row], out_specs=[row, vec] + [row] * int(with_bf16),
        out_shape=[jax.ShapeDtypeStruct((t, d), f32), jax.ShapeDtypeStruct((1, d), f32)]
        + [jax.ShapeDtypeStruct((t, d), bf16)] * int(with_bf16),
        compiler_params=pltpu.CompilerParams(dimension_semantics=("arbitrary",)),
    )(x, w, dy, res)


def loss_head(h, w, target, *, name):
    t, d = h.shape
    tr = _pick(t, 256, 8)

    def body(h_ref, w_ref, t_ref, loss_ref, dh_ref, dw_ref, dhb_ref):
        xv, wv = h_ref[...], w_ref[...]
        r = lax.rsqrt(jnp.mean(xv * xv, axis=-1, keepdims=True) + RMS_EPS)
        err = xv * r * wv - t_ref[...]
        part = 0.5 * jnp.sum(jnp.mean(err * err, axis=-1, keepdims=True), axis=0, keepdims=True)
        dx, dw = _rms_bwd_math(xv, wv, err * (1.0 / d))
        dh_ref[...] = dx
        dhb_ref[...] = dx.astype(bf16)

        @pl.when(pl.program_id(0) == 0)
        def _():
            dw_ref[...] = jnp.zeros_like(dw_ref)
            loss_ref[...] = jnp.zeros_like(loss_ref)

        dw_ref[...] += dw
        loss_ref[...] += part

    row = pl.BlockSpec((tr, d), lambda i: (i, 0))
    vec = pl.BlockSpec((1, d), lambda i: (0, 0))
    return pl.pallas_call(
        body, name=name, grid=(t // tr,), in_specs=[row, vec, row],
        out_specs=[pl.BlockSpec((8, 128), lambda i: (0, 0)), row, vec, row],
        out_shape=[jax.ShapeDtypeStruct((8, 128), f32), jax.ShapeDtypeStruct((t, d), f32), jax.ShapeDtypeStruct((1, d), f32),
                   jax.ShapeDtypeStruct((t, d), bf16)],
        compiler_params=pltpu.CompilerParams(dimension_semantics=("arbitrary",)),
    )(h, w, target)


CONV_ROWS = 512
VREG_ELEMS = 8 * 128


def _conv_chunk(tc):
    return 16 if (16 + 8) * tc * 3 > 48 * VREG_ELEMS else 32


def conv_fwd(src, col0, width, w, b, *, silu, name):
    t = src.shape[0]
    tc = _pick(math.gcd(width, col0), 768)
    assert col0 % tc == 0
    cb = col0 // tc
    r = CONV_ROWS
    ch = _conv_chunk(tc)

    def body(u_ref, w_ref, b_ref, *rest):
        ext = rest[-1]
        j = pl.program_id(1)

        @pl.when(j == 0)
        def _():
            ext[0:8, :] = jnp.zeros((8, tc), f32)

        @pl.when(j > 0)
        def _():
            ext[0:8, :] = ext[r:r + 8, :]

        ext[8:r + 8, :] = u_ref[...]
        wv = w_ref[...]
        bv = b_ref[...]

        def chunk(c, carry):
            r0 = pl.multiple_of(c * ch, ch)
            v = ext[pl.ds(r0, ch + 8), :]
            acc = bv + wv[3:4, :] * v[8:, :]
            for s in (1, 2, 3):
                acc = acc + wv[3 - s:4 - s, :] * pltpu.roll(v, s, 0)[8:, :]
            rest[0][pl.ds(r0, ch), :] = acc
            if silu:
                rest[1][pl.ds(r0, ch), :] = _silu(acc)
            return carry

        lax.fori_loop(0, r // ch, chunk, 0)

    tile = pl.BlockSpec((r, tc), lambda c, j: (j, c))
    n_out = 2 if silu else 1
    return pl.pallas_call(
        body, name=name, grid=(width // tc, t // r),
        in_specs=[pl.BlockSpec((r, tc), lambda c, j: (j, cb + c)), pl.BlockSpec((4, tc), lambda c, j: (0, c)),
                  pl.BlockSpec((1, tc), lambda c, j: (0, c))],
        out_specs=[tile] * n_out, out_shape=[jax.ShapeDtypeStruct((t, width), f32)] * n_out,
        scratch_shapes=[pltpu.VMEM((r + 8, tc), f32)],
        compiler_params=pltpu.CompilerParams(dimension_semantics=("parallel", "arbitrary")),
    )(src, w, b)


def conv_bwd(dpost, pre, src, col0, w, dst, *, name):
    t, width = dpost.shape
    tc = _pick(math.gcd(width, col0), 768)
    assert col0 % tc == 0
    cb = col0 // tc
    r = CONV_ROWS
    ch = _conv_chunk(tc)
    nt = t // r
    has_pre = pre is not None

    def body(*refs):
        refs = refs[1:]
        if has_pre:
            d_ref, p_ref, u_ref, w_ref, du_ref, dw_ref, db_ref, ext = refs
        else:
            d_ref, u_ref, w_ref, du_ref, dw_ref, db_ref, ext = refs
        j = pl.program_id(1)

        @pl.when(j == 0)
        def _():
            ext[r:r + 8, :] = jnp.zeros((8, tc), f32)
            dw_ref[...] = jnp.zeros_like(dw_ref)
            db_ref[...] = jnp.zeros_like(db_ref)

        @pl.when(j > 0)
        def _():
            ext[r:r + 8, :] = ext[0:8, :]

        dpre = d_ref[...]
        if has_pre:
            dpre = dpre * _dsilu(p_ref[...])
        ext[0:r, :] = dpre
        wv = w_ref[...]

        def chunk(c, sums):
            r0 = pl.multiple_of(c * ch, ch)
            v = ext[pl.ds(r0, ch + 8), :]
            uv = u_ref[pl.ds(r0, ch), :]
            d0 = v[0:ch, :]
            du = wv[3:4, :] * d0
            new = [None] * 5
            new[3] = sums[3] + jnp.sum(d0 * uv, axis=0, keepdims=True)
            for s in (1, 2, 3):
                sh = pltpu.roll(v, ch + 8 - s, 0)[0:ch, :]
                du = du + wv[3 - s:4 - s, :] * sh
                new[3 - s] = sums[3 - s] + jnp.sum(sh * uv, axis=0, keepdims=True)
            new[4] = sums[4] + jnp.sum(d0, axis=0, keepdims=True)
            du_ref[pl.ds(r0, ch), :] = du.astype(bf16)
            return tuple(new)

        sums = lax.fori_loop(0, r // ch, chunk, tuple(jnp.zeros((1, tc), f32) for _ in range(5)))
        for k in range(4):
            dw_ref[k:k + 1, :] += sums[k]
        db_ref[...] += sums[4]

    rev = pl.BlockSpec((r, tc), lambda c, j: (nt - 1 - j, c))
    win = pl.BlockSpec((r, tc), lambda c, j: (nt - 1 - j, cb + c))
    in_specs = [ANY, rev] + ([rev] if has_pre else []) + [win, pl.BlockSpec((4, tc), lambda c, j: (0, c))]
    ins = [dst, dpost] + ([pre] if has_pre else []) + [src, w]
    return pl.pallas_call(
        body, name=name, grid=(width // tc, nt), in_specs=in_specs,
        out_specs=[win, pl.BlockSpec((4, tc), lambda c, j: (0, c)), pl.BlockSpec((1, tc), lambda c, j: (0, c))],
        out_shape=[jax.ShapeDtypeStruct(dst.shape, bf16), jax.ShapeDtypeStruct((4, width), f32),
                   jax.ShapeDtypeStruct((1, width), f32)],
        input_output_aliases={0: 0},
        scratch_shapes=[pltpu.VMEM((r + 8, tc), f32)],
        compiler_params=pltpu.CompilerParams(dimension_semantics=("parallel", "arbitrary")),
    )(*ins)


def _ssd_common(xbc_ref, dtr_ref, dtrT_ref, par_row_ref, par_col_ref):
    l = SSM_CHUNK
    x = xbc_ref[:, 0:SSM_GROUP_W]
    bm = xbc_ref[:, SSM_GROUP_W:SSM_GROUP_W + SSM_D_STATE]
    cm = xbc_ref[:, SSM_GROUP_W + SSM_D_STATE:XBC_GROUP_W]
    par_row = par_row_ref[0]
    par_col = par_col_ref[0]
    bias_row, alog_row, d_row = par_row[0:1, :], par_row[1:2, :], par_row[2:3, :]
    bias_col, alog_col = par_col[:, 0:1], par_col[:, 1:2]
    dtr = dtr_ref[0]
    dt = _softplus(dtr + bias_row)
    dt_t = _softplus(dtrT_ref[0] + bias_col)
    a_row = -jnp.exp(alog_row)
    a_col = -jnp.exp(alog_col)
    li = lax.broadcasted_iota(jnp.int32, (l, l), 0)
    si = lax.broadcasted_iota(jnp.int32, (l, l), 1)
    tri = (li >= si).astype(f32)
    cs = _dot_01(tri, dt * a_row, NN, 1, 3)
    cs_t = _dot_01(dt_t * a_col, tri, NT, 0, 3)
    off = lax.broadcasted_iota(jnp.int32, (SSM_HPG, SSM_GROUP_W), 1) - SSM_HEAD_DIM * lax.broadcasted_iota(
        jnp.int32, (SSM_HPG, SSM_GROUP_W), 0)
    ex = ((off >= 0) & (off < SSM_HEAD_DIM)).astype(f32)
    cs_x = _dot_01(cs, ex, NN, 0, 3)
    cl_x = cs_x[l - 1:l, :]
    return dict(x=x, bm=bm, cm=cm, dtr=dtr, dt=dt, a_row=a_row, bias_row=bias_row, tri=tri, li=li, si=si, cs=cs,
                cs_t=cs_t, ex=ex, dt_x=_dot_01(dt, ex, NN, 0, 2), d_x=_dot_01(par_row, ex, NN, 0, 2)[2:3, :], e_x=jnp.exp(cs_x),
                el_x=jnp.exp(cl_x), dec_x=jnp.exp(cl_x - cs_x))


def ssd_fwd(xbc, dtr, dtr_t, par_row, par_col, *, name):
    t = xbc.shape[0]
    nc = t // SSM_CHUNK
    l, p = SSM_CHUNK, SSM_HEAD_DIM

    def body(xbc_ref, dtr_ref, dtrT_ref, prow_ref, pcol_ref, y_ref, sin_ref, state):
        @pl.when(pl.program_id(1) == 0)
        def _():
            state[...] = jnp.zeros_like(state)

        q = _ssd_common(xbc_ref, dtr_ref, dtrT_ref, prow_ref, pcol_ref)
        st = state[...]
        sin_ref[0] = st
        xd = q["x"] * q["dt_x"]
        g = _dot(q["cm"], q["bm"], NT)
        for r in range(SSM_HPG):
            sl = slice(r * p, (r + 1) * p)
            diff = q["cs"][:, r:r + 1] - q["cs_t"][r:r + 1, :]
            lm = jnp.where(q["li"] >= q["si"], jnp.exp(jnp.minimum(diff, 0.0)), 0.0)
            y_ref[:, sl] = _dot(g * lm, xd[:, sl], NN)
        y_ref[...] += q["e_x"] * _dot(q["cm"], st, NN) + q["d_x"] * q["x"]
        state[...] = q["el_x"] * st + _dot(q["bm"].T, xd * q["dec_x"], NN)

    return pl.pallas_call(
        body, name=name, grid=(SSM_GROUPS, nc),
        in_specs=[pl.BlockSpec((l, XBC_GROUP_W), lambda g, c: (c, g)),
                  pl.BlockSpec((1, l, SSM_HPG), lambda g, c: (g, c, 0)),
                  pl.BlockSpec((1, SSM_HPG, l), lambda g, c: (g, 0, c)),
                  pl.BlockSpec((1, 8, 8), lambda g, c: (g, 0, 0)),
                  pl.BlockSpec((1, 8, 8), lambda g, c: (g, 0, 0))],
        out_specs=[pl.BlockSpec((l, SSM_GROUP_W), lambda g, c: (c, g)),
                   pl.BlockSpec((1, SSM_D_STATE, SSM_GROUP_W), lambda g, c: (c, 0, g))],
        out_shape=[jax.ShapeDtypeStruct((t, SSM_D_INNER), f32),
                   jax.ShapeDtypeStruct((nc, SSM_D_STATE, SSM_D_INNER), f32)],
        scratch_shapes=[pltpu.VMEM((SSM_D_STATE, SSM_GROUP_W), f32)],
        compiler_params=pltpu.CompilerParams(dimension_semantics=("parallel", "arbitrary")),
    )(xbc, dtr, dtr_t, par_row, par_col)


def ssd_bwd(xbc, dtr, dtr_t, par_row, par_col, s_in, dy, *, name):
    t = xbc.shape[0]
    nc = t // SSM_CHUNK
    l, p = SSM_CHUNK, SSM_HEAD_DIM

    def body(xbc_ref, dtr_ref, dtrT_ref, prow_ref, pcol_ref, sin_ref, dy_ref, dxbc_ref, ddtr_ref, dpar_ref,
             dstate, yd_buf, dxd_buf):
        @pl.when(pl.program_id(1) == 0)
        def _():
            dstate[...] = jnp.zeros_like(dstate)
            dpar_ref[...] = jnp.zeros_like(dpar_ref)

        q = _ssd_common(xbc_ref, dtr_ref, dtrT_ref, prow_ref, pcol_ref)
        x, bm, cm, ex, li, si = q["x"], q["bm"], q["cm"], q["ex"], q["li"], q["si"]
        e_x, el_x, dec_x = q["e_x"], q["el_x"], q["dec_x"]
        st = sin_ref[0]
        dst = dstate[...]
        dy = dy_ref[...]
        xd = x * q["dt_x"]
        g = _dot(cm, bm, NT)
        dg = jnp.zeros((l, l), f32)
        for r in range(SSM_HPG):
            sl = slice(r * p, (r + 1) * p)
            diff = q["cs"][:, r:r + 1] - q["cs_t"][r:r + 1, :]
            lm = jnp.where(li >= si, jnp.exp(jnp.minimum(diff, 0.0)), 0.0)
            m = (g * lm).astype(bf16)
            xdh, dyh = xd[:, sl].astype(bf16), dy[:, sl].astype(bf16)
            yd_buf[:, sl] = _dot(m, xdh, NN)
            dxd_buf[:, sl] = _dot(m, dyh, TN)
            dg = dg + _dot(dyh, xdh, NT) * lm
        yd, dxd_diag = yd_buf[...], dxd_buf[...]
        yo = e_x * _dot(cm, st, NN)
        dz = e_x * dy
        wv = _dot(bm, dst, NN)
        xw = xd * wv * dec_x
        row8 = lax.broadcasted_iota(jnp.int32, (l, SSM_HPG), 0)
        dy_b, xd_b = dy.astype(bf16).astype(f32), xd.astype(bf16).astype(f32)
        dcs = _dot_01(dy_b * yd - xd_b * dxd_diag + dy * yo - xw, ex, NT, 0, 3)
        tail = jnp.sum(xw, axis=0, keepdims=True) + el_x * jnp.sum(dst * st, axis=0, keepdims=True)
        dcl = _dot_01(jnp.broadcast_to(tail, (SSM_HPG, SSM_GROUP_W)), ex, NT, 0, 3)[0:1, :]
        dcs = dcs + jnp.where(row8 == l - 1, dcl, 0.0)
        dda = _dot_01(q["tri"], dcs, TN, 1, 3)
        dxd = dxd_diag + dec_x * wv
        ddt = _dot_01(dxd * x, ex, NT, 0, 3) + dda * q["a_row"]
        ddtr = ddt * _sigmoid(q["dtr"] + q["bias_row"])
        ddtr_ref[0] = ddtr
        dd = _dot_01(jnp.broadcast_to(jnp.sum(dy * x, axis=0, keepdims=True), (SSM_HPG, SSM_GROUP_W)), ex, NT, 0, 2)[0:1, :]
        dpar_ref[0, 0:1, :] += jnp.sum(ddtr, axis=0, keepdims=True)
        dpar_ref[0, 1:2, :] += jnp.sum(dda * q["dt"], axis=0, keepdims=True) * q["a_row"]
        dpar_ref[0, 2:3, :] += dd
        dxbc_ref[:, 0:SSM_GROUP_W] = dxd * q["dt_x"] + q["d_x"] * dy
        dxbc_ref[:, SSM_GROUP_W:SSM_GROUP_W + SSM_D_STATE] = _dot(dg, cm, TN) + _dot(xd * dec_x, dst, NT)
        dxbc_ref[:, SSM_GROUP_W + SSM_D_STATE:XBC_GROUP_W] = _dot(dg, bm, NN) + _dot(dz, st, NT)
        dstate[...] = _dot(cm.T, dz, NN) + el_x * dst

    rc = lambda c: nc - 1 - c
    return pl.pallas_call(
        body, name=name, grid=(SSM_GROUPS, nc),
        in_specs=[pl.BlockSpec((l, XBC_GROUP_W), lambda g, c: (rc(c), g)),
                  pl.BlockSpec((1, l, SSM_HPG), lambda g, c: (g, rc(c), 0)),
                  pl.BlockSpec((1, SSM_HPG, l), lambda g, c: (g, 0, rc(c))),
                  pl.BlockSpec((1, 8, 8), lambda g, c: (g, 0, 0)),
                  pl.BlockSpec((1, 8, 8), lambda g, c: (g, 0, 0)),
                  pl.BlockSpec((1, SSM_D_STATE, SSM_GROUP_W), lambda g, c: (rc(c), 0, g)),
                  pl.BlockSpec((l, SSM_GROUP_W), lambda g, c: (rc(c), g))],
        out_specs=[pl.BlockSpec((l, XBC_GROUP_W), lambda g, c: (rc(c), g)),
                   pl.BlockSpec((1, l, SSM_HPG), lambda g, c: (g, rc(c), 0)),
                   pl.BlockSpec((1, 8, 8), lambda g, c: (g, 0, 0))],
        out_shape=[jax.ShapeDtypeStruct((t, SSM_CONV_DIM), f32),
                   jax.ShapeDtypeStruct((SSM_GROUPS, t, SSM_HPG), f32),
                   jax.ShapeDtypeStruct((SSM_GROUPS, 8, 8), f32)],
        scratch_shapes=[pltpu.VMEM((SSM_D_STATE, SSM_GROUP_W), f32), pltpu.VMEM((l, SSM_GROUP_W), f32),
                        pltpu.VMEM((l, SSM_GROUP_W), f32)],
        compiler_params=pltpu.CompilerParams(dimension_semantics=("parallel", "arbitrary")),
    )(xbc, dtr, dtr_t, par_row, par_col, s_in, dy)


def gnorm_fwd(y, proj, w, *, name):
    t = y.shape[0]
    tr = _pick(t, 512, 8)
    gw = SSM_GROUP_W
    zb = OFF_Z // gw

    def body(y_ref, z_ref, w_ref, o_ref):
        y2 = y_ref[...] * _silu(z_ref[...])
        r = lax.rsqrt(jnp.mean(y2 * y2, axis=-1, keepdims=True) + RMS_EPS)
        o_ref[...] = (y2 * r * w_ref[...]).astype(bf16)

    return pl.pallas_call(
        body, name=name, grid=(SSM_GROUPS, t // tr),
        in_specs=[pl.BlockSpec((tr, gw), lambda g, i: (i, g)), pl.BlockSpec((tr, gw), lambda g, i: (i, zb + g)),
                  pl.BlockSpec((1, gw), lambda g, i: (0, g))],
        out_specs=pl.BlockSpec((tr, gw), lambda g, i: (i, g)), out_shape=jax.ShapeDtypeStruct((t, SSM_D_INNER), bf16),
    )(y, proj, w)


def gnorm_bwd(y, proj, w, dout, dst, *, name):
    t = y.shape[0]
    tr = _pick(t, 512, 8)
    gw = SSM_GROUP_W
    zb = OFF_Z // gw

    def body(_, y_ref, z_ref, w_ref, do_ref, dy_ref, dz_ref, dw_ref):
        yv, zv = y_ref[...], z_ref[...]
        sz = _silu(zv)
        y2 = yv * sz
        dy2, dw = _rms_bwd_math(y2, w_ref[...], do_ref[...].astype(f32))
        dy_ref[...] = dy2 * sz
        dz_ref[...] = (dy2 * yv * _dsilu(zv)).astype(bf16)

        @pl.when(pl.program_id(1) == 0)
        def _():
            dw_ref[...] = jnp.zeros_like(dw_ref)

        dw_ref[...] += dw

    tile = pl.BlockSpec((tr, gw), lambda g, i: (i, g))
    vec = pl.BlockSpec((1, gw), lambda g, i: (0, g))
    return pl.pallas_call(
        body, name=name, grid=(SSM_GROUPS, t // tr),
        in_specs=[ANY, tile, pl.BlockSpec((tr, gw), lambda g, i: (i, zb + g)), vec, tile],
        out_specs=[tile, pl.BlockSpec((tr, gw), lambda g, i: (i, zb + g)), vec],
        out_shape=[jax.ShapeDtypeStruct((t, SSM_D_INNER), f32), jax.ShapeDtypeStruct(dst.shape, bf16),
                   jax.ShapeDtypeStruct((1, SSM_D_INNER), f32)],
        input_output_aliases={0: 1},
        compiler_params=pltpu.CompilerParams(dimension_semantics=("parallel", "arbitrary")),
    )(dst, y, proj, w, dout)


LRU_ROWS = 256


def _lru_gates(uv, wr_ref, wi_ref, br_ref, bi_ref, lam_ref):
    rg = _sigmoid(_dot(uv, wr_ref[0], NN) + br_ref[...])
    ig = _sigmoid(_dot(uv, wi_ref[0], NN) + bi_ref[...])
    sp = _softplus(-lam_ref[...])
    la = -LRU_C * rg * sp
    a = jnp.exp(la)
    s = jnp.sqrt(jnp.maximum(-_expm1(2.0 * la), 0.0))
    return rg, ig, sp, la, a, s


def lru_fwd(u, proj, w_r, b_r, w_i, b_i, lam, *, name):
    t = u.shape[0]
    r = LRU_ROWS
    lb = LRU_BLOCK
    yb = OFF_LY // lb

    def body(u_ref, y_ref, wr_ref, br_ref, wi_ref, bi_ref, lam_ref, h_ref, o_ref, carry):
        @pl.when(pl.program_id(1) == 0)
        def _():
            carry[...] = jnp.zeros_like(carry)

        uv = u_ref[...]
        _, ig, _, _, a, s = _lru_gates(uv, wr_ref, wi_ref, br_ref, bi_ref, lam_ref)
        b = s * ig * uv
        row = lax.broadcasted_iota(jnp.int32, (r, lb), 0)
        d = 1
        while d < r:
            keep = row >= d
            b = b + a * jnp.where(keep, pltpu.roll(b, d, 0), 0.0)
            a = a * jnp.where(keep, pltpu.roll(a, d, 0), 1.0)
            d *= 2
        h = b + a * carry[0:1, :]
        carry[0:1, :] = h[r - 1:r, :]
        h_ref[...] = h
        o_ref[...] = (h * _gelu(y_ref[...])).astype(bf16)

    tile = pl.BlockSpec((r, lb), lambda hb, j: (j, hb))
    vec = pl.BlockSpec((1, lb), lambda hb, j: (0, hb))
    wsp = pl.BlockSpec((1, lb, lb), lambda hb, j: (hb, 0, 0))
    return pl.pallas_call(
        body, name=name, grid=(LRU_BLOCKS, t // r),
        in_specs=[tile, pl.BlockSpec((r, lb), lambda hb, j: (j, yb + hb)), wsp, vec, wsp, vec, vec],
        out_specs=[tile, tile],
        out_shape=[jax.ShapeDtypeStruct((t, LRU_WIDTH), f32), jax.ShapeDtypeStruct((t, LRU_WIDTH), bf16)],
        scratch_shapes=[pltpu.VMEM((8, lb), f32)],
        compiler_params=pltpu.CompilerParams(dimension_semantics=("parallel", "arbitrary")),
    )(u, proj, w_r, b_r, w_i, b_i, lam)


def lru_bwd(u, proj, hseq, dout, w_r, b_r, w_i, b_i, lam, dst, *, name):
    t = u.shape[0]
    r = LRU_ROWS
    nt = t // r
    lb = LRU_BLOCK
    yb = OFF_LY // lb

    def body(_, u_ref, y_ref, h_ref, hp_ref, do_ref, wr_ref, br_ref, wi_ref, bi_ref, lam_ref,
             du_ref, dy_ref, dwr_ref, dwi_ref, dbr_ref, dbi_ref, dlam_ref, carry_dh, carry_a):
        j = pl.program_id(1)

        @pl.when(j == 0)
        def _():
            carry_dh[...] = jnp.zeros_like(carry_dh)
            carry_a[...] = jnp.zeros_like(carry_a)
            dwr_ref[...] = jnp.zeros_like(dwr_ref)
            dwi_ref[...] = jnp.zeros_like(dwi_ref)
            dbr_ref[...] = jnp.zeros_like(dbr_ref)
            dbi_ref[...] = jnp.zeros_like(dbi_ref)
            dlam_ref[...] = jnp.zeros_like(dlam_ref)

        uv = u_ref[...]
        yv = y_ref[...]
        hv = h_ref[...]
        dov = do_ref[...]
        rg, ig, sp, la, a, s = _lru_gates(uv, wr_ref, wi_ref, br_ref, bi_ref, lam_ref)
        dy_ref[...] = (dov * hv * _dgelu(yv)).astype(bf16)
        gq = dov * _gelu(yv)
        row = lax.broadcasted_iota(jnp.int32, (r, lb), 0)
        an = jnp.where(row < r - 1, pltpu.roll(a, r - 1, 0), carry_a[0:1, :])
        d = 1
        while d < r:
            keep = row < r - d
            gq = gq + an * jnp.where(keep, pltpu.roll(gq, r - d, 0), 0.0)
            an = an * jnp.where(keep, pltpu.roll(an, r - d, 0), 1.0)
            d *= 2
        dh = gq + an * carry_dh[0:1, :]
        carry_dh[0:1, :] = dh[0:1, :]
        carry_a[0:1, :] = a[0:1, :]
        first = jnp.where(j == nt - 1, 0.0, 1.0) * hp_ref[7:8, :]
        hprev = jnp.where(row >= 1, pltpu.roll(hv, 1, 0), first)
        da = dh * hprev
        iu = ig * uv
        e2 = jnp.exp(2.0 * la)
        dla = da * a - dh * iu * e2 / jnp.maximum(s, 1e-30)
        drp = dla * (-LRU_C * sp) * rg * (1.0 - rg)
        dip = dh * s * uv * ig * (1.0 - ig)
        dlam_ref[...] += jnp.sum(dla * (LRU_C * rg) * _sigmoid(-lam_ref[...]), axis=0, keepdims=True)
        du_ref[...] = dh * s * ig + _dot(drp, wr_ref[0], NT) + _dot(dip, wi_ref[0], NT)
        dwr_ref[0] += _dot(uv, drp, TN)
        dwi_ref[0] += _dot(uv, dip, TN)
        dbr_ref[...] += jnp.sum(drp, axis=0, keepdims=True)
        dbi_ref[...] += jnp.sum(dip, axis=0, keepdims=True)

    rj = lambda j: nt - 1 - j
    tile = pl.BlockSpec((r, lb), lambda hb, j: (rj(j), hb))
    vec = pl.BlockSpec((1, lb), lambda hb, j: (0, hb))
    wsp = pl.BlockSpec((1, lb, lb), lambda hb, j: (hb, 0, 0))
    hprev_spec = pl.BlockSpec((8, lb), lambda hb, j: (jnp.maximum(rj(j) * (r // 8) - 1, 0), hb))
    ywin = pl.BlockSpec((r, lb), lambda hb, j: (rj(j), yb + hb))
    return pl.pallas_call(
        body, name=name, grid=(LRU_BLOCKS, nt),
        in_specs=[ANY, tile, ywin, tile, hprev_spec, tile, wsp, vec, wsp, vec, vec],
        out_specs=[tile, ywin, wsp, wsp, vec, vec, vec],
        out_shape=[jax.ShapeDtypeStruct((t, LRU_WIDTH), f32), jax.ShapeDtypeStruct(dst.shape, bf16),
                   jax.ShapeDtypeStruct((LRU_BLOCKS, lb, lb), f32), jax.ShapeDtypeStruct((LRU_BLOCKS, lb, lb), f32),
                   jax.ShapeDtypeStruct((1, LRU_WIDTH), f32), jax.ShapeDtypeStruct((1, LRU_WIDTH), f32),
                   jax.ShapeDtypeStruct((1, LRU_WIDTH), f32)],
        input_output_aliases={0: 1},
        scratch_shapes=[pltpu.VMEM((8, lb), f32), pltpu.VMEM((8, lb), f32)],
        compiler_params=pltpu.CompilerParams(dimension_semantics=("parallel", "arbitrary")),
    )(dst, u, proj, hseq, hseq, dout, w_r, b_r, w_i, b_i, lam)


def merge_fwd(proj, bg, y_ssm, y_lru, *, name):
    t, d = y_ssm.shape
    tr = _pick(t, 256, 8)
    gb = OFF_GATES // d

    def body(gs_ref, gl_ref, bs_ref, bl_ref, ys_ref, yl_ref, o_ref):
        gs = _sigmoid(gs_ref[...] + bs_ref[...])
        gl = _sigmoid(gl_ref[...] + bl_ref[...])
        o_ref[...] = (gs * ys_ref[...].astype(f32) + gl * yl_ref[...].astype(f32)).astype(bf16)

    row = pl.BlockSpec((tr, d), lambda i: (i, 0))
    return pl.pallas_call(
        body, name=name, grid=(t // tr,),
        in_specs=[pl.BlockSpec((tr, d), lambda i: (i, gb)), pl.BlockSpec((tr, d), lambda i: (i, gb + 1)),
                  pl.BlockSpec((1, d), lambda i: (0, 0)), pl.BlockSpec((1, d), lambda i: (0, 1)), row, row],
        out_specs=row, out_shape=jax.ShapeDtypeStruct((t, d), bf16),
    )(proj, proj, bg, bg, y_ssm, y_lru)


def merge_bwd(proj, bg, y_ssm, y_lru, dmix, *, name):
    t, d = y_ssm.shape
    tr = _pick(t, 256, 8)
    gb = OFF_GATES // d

    def body(gs_ref, gl_ref, bs_ref, bl_ref, ys_ref, yl_ref, dm_ref, dg_ref, dys_ref, dyl_ref, dbg_ref):
        gs = _sigmoid(gs_ref[...] + bs_ref[...])
        gl = _sigmoid(gl_ref[...] + bl_ref[...])
        dm = dm_ref[...].astype(f32)
        dys_ref[...] = (dm * gs).astype(bf16)
        dyl_ref[...] = (dm * gl).astype(bf16)
        dgs = dm * ys_ref[...].astype(f32) * gs * (1.0 - gs)
        dgl = dm * yl_ref[...].astype(f32) * gl * (1.0 - gl)
        dg_ref[:, 0:d] = dgs.astype(bf16)
        dg_ref[:, d:2 * d] = dgl.astype(bf16)

        @pl.when(pl.program_id(0) == 0)
        def _():
            dbg_ref[...] = jnp.zeros_like(dbg_ref)

        dbg_ref[:, 0:d] += jnp.sum(dgs, axis=0, keepdims=True)
        dbg_ref[:, d:2 * d] += jnp.sum(dgl, axis=0, keepdims=True)

    row = pl.BlockSpec((tr, d), lambda i: (i, 0))
    return pl.pallas_call(
        body, name=name, grid=(t // tr,),
        in_specs=[pl.BlockSpec((tr, d), lambda i: (i, gb)), pl.BlockSpec((tr, d), lambda i: (i, gb + 1)),
                  pl.BlockSpec((1, d), lambda i: (0, 0)), pl.BlockSpec((1, d), lambda i: (0, 1)), row, row, row],
        out_specs=[pl.BlockSpec((tr, 2 * d), lambda i: (i, OFF_GATES // (2 * d))), row, row,
                   pl.BlockSpec((1, 2 * d), lambda i: (0, 0))],
        out_shape=[jax.ShapeDtypeStruct((t, PROJ_W), bf16), jax.ShapeDtypeStruct((t, d), bf16),
                   jax.ShapeDtypeStruct((t, d), bf16), jax.ShapeDtypeStruct((1, 2 * d), f32)],
        compiler_params=pltpu.CompilerParams(dimension_semantics=("arbitrary",)),
    )(proj, proj, bg, bg, y_ssm, y_lru, dmix)


def swiglu_fwd(ff, *, name):
    t = ff.shape[0]
    hd = FFN_HIDDEN
    tr = _pick(t, 128, 8)

    def body(f_ref, o_ref):
        o_ref[...] = (_silu(f_ref[:, 0:hd].astype(f32)) * f_ref[:, hd:2 * hd].astype(f32)).astype(bf16)

    return pl.pallas_call(
        body, name=name, grid=(t // tr,), in_specs=[pl.BlockSpec((tr, 2 * hd), lambda i: (i, 0))],
        out_specs=pl.BlockSpec((tr, hd), lambda i: (i, 0)), out_shape=jax.ShapeDtypeStruct((t, hd), bf16),
    )(ff)


def swiglu_bwd(ff, dact, *, name):
    t = ff.shape[0]
    hd = FFN_HIDDEN
    tr = _pick(t, 128, 8)

    def body(f_ref, d_ref, o_ref):
        gate, up, dv = f_ref[:, 0:hd].astype(f32), f_ref[:, hd:2 * hd].astype(f32), d_ref[...].astype(f32)
        o_ref[:, 0:hd] = (dv * up * _dsilu(gate)).astype(bf16)
        o_ref[:, hd:2 * hd] = (dv * _silu(gate)).astype(bf16)

    return pl.pallas_call(
        body, name=name, grid=(t // tr,),
        in_specs=[pl.BlockSpec((tr, 2 * hd), lambda i: (i, 0)), pl.BlockSpec((tr, hd), lambda i: (i, 0))],
        out_specs=pl.BlockSpec((tr, 2 * hd), lambda i: (i, 0)), out_shape=jax.ShapeDtypeStruct((t, 2 * hd), bf16),
    )(ff, dact)


def _adam_math(w, g, m, v):
    m = ADAM_B1 * m + (1.0 - ADAM_B1) * g
    v = ADAM_B2 * v + (1.0 - ADAM_B2) * (g * g)
    m_hat = m / (1.0 - ADAM_B1 ** ADAM_STEP)
    v_hat = v / (1.0 - ADAM_B2 ** ADAM_STEP)
    delta = -ADAM_LR * (m_hat / (jnp.sqrt(v_hat) + ADAM_EPS) + ADAM_WD * w)
    return delta, m, v


def _row_tile(rows, cols):
    cap = max(8, (1 << 18) // cols)
    return _pick(rows, cap, 8) if rows % 8 == 0 else rows


def adamw(w, g, m, v, *, name):
    rows, cols = w.shape
    tr = _row_tile(rows, cols)

    def body(w_ref, g_ref, m_ref, v_ref, d_ref, nm_ref, nv_ref):
        d, nm, nv = _adam_math(w_ref[...], g_ref[...], m_ref[...], v_ref[...])
        d_ref[...] = d
        nm_ref[...] = nm
        nv_ref[...] = nv

    tile = pl.BlockSpec((tr, cols), lambda i: (i, 0))
    return pl.pallas_call(
        body, name=name, grid=(rows // tr,), in_specs=[tile] * 4, out_specs=[tile] * 3,
        out_shape=[jax.ShapeDtypeStruct((rows, cols), f32)] * 3,
    )(w, g, m, v)


def adamw_many(ws, gs, ms, vs, *, name):
    n = len(ws)

    def body(*refs):
        for i in range(n):
            d, nm, nv = _adam_math(refs[i][...], refs[n + i][...], refs[2 * n + i][...], refs[3 * n + i][...])
            refs[4 * n + 3 * i][...] = d
            refs[4 * n + 3 * i + 1][...] = nm
            refs[4 * n + 3 * i + 2][...] = nv

    outs = pl.pallas_call(
        body, name=name, out_shape=[jax.ShapeDtypeStruct(w.shape, f32) for w in ws for _ in range(3)],
    )(*ws, *gs, *ms, *vs)
    return [tuple(outs[3 * i:3 * i + 3]) for i in range(n)]


def pair_add(dw, rbuf, idx, *, name):
    n, rows, cols = dw.shape
    hr = rows // 2
    tr = _row_tile(hr, cols)
    nrt = hr // tr

    def body(idx_ref, a_ref, b_ref, o_ref, own_ref):
        s = a_ref[...] + b_ref[...]
        o_ref[...] = s.astype(bf16)

        @pl.when(pl.program_id(1) == idx_ref[0])
        def _():
            own_ref[...] = s[0]

    return pl.pallas_call(
        body, name=name,
        grid_spec=pltpu.PrefetchScalarGridSpec(
            num_scalar_prefetch=1, grid=(nrt, n),
            in_specs=[pl.BlockSpec((1, tr, cols), lambda i, k, idx: (k, idx[1] * nrt + i, 0)),
                      pl.BlockSpec((1, tr, cols), lambda i, k, idx: (k, i, 0))],
            out_specs=[pl.BlockSpec((1, tr, cols), lambda i, k, idx: (k, i, 0)),
                       pl.BlockSpec((tr, cols), lambda i, k, idx: (i, 0))]),
        out_shape=[jax.ShapeDtypeStruct((n, hr, cols), bf16), jax.ShapeDtypeStruct((hr, cols), f32)],
    )(idx, dw, rbuf)


def chip_sum(own, rbuf, idx, *, name):
    hr, cols = own.shape
    tr = _row_tile(hr, cols)
    nrt = hr // tr

    def body(idx_ref, a_ref, b_ref, o_ref):
        o_ref[...] = ((a_ref[...] + b_ref[0].astype(f32)) + b_ref[1].astype(f32)) + b_ref[2].astype(f32)

    return pl.pallas_call(
        body, name=name,
        grid_spec=pltpu.PrefetchScalarGridSpec(
            num_scalar_prefetch=1, grid=(nrt,),
            in_specs=[pl.BlockSpec((tr, cols), lambda i, idx: (i, 0)),
                      pl.BlockSpec((3, tr, cols), lambda i, idx: (0, i, 0))],
            out_specs=pl.BlockSpec((tr, cols), lambda i, idx: (idx[1] * nrt + i, 0))),
        out_shape=jax.ShapeDtypeStruct((2 * hr, cols), f32),
    )(idx, own, rbuf)


def sum8(rbuf, *, name):
    n, rows, cols = rbuf.shape
    tr = _row_tile(rows, cols * n)

    def body(a_ref, o_ref):
        acc = a_ref[0]
        for k in range(1, n):
            acc = acc + a_ref[k]
        o_ref[...] = acc

    return pl.pallas_call(
        body, name=name, grid=(rows // tr,), in_specs=[pl.BlockSpec((n, tr, cols), lambda i: (0, i, 0))],
        out_specs=pl.BlockSpec((tr, cols), lambda i: (i, 0)), out_shape=jax.ShapeDtypeStruct((rows, cols), f32),
    )(rbuf)


def _coords():
    return lax.axis_index("x"), lax.axis_index("y"), lax.axis_index("c")


def _other_chips(x, y):
    return [(1 - x, y), (x, 1 - y), (1 - x, 1 - y)]


def gather_weights(shards, *, name):
    n = len(shards)
    halves = [s.shape[0] // 2 for s in shards]

    def body(*refs):
        ins, outs = refs[:n], refs[n:2 * n]
        send1, recv1, send2, recv2 = refs[2 * n:]
        x, y, c = _coords()
        me = 2 * x + y
        chips = _other_chips(x, y)
        sibling = (x, y, 1 - c)

        def half(i, k, hc):
            return outs[i].at[k, pl.ds(hc * halves[i], halves[i]), :]

        def ici(i, j):
            return pltpu.make_async_remote_copy(
                src_ref=ins[i].at[pl.ds(c * halves[i], halves[i]), :], dst_ref=half(i, me, c),
                send_sem=send1.at[i, j], recv_sem=recv1.at[i, j], device_id=(*chips[j], c), device_id_type=MESH)

        def landed(i, j):
            kj = 2 * chips[j][0] + chips[j][1]
            return pltpu.make_async_remote_copy(
                src_ref=half(i, kj, c), dst_ref=half(i, kj, c),
                send_sem=send2.at[i, j], recv_sem=recv1.at[i, j], device_id=sibling, device_id_type=MESH)

        def from_sibling(i, j):
            kj = 2 * chips[j][0] + chips[j][1]
            return pltpu.make_async_remote_copy(
                src_ref=half(i, kj, 1 - c), dst_ref=half(i, kj, 1 - c),
                send_sem=send2.at[i, j], recv_sem=recv2.at[i, j], device_id=sibling, device_id_type=MESH)

        def d2d(i, j):
            kj = 2 * chips[j][0] + chips[j][1]
            return pltpu.make_async_remote_copy(
                src_ref=half(i, kj, c), dst_ref=half(i, kj, c),
                send_sem=send2.at[i, j], recv_sem=recv2.at[i, j], device_id=sibling, device_id_type=MESH)

        for j in range(3):
            for i in range(n):
                ici(i, j).start()
        for j in range(3):
            for i in range(n):
                landed(i, j).wait_recv()
                d2d(i, j).start()
        for j in range(3):
            for i in range(n):
                from_sibling(i, j).wait_recv()
        for j in range(3):
            for i in range(n):
                ici(i, j).wait_send()
                d2d(i, j).wait_send()

    return pl.pallas_call(
        body, name=name, in_specs=[ANY] * n, out_specs=[ANY] * n,
        out_shape=[jax.ShapeDtypeStruct((N_CHIPS,) + s.shape, s.dtype) for s in shards],
        scratch_shapes=[pltpu.SemaphoreType.DMA((n, 3))] * 4,
    )(*shards)


def pair_exchange(grads, *, name):
    n = len(grads)
    halves = [g.shape[1] // 2 for g in grads]

    def body(*refs):
        ins, outs = refs[:n], refs[n:2 * n]
        send, recv = refs[2 * n:]
        x, y, c = _coords()
        cps = [pltpu.make_async_remote_copy(
            src_ref=ins[i].at[:, pl.ds((1 - c) * halves[i], halves[i]), :], dst_ref=outs[i],
            send_sem=send.at[i], recv_sem=recv.at[i], device_id=(x, y, 1 - c), device_id_type=MESH) for i in range(n)]
        for cp in cps:
            cp.start()
        for cp in cps:
            cp.wait()

    return pl.pallas_call(
        body, name=name, in_specs=[ANY] * n, out_specs=[ANY] * n,
        out_shape=[jax.ShapeDtypeStruct((N_CHIPS, g.shape[1] // 2, g.shape[2]), g.dtype) for g in grads],
        scratch_shapes=[pltpu.SemaphoreType.DMA((n,))] * 2,
    )(*grads)


def pair_gather(bufs, *, name):
    n = len(bufs)

    def body(*refs):
        ins, outs = refs[:n], refs[n:2 * n]
        send, recv = refs[2 * n:]
        x, y, c = _coords()
        cps = []
        for i in range(n):
            hr = ins[i].shape[0] // 2
            cps.append(pltpu.make_async_remote_copy(
                src_ref=ins[i].at[pl.ds(c * hr, hr), :], dst_ref=outs[i].at[pl.ds(c * hr, hr), :],
                send_sem=send.at[i], recv_sem=recv.at[i], device_id=(x, y, 1 - c), device_id_type=MESH))
        for cp in cps:
            cp.start()
        for i in range(n):
            hr = ins[i].shape[0] // 2
            pltpu.make_async_remote_copy(
                src_ref=ins[i].at[pl.ds((1 - c) * hr, hr), :], dst_ref=outs[i].at[pl.ds((1 - c) * hr, hr), :],
                send_sem=send.at[i], recv_sem=recv.at[i], device_id=(x, y, 1 - c), device_id_type=MESH).wait_recv()
        for cp in cps:
            cp.wait_send()

    return pl.pallas_call(
        body, name=name, in_specs=[ANY] * n, out_specs=[ANY] * n,
        out_shape=[jax.ShapeDtypeStruct(b.shape, b.dtype) for b in bufs],
        input_output_aliases={i: i for i in range(n)},
        scratch_shapes=[pltpu.SemaphoreType.DMA((n,))] * 2,
    )(*bufs)


def all_exchange(buf, *, name):
    rows, cols = buf.shape

    def body(in_ref, out_ref, send, recv):
        x, y, c = _coords()
        me = 4 * x + 2 * y + c
        cps = []
        for d in range(1, 8):
            px = 1 - x if d & 4 else x
            py = 1 - y if d & 2 else y
            pc = 1 - c if d & 1 else c
            cps.append(pltpu.make_async_remote_copy(
                src_ref=in_ref, dst_ref=out_ref.at[me], send_sem=send.at[d - 1], recv_sem=recv.at[d - 1],
                device_id=(px, py, pc), device_id_type=MESH))
        for cp in cps:
            cp.start()
        for d in range(1, 8):
            px = 1 - x if d & 4 else x
            py = 1 - y if d & 2 else y
            pc = 1 - c if d & 1 else c
            src = 4 * px + 2 * py + pc
            pltpu.make_async_remote_copy(
                src_ref=in_ref, dst_ref=out_ref.at[src], send_sem=send.at[d - 1], recv_sem=recv.at[d - 1],
                device_id=(px, py, pc), device_id_type=MESH).wait_recv()
        for cp in cps:
            cp.wait_send()

    return pl.pallas_call(
        body, name=name, in_specs=[ANY], out_specs=ANY,
        out_shape=jax.ShapeDtypeStruct((8, rows, cols), buf.dtype),
        scratch_shapes=[pltpu.SemaphoreType.DMA((7,)), pltpu.SemaphoreType.DMA((7,))],
    )(buf)


HBM = pl.BlockSpec(memory_space=pltpu.HBM)
SEM = pl.BlockSpec(memory_space=pltpu.SEMAPHORE)
EFFECT = pltpu.SideEffectType.DATAFLOW_SIDE_EFFECTING


def split_start(arrays, after, copies, sem_shape, *, name):
    na = len(arrays)

    def body(*refs):
        for cp in copies(refs[:na], refs[na + 1], refs[na + 2]):
            cp.start()
        refs[-1][...] = jnp.zeros((8, 128), f32)

    outs = pl.pallas_call(
        body, name=name,
        out_shape=(pltpu.SemaphoreType.DMA(sem_shape), pltpu.SemaphoreType.DMA(sem_shape),
                   *[pltpu.HBM(a.shape, a.dtype) for a in arrays], jax.ShapeDtypeStruct((8, 128), f32)),
        in_specs=[HBM] * na + [ANY], out_specs=(SEM, SEM, *[HBM] * na, pl.BlockSpec(memory_space=pltpu.VMEM)),
        input_output_aliases={i: 2 + i for i in range(na)},
        compiler_params=pltpu.CompilerParams(has_side_effects=EFFECT),
    )(*[pltpu.with_memory_space_constraint(a, pltpu.HBM) for a in arrays], after)
    return outs[0], outs[1], list(outs[2:2 + na]), outs[-1]


def split_wait(send, recv, arrays, after, copies, *, name):
    na = len(arrays)

    def body(*refs):
        for cp in copies(refs[:na], refs[na], refs[na + 1]):
            cp.wait_send()
            cp.wait_recv()

    outs = pl.pallas_call(
        body, name=name, out_shape=tuple(pltpu.HBM(a.shape, a.dtype) for a in arrays),
        in_specs=[HBM] * na + [SEM, SEM, ANY], out_specs=tuple([HBM] * na),
        input_output_aliases={i: i for i in range(na)},
        compiler_params=pltpu.CompilerParams(has_side_effects=EFFECT),
    )(*arrays, send, recv, after)
    return list(outs)


def gather_copies(n):
    def copies(refs, send, recv):
        x, y, c = _coords()
        me = 2 * x + y
        chips = _other_chips(x, y)
        return [pltpu.make_async_remote_copy(
            src_ref=refs[i], dst_ref=refs[n + i].at[me], send_sem=send.at[3 * i + j], recv_sem=recv.at[3 * i + j],
            device_id=(*chips[j], c), device_id_type=MESH) for j in range(3) for i in range(n)]
    return copies


def pair_copies(n):
    def copies(refs, send, recv):
        x, y, c = _coords()
        cps = []
        for i in range(n):
            hr = refs[i].shape[1] // 2
            cps.append(pltpu.make_async_remote_copy(
                src_ref=refs[i].at[:, pl.ds((1 - c) * hr, hr), :], dst_ref=refs[n + i], send_sem=send.at[i],
                recv_sem=recv.at[i], device_id=(x, y, 1 - c), device_id_type=MESH))
        return cps
    return copies


def all_copies():
    def copies(refs, send, recv):
        x, y, c = _coords()
        me = 4 * x + 2 * y + c
        cps = []
        for d in range(1, 8):
            peer = (1 - x if d & 4 else x, 1 - y if d & 2 else y, 1 - c if d & 1 else c)
            cps.append(pltpu.make_async_remote_copy(
                src_ref=refs[0], dst_ref=refs[1].at[me], send_sem=send.at[d - 1], recv_sem=recv.at[d - 1],
                device_id=peer, device_id_type=MESH))
        return cps
    return copies


def reduce_copies(n):
    def copies(refs, send, recv):
        x, y, c = _coords()
        chips = _other_chips(x, y)
        return [pltpu.make_async_remote_copy(
            src_ref=refs[i].at[2 * chips[j][0] + chips[j][1]], dst_ref=refs[n + i].at[j],
            send_sem=send.at[3 * i + j], recv_sem=recv.at[3 * i + j], device_id=(*chips[j], c), device_id_type=MESH)
            for j in range(3) for i in range(n)]
    return copies


def _pack(arrs):
    flat = []
    for a in arrs:
        v = a.reshape(-1).astype(f32)
        pad = (-v.shape[0]) % 128
        flat.append(jnp.pad(v, (0, pad)) if pad else v)
    v = jnp.concatenate(flat)
    rows = v.shape[0] // 128
    pad_rows = (-rows) % 256
    v = v.reshape(rows, 128)
    return jnp.pad(v, ((0, pad_rows), (0, 0))) if pad_rows else v


def _unpack(buf, shapes):
    out, row = [], 0
    for s in shapes:
        size = math.prod(s)
        rows = -(-size // 128)
        out.append(buf[row:row + rows].reshape(-1)[:size].reshape(s))
        row += rows
    return out


def _ref_of_perm():
    ref = np.arange(IN_PROJ_DIM)
    xbc = ref[4096:7168]
    xbc_p = [np.concatenate([xbc[g * 512:(g + 1) * 512], xbc[2048 + g * 128:2048 + (g + 1) * 128],
                             xbc[2560 + g * 128:2560 + (g + 1) * 128]]) for g in range(SSM_GROUPS)]
    return np.concatenate([ref[0:2048], ref[2048:4096], ref[7200:8480], ref[8480:9760], ref[7168:7200],
                           -np.ones(DT_PAD_W - SSM_HEADS, np.int64)] + xbc_p)


def _runs(vals):
    out, start = [], 0
    for i in range(1, len(vals) + 1):
        if i == len(vals) or not (vals[i] == vals[i - 1] + 1 or (vals[i] < 0 and vals[i - 1] < 0)):
            out.append((start, int(vals[start]), i - start))
            start = i
    return out


def _perm_in_from_shards(g):
    ref_of_perm = _ref_of_perm()
    sw = IN_PROJ_DIM // N_CHIPS
    parts = []
    for _, first, length in _runs(ref_of_perm):
        if first < 0:
            parts.append(jnp.zeros((g.shape[1], length), g.dtype))
            continue
        lo = first
        while lo < first + length:
            k = lo // sw
            hi = min(first + length, (k + 1) * sw)
            parts.append(g[k, :, lo - k * sw:hi - k * sw])
            lo = hi
    return jnp.concatenate(parts, axis=-1)


def _unperm_in_to_shards(w):
    ref_of_perm = _ref_of_perm()
    perm_of_ref = np.zeros(IN_PROJ_DIM, np.int64)
    perm_of_ref[ref_of_perm[ref_of_perm >= 0]] = np.nonzero(ref_of_perm >= 0)[0]
    sw = IN_PROJ_DIM // N_CHIPS
    shards = []
    for k in range(N_CHIPS):
        runs = _runs(perm_of_ref[k * sw:(k + 1) * sw])
        shards.append(jnp.concatenate([w[:, first:first + length] for _, first, length in runs], axis=-1))
    return jnp.stack(shards)


def _perm_xbc_cols(w):
    parts = []
    for g in range(SSM_GROUPS):
        parts += [w[..., g * 512:(g + 1) * 512], w[..., 2048 + g * 128:2048 + (g + 1) * 128],
                  w[..., 2560 + g * 128:2560 + (g + 1) * 128]]
    return jnp.concatenate(parts, axis=-1)


def _unperm_xbc_cols(w):
    xs = [w[..., g * XBC_GROUP_W:g * XBC_GROUP_W + 512] for g in range(SSM_GROUPS)]
    bs = [w[..., g * XBC_GROUP_W + 512:g * XBC_GROUP_W + 640] for g in range(SSM_GROUPS)]
    cs = [w[..., g * XBC_GROUP_W + 640:(g + 1) * XBC_GROUP_W] for g in range(SSM_GROUPS)]
    return jnp.concatenate(xs + bs + cs, axis=-1)


def _from_col_shards(w):
    n, r, c = w.shape
    return jnp.transpose(w, (1, 0, 2)).reshape(r, n * c)


def kernel(x, norm1_w, w_in, b_branch_gate, ssm_conv_w, ssm_conv_b, ssm_dt_bias, ssm_a_log, ssm_d, ssm_norm_w, w_out_ssm, lru_conv_w, lru_conv_b, lru_w_r, lru_b_r, lru_w_i, lru_b_i, lru_lambda, w_out_lru, w_out, norm2_w, w_ffn_in, w_ffn_out, norm_f_w, loss_target, m_norm1_w, m_w_in, m_b_branch_gate, m_ssm_conv_w, m_ssm_conv_b, m_ssm_dt_bias, m_ssm_a_log, m_ssm_d, m_ssm_norm_w, m_w_out_ssm, m_lru_conv_w, m_lru_conv_b, m_lru_w_r, m_lru_b_r, m_lru_w_i, m_lru_b_i, m_lru_lambda, m_w_out_lru, m_w_out, m_norm2_w, m_w_ffn_in, m_w_ffn_out, m_norm_f_w, v_norm1_w, v_w_in, v_b_branch_gate, v_ssm_conv_w, v_ssm_conv_b, v_ssm_dt_bias, v_ssm_a_log, v_ssm_d, v_ssm_norm_w, v_w_out_ssm, v_lru_conv_w, v_lru_conv_b, v_lru_w_r, v_lru_b_r, v_lru_w_i, v_lru_b_i, v_lru_lambda, v_w_out_lru, v_w_out, v_norm2_w, v_w_ffn_in, v_w_ffn_out, v_norm_f_w):
    xi, yi, ci = lax.axis_index("x"), lax.axis_index("y"), lax.axis_index("c")
    me = 2 * xi + yi
    idx = jnp.stack([me, ci]).astype(jnp.int32)
    x2 = x[0]
    tgt = loss_target[0]

    big_names = ["w_in", "w_out_ssm", "w_out_lru", "w_out", "w_ffn_in", "w_ffn_out"]
    big_w = dict(w_in=w_in[0], w_out_ssm=w_out_ssm[0], w_out_lru=w_out_lru[0], w_out=w_out[0], w_ffn_in=w_ffn_in[0],
                 w_ffn_out=w_ffn_out[0])
    big_m = dict(w_in=m_w_in[0], w_out_ssm=m_w_out_ssm[0], w_out_lru=m_w_out_lru[0], w_out=m_w_out[0],
                 w_ffn_in=m_w_ffn_in[0], w_ffn_out=m_w_ffn_out[0])
    big_v = dict(w_in=v_w_in[0], w_out_ssm=v_w_out_ssm[0], w_out_lru=v_w_out_lru[0], w_out=v_w_out[0],
                 w_ffn_in=v_w_ffn_in[0], w_ffn_out=v_w_ffn_out[0])
    conv_pad = jnp.zeros((16, 768), f32).at[0:4, :].set(ssm_conv_w[0]).at[8:12, 0:320].set(lru_conv_w[0])
    mine = [big_w["w_in"].astype(bf16), conv_pad]
    gathered = gather_weights(mine, name="gather_weights")
    g_in, g_conv = [lax.dynamic_update_index_in_dim(g, s, me, 0) for g, s in zip(gathered, mine)]
    w_in_p = _perm_in_from_shards(g_in)
    late_names = big_names[1:]
    late = [big_w[k].astype(bf16) for k in late_names]
    late_lands = [lax.empty((N_CHIPS,) + s.shape, bf16) for s in late]
    g_send, g_recv, g_arrays, g_token = split_start(late + late_lands, g_conv, gather_copies(5), (15,),
                                                    name="gather_late_start")
    ssm_cw_full = _from_col_shards(g_conv[:, 0:4, :])
    lru_cw_full = _from_col_shards(g_conv[:, 8:12, 0:320])
    ssm_cw_p = _perm_xbc_cols(ssm_cw_full)
    ssm_cb_p = _perm_xbc_cols(ssm_conv_b)

    par = jnp.stack([ssm_dt_bias[0], ssm_a_log[0], ssm_d[0]], axis=0).reshape(3, SSM_GROUPS, SSM_HPG)
    par_row = jnp.zeros((SSM_GROUPS, 8, 8), f32).at[:, 0:3, :].set(jnp.transpose(par, (1, 0, 2)))
    par_col = jnp.transpose(par_row, (0, 2, 1))

    hn1 = rms_fwd(x2, norm1_w + g_token[0:1, 0:1], name="rms1_fwd")
    proj = mm(hn1, w_in_p, "nn", name="in_proj")
    t = x2.shape[0]
    dtr = jnp.transpose(proj[:, OFF_DT:OFF_DT + 32].reshape(t, SSM_GROUPS, SSM_HPG), (1, 0, 2))
    dtr_t = jnp.transpose(dtr, (0, 2, 1))
    xbc_pre, xbc_post = conv_fwd(proj, OFF_XBC, SSM_CONV_DIM, ssm_cw_p, ssm_cb_p, silu=True, name="ssm_conv_fwd")
    y_ssd, s_in = ssd_fwd(xbc_post, dtr, dtr_t, par_row, par_col, name="ssd_fwd")
    yn = gnorm_fwd(y_ssd, proj, ssm_norm_w, name="gnorm_fwd")
    g_arrays = split_wait(g_send, g_recv, g_arrays, yn, gather_copies(5), name="gather_late_wait")
    g_out_ssm, g_out_lru, g_out, g_ffn_in, g_ffn_out = [
        lax.dynamic_update_index_in_dim(g, s, me, 0) for g, s in zip(g_arrays[5:], late)]
    w_out_ssm_f = g_out_ssm.reshape(SSM_D_INNER, D_MODEL)
    w_out_lru_f = g_out_lru.reshape(LRU_WIDTH, D_MODEL)
    w_out_f = g_out.reshape(D_MODEL, D_MODEL)
    w_ffn_out_f = g_ffn_out.reshape(FFN_HIDDEN, D_MODEL)
    y_ssm = mm(yn, w_out_ssm_f, "nn", out_dtype=bf16, name="out_ssm")
    (u_lru,) = conv_fwd(proj, OFF_LX, LRU_WIDTH, lru_cw_full, lru_conv_b, silu=False, name="lru_conv_fwd")
    h_lru, o_lru = lru_fwd(u_lru, proj, lru_w_r[0], lru_b_r, lru_w_i[0], lru_b_i, lru_lambda, name="lru_fwd")
    y_lru = mm(o_lru, w_out_lru_f, "nn", out_dtype=bf16, name="out_lru")
    mix = merge_fwd(proj, b_branch_gate, y_ssm, y_lru, name="merge_fwd")
    h1 = mm(mix, w_out_f, "nn", add=x2, name="out_proj")
    hn2 = rms_fwd(h1, norm2_w, name="rms2_fwd")
    ff = mm(hn2, g_ffn_in, "nn", b_shards=True, out_dtype=bf16, name="ffn_in")
    act = swiglu_fwd(ff, name="swiglu_fwd")
    h2 = mm(act, w_ffn_out_f, "nn", add=h1, name="ffn_out")
    loss_tile, dh2, d_norm_f, dh2_b = loss_head(h2, norm_f_w.reshape(1, D_MODEL), tgt, name="loss_head")
    loss = lax.psum(loss_tile[0, 0], ("x", "y", "c"))

    d_w_ffn_out = mm(act, dh2_b, "tn", name="d_w_ffn_out")
    dact = mm(dh2_b, w_ffn_out_f, "nt", out_dtype=bf16, name="d_act")
    dff = swiglu_bwd(ff, dact, name="swiglu_bwd")
    d_w_ffn_in = mm(hn2, dff, "tn", out_shards=N_CHIPS, name="d_w_ffn_in")
    dhn2 = mm(dff, g_ffn_in, "nt", b_shards=True, name="d_hn2")
    dh1, d_norm2, dh1_b = rms_bwd(h1, norm2_w, dhn2, dh2, with_bf16=True, name="rms2_bwd")
    d_w_out = mm(mix, dh1_b, "tn", name="d_w_out")
    dmix = mm(dh1_b, w_out_f, "nt", out_dtype=bf16, name="d_mix")
    dproj, dy_ssm, dy_lru, d_bg = merge_bwd(proj, b_branch_gate, y_ssm, y_lru, dmix, name="merge_bwd")
    d_w_out_ssm = mm(yn, dy_ssm, "tn", name="d_w_out_ssm")
    d_w_out_lru = mm(o_lru, dy_lru, "tn", name="d_w_out_lru")
    early_g = [d_w_out_ssm.reshape(N_CHIPS, 512, D_MODEL), d_w_out_lru.reshape(N_CHIPS, 320, D_MODEL),
               d_w_out.reshape(N_CHIPS, 256, D_MODEL), d_w_ffn_in, d_w_ffn_out.reshape(N_CHIPS, 704, D_MODEL)]
    p_lands = [lax.empty((N_CHIPS, g.shape[1] // 2, g.shape[2]), f32) for g in early_g]
    p_send, p_recv, p_arrays, p_token = split_start(early_g + p_lands, early_g[0], pair_copies(5), (5,),
                                                    name="pair_early_start")
    dyn = mm(dy_ssm, w_out_ssm_f, "nt", out_dtype=bf16, after=p_token, name="d_yn")
    dy_ssd, dproj, d_ssm_norm = gnorm_bwd(y_ssd, proj, ssm_norm_w, dyn, dproj, name="gnorm_bwd")
    p_arrays = split_wait(p_send, p_recv, p_arrays, dy_ssd, pair_copies(5), name="pair_early_wait")
    e_pairs = [pair_add(g, rb, idx, name="pair_add_" + k) for g, rb, k in zip(p_arrays[:5], p_arrays[5:], late_names)]
    e_lands = [lax.empty((3,) + p[0].shape[1:], bf16) for p in e_pairs]
    e_send, e_recv, e_arrays, e_token = split_start([p[0] for p in e_pairs] + e_lands, e_pairs[0][1], reduce_copies(5),
                                                    (15,), name="reduce_early_start")
    dxbc_post, ddtr, dpar = ssd_bwd(xbc_post, dtr, dtr_t, par_row + e_token[0:1, 0:1], par_col, s_in, dy_ssd,
                                    name="ssd_bwd")
    dproj, d_ssm_cw_p, d_ssm_cb_p = conv_bwd(dxbc_post, xbc_pre, proj, OFF_XBC, ssm_cw_p, dproj, name="ssm_conv_bwd")
    do_lru = mm(dy_lru, w_out_lru_f, "nt", name="d_o_lru")
    du_lru, dproj, d_w_r, d_w_i, d_b_r, d_b_i, d_lam = lru_bwd(u_lru, proj, h_lru, do_lru, lru_w_r[0], lru_b_r, lru_w_i[0],
                                                               lru_b_i, lru_lambda, dproj, name="lru_bwd")
    dproj, d_lru_cw, d_lru_cb = conv_bwd(du_lru, None, proj, OFF_LX, lru_cw_full, dproj, name="lru_conv_bwd")
    ddt_cols = jnp.transpose(ddtr, (1, 0, 2)).reshape(t, SSM_HEADS).astype(bf16)
    ddt_cols = jnp.pad(ddt_cols, ((0, 0), (0, DT_PAD_W - SSM_HEADS)))
    dproj = lax.dynamic_update_slice(dproj, ddt_cols, (0, OFF_DT))

    d_ssm_cw = _unperm_xbc_cols(d_ssm_cw_p)
    d_ssm_cb = _unperm_xbc_cols(d_ssm_cb_p)
    dpar_h = jnp.transpose(dpar[:, 0:3, :], (1, 0, 2)).reshape(3, SSM_HEADS)
    small_names = ["norm1_w", "b_branch_gate", "ssm_conv_b", "ssm_dt_bias", "ssm_a_log", "ssm_d", "ssm_norm_w",
                   "lru_conv_b", "lru_w_r", "lru_b_r", "lru_w_i", "lru_b_i", "lru_lambda", "norm2_w", "norm_f_w"]
    small_g = dict(norm1_w=jnp.zeros_like(norm1_w), b_branch_gate=d_bg, ssm_conv_b=d_ssm_cb, ssm_dt_bias=dpar_h[0:1], ssm_a_log=dpar_h[1:2],
                   ssm_d=dpar_h[2:3], ssm_norm_w=d_ssm_norm, lru_conv_b=d_lru_cb, lru_w_r=d_w_r[None], lru_b_r=d_b_r,
                   lru_w_i=d_w_i[None], lru_b_i=d_b_i, lru_lambda=d_lam, norm2_w=d_norm2, norm_f_w=d_norm_f.reshape(D_MODEL))
    small_w = dict(norm1_w=norm1_w, b_branch_gate=b_branch_gate, ssm_conv_b=ssm_conv_b, ssm_dt_bias=ssm_dt_bias,
                   ssm_a_log=ssm_a_log, ssm_d=ssm_d, ssm_norm_w=ssm_norm_w, lru_conv_b=lru_conv_b, lru_w_r=lru_w_r,
                   lru_b_r=lru_b_r, lru_w_i=lru_w_i, lru_b_i=lru_b_i, lru_lambda=lru_lambda, norm2_w=norm2_w, norm_f_w=norm_f_w)
    small_m = dict(norm1_w=m_norm1_w, b_branch_gate=m_b_branch_gate, ssm_conv_b=m_ssm_conv_b, ssm_dt_bias=m_ssm_dt_bias,
                   ssm_a_log=m_ssm_a_log, ssm_d=m_ssm_d, ssm_norm_w=m_ssm_norm_w, lru_conv_b=m_lru_conv_b, lru_w_r=m_lru_w_r,
                   lru_b_r=m_lru_b_r, lru_w_i=m_lru_w_i, lru_b_i=m_lru_b_i, lru_lambda=m_lru_lambda, norm2_w=m_norm2_w,
                   norm_f_w=m_norm_f_w)
    small_v = dict(norm1_w=v_norm1_w, b_branch_gate=v_b_branch_gate, ssm_conv_b=v_ssm_conv_b, ssm_dt_bias=v_ssm_dt_bias,
                   ssm_a_log=v_ssm_a_log, ssm_d=v_ssm_d, ssm_norm_w=v_ssm_norm_w, lru_conv_b=v_lru_conv_b, lru_w_r=v_lru_w_r,
                   lru_b_r=v_lru_b_r, lru_w_i=v_lru_w_i, lru_b_i=v_lru_b_i, lru_lambda=v_lru_lambda, norm2_w=v_norm2_w,
                   norm_f_w=v_norm_f_w)
    shapes = [small_w[k].shape for k in small_names]
    conv_shapes = [(4, SSM_CONV_DIM), (4, LRU_WIDTH)]
    g_pack = _pack([small_g[k] for k in small_names] + [d_ssm_cw, d_lru_cw])
    s_send, s_recv, s_arrays, s_token = split_start([g_pack, lax.empty((8,) + g_pack.shape, f32)], g_pack, all_copies(),
                                                    (7,), name="small_start")
    d_w_in_p = mm(hn1, dproj, "tn", after=s_token, name="d_w_in")

    d_w_in_s = _unperm_in_to_shards(d_w_in_p)
    (l_sib,) = pair_exchange([d_w_in_s], name="pair_exchange_late")
    l_pair = pair_add(d_w_in_s, l_sib, idx, name="pair_add_w_in")
    l_land = lax.empty((3,) + l_pair[0].shape[1:], bf16)
    l_send, l_recv, l_arrays, l_token = split_start([l_pair[0], l_land], l_pair[1], reduce_copies(1), (3,),
                                                    name="reduce_late_start")
    dhn1 = mm(dproj, w_in_p, "nt", after=l_token, name="d_hn1")
    grad_x, d_norm1 = rms_bwd(x2, norm1_w, dhn1, dh1, name="rms1_bwd")

    e_arrays = split_wait(e_send, e_recv, e_arrays, d_norm1, reduce_copies(5), name="reduce_early_wait")
    e_half = [chip_sum(p[1], rb, idx, name="chip_sum_" + k) for p, rb, k in zip(e_pairs, e_arrays[5:], late_names)]
    big_out = {}
    for k, g in zip(late_names, pair_gather(e_half, name="pair_gather_early")):
        big_out[k] = (g,) + tuple(adamw(big_w[k], g, big_m[k], big_v[k], name="adamw_" + k))

    s_arrays = split_wait(s_send, s_recv, s_arrays, d_norm1, all_copies(), name="small_wait")
    g_sum = sum8(lax.dynamic_update_index_in_dim(s_arrays[1], g_pack, 2 * me + ci, 0), name="sum8")
    n1 = d_norm1.reshape(8, 128)
    n1_sum = sum8(lax.dynamic_update_index_in_dim(all_exchange(n1, name="all_exchange_norm1"), n1, 2 * me + ci, 0),
                  name="sum8_norm1")
    g_sum = lax.dynamic_update_slice(g_sum, n1_sum, (0, 0))
    g_small = _unpack(g_sum, shapes + conv_shapes)
    g_small[-2] = lax.dynamic_slice_in_dim(g_small[-2], me * 768, 768, axis=1)
    g_small[-1] = lax.dynamic_slice_in_dim(g_small[-1], me * 320, 320, axis=1)
    all_names = small_names + ["ssm_conv_w", "lru_conv_w"]
    small_w.update(ssm_conv_w=ssm_conv_w[0], lru_conv_w=lru_conv_w[0])
    small_m.update(ssm_conv_w=m_ssm_conv_w[0], lru_conv_w=m_lru_conv_w[0])
    small_v.update(ssm_conv_w=v_ssm_conv_w[0], lru_conv_w=v_lru_conv_w[0])
    as2d = lambda a: a.reshape(-1, a.shape[-1])
    upd = adamw_many([as2d(small_w[k]) for k in all_names], [as2d(g) for g in g_small],
                     [as2d(small_m[k]) for k in all_names], [as2d(small_v[k]) for k in all_names], name="adamw_small")
    small_out = {}
    for k, g, u in zip(all_names, g_small, upd):
        small_out[k] = (g,) + tuple(o.reshape(g.shape) for o in u)
    l_arrays = split_wait(l_send, l_recv, l_arrays, upd[0][0], reduce_copies(1), name="reduce_late_wait")
    l_half = chip_sum(l_pair[1], l_arrays[1], idx, name="chip_sum_w_in")
    (g_w_in,) = pair_gather([l_half], name="pair_gather_late")
    big_out["w_in"] = (g_w_in,) + tuple(adamw(big_w["w_in"], g_w_in, big_m["w_in"], big_v["w_in"], name="adamw_w_in"))

    order = ["norm1_w", "w_in", "b_branch_gate", "ssm_conv_w", "ssm_conv_b", "ssm_dt_bias", "ssm_a_log", "ssm_d", "ssm_norm_w",
             "w_out_ssm", "lru_conv_w", "lru_conv_b", "lru_w_r", "lru_b_r", "lru_w_i", "lru_b_i", "lru_lambda", "w_out_lru",
             "w_out", "norm2_w", "w_ffn_in", "w_ffn_out", "norm_f_w"]
    outs = [loss, grad_x[None]]
    for which in range(4):
        for k in order:
            if k in big_out:
                outs.append(big_out[k][which][None])
            elif k in ("ssm_conv_w", "lru_conv_w"):
                outs.append(small_out[k][which][None])
            else:
                outs.append(small_out[k][which])
    return tuple(outs)
```

```python
import functools
import math

import jax
import jax.numpy as jnp
import numpy as np
from jax import lax
from jax.experimental import pallas as pl
from jax.experimental.pallas import tpu as pltpu

f32 = jnp.float32
bf16 = jnp.bfloat16

D_MODEL = 1024
SSM_D_INNER = 2048
SSM_HEADS = 32
SSM_HEAD_DIM = 64
SSM_GROUPS = 4
SSM_HPG = 8
SSM_D_STATE = 128
SSM_CHUNK = 128
SSM_GROUP_W = 512
SSM_CONV_DIM = 3072
XBC_GROUP_W = 768
LRU_WIDTH = 1280
LRU_BLOCKS = 10
LRU_BLOCK = 128
LRU_C = 8.0
FFN_HIDDEN = 2816
RMS_EPS = 1e-6
IN_PROJ_DIM = 9760
N_CHIPS = 4

OFF_GATES = 0
OFF_Z = 2048
OFF_LX = 4096
OFF_LY = 5376
OFF_DT = 6656
DT_PAD_W = 256
OFF_XBC = 6912
PROJ_W = 9984

ADAM_LR = 0.001
ADAM_B1 = 0.9
ADAM_B2 = 0.999
ADAM_EPS = 1e-08
ADAM_WD = 0.01
ADAM_STEP = 10

MESH = pl.DeviceIdType.MESH
ANY = pl.BlockSpec(memory_space=pl.ANY)

NN = (((1,), (0,)), ((), ()))
NT = (((1,), (1,)), ((), ()))
TN = (((0,), (0,)), ((), ()))


def _pick(n, cap, mult=128):
    best = None
    for t in range(mult, min(n, cap) + 1, mult):
        if n % t == 0:
            best = t
    return best if best is not None else n


def _sigmoid(x):
    return 0.5 * jnp.tanh(0.5 * x) + 0.5


def _softplus(x):
    return jnp.maximum(x, 0.0) + jnp.log(1.0 + jnp.exp(-jnp.abs(x)))


def _silu(x):
    return x * _sigmoid(x)


def _dsilu(x):
    s = _sigmoid(x)
    return s * (1.0 + x * (1.0 - s))


_GELU_K = math.sqrt(2.0 / math.pi)


def _gelu(x):
    return 0.5 * x * (1.0 + jnp.tanh(_GELU_K * (x + 0.044715 * x * x * x)))


def _dgelu(x):
    t = jnp.tanh(_GELU_K * (x + 0.044715 * x * x * x))
    return 0.5 * (1.0 + t) + 0.5 * x * (1.0 - t * t) * _GELU_K * (1.0 + 3.0 * 0.044715 * x * x)


def _expm1(x):
    poly = x * (1.0 + x * (0.5 + x * (1.0 / 6.0 + x * (1.0 / 24.0 + x * (1.0 / 120.0 + x * (1.0 / 720.0))))))
    return jnp.where(jnp.abs(x) < 0.1, poly, jnp.exp(x) - 1.0)


def _dot(a, b, dn):
    return lax.dot_general(a.astype(bf16), b.astype(bf16), dn, preferred_element_type=f32)


def _dot_01(a, b, dn, split, terms):
    r = a if split == 0 else b
    out = None
    for _ in range(terms):
        h = r.astype(bf16)
        r = r - h.astype(f32)
        d = lax.dot_general(h if split == 0 else a.astype(bf16), b.astype(bf16) if split == 0 else h, dn,
                            preferred_element_type=f32)
        out = d if out is None else out + d
    return out


MM_VMEM_BUDGET = 40 * 2 ** 20

def mm(a, b, mode, *, name, add=None, after=None, out_dtype=f32, b_shards=False, out_shards=0, epi=None):
    bs = b.shape[1:] if b_shards else b.shape
    shard_w = b.shape[2] if b_shards else None
    bcols = bs[1] * (b.shape[0] if b_shards else 1)
    if mode == "nn":
        (m, k), (k2, n) = a.shape, (bs[0], bcols)
    elif mode == "nt":
        (m, k), (n, k2) = a.shape, (bs[0], bcols)
    else:
        (k, m), (k2, n) = a.shape, b.shape
    assert k == k2, (a.shape, b.shape, mode)
    tn = _pick(n, 1536)
    if b_shards and mode == "nn":
        tn = shard_w
    if out_shards:
        tn = n // out_shards
    isz = lambda v: jnp.dtype(v.dtype).itemsize
    if epi is not None:
        assert n <= 1536 and not out_shards
        tn = n
        epi_fn, epi_rows, epi_vecs, epi_outs = epi
        tile_bytes = sum(isz(v) for v in epi_rows) + sum(jnp.dtype(o[1]).itemsize for o in epi_outs if o[0] == "row")
    else:
        epi_rows, epi_vecs, epi_outs = [], [], []
        tile_bytes = jnp.dtype(out_dtype).itemsize
    tks = [shard_w] if (b_shards and mode == "nt") else sorted({k, _pick(k, 3328), _pick(k, 2048), _pick(k, 1024)}, reverse=True)

    def vmem_of(tm, tk):
        blocks = tm * tk * isz(a) + tk * tn * isz(b) + tm * tn * (4 * int(add is not None) + tile_bytes)
        return 2 * blocks + 4 * tm * tn * int(k > tk)

    fits = [(tk, tm) for tk in tks for tm in (_pick(m, 1536), _pick(m, 1024), _pick(m, 512)) if vmem_of(tm, tk) <= MM_VMEM_BUDGET]
    tk, tm = fits[0] if fits else (tks[-1], _pick(m, 256))
    nk = k // tk
    dn = {"nn": NN, "nt": NT, "tn": TN}[mode]
    a_spec = pl.BlockSpec((tk, tm), lambda i, j, kk: (kk, i)) if mode == "tn" else pl.BlockSpec((tm, tk), lambda i, j, kk: (i, kk))
    b_spec = pl.BlockSpec((tn, tk), lambda i, j, kk: (j, kk)) if mode == "nt" else pl.BlockSpec((tk, tn), lambda i, j, kk: (kk, j))
    if b_shards:
        b_spec = (pl.BlockSpec((None, tn, tk), lambda i, j, kk: (kk, j, 0)) if mode == "nt"
                  else pl.BlockSpec((None, tk, tn), lambda i, j, kk: (j, kk, 0)))
    o_spec = pl.BlockSpec((tm, tn), lambda i, j, kk: (i, j))
    out_shape = jax.ShapeDtypeStruct((m, n), out_dtype)
    if out_shards:
        assert add is None
        o_spec = pl.BlockSpec((None, tm, tn), lambda i, j, kk: (j, i, 0))
        out_shape = jax.ShapeDtypeStruct((out_shards, m, tn), out_dtype)
    has_add = add is not None

    n_extra = int(has_add) + int(after is not None)
    n_rows, n_vecs, n_outs = len(epi_rows), len(epi_vecs), len(epi_outs)

    def body(a_ref, b_ref, *rest):
        add_ref = rest[0] if has_add else None
        o_ref = rest[n_extra]

        def finish(r):
            if has_add:
                r = r + add_ref[...]
            if epi is None:
                o_ref[...] = r.astype(out_dtype)
            else:
                e = rest[n_extra:]
                epi_fn(r, e[:n_rows], e[n_rows:n_rows + n_vecs], e[n_rows + n_vecs:n_rows + n_vecs + n_outs],
                       pl.program_id(0) == 0)

        if nk == 1:
            finish(_dot(a_ref[...], b_ref[...], dn))
            return
        acc = rest[-1]
        kk = pl.program_id(2)

        @pl.when(kk == 0)
        def _():
            acc[...] = jnp.zeros_like(acc)

        acc[...] += _dot(a_ref[...], b_ref[...], dn)

        @pl.when(kk == nk - 1)
        def _():
            finish(acc[...])

    ins = [a, b] + ([add] if has_add else []) + ([after] if after is not None else [])
    in_specs = [a_spec, b_spec] + ([o_spec] if has_add else []) + ([ANY] if after is not None else [])
    sem0 = "parallel"
    if epi is not None:
        vec_spec = pl.BlockSpec((1, tn), lambda i, j, kk: (0, 0))
        ins += list(epi_rows) + list(epi_vecs)
        in_specs += [o_spec] * n_rows + [vec_spec] * n_vecs
        o_spec, out_shape = [], []
        for o in epi_outs:
            if o[0] == "row":
                o_spec.append(pl.BlockSpec((tm, tn), lambda i, j, kk: (i, j)))
                out_shape.append(jax.ShapeDtypeStruct((m, n), o[1]))
            elif o[0] == "vec":
                o_spec.append(vec_spec)
                out_shape.append(jax.ShapeDtypeStruct((1, n), f32))
                sem0 = "arbitrary"
            else:
                o_spec.append(pl.BlockSpec((8, 128), lambda i, j, kk: (0, 0)))
                out_shape.append(jax.ShapeDtypeStruct((8, 128), f32))
                sem0 = "arbitrary"
    return pl.pallas_call(
        body, name=name, grid=(m // tm, n // tn, nk), in_specs=in_specs, out_specs=o_spec, out_shape=out_shape,
        scratch_shapes=[pltpu.VMEM((tm, tn), f32)] if nk > 1 else [],
        compiler_params=pltpu.CompilerParams(dimension_semantics=(sem0, sem0, "arbitrary")),
    )(*ins)


def rms_fwd(x, w, *, name):
    t, d = x.shape
    tr = _pick(t, 256, 8)

    def body(x_ref, w_ref, o_ref):
        xv = x_ref[...]
        r = lax.rsqrt(jnp.mean(xv * xv, axis=-1, keepdims=True) + RMS_EPS)
        o_ref[...] = (xv * r * w_ref[...]).astype(bf16)

    return pl.pallas_call(
        body, name=name, grid=(t // tr,),
        in_specs=[pl.BlockSpec((tr, d), lambda i: (i, 0)), pl.BlockSpec((1, d), lambda i: (0, 0))],
        out_specs=pl.BlockSpec((tr, d), lambda i: (i, 0)), out_shape=jax.ShapeDtypeStruct((t, d), bf16),
    )(x, w)


def _rms_bwd_math(xv, wv, dy):
    r = lax.rsqrt(jnp.mean(xv * xv, axis=-1, keepdims=True) + RMS_EPS)
    g = dy * wv
    dx = r * g - xv * (r * r * r) * jnp.mean(g * xv, axis=-1, keepdims=True)
    dw = jnp.sum(dy * xv * r, axis=0, keepdims=True)
    return dx, dw


def epi_rms_fwd(r, rows, vecs, outs, first):
    outs[0][...] = r
    rr = lax.rsqrt(jnp.mean(r * r, axis=-1, keepdims=True) + RMS_EPS)
    outs[1][...] = (r * rr * vecs[0][...]).astype(bf16)


def epi_rms_bwd(r, rows, vecs, outs, first):
    dx, dw = _rms_bwd_math(rows[0][...], vecs[0][...], r)
    dx = dx + rows[1][...]
    outs[0][...] = dx
    if len(outs) == 3:
        outs[1][...] = dx.astype(bf16)
    dw_ref = outs[-1]

    @pl.when(first)
    def _():
        dw_ref[...] = jnp.zeros_like(dw_ref)

    dw_ref[...] += dw


def epi_loss(r, rows, vecs, outs, first):
    wv = vecs[0][...]
    rr = lax.rsqrt(jnp.mean(r * r, axis=-1, keepdims=True) + RMS_EPS)
    err = r * rr * wv - rows[0][...]
    part = 0.5 * jnp.sum(jnp.mean(err * err, axis=-1, keepdims=True), axis=0, keepdims=True)
    dx, dw = _rms_bwd_math(r, wv, err * (1.0 / r.shape[-1]))
    outs[0][...] = dx
    outs[1][...] = dx.astype(bf16)

    @pl.when(first)
    def _():
        outs[2][...] = jnp.zeros_like(outs[2])
        outs[3][...] = jnp.zeros_like(outs[3])

    outs[2][...] += dw
    outs[3][...] += part


CONV_ROWS = 512
VREG_ELEMS = 8 * 128


def _conv_chunk(tc):
    return 16 if (16 + 8) * tc * 3 > 48 * VREG_ELEMS else 32


def conv_fwd(src, col0, width, w, b, *, silu, name):
    t = src.shape[0]
    tc = _pick(math.gcd(width, col0), 768)
    assert col0 % tc == 0
    cb = col0 // tc
    r = CONV_ROWS
    ch = _conv_chunk(tc)

    def body(u_ref, w_ref, b_ref, *rest):
        ext = rest[-1]
        j = pl.program_id(1)

        @pl.when(j == 0)
        def _():
            ext[0:8, :] = jnp.zeros((8, tc), f32)

        @pl.when(j > 0)
        def _():
            ext[0:8, :] = ext[r:r + 8, :]

        ext[8:r + 8, :] = u_ref[...]
        wv = w_ref[...]
        bv = b_ref[...]

        def chunk(c, carry):
            r0 = pl.multiple_of(c * ch, ch)
            v = ext[pl.ds(r0, ch + 8), :]
            acc = bv + wv[3:4, :] * v[8:, :]
            for s in (1, 2, 3):
                acc = acc + wv[3 - s:4 - s, :] * pltpu.roll(v, s, 0)[8:, :]
            rest[0][pl.ds(r0, ch), :] = acc
            if silu:
                rest[1][pl.ds(r0, ch), :] = _silu(acc)
            return carry

        lax.fori_loop(0, r // ch, chunk, 0)

    tile = pl.BlockSpec((r, tc), lambda c, j: (j, c))
    n_out = 2 if silu else 1
    return pl.pallas_call(
        body, name=name, grid=(width // tc, t // r),
        in_specs=[pl.BlockSpec((r, tc), lambda c, j: (j, cb + c)), pl.BlockSpec((4, tc), lambda c, j: (0, c)),
                  pl.BlockSpec((1, tc), lambda c, j: (0, c))],
        out_specs=[tile] * n_out, out_shape=[jax.ShapeDtypeStruct((t, width), f32)] * n_out,
        scratch_shapes=[pltpu.VMEM((r + 8, tc), f32)],
        compiler_params=pltpu.CompilerParams(dimension_semantics=("parallel", "arbitrary")),
    )(src, w, b)


def conv_bwd(dpost, pre, src, col0, w, dst, *, name):
    t, width = dpost.shape
    tc = _pick(math.gcd(width, col0), 768)
    assert col0 % tc == 0
    cb = col0 // tc
    r = CONV_ROWS
    ch = _conv_chunk(tc)
    nt = t // r
    has_pre = pre is not None

    def body(*refs):
        refs = refs[1:]
        if has_pre:
            d_ref, p_ref, u_ref, w_ref, du_ref, dw_ref, db_ref, ext = refs
        else:
            d_ref, u_ref, w_ref, du_ref, dw_ref, db_ref, ext = refs
        j = pl.program_id(1)

        @pl.when(j == 0)
        def _():
            ext[r:r + 8, :] = jnp.zeros((8, tc), f32)
            dw_ref[...] = jnp.zeros_like(dw_ref)
            db_ref[...] = jnp.zeros_like(db_ref)

        @pl.when(j > 0)
        def _():
            ext[r:r + 8, :] = ext[0:8, :]

        dpre = d_ref[...]
        if has_pre:
            dpre = dpre * _dsilu(p_ref[...])
        ext[0:r, :] = dpre
        wv = w_ref[...]

        def chunk(c, sums):
            r0 = pl.multiple_of(c * ch, ch)
            v = ext[pl.ds(r0, ch + 8), :]
            uv = u_ref[pl.ds(r0, ch), :]
            d0 = v[0:ch, :]
            du = wv[3:4, :] * d0
            new = [None] * 5
            new[3] = sums[3] + jnp.sum(d0 * uv, axis=0, keepdims=True)
            for s in (1, 2, 3):
                sh = pltpu.roll(v, ch + 8 - s, 0)[0:ch, :]
                du = du + wv[3 - s:4 - s, :] * sh
                new[3 - s] = sums[3 - s] + jnp.sum(sh * uv, axis=0, keepdims=True)
            new[4] = sums[4] + jnp.sum(d0, axis=0, keepdims=True)
            du_ref[pl.ds(r0, ch), :] = du.astype(bf16)
            return tuple(new)

        sums = lax.fori_loop(0, r // ch, chunk, tuple(jnp.zeros((1, tc), f32) for _ in range(5)))
        for k in range(4):
            dw_ref[k:k + 1, :] += sums[k]
        db_ref[...] += sums[4]

    rev = pl.BlockSpec((r, tc), lambda c, j: (nt - 1 - j, c))
    win = pl.BlockSpec((r, tc), lambda c, j: (nt - 1 - j, cb + c))
    in_specs = [ANY, rev] + ([rev] if has_pre else []) + [win, pl.BlockSpec((4, tc), lambda c, j: (0, c))]
    ins = [dst, dpost] + ([pre] if has_pre else []) + [src, w]
    return pl.pallas_call(
        body, name=name, grid=(width // tc, nt), in_specs=in_specs,
        out_specs=[win, pl.BlockSpec((4, tc), lambda c, j: (0, c)), pl.BlockSpec((1, tc), lambda c, j: (0, c))],
        out_shape=[jax.ShapeDtypeStruct(dst.shape, bf16), jax.ShapeDtypeStruct((4, width), f32),
                   jax.ShapeDtypeStruct((1, width), f32)],
        input_output_aliases={0: 0},
        scratch_shapes=[pltpu.VMEM((r + 8, tc), f32)],
        compiler_params=pltpu.CompilerParams(dimension_semantics=("parallel", "arbitrary")),
    )(*ins)


def _ssd_common(xbc_ref, dtr_ref, dtrT_ref, par_row_ref, par_col_ref):
    l = SSM_CHUNK
    x = xbc_ref[:, 0:SSM_GROUP_W]
    bm = xbc_ref[:, SSM_GROUP_W:SSM_GROUP_W + SSM_D_STATE]
    cm = xbc_ref[:, SSM_GROUP_W + SSM_D_STATE:XBC_GROUP_W]
    par_row = par_row_ref[0]
    par_col = par_col_ref[0]
    bias_row, alog_row, d_row = par_row[0:1, :], par_row[1:2, :], par_row[2:3, :]
    bias_col, alog_col = par_col[:, 0:1], par_col[:, 1:2]
    dtr = dtr_ref[0]
    dt = _softplus(dtr + bias_row)
    dt_t = _softplus(dtrT_ref[0] + bias_col)
    a_row = -jnp.exp(alog_row)
    a_col = -jnp.exp(alog_col)
    li = lax.broadcasted_iota(jnp.int32, (l, l), 0)
    si = lax.broadcasted_iota(jnp.int32, (l, l), 1)
    tri = (li >= si).astype(f32)
    cs = _dot_01(tri, dt * a_row, NN, 1, 3)
    cs_t = _dot_01(dt_t * a_col, tri, NT, 0, 3)
    off = lax.broadcasted_iota(jnp.int32, (SSM_HPG, SSM_GROUP_W), 1) - SSM_HEAD_DIM * lax.broadcasted_iota(
        jnp.int32, (SSM_HPG, SSM_GROUP_W), 0)
    ex = ((off >= 0) & (off < SSM_HEAD_DIM)).astype(f32)
    cs_x = _dot_01(cs, ex, NN, 0, 3)
    cl_x = cs_x[l - 1:l, :]
    return dict(x=x, bm=bm, cm=cm, dtr=dtr, dt=dt, a_row=a_row, bias_row=bias_row, tri=tri, li=li, si=si, cs=cs,
                cs_t=cs_t, ex=ex, dt_x=_dot_01(dt, ex, NN, 0, 2), d_x=_dot_01(par_row, ex, NN, 0, 2)[2:3, :], e_x=jnp.exp(cs_x),
                el_x=jnp.exp(cl_x), dec_x=jnp.exp(cl_x - cs_x))


def ssd_fwd(xbc, dtr, dtr_t, par_row, par_col, *, name):
    t = xbc.shape[0]
    nc = t // SSM_CHUNK
    l, p = SSM_CHUNK, SSM_HEAD_DIM

    def body(xbc_ref, dtr_ref, dtrT_ref, prow_ref, pcol_ref, y_ref, sin_ref, state):
        @pl.when(pl.program_id(1) == 0)
        def _():
            state[...] = jnp.zeros_like(state)

        q = _ssd_common(xbc_ref, dtr_ref, dtrT_ref, prow_ref, pcol_ref)
        st = state[...]
        sin_ref[0] = st
        xd = q["x"] * q["dt_x"]
        g = _dot(q["cm"], q["bm"], NT)
        for r in range(SSM_HPG):
            sl = slice(r * p, (r + 1) * p)
            diff = q["cs"][:, r:r + 1] - q["cs_t"][r:r + 1, :]
            lm = jnp.where(q["li"] >= q["si"], jnp.exp(jnp.minimum(diff, 0.0)), 0.0)
            y_ref[:, sl] = _dot(g * lm, xd[:, sl], NN)
        y_ref[...] += q["e_x"] * _dot(q["cm"], st, NN) + q["d_x"] * q["x"]
        state[...] = q["el_x"] * st + _dot(q["bm"].T, xd * q["dec_x"], NN)

    return pl.pallas_call(
        body, name=name, grid=(SSM_GROUPS, nc),
        in_specs=[pl.BlockSpec((l, XBC_GROUP_W), lambda g, c: (c, g)),
                  pl.BlockSpec((1, l, SSM_HPG), lambda g, c: (g, c, 0)),
                  pl.BlockSpec((1, SSM_HPG, l), lambda g, c: (g, 0, c)),
                  pl.BlockSpec((1, 8, 8), lambda g, c: (g, 0, 0)),
                  pl.BlockSpec((1, 8, 8), lambda g, c: (g, 0, 0))],
        out_specs=[pl.BlockSpec((l, SSM_GROUP_W), lambda g, c: (c, g)),
                   pl.BlockSpec((1, SSM_D_STATE, SSM_GROUP_W), lambda g, c: (c, 0, g))],
        out_shape=[jax.ShapeDtypeStruct((t, SSM_D_INNER), f32),
                   jax.ShapeDtypeStruct((nc, SSM_D_STATE, SSM_D_INNER), f32)],
        scratch_shapes=[pltpu.VMEM((SSM_D_STATE, SSM_GROUP_W), f32)],
        compiler_params=pltpu.CompilerParams(dimension_semantics=("parallel", "arbitrary")),
    )(xbc, dtr, dtr_t, par_row, par_col)


def ssd_bwd(xbc, dtr, dtr_t, par_row, par_col, s_in, dy, *, name):
    t = xbc.shape[0]
    nc = t // SSM_CHUNK
    l, p = SSM_CHUNK, SSM_HEAD_DIM

    def body(xbc_ref, dtr_ref, dtrT_ref, prow_ref, pcol_ref, sin_ref, dy_ref, dxbc_ref, ddtr_ref, dpar_ref,
             dstate, yd_buf, dxd_buf):
        @pl.when(pl.program_id(1) == 0)
        def _():
            dstate[...] = jnp.zeros_like(dstate)
            dpar_ref[...] = jnp.zeros_like(dpar_ref)

        q = _ssd_common(xbc_ref, dtr_ref, dtrT_ref, prow_ref, pcol_ref)
        x, bm, cm, ex, li, si = q["x"], q["bm"], q["cm"], q["ex"], q["li"], q["si"]
        e_x, el_x, dec_x = q["e_x"], q["el_x"], q["dec_x"]
        st = sin_ref[0]
        dst = dstate[...]
        dy = dy_ref[...]
        xd = x * q["dt_x"]
        g = _dot(cm, bm, NT)
        dg = jnp.zeros((l, l), f32)
        for r in range(SSM_HPG):
            sl = slice(r * p, (r + 1) * p)
            diff = q["cs"][:, r:r + 1] - q["cs_t"][r:r + 1, :]
            lm = jnp.where(li >= si, jnp.exp(jnp.minimum(diff, 0.0)), 0.0)
            m = (g * lm).astype(bf16)
            xdh, dyh = xd[:, sl].astype(bf16), dy[:, sl].astype(bf16)
            yd_buf[:, sl] = _dot(m, xdh, NN)
            dxd_buf[:, sl] = _dot(m, dyh, TN)
            dg = dg + _dot(dyh, xdh, NT) * lm
        yd, dxd_diag = yd_buf[...], dxd_buf[...]
        yo = e_x * _dot(cm, st, NN)
        dz = e_x * dy
        wv = _dot(bm, dst, NN)
        xw = xd * wv * dec_x
        row8 = lax.broadcasted_iota(jnp.int32, (l, SSM_HPG), 0)
        dy_b, xd_b = dy.astype(bf16).astype(f32), xd.astype(bf16).astype(f32)
        dcs = _dot_01(dy_b * yd - xd_b * dxd_diag + dy * yo - xw, ex, NT, 0, 3)
        tail = jnp.sum(xw, axis=0, keepdims=True) + el_x * jnp.sum(dst * st, axis=0, keepdims=True)
        dcl = _dot_01(jnp.broadcast_to(tail, (SSM_HPG, SSM_GROUP_W)), ex, NT, 0, 3)[0:1, :]
        dcs = dcs + jnp.where(row8 == l - 1, dcl, 0.0)
        dda = _dot_01(q["tri"], dcs, TN, 1, 3)
        dxd = dxd_diag + dec_x * wv
        ddt = _dot_01(dxd * x, ex, NT, 0, 3) + dda * q["a_row"]
        ddtr = ddt * _sigmoid(q["dtr"] + q["bias_row"])
        ddtr_ref[0] = ddtr
        dd = _dot_01(jnp.broadcast_to(jnp.sum(dy * x, axis=0, keepdims=True), (SSM_HPG, SSM_GROUP_W)), ex, NT, 0, 2)[0:1, :]
        dpar_ref[0, 0:1, :] += jnp.sum(ddtr, axis=0, keepdims=True)
        dpar_ref[0, 1:2, :] += jnp.sum(dda * q["dt"], axis=0, keepdims=True) * q["a_row"]
        dpar_ref[0, 2:3, :] += dd
        dxbc_ref[:, 0:SSM_GROUP_W] = dxd * q["dt_x"] + q["d_x"] * dy
        dxbc_ref[:, SSM_GROUP_W:SSM_GROUP_W + SSM_D_STATE] = _dot(dg, cm, TN) + _dot(xd * dec_x, dst, NT)
        dxbc_ref[:, SSM_GROUP_W + SSM_D_STATE:XBC_GROUP_W] = _dot(dg, bm, NN) + _dot(dz, st, NT)
        dstate[...] = _dot(cm.T, dz, NN) + el_x * dst

    rc = lambda c: nc - 1 - c
    return pl.pallas_call(
        body, name=name, grid=(SSM_GROUPS, nc),
        in_specs=[pl.BlockSpec((l, XBC_GROUP_W), lambda g, c: (rc(c), g)),
                  pl.BlockSpec((1, l, SSM_HPG), lambda g, c: (g, rc(c), 0)),
                  pl.BlockSpec((1, SSM_HPG, l), lambda g, c: (g, 0, rc(c))),
                  pl.BlockSpec((1, 8, 8), lambda g, c: (g, 0, 0)),
                  pl.BlockSpec((1, 8, 8), lambda g, c: (g, 0, 0)),
                  pl.BlockSpec((1, SSM_D_STATE, SSM_GROUP_W), lambda g, c: (rc(c), 0, g)),
                  pl.BlockSpec((l, SSM_GROUP_W), lambda g, c: (rc(c), g))],
        out_specs=[pl.BlockSpec((l, XBC_GROUP_W), lambda g, c: (rc(c), g)),
                   pl.BlockSpec((1, l, SSM_HPG), lambda g, c: (g, rc(c), 0)),
                   pl.BlockSpec((1, 8, 8), lambda g, c: (g, 0, 0))],
        out_shape=[jax.ShapeDtypeStruct((t, SSM_CONV_DIM), f32),
                   jax.ShapeDtypeStruct((SSM_GROUPS, t, SSM_HPG), f32),
                   jax.ShapeDtypeStruct((SSM_GROUPS, 8, 8), f32)],
        scratch_shapes=[pltpu.VMEM((SSM_D_STATE, SSM_GROUP_W), f32), pltpu.VMEM((l, SSM_GROUP_W), f32),
                        pltpu.VMEM((l, SSM_GROUP_W), f32)],
        compiler_params=pltpu.CompilerParams(dimension_semantics=("parallel", "arbitrary")),
    )(xbc, dtr, dtr_t, par_row, par_col, s_in, dy)


def gnorm_fwd(y, proj, w, *, name):
    t = y.shape[0]
    tr = _pick(t, 512, 8)
    gw = SSM_GROUP_W
    zb = OFF_Z // gw

    def body(y_ref, z_ref, w_ref, o_ref):
        y2 = y_ref[...] * _silu(z_ref[...])
        r = lax.rsqrt(jnp.mean(y2 * y2, axis=-1, keepdims=True) + RMS_EPS)
        o_ref[...] = (y2 * r * w_ref[...]).astype(bf16)

    return pl.pallas_call(
        body, name=name, grid=(SSM_GROUPS, t // tr),
        in_specs=[pl.BlockSpec((tr, gw), lambda g, i: (i, g)), pl.BlockSpec((tr, gw), lambda g, i: (i, zb + g)),
                  pl.BlockSpec((1, gw), lambda g, i: (0, g))],
        out_specs=pl.BlockSpec((tr, gw), lambda g, i: (i, g)), out_shape=jax.ShapeDtypeStruct((t, SSM_D_INNER), bf16),
    )(y, proj, w)


def gnorm_bwd(y, proj, w, dout, dst, *, name):
    t = y.shape[0]
    tr = _pick(t, 512, 8)
    gw = SSM_GROUP_W
    zb = OFF_Z // gw

    def body(_, y_ref, z_ref, w_ref, do_ref, dy_ref, dz_ref, dw_ref):
        yv, zv = y_ref[...], z_ref[...]
        sz = _silu(zv)
        y2 = yv * sz
        dy2, dw = _rms_bwd_math(y2, w_ref[...], do_ref[...].astype(f32))
        dy_ref[...] = dy2 * sz
        dz_ref[...] = (dy2 * yv * _dsilu(zv)).astype(bf16)

        @pl.when(pl.program_id(1) == 0)
        def _():
            dw_ref[...] = jnp.zeros_like(dw_ref)

        dw_ref[...] += dw

    tile = pl.BlockSpec((tr, gw), lambda g, i: (i, g))
    vec = pl.BlockSpec((1, gw), lambda g, i: (0, g))
    return pl.pallas_call(
        body, name=name, grid=(SSM_GROUPS, t // tr),
        in_specs=[ANY, tile, pl.BlockSpec((tr, gw), lambda g, i: (i, zb + g)), vec, tile],
        out_specs=[tile, pl.BlockSpec((tr, gw), lambda g, i: (i, zb + g)), vec],
        out_shape=[jax.ShapeDtypeStruct((t, SSM_D_INNER), f32), jax.ShapeDtypeStruct(dst.shape, bf16),
                   jax.ShapeDtypeStruct((1, SSM_D_INNER), f32)],
        input_output_aliases={0: 1},
        compiler_params=pltpu.CompilerParams(dimension_semantics=("parallel", "arbitrary")),
    )(dst, y, proj, w, dout)


LRU_ROWS = 256


def _lru_gates(uv, wr_ref, wi_ref, br_ref, bi_ref, lam_ref):
    rg = _sigmoid(_dot(uv, wr_ref[0], NN) + br_ref[...])
    ig = _sigmoid(_dot(uv, wi_ref[0], NN) + bi_ref[...])
    sp = _softplus(-lam_ref[...])
    la = -LRU_C * rg * sp
    a = jnp.exp(la)
    s = jnp.sqrt(jnp.maximum(-_expm1(2.0 * la), 0.0))
    return rg, ig, sp, la, a, s


def lru_fwd(u, proj, w_r, b_r, w_i, b_i, lam, *, name):
    t = u.shape[0]
    r = LRU_ROWS
    lb = LRU_BLOCK
    yb = OFF_LY // lb

    def body(u_ref, y_ref, wr_ref, br_ref, wi_ref, bi_ref, lam_ref, h_ref, o_ref, carry):
        @pl.when(pl.program_id(1) == 0)
        def _():
            carry[...] = jnp.zeros_like(carry)

        uv = u_ref[...]
        _, ig, _, _, a, s = _lru_gates(uv, wr_ref, wi_ref, br_ref, bi_ref, lam_ref)
        b = s * ig * uv
        row = lax.broadcasted_iota(jnp.int32, (r, lb), 0)
        d = 1
        while d < r:
            keep = row >= d
            b = b + a * jnp.where(keep, pltpu.roll(b, d, 0), 0.0)
            a = a * jnp.where(keep, pltpu.roll(a, d, 0), 1.0)
            d *= 2
        h = b + a * carry[0:1, :]
        carry[0:1, :] = h[r - 1:r, :]
        h_ref[...] = h
        o_ref[...] = (h * _gelu(y_ref[...])).astype(bf16)

    tile = pl.BlockSpec((r, lb), lambda hb, j: (j, hb))
    vec = pl.BlockSpec((1, lb), lambda hb, j: (0, hb))
    wsp = pl.BlockSpec((1, lb, lb), lambda hb, j: (hb, 0, 0))
    return pl.pallas_call(
        body, name=name, grid=(LRU_BLOCKS, t // r),
        in_specs=[tile, pl.BlockSpec((r, lb), lambda hb, j: (j, yb + hb)), wsp, vec, wsp, vec, vec],
        out_specs=[tile, tile],
        out_shape=[jax.ShapeDtypeStruct((t, LRU_WIDTH), f32), jax.ShapeDtypeStruct((t, LRU_WIDTH), bf16)],
        scratch_shapes=[pltpu.VMEM((8, lb), f32)],
        compiler_params=pltpu.CompilerParams(dimension_semantics=("parallel", "arbitrary")),
    )(u, proj, w_r, b_r, w_i, b_i, lam)


def lru_bwd(u, proj, hseq, dout, w_r, b_r, w_i, b_i, lam, dst, *, name):
    t = u.shape[0]
    r = LRU_ROWS
    nt = t // r
    lb = LRU_BLOCK
    yb = OFF_LY // lb

    def body(_, u_ref, y_ref, h_ref, hp_ref, do_ref, wr_ref, br_ref, wi_ref, bi_ref, lam_ref,
             du_ref, dy_ref, dwr_ref, dwi_ref, dbr_ref, dbi_ref, dlam_ref, carry_dh, carry_a):
        j = pl.program_id(1)

        @pl.when(j == 0)
        def _():
            carry_dh[...] = jnp.zeros_like(carry_dh)
            carry_a[...] = jnp.zeros_like(carry_a)
            dwr_ref[...] = jnp.zeros_like(dwr_ref)
            dwi_ref[...] = jnp.zeros_like(dwi_ref)
            dbr_ref[...] = jnp.zeros_like(dbr_ref)
            dbi_ref[...] = jnp.zeros_like(dbi_ref)
            dlam_ref[...] = jnp.zeros_like(dlam_ref)

        uv = u_ref[...]
        yv = y_ref[...]
        hv = h_ref[...]
        dov = do_ref[...]
        rg, ig, sp, la, a, s = _lru_gates(uv, wr_ref, wi_ref, br_ref, bi_ref, lam_ref)
        dy_ref[...] = (dov * hv * _dgelu(yv)).astype(bf16)
        gq = dov * _gelu(yv)
        row = lax.broadcasted_iota(jnp.int32, (r, lb), 0)
        an = jnp.where(row < r - 1, pltpu.roll(a, r - 1, 0), carry_a[0:1, :])
        d = 1
        while d < r:
            keep = row < r - d
            gq = gq + an * jnp.where(keep, pltpu.roll(gq, r - d, 0), 0.0)
            an = an * jnp.where(keep, pltpu.roll(an, r - d, 0), 1.0)
            d *= 2
        dh = gq + an * carry_dh[0:1, :]
        carry_dh[0:1, :] = dh[0:1, :]
        carry_a[0:1, :] = a[0:1, :]
        first = jnp.where(j == nt - 1, 0.0, 1.0) * hp_ref[7:8, :]
        hprev = jnp.where(row >= 1, pltpu.roll(hv, 1, 0), first)
        da = dh * hprev
        iu = ig * uv
        e2 = jnp.exp(2.0 * la)
        dla = da * a - dh * iu * e2 / jnp.maximum(s, 1e-30)
        drp = dla * (-LRU_C * sp) * rg * (1.0 - rg)
        dip = dh * s * uv * ig * (1.0 - ig)
        dlam_ref[...] += jnp.sum(dla * (LRU_C * rg) * _sigmoid(-lam_ref[...]), axis=0, keepdims=True)
        du_ref[...] = dh * s * ig + _dot(drp, wr_ref[0], NT) + _dot(dip, wi_ref[0], NT)
        dwr_ref[0] += _dot(uv, drp, TN)
        dwi_ref[0] += _dot(uv, dip, TN)
        dbr_ref[...] += jnp.sum(drp, axis=0, keepdims=True)
        dbi_ref[...] += jnp.sum(dip, axis=0, keepdims=True)

    rj = lambda j: nt - 1 - j
    tile = pl.BlockSpec((r, lb), lambda hb, j: (rj(j), hb))
    vec = pl.BlockSpec((1, lb), lambda hb, j: (0, hb))
    wsp = pl.BlockSpec((1, lb, lb), lambda hb, j: (hb, 0, 0))
    hprev_spec = pl.BlockSpec((8, lb), lambda hb, j: (jnp.maximum(rj(j) * (r // 8) - 1, 0), hb))
    ywin = pl.BlockSpec((r, lb), lambda hb, j: (rj(j), yb + hb))
    return pl.pallas_call(
        body, name=name, grid=(LRU_BLOCKS, nt),
        in_specs=[ANY, tile, ywin, tile, hprev_spec, tile, wsp, vec, wsp, vec, vec],
        out_specs=[tile, ywin, wsp, wsp, vec, vec, vec],
        out_shape=[jax.ShapeDtypeStruct((t, LRU_WIDTH), f32), jax.ShapeDtypeStruct(dst.shape, bf16),
                   jax.ShapeDtypeStruct((LRU_BLOCKS, lb, lb), f32), jax.ShapeDtypeStruct((LRU_BLOCKS, lb, lb), f32),
                   jax.ShapeDtypeStruct((1, LRU_WIDTH), f32), jax.ShapeDtypeStruct((1, LRU_WIDTH), f32),
                   jax.ShapeDtypeStruct((1, LRU_WIDTH), f32)],
        input_output_aliases={0: 1},
        scratch_shapes=[pltpu.VMEM((8, lb), f32), pltpu.VMEM((8, lb), f32)],
        compiler_params=pltpu.CompilerParams(dimension_semantics=("parallel", "arbitrary")),
    )(dst, u, proj, hseq, hseq, dout, w_r, b_r, w_i, b_i, lam)


def merge_fwd(proj, bg, y_ssm, y_lru, *, name):
    t, d = y_ssm.shape
    tr = _pick(t, 256, 8)
    gb = OFF_GATES // d

    def body(gs_ref, gl_ref, bs_ref, bl_ref, ys_ref, yl_ref, o_ref):
        gs = _sigmoid(gs_ref[...] + bs_ref[...])
        gl = _sigmoid(gl_ref[...] + bl_ref[...])
        o_ref[...] = (gs * ys_ref[...].astype(f32) + gl * yl_ref[...].astype(f32)).astype(bf16)

    row = pl.BlockSpec((tr, d), lambda i: (i, 0))
    return pl.pallas_call(
        body, name=name, grid=(t // tr,),
        in_specs=[pl.BlockSpec((tr, d), lambda i: (i, gb)), pl.BlockSpec((tr, d), lambda i: (i, gb + 1)),
                  pl.BlockSpec((1, d), lambda i: (0, 0)), pl.BlockSpec((1, d), lambda i: (0, 1)), row, row],
        out_specs=row, out_shape=jax.ShapeDtypeStruct((t, d), bf16),
    )(proj, proj, bg, bg, y_ssm, y_lru)


def merge_bwd(proj, bg, y_ssm, y_lru, dmix, *, name):
    t, d = y_ssm.shape
    tr = _pick(t, 256, 8)
    gb = OFF_GATES // d

    def body(gs_ref, gl_ref, bs_ref, bl_ref, ys_ref, yl_ref, dm_ref, dg_ref, dys_ref, dyl_ref, dbg_ref):
        gs = _sigmoid(gs_ref[...] + bs_ref[...])
        gl = _sigmoid(gl_ref[...] + bl_ref[...])
        dm = dm_ref[...].astype(f32)
        dys_ref[...] = (dm * gs).astype(bf16)
        dyl_ref[...] = (dm * gl).astype(bf16)
        dgs = dm * ys_ref[...].astype(f32) * gs * (1.0 - gs)
        dgl = dm * yl_ref[...].astype(f32) * gl * (1.0 - gl)
        dg_ref[:, 0:d] = dgs.astype(bf16)
        dg_ref[:, d:2 * d] = dgl.astype(bf16)

        @pl.when(pl.program_id(0) == 0)
        def _():
            dbg_ref[...] = jnp.zeros_like(dbg_ref)

        dbg_ref[:, 0:d] += jnp.sum(dgs, axis=0, keepdims=True)
        dbg_ref[:, d:2 * d] += jnp.sum(dgl, axis=0, keepdims=True)

    row = pl.BlockSpec((tr, d), lambda i: (i, 0))
    return pl.pallas_call(
        body, name=name, grid=(t // tr,),
        in_specs=[pl.BlockSpec((tr, d), lambda i: (i, gb)), pl.BlockSpec((tr, d), lambda i: (i, gb + 1)),
                  pl.BlockSpec((1, d), lambda i: (0, 0)), pl.BlockSpec((1, d), lambda i: (0, 1)), row, row, row],
        out_specs=[pl.BlockSpec((tr, 2 * d), lambda i: (i, OFF_GATES // (2 * d))), row, row,
                   pl.BlockSpec((1, 2 * d), lambda i: (0, 0))],
        out_shape=[jax.ShapeDtypeStruct((t, PROJ_W), bf16), jax.ShapeDtypeStruct((t, d), bf16),
                   jax.ShapeDtypeStruct((t, d), bf16), jax.ShapeDtypeStruct((1, 2 * d), f32)],
        compiler_params=pltpu.CompilerParams(dimension_semantics=("arbitrary",)),
    )(proj, proj, bg, bg, y_ssm, y_lru, dmix)


def swiglu_fwd(ff, *, name):
    t = ff.shape[0]
    hd = FFN_HIDDEN
    tr = _pick(t, 128, 8)

    def body(f_ref, o_ref):
        o_ref[...] = (_silu(f_ref[:, 0:hd].astype(f32)) * f_ref[:, hd:2 * hd].astype(f32)).astype(bf16)

    return pl.pallas_call(
        body, name=name, grid=(t // tr,), in_specs=[pl.BlockSpec((tr, 2 * hd), lambda i: (i, 0))],
        out_specs=pl.BlockSpec((tr, hd), lambda i: (i, 0)), out_shape=jax.ShapeDtypeStruct((t, hd), bf16),
    )(ff)


def swiglu_bwd(ff, dact, *, name):
    t = ff.shape[0]
    hd = FFN_HIDDEN
    tr = _pick(t, 128, 8)

    def body(f_ref, d_ref, o_ref):
        gate, up, dv = f_ref[:, 0:hd].astype(f32), f_ref[:, hd:2 * hd].astype(f32), d_ref[...].astype(f32)
        o_ref[:, 0:hd] = (dv * up * _dsilu(gate)).astype(bf16)
        o_ref[:, hd:2 * hd] = (dv * _silu(gate)).astype(bf16)

    return pl.pallas_call(
        body, name=name, grid=(t // tr,),
        in_specs=[pl.BlockSpec((tr, 2 * hd), lambda i: (i, 0)), pl.BlockSpec((tr, hd), lambda i: (i, 0))],
        out_specs=pl.BlockSpec((tr, 2 * hd), lambda i: (i, 0)), out_shape=jax.ShapeDtypeStruct((t, 2 * hd), bf16),
    )(ff, dact)


def _adam_math(w, g, m, v):
    m = ADAM_B1 * m + (1.0 - ADAM_B1) * g
    v = ADAM_B2 * v + (1.0 - ADAM_B2) * (g * g)
    m_hat = m / (1.0 - ADAM_B1 ** ADAM_STEP)
    v_hat = v / (1.0 - ADAM_B2 ** ADAM_STEP)
    delta = -ADAM_LR * (m_hat / (jnp.sqrt(v_hat) + ADAM_EPS) + ADAM_WD * w)
    return delta, m, v


def _row_tile(rows, cols):
    cap = max(8, (1 << 18) // cols)
    return _pick(rows, cap, 8) if rows % 8 == 0 else rows


def adamw(w, g, m, v, *, name):
    rows, cols = w.shape
    tr = _row_tile(rows, cols)

    def body(w_ref, g_ref, m_ref, v_ref, d_ref, nm_ref, nv_ref):
        d, nm, nv = _adam_math(w_ref[...], g_ref[...], m_ref[...], v_ref[...])
        d_ref[...] = d
        nm_ref[...] = nm
        nv_ref[...] = nv

    tile = pl.BlockSpec((tr, cols), lambda i: (i, 0))
    return pl.pallas_call(
        body, name=name, grid=(rows // tr,), in_specs=[tile] * 4, out_specs=[tile] * 3,
        out_shape=[jax.ShapeDtypeStruct((rows, cols), f32)] * 3,
    )(w, g, m, v)


def adamw_many(ws, gs, ms, vs, *, name):
    n = len(ws)

    def body(*refs):
        for i in range(n):
            d, nm, nv = _adam_math(refs[i][...], refs[n + i][...], refs[2 * n + i][...], refs[3 * n + i][...])
            refs[4 * n + 3 * i][...] = d
            refs[4 * n + 3 * i + 1][...] = nm
            refs[4 * n + 3 * i + 2][...] = nv

    outs = pl.pallas_call(
        body, name=name, out_shape=[jax.ShapeDtypeStruct(w.shape, f32) for w in ws for _ in range(3)],
    )(*ws, *gs, *ms, *vs)
    return [tuple(outs[3 * i:3 * i + 3]) for i in range(n)]


def pair_add(dw, rbuf, idx, *, name):
    n, rows, cols = dw.shape
    hr = rows // 2
    tr = _row_tile(hr, cols)
    nrt = hr // tr

    def body(idx_ref, a_ref, b_ref, o_ref, own_ref):
        s = a_ref[...] + b_ref[...]
        o_ref[...] = s.astype(bf16)

        @pl.when(pl.program_id(1) == idx_ref[0])
        def _():
            own_ref[...] = s[0]

    return pl.pallas_call(
        body, name=name,
        grid_spec=pltpu.PrefetchScalarGridSpec(
            num_scalar_prefetch=1, grid=(nrt, n),
            in_specs=[pl.BlockSpec((1, tr, cols), lambda i, k, idx: (k, idx[1] * nrt + i, 0)),
                      pl.BlockSpec((1, tr, cols), lambda i, k, idx: (k, i, 0))],
            out_specs=[pl.BlockSpec((1, tr, cols), lambda i, k, idx: (k, i, 0)),
                       pl.BlockSpec((tr, cols), lambda i, k, idx: (i, 0))]),
        out_shape=[jax.ShapeDtypeStruct((n, hr, cols), bf16), jax.ShapeDtypeStruct((hr, cols), f32)],
    )(idx, dw, rbuf)


def chip_sum(own, rbuf, idx, *, name):
    hr, cols = own.shape
    tr = _row_tile(hr, cols)
    nrt = hr // tr

    def body(idx_ref, a_ref, b_ref, o_ref):
        o_ref[...] = ((a_ref[...] + b_ref[0].astype(f32)) + b_ref[1].astype(f32)) + b_ref[2].astype(f32)

    return pl.pallas_call(
        body, name=name,
        grid_spec=pltpu.PrefetchScalarGridSpec(
            num_scalar_prefetch=1, grid=(nrt,),
            in_specs=[pl.BlockSpec((tr, cols), lambda i, idx: (i, 0)),
                      pl.BlockSpec((3, tr, cols), lambda i, idx: (0, i, 0))],
            out_specs=pl.BlockSpec((tr, cols), lambda i, idx: (idx[1] * nrt + i, 0))),
        out_shape=jax.ShapeDtypeStruct((2 * hr, cols), f32),
    )(idx, own, rbuf)


def sum8(rbuf, *, name):
    n, rows, cols = rbuf.shape
    tr = _row_tile(rows, cols * n)

    def body(a_ref, o_ref):
        acc = a_ref[0]
        for k in range(1, n):
            acc = acc + a_ref[k]
        o_ref[...] = acc

    return pl.pallas_call(
        body, name=name, grid=(rows // tr,), in_specs=[pl.BlockSpec((n, tr, cols), lambda i: (0, i, 0))],
        out_specs=pl.BlockSpec((tr, cols), lambda i: (i, 0)), out_shape=jax.ShapeDtypeStruct((rows, cols), f32),
    )(rbuf)


def _coords():
    return lax.axis_index("x"), lax.axis_index("y"), lax.axis_index("c")


def _other_chips(x, y):
    return [(1 - x, y), (x, 1 - y), (1 - x, 1 - y)]


def gather_weights(shards, *, name):
    n = len(shards)
    halves = [s.shape[0] // 2 for s in shards]

    def body(*refs):
        ins, outs = refs[:n], refs[n:2 * n]
        send1, recv1, send2, recv2 = refs[2 * n:]
        x, y, c = _coords()
        me = 2 * x + y
        chips = _other_chips(x, y)
        sibling = (x, y, 1 - c)

        def half(i, k, hc):
            return outs[i].at[k, pl.ds(hc * halves[i], halves[i]), :]

        def ici(i, j):
            return pltpu.make_async_remote_copy(
                src_ref=ins[i].at[pl.ds(c * halves[i], halves[i]), :], dst_ref=half(i, me, c),
                send_sem=send1.at[i, j], recv_sem=recv1.at[i, j], device_id=(*chips[j], c), device_id_type=MESH)

        def landed(i, j):
            kj = 2 * chips[j][0] + chips[j][1]
            return pltpu.make_async_remote_copy(
                src_ref=half(i, kj, c), dst_ref=half(i, kj, c),
                send_sem=send2.at[i, j], recv_sem=recv1.at[i, j], device_id=sibling, device_id_type=MESH)

        def from_sibling(i, j):
            kj = 2 * chips[j][0] + chips[j][1]
            return pltpu.make_async_remote_copy(
                src_ref=half(i, kj, 1 - c), dst_ref=half(i, kj, 1 - c),
                send_sem=send2.at[i, j], recv_sem=recv2.at[i, j], device_id=sibling, device_id_type=MESH)

        def d2d(i, j):
            kj = 2 * chips[j][0] + chips[j][1]
            return pltpu.make_async_remote_copy(
                src_ref=half(i, kj, c), dst_ref=half(i, kj, c),
                send_sem=send2.at[i, j], recv_sem=recv2.at[i, j], device_id=sibling, device_id_type=MESH)

        for j in range(3):
            for i in range(n):
                ici(i, j).start()
        for j in range(3):
            for i in range(n):
                landed(i, j).wait_recv()
                d2d(i, j).start()
        for j in range(3):
            for i in range(n):
                from_sibling(i, j).wait_recv()
        for j in range(3):
            for i in range(n):
                ici(i, j).wait_send()
                d2d(i, j).wait_send()

    return pl.pallas_call(
        body, name=name, in_specs=[ANY] * n, out_specs=[ANY] * n,
        out_shape=[jax.ShapeDtypeStruct((N_CHIPS,) + s.shape, s.dtype) for s in shards],
        scratch_shapes=[pltpu.SemaphoreType.DMA((n, 3))] * 4,
    )(*shards)


def pair_exchange(grads, *, name):
    n = len(grads)
    halves = [g.shape[1] // 2 for g in grads]

    def body(*refs):
        ins, outs = refs[:n], refs[n:2 * n]
        send, recv = refs[2 * n:]
        x, y, c = _coords()
        cps = [pltpu.make_async_remote_copy(
            src_ref=ins[i].at[:, pl.ds((1 - c) * halves[i], halves[i]), :], dst_ref=outs[i],
            send_sem=send.at[i], recv_sem=recv.at[i], device_id=(x, y, 1 - c), device_id_type=MESH) for i in range(n)]
        for cp in cps:
            cp.start()
        for cp in cps:
            cp.wait()

    return pl.pallas_call(
        body, name=name, in_specs=[ANY] * n, out_specs=[ANY] * n,
        out_shape=[jax.ShapeDtypeStruct((N_CHIPS, g.shape[1] // 2, g.shape[2]), g.dtype) for g in grads],
        scratch_shapes=[pltpu.SemaphoreType.DMA((n,))] * 2,
    )(*grads)


def pair_gather(bufs, *, name):
    n = len(bufs)

    def body(*refs):
        ins, outs = refs[:n], refs[n:2 * n]
        send, recv = refs[2 * n:]
        x, y, c = _coords()
        cps = []
        for i in range(n):
            hr = ins[i].shape[0] // 2
            cps.append(pltpu.make_async_remote_copy(
                src_ref=ins[i].at[pl.ds(c * hr, hr), :], dst_ref=outs[i].at[pl.ds(c * hr, hr), :],
                send_sem=send.at[i], recv_sem=recv.at[i], device_id=(x, y, 1 - c), device_id_type=MESH))
        for cp in cps:
            cp.start()
        for i in range(n):
            hr = ins[i].shape[0] // 2
            pltpu.make_async_remote_copy(
                src_ref=ins[i].at[pl.ds((1 - c) * hr, hr), :], dst_ref=outs[i].at[pl.ds((1 - c) * hr, hr), :],
                send_sem=send.at[i], recv_sem=recv.at[i], device_id=(x, y, 1 - c), device_id_type=MESH).wait_recv()
        for cp in cps:
            cp.wait_send()

    return pl.pallas_call(
        body, name=name, in_specs=[ANY] * n, out_specs=[ANY] * n,
        out_shape=[jax.ShapeDtypeStruct(b.shape, b.dtype) for b in bufs],
        input_output_aliases={i: i for i in range(n)},
        scratch_shapes=[pltpu.SemaphoreType.DMA((n,))] * 2,
    )(*bufs)


def all_exchange(buf, *, name):
    rows, cols = buf.shape

    def body(in_ref, out_ref, send, recv):
        x, y, c = _coords()
        me = 4 * x + 2 * y + c
        cps = []
        for d in range(1, 8):
            px = 1 - x if d & 4 else x
            py = 1 - y if d & 2 else y
            pc = 1 - c if d & 1 else c
            cps.append(pltpu.make_async_remote_copy(
                src_ref=in_ref, dst_ref=out_ref.at[me], send_sem=send.at[d - 1], recv_sem=recv.at[d - 1],
                device_id=(px, py, pc), device_id_type=MESH))
        for cp in cps:
            cp.start()
        for d in range(1, 8):
            px = 1 - x if d & 4 else x
            py = 1 - y if d & 2 else y
            pc = 1 - c if d & 1 else c
            src = 4 * px + 2 * py + pc
            pltpu.make_async_remote_copy(
                src_ref=in_ref, dst_ref=out_ref.at[src], send_sem=send.at[d - 1], recv_sem=recv.at[d - 1],
                device_id=(px, py, pc), device_id_type=MESH).wait_recv()
        for cp in cps:
            cp.wait_send()

    return pl.pallas_call(
        body, name=name, in_specs=[ANY], out_specs=ANY,
        out_shape=jax.ShapeDtypeStruct((8, rows, cols), buf.dtype),
        scratch_shapes=[pltpu.SemaphoreType.DMA((7,)), pltpu.SemaphoreType.DMA((7,))],
    )(buf)


HBM = pl.BlockSpec(memory_space=pltpu.HBM)
SEM = pl.BlockSpec(memory_space=pltpu.SEMAPHORE)
EFFECT = pltpu.SideEffectType.DATAFLOW_SIDE_EFFECTING


def split_start(arrays, after, copies, sem_shape, *, name):
    na = len(arrays)

    def body(*refs):
        for cp in copies(refs[:na], refs[na + 1], refs[na + 2]):
            cp.start()
        refs[-1][...] = jnp.zeros((8, 128), f32)

    outs = pl.pallas_call(
        body, name=name,
        out_shape=(pltpu.SemaphoreType.DMA(sem_shape), pltpu.SemaphoreType.DMA(sem_shape),
                   *[pltpu.HBM(a.shape, a.dtype) for a in arrays], jax.ShapeDtypeStruct((8, 128), f32)),
        in_specs=[HBM] * na + [ANY], out_specs=(SEM, SEM, *[HBM] * na, pl.BlockSpec(memory_space=pltpu.VMEM)),
        input_output_aliases={i: 2 + i for i in range(na)},
        compiler_params=pltpu.CompilerParams(has_side_effects=EFFECT),
    )(*[pltpu.with_memory_space_constraint(a, pltpu.HBM) for a in arrays], after)
    return outs[0], outs[1], list(outs[2:2 + na]), outs[-1]


def split_wait(send, recv, arrays, after, copies, *, name):
    na = len(arrays)

    def body(*refs):
        for cp in copies(refs[:na], refs[na], refs[na + 1]):
            cp.wait_send()
            cp.wait_recv()

    outs = pl.pallas_call(
        body, name=name, out_shape=tuple(pltpu.HBM(a.shape, a.dtype) for a in arrays),
        in_specs=[HBM] * na + [SEM, SEM, ANY], out_specs=tuple([HBM] * na),
        input_output_aliases={i: i for i in range(na)},
        compiler_params=pltpu.CompilerParams(has_side_effects=EFFECT),
    )(*arrays, send, recv, after)
    return list(outs)


def gather_copies(n):
    def copies(refs, send, recv):
        x, y, c = _coords()
        me = 2 * x + y
        chips = _other_chips(x, y)
        return [pltpu.make_async_remote_copy(
            src_ref=refs[i], dst_ref=refs[n + i].at[me], send_sem=send.at[3 * i + j], recv_sem=recv.at[3 * i + j],
            device_id=(*chips[j], c), device_id_type=MESH) for j in range(3) for i in range(n)]
    return copies


def pair_copies(n):
    def copies(refs, send, recv):
        x, y, c = _coords()
        cps = []
        for i in range(n):
            hr = refs[i].shape[1] // 2
            cps.append(pltpu.make_async_remote_copy(
                src_ref=refs[i].at[:, pl.ds((1 - c) * hr, hr), :], dst_ref=refs[n + i], send_sem=send.at[i],
                recv_sem=recv.at[i], device_id=(x, y, 1 - c), device_id_type=MESH))
        return cps
    return copies


def all_copies():
    def copies(refs, send, recv):
        x, y, c = _coords()
        me = 4 * x + 2 * y + c
        cps = []
        for d in range(1, 8):
            peer = (1 - x if d & 4 else x, 1 - y if d & 2 else y, 1 - c if d & 1 else c)
            cps.append(pltpu.make_async_remote_copy(
                src_ref=refs[0], dst_ref=refs[1].at[me], send_sem=send.at[d - 1], recv_sem=recv.at[d - 1],
                device_id=peer, device_id_type=MESH))
        return cps
    return copies


def reduce_copies(n):
    def copies(refs, send, recv):
        x, y, c = _coords()
        chips = _other_chips(x, y)
        return [pltpu.make_async_remote_copy(
            src_ref=refs[i].at[2 * chips[j][0] + chips[j][1]], dst_ref=refs[n + i].at[j],
            send_sem=send.at[3 * i + j], recv_sem=recv.at[3 * i + j], device_id=(*chips[j], c), device_id_type=MESH)
            for j in range(3) for i in range(n)]
    return copies


def _pack(arrs):
    flat = []
    for a in arrs:
        v = a.reshape(-1).astype(f32)
        pad = (-v.shape[0]) % 128
        flat.append(jnp.pad(v, (0, pad)) if pad else v)
    v = jnp.concatenate(flat)
    rows = v.shape[0] // 128
    pad_rows = (-rows) % 256
    v = v.reshape(rows, 128)
    return jnp.pad(v, ((0, pad_rows), (0, 0))) if pad_rows else v


def _unpack(buf, shapes):
    out, row = [], 0
    for s in shapes:
        size = math.prod(s)
        rows = -(-size // 128)
        out.append(buf[row:row + rows].reshape(-1)[:size].reshape(s))
        row += rows
    return out


def _ref_of_perm():
    ref = np.arange(IN_PROJ_DIM)
    xbc = ref[4096:7168]
    xbc_p = [np.concatenate([xbc[g * 512:(g + 1) * 512], xbc[2048 + g * 128:2048 + (g + 1) * 128],
                             xbc[2560 + g * 128:2560 + (g + 1) * 128]]) for g in range(SSM_GROUPS)]
    return np.concatenate([ref[0:2048], ref[2048:4096], ref[7200:8480], ref[8480:9760], ref[7168:7200],
                           -np.ones(DT_PAD_W - SSM_HEADS, np.int64)] + xbc_p)


def _runs(vals):
    out, start = [], 0
    for i in range(1, len(vals) + 1):
        if i == len(vals) or not (vals[i] == vals[i - 1] + 1 or (vals[i] < 0 and vals[i - 1] < 0)):
            out.append((start, int(vals[start]), i - start))
            start = i
    return out


def _perm_in_from_shards(g):
    ref_of_perm = _ref_of_perm()
    sw = IN_PROJ_DIM // N_CHIPS
    parts = []
    for _, first, length in _runs(ref_of_perm):
        if first < 0:
            parts.append(jnp.zeros((g.shape[1], length), g.dtype))
            continue
        lo = first
        while lo < first + length:
            k = lo // sw
            hi = min(first + length, (k + 1) * sw)
            parts.append(g[k, :, lo - k * sw:hi - k * sw])
            lo = hi
    return jnp.concatenate(parts, axis=-1)


def _unperm_in_to_shards(w):
    ref_of_perm = _ref_of_perm()
    perm_of_ref = np.zeros(IN_PROJ_DIM, np.int64)
    perm_of_ref[ref_of_perm[ref_of_perm >= 0]] = np.nonzero(ref_of_perm >= 0)[0]
    sw = IN_PROJ_DIM // N_CHIPS
    shards = []
    for k in range(N_CHIPS):
        runs = _runs(perm_of_ref[k * sw:(k + 1) * sw])
        shards.append(jnp.concatenate([w[:, first:first + length] for _, first, length in runs], axis=-1))
    return jnp.stack(shards)


def _perm_xbc_cols(w):
    parts = []
    for g in range(SSM_GROUPS):
        parts += [w[..., g * 512:(g + 1) * 512], w[..., 2048 + g * 128:2048 + (g + 1) * 128],
                  w[..., 2560 + g * 128:2560 + (g + 1) * 128]]
    return jnp.concatenate(parts, axis=-1)


def _unperm_xbc_cols(w):
    xs = [w[..., g * XBC_GROUP_W:g * XBC_GROUP_W + 512] for g in range(SSM_GROUPS)]
    bs = [w[..., g * XBC_GROUP_W + 512:g * XBC_GROUP_W + 640] for g in range(SSM_GROUPS)]
    cs = [w[..., g * XBC_GROUP_W + 640:(g + 1) * XBC_GROUP_W] for g in range(SSM_GROUPS)]
    return jnp.concatenate(xs + bs + cs, axis=-1)


def _from_col_shards(w):
    n, r, c = w.shape
    return jnp.transpose(w, (1, 0, 2)).reshape(r, n * c)


def kernel(x, norm1_w, w_in, b_branch_gate, ssm_conv_w, ssm_conv_b, ssm_dt_bias, ssm_a_log, ssm_d, ssm_norm_w, w_out_ssm, lru_conv_w, lru_conv_b, lru_w_r, lru_b_r, lru_w_i, lru_b_i, lru_lambda, w_out_lru, w_out, norm2_w, w_ffn_in, w_ffn_out, norm_f_w, loss_target, m_norm1_w, m_w_in, m_b_branch_gate, m_ssm_conv_w, m_ssm_conv_b, m_ssm_dt_bias, m_ssm_a_log, m_ssm_d, m_ssm_norm_w, m_w_out_ssm, m_lru_conv_w, m_lru_conv_b, m_lru_w_r, m_lru_b_r, m_lru_w_i, m_lru_b_i, m_lru_lambda, m_w_out_lru, m_w_out, m_norm2_w, m_w_ffn_in, m_w_ffn_out, m_norm_f_w, v_norm1_w, v_w_in, v_b_branch_gate, v_ssm_conv_w, v_ssm_conv_b, v_ssm_dt_bias, v_ssm_a_log, v_ssm_d, v_ssm_norm_w, v_w_out_ssm, v_lru_conv_w, v_lru_conv_b, v_lru_w_r, v_lru_b_r, v_lru_w_i, v_lru_b_i, v_lru_lambda, v_w_out_lru, v_w_out, v_norm2_w, v_w_ffn_in, v_w_ffn_out, v_norm_f_w):
    xi, yi, ci = lax.axis_index("x"), lax.axis_index("y"), lax.axis_index("c")
    me = 2 * xi + yi
    idx = jnp.stack([me, ci]).astype(jnp.int32)
    x2 = x[0]
    tgt = loss_target[0]

    big_names = ["w_in", "w_out_ssm", "w_out_lru", "w_out", "w_ffn_in", "w_ffn_out"]
    big_w = dict(w_in=w_in[0], w_out_ssm=w_out_ssm[0], w_out_lru=w_out_lru[0], w_out=w_out[0], w_ffn_in=w_ffn_in[0],
                 w_ffn_out=w_ffn_out[0])
    big_m = dict(w_in=m_w_in[0], w_out_ssm=m_w_out_ssm[0], w_out_lru=m_w_out_lru[0], w_out=m_w_out[0],
                 w_ffn_in=m_w_ffn_in[0], w_ffn_out=m_w_ffn_out[0])
    big_v = dict(w_in=v_w_in[0], w_out_ssm=v_w_out_ssm[0], w_out_lru=v_w_out_lru[0], w_out=v_w_out[0],
                 w_ffn_in=v_w_ffn_in[0], w_ffn_out=v_w_ffn_out[0])
    conv_pad = jnp.zeros((16, 768), f32).at[0:4, :].set(ssm_conv_w[0]).at[8:12, 0:320].set(lru_conv_w[0])
    mine = [big_w["w_in"].astype(bf16), conv_pad]
    gathered = gather_weights(mine, name="gather_weights")
    g_in, g_conv = [lax.dynamic_update_index_in_dim(g, s, me, 0) for g, s in zip(gathered, mine)]
    w_in_p = _perm_in_from_shards(g_in)
    late_names = big_names[1:]
    late = [big_w[k].astype(bf16) for k in late_names]
    late_lands = [lax.empty((N_CHIPS,) + s.shape, bf16) for s in late]
    g_send, g_recv, g_arrays, g_token = split_start(late + late_lands, g_conv, gather_copies(5), (15,),
                                                    name="gather_late_start")
    ssm_cw_full = _from_col_shards(g_conv[:, 0:4, :])
    lru_cw_full = _from_col_shards(g_conv[:, 8:12, 0:320])
    ssm_cw_p = _perm_xbc_cols(ssm_cw_full)
    ssm_cb_p = _perm_xbc_cols(ssm_conv_b)

    par = jnp.stack([ssm_dt_bias[0], ssm_a_log[0], ssm_d[0]], axis=0).reshape(3, SSM_GROUPS, SSM_HPG)
    par_row = jnp.zeros((SSM_GROUPS, 8, 8), f32).at[:, 0:3, :].set(jnp.transpose(par, (1, 0, 2)))
    par_col = jnp.transpose(par_row, (0, 2, 1))

    hn1 = rms_fwd(x2, norm1_w + g_token[0:1, 0:1], name="rms1_fwd")
    proj = mm(hn1, w_in_p, "nn", name="in_proj")
    t = x2.shape[0]
    dtr = jnp.transpose(proj[:, OFF_DT:OFF_DT + 32].reshape(t, SSM_GROUPS, SSM_HPG), (1, 0, 2))
    dtr_t = jnp.transpose(dtr, (0, 2, 1))
    xbc_pre, xbc_post = conv_fwd(proj, OFF_XBC, SSM_CONV_DIM, ssm_cw_p, ssm_cb_p, silu=True, name="ssm_conv_fwd")
    y_ssd, s_in = ssd_fwd(xbc_post, dtr, dtr_t, par_row, par_col, name="ssd_fwd")
    yn = gnorm_fwd(y_ssd, proj, ssm_norm_w, name="gnorm_fwd")
    g_arrays = split_wait(g_send, g_recv, g_arrays, yn, gather_copies(5), name="gather_late_wait")
    g_out_ssm, g_out_lru, g_out, g_ffn_in, g_ffn_out = [
        lax.dynamic_update_index_in_dim(g, s, me, 0) for g, s in zip(g_arrays[5:], late)]
    w_out_ssm_f = g_out_ssm.reshape(SSM_D_INNER, D_MODEL)
    w_out_lru_f = g_out_lru.reshape(LRU_WIDTH, D_MODEL)
    w_out_f = g_out.reshape(D_MODEL, D_MODEL)
    w_ffn_out_f = g_ffn_out.reshape(FFN_HIDDEN, D_MODEL)
    y_ssm = mm(yn, w_out_ssm_f, "nn", out_dtype=bf16, name="out_ssm")
    (u_lru,) = conv_fwd(proj, OFF_LX, LRU_WIDTH, lru_cw_full, lru_conv_b, silu=False, name="lru_conv_fwd")
    h_lru, o_lru = lru_fwd(u_lru, proj, lru_w_r[0], lru_b_r, lru_w_i[0], lru_b_i, lru_lambda, name="lru_fwd")
    y_lru = mm(o_lru, w_out_lru_f, "nn", out_dtype=bf16, name="out_lru")
    mix = merge_fwd(proj, b_branch_gate, y_ssm, y_lru, name="merge_fwd")
    h1, hn2 = mm(mix, w_out_f, "nn", add=x2, name="out_proj",
                 epi=(epi_rms_fwd, [], [norm2_w], [("row", f32), ("row", bf16)]))
    ff = mm(hn2, g_ffn_in, "nn", b_shards=True, out_dtype=bf16, name="ffn_in")
    act = swiglu_fwd(ff, name="swiglu_fwd")
    dh2, dh2_b, d_norm_f, loss_tile = mm(act, w_ffn_out_f, "nn", add=h1, name="ffn_out",
                                         epi=(epi_loss, [tgt], [norm_f_w.reshape(1, D_MODEL)],
                                              [("row", f32), ("row", bf16), ("vec",), ("tile",)]))
    loss = lax.psum(loss_tile[0, 0], ("x", "y", "c"))

    d_w_ffn_out = mm(act, dh2_b, "tn", name="d_w_ffn_out")
    dact = mm(dh2_b, w_ffn_out_f, "nt", out_dtype=bf16, name="d_act")
    dff = swiglu_bwd(ff, dact, name="swiglu_bwd")
    d_w_ffn_in = mm(hn2, dff, "tn", out_shards=N_CHIPS, name="d_w_ffn_in")
    dh1, dh1_b, d_norm2 = mm(dff, g_ffn_in, "nt", b_shards=True, name="d_hn2",
                             epi=(epi_rms_bwd, [h1, dh2], [norm2_w], [("row", f32), ("row", bf16), ("vec",)]))
    d_w_out = mm(mix, dh1_b, "tn", name="d_w_out")
    dmix = mm(dh1_b, w_out_f, "nt", out_dtype=bf16, name="d_mix")
    dproj, dy_ssm, dy_lru, d_bg = merge_bwd(proj, b_branch_gate, y_ssm, y_lru, dmix, name="merge_bwd")
    d_w_out_ssm = mm(yn, dy_ssm, "tn", name="d_w_out_ssm")
    d_w_out_lru = mm(o_lru, dy_lru, "tn", name="d_w_out_lru")
    early_g = [d_w_out_ssm.reshape(N_CHIPS, 512, D_MODEL), d_w_out_lru.reshape(N_CHIPS, 320, D_MODEL),
               d_w_out.reshape(N_CHIPS, 256, D_MODEL), d_w_ffn_in, d_w_ffn_out.reshape(N_CHIPS, 704, D_MODEL)]
    p_lands = [lax.empty((N_CHIPS, g.shape[1] // 2, g.shape[2]), f32) for g in early_g]
    p_send, p_recv, p_arrays, p_token = split_start(early_g + p_lands, early_g[0], pair_copies(5), (5,),
                                                    name="pair_early_start")
    dyn = mm(dy_ssm, w_out_ssm_f, "nt", out_dtype=bf16, after=p_token, name="d_yn")
    dy_ssd, dproj, d_ssm_norm = gnorm_bwd(y_ssd, proj, ssm_norm_w, dyn, dproj, name="gnorm_bwd")
    p_arrays = split_wait(p_send, p_recv, p_arrays, dy_ssd, pair_copies(5), name="pair_early_wait")
    e_pairs = [pair_add(g, rb, idx, name="pair_add_" + k) for g, rb, k in zip(p_arrays[:5], p_arrays[5:], late_names)]
    e_lands = [lax.empty((3,) + p[0].shape[1:], bf16) for p in e_pairs]
    e_send, e_recv, e_arrays, e_token = split_start([p[0] for p in e_pairs] + e_lands, e_pairs[0][1], reduce_copies(5),
                                                    (15,), name="reduce_early_start")
    dxbc_post, ddtr, dpar = ssd_bwd(xbc_post, dtr, dtr_t, par_row + e_token[0:1, 0:1], par_col, s_in, dy_ssd,
                                    name="ssd_bwd")
    dproj, d_ssm_cw_p, d_ssm_cb_p = conv_bwd(dxbc_post, xbc_pre, proj, OFF_XBC, ssm_cw_p, dproj, name="ssm_conv_bwd")
    do_lru = mm(dy_lru, w_out_lru_f, "nt", name="d_o_lru")
    du_lru, dproj, d_w_r, d_w_i, d_b_r, d_b_i, d_lam = lru_bwd(u_lru, proj, h_lru, do_lru, lru_w_r[0], lru_b_r, lru_w_i[0],
                                                               lru_b_i, lru_lambda, dproj, name="lru_bwd")
    dproj, d_lru_cw, d_lru_cb = conv_bwd(du_lru, None, proj, OFF_LX, lru_cw_full, dproj, name="lru_conv_bwd")
    ddt_cols = jnp.transpose(ddtr, (1, 0, 2)).reshape(t, SSM_HEADS).astype(bf16)
    ddt_cols = jnp.pad(ddt_cols, ((0, 0), (0, DT_PAD_W - SSM_HEADS)))
    dproj = lax.dynamic_update_slice(dproj, ddt_cols, (0, OFF_DT))

    d_ssm_cw = _unperm_xbc_cols(d_ssm_cw_p)
    d_ssm_cb = _unperm_xbc_cols(d_ssm_cb_p)
    dpar_h = jnp.transpose(dpar[:, 0:3, :], (1, 0, 2)).reshape(3, SSM_HEADS)
    small_names = ["norm1_w", "b_branch_gate", "ssm_conv_b", "ssm_dt_bias", "ssm_a_log", "ssm_d", "ssm_norm_w",
                   "lru_conv_b", "lru_w_r", "lru_b_r", "lru_w_i", "lru_b_i", "lru_lambda", "norm2_w", "norm_f_w"]
    small_g = dict(norm1_w=jnp.zeros_like(norm1_w), b_branch_gate=d_bg, ssm_conv_b=d_ssm_cb, ssm_dt_bias=dpar_h[0:1], ssm_a_log=dpar_h[1:2],
                   ssm_d=dpar_h[2:3], ssm_norm_w=d_ssm_norm, lru_conv_b=d_lru_cb, lru_w_r=d_w_r[None], lru_b_r=d_b_r,
                   lru_w_i=d_w_i[None], lru_b_i=d_b_i, lru_lambda=d_lam, norm2_w=d_norm2, norm_f_w=d_norm_f.reshape(D_MODEL))
    small_w = dict(norm1_w=norm1_w, b_branch_gate=b_branch_gate, ssm_conv_b=ssm_conv_b, ssm_dt_bias=ssm_dt_bias,
                   ssm_a_log=ssm_a_log, ssm_d=ssm_d, ssm_norm_w=ssm_norm_w, lru_conv_b=lru_conv_b, lru_w_r=lru_w_r,
                   lru_b_r=lru_b_r, lru_w_i=lru_w_i, lru_b_i=lru_b_i, lru_lambda=lru_lambda, norm2_w=norm2_w, norm_f_w=norm_f_w)
    small_m = dict(norm1_w=m_norm1_w, b_branch_gate=m_b_branch_gate, ssm_conv_b=m_ssm_conv_b, ssm_dt_bias=m_ssm_dt_bias,
                   ssm_a_log=m_ssm_a_log, ssm_d=m_ssm_d, ssm_norm_w=m_ssm_norm_w, lru_conv_b=m_lru_conv_b, lru_w_r=m_lru_w_r,
                   lru_b_r=m_lru_b_r, lru_w_i=m_lru_w_i, lru_b_i=m_lru_b_i, lru_lambda=m_lru_lambda, norm2_w=m_norm2_w,
                   norm_f_w=m_norm_f_w)
    small_v = dict(norm1_w=v_norm1_w, b_branch_gate=v_b_branch_gate, ssm_conv_b=v_ssm_conv_b, ssm_dt_bias=v_ssm_dt_bias,
                   ssm_a_log=v_ssm_a_log, ssm_d=v_ssm_d, ssm_norm_w=v_ssm_norm_w, lru_conv_b=v_lru_conv_b, lru_w_r=v_lru_w_r,
                   lru_b_r=v_lru_b_r, lru_w_i=v_lru_w_i, lru_b_i=v_lru_b_i, lru_lambda=v_lru_lambda, norm2_w=v_norm2_w,
                   norm_f_w=v_norm_f_w)
    shapes = [small_w[k].shape for k in small_names]
    conv_shapes = [(4, SSM_CONV_DIM), (4, LRU_WIDTH)]
    g_pack = _pack([small_g[k] for k in small_names] + [d_ssm_cw, d_lru_cw])
    s_send, s_recv, s_arrays, s_token = split_start([g_pack, lax.empty((8,) + g_pack.shape, f32)], g_pack, all_copies(),
                                                    (7,), name="small_start")
    d_w_in_p = mm(hn1, dproj, "tn", after=s_token, name="d_w_in")

    d_w_in_s = _unperm_in_to_shards(d_w_in_p)
    (l_sib,) = pair_exchange([d_w_in_s], name="pair_exchange_late")
    l_pair = pair_add(d_w_in_s, l_sib, idx, name="pair_add_w_in")
    l_land = lax.empty((3,) + l_pair[0].shape[1:], bf16)
    l_send, l_recv, l_arrays, l_token = split_start([l_pair[0], l_land], l_pair[1], reduce_copies(1), (3,),
                                                    name="reduce_late_start")
    grad_x, d_norm1 = mm(dproj, w_in_p, "nt", after=l_token, name="d_hn1",
                         epi=(epi_rms_bwd, [x2, dh1], [norm1_w], [("row", f32), ("vec",)]))

    e_arrays = split_wait(e_send, e_recv, e_arrays, d_norm1, reduce_copies(5), name="reduce_early_wait")
    e_half = [chip_sum(p[1], rb, idx, name="chip_sum_" + k) for p, rb, k in zip(e_pairs, e_arrays[5:], late_names)]
    big_out = {}
    for k, g in zip(late_names, pair_gather(e_half, name="pair_gather_early")):
        big_out[k] = (g,) + tuple(adamw(big_w[k], g, big_m[k], big_v[k], name="adamw_" + k))

    s_arrays = split_wait(s_send, s_recv, s_arrays, d_norm1, all_copies(), name="small_wait")
    g_sum = sum8(lax.dynamic_update_index_in_dim(s_arrays[1], g_pack, 2 * me + ci, 0), name="sum8")
    n1 = d_norm1.reshape(8, 128)
    n1_sum = sum8(lax.dynamic_update_index_in_dim(all_exchange(n1, name="all_exchange_norm1"), n1, 2 * me + ci, 0),
                  name="sum8_norm1")
    g_sum = lax.dynamic_update_slice(g_sum, n1_sum, (0, 0))
    g_small = _unpack(g_sum, shapes + conv_shapes)
    g_small[-2] = lax.dynamic_slice_in_dim(g_small[-2], me * 768, 768, axis=1)
    g_small[-1] = lax.dynamic_slice_in_dim(g_small[-1], me * 320, 320, axis=1)
    all_names = small_names + ["ssm_conv_w", "lru_conv_w"]
    small_w.update(ssm_conv_w=ssm_conv_w[0], lru_conv_w=lru_conv_w[0])
    small_m.update(ssm_conv_w=m_ssm_conv_w[0], lru_conv_w=m_lru_conv_w[0])
    small_v.update(ssm_conv_w=v_ssm_conv_w[0], lru_conv_w=v_lru_conv_w[0])
    as2d = lambda a: a.reshape(-1, a.shape[-1])
    upd = adamw_many([as2d(small_w[k]) for k in all_names], [as2d(g) for g in g_small],
                     [as2d(small_m[k]) for k in all_names], [as2d(small_v[k]) for k in all_names], name="adamw_small")
    small_out = {}
    for k, g, u in zip(all_names, g_small, upd):
        small_out[k] = (g,) + tuple(o.reshape(g.shape) for o in u)
    l_arrays = split_wait(l_send, l_recv, l_arrays, upd[0][0], reduce_copies(1), name="reduce_late_wait")
    l_half = chip_sum(l_pair[1], l_arrays[1], idx, name="chip_sum_w_in")
    (g_w_in,) = pair_gather([l_half], name="pair_gather_late")
    big_out["w_in"] = (g_w_in,) + tuple(adamw(big_w["w_in"], g_w_in, big_m["w_in"], big_v["w_in"], name="adamw_w_in"))

    order = ["norm1_w", "w_in", "b_branch_gate", "ssm_conv_w", "ssm_conv_b", "ssm_dt_bias", "ssm_a_log", "ssm_d", "ssm_norm_w",
             "w_out_ssm", "lru_conv_w", "lru_conv_b", "lru_w_r", "lru_b_r", "lru_w_i", "lru_b_i", "lru_lambda", "w_out_lru",
             "w_out", "norm2_w", "w_ffn_in", "w_ffn_out", "norm_f_w"]
    outs = [loss, grad_x[None]]
    for which in range(4):
        for k in order:
            if k in big_out:
                outs.append(big_out[k][which][None])
            elif k in ("ssm_conv_w", "lru_conv_w"):
                outs.append(small_out[k][which][None])
            else:
                outs.append(small_out[k][which])
    return tuple(outs)
```

```python
import functools
import math

import jax
import jax.numpy as jnp
import numpy as np
from jax import lax
from jax.experimental import pallas as pl
from jax.experimental.pallas import tpu as pltpu

f32 = jnp.float32
bf16 = jnp.bfloat16

D_MODEL = 1024
SSM_D_INNER = 2048
SSM_HEADS = 32
SSM_HEAD_DIM = 64
SSM_GROUPS = 4
SSM_HPG = 8
SSM_D_STATE = 128
SSM_CHUNK = 128
SSM_GROUP_W = 512
SSM_CONV_DIM = 3072
XBC_GROUP_W = 768
LRU_WIDTH = 1280
LRU_BLOCKS = 10
LRU_BLOCK = 128
LRU_C = 8.0
FFN_HIDDEN = 2816
RMS_EPS = 1e-6
IN_PROJ_DIM = 9760
N_CHIPS = 4

OFF_GATES = 0
OFF_Z = 2048
OFF_LX = 4096
OFF_LY = 5376
OFF_DT = 6656
DT_PAD_W = 256
OFF_XBC = 6912
PROJ_W = 9984

ADAM_LR = 0.001
ADAM_B1 = 0.9
ADAM_B2 = 0.999
ADAM_EPS = 1e-08
ADAM_WD = 0.01
ADAM_STEP = 10

MESH = pl.DeviceIdType.MESH
ANY = pl.BlockSpec(memory_space=pl.ANY)

NN = (((1,), (0,)), ((), ()))
NT = (((1,), (1,)), ((), ()))
TN = (((0,), (0,)), ((), ()))


def _pick(n, cap, mult=128):
    best = None
    for t in range(mult, min(n, cap) + 1, mult):
        if n % t == 0:
            best = t
    return best if best is not None else n


def _sigmoid(x):
    return 0.5 * jnp.tanh(0.5 * x) + 0.5


def _softplus(x):
    return jnp.maximum(x, 0.0) + jnp.log(1.0 + jnp.exp(-jnp.abs(x)))


def _silu(x):
    return x * _sigmoid(x)


def _dsilu(x):
    s = _sigmoid(x)
    return s * (1.0 + x * (1.0 - s))


_GELU_K = math.sqrt(2.0 / math.pi)


def _gelu(x):
    return 0.5 * x * (1.0 + jnp.tanh(_GELU_K * (x + 0.044715 * x * x * x)))


def _dgelu(x):
    t = jnp.tanh(_GELU_K * (x + 0.044715 * x * x * x))
    return 0.5 * (1.0 + t) + 0.5 * x * (1.0 - t * t) * _GELU_K * (1.0 + 3.0 * 0.044715 * x * x)


def _expm1(x):
    poly = x * (1.0 + x * (0.5 + x * (1.0 / 6.0 + x * (1.0 / 24.0 + x * (1.0 / 120.0 + x * (1.0 / 720.0))))))
    return jnp.where(jnp.abs(x) < 0.1, poly, jnp.exp(x) - 1.0)


def _dot(a, b, dn):
    return lax.dot_general(a.astype(bf16), b.astype(bf16), dn, preferred_element_type=f32)


def _dot_01(a, b, dn, split, terms):
    r = a if split == 0 else b
    out = None
    for _ in range(terms):
        h = r.astype(bf16)
        r = r - h.astype(f32)
        d = lax.dot_general(h if split == 0 else a.astype(bf16), b.astype(bf16) if split == 0 else h, dn,
                            preferred_element_type=f32)
        out = d if out is None else out + d
    return out


MM_VMEM_BUDGET = 48 * 2 ** 20

def mm(a, b, mode, *, name, add=None, after=None, out_dtype=f32, b_shards=False, out_shards=0, epi=None):
    bs = b.shape[1:] if b_shards else b.shape
    shard_w = b.shape[2] if b_shards else None
    bcols = bs[1] * (b.shape[0] if b_shards else 1)
    if mode == "nn":
        (m, k), (k2, n) = a.shape, (bs[0], bcols)
    elif mode == "nt":
        (m, k), (n, k2) = a.shape, (bs[0], bcols)
    else:
        (k, m), (k2, n) = a.shape, b.shape
    assert k == k2, (a.shape, b.shape, mode)
    tn = _pick(n, 1536)
    if b_shards and mode == "nn":
        tn = shard_w
    if out_shards:
        tn = n // out_shards
    isz = lambda v: jnp.dtype(v.dtype).itemsize
    if epi is not None:
        assert n <= 1536 and not out_shards
        tn = n
        epi_fn, epi_rows, epi_vecs, epi_outs = epi
        tile_bytes = sum(isz(v) for v in epi_rows) + sum(jnp.dtype(o[1]).itemsize for o in epi_outs if o[0] == "row")
    else:
        epi_rows, epi_vecs, epi_outs = [], [], []
        tile_bytes = jnp.dtype(out_dtype).itemsize
    tks = [shard_w] if (b_shards and mode == "nt") else sorted({k, _pick(k, 3328), _pick(k, 2048), _pick(k, 1024)}, reverse=True)

    def vmem_of(tm, tk):
        blocks = tm * tk * isz(a) + tk * tn * isz(b) + tm * tn * (4 * int(add is not None) + tile_bytes)
        return 2 * blocks + 4 * tm * tn * int(k > tk)

    fits = [(tk, tm) for tk in tks for tm in (_pick(m, 1536), _pick(m, 1024), _pick(m, 512)) if vmem_of(tm, tk) <= MM_VMEM_BUDGET]
    tk, tm = fits[0] if fits else (tks[-1], _pick(m, 256))
    nk = k // tk
    dn = {"nn": NN, "nt": NT, "tn": TN}[mode]
    a_spec = pl.BlockSpec((tk, tm), lambda i, j, kk: (kk, i)) if mode == "tn" else pl.BlockSpec((tm, tk), lambda i, j, kk: (i, kk))
    b_spec = pl.BlockSpec((tn, tk), lambda i, j, kk: (j, kk)) if mode == "nt" else pl.BlockSpec((tk, tn), lambda i, j, kk: (kk, j))
    if b_shards:
        b_spec = (pl.BlockSpec((None, tn, tk), lambda i, j, kk: (kk, j, 0)) if mode == "nt"
                  else pl.BlockSpec((None, tk, tn), lambda i, j, kk: (j, kk, 0)))
    o_spec = pl.BlockSpec((tm, tn), lambda i, j, kk: (i, j))
    out_shape = jax.ShapeDtypeStruct((m, n), out_dtype)
    if out_shards:
        assert add is None
        o_spec = pl.BlockSpec((None, tm, tn), lambda i, j, kk: (j, i, 0))
        out_shape = jax.ShapeDtypeStruct((out_shards, m, tn), out_dtype)
    has_add = add is not None

    n_extra = int(has_add) + int(after is not None)
    n_rows, n_vecs, n_outs = len(epi_rows), len(epi_vecs), len(epi_outs)

    def body(a_ref, b_ref, *rest):
        add_ref = rest[0] if has_add else None
        o_ref = rest[n_extra]

        def finish(r):
            if has_add:
                r = r + add_ref[...]
            if epi is None:
                o_ref[...] = r.astype(out_dtype)
            else:
                e = rest[n_extra:]
                epi_fn(r, e[:n_rows], e[n_rows:n_rows + n_vecs], e[n_rows + n_vecs:n_rows + n_vecs + n_outs],
                       pl.program_id(0) == 0)

        if nk == 1:
            finish(_dot(a_ref[...], b_ref[...], dn))
            return
        acc = rest[-1]
        kk = pl.program_id(2)

        @pl.when(kk == 0)
        def _():
            acc[...] = jnp.zeros_like(acc)

        acc[...] += _dot(a_ref[...], b_ref[...], dn)

        @pl.when(kk == nk - 1)
        def _():
            finish(acc[...])

    ins = [a, b] + ([add] if has_add else []) + ([after] if after is not None else [])
    in_specs = [a_spec, b_spec] + ([o_spec] if has_add else []) + ([ANY] if after is not None else [])
    sem0 = "parallel"
    if epi is not None:
        vec_spec = pl.BlockSpec((1, tn), lambda i, j, kk: (0, 0))
        ins += list(epi_rows) + list(epi_vecs)
        in_specs += [o_spec] * n_rows + [vec_spec] * n_vecs
        o_spec, out_shape = [], []
        for o in epi_outs:
            if o[0] == "row":
                o_spec.append(pl.BlockSpec((tm, tn), lambda i, j, kk: (i, j)))
                out_shape.append(jax.ShapeDtypeStruct((m, n), o[1]))
            elif o[0] == "vec":
                o_spec.append(vec_spec)
                out_shape.append(jax.ShapeDtypeStruct((1, n), f32))
                sem0 = "arbitrary"
            else:
                o_spec.append(pl.BlockSpec((8, 128), lambda i, j, kk: (0, 0)))
                out_shape.append(jax.ShapeDtypeStruct((8, 128), f32))
                sem0 = "arbitrary"
    return pl.pallas_call(
        body, name=name, grid=(m // tm, n // tn, nk), in_specs=in_specs, out_specs=o_spec, out_shape=out_shape,
        scratch_shapes=[pltpu.VMEM((tm, tn), f32)] if nk > 1 else [],
        compiler_params=pltpu.CompilerParams(dimension_semantics=(sem0, sem0, "arbitrary")),
    )(*ins)


def rms_fwd(x, w, *, name):
    t, d = x.shape
    tr = _pick(t, 256, 8)

    def body(x_ref, w_ref, o_ref):
        xv = x_ref[...]
        r = lax.rsqrt(jnp.mean(xv * xv, axis=-1, keepdims=True) + RMS_EPS)
        o_ref[...] = (xv * r * w_ref[...]).astype(bf16)

    return pl.pallas_call(
        body, name=name, grid=(t // tr,),
        in_specs=[pl.BlockSpec((tr, d), lambda i: (i, 0)), pl.BlockSpec((1, d), lambda i: (0, 0))],
        out_specs=pl.BlockSpec((tr, d), lambda i: (i, 0)), out_shape=jax.ShapeDtypeStruct((t, d), bf16),
    )(x, w)


def _rms_bwd_math(xv, wv, dy):
    r = lax.rsqrt(jnp.mean(xv * xv, axis=-1, keepdims=True) + RMS_EPS)
    g = dy * wv
    dx = r * g - xv * (r * r * r) * jnp.mean(g * xv, axis=-1, keepdims=True)
    dw = jnp.sum(dy * xv * r, axis=0, keepdims=True)
    return dx, dw


def epi_rms_fwd(r, rows, vecs, outs, first):
    outs[0][...] = r
    rr = lax.rsqrt(jnp.mean(r * r, axis=-1, keepdims=True) + RMS_EPS)
    outs[1][...] = (r * rr * vecs[0][...]).astype(bf16)


def epi_rms_bwd(r, rows, vecs, outs, first):
    dx, dw = _rms_bwd_math(rows[0][...], vecs[0][...], r)
    dx = dx + rows[1][...]
    outs[0][...] = dx
    if len(outs) == 3:
        outs[1][...] = dx.astype(bf16)
    dw_ref = outs[-1]

    @pl.when(first)
    def _():
        dw_ref[...] = jnp.zeros_like(dw_ref)

    dw_ref[...] += dw


def epi_loss(r, rows, vecs, outs, first):
    wv = vecs[0][...]
    rr = lax.rsqrt(jnp.mean(r * r, axis=-1, keepdims=True) + RMS_EPS)
    err = r * rr * wv - rows[0][...]
    part = 0.5 * jnp.sum(jnp.mean(err * err, axis=-1, keepdims=True), axis=0, keepdims=True)
    dx, dw = _rms_bwd_math(r, wv, err * (1.0 / r.shape[-1]))
    outs[0][...] = dx
    outs[1][...] = dx.astype(bf16)

    @pl.when(first)
    def _():
        outs[2][...] = jnp.zeros_like(outs[2])
        outs[3][...] = jnp.zeros_like(outs[3])

    outs[2][...] += dw
    outs[3][...] += part


CONV_ROWS = 512
VREG_ELEMS = 8 * 128


def _conv_chunk(tc):
    return 16 if (16 + 8) * tc * 3 > 48 * VREG_ELEMS else 32


def conv_fwd(src, col0, width, w, b, *, silu, name):
    t = src.shape[0]
    tc = _pick(math.gcd(width, col0), 768)
    assert col0 % tc == 0
    cb = col0 // tc
    r = CONV_ROWS
    ch = _conv_chunk(tc)

    def body(u_ref, w_ref, b_ref, *rest):
        ext = rest[-1]
        j = pl.program_id(1)

        @pl.when(j == 0)
        def _():
            ext[0:8, :] = jnp.zeros((8, tc), f32)

        @pl.when(j > 0)
        def _():
            ext[0:8, :] = ext[r:r + 8, :]

        ext[8:r + 8, :] = u_ref[...]
        wv = w_ref[...]
        bv = b_ref[...]

        def chunk(c, carry):
            r0 = pl.multiple_of(c * ch, ch)
            v = ext[pl.ds(r0, ch + 8), :]
            acc = bv + wv[3:4, :] * v[8:, :]
            for s in (1, 2, 3):
                acc = acc + wv[3 - s:4 - s, :] * pltpu.roll(v, s, 0)[8:, :]
            rest[0][pl.ds(r0, ch), :] = acc
            if silu:
                rest[1][pl.ds(r0, ch), :] = _silu(acc)
            return carry

        lax.fori_loop(0, r // ch, chunk, 0)

    tile = pl.BlockSpec((r, tc), lambda c, j: (j, c))
    n_out = 2 if silu else 1
    return pl.pallas_call(
        body, name=name, grid=(width // tc, t // r),
        in_specs=[pl.BlockSpec((r, tc), lambda c, j: (j, cb + c)), pl.BlockSpec((4, tc), lambda c, j: (0, c)),
                  pl.BlockSpec((1, tc), lambda c, j: (0, c))],
        out_specs=[tile] * n_out, out_shape=[jax.ShapeDtypeStruct((t, width), f32)] * n_out,
        scratch_shapes=[pltpu.VMEM((r + 8, tc), f32)],
        compiler_params=pltpu.CompilerParams(dimension_semantics=("parallel", "arbitrary")),
    )(src, w, b)


def conv_bwd(dpost, pre, src, col0, w, dst, *, name):
    t, width = dpost.shape
    tc = _pick(math.gcd(width, col0), 768)
    assert col0 % tc == 0
    cb = col0 // tc
    r = CONV_ROWS
    ch = _conv_chunk(tc)
    nt = t // r
    has_pre = pre is not None

    def body(*refs):
        refs = refs[1:]
        if has_pre:
            d_ref, p_ref, u_ref, w_ref, du_ref, dw_ref, db_ref, ext = refs
        else:
            d_ref, u_ref, w_ref, du_ref, dw_ref, db_ref, ext = refs
        j = pl.program_id(1)

        @pl.when(j == 0)
        def _():
            ext[r:r + 8, :] = jnp.zeros((8, tc), f32)
            dw_ref[...] = jnp.zeros_like(dw_ref)
            db_ref[...] = jnp.zeros_like(db_ref)

        @pl.when(j > 0)
        def _():
            ext[r:r + 8, :] = ext[0:8, :]

        dpre = d_ref[...]
        if has_pre:
            dpre = dpre * _dsilu(p_ref[...])
        ext[0:r, :] = dpre
        wv = w_ref[...]

        def fold(p):
            out = p[0:8, :]
            for i in range(1, ch // 8):
                out = out + p[8 * i:8 * i + 8, :]
            return out

        def chunk(c, sums):
            r0 = pl.multiple_of(c * ch, ch)
            v = ext[pl.ds(r0, ch + 8), :]
            uv = u_ref[pl.ds(r0, ch), :]
            d0 = v[0:ch, :]
            du = wv[3:4, :] * d0
            new = [None] * 5
            new[3] = sums[3] + fold(d0 * uv)
            for s in (1, 2, 3):
                sh = pltpu.roll(v, ch + 8 - s, 0)[0:ch, :]
                du = du + wv[3 - s:4 - s, :] * sh
                new[3 - s] = sums[3 - s] + fold(sh * uv)
            new[4] = sums[4] + fold(d0)
            du_ref[pl.ds(r0, ch), :] = du.astype(bf16)
            return tuple(new)

        sums = lax.fori_loop(0, r // ch, chunk, tuple(jnp.zeros((8, tc), f32) for _ in range(5)))
        for k in range(4):
            dw_ref[k:k + 1, :] += jnp.sum(sums[k], axis=0, keepdims=True)
        db_ref[...] += jnp.sum(sums[4], axis=0, keepdims=True)

    rev = pl.BlockSpec((r, tc), lambda c, j: (nt - 1 - j, c))
    win = pl.BlockSpec((r, tc), lambda c, j: (nt - 1 - j, cb + c))
    in_specs = [ANY, rev] + ([rev] if has_pre else []) + [win, pl.BlockSpec((4, tc), lambda c, j: (0, c))]
    ins = [dst, dpost] + ([pre] if has_pre else []) + [src, w]
    return pl.pallas_call(
        body, name=name, grid=(width // tc, nt), in_specs=in_specs,
        out_specs=[win, pl.BlockSpec((4, tc), lambda c, j: (0, c)), pl.BlockSpec((1, tc), lambda c, j: (0, c))],
        out_shape=[jax.ShapeDtypeStruct(dst.shape, bf16), jax.ShapeDtypeStruct((4, width), f32),
                   jax.ShapeDtypeStruct((1, width), f32)],
        input_output_aliases={0: 0},
        scratch_shapes=[pltpu.VMEM((r + 8, tc), f32)],
        compiler_params=pltpu.CompilerParams(dimension_semantics=("parallel", "arbitrary")),
    )(*ins)


def _ssd_common(xbc_ref, dtr_ref, dtrT_ref, par_row_ref, par_col_ref):
    l = SSM_CHUNK
    x = xbc_ref[:, 0:SSM_GROUP_W]
    bm = xbc_ref[:, SSM_GROUP_W:SSM_GROUP_W + SSM_D_STATE]
    cm = xbc_ref[:, SSM_GROUP_W + SSM_D_STATE:XBC_GROUP_W]
    par_row = par_row_ref[0]
    par_col = par_col_ref[0]
    bias_row, alog_row, d_row = par_row[0:1, :], par_row[1:2, :], par_row[2:3, :]
    bias_col, alog_col = par_col[:, 0:1], par_col[:, 1:2]
    dtr = dtr_ref[0]
    dt = _softplus(dtr + bias_row)
    dt_t = _softplus(dtrT_ref[0] + bias_col)
    a_row = -jnp.exp(alog_row)
    a_col = -jnp.exp(alog_col)
    li = lax.broadcasted_iota(jnp.int32, (l, l), 0)
    si = lax.broadcasted_iota(jnp.int32, (l, l), 1)
    tri = (li >= si).astype(f32)
    cs = _dot_01(tri, dt * a_row, NN, 1, 3)
    cs_t = _dot_01(dt_t * a_col, tri, NT, 0, 3)
    off = lax.broadcasted_iota(jnp.int32, (SSM_HPG, SSM_GROUP_W), 1) - SSM_HEAD_DIM * lax.broadcasted_iota(
        jnp.int32, (SSM_HPG, SSM_GROUP_W), 0)
    ex = ((off >= 0) & (off < SSM_HEAD_DIM)).astype(f32)
    cs_x = _dot_01(cs, ex, NN, 0, 3)
    cl_x = cs_x[l - 1:l, :]
    return dict(x=x, bm=bm, cm=cm, dtr=dtr, dt=dt, a_row=a_row, bias_row=bias_row, tri=tri, li=li, si=si, cs=cs,
                cs_t=cs_t, ex=ex, dt_x=_dot_01(dt, ex, NN, 0, 2), d_x=_dot_01(par_row, ex, NN, 0, 2)[2:3, :], e_x=jnp.exp(cs_x),
                el_x=jnp.exp(cl_x), dec_x=jnp.exp(cl_x - cs_x))


def ssd_fwd(xbc, dtr, dtr_t, par_row, par_col, *, name):
    t = xbc.shape[0]
    nc = t // SSM_CHUNK
    l, p = SSM_CHUNK, SSM_HEAD_DIM

    def body(xbc_ref, dtr_ref, dtrT_ref, prow_ref, pcol_ref, y_ref, sin_ref, state):
        @pl.when(pl.program_id(1) == 0)
        def _():
            state[...] = jnp.zeros_like(state)

        q = _ssd_common(xbc_ref, dtr_ref, dtrT_ref, prow_ref, pcol_ref)
        st = state[...]
        sin_ref[0] = st
        xd = q["x"] * q["dt_x"]
        g = _dot(q["cm"], q["bm"], NT)
        for r in range(SSM_HPG):
            sl = slice(r * p, (r + 1) * p)
            diff = q["cs"][:, r:r + 1] - q["cs_t"][r:r + 1, :]
            lm = jnp.where(q["li"] >= q["si"], jnp.exp(jnp.minimum(diff, 0.0)), 0.0)
            y_ref[:, sl] = _dot(g * lm, xd[:, sl], NN)
        y_ref[...] += q["e_x"] * _dot(q["cm"], st, NN) + q["d_x"] * q["x"]
        state[...] = q["el_x"] * st + _dot(q["bm"].T, xd * q["dec_x"], NN)

    return pl.pallas_call(
        body, name=name, grid=(SSM_GROUPS, nc),
        in_specs=[pl.BlockSpec((l, XBC_GROUP_W), lambda g, c: (c, g)),
                  pl.BlockSpec((1, l, SSM_HPG), lambda g, c: (g, c, 0)),
                  pl.BlockSpec((1, SSM_HPG, l), lambda g, c: (g, 0, c)),
                  pl.BlockSpec((1, 8, 8), lambda g, c: (g, 0, 0)),
                  pl.BlockSpec((1, 8, 8), lambda g, c: (g, 0, 0))],
        out_specs=[pl.BlockSpec((l, SSM_GROUP_W), lambda g, c: (c, g)),
                   pl.BlockSpec((1, SSM_D_STATE, SSM_GROUP_W), lambda g, c: (c, 0, g))],
        out_shape=[jax.ShapeDtypeStruct((t, SSM_D_INNER), f32),
                   jax.ShapeDtypeStruct((nc, SSM_D_STATE, SSM_D_INNER), f32)],
        scratch_shapes=[pltpu.VMEM((SSM_D_STATE, SSM_GROUP_W), f32)],
        compiler_params=pltpu.CompilerParams(dimension_semantics=("parallel", "arbitrary")),
    )(xbc, dtr, dtr_t, par_row, par_col)


def ssd_bwd(xbc, dtr, dtr_t, par_row, par_col, s_in, dy, *, name):
    t = xbc.shape[0]
    nc = t // SSM_CHUNK
    l, p = SSM_CHUNK, SSM_HEAD_DIM

    def body(xbc_ref, dtr_ref, dtrT_ref, prow_ref, pcol_ref, sin_ref, dy_ref, dxbc_ref, ddtr_ref, dpar_ref,
             dstate, yd_buf, dxd_buf):
        @pl.when(pl.program_id(1) == 0)
        def _():
            dstate[...] = jnp.zeros_like(dstate)
            dpar_ref[...] = jnp.zeros_like(dpar_ref)

        q = _ssd_common(xbc_ref, dtr_ref, dtrT_ref, prow_ref, pcol_ref)
        x, bm, cm, ex, li, si = q["x"], q["bm"], q["cm"], q["ex"], q["li"], q["si"]
        e_x, el_x, dec_x = q["e_x"], q["el_x"], q["dec_x"]
        st = sin_ref[0]
        dst = dstate[...]
        dy = dy_ref[...]
        xd = x * q["dt_x"]
        g = _dot(cm, bm, NT)
        dg = jnp.zeros((l, l), f32)
        for r in range(SSM_HPG):
            sl = slice(r * p, (r + 1) * p)
            diff = q["cs"][:, r:r + 1] - q["cs_t"][r:r + 1, :]
            lm = jnp.where(li >= si, jnp.exp(jnp.minimum(diff, 0.0)), 0.0)
            m = (g * lm).astype(bf16)
            xdh, dyh = xd[:, sl].astype(bf16), dy[:, sl].astype(bf16)
            yd_buf[:, sl] = _dot(m, xdh, NN)
            dxd_buf[:, sl] = _dot(m, dyh, TN)
            dg = dg + _dot(dyh, xdh, NT) * lm
        yd, dxd_diag = yd_buf[...], dxd_buf[...]
        yo = e_x * _dot(cm, st, NN)
        dz = e_x * dy
        wv = _dot(bm, dst, NN)
        xw = xd * wv * dec_x
        row8 = lax.broadcasted_iota(jnp.int32, (l, SSM_HPG), 0)
        dy_b, xd_b = dy.astype(bf16).astype(f32), xd.astype(bf16).astype(f32)
        dcs = _dot_01(dy_b * yd - xd_b * dxd_diag + dy * yo - xw, ex, NT, 0, 3)
        tail = jnp.sum(xw, axis=0, keepdims=True) + el_x * jnp.sum(dst * st, axis=0, keepdims=True)
        dcl = _dot_01(jnp.broadcast_to(tail, (SSM_HPG, SSM_GROUP_W)), ex, NT, 0, 3)[0:1, :]
        dcs = dcs + jnp.where(row8 == l - 1, dcl, 0.0)
        dda = _dot_01(q["tri"], dcs, TN, 1, 3)
        dxd = dxd_diag + dec_x * wv
        ddt = _dot_01(dxd * x, ex, NT, 0, 3) + dda * q["a_row"]
        ddtr = ddt * _sigmoid(q["dtr"] + q["bias_row"])
        ddtr_ref[0] = ddtr
        dd = _dot_01(jnp.broadcast_to(jnp.sum(dy * x, axis=0, keepdims=True), (SSM_HPG, SSM_GROUP_W)), ex, NT, 0, 2)[0:1, :]
        dpar_ref[0, 0:1, :] += jnp.sum(ddtr, axis=0, keepdims=True)
        dpar_ref[0, 1:2, :] += jnp.sum(dda * q["dt"], axis=0, keepdims=True) * q["a_row"]
        dpar_ref[0, 2:3, :] += dd
        dxbc_ref[:, 0:SSM_GROUP_W] = dxd * q["dt_x"] + q["d_x"] * dy
        dxbc_ref[:, SSM_GROUP_W:SSM_GROUP_W + SSM_D_STATE] = _dot(dg, cm, TN) + _dot(xd * dec_x, dst, NT)
        dxbc_ref[:, SSM_GROUP_W + SSM_D_STATE:XBC_GROUP_W] = _dot(dg, bm, NN) + _dot(dz, st, NT)
        dstate[...] = _dot(cm.T, dz, NN) + el_x * dst

    rc = lambda c: nc - 1 - c
    return pl.pallas_call(
        body, name=name, grid=(SSM_GROUPS, nc),
        in_specs=[pl.BlockSpec((l, XBC_GROUP_W), lambda g, c: (rc(c), g)),
                  pl.BlockSpec((1, l, SSM_HPG), lambda g, c: (g, rc(c), 0)),
                  pl.BlockSpec((1, SSM_HPG, l), lambda g, c: (g, 0, rc(c))),
                  pl.BlockSpec((1, 8, 8), lambda g, c: (g, 0, 0)),
                  pl.BlockSpec((1, 8, 8), lambda g, c: (g, 0, 0)),
                  pl.BlockSpec((1, SSM_D_STATE, SSM_GROUP_W), lambda g, c: (rc(c), 0, g)),
                  pl.BlockSpec((l, SSM_GROUP_W), lambda g, c: (rc(c), g))],
        out_specs=[pl.BlockSpec((l, XBC_GROUP_W), lambda g, c: (rc(c), g)),
                   pl.BlockSpec((1, l, SSM_HPG), lambda g, c: (g, rc(c), 0)),
                   pl.BlockSpec((1, 8, 8), lambda g, c: (g, 0, 0))],
        out_shape=[jax.ShapeDtypeStruct((t, SSM_CONV_DIM), f32),
                   jax.ShapeDtypeStruct((SSM_GROUPS, t, SSM_HPG), f32),
                   jax.ShapeDtypeStruct((SSM_GROUPS, 8, 8), f32)],
        scratch_shapes=[pltpu.VMEM((SSM_D_STATE, SSM_GROUP_W), f32), pltpu.VMEM((l, SSM_GROUP_W), f32),
                        pltpu.VMEM((l, SSM_GROUP_W), f32)],
        compiler_params=pltpu.CompilerParams(dimension_semantics=("parallel", "arbitrary")),
    )(xbc, dtr, dtr_t, par_row, par_col, s_in, dy)


def gnorm_fwd(y, proj, w, *, name):
    t = y.shape[0]
    tr = _pick(t, 512, 8)
    gw = SSM_GROUP_W
    zb = OFF_Z // gw

    def body(y_ref, z_ref, w_ref, o_ref):
        y2 = y_ref[...] * _silu(z_ref[...])
        r = lax.rsqrt(jnp.mean(y2 * y2, axis=-1, keepdims=True) + RMS_EPS)
        o_ref[...] = (y2 * r * w_ref[...]).astype(bf16)

    return pl.pallas_call(
        body, name=name, grid=(SSM_GROUPS, t // tr),
        in_specs=[pl.BlockSpec((tr, gw), lambda g, i: (i, g)), pl.BlockSpec((tr, gw), lambda g, i: (i, zb + g)),
                  pl.BlockSpec((1, gw), lambda g, i: (0, g))],
        out_specs=pl.BlockSpec((tr, gw), lambda g, i: (i, g)), out_shape=jax.ShapeDtypeStruct((t, SSM_D_INNER), bf16),
    )(y, proj, w)


def gnorm_bwd(y, proj, w, dout, dst, *, name):
    t = y.shape[0]
    tr = _pick(t, 512, 8)
    gw = SSM_GROUP_W
    zb = OFF_Z // gw

    def body(_, y_ref, z_ref, w_ref, do_ref, dy_ref, dz_ref, dw_ref):
        yv, zv = y_ref[...], z_ref[...]
        sz = _silu(zv)
        y2 = yv * sz
        dy2, dw = _rms_bwd_math(y2, w_ref[...], do_ref[...].astype(f32))
        dy_ref[...] = dy2 * sz
        dz_ref[...] = (dy2 * yv * _dsilu(zv)).astype(bf16)

        @pl.when(pl.program_id(1) == 0)
        def _():
            dw_ref[...] = jnp.zeros_like(dw_ref)

        dw_ref[...] += dw

    tile = pl.BlockSpec((tr, gw), lambda g, i: (i, g))
    vec = pl.BlockSpec((1, gw), lambda g, i: (0, g))
    return pl.pallas_call(
        body, name=name, grid=(SSM_GROUPS, t // tr),
        in_specs=[ANY, tile, pl.BlockSpec((tr, gw), lambda g, i: (i, zb + g)), vec, tile],
        out_specs=[tile, pl.BlockSpec((tr, gw), lambda g, i: (i, zb + g)), vec],
        out_shape=[jax.ShapeDtypeStruct((t, SSM_D_INNER), f32), jax.ShapeDtypeStruct(dst.shape, bf16),
                   jax.ShapeDtypeStruct((1, SSM_D_INNER), f32)],
        input_output_aliases={0: 1},
        compiler_params=pltpu.CompilerParams(dimension_semantics=("parallel", "arbitrary")),
    )(dst, y, proj, w, dout)


LRU_ROWS = 256


def _lru_gates(uv, wr_ref, wi_ref, br_ref, bi_ref, lam_ref):
    rg = _sigmoid(_dot(uv, wr_ref[0], NN) + br_ref[...])
    ig = _sigmoid(_dot(uv, wi_ref[0], NN) + bi_ref[...])
    sp = _softplus(-lam_ref[...])
    la = -LRU_C * rg * sp
    a = jnp.exp(la)
    s = jnp.sqrt(jnp.maximum(-_expm1(2.0 * la), 0.0))
    return rg, ig, sp, la, a, s


def lru_fwd(u, proj, w_r, b_r, w_i, b_i, lam, *, name):
    t = u.shape[0]
    r = LRU_ROWS
    lb = LRU_BLOCK
    yb = OFF_LY // lb

    def body(u_ref, y_ref, wr_ref, br_ref, wi_ref, bi_ref, lam_ref, h_ref, o_ref, carry):
        @pl.when(pl.program_id(1) == 0)
        def _():
            carry[...] = jnp.zeros_like(carry)

        uv = u_ref[...]
        _, ig, _, _, a, s = _lru_gates(uv, wr_ref, wi_ref, br_ref, bi_ref, lam_ref)
        b = s * ig * uv
        row = lax.broadcasted_iota(jnp.int32, (r, lb), 0)
        d = 1
        while d < r:
            keep = row >= d
            b = b + a * jnp.where(keep, pltpu.roll(b, d, 0), 0.0)
            a = a * jnp.where(keep, pltpu.roll(a, d, 0), 1.0)
            d *= 2
        h = b + a * carry[0:1, :]
        carry[0:1, :] = h[r - 1:r, :]
        h_ref[...] = h
        o_ref[...] = (h * _gelu(y_ref[...])).astype(bf16)

    tile = pl.BlockSpec((r, lb), lambda hb, j: (j, hb))
    vec = pl.BlockSpec((1, lb), lambda hb, j: (0, hb))
    wsp = pl.BlockSpec((1, lb, lb), lambda hb, j: (hb, 0, 0))
    return pl.pallas_call(
        body, name=name, grid=(LRU_BLOCKS, t // r),
        in_specs=[tile, pl.BlockSpec((r, lb), lambda hb, j: (j, yb + hb)), wsp, vec, wsp, vec, vec],
        out_specs=[tile, tile],
        out_shape=[jax.ShapeDtypeStruct((t, LRU_WIDTH), f32), jax.ShapeDtypeStruct((t, LRU_WIDTH), bf16)],
        scratch_shapes=[pltpu.VMEM((8, lb), f32)],
        compiler_params=pltpu.CompilerParams(dimension_semantics=("parallel", "arbitrary")),
    )(u, proj, w_r, b_r, w_i, b_i, lam)


def lru_bwd(u, proj, hseq, dout, w_r, b_r, w_i, b_i, lam, dst, *, name):
    t = u.shape[0]
    r = LRU_ROWS
    nt = t // r
    lb = LRU_BLOCK
    yb = OFF_LY // lb

    def body(_, u_ref, y_ref, h_ref, hp_ref, do_ref, wr_ref, br_ref, wi_ref, bi_ref, lam_ref,
             du_ref, dy_ref, dwr_ref, dwi_ref, dbr_ref, dbi_ref, dlam_ref, carry_dh, carry_a):
        j = pl.program_id(1)

        @pl.when(j == 0)
        def _():
            carry_dh[...] = jnp.zeros_like(carry_dh)
            carry_a[...] = jnp.zeros_like(carry_a)
            dwr_ref[...] = jnp.zeros_like(dwr_ref)
            dwi_ref[...] = jnp.zeros_like(dwi_ref)
            dbr_ref[...] = jnp.zeros_like(dbr_ref)
            dbi_ref[...] = jnp.zeros_like(dbi_ref)
            dlam_ref[...] = jnp.zeros_like(dlam_ref)

        uv = u_ref[...]
        yv = y_ref[...]
        hv = h_ref[...]
        dov = do_ref[...]
        rg, ig, sp, la, a, s = _lru_gates(uv, wr_ref, wi_ref, br_ref, bi_ref, lam_ref)
        dy_ref[...] = (dov * hv * _dgelu(yv)).astype(bf16)
        gq = dov * _gelu(yv)
        row = lax.broadcasted_iota(jnp.int32, (r, lb), 0)
        an = jnp.where(row < r - 1, pltpu.roll(a, r - 1, 0), carry_a[0:1, :])
        d = 1
        while d < r:
            keep = row < r - d
            gq = gq + an * jnp.where(keep, pltpu.roll(gq, r - d, 0), 0.0)
            an = an * jnp.where(keep, pltpu.roll(an, r - d, 0), 1.0)
            d *= 2
        dh = gq + an * carry_dh[0:1, :]
        carry_dh[0:1, :] = dh[0:1, :]
        carry_a[0:1, :] = a[0:1, :]
        first = jnp.where(j == nt - 1, 0.0, 1.0) * hp_ref[7:8, :]
        hprev = jnp.where(row >= 1, pltpu.roll(hv, 1, 0), first)
        da = dh * hprev
        iu = ig * uv
        e2 = jnp.exp(2.0 * la)
        dla = da * a - dh * iu * e2 / jnp.maximum(s, 1e-30)
        drp = dla * (-LRU_C * sp) * rg * (1.0 - rg)
        dip = dh * s * uv * ig * (1.0 - ig)
        dlam_ref[...] += jnp.sum(dla * (LRU_C * rg) * _sigmoid(-lam_ref[...]), axis=0, keepdims=True)
        du_ref[...] = dh * s * ig + _dot(drp, wr_ref[0], NT) + _dot(dip, wi_ref[0], NT)
        dwr_ref[0] += _dot(uv, drp, TN)
        dwi_ref[0] += _dot(uv, dip, TN)
        dbr_ref[...] += jnp.sum(drp, axis=0, keepdims=True)
        dbi_ref[...] += jnp.sum(dip, axis=0, keepdims=True)

    rj = lambda j: nt - 1 - j
    tile = pl.BlockSpec((r, lb), lambda hb, j: (rj(j), hb))
    vec = pl.BlockSpec((1, lb), lambda hb, j: (0, hb))
    wsp = pl.BlockSpec((1, lb, lb), lambda hb, j: (hb, 0, 0))
    hprev_spec = pl.BlockSpec((8, lb), lambda hb, j: (jnp.maximum(rj(j) * (r // 8) - 1, 0), hb))
    ywin = pl.BlockSpec((r, lb), lambda hb, j: (rj(j), yb + hb))
    return pl.pallas_call(
        body, name=name, grid=(LRU_BLOCKS, nt),
        in_specs=[ANY, tile, ywin, tile, hprev_spec, tile, wsp, vec, wsp, vec, vec],
        out_specs=[tile, ywin, wsp, wsp, vec, vec, vec],
        out_shape=[jax.ShapeDtypeStruct((t, LRU_WIDTH), f32), jax.ShapeDtypeStruct(dst.shape, bf16),
                   jax.ShapeDtypeStruct((LRU_BLOCKS, lb, lb), f32), jax.ShapeDtypeStruct((LRU_BLOCKS, lb, lb), f32),
                   jax.ShapeDtypeStruct((1, LRU_WIDTH), f32), jax.ShapeDtypeStruct((1, LRU_WIDTH), f32),
                   jax.ShapeDtypeStruct((1, LRU_WIDTH), f32)],
        input_output_aliases={0: 1},
        scratch_shapes=[pltpu.VMEM((8, lb), f32), pltpu.VMEM((8, lb), f32)],
        compiler_params=pltpu.CompilerParams(dimension_semantics=("parallel", "arbitrary")),
    )(dst, u, proj, hseq, hseq, dout, w_r, b_r, w_i, b_i, lam)


def merge_fwd(proj, bg, y_ssm, y_lru, *, name):
    t, d = y_ssm.shape
    tr = _pick(t, 256, 8)
    gb = OFF_GATES // d

    def body(gs_ref, gl_ref, bs_ref, bl_ref, ys_ref, yl_ref, o_ref):
        gs = _sigmoid(gs_ref[...] + bs_ref[...])
        gl = _sigmoid(gl_ref[...] + bl_ref[...])
        o_ref[...] = (gs * ys_ref[...].astype(f32) + gl * yl_ref[...].astype(f32)).astype(bf16)

    row = pl.BlockSpec((tr, d), lambda i: (i, 0))
    return pl.pallas_call(
        body, name=name, grid=(t // tr,),
        in_specs=[pl.BlockSpec((tr, d), lambda i: (i, gb)), pl.BlockSpec((tr, d), lambda i: (i, gb + 1)),
                  pl.BlockSpec((1, d), lambda i: (0, 0)), pl.BlockSpec((1, d), lambda i: (0, 1)), row, row],
        out_specs=row, out_shape=jax.ShapeDtypeStruct((t, d), bf16),
    )(proj, proj, bg, bg, y_ssm, y_lru)


def merge_bwd(proj, bg, y_ssm, y_lru, dmix, *, name):
    t, d = y_ssm.shape
    tr = _pick(t, 256, 8)
    gb = OFF_GATES // d

    def body(gs_ref, gl_ref, bs_ref, bl_ref, ys_ref, yl_ref, dm_ref, dg_ref, dys_ref, dyl_ref, dbg_ref):
        gs = _sigmoid(gs_ref[...] + bs_ref[...])
        gl = _sigmoid(gl_ref[...] + bl_ref[...])
        dm = dm_ref[...].astype(f32)
        dys_ref[...] = (dm * gs).astype(bf16)
        dyl_ref[...] = (dm * gl).astype(bf16)
        dgs = dm * ys_ref[...].astype(f32) * gs * (1.0 - gs)
        dgl = dm * yl_ref[...].astype(f32) * gl * (1.0 - gl)
        dg_ref[:, 0:d] = dgs.astype(bf16)
        dg_ref[:, d:2 * d] = dgl.astype(bf16)

        @pl.when(pl.program_id(0) == 0)
        def _():
            dbg_ref[...] = jnp.zeros_like(dbg_ref)

        dbg_ref[:, 0:d] += jnp.sum(dgs, axis=0, keepdims=True)
        dbg_ref[:, d:2 * d] += jnp.sum(dgl, axis=0, keepdims=True)

    row = pl.BlockSpec((tr, d), lambda i: (i, 0))
    return pl.pallas_call(
        body, name=name, grid=(t // tr,),
        in_specs=[pl.BlockSpec((tr, d), lambda i: (i, gb)), pl.BlockSpec((tr, d), lambda i: (i, gb + 1)),
                  pl.BlockSpec((1, d), lambda i: (0, 0)), pl.BlockSpec((1, d), lambda i: (0, 1)), row, row, row],
        out_specs=[pl.BlockSpec((tr, 2 * d), lambda i: (i, OFF_GATES // (2 * d))), row, row,
                   pl.BlockSpec((1, 2 * d), lambda i: (0, 0))],
        out_shape=[jax.ShapeDtypeStruct((t, PROJ_W), bf16), jax.ShapeDtypeStruct((t, d), bf16),
                   jax.ShapeDtypeStruct((t, d), bf16), jax.ShapeDtypeStruct((1, 2 * d), f32)],
        compiler_params=pltpu.CompilerParams(dimension_semantics=("arbitrary",)),
    )(proj, proj, bg, bg, y_ssm, y_lru, dmix)


def swiglu_fwd(ff, *, name):
    t = ff.shape[0]
    hd = FFN_HIDDEN
    tr = _pick(t, 128, 8)

    def body(f_ref, o_ref):
        o_ref[...] = (_silu(f_ref[:, 0:hd].astype(f32)) * f_ref[:, hd:2 * hd].astype(f32)).astype(bf16)

    return pl.pallas_call(
        body, name=name, grid=(t // tr,), in_specs=[pl.BlockSpec((tr, 2 * hd), lambda i: (i, 0))],
        out_specs=pl.BlockSpec((tr, hd), lambda i: (i, 0)), out_shape=jax.ShapeDtypeStruct((t, hd), bf16),
    )(ff)


def swiglu_bwd(ff, dact, *, name):
    t = ff.shape[0]
    hd = FFN_HIDDEN
    tr = _pick(t, 128, 8)

    def body(f_ref, d_ref, o_ref):
        gate, up, dv = f_ref[:, 0:hd].astype(f32), f_ref[:, hd:2 * hd].astype(f32), d_ref[...].astype(f32)
        o_ref[:, 0:hd] = (dv * up * _dsilu(gate)).astype(bf16)
        o_ref[:, hd:2 * hd] = (dv * _silu(gate)).astype(bf16)

    return pl.pallas_call(
        body, name=name, grid=(t // tr,),
        in_specs=[pl.BlockSpec((tr, 2 * hd), lambda i: (i, 0)), pl.BlockSpec((tr, hd), lambda i: (i, 0))],
        out_specs=pl.BlockSpec((tr, 2 * hd), lambda i: (i, 0)), out_shape=jax.ShapeDtypeStruct((t, 2 * hd), bf16),
    )(ff, dact)


def _adam_math(w, g, m, v):
    m = ADAM_B1 * m + (1.0 - ADAM_B1) * g
    v = ADAM_B2 * v + (1.0 - ADAM_B2) * (g * g)
    m_hat = m / (1.0 - ADAM_B1 ** ADAM_STEP)
    v_hat = v / (1.0 - ADAM_B2 ** ADAM_STEP)
    delta = -ADAM_LR * (m_hat / (jnp.sqrt(v_hat) + ADAM_EPS) + ADAM_WD * w)
    return delta, m, v


def _row_tile(rows, cols):
    cap = max(8, (1 << 18) // cols)
    return _pick(rows, cap, 8) if rows % 8 == 0 else rows


def adamw(w, g, m, v, *, name):
    rows, cols = w.shape
    tr = _row_tile(rows, cols)

    def body(w_ref, g_ref, m_ref, v_ref, d_ref, nm_ref, nv_ref):
        d, nm, nv = _adam_math(w_ref[...], g_ref[...], m_ref[...], v_ref[...])
        d_ref[...] = d
        nm_ref[...] = nm
        nv_ref[...] = nv

    tile = pl.BlockSpec((tr, cols), lambda i: (i, 0))
    return pl.pallas_call(
        body, name=name, grid=(rows // tr,), in_specs=[tile] * 4, out_specs=[tile] * 3,
        out_shape=[jax.ShapeDtypeStruct((rows, cols), f32)] * 3,
    )(w, g, m, v)


def adamw_many(ws, gs, ms, vs, *, name):
    n = len(ws)

    def body(*refs):
        for i in range(n):
            d, nm, nv = _adam_math(refs[i][...], refs[n + i][...], refs[2 * n + i][...], refs[3 * n + i][...])
            refs[4 * n + 3 * i][...] = d
            refs[4 * n + 3 * i + 1][...] = nm
            refs[4 * n + 3 * i + 2][...] = nv

    outs = pl.pallas_call(
        body, name=name, out_shape=[jax.ShapeDtypeStruct(w.shape, f32) for w in ws for _ in range(3)],
    )(*ws, *gs, *ms, *vs)
    return [tuple(outs[3 * i:3 * i + 3]) for i in range(n)]


def pair_add(dw, rbuf, idx, *, name):
    n, rows, cols = dw.shape
    hr = rows // 2
    tr = _row_tile(hr, cols)
    nrt = hr // tr

    def body(idx_ref, a_ref, b_ref, o_ref, own_ref):
        s = a_ref[...] + b_ref[...]
        o_ref[...] = s.astype(bf16)

        @pl.when(pl.program_id(1) == idx_ref[0])
        def _():
            own_ref[...] = s[0]

    return pl.pallas_call(
        body, name=name,
        grid_spec=pltpu.PrefetchScalarGridSpec(
            num_scalar_prefetch=1, grid=(nrt, n),
            in_specs=[pl.BlockSpec((1, tr, cols), lambda i, k, idx: (k, idx[1] * nrt + i, 0)),
                      pl.BlockSpec((1, tr, cols), lambda i, k, idx: (k, i, 0))],
            out_specs=[pl.BlockSpec((1, tr, cols), lambda i, k, idx: (k, i, 0)),
                       pl.BlockSpec((tr, cols), lambda i, k, idx: (i, 0))]),
        out_shape=[jax.ShapeDtypeStruct((n, hr, cols), bf16), jax.ShapeDtypeStruct((hr, cols), f32)],
    )(idx, dw, rbuf)


def chip_sum(own, rbuf, idx, *, name):
    hr, cols = own.shape
    tr = _row_tile(hr, cols)
    nrt = hr // tr

    def body(idx_ref, a_ref, b_ref, o_ref):
        o_ref[...] = ((a_ref[...] + b_ref[0].astype(f32)) + b_ref[1].astype(f32)) + b_ref[2].astype(f32)

    return pl.pallas_call(
        body, name=name,
        grid_spec=pltpu.PrefetchScalarGridSpec(
            num_scalar_prefetch=1, grid=(nrt,),
            in_specs=[pl.BlockSpec((tr, cols), lambda i, idx: (i, 0)),
                      pl.BlockSpec((3, tr, cols), lambda i, idx: (0, i, 0))],
            out_specs=pl.BlockSpec((tr, cols), lambda i, idx: (idx[1] * nrt + i, 0))),
        out_shape=jax.ShapeDtypeStruct((2 * hr, cols), f32),
    )(idx, own, rbuf)


def sum8(rbuf, *, name):
    n, rows, cols = rbuf.shape
    tr = _row_tile(rows, cols * n)

    def body(a_ref, o_ref):
        acc = a_ref[0]
        for k in range(1, n):
            acc = acc + a_ref[k]
        o_ref[...] = acc

    return pl.pallas_call(
        body, name=name, grid=(rows // tr,), in_specs=[pl.BlockSpec((n, tr, cols), lambda i: (0, i, 0))],
        out_specs=pl.BlockSpec((tr, cols), lambda i: (i, 0)), out_shape=jax.ShapeDtypeStruct((rows, cols), f32),
    )(rbuf)


def _coords():
    return lax.axis_index("x"), lax.axis_index("y"), lax.axis_index("c")


def _other_chips(x, y):
    return [(1 - x, y), (x, 1 - y), (1 - x, 1 - y)]


def gather_weights(shards, *, name):
    n = len(shards)
    halves = [s.shape[0] // 2 for s in shards]

    def body(*refs):
        ins, outs = refs[:n], refs[n:2 * n]
        send1, recv1, send2, recv2 = refs[2 * n:]
        x, y, c = _coords()
        me = 2 * x + y
        chips = _other_chips(x, y)
        sibling = (x, y, 1 - c)

        def half(i, k, hc):
            return outs[i].at[k, pl.ds(hc * halves[i], halves[i]), :]

        def ici(i, j):
            return pltpu.make_async_remote_copy(
                src_ref=ins[i].at[pl.ds(c * halves[i], halves[i]), :], dst_ref=half(i, me, c),
                send_sem=send1.at[i, j], recv_sem=recv1.at[i, j], device_id=(*chips[j], c), device_id_type=MESH)

        def landed(i, j):
            kj = 2 * chips[j][0] + chips[j][1]
            return pltpu.make_async_remote_copy(
                src_ref=half(i, kj, c), dst_ref=half(i, kj, c),
                send_sem=send2.at[i, j], recv_sem=recv1.at[i, j], device_id=sibling, device_id_type=MESH)

        def from_sibling(i, j):
            kj = 2 * chips[j][0] + chips[j][1]
            return pltpu.make_async_remote_copy(
                src_ref=half(i, kj, 1 - c), dst_ref=half(i, kj, 1 - c),
                send_sem=send2.at[i, j], recv_sem=recv2.at[i, j], device_id=sibling, device_id_type=MESH)

        def d2d(i, j):
            kj = 2 * chips[j][0] + chips[j][1]
            return pltpu.make_async_remote_copy(
                src_ref=half(i, kj, c), dst_ref=half(i, kj, c),
                send_sem=send2.at[i, j], recv_sem=recv2.at[i, j], device_id=sibling, device_id_type=MESH)

        for j in range(3):
            for i in range(n):
                ici(i, j).start()
        for j in range(3):
            for i in range(n):
                landed(i, j).wait_recv()
                d2d(i, j).start()
        for j in range(3):
            for i in range(n):
                from_sibling(i, j).wait_recv()
        for j in range(3):
            for i in range(n):
                ici(i, j).wait_send()
                d2d(i, j).wait_send()

    return pl.pallas_call(
        body, name=name, in_specs=[ANY] * n, out_specs=[ANY] * n,
        out_shape=[jax.ShapeDtypeStruct((N_CHIPS,) + s.shape, s.dtype) for s in shards],
        scratch_shapes=[pltpu.SemaphoreType.DMA((n, 3))] * 4,
    )(*shards)


def pair_exchange(grads, *, name):
    n = len(grads)
    halves = [g.shape[1] // 2 for g in grads]

    def body(*refs):
        ins, outs = refs[:n], refs[n:2 * n]
        send, recv = refs[2 * n:]
        x, y, c = _coords()
        cps = [pltpu.make_async_remote_copy(
            src_ref=ins[i].at[:, pl.ds((1 - c) * halves[i], halves[i]), :], dst_ref=outs[i],
            send_sem=send.at[i], recv_sem=recv.at[i], device_id=(x, y, 1 - c), device_id_type=MESH) for i in range(n)]
        for cp in cps:
            cp.start()
        for cp in cps:
            cp.wait()

    return pl.pallas_call(
        body, name=name, in_specs=[ANY] * n, out_specs=[ANY] * n,
        out_shape=[jax.ShapeDtypeStruct((N_CHIPS, g.shape[1] // 2, g.shape[2]), g.dtype) for g in grads],
        scratch_shapes=[pltpu.SemaphoreType.DMA((n,))] * 2,
    )(*grads)


def pair_gather(bufs, *, name):
    n = len(bufs)

    def body(*refs):
        ins, outs = refs[:n], refs[n:2 * n]
        send, recv = refs[2 * n:]
        x, y, c = _coords()
        cps = []
        for i in range(n):
            hr = ins[i].shape[0] // 2
            cps.append(pltpu.make_async_remote_copy(
                src_ref=ins[i].at[pl.ds(c * hr, hr), :], dst_ref=outs[i].at[pl.ds(c * hr, hr), :],
                send_sem=send.at[i], recv_sem=recv.at[i], device_id=(x, y, 1 - c), device_id_type=MESH))
        for cp in cps:
            cp.start()
        for i in range(n):
            hr = ins[i].shape[0] // 2
            pltpu.make_async_remote_copy(
                src_ref=ins[i].at[pl.ds((1 - c) * hr, hr), :], dst_ref=outs[i].at[pl.ds((1 - c) * hr, hr), :],
                send_sem=send.at[i], recv_sem=recv.at[i], device_id=(x, y, 1 - c), device_id_type=MESH).wait_recv()
        for cp in cps:
            cp.wait_send()

    return pl.pallas_call(
        body, name=name, in_specs=[ANY] * n, out_specs=[ANY] * n,
        out_shape=[jax.ShapeDtypeStruct(b.shape, b.dtype) for b in bufs],
        input_output_aliases={i: i for i in range(n)},
        scratch_shapes=[pltpu.SemaphoreType.DMA((n,))] * 2,
    )(*bufs)


def all_exchange(buf, *, name):
    rows, cols = buf.shape

    def body(in_ref, out_ref, send, recv):
        x, y, c = _coords()
        me = 4 * x + 2 * y + c
        cps = []
        for d in range(1, 8):
            px = 1 - x if d & 4 else x
            py = 1 - y if d & 2 else y
            pc = 1 - c if d & 1 else c
            cps.append(pltpu.make_async_remote_copy(
                src_ref=in_ref, dst_ref=out_ref.at[me], send_sem=send.at[d - 1], recv_sem=recv.at[d - 1],
                device_id=(px, py, pc), device_id_type=MESH))
        for cp in cps:
            cp.start()
        for d in range(1, 8):
            px = 1 - x if d & 4 else x
            py = 1 - y if d & 2 else y
            pc = 1 - c if d & 1 else c
            src = 4 * px + 2 * py + pc
            pltpu.make_async_remote_copy(
                src_ref=in_ref, dst_ref=out_ref.at[src], send_sem=send.at[d - 1], recv_sem=recv.at[d - 1],
                device_id=(px, py, pc), device_id_type=MESH).wait_recv()
        for cp in cps:
            cp.wait_send()

    return pl.pallas_call(
        body, name=name, in_specs=[ANY], out_specs=ANY,
        out_shape=jax.ShapeDtypeStruct((8, rows, cols), buf.dtype),
        scratch_shapes=[pltpu.SemaphoreType.DMA((7,)), pltpu.SemaphoreType.DMA((7,))],
    )(buf)


HBM = pl.BlockSpec(memory_space=pltpu.HBM)
SEM = pl.BlockSpec(memory_space=pltpu.SEMAPHORE)
EFFECT = pltpu.SideEffectType.DATAFLOW_SIDE_EFFECTING


def split_start(arrays, after, copies, sem_shape, *, name):
    na = len(arrays)

    def body(*refs):
        for cp in copies(refs[:na], refs[na + 1], refs[na + 2]):
            cp.start()
        refs[-1][...] = jnp.zeros((8, 128), f32)

    outs = pl.pallas_call(
        body, name=name,
        out_shape=(pltpu.SemaphoreType.DMA(sem_shape), pltpu.SemaphoreType.DMA(sem_shape),
                   *[pltpu.HBM(a.shape, a.dtype) for a in arrays], jax.ShapeDtypeStruct((8, 128), f32)),
        in_specs=[HBM] * na + [ANY], out_specs=(SEM, SEM, *[HBM] * na, pl.BlockSpec(memory_space=pltpu.VMEM)),
        input_output_aliases={i: 2 + i for i in range(na)},
        compiler_params=pltpu.CompilerParams(has_side_effects=EFFECT),
    )(*[pltpu.with_memory_space_constraint(a, pltpu.HBM) for a in arrays], after)
    return outs[0], outs[1], list(outs[2:2 + na]), outs[-1]


def split_wait(send, recv, arrays, after, copies, *, name):
    na = len(arrays)

    def body(*refs):
        for cp in copies(refs[:na], refs[na], refs[na + 1]):
            cp.wait_send()
            cp.wait_recv()

    outs = pl.pallas_call(
        body, name=name, out_shape=tuple(pltpu.HBM(a.shape, a.dtype) for a in arrays),
        in_specs=[HBM] * na + [SEM, SEM, ANY], out_specs=tuple([HBM] * na),
        input_output_aliases={i: i for i in range(na)},
        compiler_params=pltpu.CompilerParams(has_side_effects=EFFECT),
    )(*arrays, send, recv, after)
    return list(outs)


def gather_copies(n):
    def copies(refs, send, recv):
        x, y, c = _coords()
        me = 2 * x + y
        chips = _other_chips(x, y)
        return [pltpu.make_async_remote_copy(
            src_ref=refs[i], dst_ref=refs[n + i].at[me], send_sem=send.at[3 * i + j], recv_sem=recv.at[3 * i + j],
            device_id=(*chips[j], c), device_id_type=MESH) for j in range(3) for i in range(n)]
    return copies


def pair_copies(n):
    def copies(refs, send, recv):
        x, y, c = _coords()
        cps = []
        for i in range(n):
            hr = refs[i].shape[1] // 2
            cps.append(pltpu.make_async_remote_copy(
                src_ref=refs[i].at[:, pl.ds((1 - c) * hr, hr), :], dst_ref=refs[n + i], send_sem=send.at[i],
                recv_sem=recv.at[i], device_id=(x, y, 1 - c), device_id_type=MESH))
        return cps
    return copies


def all_copies():
    def copies(refs, send, recv):
        x, y, c = _coords()
        me = 4 * x + 2 * y + c
        cps = []
        for d in range(1, 8):
            peer = (1 - x if d & 4 else x, 1 - y if d & 2 else y, 1 - c if d & 1 else c)
            cps.append(pltpu.make_async_remote_copy(
                src_ref=refs[0], dst_ref=refs[1].at[me], send_sem=send.at[d - 1], recv_sem=recv.at[d - 1],
                device_id=peer, device_id_type=MESH))
        return cps
    return copies


def reduce_copies(n):
    def copies(refs, send, recv):
        x, y, c = _coords()
        chips = _other_chips(x, y)
        return [pltpu.make_async_remote_copy(
            src_ref=refs[i].at[2 * chips[j][0] + chips[j][1]], dst_ref=refs[n + i].at[j],
            send_sem=send.at[3 * i + j], recv_sem=recv.at[3 * i + j], device_id=(*chips[j], c), device_id_type=MESH)
            for j in range(3) for i in range(n)]
    return copies


def _pack(arrs):
    flat = []
    for a in arrs:
        v = a.reshape(-1).astype(f32)
        pad = (-v.shape[0]) % 128
        flat.append(jnp.pad(v, (0, pad)) if pad else v)
    v = jnp.concatenate(flat)
    rows = v.shape[0] // 128
    pad_rows = (-rows) % 256
    v = v.reshape(rows, 128)
    return jnp.pad(v, ((0, pad_rows), (0, 0))) if pad_rows else v


def _unpack(buf, shapes):
    out, row = [], 0
    for s in shapes:
        size = math.prod(s)
        rows = -(-size // 128)
        out.append(buf[row:row + rows].reshape(-1)[:size].reshape(s))
        row += rows
    return out


def _ref_of_perm():
    ref = np.arange(IN_PROJ_DIM)
    xbc = ref[4096:7168]
    xbc_p = [np.concatenate([xbc[g * 512:(g + 1) * 512], xbc[2048 + g * 128:2048 + (g + 1) * 128],
                             xbc[2560 + g * 128:2560 + (g + 1) * 128]]) for g in range(SSM_GROUPS)]
    return np.concatenate([ref[0:2048], ref[2048:4096], ref[7200:8480], ref[8480:9760], ref[7168:7200],
                           -np.ones(DT_PAD_W - SSM_HEADS, np.int64)] + xbc_p)


def _runs(vals):
    out, start = [], 0
    for i in range(1, len(vals) + 1):
        if i == len(vals) or not (vals[i] == vals[i - 1] + 1 or (vals[i] < 0 and vals[i - 1] < 0)):
            out.append((start, int(vals[start]), i - start))
            start = i
    return out


def _perm_in_from_shards(g):
    ref_of_perm = _ref_of_perm()
    sw = IN_PROJ_DIM // N_CHIPS
    parts = []
    for _, first, length in _runs(ref_of_perm):
        if first < 0:
            parts.append(jnp.zeros((g.shape[1], length), g.dtype))
            continue
        lo = first
        while lo < first + length:
            k = lo // sw
            hi = min(first + length, (k + 1) * sw)
            parts.append(g[k, :, lo - k * sw:hi - k * sw])
            lo = hi
    return jnp.concatenate(parts, axis=-1)


def _unperm_in_to_shards(w):
    ref_of_perm = _ref_of_perm()
    perm_of_ref = np.zeros(IN_PROJ_DIM, np.int64)
    perm_of_ref[ref_of_perm[ref_of_perm >= 0]] = np.nonzero(ref_of_perm >= 0)[0]
    sw = IN_PROJ_DIM // N_CHIPS
    shards = []
    for k in range(N_CHIPS):
        runs = _runs(perm_of_ref[k * sw:(k + 1) * sw])
        shards.append(jnp.concatenate([w[:, first:first + length] for _, first, length in runs], axis=-1))
    return jnp.stack(shards)


def _perm_xbc_cols(w):
    parts = []
    for g in range(SSM_GROUPS):
        parts += [w[..., g * 512:(g + 1) * 512], w[..., 2048 + g * 128:2048 + (g + 1) * 128],
                  w[..., 2560 + g * 128:2560 + (g + 1) * 128]]
    return jnp.concatenate(parts, axis=-1)


def _unperm_xbc_cols(w):
    xs = [w[..., g * XBC_GROUP_W:g * XBC_GROUP_W + 512] for g in range(SSM_GROUPS)]
    bs = [w[..., g * XBC_GROUP_W + 512:g * XBC_GROUP_W + 640] for g in range(SSM_GROUPS)]
    cs = [w[..., g * XBC_GROUP_W + 640:(g + 1) * XBC_GROUP_W] for g in range(SSM_GROUPS)]
    return jnp.concatenate(xs + bs + cs, axis=-1)


def _from_col_shards(w):
    n, r, c = w.shape
    return jnp.transpose(w, (1, 0, 2)).reshape(r, n * c)


def kernel(x, norm1_w, w_in, b_branch_gate, ssm_conv_w, ssm_conv_b, ssm_dt_bias, ssm_a_log, ssm_d, ssm_norm_w, w_out_ssm, lru_conv_w, lru_conv_b, lru_w_r, lru_b_r, lru_w_i, lru_b_i, lru_lambda, w_out_lru, w_out, norm2_w, w_ffn_in, w_ffn_out, norm_f_w, loss_target, m_norm1_w, m_w_in, m_b_branch_gate, m_ssm_conv_w, m_ssm_conv_b, m_ssm_dt_bias, m_ssm_a_log, m_ssm_d, m_ssm_norm_w, m_w_out_ssm, m_lru_conv_w, m_lru_conv_b, m_lru_w_r, m_lru_b_r, m_lru_w_i, m_lru_b_i, m_lru_lambda, m_w_out_lru, m_w_out, m_norm2_w, m_w_ffn_in, m_w_ffn_out, m_norm_f_w, v_norm1_w, v_w_in, v_b_branch_gate, v_ssm_conv_w, v_ssm_conv_b, v_ssm_dt_bias, v_ssm_a_log, v_ssm_d, v_ssm_norm_w, v_w_out_ssm, v_lru_conv_w, v_lru_conv_b, v_lru_w_r, v_lru_b_r, v_lru_w_i, v_lru_b_i, v_lru_lambda, v_w_out_lru, v_w_out, v_norm2_w, v_w_ffn_in, v_w_ffn_out, v_norm_f_w):
    xi, yi, ci = lax.axis_index("x"), lax.axis_index("y"), lax.axis_index("c")
    me = 2 * xi + yi
    idx = jnp.stack([me, ci]).astype(jnp.int32)
    x2 = x[0]
    tgt = loss_target[0]

    big_names = ["w_in", "w_out_ssm", "w_out_lru", "w_out", "w_ffn_in", "w_ffn_out"]
    big_w = dict(w_in=w_in[0], w_out_ssm=w_out_ssm[0], w_out_lru=w_out_lru[0], w_out=w_out[0], w_ffn_in=w_ffn_in[0],
                 w_ffn_out=w_ffn_out[0])
    big_m = dict(w_in=m_w_in[0], w_out_ssm=m_w_out_ssm[0], w_out_lru=m_w_out_lru[0], w_out=m_w_out[0],
                 w_ffn_in=m_w_ffn_in[0], w_ffn_out=m_w_ffn_out[0])
    big_v = dict(w_in=v_w_in[0], w_out_ssm=v_w_out_ssm[0], w_out_lru=v_w_out_lru[0], w_out=v_w_out[0],
                 w_ffn_in=v_w_ffn_in[0], w_ffn_out=v_w_ffn_out[0])
    conv_pad = jnp.zeros((16, 768), f32).at[0:4, :].set(ssm_conv_w[0]).at[8:12, 0:320].set(lru_conv_w[0])
    mine = [big_w["w_in"].astype(bf16), conv_pad]
    gathered = gather_weights(mine, name="gather_weights")
    g_in, g_conv = [lax.dynamic_update_index_in_dim(g, s, me, 0) for g, s in zip(gathered, mine)]
    w_in_p = _perm_in_from_shards(g_in)
    late_names = big_names[1:]
    late = [big_w[k].astype(bf16) for k in late_names]
    late_lands = [lax.empty((N_CHIPS,) + s.shape, bf16) for s in late]
    g_send, g_recv, g_arrays, g_token = split_start(late + late_lands, g_conv, gather_copies(5), (15,),
                                                    name="gather_late_start")
    ssm_cw_full = _from_col_shards(g_conv[:, 0:4, :])
    lru_cw_full = _from_col_shards(g_conv[:, 8:12, 0:320])
    ssm_cw_p = _perm_xbc_cols(ssm_cw_full)
    ssm_cb_p = _perm_xbc_cols(ssm_conv_b)

    par = jnp.stack([ssm_dt_bias[0], ssm_a_log[0], ssm_d[0]], axis=0).reshape(3, SSM_GROUPS, SSM_HPG)
    par_row = jnp.zeros((SSM_GROUPS, 8, 8), f32).at[:, 0:3, :].set(jnp.transpose(par, (1, 0, 2)))
    par_col = jnp.transpose(par_row, (0, 2, 1))

    hn1 = rms_fwd(x2, norm1_w + g_token[0:1, 0:1], name="rms1_fwd")
    proj = mm(hn1, w_in_p, "nn", name="in_proj")
    t = x2.shape[0]
    dtr = jnp.transpose(proj[:, OFF_DT:OFF_DT + 32].reshape(t, SSM_GROUPS, SSM_HPG), (1, 0, 2))
    dtr_t = jnp.transpose(dtr, (0, 2, 1))
    xbc_pre, xbc_post = conv_fwd(proj, OFF_XBC, SSM_CONV_DIM, ssm_cw_p, ssm_cb_p, silu=True, name="ssm_conv_fwd")
    y_ssd, s_in = ssd_fwd(xbc_post, dtr, dtr_t, par_row, par_col, name="ssd_fwd")
    yn = gnorm_fwd(y_ssd, proj, ssm_norm_w, name="gnorm_fwd")
    g_arrays = split_wait(g_send, g_recv, g_arrays, yn, gather_copies(5), name="gather_late_wait")
    g_out_ssm, g_out_lru, g_out, g_ffn_in, g_ffn_out = [
        lax.dynamic_update_index_in_dim(g, s, me, 0) for g, s in zip(g_arrays[5:], late)]
    w_out_ssm_f = g_out_ssm.reshape(SSM_D_INNER, D_MODEL)
    w_out_lru_f = g_out_lru.reshape(LRU_WIDTH, D_MODEL)
    w_out_f = g_out.reshape(D_MODEL, D_MODEL)
    w_ffn_out_f = g_ffn_out.reshape(FFN_HIDDEN, D_MODEL)
    y_ssm = mm(yn, w_out_ssm_f, "nn", out_dtype=bf16, name="out_ssm")
    (u_lru,) = conv_fwd(proj, OFF_LX, LRU_WIDTH, lru_cw_full, lru_conv_b, silu=False, name="lru_conv_fwd")
    h_lru, o_lru = lru_fwd(u_lru, proj, lru_w_r[0], lru_b_r, lru_w_i[0], lru_b_i, lru_lambda, name="lru_fwd")
    y_lru = mm(o_lru, w_out_lru_f, "nn", out_dtype=bf16, name="out_lru")
    mix = merge_fwd(proj, b_branch_gate, y_ssm, y_lru, name="merge_fwd")
    h1, hn2 = mm(mix, w_out_f, "nn", add=x2, name="out_proj",
                 epi=(epi_rms_fwd, [], [norm2_w], [("row", f32), ("row", bf16)]))
    ff = mm(hn2, g_ffn_in, "nn", b_shards=True, out_dtype=bf16, name="ffn_in")
    act = swiglu_fwd(ff, name="swiglu_fwd")
    dh2, dh2_b, d_norm_f, loss_tile = mm(act, w_ffn_out_f, "nn", add=h1, name="ffn_out",
                                         epi=(epi_loss, [tgt], [norm_f_w.reshape(1, D_MODEL)],
                                              [("row", f32), ("row", bf16), ("vec",), ("tile",)]))
    loss = lax.psum(loss_tile[0, 0], ("x", "y", "c"))

    d_w_ffn_out = mm(act, dh2_b, "tn", name="d_w_ffn_out")
    dact = mm(dh2_b, w_ffn_out_f, "nt", out_dtype=bf16, name="d_act")
    dff = swiglu_bwd(ff, dact, name="swiglu_bwd")
    d_w_ffn_in = mm(hn2, dff, "tn", out_shards=N_CHIPS, name="d_w_ffn_in")
    dh1, dh1_b, d_norm2 = mm(dff, g_ffn_in, "nt", b_shards=True, name="d_hn2",
                             epi=(epi_rms_bwd, [h1, dh2], [norm2_w], [("row", f32), ("row", bf16), ("vec",)]))
    d_w_out = mm(mix, dh1_b, "tn", name="d_w_out")
    dmix = mm(dh1_b, w_out_f, "nt", out_dtype=bf16, name="d_mix")
    dproj, dy_ssm, dy_lru, d_bg = merge_bwd(proj, b_branch_gate, y_ssm, y_lru, dmix, name="merge_bwd")
    d_w_out_ssm = mm(yn, dy_ssm, "tn", name="d_w_out_ssm")
    d_w_out_lru = mm(o_lru, dy_lru, "tn", name="d_w_out_lru")
    early_g = [d_w_out_ssm.reshape(N_CHIPS, 512, D_MODEL), d_w_out_lru.reshape(N_CHIPS, 320, D_MODEL),
               d_w_out.reshape(N_CHIPS, 256, D_MODEL), d_w_ffn_in, d_w_ffn_out.reshape(N_CHIPS, 704, D_MODEL)]
    p_lands = [lax.empty((N_CHIPS, g.shape[1] // 2, g.shape[2]), f32) for g in early_g]
    p_send, p_recv, p_arrays, p_token = split_start(early_g + p_lands, early_g[0], pair_copies(5), (5,),
                                                    name="pair_early_start")
    dyn = mm(dy_ssm, w_out_ssm_f, "nt", out_dtype=bf16, after=p_token, name="d_yn")
    dy_ssd, dproj, d_ssm_norm = gnorm_bwd(y_ssd, proj, ssm_norm_w, dyn, dproj, name="gnorm_bwd")
    p_arrays = split_wait(p_send, p_recv, p_arrays, dy_ssd, pair_copies(5), name="pair_early_wait")
    e_pairs = [pair_add(g, rb, idx, name="pair_add_" + k) for g, rb, k in zip(p_arrays[:5], p_arrays[5:], late_names)]
    e_lands = [lax.empty((3,) + p[0].shape[1:], bf16) for p in e_pairs]
    e_send, e_recv, e_arrays, e_token = split_start([p[0] for p in e_pairs] + e_lands, e_pairs[0][1], reduce_copies(5),
                                                    (15,), name="reduce_early_start")
    dxbc_post, ddtr, dpar = ssd_bwd(xbc_post, dtr, dtr_t, par_row + e_token[0:1, 0:1], par_col, s_in, dy_ssd,
                                    name="ssd_bwd")
    dproj, d_ssm_cw_p, d_ssm_cb_p = conv_bwd(dxbc_post, xbc_pre, proj, OFF_XBC, ssm_cw_p, dproj, name="ssm_conv_bwd")
    do_lru = mm(dy_lru, w_out_lru_f, "nt", name="d_o_lru")
    du_lru, dproj, d_w_r, d_w_i, d_b_r, d_b_i, d_lam = lru_bwd(u_lru, proj, h_lru, do_lru, lru_w_r[0], lru_b_r, lru_w_i[0],
                                                               lru_b_i, lru_lambda, dproj, name="lru_bwd")
    dproj, d_lru_cw, d_lru_cb = conv_bwd(du_lru, None, proj, OFF_LX, lru_cw_full, dproj, name="lru_conv_bwd")
    ddt_cols = jnp.transpose(ddtr, (1, 0, 2)).reshape(t, SSM_HEADS).astype(bf16)
    ddt_cols = jnp.pad(ddt_cols, ((0, 0), (0, DT_PAD_W - SSM_HEADS)))
    dproj = lax.dynamic_update_slice(dproj, ddt_cols, (0, OFF_DT))

    d_ssm_cw = _unperm_xbc_cols(d_ssm_cw_p)
    d_ssm_cb = _unperm_xbc_cols(d_ssm_cb_p)
    dpar_h = jnp.transpose(dpar[:, 0:3, :], (1, 0, 2)).reshape(3, SSM_HEADS)
    small_names = ["norm1_w", "b_branch_gate", "ssm_conv_b", "ssm_dt_bias", "ssm_a_log", "ssm_d", "ssm_norm_w",
                   "lru_conv_b", "lru_w_r", "lru_b_r", "lru_w_i", "lru_b_i", "lru_lambda", "norm2_w", "norm_f_w"]
    small_g = dict(norm1_w=jnp.zeros_like(norm1_w), b_branch_gate=d_bg, ssm_conv_b=d_ssm_cb, ssm_dt_bias=dpar_h[0:1], ssm_a_log=dpar_h[1:2],
                   ssm_d=dpar_h[2:3], ssm_norm_w=d_ssm_norm, lru_conv_b=d_lru_cb, lru_w_r=d_w_r[None], lru_b_r=d_b_r,
                   lru_w_i=d_w_i[None], lru_b_i=d_b_i, lru_lambda=d_lam, norm2_w=d_norm2, norm_f_w=d_norm_f.reshape(D_MODEL))
    small_w = dict(norm1_w=norm1_w, b_branch_gate=b_branch_gate, ssm_conv_b=ssm_conv_b, ssm_dt_bias=ssm_dt_bias,
                   ssm_a_log=ssm_a_log, ssm_d=ssm_d, ssm_norm_w=ssm_norm_w, lru_conv_b=lru_conv_b, lru_w_r=lru_w_r,
                   lru_b_r=lru_b_r, lru_w_i=lru_w_i, lru_b_i=lru_b_i, lru_lambda=lru_lambda, norm2_w=norm2_w, norm_f_w=norm_f_w)
    small_m = dict(norm1_w=m_norm1_w, b_branch_gate=m_b_branch_gate, ssm_conv_b=m_ssm_conv_b, ssm_dt_bias=m_ssm_dt_bias,
                   ssm_a_log=m_ssm_a_log, ssm_d=m_ssm_d, ssm_norm_w=m_ssm_norm_w, lru_conv_b=m_lru_conv_b, lru_w_r=m_lru_w_r,
                   lru_b_r=m_lru_b_r, lru_w_i=m_lru_w_i, lru_b_i=m_lru_b_i, lru_lambda=m_lru_lambda, norm2_w=m_norm2_w,
                   norm_f_w=m_norm_f_w)
    small_v = dict(norm1_w=v_norm1_w, b_branch_gate=v_b_branch_gate, ssm_conv_b=v_ssm_conv_b, ssm_dt_bias=v_ssm_dt_bias,
                   ssm_a_log=v_ssm_a_log, ssm_d=v_ssm_d, ssm_norm_w=v_ssm_norm_w, lru_conv_b=v_lru_conv_b, lru_w_r=v_lru_w_r,
                   lru_b_r=v_lru_b_r, lru_w_i=v_lru_w_i, lru_b_i=v_lru_b_i, lru_lambda=v_lru_lambda, norm2_w=v_norm2_w,
                   norm_f_w=v_norm_f_w)
    shapes = [small_w[k].shape for k in small_names]
    conv_shapes = [(4, SSM_CONV_DIM), (4, LRU_WIDTH)]
    g_pack = _pack([small_g[k] for k in small_names] + [d_ssm_cw, d_lru_cw])
    s_send, s_recv, s_arrays, s_token = split_start([g_pack, lax.empty((8,) + g_pack.shape, f32)], g_pack, all_copies(),
                                                    (7,), name="small_start")
    d_w_in_p = mm(hn1, dproj, "tn", after=s_token, name="d_w_in")

    d_w_in_s = _unperm_in_to_shards(d_w_in_p)
    (l_sib,) = pair_exchange([d_w_in_s], name="pair_exchange_late")
    l_pair = pair_add(d_w_in_s, l_sib, idx, name="pair_add_w_in")
    l_land = lax.empty((3,) + l_pair[0].shape[1:], bf16)
    l_send, l_recv, l_arrays, l_token = split_start([l_pair[0], l_land], l_pair[1], reduce_copies(1), (3,),
                                                    name="reduce_late_start")
    grad_x, d_norm1 = mm(dproj, w_in_p, "nt", after=l_token, name="d_hn1",
                         epi=(epi_rms_bwd, [x2, dh1], [norm1_w], [("row", f32), ("vec",)]))

    e_arrays = split_wait(e_send, e_recv, e_arrays, d_norm1, reduce_copies(5), name="reduce_early_wait")
    e_half = [chip_sum(p[1], rb, idx, name="chip_sum_" + k) for p, rb, k in zip(e_pairs, e_arrays[5:], late_names)]
    big_out = {}
    for k, g in zip(late_names, pair_gather(e_half, name="pair_gather_early")):
        big_out[k] = (g,) + tuple(adamw(big_w[k], g, big_m[k], big_v[k], name="adamw_" + k))

    s_arrays = split_wait(s_send, s_recv, s_arrays, d_norm1, all_copies(), name="small_wait")
    g_sum = sum8(lax.dynamic_update_index_in_dim(s_arrays[1], g_pack, 2 * me + ci, 0), name="sum8")
    n1 = d_norm1.reshape(8, 128)
    n1_sum = sum8(lax.dynamic_update_index_in_dim(all_exchange(n1, name="all_exchange_norm1"), n1, 2 * me + ci, 0),
                  name="sum8_norm1")
    g_sum = lax.dynamic_update_slice(g_sum, n1_sum, (0, 0))
    g_small = _unpack(g_sum, shapes + conv_shapes)
    g_small[-2] = lax.dynamic_slice_in_dim(g_small[-2], me * 768, 768, axis=1)
    g_small[-1] = lax.dynamic_slice_in_dim(g_small[-1], me * 320, 320, axis=1)
    all_names = small_names + ["ssm_conv_w", "lru_conv_w"]
    small_w.update(ssm_conv_w=ssm_conv_w[0], lru_conv_w=lru_conv_w[0])
    small_m.update(ssm_conv_w=m_ssm_conv_w[0], lru_conv_w=m_lru_conv_w[0])
    small_v.update(ssm_conv_w=v_ssm_conv_w[0], lru_conv_w=v_lru_conv_w[0])
    as2d = lambda a: a.reshape(-1, a.shape[-1])
    upd = adamw_many([as2d(small_w[k]) for k in all_names], [as2d(g) for g in g_small],
                     [as2d(small_m[k]) for k in all_names], [as2d(small_v[k]) for k in all_names], name="adamw_small")
    small_out = {}
    for k, g, u in zip(all_names, g_small, upd):
        small_out[k] = (g,) + tuple(o.reshape(g.shape) for o in u)
    l_arrays = split_wait(l_send, l_recv, l_arrays, upd[0][0], reduce_copies(1), name="reduce_late_wait")
    l_half = chip_sum(l_pair[1], l_arrays[1], idx, name="chip_sum_w_in")
    (g_w_in,) = pair_gather([l_half], name="pair_gather_late")
    big_out["w_in"] = (g_w_in,) + tuple(adamw(big_w["w_in"], g_w_in, big_m["w_in"], big_v["w_in"], name="adamw_w_in"))

    order = ["norm1_w", "w_in", "b_branch_gate", "ssm_conv_w", "ssm_conv_b", "ssm_dt_bias", "ssm_a_log", "ssm_d", "ssm_norm_w",
             "w_out_ssm", "lru_conv_w", "lru_conv_b", "lru_w_r", "lru_b_r", "lru_w_i", "lru_b_i", "lru_lambda", "w_out_lru",
             "w_out", "norm2_w", "w_ffn_in", "w_ffn_out", "norm_f_w"]
    outs = [loss, grad_x[None]]
    for which in range(4):
        for k in order:
            if k in big_out:
                outs.append(big_out[k][which][None])
            elif k in ("ssm_conv_w", "lru_conv_w"):
                outs.append(small_out[k][which][None])
            else:
                outs.append(small_out[k][which])
    return tuple(outs)
```

```python
import functools
import math

import jax
import jax.numpy as jnp
import numpy as np
from jax import lax
from jax.experimental import pallas as pl
from jax.experimental.pallas import tpu as pltpu

f32 = jnp.float32
bf16 = jnp.bfloat16

D_MODEL = 1024
SSM_D_INNER = 2048
SSM_HEADS = 32
SSM_HEAD_DIM = 64
SSM_GROUPS = 4
SSM_HPG = 8
SSM_D_STATE = 128
SSM_CHUNK = 128
SSM_GROUP_W = 512
SSM_CONV_DIM = 3072
XBC_GROUP_W = 768
LRU_WIDTH = 1280
LRU_BLOCKS = 10
LRU_BLOCK = 128
LRU_C = 8.0
FFN_HIDDEN = 2816
RMS_EPS = 1e-6
IN_PROJ_DIM = 9760
N_CHIPS = 4

OFF_GATES = 0
OFF_Z = 2048
OFF_LX = 4096
OFF_LY = 5376
OFF_DT = 6656
DT_PAD_W = 256
OFF_XBC = 6912
PROJ_W = 9984

ADAM_LR = 0.001
ADAM_B1 = 0.9
ADAM_B2 = 0.999
ADAM_EPS = 1e-08
ADAM_WD = 0.01
ADAM_STEP = 10

MESH = pl.DeviceIdType.MESH
ANY = pl.BlockSpec(memory_space=pl.ANY)

NN = (((1,), (0,)), ((), ()))
NT = (((1,), (1,)), ((), ()))
TN = (((0,), (0,)), ((), ()))


def _pick(n, cap, mult=128):
    best = None
    for t in range(mult, min(n, cap) + 1, mult):
        if n % t == 0:
            best = t
    return best if best is not None else n


def _sigmoid(x):
    return 0.5 * jnp.tanh(0.5 * x) + 0.5


def _softplus(x):
    return jnp.maximum(x, 0.0) + jnp.log(1.0 + jnp.exp(-jnp.abs(x)))


def _silu(x):
    return x * _sigmoid(x)


def _dsilu(x):
    s = _sigmoid(x)
    return s * (1.0 + x * (1.0 - s))


_GELU_K = math.sqrt(2.0 / math.pi)


def _gelu(x):
    return 0.5 * x * (1.0 + jnp.tanh(_GELU_K * (x + 0.044715 * x * x * x)))


def _dgelu(x):
    t = jnp.tanh(_GELU_K * (x + 0.044715 * x * x * x))
    return 0.5 * (1.0 + t) + 0.5 * x * (1.0 - t * t) * _GELU_K * (1.0 + 3.0 * 0.044715 * x * x)


def _expm1(x):
    poly = x * (1.0 + x * (0.5 + x * (1.0 / 6.0 + x * (1.0 / 24.0 + x * (1.0 / 120.0 + x * (1.0 / 720.0))))))
    return jnp.where(jnp.abs(x) < 0.1, poly, jnp.exp(x) - 1.0)


def _dot(a, b, dn):
    return lax.dot_general(a.astype(bf16), b.astype(bf16), dn, preferred_element_type=f32)


def _dot_01(a, b, dn, split, terms):
    r = a if split == 0 else b
    out = None
    for _ in range(terms):
        h = r.astype(bf16)
        r = r - h.astype(f32)
        d = lax.dot_general(h if split == 0 else a.astype(bf16), b.astype(bf16) if split == 0 else h, dn,
                            preferred_element_type=f32)
        out = d if out is None else out + d
    return out


MM_VMEM_BUDGET = 48 * 2 ** 20

def mm(a, b, mode, *, name, add=None, after=None, out_dtype=f32, b_shards=False, out_shards=0, epi=None):
    bs = b.shape[1:] if b_shards else b.shape
    shard_w = b.shape[2] if b_shards else None
    bcols = bs[1] * (b.shape[0] if b_shards else 1)
    if mode == "nn":
        (m, k), (k2, n) = a.shape, (bs[0], bcols)
    elif mode == "nt":
        (m, k), (n, k2) = a.shape, (bs[0], bcols)
    else:
        (k, m), (k2, n) = a.shape, b.shape
    assert k == k2, (a.shape, b.shape, mode)
    tn = _pick(n, 1536)
    if b_shards and mode == "nn":
        tn = shard_w
    if out_shards:
        tn = n // out_shards
    isz = lambda v: jnp.dtype(v.dtype).itemsize
    if epi is not None:
        assert n <= 1536 and not out_shards
        tn = n
        epi_fn, epi_rows, epi_vecs, epi_outs = epi
        tile_bytes = sum(isz(v) for v in epi_rows) + sum(jnp.dtype(o[1]).itemsize for o in epi_outs if o[0] == "row")
    else:
        epi_rows, epi_vecs, epi_outs = [], [], []
        tile_bytes = jnp.dtype(out_dtype).itemsize
    tks = [shard_w] if (b_shards and mode == "nt") else sorted({k, _pick(k, 3328), _pick(k, 2048), _pick(k, 1024)}, reverse=True)

    def vmem_of(tm, tk):
        blocks = tm * tk * isz(a) + tk * tn * isz(b) + tm * tn * (4 * int(add is not None) + tile_bytes)
        return 2 * blocks + 4 * tm * tn * int(k > tk)

    fits = [(tk, tm) for tk in tks for tm in (_pick(m, 1536), _pick(m, 1024), _pick(m, 512)) if vmem_of(tm, tk) <= MM_VMEM_BUDGET]
    tk, tm = fits[0] if fits else (tks[-1], _pick(m, 256))
    nk = k // tk
    dn = {"nn": NN, "nt": NT, "tn": TN}[mode]
    a_spec = pl.BlockSpec((tk, tm), lambda i, j, kk: (kk, i)) if mode == "tn" else pl.BlockSpec((tm, tk), lambda i, j, kk: (i, kk))
    b_spec = pl.BlockSpec((tn, tk), lambda i, j, kk: (j, kk)) if mode == "nt" else pl.BlockSpec((tk, tn), lambda i, j, kk: (kk, j))
    if b_shards:
        b_spec = (pl.BlockSpec((None, tn, tk), lambda i, j, kk: (kk, j, 0)) if mode == "nt"
                  else pl.BlockSpec((None, tk, tn), lambda i, j, kk: (j, kk, 0)))
    o_spec = pl.BlockSpec((tm, tn), lambda i, j, kk: (i, j))
    out_shape = jax.ShapeDtypeStruct((m, n), out_dtype)
    if out_shards:
        assert add is None
        o_spec = pl.BlockSpec((None, tm, tn), lambda i, j, kk: (j, i, 0))
        out_shape = jax.ShapeDtypeStruct((out_shards, m, tn), out_dtype)
    has_add = add is not None

    n_extra = int(has_add) + int(after is not None)
    n_rows, n_vecs, n_outs = len(epi_rows), len(epi_vecs), len(epi_outs)

    def body(a_ref, b_ref, *rest):
        add_ref = rest[0] if has_add else None
        o_ref = rest[n_extra]

        def finish(r):
            if has_add:
                r = r + add_ref[...]
            if epi is None:
                o_ref[...] = r.astype(out_dtype)
            else:
                e = rest[n_extra:]
                epi_fn(r, e[:n_rows], e[n_rows:n_rows + n_vecs], e[n_rows + n_vecs:n_rows + n_vecs + n_outs],
                       pl.program_id(0) == 0)

        if nk == 1:
            finish(_dot(a_ref[...], b_ref[...], dn))
            return
        acc = rest[-1]
        kk = pl.program_id(2)

        @pl.when(kk == 0)
        def _():
            acc[...] = jnp.zeros_like(acc)

        acc[...] += _dot(a_ref[...], b_ref[...], dn)

        @pl.when(kk == nk - 1)
        def _():
            finish(acc[...])

    ins = [a, b] + ([add] if has_add else []) + ([after] if after is not None else [])
    in_specs = [a_spec, b_spec] + ([o_spec] if has_add else []) + ([ANY] if after is not None else [])
    sem0 = "parallel"
    if epi is not None:
        vec_spec = pl.BlockSpec((1, tn), lambda i, j, kk: (0, 0))
        ins += list(epi_rows) + list(epi_vecs)
        in_specs += [o_spec] * n_rows + [vec_spec] * n_vecs
        o_spec, out_shape = [], []
        for o in epi_outs:
            if o[0] == "row":
                o_spec.append(pl.BlockSpec((tm, tn), lambda i, j, kk: (i, j)))
                out_shape.append(jax.ShapeDtypeStruct((m, n), o[1]))
            elif o[0] == "vec":
                o_spec.append(vec_spec)
                out_shape.append(jax.ShapeDtypeStruct((1, n), f32))
                sem0 = "arbitrary"
            else:
                o_spec.append(pl.BlockSpec((8, 128), lambda i, j, kk: (0, 0)))
                out_shape.append(jax.ShapeDtypeStruct((8, 128), f32))
                sem0 = "arbitrary"
    return pl.pallas_call(
        body, name=name, grid=(m // tm, n // tn, nk), in_specs=in_specs, out_specs=o_spec, out_shape=out_shape,
        scratch_shapes=[pltpu.VMEM((tm, tn), f32)] if nk > 1 else [],
        compiler_params=pltpu.CompilerParams(dimension_semantics=(sem0, sem0, "arbitrary")),
    )(*ins)


def rms_fwd(x, w, *, name):
    t, d = x.shape
    tr = _pick(t, 256, 8)

    def body(x_ref, w_ref, o_ref):
        xv = x_ref[...]
        r = lax.rsqrt(jnp.mean(xv * xv, axis=-1, keepdims=True) + RMS_EPS)
        o_ref[...] = (xv * r * w_ref[...]).astype(bf16)

    return pl.pallas_call(
        body, name=name, grid=(t // tr,),
        in_specs=[pl.BlockSpec((tr, d), lambda i: (i, 0)), pl.BlockSpec((1, d), lambda i: (0, 0))],
        out_specs=pl.BlockSpec((tr, d), lambda i: (i, 0)), out_shape=jax.ShapeDtypeStruct((t, d), bf16),
    )(x, w)


def _rms_bwd_math(xv, wv, dy):
    r = lax.rsqrt(jnp.mean(xv * xv, axis=-1, keepdims=True) + RMS_EPS)
    g = dy * wv
    dx = r * g - xv * (r * r * r) * jnp.mean(g * xv, axis=-1, keepdims=True)
    dw = jnp.sum(dy * xv * r, axis=0, keepdims=True)
    return dx, dw


def epi_rms_fwd(r, rows, vecs, outs, first):
    outs[0][...] = r
    rr = lax.rsqrt(jnp.mean(r * r, axis=-1, keepdims=True) + RMS_EPS)
    outs[1][...] = (r * rr * vecs[0][...]).astype(bf16)


def epi_rms_bwd(r, rows, vecs, outs, first):
    dx, dw = _rms_bwd_math(rows[0][...], vecs[0][...], r)
    dx = dx + rows[1][...]
    outs[0][...] = dx
    if len(outs) == 3:
        outs[1][...] = dx.astype(bf16)
    dw_ref = outs[-1]

    @pl.when(first)
    def _():
        dw_ref[...] = jnp.zeros_like(dw_ref)

    dw_ref[...] += dw


def epi_loss(r, rows, vecs, outs, first):
    wv = vecs[0][...]
    rr = lax.rsqrt(jnp.mean(r * r, axis=-1, keepdims=True) + RMS_EPS)
    err = r * rr * wv - rows[0][...]
    part = 0.5 * jnp.sum(jnp.mean(err * err, axis=-1, keepdims=True), axis=0, keepdims=True)
    dx, dw = _rms_bwd_math(r, wv, err * (1.0 / r.shape[-1]))
    outs[0][...] = dx
    outs[1][...] = dx.astype(bf16)

    @pl.when(first)
    def _():
        outs[2][...] = jnp.zeros_like(outs[2])
        outs[3][...] = jnp.zeros_like(outs[3])

    outs[2][...] += dw
    outs[3][...] += part


CONV_ROWS = 512
VREG_ELEMS = 8 * 128


def _conv_chunk(tc):
    return 16 if (16 + 8) * tc * 3 > 48 * VREG_ELEMS else 32


def conv_fwd(src, col0, width, w, b, *, silu, name):
    t = src.shape[0]
    tc = _pick(math.gcd(width, col0), 768)
    assert col0 % tc == 0
    cb = col0 // tc
    r = CONV_ROWS
    ch = _conv_chunk(tc)

    def body(u_ref, w_ref, b_ref, *rest):
        ext = rest[-1]
        j = pl.program_id(1)

        @pl.when(j == 0)
        def _():
            ext[0:8, :] = jnp.zeros((8, tc), f32)

        @pl.when(j > 0)
        def _():
            ext[0:8, :] = ext[r:r + 8, :]

        ext[8:r + 8, :] = u_ref[...]
        wv = w_ref[...]
        bv = b_ref[...]

        def chunk(c, carry):
            r0 = pl.multiple_of(c * ch, ch)
            v = ext[pl.ds(r0, ch + 8), :]
            acc = bv + wv[3:4, :] * v[8:, :]
            for s in (1, 2, 3):
                acc = acc + wv[3 - s:4 - s, :] * pltpu.roll(v, s, 0)[8:, :]
            rest[0][pl.ds(r0, ch), :] = acc
            if silu:
                rest[1][pl.ds(r0, ch), :] = _silu(acc)
            return carry

        lax.fori_loop(0, r // ch, chunk, 0)

    tile = pl.BlockSpec((r, tc), lambda c, j: (j, c))
    n_out = 2 if silu else 1
    return pl.pallas_call(
        body, name=name, grid=(width // tc, t // r),
        in_specs=[pl.BlockSpec((r, tc), lambda c, j: (j, cb + c)), pl.BlockSpec((4, tc), lambda c, j: (0, c)),
                  pl.BlockSpec((1, tc), lambda c, j: (0, c))],
        out_specs=[tile] * n_out, out_shape=[jax.ShapeDtypeStruct((t, width), f32)] * n_out,
        scratch_shapes=[pltpu.VMEM((r + 8, tc), f32)],
        compiler_params=pltpu.CompilerParams(dimension_semantics=("parallel", "arbitrary")),
    )(src, w, b)


def conv_bwd(dpost, pre, src, col0, w, dst, *, name):
    t, width = dpost.shape
    tc = _pick(math.gcd(width, col0), 768)
    assert col0 % tc == 0
    cb = col0 // tc
    r = CONV_ROWS
    ch = _conv_chunk(tc)
    nt = t // r
    has_pre = pre is not None

    def body(*refs):
        refs = refs[1:]
        if has_pre:
            d_ref, p_ref, u_ref, w_ref, du_ref, dw_ref, db_ref, ext = refs
        else:
            d_ref, u_ref, w_ref, du_ref, dw_ref, db_ref, ext = refs
        j = pl.program_id(1)

        @pl.when(j == 0)
        def _():
            ext[r:r + 8, :] = jnp.zeros((8, tc), f32)
            dw_ref[...] = jnp.zeros_like(dw_ref)
            db_ref[...] = jnp.zeros_like(db_ref)

        @pl.when(j > 0)
        def _():
            ext[r:r + 8, :] = ext[0:8, :]

        dpre = d_ref[...]
        if has_pre:
            dpre = dpre * _dsilu(p_ref[...])
        ext[0:r, :] = dpre
        wv = w_ref[...]

        def fold(p):
            out = p[0:8, :]
            for i in range(1, ch // 8):
                out = out + p[8 * i:8 * i + 8, :]
            return out

        def chunk(c, sums):
            r0 = pl.multiple_of(c * ch, ch)
            v = ext[pl.ds(r0, ch + 8), :]
            uv = u_ref[pl.ds(r0, ch), :]
            d0 = v[0:ch, :]
            du = wv[3:4, :] * d0
            new = [None] * 5
            new[3] = sums[3] + fold(d0 * uv)
            for s in (1, 2, 3):
                sh = pltpu.roll(v, ch + 8 - s, 0)[0:ch, :]
                du = du + wv[3 - s:4 - s, :] * sh
                new[3 - s] = sums[3 - s] + fold(sh * uv)
            new[4] = sums[4] + fold(d0)
            du_ref[pl.ds(r0, ch), :] = du.astype(bf16)
            return tuple(new)

        sums = lax.fori_loop(0, r // ch, chunk, tuple(jnp.zeros((8, tc), f32) for _ in range(5)))
        for k in range(4):
            dw_ref[k:k + 1, :] += jnp.sum(sums[k], axis=0, keepdims=True)
        db_ref[...] += jnp.sum(sums[4], axis=0, keepdims=True)

    rev = pl.BlockSpec((r, tc), lambda c, j: (nt - 1 - j, c))
    win = pl.BlockSpec((r, tc), lambda c, j: (nt - 1 - j, cb + c))
    in_specs = [ANY, rev] + ([rev] if has_pre else []) + [win, pl.BlockSpec((4, tc), lambda c, j: (0, c))]
    ins = [dst, dpost] + ([pre] if has_pre else []) + [src, w]
    return pl.pallas_call(
        body, name=name, grid=(width // tc, nt), in_specs=in_specs,
        out_specs=[win, pl.BlockSpec((4, tc), lambda c, j: (0, c)), pl.BlockSpec((1, tc), lambda c, j: (0, c))],
        out_shape=[jax.ShapeDtypeStruct(dst.shape, bf16), jax.ShapeDtypeStruct((4, width), f32),
                   jax.ShapeDtypeStruct((1, width), f32)],
        input_output_aliases={0: 0},
        scratch_shapes=[pltpu.VMEM((r + 8, tc), f32)],
        compiler_params=pltpu.CompilerParams(dimension_semantics=("parallel", "arbitrary")),
    )(*ins)


def _ssd_common(xbc_ref, dtr_ref, dtrT_ref, par_row_ref, par_col_ref):
    l = SSM_CHUNK
    x = xbc_ref[:, 0:SSM_GROUP_W]
    bm = xbc_ref[:, SSM_GROUP_W:SSM_GROUP_W + SSM_D_STATE]
    cm = xbc_ref[:, SSM_GROUP_W + SSM_D_STATE:XBC_GROUP_W]
    par_row = par_row_ref[0]
    par_col = par_col_ref[0]
    bias_row, alog_row, d_row = par_row[0:1, :], par_row[1:2, :], par_row[2:3, :]
    bias_col, alog_col = par_col[:, 0:1], par_col[:, 1:2]
    dtr = dtr_ref[0]
    dt = _softplus(dtr + bias_row)
    dt_t = _softplus(dtrT_ref[0] + bias_col)
    a_row = -jnp.exp(alog_row)
    a_col = -jnp.exp(alog_col)
    li = lax.broadcasted_iota(jnp.int32, (l, l), 0)
    si = lax.broadcasted_iota(jnp.int32, (l, l), 1)
    tri = (li >= si).astype(f32)
    cs = _dot_01(tri, dt * a_row, NN, 1, 3)
    cs_t = _dot_01(dt_t * a_col, tri, NT, 0, 3)
    off = lax.broadcasted_iota(jnp.int32, (SSM_HPG, SSM_GROUP_W), 1) - SSM_HEAD_DIM * lax.broadcasted_iota(
        jnp.int32, (SSM_HPG, SSM_GROUP_W), 0)
    ex = ((off >= 0) & (off < SSM_HEAD_DIM)).astype(f32)
    cs_x = _dot_01(cs, ex, NN, 0, 3)
    cl_x = cs_x[l - 1:l, :]
    return dict(x=x, bm=bm, cm=cm, dtr=dtr, dt=dt, a_row=a_row, bias_row=bias_row, tri=tri, li=li, si=si, cs=cs,
                cs_t=cs_t, ex=ex, dt_x=_dot_01(dt, ex, NN, 0, 2), d_x=_dot_01(par_row, ex, NN, 0, 2)[2:3, :], e_x=jnp.exp(cs_x),
                el_x=jnp.exp(cl_x), dec_x=jnp.exp(cl_x - cs_x))


def ssd_fwd(xbc, dtr, dtr_t, par_row, par_col, *, name):
    t = xbc.shape[0]
    nc = t // SSM_CHUNK
    l, p = SSM_CHUNK, SSM_HEAD_DIM

    def body(xbc_ref, dtr_ref, dtrT_ref, prow_ref, pcol_ref, y_ref, sin_ref, state):
        @pl.when(pl.program_id(1) == 0)
        def _():
            state[...] = jnp.zeros_like(state)

        q = _ssd_common(xbc_ref, dtr_ref, dtrT_ref, prow_ref, pcol_ref)
        st = state[...]
        sin_ref[0] = st
        xd = q["x"] * q["dt_x"]
        g = _dot(q["cm"], q["bm"], NT)
        for r in range(SSM_HPG):
            sl = slice(r * p, (r + 1) * p)
            diff = q["cs"][:, r:r + 1] - q["cs_t"][r:r + 1, :]
            lm = jnp.where(q["li"] >= q["si"], jnp.exp(jnp.minimum(diff, 0.0)), 0.0)
            y_ref[:, sl] = _dot(g * lm, xd[:, sl], NN)
        y_ref[...] += q["e_x"] * _dot(q["cm"], st, NN) + q["d_x"] * q["x"]
        state[...] = q["el_x"] * st + _dot(q["bm"].T, xd * q["dec_x"], NN)

    return pl.pallas_call(
        body, name=name, grid=(SSM_GROUPS, nc),
        in_specs=[pl.BlockSpec((l, XBC_GROUP_W), lambda g, c: (c, g)),
                  pl.BlockSpec((1, l, SSM_HPG), lambda g, c: (g, c, 0)),
                  pl.BlockSpec((1, SSM_HPG, l), lambda g, c: (g, 0, c)),
                  pl.BlockSpec((1, 8, 8), lambda g, c: (g, 0, 0)),
                  pl.BlockSpec((1, 8, 8), lambda g, c: (g, 0, 0))],
        out_specs=[pl.BlockSpec((l, SSM_GROUP_W), lambda g, c: (c, g)),
                   pl.BlockSpec((1, SSM_D_STATE, SSM_GROUP_W), lambda g, c: (c, 0, g))],
        out_shape=[jax.ShapeDtypeStruct((t, SSM_D_INNER), f32),
                   jax.ShapeDtypeStruct((nc, SSM_D_STATE, SSM_D_INNER), f32)],
        scratch_shapes=[pltpu.VMEM((SSM_D_STATE, SSM_GROUP_W), f32)],
        compiler_params=pltpu.CompilerParams(dimension_semantics=("parallel", "arbitrary")),
    )(xbc, dtr, dtr_t, par_row, par_col)


def ssd_bwd(xbc, dtr, dtr_t, par_row, par_col, s_in, dy, *, name):
    t = xbc.shape[0]
    nc = t // SSM_CHUNK
    l, p = SSM_CHUNK, SSM_HEAD_DIM

    def body(xbc_ref, dtr_ref, dtrT_ref, prow_ref, pcol_ref, sin_ref, dy_ref, dxbc_ref, ddtr_ref, dpar_ref,
             dstate, yd_buf, dxd_buf):
        @pl.when(pl.program_id(1) == 0)
        def _():
            dstate[...] = jnp.zeros_like(dstate)
            dpar_ref[...] = jnp.zeros_like(dpar_ref)

        q = _ssd_common(xbc_ref, dtr_ref, dtrT_ref, prow_ref, pcol_ref)
        x, bm, cm, ex, li, si = q["x"], q["bm"], q["cm"], q["ex"], q["li"], q["si"]
        e_x, el_x, dec_x = q["e_x"], q["el_x"], q["dec_x"]
        st = sin_ref[0]
        dst = dstate[...]
        dy = dy_ref[...]
        xd = x * q["dt_x"]
        g = _dot(cm, bm, NT)
        dg = jnp.zeros((l, l), f32)
        for r in range(SSM_HPG):
            sl = slice(r * p, (r + 1) * p)
            diff = q["cs"][:, r:r + 1] - q["cs_t"][r:r + 1, :]
            lm = jnp.where(li >= si, jnp.exp(jnp.minimum(diff, 0.0)), 0.0)
            m = (g * lm).astype(bf16)
            xdh, dyh = xd[:, sl].astype(bf16), dy[:, sl].astype(bf16)
            yd_buf[:, sl] = _dot(m, xdh, NN)
            dxd_buf[:, sl] = _dot(m, dyh, TN)
            dg = dg + _dot(dyh, xdh, NT) * lm
        yd, dxd_diag = yd_buf[...], dxd_buf[...]
        yo = e_x * _dot(cm, st, NN)
        dz = e_x * dy
        wv = _dot(bm, dst, NN)
        xw = xd * wv * dec_x
        row8 = lax.broadcasted_iota(jnp.int32, (l, SSM_HPG), 0)
        dy_b, xd_b = dy.astype(bf16).astype(f32), xd.astype(bf16).astype(f32)
        dcs = _dot_01(dy_b * yd - xd_b * dxd_diag + dy * yo - xw, ex, NT, 0, 3)
        tail = jnp.sum(xw, axis=0, keepdims=True) + el_x * jnp.sum(dst * st, axis=0, keepdims=True)
        dcl = _dot_01(jnp.broadcast_to(tail, (SSM_HPG, SSM_GROUP_W)), ex, NT, 0, 3)[0:1, :]
        dcs = dcs + jnp.where(row8 == l - 1, dcl, 0.0)
        dda = _dot_01(q["tri"], dcs, TN, 1, 3)
        dxd = dxd_diag + dec_x * wv
        ddt = _dot_01(dxd * x, ex, NT, 0, 3) + dda * q["a_row"]
        ddtr = ddt * _sigmoid(q["dtr"] + q["bias_row"])
        ddtr_ref[0] = ddtr
        dd = _dot_01(jnp.broadcast_to(jnp.sum(dy * x, axis=0, keepdims=True), (SSM_HPG, SSM_GROUP_W)), ex, NT, 0, 2)[0:1, :]
        dpar_ref[0, 0:1, :] += jnp.sum(ddtr, axis=0, keepdims=True)
        dpar_ref[0, 1:2, :] += jnp.sum(dda * q["dt"], axis=0, keepdims=True) * q["a_row"]
        dpar_ref[0, 2:3, :] += dd
        dxbc_ref[:, 0:SSM_GROUP_W] = dxd * q["dt_x"] + q["d_x"] * dy
        dxbc_ref[:, SSM_GROUP_W:SSM_GROUP_W + SSM_D_STATE] = _dot(dg, cm, TN) + _dot(xd * dec_x, dst, NT)
        dxbc_ref[:, SSM_GROUP_W + SSM_D_STATE:XBC_GROUP_W] = _dot(dg, bm, NN) + _dot(dz, st, NT)
        dstate[...] = _dot(cm.T, dz, NN) + el_x * dst

    rc = lambda c: nc - 1 - c
    return pl.pallas_call(
        body, name=name, grid=(SSM_GROUPS, nc),
        in_specs=[pl.BlockSpec((l, XBC_GROUP_W), lambda g, c: (rc(c), g)),
                  pl.BlockSpec((1, l, SSM_HPG), lambda g, c: (g, rc(c), 0)),
                  pl.BlockSpec((1, SSM_HPG, l), lambda g, c: (g, 0, rc(c))),
                  pl.BlockSpec((1, 8, 8), lambda g, c: (g, 0, 0)),
                  pl.BlockSpec((1, 8, 8), lambda g, c: (g, 0, 0)),
                  pl.BlockSpec((1, SSM_D_STATE, SSM_GROUP_W), lambda g, c: (rc(c), 0, g)),
                  pl.BlockSpec((l, SSM_GROUP_W), lambda g, c: (rc(c), g))],
        out_specs=[pl.BlockSpec((l, XBC_GROUP_W), lambda g, c: (rc(c), g)),
                   pl.BlockSpec((1, l, SSM_HPG), lambda g, c: (g, rc(c), 0)),
                   pl.BlockSpec((1, 8, 8), lambda g, c: (g, 0, 0))],
        out_shape=[jax.ShapeDtypeStruct((t, SSM_CONV_DIM), f32),
                   jax.ShapeDtypeStruct((SSM_GROUPS, t, SSM_HPG), f32),
                   jax.ShapeDtypeStruct((SSM_GROUPS, 8, 8), f32)],
        scratch_shapes=[pltpu.VMEM((SSM_D_STATE, SSM_GROUP_W), f32), pltpu.VMEM((l, SSM_GROUP_W), f32),
                        pltpu.VMEM((l, SSM_GROUP_W), f32)],
        compiler_params=pltpu.CompilerParams(dimension_semantics=("parallel", "arbitrary")),
    )(xbc, dtr, dtr_t, par_row, par_col, s_in, dy)


def gnorm_fwd(y, proj, w, *, name):
    t = y.shape[0]
    tr = _pick(t, 512, 8)
    gw = SSM_GROUP_W
    zb = OFF_Z // gw

    def body(y_ref, z_ref, w_ref, o_ref):
        y2 = y_ref[...] * _silu(z_ref[...])
        r = lax.rsqrt(jnp.mean(y2 * y2, axis=-1, keepdims=True) + RMS_EPS)
        o_ref[...] = (y2 * r * w_ref[...]).astype(bf16)

    return pl.pallas_call(
        body, name=name, grid=(SSM_GROUPS, t // tr),
        in_specs=[pl.BlockSpec((tr, gw), lambda g, i: (i, g)), pl.BlockSpec((tr, gw), lambda g, i: (i, zb + g)),
                  pl.BlockSpec((1, gw), lambda g, i: (0, g))],
        out_specs=pl.BlockSpec((tr, gw), lambda g, i: (i, g)), out_shape=jax.ShapeDtypeStruct((t, SSM_D_INNER), bf16),
    )(y, proj, w)


def gnorm_bwd(y, proj, w, dout, dst, *, name):
    t = y.shape[0]
    tr = _pick(t, 512, 8)
    gw = SSM_GROUP_W
    zb = OFF_Z // gw

    def body(_, y_ref, z_ref, w_ref, do_ref, dy_ref, dz_ref, dw_ref):
        yv, zv = y_ref[...], z_ref[...]
        sz = _silu(zv)
        y2 = yv * sz
        dy2, dw = _rms_bwd_math(y2, w_ref[...], do_ref[...].astype(f32))
        dy_ref[...] = dy2 * sz
        dz_ref[...] = (dy2 * yv * _dsilu(zv)).astype(bf16)

        @pl.when(pl.program_id(1) == 0)
        def _():
            dw_ref[...] = jnp.zeros_like(dw_ref)

        dw_ref[...] += dw

    tile = pl.BlockSpec((tr, gw), lambda g, i: (i, g))
    vec = pl.BlockSpec((1, gw), lambda g, i: (0, g))
    return pl.pallas_call(
        body, name=name, grid=(SSM_GROUPS, t // tr),
        in_specs=[ANY, tile, pl.BlockSpec((tr, gw), lambda g, i: (i, zb + g)), vec, tile],
        out_specs=[tile, pl.BlockSpec((tr, gw), lambda g, i: (i, zb + g)), vec],
        out_shape=[jax.ShapeDtypeStruct((t, SSM_D_INNER), f32), jax.ShapeDtypeStruct(dst.shape, bf16),
                   jax.ShapeDtypeStruct((1, SSM_D_INNER), f32)],
        input_output_aliases={0: 1},
        compiler_params=pltpu.CompilerParams(dimension_semantics=("parallel", "arbitrary")),
    )(dst, y, proj, w, dout)


LRU_ROWS = 512


def _lru_gates(uv, wr_ref, wi_ref, br_ref, bi_ref, lam_ref):
    rg = _sigmoid(_dot(uv, wr_ref[0], NN) + br_ref[...])
    ig = _sigmoid(_dot(uv, wi_ref[0], NN) + bi_ref[...])
    sp = _softplus(-lam_ref[...])
    la = -LRU_C * rg * sp
    a = jnp.exp(la)
    s = jnp.sqrt(jnp.maximum(-_expm1(2.0 * la), 0.0))
    return rg, ig, sp, la, a, s


def lru_fwd(u, proj, w_r, b_r, w_i, b_i, lam, *, name):
    t = u.shape[0]
    r = LRU_ROWS
    lb = LRU_BLOCK
    yb = OFF_LY // lb

    def body(u_ref, y_ref, wr_ref, br_ref, wi_ref, bi_ref, lam_ref, h_ref, o_ref, carry):
        @pl.when(pl.program_id(1) == 0)
        def _():
            carry[...] = jnp.zeros_like(carry)

        uv = u_ref[...]
        _, ig, _, _, a, s = _lru_gates(uv, wr_ref, wi_ref, br_ref, bi_ref, lam_ref)
        b = s * ig * uv
        row = lax.broadcasted_iota(jnp.int32, (r, lb), 0)
        d = 1
        while d < r:
            keep = row >= d
            b = b + a * jnp.where(keep, pltpu.roll(b, d, 0), 0.0)
            a = a * jnp.where(keep, pltpu.roll(a, d, 0), 1.0)
            d *= 2
        h = b + a * carry[0:1, :]
        carry[0:1, :] = h[r - 1:r, :]
        h_ref[...] = h
        o_ref[...] = (h * _gelu(y_ref[...])).astype(bf16)

    tile = pl.BlockSpec((r, lb), lambda hb, j: (j, hb))
    vec = pl.BlockSpec((1, lb), lambda hb, j: (0, hb))
    wsp = pl.BlockSpec((1, lb, lb), lambda hb, j: (hb, 0, 0))
    return pl.pallas_call(
        body, name=name, grid=(LRU_BLOCKS, t // r),
        in_specs=[tile, pl.BlockSpec((r, lb), lambda hb, j: (j, yb + hb)), wsp, vec, wsp, vec, vec],
        out_specs=[tile, tile],
        out_shape=[jax.ShapeDtypeStruct((t, LRU_WIDTH), f32), jax.ShapeDtypeStruct((t, LRU_WIDTH), bf16)],
        scratch_shapes=[pltpu.VMEM((8, lb), f32)],
        compiler_params=pltpu.CompilerParams(dimension_semantics=("parallel", "arbitrary")),
    )(u, proj, w_r, b_r, w_i, b_i, lam)


def lru_bwd(u, proj, hseq, dout, w_r, b_r, w_i, b_i, lam, dst, *, name):
    t = u.shape[0]
    r = LRU_ROWS
    nt = t // r
    lb = LRU_BLOCK
    yb = OFF_LY // lb

    def body(_, u_ref, y_ref, h_ref, hp_ref, do_ref, wr_ref, br_ref, wi_ref, bi_ref, lam_ref,
             du_ref, dy_ref, dwr_ref, dwi_ref, dbr_ref, dbi_ref, dlam_ref, carry_dh, carry_a):
        j = pl.program_id(1)

        @pl.when(j == 0)
        def _():
            carry_dh[...] = jnp.zeros_like(carry_dh)
            carry_a[...] = jnp.zeros_like(carry_a)
            dwr_ref[...] = jnp.zeros_like(dwr_ref)
            dwi_ref[...] = jnp.zeros_like(dwi_ref)
            dbr_ref[...] = jnp.zeros_like(dbr_ref)
            dbi_ref[...] = jnp.zeros_like(dbi_ref)
            dlam_ref[...] = jnp.zeros_like(dlam_ref)

        uv = u_ref[...]
        yv = y_ref[...]
        hv = h_ref[...]
        dov = do_ref[...]
        rg, ig, sp, la, a, s = _lru_gates(uv, wr_ref, wi_ref, br_ref, bi_ref, lam_ref)
        dy_ref[...] = (dov * hv * _dgelu(yv)).astype(bf16)
        gq = dov * _gelu(yv)
        row = lax.broadcasted_iota(jnp.int32, (r, lb), 0)
        an = jnp.where(row < r - 1, pltpu.roll(a, r - 1, 0), carry_a[0:1, :])
        d = 1
        while d < r:
            keep = row < r - d
            gq = gq + an * jnp.where(keep, pltpu.roll(gq, r - d, 0), 0.0)
            an = an * jnp.where(keep, pltpu.roll(an, r - d, 0), 1.0)
            d *= 2
        dh = gq + an * carry_dh[0:1, :]
        carry_dh[0:1, :] = dh[0:1, :]
        carry_a[0:1, :] = a[0:1, :]
        first = jnp.where(j == nt - 1, 0.0, 1.0) * hp_ref[7:8, :]
        hprev = jnp.where(row >= 1, pltpu.roll(hv, 1, 0), first)
        da = dh * hprev
        iu = ig * uv
        e2 = jnp.exp(2.0 * la)
        dla = da * a - dh * iu * e2 / jnp.maximum(s, 1e-30)
        drp = dla * (-LRU_C * sp) * rg * (1.0 - rg)
        dip = dh * s * uv * ig * (1.0 - ig)
        dlam_ref[...] += jnp.sum(dla * (LRU_C * rg) * _sigmoid(-lam_ref[...]), axis=0, keepdims=True)
        du_ref[...] = dh * s * ig + _dot(drp, wr_ref[0], NT) + _dot(dip, wi_ref[0], NT)
        dwr_ref[0] += _dot(uv, drp, TN)
        dwi_ref[0] += _dot(uv, dip, TN)
        dbr_ref[...] += jnp.sum(drp, axis=0, keepdims=True)
        dbi_ref[...] += jnp.sum(dip, axis=0, keepdims=True)

    rj = lambda j: nt - 1 - j
    tile = pl.BlockSpec((r, lb), lambda hb, j: (rj(j), hb))
    vec = pl.BlockSpec((1, lb), lambda hb, j: (0, hb))
    wsp = pl.BlockSpec((1, lb, lb), lambda hb, j: (hb, 0, 0))
    hprev_spec = pl.BlockSpec((8, lb), lambda hb, j: (jnp.maximum(rj(j) * (r // 8) - 1, 0), hb))
    ywin = pl.BlockSpec((r, lb), lambda hb, j: (rj(j), yb + hb))
    return pl.pallas_call(
        body, name=name, grid=(LRU_BLOCKS, nt),
        in_specs=[ANY, tile, ywin, tile, hprev_spec, tile, wsp, vec, wsp, vec, vec],
        out_specs=[tile, ywin, wsp, wsp, vec, vec, vec],
        out_shape=[jax.ShapeDtypeStruct((t, LRU_WIDTH), f32), jax.ShapeDtypeStruct(dst.shape, bf16),
                   jax.ShapeDtypeStruct((LRU_BLOCKS, lb, lb), f32), jax.ShapeDtypeStruct((LRU_BLOCKS, lb, lb), f32),
                   jax.ShapeDtypeStruct((1, LRU_WIDTH), f32), jax.ShapeDtypeStruct((1, LRU_WIDTH), f32),
                   jax.ShapeDtypeStruct((1, LRU_WIDTH), f32)],
        input_output_aliases={0: 1},
        scratch_shapes=[pltpu.VMEM((8, lb), f32), pltpu.VMEM((8, lb), f32)],
        compiler_params=pltpu.CompilerParams(dimension_semantics=("parallel", "arbitrary")),
    )(dst, u, proj, hseq, hseq, dout, w_r, b_r, w_i, b_i, lam)


def merge_fwd(proj, bg, y_ssm, y_lru, *, name):
    t, d = y_ssm.shape
    tr = _pick(t, 256, 8)
    gb = OFF_GATES // d

    def body(gs_ref, gl_ref, bs_ref, bl_ref, ys_ref, yl_ref, o_ref):
        gs = _sigmoid(gs_ref[...] + bs_ref[...])
        gl = _sigmoid(gl_ref[...] + bl_ref[...])
        o_ref[...] = (gs * ys_ref[...].astype(f32) + gl * yl_ref[...].astype(f32)).astype(bf16)

    row = pl.BlockSpec((tr, d), lambda i: (i, 0))
    return pl.pallas_call(
        body, name=name, grid=(t // tr,),
        in_specs=[pl.BlockSpec((tr, d), lambda i: (i, gb)), pl.BlockSpec((tr, d), lambda i: (i, gb + 1)),
                  pl.BlockSpec((1, d), lambda i: (0, 0)), pl.BlockSpec((1, d), lambda i: (0, 1)), row, row],
        out_specs=row, out_shape=jax.ShapeDtypeStruct((t, d), bf16),
    )(proj, proj, bg, bg, y_ssm, y_lru)


def merge_bwd(proj, bg, y_ssm, y_lru, dmix, *, name):
    t, d = y_ssm.shape
    tr = _pick(t, 256, 8)
    gb = OFF_GATES // d

    def body(gs_ref, gl_ref, bs_ref, bl_ref, ys_ref, yl_ref, dm_ref, dg_ref, dys_ref, dyl_ref, dbg_ref):
        gs = _sigmoid(gs_ref[...] + bs_ref[...])
        gl = _sigmoid(gl_ref[...] + bl_ref[...])
        dm = dm_ref[...].astype(f32)
        dys_ref[...] = (dm * gs).astype(bf16)
        dyl_ref[...] = (dm * gl).astype(bf16)
        dgs = dm * ys_ref[...].astype(f32) * gs * (1.0 - gs)
        dgl = dm * yl_ref[...].astype(f32) * gl * (1.0 - gl)
        dg_ref[:, 0:d] = dgs.astype(bf16)
        dg_ref[:, d:2 * d] = dgl.astype(bf16)

        @pl.when(pl.program_id(0) == 0)
        def _():
            dbg_ref[...] = jnp.zeros_like(dbg_ref)

        dbg_ref[:, 0:d] += jnp.sum(dgs, axis=0, keepdims=True)
        dbg_ref[:, d:2 * d] += jnp.sum(dgl, axis=0, keepdims=True)

    row = pl.BlockSpec((tr, d), lambda i: (i, 0))
    return pl.pallas_call(
        body, name=name, grid=(t // tr,),
        in_specs=[pl.BlockSpec((tr, d), lambda i: (i, gb)), pl.BlockSpec((tr, d), lambda i: (i, gb + 1)),
                  pl.BlockSpec((1, d), lambda i: (0, 0)), pl.BlockSpec((1, d), lambda i: (0, 1)), row, row, row],
        out_specs=[pl.BlockSpec((tr, 2 * d), lambda i: (i, OFF_GATES // (2 * d))), row, row,
                   pl.BlockSpec((1, 2 * d), lambda i: (0, 0))],
        out_shape=[jax.ShapeDtypeStruct((t, PROJ_W), bf16), jax.ShapeDtypeStruct((t, d), bf16),
                   jax.ShapeDtypeStruct((t, d), bf16), jax.ShapeDtypeStruct((1, 2 * d), f32)],
        compiler_params=pltpu.CompilerParams(dimension_semantics=("arbitrary",)),
    )(proj, proj, bg, bg, y_ssm, y_lru, dmix)


def swiglu_fwd(ff, *, name):
    t = ff.shape[0]
    hd = FFN_HIDDEN
    tr = _pick(t, 256, 8)

    def body(f_ref, o_ref):
        o_ref[...] = (_silu(f_ref[:, 0:hd].astype(f32)) * f_ref[:, hd:2 * hd].astype(f32)).astype(bf16)

    return pl.pallas_call(
        body, name=name, grid=(t // tr,), in_specs=[pl.BlockSpec((tr, 2 * hd), lambda i: (i, 0))],
        out_specs=pl.BlockSpec((tr, hd), lambda i: (i, 0)), out_shape=jax.ShapeDtypeStruct((t, hd), bf16),
    )(ff)


def swiglu_bwd(ff, dact, *, name):
    t = ff.shape[0]
    hd = FFN_HIDDEN
    tr = _pick(t, 256, 8)

    def body(f_ref, d_ref, o_ref):
        gate, up, dv = f_ref[:, 0:hd].astype(f32), f_ref[:, hd:2 * hd].astype(f32), d_ref[...].astype(f32)
        o_ref[:, 0:hd] = (dv * up * _dsilu(gate)).astype(bf16)
        o_ref[:, hd:2 * hd] = (dv * _silu(gate)).astype(bf16)

    return pl.pallas_call(
        body, name=name, grid=(t // tr,),
        in_specs=[pl.BlockSpec((tr, 2 * hd), lambda i: (i, 0)), pl.BlockSpec((tr, hd), lambda i: (i, 0))],
        out_specs=pl.BlockSpec((tr, 2 * hd), lambda i: (i, 0)), out_shape=jax.ShapeDtypeStruct((t, 2 * hd), bf16),
    )(ff, dact)


def _adam_math(w, g, m, v):
    m = ADAM_B1 * m + (1.0 - ADAM_B1) * g
    v = ADAM_B2 * v + (1.0 - ADAM_B2) * (g * g)
    m_hat = m / (1.0 - ADAM_B1 ** ADAM_STEP)
    v_hat = v / (1.0 - ADAM_B2 ** ADAM_STEP)
    delta = -ADAM_LR * (m_hat / (jnp.sqrt(v_hat) + ADAM_EPS) + ADAM_WD * w)
    return delta, m, v


def _row_tile(rows, cols):
    cap = max(8, (1 << 18) // cols)
    return _pick(rows, cap, 8) if rows % 8 == 0 else rows


def adamw(w, g, m, v, *, name):
    rows, cols = w.shape
    tr = _row_tile(rows, cols)

    def body(w_ref, g_ref, m_ref, v_ref, d_ref, nm_ref, nv_ref):
        d, nm, nv = _adam_math(w_ref[...], g_ref[...], m_ref[...], v_ref[...])
        d_ref[...] = d
        nm_ref[...] = nm
        nv_ref[...] = nv

    tile = pl.BlockSpec((tr, cols), lambda i: (i, 0))
    return pl.pallas_call(
        body, name=name, grid=(rows // tr,), in_specs=[tile] * 4, out_specs=[tile] * 3,
        out_shape=[jax.ShapeDtypeStruct((rows, cols), f32)] * 3,
    )(w, g, m, v)


def adamw_many(ws, gs, ms, vs, *, name):
    n = len(ws)

    def body(*refs):
        for i in range(n):
            d, nm, nv = _adam_math(refs[i][...], refs[n + i][...], refs[2 * n + i][...], refs[3 * n + i][...])
            refs[4 * n + 3 * i][...] = d
            refs[4 * n + 3 * i + 1][...] = nm
            refs[4 * n + 3 * i + 2][...] = nv

    outs = pl.pallas_call(
        body, name=name, out_shape=[jax.ShapeDtypeStruct(w.shape, f32) for w in ws for _ in range(3)],
    )(*ws, *gs, *ms, *vs)
    return [tuple(outs[3 * i:3 * i + 3]) for i in range(n)]


def pair_add(dw, rbuf, idx, *, name):
    n, rows, cols = dw.shape
    hr = rows // 2
    tr = _row_tile(hr, cols)
    nrt = hr // tr

    def body(idx_ref, a_ref, b_ref, o_ref, own_ref):
        s = a_ref[...] + b_ref[...]
        o_ref[...] = s.astype(bf16)

        @pl.when(pl.program_id(1) == idx_ref[0])
        def _():
            own_ref[...] = s[0]

    return pl.pallas_call(
        body, name=name,
        grid_spec=pltpu.PrefetchScalarGridSpec(
            num_scalar_prefetch=1, grid=(nrt, n),
            in_specs=[pl.BlockSpec((1, tr, cols), lambda i, k, idx: (k, idx[1] * nrt + i, 0)),
                      pl.BlockSpec((1, tr, cols), lambda i, k, idx: (k, i, 0))],
            out_specs=[pl.BlockSpec((1, tr, cols), lambda i, k, idx: (k, i, 0)),
                       pl.BlockSpec((tr, cols), lambda i, k, idx: (i, 0))]),
        out_shape=[jax.ShapeDtypeStruct((n, hr, cols), bf16), jax.ShapeDtypeStruct((hr, cols), f32)],
    )(idx, dw, rbuf)


def chip_sum(own, rbuf, idx, *, name):
    hr, cols = own.shape
    tr = _row_tile(hr, cols)
    nrt = hr // tr

    def body(idx_ref, a_ref, b_ref, o_ref):
        o_ref[...] = ((a_ref[...] + b_ref[0].astype(f32)) + b_ref[1].astype(f32)) + b_ref[2].astype(f32)

    return pl.pallas_call(
        body, name=name,
        grid_spec=pltpu.PrefetchScalarGridSpec(
            num_scalar_prefetch=1, grid=(nrt,),
            in_specs=[pl.BlockSpec((tr, cols), lambda i, idx: (i, 0)),
                      pl.BlockSpec((3, tr, cols), lambda i, idx: (0, i, 0))],
            out_specs=pl.BlockSpec((tr, cols), lambda i, idx: (idx[1] * nrt + i, 0))),
        out_shape=jax.ShapeDtypeStruct((2 * hr, cols), f32),
    )(idx, own, rbuf)


def sum8(rbuf, *, name):
    n, rows, cols = rbuf.shape
    tr = _row_tile(rows, cols * n)

    def body(a_ref, o_ref):
        acc = a_ref[0]
        for k in range(1, n):
            acc = acc + a_ref[k]
        o_ref[...] = acc

    return pl.pallas_call(
        body, name=name, grid=(rows // tr,), in_specs=[pl.BlockSpec((n, tr, cols), lambda i: (0, i, 0))],
        out_specs=pl.BlockSpec((tr, cols), lambda i: (i, 0)), out_shape=jax.ShapeDtypeStruct((rows, cols), f32),
    )(rbuf)


def _coords():
    return lax.axis_index("x"), lax.axis_index("y"), lax.axis_index("c")


def _other_chips(x, y):
    return [(1 - x, y), (x, 1 - y), (1 - x, 1 - y)]


def gather_weights(shards, *, name):
    n = len(shards)
    halves = [s.shape[0] // 2 for s in shards]

    def body(*refs):
        ins, outs = refs[:n], refs[n:2 * n]
        send1, recv1, send2, recv2 = refs[2 * n:]
        x, y, c = _coords()
        me = 2 * x + y
        chips = _other_chips(x, y)
        sibling = (x, y, 1 - c)

        def half(i, k, hc):
            return outs[i].at[k, pl.ds(hc * halves[i], halves[i]), :]

        def ici(i, j):
            return pltpu.make_async_remote_copy(
                src_ref=ins[i].at[pl.ds(c * halves[i], halves[i]), :], dst_ref=half(i, me, c),
                send_sem=send1.at[i, j], recv_sem=recv1.at[i, j], device_id=(*chips[j], c), device_id_type=MESH)

        def landed(i, j):
            kj = 2 * chips[j][0] + chips[j][1]
            return pltpu.make_async_remote_copy(
                src_ref=half(i, kj, c), dst_ref=half(i, kj, c),
                send_sem=send2.at[i, j], recv_sem=recv1.at[i, j], device_id=sibling, device_id_type=MESH)

        def from_sibling(i, j):
            kj = 2 * chips[j][0] + chips[j][1]
            return pltpu.make_async_remote_copy(
                src_ref=half(i, kj, 1 - c), dst_ref=half(i, kj, 1 - c),
                send_sem=send2.at[i, j], recv_sem=recv2.at[i, j], device_id=sibling, device_id_type=MESH)

        def d2d(i, j):
            kj = 2 * chips[j][0] + chips[j][1]
            return pltpu.make_async_remote_copy(
                src_ref=half(i, kj, c), dst_ref=half(i, kj, c),
                send_sem=send2.at[i, j], recv_sem=recv2.at[i, j], device_id=sibling, device_id_type=MESH)

        for j in range(3):
            for i in range(n):
                ici(i, j).start()
        for j in range(3):
            for i in range(n):
                landed(i, j).wait_recv()
                d2d(i, j).start()
        for j in range(3):
            for i in range(n):
                from_sibling(i, j).wait_recv()
        for j in range(3):
            for i in range(n):
                ici(i, j).wait_send()
                d2d(i, j).wait_send()

    return pl.pallas_call(
        body, name=name, in_specs=[ANY] * n, out_specs=[ANY] * n,
        out_shape=[jax.ShapeDtypeStruct((N_CHIPS,) + s.shape, s.dtype) for s in shards],
        scratch_shapes=[pltpu.SemaphoreType.DMA((n, 3))] * 4,
    )(*shards)


def pair_exchange(grads, *, name):
    n = len(grads)
    halves = [g.shape[1] // 2 for g in grads]

    def body(*refs):
        ins, outs = refs[:n], refs[n:2 * n]
        send, recv = refs[2 * n:]
        x, y, c = _coords()
        cps = [pltpu.make_async_remote_copy(
            src_ref=ins[i].at[:, pl.ds((1 - c) * halves[i], halves[i]), :], dst_ref=outs[i],
            send_sem=send.at[i], recv_sem=recv.at[i], device_id=(x, y, 1 - c), device_id_type=MESH) for i in range(n)]
        for cp in cps:
            cp.start()
        for cp in cps:
            cp.wait()

    return pl.pallas_call(
        body, name=name, in_specs=[ANY] * n, out_specs=[ANY] * n,
        out_shape=[jax.ShapeDtypeStruct((N_CHIPS, g.shape[1] // 2, g.shape[2]), g.dtype) for g in grads],
        scratch_shapes=[pltpu.SemaphoreType.DMA((n,))] * 2,
    )(*grads)


def pair_gather(bufs, *, name):
    n = len(bufs)

    def body(*refs):
        ins, outs = refs[:n], refs[n:2 * n]
        send, recv = refs[2 * n:]
        x, y, c = _coords()
        cps = []
        for i in range(n):
            hr = ins[i].shape[0] // 2
            cps.append(pltpu.make_async_remote_copy(
                src_ref=ins[i].at[pl.ds(c * hr, hr), :], dst_ref=outs[i].at[pl.ds(c * hr, hr), :],
                send_sem=send.at[i], recv_sem=recv.at[i], device_id=(x, y, 1 - c), device_id_type=MESH))
        for cp in cps:
            cp.start()
        for i in range(n):
            hr = ins[i].shape[0] // 2
            pltpu.make_async_remote_copy(
                src_ref=ins[i].at[pl.ds((1 - c) * hr, hr), :], dst_ref=outs[i].at[pl.ds((1 - c) * hr, hr), :],
                send_sem=send.at[i], recv_sem=recv.at[i], device_id=(x, y, 1 - c), device_id_type=MESH).wait_recv()
        for cp in cps:
            cp.wait_send()

    return pl.pallas_call(
        body, name=name, in_specs=[ANY] * n, out_specs=[ANY] * n,
        out_shape=[jax.ShapeDtypeStruct(b.shape, b.dtype) for b in bufs],
        input_output_aliases={i: i for i in range(n)},
        scratch_shapes=[pltpu.SemaphoreType.DMA((n,))] * 2,
    )(*bufs)


def all_exchange(buf, *, name):
    rows, cols = buf.shape

    def body(in_ref, out_ref, send, recv):
        x, y, c = _coords()
        me = 4 * x + 2 * y + c
        cps = []
        for d in range(1, 8):
            px = 1 - x if d & 4 else x
            py = 1 - y if d & 2 else y
            pc = 1 - c if d & 1 else c
            cps.append(pltpu.make_async_remote_copy(
                src_ref=in_ref, dst_ref=out_ref.at[me], send_sem=send.at[d - 1], recv_sem=recv.at[d - 1],
                device_id=(px, py, pc), device_id_type=MESH))
        for cp in cps:
            cp.start()
        for d in range(1, 8):
            px = 1 - x if d & 4 else x
            py = 1 - y if d & 2 else y
            pc = 1 - c if d & 1 else c
            src = 4 * px + 2 * py + pc
            pltpu.make_async_remote_copy(
                src_ref=in_ref, dst_ref=out_ref.at[src], send_sem=send.at[d - 1], recv_sem=recv.at[d - 1],
                device_id=(px, py, pc), device_id_type=MESH).wait_recv()
        for cp in cps:
            cp.wait_send()

    return pl.pallas_call(
        body, name=name, in_specs=[ANY], out_specs=ANY,
        out_shape=jax.ShapeDtypeStruct((8, rows, cols), buf.dtype),
        scratch_shapes=[pltpu.SemaphoreType.DMA((7,)), pltpu.SemaphoreType.DMA((7,))],
    )(buf)


HBM = pl.BlockSpec(memory_space=pltpu.HBM)
SEM = pl.BlockSpec(memory_space=pltpu.SEMAPHORE)
EFFECT = pltpu.SideEffectType.DATAFLOW_SIDE_EFFECTING


def split_start(arrays, after, copies, sem_shape, *, name):
    na = len(arrays)

    def body(*refs):
        for cp in copies(refs[:na], refs[na + 1], refs[na + 2]):
            cp.start()
        refs[-1][...] = jnp.zeros((8, 128), f32)

    outs = pl.pallas_call(
        body, name=name,
        out_shape=(pltpu.SemaphoreType.DMA(sem_shape), pltpu.SemaphoreType.DMA(sem_shape),
                   *[pltpu.HBM(a.shape, a.dtype) for a in arrays], jax.ShapeDtypeStruct((8, 128), f32)),
        in_specs=[HBM] * na + [ANY], out_specs=(SEM, SEM, *[HBM] * na, pl.BlockSpec(memory_space=pltpu.VMEM)),
        input_output_aliases={i: 2 + i for i in range(na)},
        compiler_params=pltpu.CompilerParams(has_side_effects=EFFECT),
    )(*[pltpu.with_memory_space_constraint(a, pltpu.HBM) for a in arrays], after)
    return outs[0], outs[1], list(outs[2:2 + na]), outs[-1]


def split_wait(send, recv, arrays, after, copies, *, name):
    na = len(arrays)

    def body(*refs):
        for cp in copies(refs[:na], refs[na], refs[na + 1]):
            cp.wait_send()
            cp.wait_recv()

    outs = pl.pallas_call(
        body, name=name, out_shape=tuple(pltpu.HBM(a.shape, a.dtype) for a in arrays),
        in_specs=[HBM] * na + [SEM, SEM, ANY], out_specs=tuple([HBM] * na),
        input_output_aliases={i: i for i in range(na)},
        compiler_params=pltpu.CompilerParams(has_side_effects=EFFECT),
    )(*arrays, send, recv, after)
    return list(outs)


def gather_copies(n):
    def copies(refs, send, recv):
        x, y, c = _coords()
        me = 2 * x + y
        chips = _other_chips(x, y)
        return [pltpu.make_async_remote_copy(
            src_ref=refs[i], dst_ref=refs[n + i].at[me], send_sem=send.at[3 * i + j], recv_sem=recv.at[3 * i + j],
            device_id=(*chips[j], c), device_id_type=MESH) for j in range(3) for i in range(n)]
    return copies


def pair_copies(n):
    def copies(refs, send, recv):
        x, y, c = _coords()
        cps = []
        for i in range(n):
            hr = refs[i].shape[1] // 2
            cps.append(pltpu.make_async_remote_copy(
                src_ref=refs[i].at[:, pl.ds((1 - c) * hr, hr), :], dst_ref=refs[n + i], send_sem=send.at[i],
                recv_sem=recv.at[i], device_id=(x, y, 1 - c), device_id_type=MESH))
        return cps
    return copies


def all_copies():
    def copies(refs, send, recv):
        x, y, c = _coords()
        me = 4 * x + 2 * y + c
        cps = []
        for d in range(1, 8):
            peer = (1 - x if d & 4 else x, 1 - y if d & 2 else y, 1 - c if d & 1 else c)
            cps.append(pltpu.make_async_remote_copy(
                src_ref=refs[0], dst_ref=refs[1].at[me], send_sem=send.at[d - 1], recv_sem=recv.at[d - 1],
                device_id=peer, device_id_type=MESH))
        return cps
    return copies


def reduce_copies(n):
    def copies(refs, send, recv):
        x, y, c = _coords()
        chips = _other_chips(x, y)
        return [pltpu.make_async_remote_copy(
            src_ref=refs[i].at[2 * chips[j][0] + chips[j][1]], dst_ref=refs[n + i].at[j],
            send_sem=send.at[3 * i + j], recv_sem=recv.at[3 * i + j], device_id=(*chips[j], c), device_id_type=MESH)
            for j in range(3) for i in range(n)]
    return copies


def _pack(arrs):
    flat = []
    for a in arrs:
        v = a.reshape(-1).astype(f32)
        pad = (-v.shape[0]) % 128
        flat.append(jnp.pad(v, (0, pad)) if pad else v)
    v = jnp.concatenate(flat)
    rows = v.shape[0] // 128
    pad_rows = (-rows) % 256
    v = v.reshape(rows, 128)
    return jnp.pad(v, ((0, pad_rows), (0, 0))) if pad_rows else v


def _unpack(buf, shapes):
    out, row = [], 0
    for s in shapes:
        size = math.prod(s)
        rows = -(-size // 128)
        out.append(buf[row:row + rows].reshape(-1)[:size].reshape(s))
        row += rows
    return out


def _ref_of_perm():
    ref = np.arange(IN_PROJ_DIM)
    xbc = ref[4096:7168]
    xbc_p = [np.concatenate([xbc[g * 512:(g + 1) * 512], xbc[2048 + g * 128:2048 + (g + 1) * 128],
                             xbc[2560 + g * 128:2560 + (g + 1) * 128]]) for g in range(SSM_GROUPS)]
    return np.concatenate([ref[0:2048], ref[2048:4096], ref[7200:8480], ref[8480:9760], ref[7168:7200],
                           -np.ones(DT_PAD_W - SSM_HEADS, np.int64)] + xbc_p)


def _runs(vals):
    out, start = [], 0
    for i in range(1, len(vals) + 1):
        if i == len(vals) or not (vals[i] == vals[i - 1] + 1 or (vals[i] < 0 and vals[i - 1] < 0)):
            out.append((start, int(vals[start]), i - start))
            start = i
    return out


def _perm_in_from_shards(g):
    ref_of_perm = _ref_of_perm()
    sw = IN_PROJ_DIM // N_CHIPS
    parts = []
    for _, first, length in _runs(ref_of_perm):
        if first < 0:
            parts.append(jnp.zeros((g.shape[1], length), g.dtype))
            continue
        lo = first
        while lo < first + length:
            k = lo // sw
            hi = min(first + length, (k + 1) * sw)
            parts.append(g[k, :, lo - k * sw:hi - k * sw])
            lo = hi
    return jnp.concatenate(parts, axis=-1)


def _unperm_in_to_shards(w):
    ref_of_perm = _ref_of_perm()
    perm_of_ref = np.zeros(IN_PROJ_DIM, np.int64)
    perm_of_ref[ref_of_perm[ref_of_perm >= 0]] = np.nonzero(ref_of_perm >= 0)[0]
    sw = IN_PROJ_DIM // N_CHIPS
    shards = []
    for k in range(N_CHIPS):
        runs = _runs(perm_of_ref[k * sw:(k + 1) * sw])
        shards.append(jnp.concatenate([w[:, first:first + length] for _, first, length in runs], axis=-1))
    return jnp.stack(shards)


def _perm_xbc_cols(w):
    parts = []
    for g in range(SSM_GROUPS):
        parts += [w[..., g * 512:(g + 1) * 512], w[..., 2048 + g * 128:2048 + (g + 1) * 128],
                  w[..., 2560 + g * 128:2560 + (g + 1) * 128]]
    return jnp.concatenate(parts, axis=-1)


def _unperm_xbc_cols(w):
    xs = [w[..., g * XBC_GROUP_W:g * XBC_GROUP_W + 512] for g in range(SSM_GROUPS)]
    bs = [w[..., g * XBC_GROUP_W + 512:g * XBC_GROUP_W + 640] for g in range(SSM_GROUPS)]
    cs = [w[..., g * XBC_GROUP_W + 640:(g + 1) * XBC_GROUP_W] for g in range(SSM_GROUPS)]
    return jnp.concatenate(xs + bs + cs, axis=-1)


def _from_col_shards(w):
    n, r, c = w.shape
    return jnp.transpose(w, (1, 0, 2)).reshape(r, n * c)


def kernel(x, norm1_w, w_in, b_branch_gate, ssm_conv_w, ssm_conv_b, ssm_dt_bias, ssm_a_log, ssm_d, ssm_norm_w, w_out_ssm, lru_conv_w, lru_conv_b, lru_w_r, lru_b_r, lru_w_i, lru_b_i, lru_lambda, w_out_lru, w_out, norm2_w, w_ffn_in, w_ffn_out, norm_f_w, loss_target, m_norm1_w, m_w_in, m_b_branch_gate, m_ssm_conv_w, m_ssm_conv_b, m_ssm_dt_bias, m_ssm_a_log, m_ssm_d, m_ssm_norm_w, m_w_out_ssm, m_lru_conv_w, m_lru_conv_b, m_lru_w_r, m_lru_b_r, m_lru_w_i, m_lru_b_i, m_lru_lambda, m_w_out_lru, m_w_out, m_norm2_w, m_w_ffn_in, m_w_ffn_out, m_norm_f_w, v_norm1_w, v_w_in, v_b_branch_gate, v_ssm_conv_w, v_ssm_conv_b, v_ssm_dt_bias, v_ssm_a_log, v_ssm_d, v_ssm_norm_w, v_w_out_ssm, v_lru_conv_w, v_lru_conv_b, v_lru_w_r, v_lru_b_r, v_lru_w_i, v_lru_b_i, v_lru_lambda, v_w_out_lru, v_w_out, v_norm2_w, v_w_ffn_in, v_w_ffn_out, v_norm_f_w):
    xi, yi, ci = lax.axis_index("x"), lax.axis_index("y"), lax.axis_index("c")
    me = 2 * xi + yi
    idx = jnp.stack([me, ci]).astype(jnp.int32)
    x2 = x[0]
    tgt = loss_target[0]

    big_names = ["w_in", "w_out_ssm", "w_out_lru", "w_out", "w_ffn_in", "w_ffn_out"]
    big_w = dict(w_in=w_in[0], w_out_ssm=w_out_ssm[0], w_out_lru=w_out_lru[0], w_out=w_out[0], w_ffn_in=w_ffn_in[0],
                 w_ffn_out=w_ffn_out[0])
    big_m = dict(w_in=m_w_in[0], w_out_ssm=m_w_out_ssm[0], w_out_lru=m_w_out_lru[0], w_out=m_w_out[0],
                 w_ffn_in=m_w_ffn_in[0], w_ffn_out=m_w_ffn_out[0])
    big_v = dict(w_in=v_w_in[0], w_out_ssm=v_w_out_ssm[0], w_out_lru=v_w_out_lru[0], w_out=v_w_out[0],
                 w_ffn_in=v_w_ffn_in[0], w_ffn_out=v_w_ffn_out[0])
    conv_pad = jnp.zeros((16, 768), f32).at[0:4, :].set(ssm_conv_w[0]).at[8:12, 0:320].set(lru_conv_w[0])
    mine = [big_w["w_in"].astype(bf16), conv_pad]
    gathered = gather_weights(mine, name="gather_weights")
    g_in, g_conv = [lax.dynamic_update_index_in_dim(g, s, me, 0) for g, s in zip(gathered, mine)]
    w_in_p = _perm_in_from_shards(g_in)
    late_names = big_names[1:]
    late = [big_w[k].astype(bf16) for k in late_names]
    late_lands = [lax.empty((N_CHIPS,) + s.shape, bf16) for s in late]
    g_send, g_recv, g_arrays, g_token = split_start(late + late_lands, g_conv, gather_copies(5), (15,),
                                                    name="gather_late_start")
    ssm_cw_full = _from_col_shards(g_conv[:, 0:4, :])
    lru_cw_full = _from_col_shards(g_conv[:, 8:12, 0:320])
    ssm_cw_p = _perm_xbc_cols(ssm_cw_full)
    ssm_cb_p = _perm_xbc_cols(ssm_conv_b)

    par = jnp.stack([ssm_dt_bias[0], ssm_a_log[0], ssm_d[0]], axis=0).reshape(3, SSM_GROUPS, SSM_HPG)
    par_row = jnp.zeros((SSM_GROUPS, 8, 8), f32).at[:, 0:3, :].set(jnp.transpose(par, (1, 0, 2)))
    par_col = jnp.transpose(par_row, (0, 2, 1))

    hn1 = rms_fwd(x2, norm1_w + g_token[0:1, 0:1], name="rms1_fwd")
    proj = mm(hn1, w_in_p, "nn", name="in_proj")
    t = x2.shape[0]
    dtr = jnp.transpose(proj[:, OFF_DT:OFF_DT + 32].reshape(t, SSM_GROUPS, SSM_HPG), (1, 0, 2))
    dtr_t = jnp.transpose(dtr, (0, 2, 1))
    xbc_pre, xbc_post = conv_fwd(proj, OFF_XBC, SSM_CONV_DIM, ssm_cw_p, ssm_cb_p, silu=True, name="ssm_conv_fwd")
    y_ssd, s_in = ssd_fwd(xbc_post, dtr, dtr_t, par_row, par_col, name="ssd_fwd")
    yn = gnorm_fwd(y_ssd, proj, ssm_norm_w, name="gnorm_fwd")
    g_arrays = split_wait(g_send, g_recv, g_arrays, yn, gather_copies(5), name="gather_late_wait")
    g_out_ssm, g_out_lru, g_out, g_ffn_in, g_ffn_out = [
        lax.dynamic_update_index_in_dim(g, s, me, 0) for g, s in zip(g_arrays[5:], late)]
    w_out_ssm_f = g_out_ssm.reshape(SSM_D_INNER, D_MODEL)
    w_out_lru_f = g_out_lru.reshape(LRU_WIDTH, D_MODEL)
    w_out_f = g_out.reshape(D_MODEL, D_MODEL)
    w_ffn_out_f = g_ffn_out.reshape(FFN_HIDDEN, D_MODEL)
    y_ssm = mm(yn, w_out_ssm_f, "nn", out_dtype=bf16, name="out_ssm")
    (u_lru,) = conv_fwd(proj, OFF_LX, LRU_WIDTH, lru_cw_full, lru_conv_b, silu=False, name="lru_conv_fwd")
    h_lru, o_lru = lru_fwd(u_lru, proj, lru_w_r[0], lru_b_r, lru_w_i[0], lru_b_i, lru_lambda, name="lru_fwd")
    y_lru = mm(o_lru, w_out_lru_f, "nn", out_dtype=bf16, name="out_lru")
    mix = merge_fwd(proj, b_branch_gate, y_ssm, y_lru, name="merge_fwd")
    h1, hn2 = mm(mix, w_out_f, "nn", add=x2, name="out_proj",
                 epi=(epi_rms_fwd, [], [norm2_w], [("row", f32), ("row", bf16)]))
    ff = mm(hn2, g_ffn_in, "nn", b_shards=True, out_dtype=bf16, name="ffn_in")
    act = swiglu_fwd(ff, name="swiglu_fwd")
    dh2, dh2_b, d_norm_f, loss_tile = mm(act, w_ffn_out_f, "nn", add=h1, name="ffn_out",
                                         epi=(epi_loss, [tgt], [norm_f_w.reshape(1, D_MODEL)],
                                              [("row", f32), ("row", bf16), ("vec",), ("tile",)]))

    d_w_ffn_out = mm(act, dh2_b, "tn", name="d_w_ffn_out")
    dact = mm(dh2_b, w_ffn_out_f, "nt", out_dtype=bf16, name="d_act")
    dff = swiglu_bwd(ff, dact, name="swiglu_bwd")
    d_w_ffn_in = mm(hn2, dff, "tn", out_shards=N_CHIPS, name="d_w_ffn_in")
    dh1, dh1_b, d_norm2 = mm(dff, g_ffn_in, "nt", b_shards=True, name="d_hn2",
                             epi=(epi_rms_bwd, [h1, dh2], [norm2_w], [("row", f32), ("row", bf16), ("vec",)]))
    d_w_out = mm(mix, dh1_b, "tn", name="d_w_out")
    dmix = mm(dh1_b, w_out_f, "nt", out_dtype=bf16, name="d_mix")
    dproj, dy_ssm, dy_lru, d_bg = merge_bwd(proj, b_branch_gate, y_ssm, y_lru, dmix, name="merge_bwd")
    d_w_out_ssm = mm(yn, dy_ssm, "tn", name="d_w_out_ssm")
    d_w_out_lru = mm(o_lru, dy_lru, "tn", name="d_w_out_lru")
    early_g = [d_w_out_ssm.reshape(N_CHIPS, 512, D_MODEL), d_w_out_lru.reshape(N_CHIPS, 320, D_MODEL),
               d_w_out.reshape(N_CHIPS, 256, D_MODEL), d_w_ffn_in, d_w_ffn_out.reshape(N_CHIPS, 704, D_MODEL)]
    p_lands = [lax.empty((N_CHIPS, g.shape[1] // 2, g.shape[2]), f32) for g in early_g]
    p_send, p_recv, p_arrays, p_token = split_start(early_g + p_lands, early_g[0], pair_copies(5), (5,),
                                                    name="pair_early_start")
    dyn = mm(dy_ssm, w_out_ssm_f, "nt", out_dtype=bf16, after=p_token, name="d_yn")
    dy_ssd, dproj, d_ssm_norm = gnorm_bwd(y_ssd, proj, ssm_norm_w, dyn, dproj, name="gnorm_bwd")
    p_arrays = split_wait(p_send, p_recv, p_arrays, dy_ssd, pair_copies(5), name="pair_early_wait")
    e_pairs = [pair_add(g, rb, idx, name="pair_add_" + k) for g, rb, k in zip(p_arrays[:5], p_arrays[5:], late_names)]
    e_lands = [lax.empty((3,) + p[0].shape[1:], bf16) for p in e_pairs]
    e_send, e_recv, e_arrays, e_token = split_start([p[0] for p in e_pairs] + e_lands, e_pairs[0][1], reduce_copies(5),
                                                    (15,), name="reduce_early_start")
    dxbc_post, ddtr, dpar = ssd_bwd(xbc_post, dtr, dtr_t, par_row + e_token[0:1, 0:1], par_col, s_in, dy_ssd,
                                    name="ssd_bwd")
    dproj, d_ssm_cw_p, d_ssm_cb_p = conv_bwd(dxbc_post, xbc_pre, proj, OFF_XBC, ssm_cw_p, dproj, name="ssm_conv_bwd")
    do_lru = mm(dy_lru, w_out_lru_f, "nt", name="d_o_lru")
    du_lru, dproj, d_w_r, d_w_i, d_b_r, d_b_i, d_lam = lru_bwd(u_lru, proj, h_lru, do_lru, lru_w_r[0], lru_b_r, lru_w_i[0],
                                                               lru_b_i, lru_lambda, dproj, name="lru_bwd")
    dproj, d_lru_cw, d_lru_cb = conv_bwd(du_lru, None, proj, OFF_LX, lru_cw_full, dproj, name="lru_conv_bwd")
    ddt_cols = jnp.transpose(ddtr, (1, 0, 2)).reshape(t, SSM_HEADS).astype(bf16)
    ddt_cols = jnp.pad(ddt_cols, ((0, 0), (0, DT_PAD_W - SSM_HEADS)))
    dproj = lax.dynamic_update_slice(dproj, ddt_cols, (0, OFF_DT))

    d_ssm_cw = _unperm_xbc_cols(d_ssm_cw_p)
    d_ssm_cb = _unperm_xbc_cols(d_ssm_cb_p)
    dpar_h = jnp.transpose(dpar[:, 0:3, :], (1, 0, 2)).reshape(3, SSM_HEADS)
    small_names = ["norm1_w", "b_branch_gate", "ssm_conv_b", "ssm_dt_bias", "ssm_a_log", "ssm_d", "ssm_norm_w",
                   "lru_conv_b", "lru_w_r", "lru_b_r", "lru_w_i", "lru_b_i", "lru_lambda", "norm2_w", "norm_f_w"]
    small_g = dict(norm1_w=jnp.zeros_like(norm1_w), b_branch_gate=d_bg, ssm_conv_b=d_ssm_cb, ssm_dt_bias=dpar_h[0:1], ssm_a_log=dpar_h[1:2],
                   ssm_d=dpar_h[2:3], ssm_norm_w=d_ssm_norm, lru_conv_b=d_lru_cb, lru_w_r=d_w_r[None], lru_b_r=d_b_r,
                   lru_w_i=d_w_i[None], lru_b_i=d_b_i, lru_lambda=d_lam, norm2_w=d_norm2, norm_f_w=d_norm_f.reshape(D_MODEL))
    small_w = dict(norm1_w=norm1_w, b_branch_gate=b_branch_gate, ssm_conv_b=ssm_conv_b, ssm_dt_bias=ssm_dt_bias,
                   ssm_a_log=ssm_a_log, ssm_d=ssm_d, ssm_norm_w=ssm_norm_w, lru_conv_b=lru_conv_b, lru_w_r=lru_w_r,
                   lru_b_r=lru_b_r, lru_w_i=lru_w_i, lru_b_i=lru_b_i, lru_lambda=lru_lambda, norm2_w=norm2_w, norm_f_w=norm_f_w)
    small_m = dict(norm1_w=m_norm1_w, b_branch_gate=m_b_branch_gate, ssm_conv_b=m_ssm_conv_b, ssm_dt_bias=m_ssm_dt_bias,
                   ssm_a_log=m_ssm_a_log, ssm_d=m_ssm_d, ssm_norm_w=m_ssm_norm_w, lru_conv_b=m_lru_conv_b, lru_w_r=m_lru_w_r,
                   lru_b_r=m_lru_b_r, lru_w_i=m_lru_w_i, lru_b_i=m_lru_b_i, lru_lambda=m_lru_lambda, norm2_w=m_norm2_w,
                   norm_f_w=m_norm_f_w)
    small_v = dict(norm1_w=v_norm1_w, b_branch_gate=v_b_branch_gate, ssm_conv_b=v_ssm_conv_b, ssm_dt_bias=v_ssm_dt_bias,
                   ssm_a_log=v_ssm_a_log, ssm_d=v_ssm_d, ssm_norm_w=v_ssm_norm_w, lru_conv_b=v_lru_conv_b, lru_w_r=v_lru_w_r,
                   lru_b_r=v_lru_b_r, lru_w_i=v_lru_w_i, lru_b_i=v_lru_b_i, lru_lambda=v_lru_lambda, norm2_w=v_norm2_w,
                   norm_f_w=v_norm_f_w)
    shapes = [small_w[k].shape for k in small_names]
    conv_shapes = [(4, SSM_CONV_DIM), (4, LRU_WIDTH)]
    g_pack = _pack([small_g[k] for k in small_names] + [d_ssm_cw, d_lru_cw])
    s_send, s_recv, s_arrays, s_token = split_start([g_pack, lax.empty((8,) + g_pack.shape, f32)], g_pack, all_copies(),
                                                    (7,), name="small_start")
    d_w_in_p = mm(hn1, dproj, "tn", after=s_token, name="d_w_in")

    d_w_in_s = _unperm_in_to_shards(d_w_in_p)
    (l_sib,) = pair_exchange([d_w_in_s], name="pair_exchange_late")
    l_pair = pair_add(d_w_in_s, l_sib, idx, name="pair_add_w_in")
    l_land = lax.empty((3,) + l_pair[0].shape[1:], bf16)
    l_send, l_recv, l_arrays, l_token = split_start([l_pair[0], l_land], l_pair[1], reduce_copies(1), (3,),
                                                    name="reduce_late_start")
    grad_x, d_norm1 = mm(dproj, w_in_p, "nt", after=l_token, name="d_hn1",
                         epi=(epi_rms_bwd, [x2, dh1], [norm1_w], [("row", f32), ("vec",)]))

    e_arrays = split_wait(e_send, e_recv, e_arrays, d_norm1, reduce_copies(5), name="reduce_early_wait")
    e_half = [chip_sum(p[1], rb, idx, name="chip_sum_" + k) for p, rb, k in zip(e_pairs, e_arrays[5:], late_names)]
    big_out = {}
    for k, g in zip(late_names, pair_gather(e_half, name="pair_gather_early")):
        big_out[k] = (g,) + tuple(adamw(big_w[k], g, big_m[k], big_v[k], name="adamw_" + k))

    s_arrays = split_wait(s_send, s_recv, s_arrays, d_norm1, all_copies(), name="small_wait")
    g_sum = sum8(lax.dynamic_update_index_in_dim(s_arrays[1], g_pack, 2 * me + ci, 0), name="sum8")
    n1 = jnp.concatenate([d_norm1.reshape(8, 128), loss_tile], axis=0)
    n1_sum = sum8(lax.dynamic_update_index_in_dim(all_exchange(n1, name="all_exchange_norm1"), n1, 2 * me + ci, 0),
                  name="sum8_norm1")
    loss = n1_sum[8, 0]
    g_sum = lax.dynamic_update_slice(g_sum, n1_sum[0:8], (0, 0))
    g_small = _unpack(g_sum, shapes + conv_shapes)
    g_small[-2] = lax.dynamic_slice_in_dim(g_small[-2], me * 768, 768, axis=1)
    g_small[-1] = lax.dynamic_slice_in_dim(g_small[-1], me * 320, 320, axis=1)
    all_names = small_names + ["ssm_conv_w", "lru_conv_w"]
    small_w.update(ssm_conv_w=ssm_conv_w[0], lru_conv_w=lru_conv_w[0])
    small_m.update(ssm_conv_w=m_ssm_conv_w[0], lru_conv_w=m_lru_conv_w[0])
    small_v.update(ssm_conv_w=v_ssm_conv_w[0], lru_conv_w=v_lru_conv_w[0])
    as2d = lambda a: a.reshape(-1, a.shape[-1])
    upd = adamw_many([as2d(small_w[k]) for k in all_names], [as2d(g) for g in g_small],
                     [as2d(small_m[k]) for k in all_names], [as2d(small_v[k]) for k in all_names], name="adamw_small")
    small_out = {}
    for k, g, u in zip(all_names, g_small, upd):
        small_out[k] = (g,) + tuple(o.reshape(g.shape) for o in u)
    l_arrays = split_wait(l_send, l_recv, l_arrays, upd[0][0], reduce_copies(1), name="reduce_late_wait")
    l_half = chip_sum(l_pair[1], l_arrays[1], idx, name="chip_sum_w_in")
    (g_w_in,) = pair_gather([l_half], name="pair_gather_late")
    big_out["w_in"] = (g_w_in,) + tuple(adamw(big_w["w_in"], g_w_in, big_m["w_in"], big_v["w_in"], name="adamw_w_in"))

    order = ["norm1_w", "w_in", "b_branch_gate", "ssm_conv_w", "ssm_conv_b", "ssm_dt_bias", "ssm_a_log", "ssm_d", "ssm_norm_w",
             "w_out_ssm", "lru_conv_w", "lru_conv_b", "lru_w_r", "lru_b_r", "lru_w_i", "lru_b_i", "lru_lambda", "w_out_lru",
             "w_out", "norm2_w", "w_ffn_in", "w_ffn_out", "norm_f_w"]
    outs = [loss, grad_x[None]]
    for which in range(4):
        for k in order:
            if k in big_out:
                outs.append(big_out[k][which][None])
            elif k in ("ssm_conv_w", "lru_conv_w"):
                outs.append(small_out[k][which][None])
            else:
                outs.append(small_out[k][which])
    return tuple(outs)
```

```python
import functools
import math

import jax
import jax.numpy as jnp
import numpy as np
from jax import lax
from jax.experimental import pallas as pl
from jax.experimental.pallas import tpu as pltpu

f32 = jnp.float32
bf16 = jnp.bfloat16

D_MODEL = 1024
SSM_D_INNER = 2048
SSM_HEADS = 32
SSM_HEAD_DIM = 64
SSM_GROUPS = 4
SSM_HPG = 8
SSM_D_STATE = 128
SSM_CHUNK = 128
SSM_GROUP_W = 512
SSM_CONV_DIM = 3072
XBC_GROUP_W = 768
LRU_WIDTH = 1280
LRU_BLOCKS = 10
LRU_BLOCK = 128
LRU_C = 8.0
FFN_HIDDEN = 2816
RMS_EPS = 1e-6
IN_PROJ_DIM = 9760
N_CHIPS = 4

OFF_GATES = 0
OFF_Z = 2048
OFF_LX = 4096
OFF_LY = 5376
OFF_DT = 6656
DT_PAD_W = 256
OFF_XBC = 6912
PROJ_W = 9984

ADAM_LR = 0.001
ADAM_B1 = 0.9
ADAM_B2 = 0.999
ADAM_EPS = 1e-08
ADAM_WD = 0.01
ADAM_STEP = 10

MESH = pl.DeviceIdType.MESH
ANY = pl.BlockSpec(memory_space=pl.ANY)

NN = (((1,), (0,)), ((), ()))
NT = (((1,), (1,)), ((), ()))
TN = (((0,), (0,)), ((), ()))


def _pick(n, cap, mult=128):
    best = None
    for t in range(mult, min(n, cap) + 1, mult):
        if n % t == 0:
            best = t
    return best if best is not None else n


def _sigmoid(x):
    return 0.5 * jnp.tanh(0.5 * x) + 0.5


def _softplus(x):
    return jnp.maximum(x, 0.0) + jnp.log(1.0 + jnp.exp(-jnp.abs(x)))


def _silu(x):
    return x * _sigmoid(x)


def _dsilu(x):
    s = _sigmoid(x)
    return s * (1.0 + x * (1.0 - s))


_GELU_K = math.sqrt(2.0 / math.pi)


def _gelu(x):
    return 0.5 * x * (1.0 + jnp.tanh(_GELU_K * (x + 0.044715 * x * x * x)))


def _dgelu(x):
    t = jnp.tanh(_GELU_K * (x + 0.044715 * x * x * x))
    return 0.5 * (1.0 + t) + 0.5 * x * (1.0 - t * t) * _GELU_K * (1.0 + 3.0 * 0.044715 * x * x)


def _expm1(x):
    poly = x * (1.0 + x * (0.5 + x * (1.0 / 6.0 + x * (1.0 / 24.0 + x * (1.0 / 120.0 + x * (1.0 / 720.0))))))
    return jnp.where(jnp.abs(x) < 0.1, poly, jnp.exp(x) - 1.0)


def _dot(a, b, dn):
    return lax.dot_general(a.astype(bf16), b.astype(bf16), dn, preferred_element_type=f32)


def _dot_01(a, b, dn, split, terms):
    r = a if split == 0 else b
    out = None
    for _ in range(terms):
        h = r.astype(bf16)
        r = r - h.astype(f32)
        d = lax.dot_general(h if split == 0 else a.astype(bf16), b.astype(bf16) if split == 0 else h, dn,
                            preferred_element_type=f32)
        out = d if out is None else out + d
    return out


MM_VMEM_BUDGET = 48 * 2 ** 20

def mm(a, b, mode, *, name, add=None, after=None, out_dtype=f32, b_shards=False, out_shards=0, epi=None):
    bs = b.shape[1:] if b_shards else b.shape
    shard_w = b.shape[2] if b_shards else None
    bcols = bs[1] * (b.shape[0] if b_shards else 1)
    if mode == "nn":
        (m, k), (k2, n) = a.shape, (bs[0], bcols)
    elif mode == "nt":
        (m, k), (n, k2) = a.shape, (bs[0], bcols)
    else:
        (k, m), (k2, n) = a.shape, b.shape
    assert k == k2, (a.shape, b.shape, mode)
    tn = _pick(n, 1536)
    if b_shards and mode == "nn":
        tn = shard_w
    if out_shards:
        tn = n // out_shards
    isz = lambda v: jnp.dtype(v.dtype).itemsize
    if epi is not None:
        assert n <= 1536 and not out_shards
        tn = n
        epi_fn, epi_rows, epi_vecs, epi_outs = epi
        tile_bytes = sum(isz(v) for v in epi_rows) + sum(jnp.dtype(o[1]).itemsize for o in epi_outs if o[0] == "row")
    else:
        epi_rows, epi_vecs, epi_outs = [], [], []
        tile_bytes = jnp.dtype(out_dtype).itemsize
    tks = [shard_w] if (b_shards and mode == "nt") else sorted({k, _pick(k, 3328), _pick(k, 2048), _pick(k, 1024)}, reverse=True)

    def vmem_of(tm, tk):
        blocks = tm * tk * isz(a) + tk * tn * isz(b) + tm * tn * (4 * int(add is not None) + tile_bytes)
        return 2 * blocks + 4 * tm * tn * int(k > tk)

    fits = [(tk, tm) for tk in tks for tm in (_pick(m, 1536), _pick(m, 1024), _pick(m, 512)) if vmem_of(tm, tk) <= MM_VMEM_BUDGET]
    tk, tm = fits[0] if fits else (tks[-1], _pick(m, 256))
    nk = k // tk
    dn = {"nn": NN, "nt": NT, "tn": TN}[mode]
    a_spec = pl.BlockSpec((tk, tm), lambda i, j, kk: (kk, i)) if mode == "tn" else pl.BlockSpec((tm, tk), lambda i, j, kk: (i, kk))
    b_spec = pl.BlockSpec((tn, tk), lambda i, j, kk: (j, kk)) if mode == "nt" else pl.BlockSpec((tk, tn), lambda i, j, kk: (kk, j))
    if b_shards:
        b_spec = (pl.BlockSpec((None, tn, tk), lambda i, j, kk: (kk, j, 0)) if mode == "nt"
                  else pl.BlockSpec((None, tk, tn), lambda i, j, kk: (j, kk, 0)))
    o_spec = pl.BlockSpec((tm, tn), lambda i, j, kk: (i, j))
    out_shape = jax.ShapeDtypeStruct((m, n), out_dtype)
    if out_shards:
        assert add is None
        o_spec = pl.BlockSpec((None, tm, tn), lambda i, j, kk: (j, i, 0))
        out_shape = jax.ShapeDtypeStruct((out_shards, m, tn), out_dtype)
    has_add = add is not None

    n_extra = int(has_add) + int(after is not None)
    n_rows, n_vecs, n_outs = len(epi_rows), len(epi_vecs), len(epi_outs)

    def body(a_ref, b_ref, *rest):
        add_ref = rest[0] if has_add else None
        o_ref = rest[n_extra]

        def finish(r):
            if has_add:
                r = r + add_ref[...]
            if epi is None:
                o_ref[...] = r.astype(out_dtype)
            else:
                e = rest[n_extra:]
                epi_fn(r, e[:n_rows], e[n_rows:n_rows + n_vecs], e[n_rows + n_vecs:n_rows + n_vecs + n_outs],
                       pl.program_id(0) == 0)

        if nk == 1:
            finish(_dot(a_ref[...], b_ref[...], dn))
            return
        acc = rest[-1]
        kk = pl.program_id(2)

        @pl.when(kk == 0)
        def _():
            acc[...] = jnp.zeros_like(acc)

        acc[...] += _dot(a_ref[...], b_ref[...], dn)

        @pl.when(kk == nk - 1)
        def _():
            finish(acc[...])

    ins = [a, b] + ([add] if has_add else []) + ([after] if after is not None else [])
    in_specs = [a_spec, b_spec] + ([o_spec] if has_add else []) + ([ANY] if after is not None else [])
    sem0 = "parallel"
    if epi is not None:
        vec_spec = pl.BlockSpec((1, tn), lambda i, j, kk: (0, 0))
        ins += list(epi_rows) + list(epi_vecs)
        in_specs += [o_spec] * n_rows + [vec_spec] * n_vecs
        o_spec, out_shape = [], []
        for o in epi_outs:
            if o[0] == "row":
                o_spec.append(pl.BlockSpec((tm, tn), lambda i, j, kk: (i, j)))
                out_shape.append(jax.ShapeDtypeStruct((m, n), o[1]))
            elif o[0] == "vec":
                o_spec.append(vec_spec)
                out_shape.append(jax.ShapeDtypeStruct((1, n), f32))
                sem0 = "arbitrary"
            else:
                o_spec.append(pl.BlockSpec((8, 128), lambda i, j, kk: (0, 0)))
                out_shape.append(jax.ShapeDtypeStruct((8, 128), f32))
                sem0 = "arbitrary"
    return pl.pallas_call(
        body, name=name, grid=(m // tm, n // tn, nk), in_specs=in_specs, out_specs=o_spec, out_shape=out_shape,
        scratch_shapes=[pltpu.VMEM((tm, tn), f32)] if nk > 1 else [],
        compiler_params=pltpu.CompilerParams(dimension_semantics=(sem0, sem0, "arbitrary")),
    )(*ins)


def rms_fwd(x, w, *, name):
    t, d = x.shape
    tr = _pick(t, 512, 8)

    def body(x_ref, w_ref, o_ref):
        xv = x_ref[...]
        r = lax.rsqrt(jnp.mean(xv * xv, axis=-1, keepdims=True) + RMS_EPS)
        o_ref[...] = (xv * r * w_ref[...]).astype(bf16)

    return pl.pallas_call(
        body, name=name, grid=(t // tr,),
        in_specs=[pl.BlockSpec((tr, d), lambda i: (i, 0)), pl.BlockSpec((1, d), lambda i: (0, 0))],
        out_specs=pl.BlockSpec((tr, d), lambda i: (i, 0)), out_shape=jax.ShapeDtypeStruct((t, d), bf16),
    )(x, w)


def _rms_bwd_math(xv, wv, dy):
    r = lax.rsqrt(jnp.mean(xv * xv, axis=-1, keepdims=True) + RMS_EPS)
    g = dy * wv
    dx = r * g - xv * (r * r * r) * jnp.mean(g * xv, axis=-1, keepdims=True)
    dw = jnp.sum(dy * xv * r, axis=0, keepdims=True)
    return dx, dw


def epi_rms_fwd(r, rows, vecs, outs, first):
    outs[0][...] = r
    rr = lax.rsqrt(jnp.mean(r * r, axis=-1, keepdims=True) + RMS_EPS)
    outs[1][...] = (r * rr * vecs[0][...]).astype(bf16)


def epi_rms_bwd(r, rows, vecs, outs, first):
    dx, dw = _rms_bwd_math(rows[0][...], vecs[0][...], r)
    dx = dx + rows[1][...]
    outs[0][...] = dx
    if len(outs) == 3:
        outs[1][...] = dx.astype(bf16)
    dw_ref = outs[-1]

    @pl.when(first)
    def _():
        dw_ref[...] = jnp.zeros_like(dw_ref)

    dw_ref[...] += dw


def epi_loss(r, rows, vecs, outs, first):
    wv = vecs[0][...]
    rr = lax.rsqrt(jnp.mean(r * r, axis=-1, keepdims=True) + RMS_EPS)
    err = r * rr * wv - rows[0][...]
    part = 0.5 * jnp.sum(jnp.mean(err * err, axis=-1, keepdims=True), axis=0, keepdims=True)
    dx, dw = _rms_bwd_math(r, wv, err * (1.0 / r.shape[-1]))
    outs[0][...] = dx
    outs[1][...] = dx.astype(bf16)

    @pl.when(first)
    def _():
        outs[2][...] = jnp.zeros_like(outs[2])
        outs[3][...] = jnp.zeros_like(outs[3])

    outs[2][...] += dw
    outs[3][...] += part


CONV_ROWS = 1024
VREG_ELEMS = 8 * 128


def _conv_chunk(tc):
    return 16 if (16 + 8) * tc * 3 > 48 * VREG_ELEMS else 32


def conv_fwd(src, col0, width, w, b, *, silu, name):
    t = src.shape[0]
    tc = _pick(math.gcd(width, col0), 768)
    assert col0 % tc == 0
    cb = col0 // tc
    r = CONV_ROWS
    ch = _conv_chunk(tc)

    def body(u_ref, w_ref, b_ref, *rest):
        ext = rest[-1]
        j = pl.program_id(1)

        @pl.when(j == 0)
        def _():
            ext[0:8, :] = jnp.zeros((8, tc), f32)

        @pl.when(j > 0)
        def _():
            ext[0:8, :] = ext[r:r + 8, :]

        ext[8:r + 8, :] = u_ref[...]
        wv = w_ref[...]
        bv = b_ref[...]

        def chunk(c, carry):
            r0 = pl.multiple_of(c * ch, ch)
            v = ext[pl.ds(r0, ch + 8), :]
            acc = bv + wv[3:4, :] * v[8:, :]
            for s in (1, 2, 3):
                acc = acc + wv[3 - s:4 - s, :] * pltpu.roll(v, s, 0)[8:, :]
            rest[0][pl.ds(r0, ch), :] = acc
            if silu:
                rest[1][pl.ds(r0, ch), :] = _silu(acc)
            return carry

        lax.fori_loop(0, r // ch, chunk, 0)

    tile = pl.BlockSpec((r, tc), lambda c, j: (j, c))
    n_out = 2 if silu else 1
    return pl.pallas_call(
        body, name=name, grid=(width // tc, t // r),
        in_specs=[pl.BlockSpec((r, tc), lambda c, j: (j, cb + c)), pl.BlockSpec((4, tc), lambda c, j: (0, c)),
                  pl.BlockSpec((1, tc), lambda c, j: (0, c))],
        out_specs=[tile] * n_out, out_shape=[jax.ShapeDtypeStruct((t, width), f32)] * n_out,
        scratch_shapes=[pltpu.VMEM((r + 8, tc), f32)],
        compiler_params=pltpu.CompilerParams(dimension_semantics=("parallel", "arbitrary")),
    )(src, w, b)


def conv_bwd(dpost, pre, src, col0, w, dst, *, name):
    t, width = dpost.shape
    tc = _pick(math.gcd(width, col0), 768)
    assert col0 % tc == 0
    cb = col0 // tc
    r = CONV_ROWS
    ch = _conv_chunk(tc)
    nt = t // r
    has_pre = pre is not None

    def body(*refs):
        refs = refs[1:]
        if has_pre:
            d_ref, p_ref, u_ref, w_ref, du_ref, dw_ref, db_ref, ext = refs
        else:
            d_ref, u_ref, w_ref, du_ref, dw_ref, db_ref, ext = refs
        j = pl.program_id(1)

        @pl.when(j == 0)
        def _():
            ext[r:r + 8, :] = jnp.zeros((8, tc), f32)
            dw_ref[...] = jnp.zeros_like(dw_ref)
            db_ref[...] = jnp.zeros_like(db_ref)

        @pl.when(j > 0)
        def _():
            ext[r:r + 8, :] = ext[0:8, :]

        dpre = d_ref[...]
        if has_pre:
            dpre = dpre * _dsilu(p_ref[...])
        ext[0:r, :] = dpre
        wv = w_ref[...]

        def fold(p):
            out = p[0:8, :]
            for i in range(1, ch // 8):
                out = out + p[8 * i:8 * i + 8, :]
            return out

        def chunk(c, sums):
            r0 = pl.multiple_of(c * ch, ch)
            v = ext[pl.ds(r0, ch + 8), :]
            uv = u_ref[pl.ds(r0, ch), :]
            d0 = v[0:ch, :]
            du = wv[3:4, :] * d0
            new = [None] * 5
            new[3] = sums[3] + fold(d0 * uv)
            for s in (1, 2, 3):
                sh = pltpu.roll(v, ch + 8 - s, 0)[0:ch, :]
                du = du + wv[3 - s:4 - s, :] * sh
                new[3 - s] = sums[3 - s] + fold(sh * uv)
            new[4] = sums[4] + fold(d0)
            du_ref[pl.ds(r0, ch), :] = du.astype(bf16)
            return tuple(new)

        sums = lax.fori_loop(0, r // ch, chunk, tuple(jnp.zeros((8, tc), f32) for _ in range(5)))
        for k in range(4):
            dw_ref[k:k + 1, :] += jnp.sum(sums[k], axis=0, keepdims=True)
        db_ref[...] += jnp.sum(sums[4], axis=0, keepdims=True)

    rev = pl.BlockSpec((r, tc), lambda c, j: (nt - 1 - j, c))
    win = pl.BlockSpec((r, tc), lambda c, j: (nt - 1 - j, cb + c))
    in_specs = [ANY, rev] + ([rev] if has_pre else []) + [win, pl.BlockSpec((4, tc), lambda c, j: (0, c))]
    ins = [dst, dpost] + ([pre] if has_pre else []) + [src, w]
    return pl.pallas_call(
        body, name=name, grid=(width // tc, nt), in_specs=in_specs,
        out_specs=[win, pl.BlockSpec((4, tc), lambda c, j: (0, c)), pl.BlockSpec((1, tc), lambda c, j: (0, c))],
        out_shape=[jax.ShapeDtypeStruct(dst.shape, bf16), jax.ShapeDtypeStruct((4, width), f32),
                   jax.ShapeDtypeStruct((1, width), f32)],
        input_output_aliases={0: 0},
        scratch_shapes=[pltpu.VMEM((r + 8, tc), f32)],
        compiler_params=pltpu.CompilerParams(dimension_semantics=("parallel", "arbitrary")),
    )(*ins)


def _ssd_common(xbc_ref, dtr_ref, dtrT_ref, par_row_ref, par_col_ref):
    l = SSM_CHUNK
    x = xbc_ref[:, 0:SSM_GROUP_W]
    bm = xbc_ref[:, SSM_GROUP_W:SSM_GROUP_W + SSM_D_STATE]
    cm = xbc_ref[:, SSM_GROUP_W + SSM_D_STATE:XBC_GROUP_W]
    par_row = par_row_ref[0]
    par_col = par_col_ref[0]
    bias_row, alog_row, d_row = par_row[0:1, :], par_row[1:2, :], par_row[2:3, :]
    bias_col, alog_col = par_col[:, 0:1], par_col[:, 1:2]
    dtr = dtr_ref[0]
    dt = _softplus(dtr + bias_row)
    dt_t = _softplus(dtrT_ref[0] + bias_col)
    a_row = -jnp.exp(alog_row)
    a_col = -jnp.exp(alog_col)
    li = lax.broadcasted_iota(jnp.int32, (l, l), 0)
    si = lax.broadcasted_iota(jnp.int32, (l, l), 1)
    tri = (li >= si).astype(f32)
    cs = _dot_01(tri, dt * a_row, NN, 1, 3)
    cs_t = _dot_01(dt_t * a_col, tri, NT, 0, 3)
    off = lax.broadcasted_iota(jnp.int32, (SSM_HPG, SSM_GROUP_W), 1) - SSM_HEAD_DIM * lax.broadcasted_iota(
        jnp.int32, (SSM_HPG, SSM_GROUP_W), 0)
    ex = ((off >= 0) & (off < SSM_HEAD_DIM)).astype(f32)
    cs_x = _dot_01(cs, ex, NN, 0, 3)
    cl_x = cs_x[l - 1:l, :]
    return dict(x=x, bm=bm, cm=cm, dtr=dtr, dt=dt, a_row=a_row, bias_row=bias_row, tri=tri, li=li, si=si, cs=cs,
                cs_t=cs_t, ex=ex, dt_x=_dot_01(dt, ex, NN, 0, 2), d_x=_dot_01(par_row, ex, NN, 0, 2)[2:3, :], e_x=jnp.exp(cs_x),
                el_x=jnp.exp(cl_x), dec_x=jnp.exp(cl_x - cs_x))


def ssd_fwd(xbc, dtr, dtr_t, par_row, par_col, *, name):
    t = xbc.shape[0]
    nc = t // SSM_CHUNK
    l, p = SSM_CHUNK, SSM_HEAD_DIM

    def body(xbc_ref, dtr_ref, dtrT_ref, prow_ref, pcol_ref, y_ref, sin_ref, state):
        @pl.when(pl.program_id(1) == 0)
        def _():
            state[...] = jnp.zeros_like(state)

        q = _ssd_common(xbc_ref, dtr_ref, dtrT_ref, prow_ref, pcol_ref)
        st = state[...]
        sin_ref[0] = st
        xd = q["x"] * q["dt_x"]
        g = _dot(q["cm"], q["bm"], NT)
        for r in range(SSM_HPG):
            sl = slice(r * p, (r + 1) * p)
            diff = q["cs"][:, r:r + 1] - q["cs_t"][r:r + 1, :]
            lm = jnp.where(q["li"] >= q["si"], jnp.exp(jnp.minimum(diff, 0.0)), 0.0)
            y_ref[:, sl] = _dot(g * lm, xd[:, sl], NN)
        y_ref[...] += q["e_x"] * _dot(q["cm"], st, NN) + q["d_x"] * q["x"]
        state[...] = q["el_x"] * st + _dot(q["bm"].T, xd * q["dec_x"], NN)

    return pl.pallas_call(
        body, name=name, grid=(SSM_GROUPS, nc),
        in_specs=[pl.BlockSpec((l, XBC_GROUP_W), lambda g, c: (c, g)),
                  pl.BlockSpec((1, l, SSM_HPG), lambda g, c: (g, c, 0)),
                  pl.BlockSpec((1, SSM_HPG, l), lambda g, c: (g, 0, c)),
                  pl.BlockSpec((1, 8, 8), lambda g, c: (g, 0, 0)),
                  pl.BlockSpec((1, 8, 8), lambda g, c: (g, 0, 0))],
        out_specs=[pl.BlockSpec((l, SSM_GROUP_W), lambda g, c: (c, g)),
                   pl.BlockSpec((1, SSM_D_STATE, SSM_GROUP_W), lambda g, c: (c, 0, g))],
        out_shape=[jax.ShapeDtypeStruct((t, SSM_D_INNER), f32),
                   jax.ShapeDtypeStruct((nc, SSM_D_STATE, SSM_D_INNER), f32)],
        scratch_shapes=[pltpu.VMEM((SSM_D_STATE, SSM_GROUP_W), f32)],
        compiler_params=pltpu.CompilerParams(dimension_semantics=("parallel", "arbitrary")),
    )(xbc, dtr, dtr_t, par_row, par_col)


def ssd_bwd(xbc, dtr, dtr_t, par_row, par_col, s_in, dy, *, name):
    t = xbc.shape[0]
    nc = t // SSM_CHUNK
    l, p = SSM_CHUNK, SSM_HEAD_DIM

    def body(xbc_ref, dtr_ref, dtrT_ref, prow_ref, pcol_ref, sin_ref, dy_ref, dxbc_ref, ddtr_ref, dpar_ref,
             dstate, yd_buf, dxd_buf):
        @pl.when(pl.program_id(1) == 0)
        def _():
            dstate[...] = jnp.zeros_like(dstate)
            dpar_ref[...] = jnp.zeros_like(dpar_ref)

        q = _ssd_common(xbc_ref, dtr_ref, dtrT_ref, prow_ref, pcol_ref)
        x, bm, cm, ex, li, si = q["x"], q["bm"], q["cm"], q["ex"], q["li"], q["si"]
        e_x, el_x, dec_x = q["e_x"], q["el_x"], q["dec_x"]
        st = sin_ref[0]
        dst = dstate[...]
        dy = dy_ref[...]
        xd = x * q["dt_x"]
        g = _dot(cm, bm, NT)
        dg = jnp.zeros((l, l), f32)
        for r in range(SSM_HPG):
            sl = slice(r * p, (r + 1) * p)
            diff = q["cs"][:, r:r + 1] - q["cs_t"][r:r + 1, :]
            lm = jnp.where(li >= si, jnp.exp(jnp.minimum(diff, 0.0)), 0.0)
            m = (g * lm).astype(bf16)
            xdh, dyh = xd[:, sl].astype(bf16), dy[:, sl].astype(bf16)
            yd_buf[:, sl] = _dot(m, xdh, NN)
            dxd_buf[:, sl] = _dot(m, dyh, TN)
            dg = dg + _dot(dyh, xdh, NT) * lm
        yd, dxd_diag = yd_buf[...], dxd_buf[...]
        yo = e_x * _dot(cm, st, NN)
        dz = e_x * dy
        wv = _dot(bm, dst, NN)
        xw = xd * wv * dec_x
        row8 = lax.broadcasted_iota(jnp.int32, (l, SSM_HPG), 0)
        dy_b, xd_b = dy.astype(bf16).astype(f32), xd.astype(bf16).astype(f32)
        dcs = _dot_01(dy_b * yd - xd_b * dxd_diag + dy * yo - xw, ex, NT, 0, 3)
        tail = jnp.sum(xw, axis=0, keepdims=True) + el_x * jnp.sum(dst * st, axis=0, keepdims=True)
        dcl = _dot_01(jnp.broadcast_to(tail, (SSM_HPG, SSM_GROUP_W)), ex, NT, 0, 3)[0:1, :]
        dcs = dcs + jnp.where(row8 == l - 1, dcl, 0.0)
        dda = _dot_01(q["tri"], dcs, TN, 1, 3)
        dxd = dxd_diag + dec_x * wv
        ddt = _dot_01(dxd * x, ex, NT, 0, 3) + dda * q["a_row"]
        ddtr = ddt * _sigmoid(q["dtr"] + q["bias_row"])
        ddtr_ref[0] = ddtr
        dd = _dot_01(jnp.broadcast_to(jnp.sum(dy * x, axis=0, keepdims=True), (SSM_HPG, SSM_GROUP_W)), ex, NT, 0, 2)[0:1, :]
        dpar_ref[0, 0:1, :] += jnp.sum(ddtr, axis=0, keepdims=True)
        dpar_ref[0, 1:2, :] += jnp.sum(dda * q["dt"], axis=0, keepdims=True) * q["a_row"]
        dpar_ref[0, 2:3, :] += dd
        dxbc_ref[:, 0:SSM_GROUP_W] = dxd * q["dt_x"] + q["d_x"] * dy
        dxbc_ref[:, SSM_GROUP_W:SSM_GROUP_W + SSM_D_STATE] = _dot(dg, cm, TN) + _dot(xd * dec_x, dst, NT)
        dxbc_ref[:, SSM_GROUP_W + SSM_D_STATE:XBC_GROUP_W] = _dot(dg, bm, NN) + _dot(dz, st, NT)
        dstate[...] = _dot(cm.T, dz, NN) + el_x * dst

    rc = lambda c: nc - 1 - c
    return pl.pallas_call(
        body, name=name, grid=(SSM_GROUPS, nc),
        in_specs=[pl.BlockSpec((l, XBC_GROUP_W), lambda g, c: (rc(c), g)),
                  pl.BlockSpec((1, l, SSM_HPG), lambda g, c: (g, rc(c), 0)),
                  pl.BlockSpec((1, SSM_HPG, l), lambda g, c: (g, 0, rc(c))),
                  pl.BlockSpec((1, 8, 8), lambda g, c: (g, 0, 0)),
                  pl.BlockSpec((1, 8, 8), lambda g, c: (g, 0, 0)),
                  pl.BlockSpec((1, SSM_D_STATE, SSM_GROUP_W), lambda g, c: (rc(c), 0, g)),
                  pl.BlockSpec((l, SSM_GROUP_W), lambda g, c: (rc(c), g))],
        out_specs=[pl.BlockSpec((l, XBC_GROUP_W), lambda g, c: (rc(c), g)),
                   pl.BlockSpec((1, l, SSM_HPG), lambda g, c: (g, rc(c), 0)),
                   pl.BlockSpec((1, 8, 8), lambda g, c: (g, 0, 0))],
        out_shape=[jax.ShapeDtypeStruct((t, SSM_CONV_DIM), f32),
                   jax.ShapeDtypeStruct((SSM_GROUPS, t, SSM_HPG), f32),
                   jax.ShapeDtypeStruct((SSM_GROUPS, 8, 8), f32)],
        scratch_shapes=[pltpu.VMEM((SSM_D_STATE, SSM_GROUP_W), f32), pltpu.VMEM((l, SSM_GROUP_W), f32),
                        pltpu.VMEM((l, SSM_GROUP_W), f32)],
        compiler_params=pltpu.CompilerParams(dimension_semantics=("parallel", "arbitrary")),
    )(xbc, dtr, dtr_t, par_row, par_col, s_in, dy)


def gnorm_fwd(y, proj, w, *, name):
    t = y.shape[0]
    tr = _pick(t, 1024, 8)
    gw = SSM_GROUP_W
    zb = OFF_Z // gw

    def body(y_ref, z_ref, w_ref, o_ref):
        y2 = y_ref[...] * _silu(z_ref[...])
        r = lax.rsqrt(jnp.mean(y2 * y2, axis=-1, keepdims=True) + RMS_EPS)
        o_ref[...] = (y2 * r * w_ref[...]).astype(bf16)

    return pl.pallas_call(
        body, name=name, grid=(SSM_GROUPS, t // tr),
        in_specs=[pl.BlockSpec((tr, gw), lambda g, i: (i, g)), pl.BlockSpec((tr, gw), lambda g, i: (i, zb + g)),
                  pl.BlockSpec((1, gw), lambda g, i: (0, g))],
        out_specs=pl.BlockSpec((tr, gw), lambda g, i: (i, g)), out_shape=jax.ShapeDtypeStruct((t, SSM_D_INNER), bf16),
    )(y, proj, w)


def gnorm_bwd(y, proj, w, dout, dst, *, name):
    t = y.shape[0]
    tr = _pick(t, 1024, 8)
    gw = SSM_GROUP_W
    zb = OFF_Z // gw

    def body(_, y_ref, z_ref, w_ref, do_ref, dy_ref, dz_ref, dw_ref):
        yv, zv = y_ref[...], z_ref[...]
        sz = _silu(zv)
        y2 = yv * sz
        dy2, dw = _rms_bwd_math(y2, w_ref[...], do_ref[...].astype(f32))
        dy_ref[...] = dy2 * sz
        dz_ref[...] = (dy2 * yv * _dsilu(zv)).astype(bf16)

        @pl.when(pl.program_id(1) == 0)
        def _():
            dw_ref[...] = jnp.zeros_like(dw_ref)

        dw_ref[...] += dw

    tile = pl.BlockSpec((tr, gw), lambda g, i: (i, g))
    vec = pl.BlockSpec((1, gw), lambda g, i: (0, g))
    return pl.pallas_call(
        body, name=name, grid=(SSM_GROUPS, t // tr),
        in_specs=[ANY, tile, pl.BlockSpec((tr, gw), lambda g, i: (i, zb + g)), vec, tile],
        out_specs=[tile, pl.BlockSpec((tr, gw), lambda g, i: (i, zb + g)), vec],
        out_shape=[jax.ShapeDtypeStruct((t, SSM_D_INNER), f32), jax.ShapeDtypeStruct(dst.shape, bf16),
                   jax.ShapeDtypeStruct((1, SSM_D_INNER), f32)],
        input_output_aliases={0: 1},
        compiler_params=pltpu.CompilerParams(dimension_semantics=("parallel", "arbitrary")),
    )(dst, y, proj, w, dout)


LRU_ROWS = 1024


def _lru_gates(uv, wr_ref, wi_ref, br_ref, bi_ref, lam_ref):
    rg = _sigmoid(_dot(uv, wr_ref[0], NN) + br_ref[...])
    ig = _sigmoid(_dot(uv, wi_ref[0], NN) + bi_ref[...])
    sp = _softplus(-lam_ref[...])
    la = -LRU_C * rg * sp
    a = jnp.exp(la)
    s = jnp.sqrt(jnp.maximum(-_expm1(2.0 * la), 0.0))
    return rg, ig, sp, la, a, s


def lru_fwd(u, proj, w_r, b_r, w_i, b_i, lam, *, name):
    t = u.shape[0]
    r = LRU_ROWS
    lb = LRU_BLOCK
    yb = OFF_LY // lb

    def body(u_ref, y_ref, wr_ref, br_ref, wi_ref, bi_ref, lam_ref, h_ref, o_ref, carry):
        @pl.when(pl.program_id(1) == 0)
        def _():
            carry[...] = jnp.zeros_like(carry)

        uv = u_ref[...]
        _, ig, _, _, a, s = _lru_gates(uv, wr_ref, wi_ref, br_ref, bi_ref, lam_ref)
        b = s * ig * uv
        row = lax.broadcasted_iota(jnp.int32, (r, lb), 0)
        d = 1
        while d < r:
            keep = row >= d
            b = b + a * jnp.where(keep, pltpu.roll(b, d, 0), 0.0)
            a = a * jnp.where(keep, pltpu.roll(a, d, 0), 1.0)
            d *= 2
        h = b + a * carry[0:1, :]
        carry[0:1, :] = h[r - 1:r, :]
        h_ref[...] = h
        o_ref[...] = (h * _gelu(y_ref[...])).astype(bf16)

    tile = pl.BlockSpec((r, lb), lambda hb, j: (j, hb))
    vec = pl.BlockSpec((1, lb), lambda hb, j: (0, hb))
    wsp = pl.BlockSpec((1, lb, lb), lambda hb, j: (hb, 0, 0))
    return pl.pallas_call(
        body, name=name, grid=(LRU_BLOCKS, t // r),
        in_specs=[tile, pl.BlockSpec((r, lb), lambda hb, j: (j, yb + hb)), wsp, vec, wsp, vec, vec],
        out_specs=[tile, tile],
        out_shape=[jax.ShapeDtypeStruct((t, LRU_WIDTH), f32), jax.ShapeDtypeStruct((t, LRU_WIDTH), bf16)],
        scratch_shapes=[pltpu.VMEM((8, lb), f32)],
        compiler_params=pltpu.CompilerParams(dimension_semantics=("parallel", "arbitrary")),
    )(u, proj, w_r, b_r, w_i, b_i, lam)


def lru_bwd(u, proj, hseq, dout, w_r, b_r, w_i, b_i, lam, dst, *, name):
    t = u.shape[0]
    r = LRU_ROWS
    nt = t // r
    lb = LRU_BLOCK
    yb = OFF_LY // lb

    def body(_, u_ref, y_ref, h_ref, hp_ref, do_ref, wr_ref, br_ref, wi_ref, bi_ref, lam_ref,
             du_ref, dy_ref, dwr_ref, dwi_ref, dbr_ref, dbi_ref, dlam_ref, carry_dh, carry_a):
        j = pl.program_id(1)

        @pl.when(j == 0)
        def _():
            carry_dh[...] = jnp.zeros_like(carry_dh)
            carry_a[...] = jnp.zeros_like(carry_a)
            dwr_ref[...] = jnp.zeros_like(dwr_ref)
            dwi_ref[...] = jnp.zeros_like(dwi_ref)
            dbr_ref[...] = jnp.zeros_like(dbr_ref)
            dbi_ref[...] = jnp.zeros_like(dbi_ref)
            dlam_ref[...] = jnp.zeros_like(dlam_ref)

        uv = u_ref[...]
        yv = y_ref[...]
        hv = h_ref[...]
        dov = do_ref[...]
        rg, ig, sp, la, a, s = _lru_gates(uv, wr_ref, wi_ref, br_ref, bi_ref, lam_ref)
        dy_ref[...] = (dov * hv * _dgelu(yv)).astype(bf16)
        gq = dov * _gelu(yv)
        row = lax.broadcasted_iota(jnp.int32, (r, lb), 0)
        an = jnp.where(row < r - 1, pltpu.roll(a, r - 1, 0), carry_a[0:1, :])
        d = 1
        while d < r:
            keep = row < r - d
            gq = gq + an * jnp.where(keep, pltpu.roll(gq, r - d, 0), 0.0)
            an = an * jnp.where(keep, pltpu.roll(an, r - d, 0), 1.0)
            d *= 2
        dh = gq + an * carry_dh[0:1, :]
        carry_dh[0:1, :] = dh[0:1, :]
        carry_a[0:1, :] = a[0:1, :]
        first = jnp.where(j == nt - 1, 0.0, 1.0) * hp_ref[7:8, :]
        hprev = jnp.where(row >= 1, pltpu.roll(hv, 1, 0), first)
        da = dh * hprev
        iu = ig * uv
        e2 = jnp.exp(2.0 * la)
        dla = da * a - dh * iu * e2 / jnp.maximum(s, 1e-30)
        drp = dla * (-LRU_C * sp) * rg * (1.0 - rg)
        dip = dh * s * uv * ig * (1.0 - ig)
        dlam_ref[...] += jnp.sum(dla * (LRU_C * rg) * _sigmoid(-lam_ref[...]), axis=0, keepdims=True)
        du_ref[...] = dh * s * ig + _dot(drp, wr_ref[0], NT) + _dot(dip, wi_ref[0], NT)
        dwr_ref[0] += _dot(uv, drp, TN)
        dwi_ref[0] += _dot(uv, dip, TN)
        dbr_ref[...] += jnp.sum(drp, axis=0, keepdims=True)
        dbi_ref[...] += jnp.sum(dip, axis=0, keepdims=True)

    rj = lambda j: nt - 1 - j
    tile = pl.BlockSpec((r, lb), lambda hb, j: (rj(j), hb))
    vec = pl.BlockSpec((1, lb), lambda hb, j: (0, hb))
    wsp = pl.BlockSpec((1, lb, lb), lambda hb, j: (hb, 0, 0))
    hprev_spec = pl.BlockSpec((8, lb), lambda hb, j: (jnp.maximum(rj(j) * (r // 8) - 1, 0), hb))
    ywin = pl.BlockSpec((r, lb), lambda hb, j: (rj(j), yb + hb))
    return pl.pallas_call(
        body, name=name, grid=(LRU_BLOCKS, nt),
        in_specs=[ANY, tile, ywin, tile, hprev_spec, tile, wsp, vec, wsp, vec, vec],
        out_specs=[tile, ywin, wsp, wsp, vec, vec, vec],
        out_shape=[jax.ShapeDtypeStruct((t, LRU_WIDTH), f32), jax.ShapeDtypeStruct(dst.shape, bf16),
                   jax.ShapeDtypeStruct((LRU_BLOCKS, lb, lb), f32), jax.ShapeDtypeStruct((LRU_BLOCKS, lb, lb), f32),
                   jax.ShapeDtypeStruct((1, LRU_WIDTH), f32), jax.ShapeDtypeStruct((1, LRU_WIDTH), f32),
                   jax.ShapeDtypeStruct((1, LRU_WIDTH), f32)],
        input_output_aliases={0: 1},
        scratch_shapes=[pltpu.VMEM((8, lb), f32), pltpu.VMEM((8, lb), f32)],
        compiler_params=pltpu.CompilerParams(dimension_semantics=("parallel", "arbitrary")),
    )(dst, u, proj, hseq, hseq, dout, w_r, b_r, w_i, b_i, lam)


def merge_fwd(proj, bg, y_ssm, y_lru, *, name):
    t, d = y_ssm.shape
    tr = _pick(t, 512, 8)
    gb = OFF_GATES // d

    def body(gs_ref, gl_ref, bs_ref, bl_ref, ys_ref, yl_ref, o_ref):
        gs = _sigmoid(gs_ref[...] + bs_ref[...])
        gl = _sigmoid(gl_ref[...] + bl_ref[...])
        o_ref[...] = (gs * ys_ref[...].astype(f32) + gl * yl_ref[...].astype(f32)).astype(bf16)

    row = pl.BlockSpec((tr, d), lambda i: (i, 0))
    return pl.pallas_call(
        body, name=name, grid=(t // tr,),
        in_specs=[pl.BlockSpec((tr, d), lambda i: (i, gb)), pl.BlockSpec((tr, d), lambda i: (i, gb + 1)),
                  pl.BlockSpec((1, d), lambda i: (0, 0)), pl.BlockSpec((1, d), lambda i: (0, 1)), row, row],
        out_specs=row, out_shape=jax.ShapeDtypeStruct((t, d), bf16),
    )(proj, proj, bg, bg, y_ssm, y_lru)


def merge_bwd(proj, bg, y_ssm, y_lru, dmix, *, name):
    t, d = y_ssm.shape
    tr = _pick(t, 512, 8)
    gb = OFF_GATES // d

    def body(gs_ref, gl_ref, bs_ref, bl_ref, ys_ref, yl_ref, dm_ref, dg_ref, dys_ref, dyl_ref, dbg_ref):
        gs = _sigmoid(gs_ref[...] + bs_ref[...])
        gl = _sigmoid(gl_ref[...] + bl_ref[...])
        dm = dm_ref[...].astype(f32)
        dys_ref[...] = (dm * gs).astype(bf16)
        dyl_ref[...] = (dm * gl).astype(bf16)
        dgs = dm * ys_ref[...].astype(f32) * gs * (1.0 - gs)
        dgl = dm * yl_ref[...].astype(f32) * gl * (1.0 - gl)
        dg_ref[:, 0:d] = dgs.astype(bf16)
        dg_ref[:, d:2 * d] = dgl.astype(bf16)

        @pl.when(pl.program_id(0) == 0)
        def _():
            dbg_ref[...] = jnp.zeros_like(dbg_ref)

        dbg_ref[:, 0:d] += jnp.sum(dgs, axis=0, keepdims=True)
        dbg_ref[:, d:2 * d] += jnp.sum(dgl, axis=0, keepdims=True)

    row = pl.BlockSpec((tr, d), lambda i: (i, 0))
    return pl.pallas_call(
        body, name=name, grid=(t // tr,),
        in_specs=[pl.BlockSpec((tr, d), lambda i: (i, gb)), pl.BlockSpec((tr, d), lambda i: (i, gb + 1)),
                  pl.BlockSpec((1, d), lambda i: (0, 0)), pl.BlockSpec((1, d), lambda i: (0, 1)), row, row, row],
        out_specs=[pl.BlockSpec((tr, 2 * d), lambda i: (i, OFF_GATES // (2 * d))), row, row,
                   pl.BlockSpec((1, 2 * d), lambda i: (0, 0))],
        out_shape=[jax.ShapeDtypeStruct((t, PROJ_W), bf16), jax.ShapeDtypeStruct((t, d), bf16),
                   jax.ShapeDtypeStruct((t, d), bf16), jax.ShapeDtypeStruct((1, 2 * d), f32)],
        compiler_params=pltpu.CompilerParams(dimension_semantics=("arbitrary",)),
    )(proj, proj, bg, bg, y_ssm, y_lru, dmix)


def swiglu_fwd(ff, *, name):
    t = ff.shape[0]
    hd = FFN_HIDDEN
    tr = _pick(t, 512, 8)

    def body(f_ref, o_ref):
        o_ref[...] = (_silu(f_ref[:, 0:hd].astype(f32)) * f_ref[:, hd:2 * hd].astype(f32)).astype(bf16)

    return pl.pallas_call(
        body, name=name, grid=(t // tr,), in_specs=[pl.BlockSpec((tr, 2 * hd), lambda i: (i, 0))],
        out_specs=pl.BlockSpec((tr, hd), lambda i: (i, 0)), out_shape=jax.ShapeDtypeStruct((t, hd), bf16),
    )(ff)


def swiglu_bwd(ff, dact, *, name):
    t = ff.shape[0]
    hd = FFN_HIDDEN
    tr = _pick(t, 512, 8)

    def body(f_ref, d_ref, o_ref):
        gate, up, dv = f_ref[:, 0:hd].astype(f32), f_ref[:, hd:2 * hd].astype(f32), d_ref[...].astype(f32)
        o_ref[:, 0:hd] = (dv * up * _dsilu(gate)).astype(bf16)
        o_ref[:, hd:2 * hd] = (dv * _silu(gate)).astype(bf16)

    return pl.pallas_call(
        body, name=name, grid=(t // tr,),
        in_specs=[pl.BlockSpec((tr, 2 * hd), lambda i: (i, 0)), pl.BlockSpec((tr, hd), lambda i: (i, 0))],
        out_specs=pl.BlockSpec((tr, 2 * hd), lambda i: (i, 0)), out_shape=jax.ShapeDtypeStruct((t, 2 * hd), bf16),
    )(ff, dact)


def _adam_math(w, g, m, v):
    m = ADAM_B1 * m + (1.0 - ADAM_B1) * g
    v = ADAM_B2 * v + (1.0 - ADAM_B2) * (g * g)
    m_hat = m / (1.0 - ADAM_B1 ** ADAM_STEP)
    v_hat = v / (1.0 - ADAM_B2 ** ADAM_STEP)
    delta = -ADAM_LR * (m_hat / (jnp.sqrt(v_hat) + ADAM_EPS) + ADAM_WD * w)
    return delta, m, v


def _row_tile(rows, cols):
    cap = max(8, (1 << 19) // cols)
    return _pick(rows, cap, 8) if rows % 8 == 0 else rows


def adamw(w, g, m, v, *, name):
    rows, cols = w.shape
    tr = _row_tile(rows, cols)

    def body(w_ref, g_ref, m_ref, v_ref, d_ref, nm_ref, nv_ref):
        d, nm, nv = _adam_math(w_ref[...], g_ref[...], m_ref[...], v_ref[...])
        d_ref[...] = d
        nm_ref[...] = nm
        nv_ref[...] = nv

    tile = pl.BlockSpec((tr, cols), lambda i: (i, 0))
    return pl.pallas_call(
        body, name=name, grid=(rows // tr,), in_specs=[tile] * 4, out_specs=[tile] * 3,
        out_shape=[jax.ShapeDtypeStruct((rows, cols), f32)] * 3,
    )(w, g, m, v)


def adamw_many(ws, gs, ms, vs, *, name):
    n = len(ws)

    def body(*refs):
        for i in range(n):
            d, nm, nv = _adam_math(refs[i][...], refs[n + i][...], refs[2 * n + i][...], refs[3 * n + i][...])
            refs[4 * n + 3 * i][...] = d
            refs[4 * n + 3 * i + 1][...] = nm
            refs[4 * n + 3 * i + 2][...] = nv

    outs = pl.pallas_call(
        body, name=name, out_shape=[jax.ShapeDtypeStruct(w.shape, f32) for w in ws for _ in range(3)],
    )(*ws, *gs, *ms, *vs)
    return [tuple(outs[3 * i:3 * i + 3]) for i in range(n)]


def pair_add(dw, rbuf, idx, *, name):
    n, rows, cols = dw.shape
    hr = rows // 2
    tr = _row_tile(hr, cols)
    nrt = hr // tr

    def body(idx_ref, a_ref, b_ref, o_ref, own_ref):
        s = a_ref[...] + b_ref[...]
        o_ref[...] = s.astype(bf16)

        @pl.when(pl.program_id(1) == idx_ref[0])
        def _():
            own_ref[...] = s[0]

    return pl.pallas_call(
        body, name=name,
        grid_spec=pltpu.PrefetchScalarGridSpec(
            num_scalar_prefetch=1, grid=(nrt, n),
            in_specs=[pl.BlockSpec((1, tr, cols), lambda i, k, idx: (k, idx[1] * nrt + i, 0)),
                      pl.BlockSpec((1, tr, cols), lambda i, k, idx: (k, i, 0))],
            out_specs=[pl.BlockSpec((1, tr, cols), lambda i, k, idx: (k, i, 0)),
                       pl.BlockSpec((tr, cols), lambda i, k, idx: (i, 0))]),
        out_shape=[jax.ShapeDtypeStruct((n, hr, cols), bf16), jax.ShapeDtypeStruct((hr, cols), f32)],
    )(idx, dw, rbuf)


def chip_sum(own, rbuf, idx, *, name):
    hr, cols = own.shape
    tr = _row_tile(hr, cols)
    nrt = hr // tr

    def body(idx_ref, a_ref, b_ref, o_ref):
        o_ref[...] = ((a_ref[...] + b_ref[0].astype(f32)) + b_ref[1].astype(f32)) + b_ref[2].astype(f32)

    return pl.pallas_call(
        body, name=name,
        grid_spec=pltpu.PrefetchScalarGridSpec(
            num_scalar_prefetch=1, grid=(nrt,),
            in_specs=[pl.BlockSpec((tr, cols), lambda i, idx: (i, 0)),
                      pl.BlockSpec((3, tr, cols), lambda i, idx: (0, i, 0))],
            out_specs=pl.BlockSpec((tr, cols), lambda i, idx: (idx[1] * nrt + i, 0))),
        out_shape=jax.ShapeDtypeStruct((2 * hr, cols), f32),
    )(idx, own, rbuf)


def sum8(rbuf, *, name):
    n, rows, cols = rbuf.shape
    tr = _row_tile(rows, cols * n)

    def body(a_ref, o_ref):
        acc = a_ref[0]
        for k in range(1, n):
            acc = acc + a_ref[k]
        o_ref[...] = acc

    return pl.pallas_call(
        body, name=name, grid=(rows // tr,), in_specs=[pl.BlockSpec((n, tr, cols), lambda i: (0, i, 0))],
        out_specs=pl.BlockSpec((tr, cols), lambda i: (i, 0)), out_shape=jax.ShapeDtypeStruct((rows, cols), f32),
    )(rbuf)


def _coords():
    return lax.axis_index("x"), lax.axis_index("y"), lax.axis_index("c")


def _other_chips(x, y):
    return [(1 - x, y), (x, 1 - y), (1 - x, 1 - y)]


def gather_weights(shards, *, name):
    n = len(shards)
    halves = [s.shape[0] // 2 for s in shards]

    def body(*refs):
        ins, outs = refs[:n], refs[n:2 * n]
        send1, recv1, send2, recv2 = refs[2 * n:]
        x, y, c = _coords()
        me = 2 * x + y
        chips = _other_chips(x, y)
        sibling = (x, y, 1 - c)

        def half(i, k, hc):
            return outs[i].at[k, pl.ds(hc * halves[i], halves[i]), :]

        def ici(i, j):
            return pltpu.make_async_remote_copy(
                src_ref=ins[i].at[pl.ds(c * halves[i], halves[i]), :], dst_ref=half(i, me, c),
                send_sem=send1.at[i, j], recv_sem=recv1.at[i, j], device_id=(*chips[j], c), device_id_type=MESH)

        def landed(i, j):
            kj = 2 * chips[j][0] + chips[j][1]
            return pltpu.make_async_remote_copy(
                src_ref=half(i, kj, c), dst_ref=half(i, kj, c),
                send_sem=send2.at[i, j], recv_sem=recv1.at[i, j], device_id=sibling, device_id_type=MESH)

        def from_sibling(i, j):
            kj = 2 * chips[j][0] + chips[j][1]
            return pltpu.make_async_remote_copy(
                src_ref=half(i, kj, 1 - c), dst_ref=half(i, kj, 1 - c),
                send_sem=send2.at[i, j], recv_sem=recv2.at[i, j], device_id=sibling, device_id_type=MESH)

        def d2d(i, j):
            kj = 2 * chips[j][0] + chips[j][1]
            return pltpu.make_async_remote_copy(
                src_ref=half(i, kj, c), dst_ref=half(i, kj, c),
                send_sem=send2.at[i, j], recv_sem=recv2.at[i, j], device_id=sibling, device_id_type=MESH)

        for j in range(3):
            for i in range(n):
                ici(i, j).start()
        for j in range(3):
            for i in range(n):
                landed(i, j).wait_recv()
                d2d(i, j).start()
        for j in range(3):
            for i in range(n):
                from_sibling(i, j).wait_recv()
        for j in range(3):
            for i in range(n):
                ici(i, j).wait_send()
                d2d(i, j).wait_send()

    return pl.pallas_call(
        body, name=name, in_specs=[ANY] * n, out_specs=[ANY] * n,
        out_shape=[jax.ShapeDtypeStruct((N_CHIPS,) + s.shape, s.dtype) for s in shards],
        scratch_shapes=[pltpu.SemaphoreType.DMA((n, 3))] * 4,
    )(*shards)


def pair_exchange(grads, *, name):
    n = len(grads)
    halves = [g.shape[1] // 2 for g in grads]

    def body(*refs):
        ins, outs = refs[:n], refs[n:2 * n]
        send, recv = refs[2 * n:]
        x, y, c = _coords()
        cps = [pltpu.make_async_remote_copy(
            src_ref=ins[i].at[:, pl.ds((1 - c) * halves[i], halves[i]), :], dst_ref=outs[i],
            send_sem=send.at[i], recv_sem=recv.at[i], device_id=(x, y, 1 - c), device_id_type=MESH) for i in range(n)]
        for cp in cps:
            cp.start()
        for cp in cps:
            cp.wait()

    return pl.pallas_call(
        body, name=name, in_specs=[ANY] * n, out_specs=[ANY] * n,
        out_shape=[jax.ShapeDtypeStruct((N_CHIPS, g.shape[1] // 2, g.shape[2]), g.dtype) for g in grads],
        scratch_shapes=[pltpu.SemaphoreType.DMA((n,))] * 2,
    )(*grads)


def pair_gather(bufs, *, name):
    n = len(bufs)

    def body(*refs):
        ins, outs = refs[:n], refs[n:2 * n]
        send, recv = refs[2 * n:]
        x, y, c = _coords()
        cps = []
        for i in range(n):
            hr = ins[i].shape[0] // 2
            cps.append(pltpu.make_async_remote_copy(
                src_ref=ins[i].at[pl.ds(c * hr, hr), :], dst_ref=outs[i].at[pl.ds(c * hr, hr), :],
                send_sem=send.at[i], recv_sem=recv.at[i], device_id=(x, y, 1 - c), device_id_type=MESH))
        for cp in cps:
            cp.start()
        for i in range(n):
            hr = ins[i].shape[0] // 2
            pltpu.make_async_remote_copy(
                src_ref=ins[i].at[pl.ds((1 - c) * hr, hr), :], dst_ref=outs[i].at[pl.ds((1 - c) * hr, hr), :],
                send_sem=send.at[i], recv_sem=recv.at[i], device_id=(x, y, 1 - c), device_id_type=MESH).wait_recv()
        for cp in cps:
            cp.wait_send()

    return pl.pallas_call(
        body, name=name, in_specs=[ANY] * n, out_specs=[ANY] * n,
        out_shape=[jax.ShapeDtypeStruct(b.shape, b.dtype) for b in bufs],
        input_output_aliases={i: i for i in range(n)},
        scratch_shapes=[pltpu.SemaphoreType.DMA((n,))] * 2,
    )(*bufs)


def all_exchange(buf, *, name):
    rows, cols = buf.shape

    def body(in_ref, out_ref, send, recv):
        x, y, c = _coords()
        me = 4 * x + 2 * y + c
        cps = []
        for d in range(1, 8):
            px = 1 - x if d & 4 else x
            py = 1 - y if d & 2 else y
            pc = 1 - c if d & 1 else c
            cps.append(pltpu.make_async_remote_copy(
                src_ref=in_ref, dst_ref=out_ref.at[me], send_sem=send.at[d - 1], recv_sem=recv.at[d - 1],
                device_id=(px, py, pc), device_id_type=MESH))
        for cp in cps:
            cp.start()
        for d in range(1, 8):
            px = 1 - x if d & 4 else x
            py = 1 - y if d & 2 else y
            pc = 1 - c if d & 1 else c
            src = 4 * px + 2 * py + pc
            pltpu.make_async_remote_copy(
                src_ref=in_ref, dst_ref=out_ref.at[src], send_sem=send.at[d - 1], recv_sem=recv.at[d - 1],
                device_id=(px, py, pc), device_id_type=MESH).wait_recv()
        for cp in cps:
            cp.wait_send()

    return pl.pallas_call(
        body, name=name, in_specs=[ANY], out_specs=ANY,
        out_shape=jax.ShapeDtypeStruct((8, rows, cols), buf.dtype),
        scratch_shapes=[pltpu.SemaphoreType.DMA((7,)), pltpu.SemaphoreType.DMA((7,))],
    )(buf)


HBM = pl.BlockSpec(memory_space=pltpu.HBM)
SEM = pl.BlockSpec(memory_space=pltpu.SEMAPHORE)
EFFECT = pltpu.SideEffectType.DATAFLOW_SIDE_EFFECTING


def split_start(arrays, after, copies, sem_shape, *, name):
    na = len(arrays)

    def body(*refs):
        for cp in copies(refs[:na], refs[na + 1], refs[na + 2]):
            cp.start()
        refs[-1][...] = jnp.zeros((8, 128), f32)

    outs = pl.pallas_call(
        body, name=name,
        out_shape=(pltpu.SemaphoreType.DMA(sem_shape), pltpu.SemaphoreType.DMA(sem_shape),
                   *[pltpu.HBM(a.shape, a.dtype) for a in arrays], jax.ShapeDtypeStruct((8, 128), f32)),
        in_specs=[HBM] * na + [ANY], out_specs=(SEM, SEM, *[HBM] * na, pl.BlockSpec(memory_space=pltpu.VMEM)),
        input_output_aliases={i: 2 + i for i in range(na)},
        compiler_params=pltpu.CompilerParams(has_side_effects=EFFECT),
    )(*[pltpu.with_memory_space_constraint(a, pltpu.HBM) for a in arrays], after)
    return outs[0], outs[1], list(outs[2:2 + na]), outs[-1]


def split_wait(send, recv, arrays, after, copies, *, name):
    na = len(arrays)

    def body(*refs):
        for cp in copies(refs[:na], refs[na], refs[na + 1]):
            cp.wait_send()
            cp.wait_recv()

    outs = pl.pallas_call(
        body, name=name, out_shape=tuple(pltpu.HBM(a.shape, a.dtype) for a in arrays),
        in_specs=[HBM] * na + [SEM, SEM, ANY], out_specs=tuple([HBM] * na),
        input_output_aliases={i: i for i in range(na)},
        compiler_params=pltpu.CompilerParams(has_side_effects=EFFECT),
    )(*arrays, send, recv, after)
    return list(outs)


def gather_copies(n):
    def copies(refs, send, recv):
        x, y, c = _coords()
        me = 2 * x + y
        chips = _other_chips(x, y)
        return [pltpu.make_async_remote_copy(
            src_ref=refs[i], dst_ref=refs[n + i].at[me], send_sem=send.at[3 * i + j], recv_sem=recv.at[3 * i + j],
            device_id=(*chips[j], c), device_id_type=MESH) for j in range(3) for i in range(n)]
    return copies


def pair_copies(n):
    def copies(refs, send, recv):
        x, y, c = _coords()
        cps = []
        for i in range(n):
            hr = refs[i].shape[1] // 2
            cps.append(pltpu.make_async_remote_copy(
                src_ref=refs[i].at[:, pl.ds((1 - c) * hr, hr), :], dst_ref=refs[n + i], send_sem=send.at[i],
                recv_sem=recv.at[i], device_id=(x, y, 1 - c), device_id_type=MESH))
        return cps
    return copies


def all_copies():
    def copies(refs, send, recv):
        x, y, c = _coords()
        me = 4 * x + 2 * y + c
        cps = []
        for d in range(1, 8):
            peer = (1 - x if d & 4 else x, 1 - y if d & 2 else y, 1 - c if d & 1 else c)
            cps.append(pltpu.make_async_remote_copy(
                src_ref=refs[0], dst_ref=refs[1].at[me], send_sem=send.at[d - 1], recv_sem=recv.at[d - 1],
                device_id=peer, device_id_type=MESH))
        return cps
    return copies


def reduce_copies(n):
    def copies(refs, send, recv):
        x, y, c = _coords()
        chips = _other_chips(x, y)
        return [pltpu.make_async_remote_copy(
            src_ref=refs[i].at[2 * chips[j][0] + chips[j][1]], dst_ref=refs[n + i].at[j],
            send_sem=send.at[3 * i + j], recv_sem=recv.at[3 * i + j], device_id=(*chips[j], c), device_id_type=MESH)
            for j in range(3) for i in range(n)]
    return copies


def _pack(arrs):
    flat = []
    for a in arrs:
        v = a.reshape(-1).astype(f32)
        pad = (-v.shape[0]) % 128
        flat.append(jnp.pad(v, (0, pad)) if pad else v)
    v = jnp.concatenate(flat)
    rows = v.shape[0] // 128
    pad_rows = (-rows) % 256
    v = v.reshape(rows, 128)
    return jnp.pad(v, ((0, pad_rows), (0, 0))) if pad_rows else v


def _unpack(buf, shapes):
    out, row = [], 0
    for s in shapes:
        size = math.prod(s)
        rows = -(-size // 128)
        out.append(buf[row:row + rows].reshape(-1)[:size].reshape(s))
        row += rows
    return out


def _ref_of_perm():
    ref = np.arange(IN_PROJ_DIM)
    xbc = ref[4096:7168]
    xbc_p = [np.concatenate([xbc[g * 512:(g + 1) * 512], xbc[2048 + g * 128:2048 + (g + 1) * 128],
                             xbc[2560 + g * 128:2560 + (g + 1) * 128]]) for g in range(SSM_GROUPS)]
    return np.concatenate([ref[0:2048], ref[2048:4096], ref[7200:8480], ref[8480:9760], ref[7168:7200],
                           -np.ones(DT_PAD_W - SSM_HEADS, np.int64)] + xbc_p)


def _runs(vals):
    out, start = [], 0
    for i in range(1, len(vals) + 1):
        if i == len(vals) or not (vals[i] == vals[i - 1] + 1 or (vals[i] < 0 and vals[i - 1] < 0)):
            out.append((start, int(vals[start]), i - start))
            start = i
    return out


def _perm_in_from_shards(g):
    ref_of_perm = _ref_of_perm()
    sw = IN_PROJ_DIM // N_CHIPS
    parts = []
    for _, first, length in _runs(ref_of_perm):
        if first < 0:
            parts.append(jnp.zeros((g.shape[1], length), g.dtype))
            continue
        lo = first
        while lo < first + length:
            k = lo // sw
            hi = min(first + length, (k + 1) * sw)
            parts.append(g[k, :, lo - k * sw:hi - k * sw])
            lo = hi
    return jnp.concatenate(parts, axis=-1)


def _unperm_in_to_shards(w):
    ref_of_perm = _ref_of_perm()
    perm_of_ref = np.zeros(IN_PROJ_DIM, np.int64)
    perm_of_ref[ref_of_perm[ref_of_perm >= 0]] = np.nonzero(ref_of_perm >= 0)[0]
    sw = IN_PROJ_DIM // N_CHIPS
    shards = []
    for k in range(N_CHIPS):
        runs = _runs(perm_of_ref[k * sw:(k + 1) * sw])
        shards.append(jnp.concatenate([w[:, first:first + length] for _, first, length in runs], axis=-1))
    return jnp.stack(shards)


def _perm_xbc_cols(w):
    parts = []
    for g in range(SSM_GROUPS):
        parts += [w[..., g * 512:(g + 1) * 512], w[..., 2048 + g * 128:2048 + (g + 1) * 128],
                  w[..., 2560 + g * 128:2560 + (g + 1) * 128]]
    return jnp.concatenate(parts, axis=-1)


def _unperm_xbc_cols(w):
    xs = [w[..., g * XBC_GROUP_W:g * XBC_GROUP_W + 512] for g in range(SSM_GROUPS)]
    bs = [w[..., g * XBC_GROUP_W + 512:g * XBC_GROUP_W + 640] for g in range(SSM_GROUPS)]
    cs = [w[..., g * XBC_GROUP_W + 640:(g + 1) * XBC_GROUP_W] for g in range(SSM_GROUPS)]
    return jnp.concatenate(xs + bs + cs, axis=-1)


def _from_col_shards(w):
    n, r, c = w.shape
    return jnp.transpose(w, (1, 0, 2)).reshape(r, n * c)


def kernel(x, norm1_w, w_in, b_branch_gate, ssm_conv_w, ssm_conv_b, ssm_dt_bias, ssm_a_log, ssm_d, ssm_norm_w, w_out_ssm, lru_conv_w, lru_conv_b, lru_w_r, lru_b_r, lru_w_i, lru_b_i, lru_lambda, w_out_lru, w_out, norm2_w, w_ffn_in, w_ffn_out, norm_f_w, loss_target, m_norm1_w, m_w_in, m_b_branch_gate, m_ssm_conv_w, m_ssm_conv_b, m_ssm_dt_bias, m_ssm_a_log, m_ssm_d, m_ssm_norm_w, m_w_out_ssm, m_lru_conv_w, m_lru_conv_b, m_lru_w_r, m_lru_b_r, m_lru_w_i, m_lru_b_i, m_lru_lambda, m_w_out_lru, m_w_out, m_norm2_w, m_w_ffn_in, m_w_ffn_out, m_norm_f_w, v_norm1_w, v_w_in, v_b_branch_gate, v_ssm_conv_w, v_ssm_conv_b, v_ssm_dt_bias, v_ssm_a_log, v_ssm_d, v_ssm_norm_w, v_w_out_ssm, v_lru_conv_w, v_lru_conv_b, v_lru_w_r, v_lru_b_r, v_lru_w_i, v_lru_b_i, v_lru_lambda, v_w_out_lru, v_w_out, v_norm2_w, v_w_ffn_in, v_w_ffn_out, v_norm_f_w):
    xi, yi, ci = lax.axis_index("x"), lax.axis_index("y"), lax.axis_index("c")
    me = 2 * xi + yi
    idx = jnp.stack([me, ci]).astype(jnp.int32)
    x2 = x[0]
    tgt = loss_target[0]

    big_names = ["w_in", "w_out_ssm", "w_out_lru", "w_out", "w_ffn_in", "w_ffn_out"]
    big_w = dict(w_in=w_in[0], w_out_ssm=w_out_ssm[0], w_out_lru=w_out_lru[0], w_out=w_out[0], w_ffn_in=w_ffn_in[0],
                 w_ffn_out=w_ffn_out[0])
    big_m = dict(w_in=m_w_in[0], w_out_ssm=m_w_out_ssm[0], w_out_lru=m_w_out_lru[0], w_out=m_w_out[0],
                 w_ffn_in=m_w_ffn_in[0], w_ffn_out=m_w_ffn_out[0])
    big_v = dict(w_in=v_w_in[0], w_out_ssm=v_w_out_ssm[0], w_out_lru=v_w_out_lru[0], w_out=v_w_out[0],
                 w_ffn_in=v_w_ffn_in[0], w_ffn_out=v_w_ffn_out[0])
    conv_pad = jnp.zeros((16, 768), f32).at[0:4, :].set(ssm_conv_w[0]).at[8:12, 0:320].set(lru_conv_w[0])
    mine = [big_w["w_in"].astype(bf16), conv_pad]
    gathered = gather_weights(mine, name="gather_weights")
    g_in, g_conv = [lax.dynamic_update_index_in_dim(g, s, me, 0) for g, s in zip(gathered, mine)]
    w_in_p = _perm_in_from_shards(g_in)
    late_names = big_names[1:]
    late = [big_w[k].astype(bf16) for k in late_names]
    late_lands = [lax.empty((N_CHIPS,) + s.shape, bf16) for s in late]
    g_send, g_recv, g_arrays, g_token = split_start(late + late_lands, g_conv, gather_copies(5), (15,),
                                                    name="gather_late_start")
    ssm_cw_full = _from_col_shards(g_conv[:, 0:4, :])
    lru_cw_full = _from_col_shards(g_conv[:, 8:12, 0:320])
    ssm_cw_p = _perm_xbc_cols(ssm_cw_full)
    ssm_cb_p = _perm_xbc_cols(ssm_conv_b)

    par = jnp.stack([ssm_dt_bias[0], ssm_a_log[0], ssm_d[0]], axis=0).reshape(3, SSM_GROUPS, SSM_HPG)
    par_row = jnp.zeros((SSM_GROUPS, 8, 8), f32).at[:, 0:3, :].set(jnp.transpose(par, (1, 0, 2)))
    par_col = jnp.transpose(par_row, (0, 2, 1))

    hn1 = rms_fwd(x2, norm1_w + g_token[0:1, 0:1], name="rms1_fwd")
    proj = mm(hn1, w_in_p, "nn", name="in_proj")
    t = x2.shape[0]
    dtr = jnp.transpose(proj[:, OFF_DT:OFF_DT + 32].reshape(t, SSM_GROUPS, SSM_HPG), (1, 0, 2))
    dtr_t = jnp.transpose(dtr, (0, 2, 1))
    xbc_pre, xbc_post = conv_fwd(proj, OFF_XBC, SSM_CONV_DIM, ssm_cw_p, ssm_cb_p, silu=True, name="ssm_conv_fwd")
    y_ssd, s_in = ssd_fwd(xbc_post, dtr, dtr_t, par_row, par_col, name="ssd_fwd")
    yn = gnorm_fwd(y_ssd, proj, ssm_norm_w, name="gnorm_fwd")
    g_arrays = split_wait(g_send, g_recv, g_arrays, yn, gather_copies(5), name="gather_late_wait")
    g_out_ssm, g_out_lru, g_out, g_ffn_in, g_ffn_out = [
        lax.dynamic_update_index_in_dim(g, s, me, 0) for g, s in zip(g_arrays[5:], late)]
    w_out_ssm_f = g_out_ssm.reshape(SSM_D_INNER, D_MODEL)
    w_out_lru_f = g_out_lru.reshape(LRU_WIDTH, D_MODEL)
    w_out_f = g_out.reshape(D_MODEL, D_MODEL)
    w_ffn_out_f = g_ffn_out.reshape(FFN_HIDDEN, D_MODEL)
    y_ssm = mm(yn, w_out_ssm_f, "nn", out_dtype=bf16, name="out_ssm")
    (u_lru,) = conv_fwd(proj, OFF_LX, LRU_WIDTH, lru_cw_full, lru_conv_b, silu=False, name="lru_conv_fwd")
    h_lru, o_lru = lru_fwd(u_lru, proj, lru_w_r[0], lru_b_r, lru_w_i[0], lru_b_i, lru_lambda, name="lru_fwd")
    y_lru = mm(o_lru, w_out_lru_f, "nn", out_dtype=bf16, name="out_lru")
    mix = merge_fwd(proj, b_branch_gate, y_ssm, y_lru, name="merge_fwd")
    h1, hn2 = mm(mix, w_out_f, "nn", add=x2, name="out_proj",
                 epi=(epi_rms_fwd, [], [norm2_w], [("row", f32), ("row", bf16)]))
    ff = mm(hn2, g_ffn_in, "nn", b_shards=True, out_dtype=bf16, name="ffn_in")
    act = swiglu_fwd(ff, name="swiglu_fwd")
    dh2, dh2_b, d_norm_f, loss_tile = mm(act, w_ffn_out_f, "nn", add=h1, name="ffn_out",
                                         epi=(epi_loss, [tgt], [norm_f_w.reshape(1, D_MODEL)],
                                              [("row", f32), ("row", bf16), ("vec",), ("tile",)]))

    d_w_ffn_out = mm(act, dh2_b, "tn", name="d_w_ffn_out")
    dact = mm(dh2_b, w_ffn_out_f, "nt", out_dtype=bf16, name="d_act")
    dff = swiglu_bwd(ff, dact, name="swiglu_bwd")
    d_w_ffn_in = mm(hn2, dff, "tn", out_shards=N_CHIPS, name="d_w_ffn_in")
    dh1, dh1_b, d_norm2 = mm(dff, g_ffn_in, "nt", b_shards=True, name="d_hn2",
                             epi=(epi_rms_bwd, [h1, dh2], [norm2_w], [("row", f32), ("row", bf16), ("vec",)]))
    d_w_out = mm(mix, dh1_b, "tn", name="d_w_out")
    dmix = mm(dh1_b, w_out_f, "nt", out_dtype=bf16, name="d_mix")
    dproj, dy_ssm, dy_lru, d_bg = merge_bwd(proj, b_branch_gate, y_ssm, y_lru, dmix, name="merge_bwd")
    d_w_out_ssm = mm(yn, dy_ssm, "tn", name="d_w_out_ssm")
    d_w_out_lru = mm(o_lru, dy_lru, "tn", name="d_w_out_lru")
    early_g = [d_w_out_ssm.reshape(N_CHIPS, 512, D_MODEL), d_w_out_lru.reshape(N_CHIPS, 320, D_MODEL),
               d_w_out.reshape(N_CHIPS, 256, D_MODEL), d_w_ffn_in, d_w_ffn_out.reshape(N_CHIPS, 704, D_MODEL)]
    p_lands = [lax.empty((N_CHIPS, g.shape[1] // 2, g.shape[2]), f32) for g in early_g]
    p_send, p_recv, p_arrays, p_token = split_start(early_g + p_lands, early_g[0], pair_copies(5), (5,),
                                                    name="pair_early_start")
    dyn = mm(dy_ssm, w_out_ssm_f, "nt", out_dtype=bf16, after=p_token, name="d_yn")
    dy_ssd, dproj, d_ssm_norm = gnorm_bwd(y_ssd, proj, ssm_norm_w, dyn, dproj, name="gnorm_bwd")
    p_arrays = split_wait(p_send, p_recv, p_arrays, dy_ssd, pair_copies(5), name="pair_early_wait")
    e_pairs = [pair_add(g, rb, idx, name="pair_add_" + k) for g, rb, k in zip(p_arrays[:5], p_arrays[5:], late_names)]
    e_lands = [lax.empty((3,) + p[0].shape[1:], bf16) for p in e_pairs]
    e_send, e_recv, e_arrays, e_token = split_start([p[0] for p in e_pairs] + e_lands, e_pairs[0][1], reduce_copies(5),
                                                    (15,), name="reduce_early_start")
    dxbc_post, ddtr, dpar = ssd_bwd(xbc_post, dtr, dtr_t, par_row + e_token[0:1, 0:1], par_col, s_in, dy_ssd,
                                    name="ssd_bwd")
    dproj, d_ssm_cw_p, d_ssm_cb_p = conv_bwd(dxbc_post, xbc_pre, proj, OFF_XBC, ssm_cw_p, dproj, name="ssm_conv_bwd")
    do_lru = mm(dy_lru, w_out_lru_f, "nt", name="d_o_lru")
    du_lru, dproj, d_w_r, d_w_i, d_b_r, d_b_i, d_lam = lru_bwd(u_lru, proj, h_lru, do_lru, lru_w_r[0], lru_b_r, lru_w_i[0],
                                                               lru_b_i, lru_lambda, dproj, name="lru_bwd")
    dproj, d_lru_cw, d_lru_cb = conv_bwd(du_lru, None, proj, OFF_LX, lru_cw_full, dproj, name="lru_conv_bwd")
    ddt_cols = jnp.transpose(ddtr, (1, 0, 2)).reshape(t, SSM_HEADS).astype(bf16)
    ddt_cols = jnp.pad(ddt_cols, ((0, 0), (0, DT_PAD_W - SSM_HEADS)))
    dproj = lax.dynamic_update_slice(dproj, ddt_cols, (0, OFF_DT))

    d_ssm_cw = _unperm_xbc_cols(d_ssm_cw_p)
    d_ssm_cb = _unperm_xbc_cols(d_ssm_cb_p)
    dpar_h = jnp.transpose(dpar[:, 0:3, :], (1, 0, 2)).reshape(3, SSM_HEADS)
    small_names = ["norm1_w", "b_branch_gate", "ssm_conv_b", "ssm_dt_bias", "ssm_a_log", "ssm_d", "ssm_norm_w",
                   "lru_conv_b", "lru_w_r", "lru_b_r", "lru_w_i", "lru_b_i", "lru_lambda", "norm2_w", "norm_f_w"]
    small_g = dict(norm1_w=jnp.zeros_like(norm1_w), b_branch_gate=d_bg, ssm_conv_b=d_ssm_cb, ssm_dt_bias=dpar_h[0:1], ssm_a_log=dpar_h[1:2],
                   ssm_d=dpar_h[2:3], ssm_norm_w=d_ssm_norm, lru_conv_b=d_lru_cb, lru_w_r=d_w_r[None], lru_b_r=d_b_r,
                   lru_w_i=d_w_i[None], lru_b_i=d_b_i, lru_lambda=d_lam, norm2_w=d_norm2, norm_f_w=d_norm_f.reshape(D_MODEL))
    small_w = dict(norm1_w=norm1_w, b_branch_gate=b_branch_gate, ssm_conv_b=ssm_conv_b, ssm_dt_bias=ssm_dt_bias,
                   ssm_a_log=ssm_a_log, ssm_d=ssm_d, ssm_norm_w=ssm_norm_w, lru_conv_b=lru_conv_b, lru_w_r=lru_w_r,
                   lru_b_r=lru_b_r, lru_w_i=lru_w_i, lru_b_i=lru_b_i, lru_lambda=lru_lambda, norm2_w=norm2_w, norm_f_w=norm_f_w)
    small_m = dict(norm1_w=m_norm1_w, b_branch_gate=m_b_branch_gate, ssm_conv_b=m_ssm_conv_b, ssm_dt_bias=m_ssm_dt_bias,
                   ssm_a_log=m_ssm_a_log, ssm_d=m_ssm_d, ssm_norm_w=m_ssm_norm_w, lru_conv_b=m_lru_conv_b, lru_w_r=m_lru_w_r,
                   lru_b_r=m_lru_b_r, lru_w_i=m_lru_w_i, lru_b_i=m_lru_b_i, lru_lambda=m_lru_lambda, norm2_w=m_norm2_w,
                   norm_f_w=m_norm_f_w)
    small_v = dict(norm1_w=v_norm1_w, b_branch_gate=v_b_branch_gate, ssm_conv_b=v_ssm_conv_b, ssm_dt_bias=v_ssm_dt_bias,
                   ssm_a_log=v_ssm_a_log, ssm_d=v_ssm_d, ssm_norm_w=v_ssm_norm_w, lru_conv_b=v_lru_conv_b, lru_w_r=v_lru_w_r,
                   lru_b_r=v_lru_b_r, lru_w_i=v_lru_w_i, lru_b_i=v_lru_b_i, lru_lambda=v_lru_lambda, norm2_w=v_norm2_w,
                   norm_f_w=v_norm_f_w)
    shapes = [small_w[k].shape for k in small_names]
    conv_shapes = [(4, SSM_CONV_DIM), (4, LRU_WIDTH)]
    g_pack = _pack([small_g[k] for k in small_names] + [d_ssm_cw, d_lru_cw])
    s_send, s_recv, s_arrays, s_token = split_start([g_pack, lax.empty((8,) + g_pack.shape, f32)], g_pack, all_copies(),
                                                    (7,), name="small_start")
    d_w_in_p = mm(hn1, dproj, "tn", after=s_token, name="d_w_in")

    d_w_in_s = _unperm_in_to_shards(d_w_in_p)
    (l_sib,) = pair_exchange([d_w_in_s], name="pair_exchange_late")
    l_pair = pair_add(d_w_in_s, l_sib, idx, name="pair_add_w_in")
    l_land = lax.empty((3,) + l_pair[0].shape[1:], bf16)
    l_send, l_recv, l_arrays, l_token = split_start([l_pair[0], l_land], l_pair[1], reduce_copies(1), (3,),
                                                    name="reduce_late_start")
    grad_x, d_norm1 = mm(dproj, w_in_p, "nt", after=l_token, name="d_hn1",
                         epi=(epi_rms_bwd, [x2, dh1], [norm1_w], [("row", f32), ("vec",)]))

    e_arrays = split_wait(e_send, e_recv, e_arrays, d_norm1, reduce_copies(5), name="reduce_early_wait")
    e_half = [chip_sum(p[1], rb, idx, name="chip_sum_" + k) for p, rb, k in zip(e_pairs, e_arrays[5:], late_names)]
    big_out = {}
    for k, g in zip(late_names, pair_gather(e_half, name="pair_gather_early")):
        big_out[k] = (g,) + tuple(adamw(big_w[k], g, big_m[k], big_v[k], name="adamw_" + k))

    s_arrays = split_wait(s_send, s_recv, s_arrays, d_norm1, all_copies(), name="small_wait")
    g_sum = sum8(lax.dynamic_update_index_in_dim(s_arrays[1], g_pack, 2 * me + ci, 0), name="sum8")
    n1 = jnp.concatenate([d_norm1.reshape(8, 128), loss_tile], axis=0)
    n1_sum = sum8(lax.dynamic_update_index_in_dim(all_exchange(n1, name="all_exchange_norm1"), n1, 2 * me + ci, 0),
                  name="sum8_norm1")
    loss = n1_sum[8, 0]
    g_sum = lax.dynamic_update_slice(g_sum, n1_sum[0:8], (0, 0))
    g_small = _unpack(g_sum, shapes + conv_shapes)
    g_small[-2] = lax.dynamic_slice_in_dim(g_small[-2], me * 768, 768, axis=1)
    g_small[-1] = lax.dynamic_slice_in_dim(g_small[-1], me * 320, 320, axis=1)
    all_names = small_names + ["ssm_conv_w", "lru_conv_w"]
    small_w.update(ssm_conv_w=ssm_conv_w[0], lru_conv_w=lru_conv_w[0])
    small_m.update(ssm_conv_w=m_ssm_conv_w[0], lru_conv_w=m_lru_conv_w[0])
    small_v.update(ssm_conv_w=v_ssm_conv_w[0], lru_conv_w=v_lru_conv_w[0])
    as2d = lambda a: a.reshape(-1, a.shape[-1])
    upd = adamw_many([as2d(small_w[k]) for k in all_names], [as2d(g) for g in g_small],
                     [as2d(small_m[k]) for k in all_names], [as2d(small_v[k]) for k in all_names], name="adamw_small")
    small_out = {}
    for k, g, u in zip(all_names, g_small, upd):
        small_out[k] = (g,) + tuple(o.reshape(g.shape) for o in u)
    l_arrays = split_wait(l_send, l_recv, l_arrays, upd[0][0], reduce_copies(1), name="reduce_late_wait")
    l_half = chip_sum(l_pair[1], l_arrays[1], idx, name="chip_sum_w_in")
    (g_w_in,) = pair_gather([l_half], name="pair_gather_late")
    big_out["w_in"] = (g_w_in,) + tuple(adamw(big_w["w_in"], g_w_in, big_m["w_in"], big_v["w_in"], name="adamw_w_in"))

    order = ["norm1_w", "w_in", "b_branch_gate", "ssm_conv_w", "ssm_conv_b", "ssm_dt_bias", "ssm_a_log", "ssm_d", "ssm_norm_w",
             "w_out_ssm", "lru_conv_w", "lru_conv_b", "lru_w_r", "lru_b_r", "lru_w_i", "lru_b_i", "lru_lambda", "w_out_lru",
             "w_out", "norm2_w", "w_ffn_in", "w_ffn_out", "norm_f_w"]
    outs = [loss, grad_x[None]]
    for which in range(4):
        for k in order:
            if k in big_out:
                outs.append(big_out[k][which][None])
            elif k in ("ssm_conv_w", "lru_conv_w"):
                outs.append(small_out[k][which][None])
            else:
                outs.append(small_out[k][which])
    return tuple(outs)
```

```python
import math

import jax
import jax.numpy as jnp
import numpy as np
from jax import lax
from jax.experimental import pallas as pl
from jax.experimental.pallas import tpu as pltpu

f32 = jnp.float32
bf16 = jnp.bfloat16

D_MODEL = 1024
SSM_D_INNER = 2048
SSM_HEADS = 32
SSM_HEAD_DIM = 64
SSM_GROUPS = 4
SSM_HPG = 8
SSM_D_STATE = 128
SSM_CHUNK = 128
SSM_GROUP_W = 512
SSM_CONV_DIM = 3072
XBC_GROUP_W = 768
LRU_WIDTH = 1280
LRU_BLOCKS = 10
LRU_BLOCK = 128
LRU_C = 8.0
FFN_HIDDEN = 2816
RMS_EPS = 1e-6
IN_PROJ_DIM = 9760
N_CHIPS = 4

OFF_GATES = 0
OFF_Z = 2048
OFF_LX = 4096
OFF_LY = 5376
OFF_DT = 6656
DT_PAD_W = 256
OFF_XBC = 6912
PROJ_W = 9984

ADAM_LR = 0.001
ADAM_B1 = 0.9
ADAM_B2 = 0.999
ADAM_EPS = 1e-08
ADAM_WD = 0.01
ADAM_STEP = 10

MESH = pl.DeviceIdType.MESH
ANY = pl.BlockSpec(memory_space=pl.ANY)

NN = (((1,), (0,)), ((), ()))
NT = (((1,), (1,)), ((), ()))
TN = (((0,), (0,)), ((), ()))


def _pick(n, cap, mult=128):
    best = None
    for t in range(mult, min(n, cap) + 1, mult):
        if n % t == 0:
            best = t
    return best if best is not None else n


def _sigmoid(x):
    return 0.5 * jnp.tanh(0.5 * x) + 0.5


def _softplus(x):
    return jnp.maximum(x, 0.0) + jnp.log(1.0 + jnp.exp(-jnp.abs(x)))


def _silu(x):
    return x * _sigmoid(x)


def _dsilu(x):
    s = _sigmoid(x)
    return s * (1.0 + x * (1.0 - s))


_GELU_K = math.sqrt(2.0 / math.pi)


def _gelu(x):
    return 0.5 * x * (1.0 + jnp.tanh(_GELU_K * (x + 0.044715 * x * x * x)))


def _dgelu(x):
    t = jnp.tanh(_GELU_K * (x + 0.044715 * x * x * x))
    return 0.5 * (1.0 + t) + 0.5 * x * (1.0 - t * t) * _GELU_K * (1.0 + 3.0 * 0.044715 * x * x)


def _expm1(x):
    poly = x * (1.0 + x * (0.5 + x * (1.0 / 6.0 + x * (1.0 / 24.0 + x * (1.0 / 120.0 + x * (1.0 / 720.0))))))
    return jnp.where(jnp.abs(x) < 0.1, poly, jnp.exp(x) - 1.0)


def _dot(a, b, dn):
    return lax.dot_general(a.astype(bf16), b.astype(bf16), dn, preferred_element_type=f32)


def _dot_01(a, b, dn, split, terms):
    r = a if split == 0 else b
    out = None
    for _ in range(terms):
        h = r.astype(bf16)
        r = r - h.astype(f32)
        d = lax.dot_general(h if split == 0 else a.astype(bf16), b.astype(bf16) if split == 0 else h, dn,
                            preferred_element_type=f32)
        out = d if out is None else out + d
    return out


MM_VMEM_BUDGET = 48 * 2 ** 20

def mm(a, b, mode, *, name, add=None, after=None, out_dtype=f32, b_shards=False, out_shards=0, epi=None):
    bs = b.shape[1:] if b_shards else b.shape
    shard_w = b.shape[2] if b_shards else None
    bcols = bs[1] * (b.shape[0] if b_shards else 1)
    if mode == "nn":
        (m, k), (k2, n) = a.shape, (bs[0], bcols)
    elif mode == "nt":
        (m, k), (n, k2) = a.shape, (bs[0], bcols)
    else:
        (k, m), (k2, n) = a.shape, b.shape
    assert k == k2, (a.shape, b.shape, mode)
    tn = _pick(n, 1536)
    if b_shards and mode == "nn":
        tn = shard_w
    if out_shards:
        tn = n // out_shards
    isz = lambda v: jnp.dtype(v.dtype).itemsize
    if epi is not None:
        assert n <= 1536 and not out_shards
        tn = n
        epi_fn, epi_rows, epi_vecs, epi_outs = epi
        tile_bytes = sum(isz(v) for v in epi_rows) + sum(jnp.dtype(o[1]).itemsize for o in epi_outs if o[0] == "row")
    else:
        epi_rows, epi_vecs, epi_outs = [], [], []
        tile_bytes = jnp.dtype(out_dtype).itemsize
    tks = [shard_w] if (b_shards and mode == "nt") else sorted({k, _pick(k, 3328), _pick(k, 2048), _pick(k, 1024)}, reverse=True)

    def vmem_of(tm, tk):
        blocks = tm * tk * isz(a) + tk * tn * isz(b) + tm * tn * (4 * int(add is not None) + tile_bytes)
        return 2 * blocks + 4 * tm * tn * int(k > tk)

    fits = [(tk, tm) for tk in tks for tm in (_pick(m, 1536), _pick(m, 1024), _pick(m, 512)) if vmem_of(tm, tk) <= MM_VMEM_BUDGET]
    tk, tm = fits[0] if fits else (tks[-1], _pick(m, 256))
    nk = k // tk
    dn = {"nn": NN, "nt": NT, "tn": TN}[mode]
    a_spec = pl.BlockSpec((tk, tm), lambda i, j, kk: (kk, i)) if mode == "tn" else pl.BlockSpec((tm, tk), lambda i, j, kk: (i, kk))
    b_spec = pl.BlockSpec((tn, tk), lambda i, j, kk: (j, kk)) if mode == "nt" else pl.BlockSpec((tk, tn), lambda i, j, kk: (kk, j))
    if b_shards:
        b_spec = (pl.BlockSpec((None, tn, tk), lambda i, j, kk: (kk, j, 0)) if mode == "nt"
                  else pl.BlockSpec((None, tk, tn), lambda i, j, kk: (j, kk, 0)))
    o_spec = pl.BlockSpec((tm, tn), lambda i, j, kk: (i, j))
    out_shape = jax.ShapeDtypeStruct((m, n), out_dtype)
    if out_shards:
        assert add is None
        o_spec = pl.BlockSpec((None, tm, tn), lambda i, j, kk: (j, i, 0))
        out_shape = jax.ShapeDtypeStruct((out_shards, m, tn), out_dtype)
    has_add = add is not None

    n_extra = int(has_add) + int(after is not None)
    n_rows, n_vecs, n_outs = len(epi_rows), len(epi_vecs), len(epi_outs)

    def body(a_ref, b_ref, *rest):
        add_ref = rest[0] if has_add else None
        o_ref = rest[n_extra]

        def finish(r):
            if has_add:
                r = r + add_ref[...]
            if epi is None:
                o_ref[...] = r.astype(out_dtype)
            else:
                e = rest[n_extra:]
                epi_fn(r, e[:n_rows], e[n_rows:n_rows + n_vecs], e[n_rows + n_vecs:n_rows + n_vecs + n_outs],
                       pl.program_id(0) == 0)

        if nk == 1:
            finish(_dot(a_ref[...], b_ref[...], dn))
            return
        acc = rest[-1]
        kk = pl.program_id(2)

        @pl.when(kk == 0)
        def _():
            acc[...] = jnp.zeros_like(acc)

        acc[...] += _dot(a_ref[...], b_ref[...], dn)

        @pl.when(kk == nk - 1)
        def _():
            finish(acc[...])

    ins = [a, b] + ([add] if has_add else []) + ([after] if after is not None else [])
    in_specs = [a_spec, b_spec] + ([o_spec] if has_add else []) + ([ANY] if after is not None else [])
    sem0 = "parallel"
    if epi is not None:
        vec_spec = pl.BlockSpec((1, tn), lambda i, j, kk: (0, 0))
        ins += list(epi_rows) + list(epi_vecs)
        in_specs += [o_spec] * n_rows + [vec_spec] * n_vecs
        o_spec, out_shape = [], []
        for o in epi_outs:
            if o[0] == "row":
                o_spec.append(pl.BlockSpec((tm, tn), lambda i, j, kk: (i, j)))
                out_shape.append(jax.ShapeDtypeStruct((m, n), o[1]))
            elif o[0] == "vec":
                o_spec.append(vec_spec)
                out_shape.append(jax.ShapeDtypeStruct((1, n), f32))
                sem0 = "arbitrary"
            else:
                o_spec.append(pl.BlockSpec((8, 128), lambda i, j, kk: (0, 0)))
                out_shape.append(jax.ShapeDtypeStruct((8, 128), f32))
                sem0 = "arbitrary"
    return pl.pallas_call(
        body, name=name, grid=(m // tm, n // tn, nk), in_specs=in_specs, out_specs=o_spec, out_shape=out_shape,
        scratch_shapes=[pltpu.VMEM((tm, tn), f32)] if nk > 1 else [],
        compiler_params=pltpu.CompilerParams(dimension_semantics=(sem0, sem0, "arbitrary")),
    )(*ins)


def rms_fwd(x, w, *, name):
    t, d = x.shape
    tr = _pick(t, 512, 8)

    def body(x_ref, w_ref, o_ref):
        xv = x_ref[...]
        r = lax.rsqrt(jnp.mean(xv * xv, axis=-1, keepdims=True) + RMS_EPS)
        o_ref[...] = (xv * r * w_ref[...]).astype(bf16)

    return pl.pallas_call(
        body, name=name, grid=(t // tr,),
        in_specs=[pl.BlockSpec((tr, d), lambda i: (i, 0)), pl.BlockSpec((1, d), lambda i: (0, 0))],
        out_specs=pl.BlockSpec((tr, d), lambda i: (i, 0)), out_shape=jax.ShapeDtypeStruct((t, d), bf16),
    )(x, w)


def _rms_bwd_math(xv, wv, dy):
    r = lax.rsqrt(jnp.mean(xv * xv, axis=-1, keepdims=True) + RMS_EPS)
    g = dy * wv
    dx = r * g - xv * (r * r * r) * jnp.mean(g * xv, axis=-1, keepdims=True)
    dw = jnp.sum(dy * xv * r, axis=0, keepdims=True)
    return dx, dw


def epi_rms_fwd(r, rows, vecs, outs, first):
    outs[0][...] = r
    rr = lax.rsqrt(jnp.mean(r * r, axis=-1, keepdims=True) + RMS_EPS)
    outs[1][...] = (r * rr * vecs[0][...]).astype(bf16)


def epi_rms_bwd(r, rows, vecs, outs, first):
    dx, dw = _rms_bwd_math(rows[0][...], vecs[0][...], r)
    dx = dx + rows[1][...]
    outs[0][...] = dx
    if len(outs) == 3:
        outs[1][...] = dx.astype(bf16)
    dw_ref = outs[-1]

    @pl.when(first)
    def _():
        dw_ref[...] = jnp.zeros_like(dw_ref)

    dw_ref[...] += dw


def epi_loss(r, rows, vecs, outs, first):
    wv = vecs[0][...]
    rr = lax.rsqrt(jnp.mean(r * r, axis=-1, keepdims=True) + RMS_EPS)
    err = r * rr * wv - rows[0][...]
    part = 0.5 * jnp.sum(jnp.mean(err * err, axis=-1, keepdims=True), axis=0, keepdims=True)
    dx, dw = _rms_bwd_math(r, wv, err * (1.0 / r.shape[-1]))
    outs[0][...] = dx
    outs[1][...] = dx.astype(bf16)

    @pl.when(first)
    def _():
        outs[2][...] = jnp.zeros_like(outs[2])
        outs[3][...] = jnp.zeros_like(outs[3])

    outs[2][...] += dw
    outs[3][...] += part


CONV_ROWS = 1024
VREG_ELEMS = 8 * 128


def _conv_chunk(tc):
    return 16 if (16 + 8) * tc * 3 > 48 * VREG_ELEMS else 32


def conv_fwd(src, col0, width, w, b, *, silu, name):
    t = src.shape[0]
    tc = _pick(math.gcd(width, col0), 768)
    assert col0 % tc == 0
    cb = col0 // tc
    r = CONV_ROWS
    ch = _conv_chunk(tc)

    def body(u_ref, w_ref, b_ref, *rest):
        ext = rest[-1]
        j = pl.program_id(1)

        @pl.when(j == 0)
        def _():
            ext[0:8, :] = jnp.zeros((8, tc), f32)

        @pl.when(j > 0)
        def _():
            ext[0:8, :] = ext[r:r + 8, :]

        ext[8:r + 8, :] = u_ref[...]
        wv = w_ref[...]
        bv = b_ref[...]

        def chunk(c, carry):
            r0 = pl.multiple_of(c * ch, ch)
            v = ext[pl.ds(r0, ch + 8), :]
            acc = bv + wv[3:4, :] * v[8:, :]
            for s in (1, 2, 3):
                acc = acc + wv[3 - s:4 - s, :] * pltpu.roll(v, s, 0)[8:, :]
            rest[0][pl.ds(r0, ch), :] = acc
            if silu:
                rest[1][pl.ds(r0, ch), :] = _silu(acc)
            return carry

        lax.fori_loop(0, r // ch, chunk, 0)

    tile = pl.BlockSpec((r, tc), lambda c, j: (j, c))
    n_out = 2 if silu else 1
    return pl.pallas_call(
        body, name=name, grid=(width // tc, t // r),
        in_specs=[pl.BlockSpec((r, tc), lambda c, j: (j, cb + c)), pl.BlockSpec((4, tc), lambda c, j: (0, c)),
                  pl.BlockSpec((1, tc), lambda c, j: (0, c))],
        out_specs=[tile] * n_out, out_shape=[jax.ShapeDtypeStruct((t, width), f32)] * n_out,
        scratch_shapes=[pltpu.VMEM((r + 8, tc), f32)],
        compiler_params=pltpu.CompilerParams(dimension_semantics=("parallel", "arbitrary")),
    )(src, w, b)


def conv_bwd(dpost, pre, src, col0, w, dst, *, name):
    t, width = dpost.shape
    tc = _pick(math.gcd(width, col0), 768)
    assert col0 % tc == 0
    cb = col0 // tc
    r = CONV_ROWS
    ch = _conv_chunk(tc)
    nt = t // r
    has_pre = pre is not None

    def body(*refs):
        refs = refs[1:]
        if has_pre:
            d_ref, p_ref, u_ref, w_ref, du_ref, dw_ref, db_ref, ext = refs
        else:
            d_ref, u_ref, w_ref, du_ref, dw_ref, db_ref, ext = refs
        j = pl.program_id(1)

        @pl.when(j == 0)
        def _():
            ext[r:r + 8, :] = jnp.zeros((8, tc), f32)
            dw_ref[...] = jnp.zeros_like(dw_ref)
            db_ref[...] = jnp.zeros_like(db_ref)

        @pl.when(j > 0)
        def _():
            ext[r:r + 8, :] = ext[0:8, :]

        dpre = d_ref[...]
        if has_pre:
            dpre = dpre * _dsilu(p_ref[...])
        ext[0:r, :] = dpre
        wv = w_ref[...]

        def fold(p):
            out = p[0:8, :]
            for i in range(1, ch // 8):
                out = out + p[8 * i:8 * i + 8, :]
            return out

        def chunk(c, sums):
            r0 = pl.multiple_of(c * ch, ch)
            v = ext[pl.ds(r0, ch + 8), :]
            uv = u_ref[pl.ds(r0, ch), :]
            d0 = v[0:ch, :]
            du = wv[3:4, :] * d0
            new = [None] * 5
            new[3] = sums[3] + fold(d0 * uv)
            for s in (1, 2, 3):
                sh = pltpu.roll(v, ch + 8 - s, 0)[0:ch, :]
                du = du + wv[3 - s:4 - s, :] * sh
                new[3 - s] = sums[3 - s] + fold(sh * uv)
            new[4] = sums[4] + fold(d0)
            du_ref[pl.ds(r0, ch), :] = du.astype(bf16)
            return tuple(new)

        sums = lax.fori_loop(0, r // ch, chunk, tuple(jnp.zeros((8, tc), f32) for _ in range(5)))
        for k in range(4):
            dw_ref[k:k + 1, :] += jnp.sum(sums[k], axis=0, keepdims=True)
        db_ref[...] += jnp.sum(sums[4], axis=0, keepdims=True)

    rev = pl.BlockSpec((r, tc), lambda c, j: (nt - 1 - j, c))
    win = pl.BlockSpec((r, tc), lambda c, j: (nt - 1 - j, cb + c))
    in_specs = [ANY, rev] + ([rev] if has_pre else []) + [win, pl.BlockSpec((4, tc), lambda c, j: (0, c))]
    ins = [dst, dpost] + ([pre] if has_pre else []) + [src, w]
    return pl.pallas_call(
        body, name=name, grid=(width // tc, nt), in_specs=in_specs,
        out_specs=[win, pl.BlockSpec((4, tc), lambda c, j: (0, c)), pl.BlockSpec((1, tc), lambda c, j: (0, c))],
        out_shape=[jax.ShapeDtypeStruct(dst.shape, bf16), jax.ShapeDtypeStruct((4, width), f32),
                   jax.ShapeDtypeStruct((1, width), f32)],
        input_output_aliases={0: 0},
        scratch_shapes=[pltpu.VMEM((r + 8, tc), f32)],
        compiler_params=pltpu.CompilerParams(dimension_semantics=("parallel", "arbitrary")),
    )(*ins)


def _ssd_common(xbc_ref, dtr_ref, dtrT_ref, par_row_ref, par_col_ref):
    l = SSM_CHUNK
    x = xbc_ref[:, 0:SSM_GROUP_W]
    bm = xbc_ref[:, SSM_GROUP_W:SSM_GROUP_W + SSM_D_STATE]
    cm = xbc_ref[:, SSM_GROUP_W + SSM_D_STATE:XBC_GROUP_W]
    par_row = par_row_ref[0]
    par_col = par_col_ref[0]
    bias_row, alog_row = par_row[0:1, :], par_row[1:2, :]
    bias_col, alog_col = par_col[:, 0:1], par_col[:, 1:2]
    dtr = dtr_ref[0]
    dt = _softplus(dtr + bias_row)
    dt_t = _softplus(dtrT_ref[0] + bias_col)
    a_row = -jnp.exp(alog_row)
    a_col = -jnp.exp(alog_col)
    li = lax.broadcasted_iota(jnp.int32, (l, l), 0)
    si = lax.broadcasted_iota(jnp.int32, (l, l), 1)
    tri = (li >= si).astype(f32)
    cs = _dot_01(tri, dt * a_row, NN, 1, 3)
    cs_t = _dot_01(dt_t * a_col, tri, NT, 0, 3)
    off = lax.broadcasted_iota(jnp.int32, (SSM_HPG, SSM_GROUP_W), 1) - SSM_HEAD_DIM * lax.broadcasted_iota(
        jnp.int32, (SSM_HPG, SSM_GROUP_W), 0)
    ex = ((off >= 0) & (off < SSM_HEAD_DIM)).astype(f32)
    cs_x = _dot_01(cs, ex, NN, 0, 3)
    cl_x = cs_x[l - 1:l, :]
    return dict(x=x, bm=bm, cm=cm, dtr=dtr, dt=dt, a_row=a_row, bias_row=bias_row, tri=tri, li=li, si=si, cs=cs,
                cs_t=cs_t, ex=ex, dt_x=_dot_01(dt, ex, NN, 0, 2), d_x=_dot_01(par_row, ex, NN, 0, 2)[2:3, :], e_x=jnp.exp(cs_x),
                el_x=jnp.exp(cl_x), dec_x=jnp.exp(cl_x - cs_x))


def ssd_fwd(xbc, dtr, dtr_t, par_row, par_col, *, name):
    t = xbc.shape[0]
    nc = t // SSM_CHUNK
    l, p = SSM_CHUNK, SSM_HEAD_DIM

    def body(xbc_ref, dtr_ref, dtrT_ref, prow_ref, pcol_ref, y_ref, sin_ref, state):
        @pl.when(pl.program_id(1) == 0)
        def _():
            state[...] = jnp.zeros_like(state)

        q = _ssd_common(xbc_ref, dtr_ref, dtrT_ref, prow_ref, pcol_ref)
        st = state[...]
        sin_ref[0] = st
        xd = q["x"] * q["dt_x"]
        g = _dot(q["cm"], q["bm"], NT)
        for r in range(SSM_HPG):
            sl = slice(r * p, (r + 1) * p)
            diff = q["cs"][:, r:r + 1] - q["cs_t"][r:r + 1, :]
            lm = jnp.where(q["li"] >= q["si"], jnp.exp(jnp.minimum(diff, 0.0)), 0.0)
            y_ref[:, sl] = _dot(g * lm, xd[:, sl], NN)
        y_ref[...] += q["e_x"] * _dot(q["cm"], st, NN) + q["d_x"] * q["x"]
        state[...] = q["el_x"] * st + _dot(q["bm"].T, xd * q["dec_x"], NN)

    return pl.pallas_call(
        body, name=name, grid=(SSM_GROUPS, nc),
        in_specs=[pl.BlockSpec((l, XBC_GROUP_W), lambda g, c: (c, g)),
                  pl.BlockSpec((1, l, SSM_HPG), lambda g, c: (g, c, 0)),
                  pl.BlockSpec((1, SSM_HPG, l), lambda g, c: (g, 0, c)),
                  pl.BlockSpec((1, 8, 8), lambda g, c: (g, 0, 0)),
                  pl.BlockSpec((1, 8, 8), lambda g, c: (g, 0, 0))],
        out_specs=[pl.BlockSpec((l, SSM_GROUP_W), lambda g, c: (c, g)),
                   pl.BlockSpec((1, SSM_D_STATE, SSM_GROUP_W), lambda g, c: (c, 0, g))],
        out_shape=[jax.ShapeDtypeStruct((t, SSM_D_INNER), f32),
                   jax.ShapeDtypeStruct((nc, SSM_D_STATE, SSM_D_INNER), f32)],
        scratch_shapes=[pltpu.VMEM((SSM_D_STATE, SSM_GROUP_W), f32)],
        compiler_params=pltpu.CompilerParams(dimension_semantics=("parallel", "arbitrary")),
    )(xbc, dtr, dtr_t, par_row, par_col)


def ssd_bwd(xbc, dtr, dtr_t, par_row, par_col, s_in, dy, *, name):
    t = xbc.shape[0]
    nc = t // SSM_CHUNK
    l, p = SSM_CHUNK, SSM_HEAD_DIM

    def body(xbc_ref, dtr_ref, dtrT_ref, prow_ref, pcol_ref, sin_ref, dy_ref, dxbc_ref, ddtr_ref, dpar_ref,
             dstate, yd_buf, dxd_buf):
        @pl.when(pl.program_id(1) == 0)
        def _():
            dstate[...] = jnp.zeros_like(dstate)
            dpar_ref[...] = jnp.zeros_like(dpar_ref)

        q = _ssd_common(xbc_ref, dtr_ref, dtrT_ref, prow_ref, pcol_ref)
        x, bm, cm, ex, li, si = q["x"], q["bm"], q["cm"], q["ex"], q["li"], q["si"]
        e_x, el_x, dec_x = q["e_x"], q["el_x"], q["dec_x"]
        st = sin_ref[0]
        dst = dstate[...]
        dy = dy_ref[...]
        xd = x * q["dt_x"]
        g = _dot(cm, bm, NT)
        dg = jnp.zeros((l, l), f32)
        for r in range(SSM_HPG):
            sl = slice(r * p, (r + 1) * p)
            diff = q["cs"][:, r:r + 1] - q["cs_t"][r:r + 1, :]
            lm = jnp.where(li >= si, jnp.exp(jnp.minimum(diff, 0.0)), 0.0)
            m = (g * lm).astype(bf16)
            xdh, dyh = xd[:, sl].astype(bf16), dy[:, sl].astype(bf16)
            yd_buf[:, sl] = _dot(m, xdh, NN)
            dxd_buf[:, sl] = _dot(m, dyh, TN)
            dg = dg + _dot(dyh, xdh, NT) * lm
        yd, dxd_diag = yd_buf[...], dxd_buf[...]
        yo = e_x * _dot(cm, st, NN)
        dz = e_x * dy
        wv = _dot(bm, dst, NN)
        xw = xd * wv * dec_x
        row8 = lax.broadcasted_iota(jnp.int32, (l, SSM_HPG), 0)
        dy_b, xd_b = dy.astype(bf16).astype(f32), xd.astype(bf16).astype(f32)
        dcs = _dot_01(dy_b * yd - xd_b * dxd_diag + dy * yo - xw, ex, NT, 0, 3)
        tail = jnp.sum(xw, axis=0, keepdims=True) + el_x * jnp.sum(dst * st, axis=0, keepdims=True)
        dcl = _dot_01(jnp.broadcast_to(tail, (SSM_HPG, SSM_GROUP_W)), ex, NT, 0, 3)[0:1, :]
        dcs = dcs + jnp.where(row8 == l - 1, dcl, 0.0)
        dda = _dot_01(q["tri"], dcs, TN, 1, 3)
        dxd = dxd_diag + dec_x * wv
        ddt = _dot_01(dxd * x, ex, NT, 0, 3) + dda * q["a_row"]
        ddtr = ddt * _sigmoid(q["dtr"] + q["bias_row"])
        ddtr_ref[0] = ddtr
        dd = _dot_01(jnp.broadcast_to(jnp.sum(dy * x, axis=0, keepdims=True), (SSM_HPG, SSM_GROUP_W)), ex, NT, 0, 2)[0:1, :]
        dpar_ref[0, 0:1, :] += jnp.sum(ddtr, axis=0, keepdims=True)
        dpar_ref[0, 1:2, :] += jnp.sum(dda * q["dt"], axis=0, keepdims=True) * q["a_row"]
        dpar_ref[0, 2:3, :] += dd
        dxbc_ref[:, 0:SSM_GROUP_W] = dxd * q["dt_x"] + q["d_x"] * dy
        dxbc_ref[:, SSM_GROUP_W:SSM_GROUP_W + SSM_D_STATE] = _dot(dg, cm, TN) + _dot(xd * dec_x, dst, NT)
        dxbc_ref[:, SSM_GROUP_W + SSM_D_STATE:XBC_GROUP_W] = _dot(dg, bm, NN) + _dot(dz, st, NT)
        dstate[...] = _dot(cm.T, dz, NN) + el_x * dst

    rc = lambda c: nc - 1 - c
    return pl.pallas_call(
        body, name=name, grid=(SSM_GROUPS, nc),
        in_specs=[pl.BlockSpec((l, XBC_GROUP_W), lambda g, c: (rc(c), g)),
                  pl.BlockSpec((1, l, SSM_HPG), lambda g, c: (g, rc(c), 0)),
                  pl.BlockSpec((1, SSM_HPG, l), lambda g, c: (g, 0, rc(c))),
                  pl.BlockSpec((1, 8, 8), lambda g, c: (g, 0, 0)),
                  pl.BlockSpec((1, 8, 8), lambda g, c: (g, 0, 0)),
                  pl.BlockSpec((1, SSM_D_STATE, SSM_GROUP_W), lambda g, c: (rc(c), 0, g)),
                  pl.BlockSpec((l, SSM_GROUP_W), lambda g, c: (rc(c), g))],
        out_specs=[pl.BlockSpec((l, XBC_GROUP_W), lambda g, c: (rc(c), g)),
                   pl.BlockSpec((1, l, SSM_HPG), lambda g, c: (g, rc(c), 0)),
                   pl.BlockSpec((1, 8, 8), lambda g, c: (g, 0, 0))],
        out_shape=[jax.ShapeDtypeStruct((t, SSM_CONV_DIM), f32),
                   jax.ShapeDtypeStruct((SSM_GROUPS, t, SSM_HPG), f32),
                   jax.ShapeDtypeStruct((SSM_GROUPS, 8, 8), f32)],
        scratch_shapes=[pltpu.VMEM((SSM_D_STATE, SSM_GROUP_W), f32), pltpu.VMEM((l, SSM_GROUP_W), f32),
                        pltpu.VMEM((l, SSM_GROUP_W), f32)],
        compiler_params=pltpu.CompilerParams(dimension_semantics=("parallel", "arbitrary")),
    )(xbc, dtr, dtr_t, par_row, par_col, s_in, dy)


def gnorm_fwd(y, proj, w, *, name):
    t = y.shape[0]
    tr = _pick(t, 2048, 8)
    gw = SSM_GROUP_W
    zb = OFF_Z // gw

    def body(y_ref, z_ref, w_ref, o_ref):
        y2 = y_ref[...] * _silu(z_ref[...])
        r = lax.rsqrt(jnp.mean(y2 * y2, axis=-1, keepdims=True) + RMS_EPS)
        o_ref[...] = (y2 * r * w_ref[...]).astype(bf16)

    return pl.pallas_call(
        body, name=name, grid=(SSM_GROUPS, t // tr),
        in_specs=[pl.BlockSpec((tr, gw), lambda g, i: (i, g)), pl.BlockSpec((tr, gw), lambda g, i: (i, zb + g)),
                  pl.BlockSpec((1, gw), lambda g, i: (0, g))],
        out_specs=pl.BlockSpec((tr, gw), lambda g, i: (i, g)), out_shape=jax.ShapeDtypeStruct((t, SSM_D_INNER), bf16),
    )(y, proj, w)


def gnorm_bwd(y, proj, w, dout, dst, *, name):
    t = y.shape[0]
    tr = _pick(t, 2048, 8)
    gw = SSM_GROUP_W
    zb = OFF_Z // gw

    def body(_, y_ref, z_ref, w_ref, do_ref, dy_ref, dz_ref, dw_ref):
        yv, zv = y_ref[...], z_ref[...]
        sz = _silu(zv)
        y2 = yv * sz
        dy2, dw = _rms_bwd_math(y2, w_ref[...], do_ref[...].astype(f32))
        dy_ref[...] = dy2 * sz
        dz_ref[...] = (dy2 * yv * _dsilu(zv)).astype(bf16)

        @pl.when(pl.program_id(1) == 0)
        def _():
            dw_ref[...] = jnp.zeros_like(dw_ref)

        dw_ref[...] += dw

    tile = pl.BlockSpec((tr, gw), lambda g, i: (i, g))
    vec = pl.BlockSpec((1, gw), lambda g, i: (0, g))
    return pl.pallas_call(
        body, name=name, grid=(SSM_GROUPS, t // tr),
        in_specs=[ANY, tile, pl.BlockSpec((tr, gw), lambda g, i: (i, zb + g)), vec, tile],
        out_specs=[tile, pl.BlockSpec((tr, gw), lambda g, i: (i, zb + g)), vec],
        out_shape=[jax.ShapeDtypeStruct((t, SSM_D_INNER), f32), jax.ShapeDtypeStruct(dst.shape, bf16),
                   jax.ShapeDtypeStruct((1, SSM_D_INNER), f32)],
        input_output_aliases={0: 1},
        compiler_params=pltpu.CompilerParams(dimension_semantics=("parallel", "arbitrary")),
    )(dst, y, proj, w, dout)


LRU_ROWS = 2048


def _lru_gates(uv, wr_ref, wi_ref, br_ref, bi_ref, lam_ref):
    rg = _sigmoid(_dot(uv, wr_ref[0], NN) + br_ref[...])
    ig = _sigmoid(_dot(uv, wi_ref[0], NN) + bi_ref[...])
    sp = _softplus(-lam_ref[...])
    la = -LRU_C * rg * sp
    a = jnp.exp(la)
    s = jnp.sqrt(jnp.maximum(-_expm1(2.0 * la), 0.0))
    return rg, ig, sp, la, a, s


def lru_fwd(u, proj, w_r, b_r, w_i, b_i, lam, *, name):
    t = u.shape[0]
    r = LRU_ROWS
    lb = LRU_BLOCK
    yb = OFF_LY // lb

    def body(u_ref, y_ref, wr_ref, br_ref, wi_ref, bi_ref, lam_ref, h_ref, o_ref, carry):
        @pl.when(pl.program_id(1) == 0)
        def _():
            carry[...] = jnp.zeros_like(carry)

        uv = u_ref[...]
        _, ig, _, _, a, s = _lru_gates(uv, wr_ref, wi_ref, br_ref, bi_ref, lam_ref)
        b = s * ig * uv
        row = lax.broadcasted_iota(jnp.int32, (r, lb), 0)
        d = 1
        while d < r:
            keep = row >= d
            b = b + a * jnp.where(keep, pltpu.roll(b, d, 0), 0.0)
            a = a * jnp.where(keep, pltpu.roll(a, d, 0), 1.0)
            d *= 2
        h = b + a * carry[0:1, :]
        carry[0:1, :] = h[r - 1:r, :]
        h_ref[...] = h
        o_ref[...] = (h * _gelu(y_ref[...])).astype(bf16)

    tile = pl.BlockSpec((r, lb), lambda hb, j: (j, hb))
    vec = pl.BlockSpec((1, lb), lambda hb, j: (0, hb))
    wsp = pl.BlockSpec((1, lb, lb), lambda hb, j: (hb, 0, 0))
    return pl.pallas_call(
        body, name=name, grid=(LRU_BLOCKS, t // r),
        in_specs=[tile, pl.BlockSpec((r, lb), lambda hb, j: (j, yb + hb)), wsp, vec, wsp, vec, vec],
        out_specs=[tile, tile],
        out_shape=[jax.ShapeDtypeStruct((t, LRU_WIDTH), f32), jax.ShapeDtypeStruct((t, LRU_WIDTH), bf16)],
        scratch_shapes=[pltpu.VMEM((8, lb), f32)],
        compiler_params=pltpu.CompilerParams(dimension_semantics=("parallel", "arbitrary")),
    )(u, proj, w_r, b_r, w_i, b_i, lam)


def lru_bwd(u, proj, hseq, dout, w_r, b_r, w_i, b_i, lam, dst, *, name):
    t = u.shape[0]
    r = LRU_ROWS
    nt = t // r
    lb = LRU_BLOCK
    yb = OFF_LY // lb

    def body(_, u_ref, y_ref, h_ref, hp_ref, do_ref, wr_ref, br_ref, wi_ref, bi_ref, lam_ref,
             du_ref, dy_ref, dwr_ref, dwi_ref, dbr_ref, dbi_ref, dlam_ref, carry_dh, carry_a):
        j = pl.program_id(1)

        @pl.when(j == 0)
        def _():
            carry_dh[...] = jnp.zeros_like(carry_dh)
            carry_a[...] = jnp.zeros_like(carry_a)
            dwr_ref[...] = jnp.zeros_like(dwr_ref)
            dwi_ref[...] = jnp.zeros_like(dwi_ref)
            dbr_ref[...] = jnp.zeros_like(dbr_ref)
            dbi_ref[...] = jnp.zeros_like(dbi_ref)
            dlam_ref[...] = jnp.zeros_like(dlam_ref)

        uv = u_ref[...]
        yv = y_ref[...]
        hv = h_ref[...]
        dov = do_ref[...]
        rg, ig, sp, la, a, s = _lru_gates(uv, wr_ref, wi_ref, br_ref, bi_ref, lam_ref)
        dy_ref[...] = (dov * hv * _dgelu(yv)).astype(bf16)
        gq = dov * _gelu(yv)
        row = lax.broadcasted_iota(jnp.int32, (r, lb), 0)
        an = jnp.where(row < r - 1, pltpu.roll(a, r - 1, 0), carry_a[0:1, :])
        d = 1
        while d < r:
            keep = row < r - d
            gq = gq + an * jnp.where(keep, pltpu.roll(gq, r - d, 0), 0.0)
            an = an * jnp.where(keep, pltpu.roll(an, r - d, 0), 1.0)
            d *= 2
        dh = gq + an * carry_dh[0:1, :]
        carry_dh[0:1, :] = dh[0:1, :]
        carry_a[0:1, :] = a[0:1, :]
        first = jnp.where(j == nt - 1, 0.0, 1.0) * hp_ref[7:8, :]
        hprev = jnp.where(row >= 1, pltpu.roll(hv, 1, 0), first)
        da = dh * hprev
        iu = ig * uv
        e2 = jnp.exp(2.0 * la)
        dla = da * a - dh * iu * e2 / jnp.maximum(s, 1e-30)
        drp = dla * (-LRU_C * sp) * rg * (1.0 - rg)
        dip = dh * s * uv * ig * (1.0 - ig)
        dlam_ref[...] += jnp.sum(dla * (LRU_C * rg) * _sigmoid(-lam_ref[...]), axis=0, keepdims=True)
        du_ref[...] = dh * s * ig + _dot(drp, wr_ref[0], NT) + _dot(dip, wi_ref[0], NT)
        dwr_ref[0] += _dot(uv, drp, TN)
        dwi_ref[0] += _dot(uv, dip, TN)
        dbr_ref[...] += jnp.sum(drp, axis=0, keepdims=True)
        dbi_ref[...] += jnp.sum(dip, axis=0, keepdims=True)

    rj = lambda j: nt - 1 - j
    tile = pl.BlockSpec((r, lb), lambda hb, j: (rj(j), hb))
    vec = pl.BlockSpec((1, lb), lambda hb, j: (0, hb))
    wsp = pl.BlockSpec((1, lb, lb), lambda hb, j: (hb, 0, 0))
    hprev_spec = pl.BlockSpec((8, lb), lambda hb, j: (jnp.maximum(rj(j) * (r // 8) - 1, 0), hb))
    ywin = pl.BlockSpec((r, lb), lambda hb, j: (rj(j), yb + hb))
    return pl.pallas_call(
        body, name=name, grid=(LRU_BLOCKS, nt),
        in_specs=[ANY, tile, ywin, tile, hprev_spec, tile, wsp, vec, wsp, vec, vec],
        out_specs=[tile, ywin, wsp, wsp, vec, vec, vec],
        out_shape=[jax.ShapeDtypeStruct((t, LRU_WIDTH), f32), jax.ShapeDtypeStruct(dst.shape, bf16),
                   jax.ShapeDtypeStruct((LRU_BLOCKS, lb, lb), f32), jax.ShapeDtypeStruct((LRU_BLOCKS, lb, lb), f32),
                   jax.ShapeDtypeStruct((1, LRU_WIDTH), f32), jax.ShapeDtypeStruct((1, LRU_WIDTH), f32),
                   jax.ShapeDtypeStruct((1, LRU_WIDTH), f32)],
        input_output_aliases={0: 1},
        scratch_shapes=[pltpu.VMEM((8, lb), f32), pltpu.VMEM((8, lb), f32)],
        compiler_params=pltpu.CompilerParams(dimension_semantics=("parallel", "arbitrary")),
    )(dst, u, proj, hseq, hseq, dout, w_r, b_r, w_i, b_i, lam)


def merge_fwd(proj, bg, y_ssm, y_lru, *, name):
    t, d = y_ssm.shape
    tr = _pick(t, 512, 8)
    gb = OFF_GATES // d

    def body(gs_ref, gl_ref, bs_ref, bl_ref, ys_ref, yl_ref, o_ref):
        gs = _sigmoid(gs_ref[...] + bs_ref[...])
        gl = _sigmoid(gl_ref[...] + bl_ref[...])
        o_ref[...] = (gs * ys_ref[...].astype(f32) + gl * yl_ref[...].astype(f32)).astype(bf16)

    row = pl.BlockSpec((tr, d), lambda i: (i, 0))
    return pl.pallas_call(
        body, name=name, grid=(t // tr,),
        in_specs=[pl.BlockSpec((tr, d), lambda i: (i, gb)), pl.BlockSpec((tr, d), lambda i: (i, gb + 1)),
                  pl.BlockSpec((1, d), lambda i: (0, 0)), pl.BlockSpec((1, d), lambda i: (0, 1)), row, row],
        out_specs=row, out_shape=jax.ShapeDtypeStruct((t, d), bf16),
    )(proj, proj, bg, bg, y_ssm, y_lru)


def merge_bwd(proj, bg, y_ssm, y_lru, dmix, *, name):
    t, d = y_ssm.shape
    tr = _pick(t, 512, 8)
    gb = OFF_GATES // d

    def body(gs_ref, gl_ref, bs_ref, bl_ref, ys_ref, yl_ref, dm_ref, dg_ref, dys_ref, dyl_ref, dbg_ref):
        gs = _sigmoid(gs_ref[...] + bs_ref[...])
        gl = _sigmoid(gl_ref[...] + bl_ref[...])
        dm = dm_ref[...].astype(f32)
        dys_ref[...] = (dm * gs).astype(bf16)
        dyl_ref[...] = (dm * gl).astype(bf16)
        dgs = dm * ys_ref[...].astype(f32) * gs * (1.0 - gs)
        dgl = dm * yl_ref[...].astype(f32) * gl * (1.0 - gl)
        dg_ref[:, 0:d] = dgs.astype(bf16)
        dg_ref[:, d:2 * d] = dgl.astype(bf16)

        @pl.when(pl.program_id(0) == 0)
        def _():
            dbg_ref[...] = jnp.zeros_like(dbg_ref)

        dbg_ref[:, 0:d] += jnp.sum(dgs, axis=0, keepdims=True)
        dbg_ref[:, d:2 * d] += jnp.sum(dgl, axis=0, keepdims=True)

    row = pl.BlockSpec((tr, d), lambda i: (i, 0))
    return pl.pallas_call(
        body, name=name, grid=(t // tr,),
        in_specs=[pl.BlockSpec((tr, d), lambda i: (i, gb)), pl.BlockSpec((tr, d), lambda i: (i, gb + 1)),
                  pl.BlockSpec((1, d), lambda i: (0, 0)), pl.BlockSpec((1, d), lambda i: (0, 1)), row, row, row],
        out_specs=[pl.BlockSpec((tr, 2 * d), lambda i: (i, OFF_GATES // (2 * d))), row, row,
                   pl.BlockSpec((1, 2 * d), lambda i: (0, 0))],
        out_shape=[jax.ShapeDtypeStruct((t, PROJ_W), bf16), jax.ShapeDtypeStruct((t, d), bf16),
                   jax.ShapeDtypeStruct((t, d), bf16), jax.ShapeDtypeStruct((1, 2 * d), f32)],
        compiler_params=pltpu.CompilerParams(dimension_semantics=("arbitrary",)),
    )(proj, proj, bg, bg, y_ssm, y_lru, dmix)


def swiglu_fwd(ff, *, name):
    t = ff.shape[0]
    hd = FFN_HIDDEN
    tr = _pick(t, 512, 8)

    def body(f_ref, o_ref):
        o_ref[...] = (_silu(f_ref[:, 0:hd].astype(f32)) * f_ref[:, hd:2 * hd].astype(f32)).astype(bf16)

    return pl.pallas_call(
        body, name=name, grid=(t // tr,), in_specs=[pl.BlockSpec((tr, 2 * hd), lambda i: (i, 0))],
        out_specs=pl.BlockSpec((tr, hd), lambda i: (i, 0)), out_shape=jax.ShapeDtypeStruct((t, hd), bf16),
    )(ff)


def swiglu_bwd(ff, dact, *, name):
    t = ff.shape[0]
    hd = FFN_HIDDEN
    tr = _pick(t, 512, 8)

    def body(f_ref, d_ref, o_ref):
        gate, up, dv = f_ref[:, 0:hd].astype(f32), f_ref[:, hd:2 * hd].astype(f32), d_ref[...].astype(f32)
        o_ref[:, 0:hd] = (dv * up * _dsilu(gate)).astype(bf16)
        o_ref[:, hd:2 * hd] = (dv * _silu(gate)).astype(bf16)

    return pl.pallas_call(
        body, name=name, grid=(t // tr,),
        in_specs=[pl.BlockSpec((tr, 2 * hd), lambda i: (i, 0)), pl.BlockSpec((tr, hd), lambda i: (i, 0))],
        out_specs=pl.BlockSpec((tr, 2 * hd), lambda i: (i, 0)), out_shape=jax.ShapeDtypeStruct((t, 2 * hd), bf16),
    )(ff, dact)


def _adam_math(w, g, m, v):
    m = ADAM_B1 * m + (1.0 - ADAM_B1) * g
    v = ADAM_B2 * v + (1.0 - ADAM_B2) * (g * g)
    m_hat = m / (1.0 - ADAM_B1 ** ADAM_STEP)
    v_hat = v / (1.0 - ADAM_B2 ** ADAM_STEP)
    delta = -ADAM_LR * (m_hat / (jnp.sqrt(v_hat) + ADAM_EPS) + ADAM_WD * w)
    return delta, m, v


def _row_tile(rows, cols):
    cap = max(8, (1 << 19) // cols)
    return _pick(rows, cap, 8) if rows % 8 == 0 else rows


def adamw(w, g, m, v, *, name):
    rows, cols = w.shape
    tr = _row_tile(rows, cols)

    def body(w_ref, g_ref, m_ref, v_ref, d_ref, nm_ref, nv_ref):
        d, nm, nv = _adam_math(w_ref[...], g_ref[...], m_ref[...], v_ref[...])
        d_ref[...] = d
        nm_ref[...] = nm
        nv_ref[...] = nv

    tile = pl.BlockSpec((tr, cols), lambda i: (i, 0))
    return pl.pallas_call(
        body, name=name, grid=(rows // tr,), in_specs=[tile] * 4, out_specs=[tile] * 3,
        out_shape=[jax.ShapeDtypeStruct((rows, cols), f32)] * 3,
    )(w, g, m, v)


def adamw_many(ws, gs, ms, vs, *, name):
    n = len(ws)

    def body(*refs):
        for i in range(n):
            d, nm, nv = _adam_math(refs[i][...], refs[n + i][...], refs[2 * n + i][...], refs[3 * n + i][...])
            refs[4 * n + 3 * i][...] = d
            refs[4 * n + 3 * i + 1][...] = nm
            refs[4 * n + 3 * i + 2][...] = nv

    outs = pl.pallas_call(
        body, name=name, out_shape=[jax.ShapeDtypeStruct(w.shape, f32) for w in ws for _ in range(3)],
    )(*ws, *gs, *ms, *vs)
    return [tuple(outs[3 * i:3 * i + 3]) for i in range(n)]


def pair_add(dw, rbuf, idx, *, name):
    n, rows, cols = dw.shape
    hr = rows // 2
    tr = _row_tile(hr, cols)
    nrt = hr // tr

    def body(idx_ref, a_ref, b_ref, o_ref, own_ref):
        s = a_ref[...] + b_ref[...]
        o_ref[...] = s.astype(bf16)

        @pl.when(pl.program_id(1) == idx_ref[0])
        def _():
            own_ref[...] = s[0]

    return pl.pallas_call(
        body, name=name,
        grid_spec=pltpu.PrefetchScalarGridSpec(
            num_scalar_prefetch=1, grid=(nrt, n),
            in_specs=[pl.BlockSpec((1, tr, cols), lambda i, k, idx: (k, idx[1] * nrt + i, 0)),
                      pl.BlockSpec((1, tr, cols), lambda i, k, idx: (k, i, 0))],
            out_specs=[pl.BlockSpec((1, tr, cols), lambda i, k, idx: (k, i, 0)),
                       pl.BlockSpec((tr, cols), lambda i, k, idx: (i, 0))]),
        out_shape=[jax.ShapeDtypeStruct((n, hr, cols), bf16), jax.ShapeDtypeStruct((hr, cols), f32)],
    )(idx, dw, rbuf)


def chip_sum(own, rbuf, idx, *, name):
    hr, cols = own.shape
    tr = _row_tile(hr, cols)
    nrt = hr // tr

    def body(idx_ref, a_ref, b_ref, o_ref):
        o_ref[...] = ((a_ref[...] + b_ref[0].astype(f32)) + b_ref[1].astype(f32)) + b_ref[2].astype(f32)

    return pl.pallas_call(
        body, name=name,
        grid_spec=pltpu.PrefetchScalarGridSpec(
            num_scalar_prefetch=1, grid=(nrt,),
            in_specs=[pl.BlockSpec((tr, cols), lambda i, idx: (i, 0)),
                      pl.BlockSpec((3, tr, cols), lambda i, idx: (0, i, 0))],
            out_specs=pl.BlockSpec((tr, cols), lambda i, idx: (idx[1] * nrt + i, 0))),
        out_shape=jax.ShapeDtypeStruct((2 * hr, cols), f32),
    )(idx, own, rbuf)


def sum8(rbuf, *, name):
    n, rows, cols = rbuf.shape
    tr = _row_tile(rows, cols * n)

    def body(a_ref, o_ref):
        acc = a_ref[0]
        for k in range(1, n):
            acc = acc + a_ref[k]
        o_ref[...] = acc

    return pl.pallas_call(
        body, name=name, grid=(rows // tr,), in_specs=[pl.BlockSpec((n, tr, cols), lambda i: (0, i, 0))],
        out_specs=pl.BlockSpec((tr, cols), lambda i: (i, 0)), out_shape=jax.ShapeDtypeStruct((rows, cols), f32),
    )(rbuf)


def _coords():
    return lax.axis_index("x"), lax.axis_index("y"), lax.axis_index("c")


def _other_chips(x, y):
    return [(1 - x, y), (x, 1 - y), (1 - x, 1 - y)]


def gather_weights(shards, *, name):
    n = len(shards)
    halves = [s.shape[0] // 2 for s in shards]

    def body(*refs):
        ins, outs = refs[:n], refs[n:2 * n]
        send1, recv1, send2, recv2 = refs[2 * n:]
        x, y, c = _coords()
        me = 2 * x + y
        chips = _other_chips(x, y)
        sibling = (x, y, 1 - c)

        def half(i, k, hc):
            return outs[i].at[k, pl.ds(hc * halves[i], halves[i]), :]

        def ici(i, j):
            return pltpu.make_async_remote_copy(
                src_ref=ins[i].at[pl.ds(c * halves[i], halves[i]), :], dst_ref=half(i, me, c),
                send_sem=send1.at[i, j], recv_sem=recv1.at[i, j], device_id=(*chips[j], c), device_id_type=MESH)

        def landed(i, j):
            kj = 2 * chips[j][0] + chips[j][1]
            return pltpu.make_async_remote_copy(
                src_ref=half(i, kj, c), dst_ref=half(i, kj, c),
                send_sem=send2.at[i, j], recv_sem=recv1.at[i, j], device_id=sibling, device_id_type=MESH)

        def from_sibling(i, j):
            kj = 2 * chips[j][0] + chips[j][1]
            return pltpu.make_async_remote_copy(
                src_ref=half(i, kj, 1 - c), dst_ref=half(i, kj, 1 - c),
                send_sem=send2.at[i, j], recv_sem=recv2.at[i, j], device_id=sibling, device_id_type=MESH)

        def d2d(i, j):
            kj = 2 * chips[j][0] + chips[j][1]
            return pltpu.make_async_remote_copy(
                src_ref=half(i, kj, c), dst_ref=half(i, kj, c),
                send_sem=send2.at[i, j], recv_sem=recv2.at[i, j], device_id=sibling, device_id_type=MESH)

        for j in range(3):
            for i in range(n):
                ici(i, j).start()
        for j in range(3):
            for i in range(n):
                landed(i, j).wait_recv()
                d2d(i, j).start()
        for j in range(3):
            for i in range(n):
                from_sibling(i, j).wait_recv()
        for j in range(3):
            for i in range(n):
                ici(i, j).wait_send()
                d2d(i, j).wait_send()

    return pl.pallas_call(
        body, name=name, in_specs=[ANY] * n, out_specs=[ANY] * n,
        out_shape=[jax.ShapeDtypeStruct((N_CHIPS,) + s.shape, s.dtype) for s in shards],
        scratch_shapes=[pltpu.SemaphoreType.DMA((n, 3))] * 4,
    )(*shards)


def pair_exchange(grads, *, name):
    n = len(grads)
    halves = [g.shape[1] // 2 for g in grads]

    def body(*refs):
        ins, outs = refs[:n], refs[n:2 * n]
        send, recv = refs[2 * n:]
        x, y, c = _coords()
        cps = [pltpu.make_async_remote_copy(
            src_ref=ins[i].at[:, pl.ds((1 - c) * halves[i], halves[i]), :], dst_ref=outs[i],
            send_sem=send.at[i], recv_sem=recv.at[i], device_id=(x, y, 1 - c), device_id_type=MESH) for i in range(n)]
        for cp in cps:
            cp.start()
        for cp in cps:
            cp.wait()

    return pl.pallas_call(
        body, name=name, in_specs=[ANY] * n, out_specs=[ANY] * n,
        out_shape=[jax.ShapeDtypeStruct((N_CHIPS, g.shape[1] // 2, g.shape[2]), g.dtype) for g in grads],
        scratch_shapes=[pltpu.SemaphoreType.DMA((n,))] * 2,
    )(*grads)


def pair_gather(bufs, *, name):
    n = len(bufs)

    def body(*refs):
        ins, outs = refs[:n], refs[n:2 * n]
        send, recv = refs[2 * n:]
        x, y, c = _coords()
        cps = []
        for i in range(n):
            hr = ins[i].shape[0] // 2
            cps.append(pltpu.make_async_remote_copy(
                src_ref=ins[i].at[pl.ds(c * hr, hr), :], dst_ref=outs[i].at[pl.ds(c * hr, hr), :],
                send_sem=send.at[i], recv_sem=recv.at[i], device_id=(x, y, 1 - c), device_id_type=MESH))
        for cp in cps:
            cp.start()
        for i in range(n):
            hr = ins[i].shape[0] // 2
            pltpu.make_async_remote_copy(
                src_ref=ins[i].at[pl.ds((1 - c) * hr, hr), :], dst_ref=outs[i].at[pl.ds((1 - c) * hr, hr), :],
                send_sem=send.at[i], recv_sem=recv.at[i], device_id=(x, y, 1 - c), device_id_type=MESH).wait_recv()
        for cp in cps:
            cp.wait_send()

    return pl.pallas_call(
        body, name=name, in_specs=[ANY] * n, out_specs=[ANY] * n,
        out_shape=[jax.ShapeDtypeStruct(b.shape, b.dtype) for b in bufs],
        input_output_aliases={i: i for i in range(n)},
        scratch_shapes=[pltpu.SemaphoreType.DMA((n,))] * 2,
    )(*bufs)


def all_exchange(buf, *, name):
    rows, cols = buf.shape

    def body(in_ref, out_ref, send, recv):
        x, y, c = _coords()
        me = 4 * x + 2 * y + c
        cps = []
        for d in range(1, 8):
            px = 1 - x if d & 4 else x
            py = 1 - y if d & 2 else y
            pc = 1 - c if d & 1 else c
            cps.append(pltpu.make_async_remote_copy(
                src_ref=in_ref, dst_ref=out_ref.at[me], send_sem=send.at[d - 1], recv_sem=recv.at[d - 1],
                device_id=(px, py, pc), device_id_type=MESH))
        for cp in cps:
            cp.start()
        for d in range(1, 8):
            px = 1 - x if d & 4 else x
            py = 1 - y if d & 2 else y
            pc = 1 - c if d & 1 else c
            src = 4 * px + 2 * py + pc
            pltpu.make_async_remote_copy(
                src_ref=in_ref, dst_ref=out_ref.at[src], send_sem=send.at[d - 1], recv_sem=recv.at[d - 1],
                device_id=(px, py, pc), device_id_type=MESH).wait_recv()
        for cp in cps:
            cp.wait_send()

    return pl.pallas_call(
        body, name=name, in_specs=[ANY], out_specs=ANY,
        out_shape=jax.ShapeDtypeStruct((8, rows, cols), buf.dtype),
        scratch_shapes=[pltpu.SemaphoreType.DMA((7,)), pltpu.SemaphoreType.DMA((7,))],
    )(buf)


HBM = pl.BlockSpec(memory_space=pltpu.HBM)
SEM = pl.BlockSpec(memory_space=pltpu.SEMAPHORE)
EFFECT = pltpu.SideEffectType.DATAFLOW_SIDE_EFFECTING


def split_start(arrays, after, copies, sem_shape, *, name):
    na = len(arrays)

    def body(*refs):
        for cp in copies(refs[:na], refs[na + 1], refs[na + 2]):
            cp.start()
        refs[-1][...] = jnp.zeros((8, 128), f32)

    outs = pl.pallas_call(
        body, name=name,
        out_shape=(pltpu.SemaphoreType.DMA(sem_shape), pltpu.SemaphoreType.DMA(sem_shape),
                   *[pltpu.HBM(a.shape, a.dtype) for a in arrays], jax.ShapeDtypeStruct((8, 128), f32)),
        in_specs=[HBM] * na + [ANY], out_specs=(SEM, SEM, *[HBM] * na, pl.BlockSpec(memory_space=pltpu.VMEM)),
        input_output_aliases={i: 2 + i for i in range(na)},
        compiler_params=pltpu.CompilerParams(has_side_effects=EFFECT),
    )(*[pltpu.with_memory_space_constraint(a, pltpu.HBM) for a in arrays], after)
    return outs[0], outs[1], list(outs[2:2 + na]), outs[-1]


def split_wait(send, recv, arrays, after, copies, *, name):
    na = len(arrays)

    def body(*refs):
        for cp in copies(refs[:na], refs[na], refs[na + 1]):
            cp.wait_send()
            cp.wait_recv()

    outs = pl.pallas_call(
        body, name=name, out_shape=tuple(pltpu.HBM(a.shape, a.dtype) for a in arrays),
        in_specs=[HBM] * na + [SEM, SEM, ANY], out_specs=tuple([HBM] * na),
        input_output_aliases={i: i for i in range(na)},
        compiler_params=pltpu.CompilerParams(has_side_effects=EFFECT),
    )(*arrays, send, recv, after)
    return list(outs)


def gather_copies(n):
    def copies(refs, send, recv):
        x, y, c = _coords()
        me = 2 * x + y
        chips = _other_chips(x, y)
        return [pltpu.make_async_remote_copy(
            src_ref=refs[i], dst_ref=refs[n + i].at[me], send_sem=send.at[3 * i + j], recv_sem=recv.at[3 * i + j],
            device_id=(*chips[j], c), device_id_type=MESH) for j in range(3) for i in range(n)]
    return copies


def pair_copies(n):
    def copies(refs, send, recv):
        x, y, c = _coords()
        cps = []
        for i in range(n):
            hr = refs[i].shape[1] // 2
            cps.append(pltpu.make_async_remote_copy(
                src_ref=refs[i].at[:, pl.ds((1 - c) * hr, hr), :], dst_ref=refs[n + i], send_sem=send.at[i],
                recv_sem=recv.at[i], device_id=(x, y, 1 - c), device_id_type=MESH))
        return cps
    return copies


def all_copies():
    def copies(refs, send, recv):
        x, y, c = _coords()
        me = 4 * x + 2 * y + c
        cps = []
        for d in range(1, 8):
            peer = (1 - x if d & 4 else x, 1 - y if d & 2 else y, 1 - c if d & 1 else c)
            cps.append(pltpu.make_async_remote_copy(
                src_ref=refs[0], dst_ref=refs[1].at[me], send_sem=send.at[d - 1], recv_sem=recv.at[d - 1],
                device_id=peer, device_id_type=MESH))
        return cps
    return copies


def reduce_copies(n):
    def copies(refs, send, recv):
        x, y, c = _coords()
        chips = _other_chips(x, y)
        return [pltpu.make_async_remote_copy(
            src_ref=refs[i].at[2 * chips[j][0] + chips[j][1]], dst_ref=refs[n + i].at[j],
            send_sem=send.at[3 * i + j], recv_sem=recv.at[3 * i + j], device_id=(*chips[j], c), device_id_type=MESH)
            for j in range(3) for i in range(n)]
    return copies


def _pack(arrs):
    flat = []
    for a in arrs:
        v = a.reshape(-1).astype(f32)
        pad = (-v.shape[0]) % 128
        flat.append(jnp.pad(v, (0, pad)) if pad else v)
    v = jnp.concatenate(flat)
    rows = v.shape[0] // 128
    pad_rows = (-rows) % 256
    v = v.reshape(rows, 128)
    return jnp.pad(v, ((0, pad_rows), (0, 0))) if pad_rows else v


def _unpack(buf, shapes):
    out, row = [], 0
    for s in shapes:
        size = math.prod(s)
        rows = -(-size // 128)
        out.append(buf[row:row + rows].reshape(-1)[:size].reshape(s))
        row += rows
    return out


def _ref_of_perm():
    ref = np.arange(IN_PROJ_DIM)
    xbc = ref[4096:7168]
    xbc_p = [np.concatenate([xbc[g * 512:(g + 1) * 512], xbc[2048 + g * 128:2048 + (g + 1) * 128],
                             xbc[2560 + g * 128:2560 + (g + 1) * 128]]) for g in range(SSM_GROUPS)]
    return np.concatenate([ref[0:2048], ref[2048:4096], ref[7200:8480], ref[8480:9760], ref[7168:7200],
                           -np.ones(DT_PAD_W - SSM_HEADS, np.int64)] + xbc_p)


def _runs(vals):
    out, start = [], 0
    for i in range(1, len(vals) + 1):
        if i == len(vals) or not (vals[i] == vals[i - 1] + 1 or (vals[i] < 0 and vals[i - 1] < 0)):
            out.append((start, int(vals[start]), i - start))
            start = i
    return out


def _perm_in_from_shards(g):
    ref_of_perm = _ref_of_perm()
    sw = IN_PROJ_DIM // N_CHIPS
    parts = []
    for _, first, length in _runs(ref_of_perm):
        if first < 0:
            parts.append(jnp.zeros((g.shape[1], length), g.dtype))
            continue
        lo = first
        while lo < first + length:
            k = lo // sw
            hi = min(first + length, (k + 1) * sw)
            parts.append(g[k, :, lo - k * sw:hi - k * sw])
            lo = hi
    return jnp.concatenate(parts, axis=-1)


def _unperm_in_to_shards(w):
    ref_of_perm = _ref_of_perm()
    perm_of_ref = np.zeros(IN_PROJ_DIM, np.int64)
    perm_of_ref[ref_of_perm[ref_of_perm >= 0]] = np.nonzero(ref_of_perm >= 0)[0]
    sw = IN_PROJ_DIM // N_CHIPS
    shards = []
    for k in range(N_CHIPS):
        runs = _runs(perm_of_ref[k * sw:(k + 1) * sw])
        shards.append(jnp.concatenate([w[:, first:first + length] for _, first, length in runs], axis=-1))
    return jnp.stack(shards)


def _perm_xbc_cols(w):
    parts = []
    for g in range(SSM_GROUPS):
        parts += [w[..., g * 512:(g + 1) * 512], w[..., 2048 + g * 128:2048 + (g + 1) * 128],
                  w[..., 2560 + g * 128:2560 + (g + 1) * 128]]
    return jnp.concatenate(parts, axis=-1)


def _unperm_xbc_cols(w):
    xs = [w[..., g * XBC_GROUP_W:g * XBC_GROUP_W + 512] for g in range(SSM_GROUPS)]
    bs = [w[..., g * XBC_GROUP_W + 512:g * XBC_GROUP_W + 640] for g in range(SSM_GROUPS)]
    cs = [w[..., g * XBC_GROUP_W + 640:(g + 1) * XBC_GROUP_W] for g in range(SSM_GROUPS)]
    return jnp.concatenate(xs + bs + cs, axis=-1)


def _from_col_shards(w):
    n, r, c = w.shape
    return jnp.transpose(w, (1, 0, 2)).reshape(r, n * c)


def kernel(x, norm1_w, w_in, b_branch_gate, ssm_conv_w, ssm_conv_b, ssm_dt_bias, ssm_a_log, ssm_d, ssm_norm_w, w_out_ssm, lru_conv_w, lru_conv_b, lru_w_r, lru_b_r, lru_w_i, lru_b_i, lru_lambda, w_out_lru, w_out, norm2_w, w_ffn_in, w_ffn_out, norm_f_w, loss_target, m_norm1_w, m_w_in, m_b_branch_gate, m_ssm_conv_w, m_ssm_conv_b, m_ssm_dt_bias, m_ssm_a_log, m_ssm_d, m_ssm_norm_w, m_w_out_ssm, m_lru_conv_w, m_lru_conv_b, m_lru_w_r, m_lru_b_r, m_lru_w_i, m_lru_b_i, m_lru_lambda, m_w_out_lru, m_w_out, m_norm2_w, m_w_ffn_in, m_w_ffn_out, m_norm_f_w, v_norm1_w, v_w_in, v_b_branch_gate, v_ssm_conv_w, v_ssm_conv_b, v_ssm_dt_bias, v_ssm_a_log, v_ssm_d, v_ssm_norm_w, v_w_out_ssm, v_lru_conv_w, v_lru_conv_b, v_lru_w_r, v_lru_b_r, v_lru_w_i, v_lru_b_i, v_lru_lambda, v_w_out_lru, v_w_out, v_norm2_w, v_w_ffn_in, v_w_ffn_out, v_norm_f_w):
    xi, yi, ci = lax.axis_index("x"), lax.axis_index("y"), lax.axis_index("c")
    me = 2 * xi + yi
    idx = jnp.stack([me, ci]).astype(jnp.int32)
    x2 = x[0]
    tgt = loss_target[0]

    big_names = ["w_in", "w_out_ssm", "w_out_lru", "w_out", "w_ffn_in", "w_ffn_out"]
    big_w = dict(w_in=w_in[0], w_out_ssm=w_out_ssm[0], w_out_lru=w_out_lru[0], w_out=w_out[0], w_ffn_in=w_ffn_in[0],
                 w_ffn_out=w_ffn_out[0])
    big_m = dict(w_in=m_w_in[0], w_out_ssm=m_w_out_ssm[0], w_out_lru=m_w_out_lru[0], w_out=m_w_out[0],
                 w_ffn_in=m_w_ffn_in[0], w_ffn_out=m_w_ffn_out[0])
    big_v = dict(w_in=v_w_in[0], w_out_ssm=v_w_out_ssm[0], w_out_lru=v_w_out_lru[0], w_out=v_w_out[0],
                 w_ffn_in=v_w_ffn_in[0], w_ffn_out=v_w_ffn_out[0])
    conv_pad = jnp.zeros((16, 768), f32).at[0:4, :].set(ssm_conv_w[0]).at[8:12, 0:320].set(lru_conv_w[0])
    mine = [big_w["w_in"].astype(bf16), conv_pad]
    gathered = gather_weights(mine, name="gather_weights")
    g_in, g_conv = [lax.dynamic_update_index_in_dim(g, s, me, 0) for g, s in zip(gathered, mine)]
    w_in_p = _perm_in_from_shards(g_in)
    late_names = big_names[1:]
    late = [big_w[k].astype(bf16) for k in late_names]
    late_lands = [lax.empty((N_CHIPS,) + s.shape, bf16) for s in late]
    g_send, g_recv, g_arrays, g_token = split_start(late + late_lands, g_conv, gather_copies(5), (15,),
                                                    name="gather_late_start")
    ssm_cw_full = _from_col_shards(g_conv[:, 0:4, :])
    lru_cw_full = _from_col_shards(g_conv[:, 8:12, 0:320])
    ssm_cw_p = _perm_xbc_cols(ssm_cw_full)
    ssm_cb_p = _perm_xbc_cols(ssm_conv_b)

    par = jnp.stack([ssm_dt_bias[0], ssm_a_log[0], ssm_d[0]], axis=0).reshape(3, SSM_GROUPS, SSM_HPG)
    par_row = jnp.zeros((SSM_GROUPS, 8, 8), f32).at[:, 0:3, :].set(jnp.transpose(par, (1, 0, 2)))
    par_col = jnp.transpose(par_row, (0, 2, 1))

    hn1 = rms_fwd(x2, norm1_w + g_token[0:1, 0:1], name="rms1_fwd")
    proj = mm(hn1, w_in_p, "nn", name="in_proj")
    t = x2.shape[0]
    dtr = jnp.transpose(proj[:, OFF_DT:OFF_DT + 32].reshape(t, SSM_GROUPS, SSM_HPG), (1, 0, 2))
    dtr_t = jnp.transpose(dtr, (0, 2, 1))
    xbc_pre, xbc_post = conv_fwd(proj, OFF_XBC, SSM_CONV_DIM, ssm_cw_p, ssm_cb_p, silu=True, name="ssm_conv_fwd")
    y_ssd, s_in = ssd_fwd(xbc_post, dtr, dtr_t, par_row, par_col, name="ssd_fwd")
    yn = gnorm_fwd(y_ssd, proj, ssm_norm_w, name="gnorm_fwd")
    g_arrays = split_wait(g_send, g_recv, g_arrays, yn, gather_copies(5), name="gather_late_wait")
    g_out_ssm, g_out_lru, g_out, g_ffn_in, g_ffn_out = [
        lax.dynamic_update_index_in_dim(g, s, me, 0) for g, s in zip(g_arrays[5:], late)]
    w_out_ssm_f = g_out_ssm.reshape(SSM_D_INNER, D_MODEL)
    w_out_lru_f = g_out_lru.reshape(LRU_WIDTH, D_MODEL)
    w_out_f = g_out.reshape(D_MODEL, D_MODEL)
    w_ffn_out_f = g_ffn_out.reshape(FFN_HIDDEN, D_MODEL)
    y_ssm = mm(yn, w_out_ssm_f, "nn", out_dtype=bf16, name="out_ssm")
    (u_lru,) = conv_fwd(proj, OFF_LX, LRU_WIDTH, lru_cw_full, lru_conv_b, silu=False, name="lru_conv_fwd")
    h_lru, o_lru = lru_fwd(u_lru, proj, lru_w_r[0], lru_b_r, lru_w_i[0], lru_b_i, lru_lambda, name="lru_fwd")
    y_lru = mm(o_lru, w_out_lru_f, "nn", out_dtype=bf16, name="out_lru")
    mix = merge_fwd(proj, b_branch_gate, y_ssm, y_lru, name="merge_fwd")
    h1, hn2 = mm(mix, w_out_f, "nn", add=x2, name="out_proj",
                 epi=(epi_rms_fwd, [], [norm2_w], [("row", f32), ("row", bf16)]))
    ff = mm(hn2, g_ffn_in, "nn", b_shards=True, out_dtype=bf16, name="ffn_in")
    act = swiglu_fwd(ff, name="swiglu_fwd")
    dh2, dh2_b, d_norm_f, loss_tile = mm(act, w_ffn_out_f, "nn", add=h1, name="ffn_out",
                                         epi=(epi_loss, [tgt], [norm_f_w.reshape(1, D_MODEL)],
                                              [("row", f32), ("row", bf16), ("vec",), ("tile",)]))

    d_w_ffn_out = mm(act, dh2_b, "tn", name="d_w_ffn_out")
    dact = mm(dh2_b, w_ffn_out_f, "nt", out_dtype=bf16, name="d_act")
    dff = swiglu_bwd(ff, dact, name="swiglu_bwd")
    d_w_ffn_in = mm(hn2, dff, "tn", out_shards=N_CHIPS, name="d_w_ffn_in")
    dh1, dh1_b, d_norm2 = mm(dff, g_ffn_in, "nt", b_shards=True, name="d_hn2",
                             epi=(epi_rms_bwd, [h1, dh2], [norm2_w], [("row", f32), ("row", bf16), ("vec",)]))
    d_w_out = mm(mix, dh1_b, "tn", name="d_w_out")
    dmix = mm(dh1_b, w_out_f, "nt", out_dtype=bf16, name="d_mix")
    dproj, dy_ssm, dy_lru, d_bg = merge_bwd(proj, b_branch_gate, y_ssm, y_lru, dmix, name="merge_bwd")
    d_w_out_ssm = mm(yn, dy_ssm, "tn", name="d_w_out_ssm")
    d_w_out_lru = mm(o_lru, dy_lru, "tn", name="d_w_out_lru")
    early_g = [d_w_out_ssm.reshape(N_CHIPS, 512, D_MODEL), d_w_out_lru.reshape(N_CHIPS, 320, D_MODEL),
               d_w_out.reshape(N_CHIPS, 256, D_MODEL), d_w_ffn_in, d_w_ffn_out.reshape(N_CHIPS, 704, D_MODEL)]
    p_lands = [lax.empty((N_CHIPS, g.shape[1] // 2, g.shape[2]), f32) for g in early_g]
    p_send, p_recv, p_arrays, p_token = split_start(early_g + p_lands, early_g[0], pair_copies(5), (5,),
                                                    name="pair_early_start")
    dyn = mm(dy_ssm, w_out_ssm_f, "nt", out_dtype=bf16, after=p_token, name="d_yn")
    dy_ssd, dproj, d_ssm_norm = gnorm_bwd(y_ssd, proj, ssm_norm_w, dyn, dproj, name="gnorm_bwd")
    p_arrays = split_wait(p_send, p_recv, p_arrays, dy_ssd, pair_copies(5), name="pair_early_wait")
    e_pairs = [pair_add(g, rb, idx, name="pair_add_" + k) for g, rb, k in zip(p_arrays[:5], p_arrays[5:], late_names)]
    e_lands = [lax.empty((3,) + p[0].shape[1:], bf16) for p in e_pairs]
    e_send, e_recv, e_arrays, e_token = split_start([p[0] for p in e_pairs] + e_lands, e_pairs[0][1], reduce_copies(5),
                                                    (15,), name="reduce_early_start")
    dxbc_post, ddtr, dpar = ssd_bwd(xbc_post, dtr, dtr_t, par_row + e_token[0:1, 0:1], par_col, s_in, dy_ssd,
                                    name="ssd_bwd")
    dproj, d_ssm_cw_p, d_ssm_cb_p = conv_bwd(dxbc_post, xbc_pre, proj, OFF_XBC, ssm_cw_p, dproj, name="ssm_conv_bwd")
    do_lru = mm(dy_lru, w_out_lru_f, "nt", name="d_o_lru")
    du_lru, dproj, d_w_r, d_w_i, d_b_r, d_b_i, d_lam = lru_bwd(u_lru, proj, h_lru, do_lru, lru_w_r[0], lru_b_r, lru_w_i[0],
                                                               lru_b_i, lru_lambda, dproj, name="lru_bwd")
    dproj, d_lru_cw, d_lru_cb = conv_bwd(du_lru, None, proj, OFF_LX, lru_cw_full, dproj, name="lru_conv_bwd")
    ddt_cols = jnp.transpose(ddtr, (1, 0, 2)).reshape(t, SSM_HEADS).astype(bf16)
    ddt_cols = jnp.pad(ddt_cols, ((0, 0), (0, DT_PAD_W - SSM_HEADS)))
    dproj = lax.dynamic_update_slice(dproj, ddt_cols, (0, OFF_DT))

    d_ssm_cw = _unperm_xbc_cols(d_ssm_cw_p)
    d_ssm_cb = _unperm_xbc_cols(d_ssm_cb_p)
    dpar_h = jnp.transpose(dpar[:, 0:3, :], (1, 0, 2)).reshape(3, SSM_HEADS)
    small_names = ["norm1_w", "b_branch_gate", "ssm_conv_b", "ssm_dt_bias", "ssm_a_log", "ssm_d", "ssm_norm_w",
                   "lru_conv_b", "lru_w_r", "lru_b_r", "lru_w_i", "lru_b_i", "lru_lambda", "norm2_w", "norm_f_w"]
    small_g = dict(norm1_w=jnp.zeros_like(norm1_w), b_branch_gate=d_bg, ssm_conv_b=d_ssm_cb, ssm_dt_bias=dpar_h[0:1], ssm_a_log=dpar_h[1:2],
                   ssm_d=dpar_h[2:3], ssm_norm_w=d_ssm_norm, lru_conv_b=d_lru_cb, lru_w_r=d_w_r[None], lru_b_r=d_b_r,
                   lru_w_i=d_w_i[None], lru_b_i=d_b_i, lru_lambda=d_lam, norm2_w=d_norm2, norm_f_w=d_norm_f.reshape(D_MODEL))
    small_w = dict(norm1_w=norm1_w, b_branch_gate=b_branch_gate, ssm_conv_b=ssm_conv_b, ssm_dt_bias=ssm_dt_bias,
                   ssm_a_log=ssm_a_log, ssm_d=ssm_d, ssm_norm_w=ssm_norm_w, lru_conv_b=lru_conv_b, lru_w_r=lru_w_r,
                   lru_b_r=lru_b_r, lru_w_i=lru_w_i, lru_b_i=lru_b_i, lru_lambda=lru_lambda, norm2_w=norm2_w, norm_f_w=norm_f_w)
    small_m = dict(norm1_w=m_norm1_w, b_branch_gate=m_b_branch_gate, ssm_conv_b=m_ssm_conv_b, ssm_dt_bias=m_ssm_dt_bias,
                   ssm_a_log=m_ssm_a_log, ssm_d=m_ssm_d, ssm_norm_w=m_ssm_norm_w, lru_conv_b=m_lru_conv_b, lru_w_r=m_lru_w_r,
                   lru_b_r=m_lru_b_r, lru_w_i=m_lru_w_i, lru_b_i=m_lru_b_i, lru_lambda=m_lru_lambda, norm2_w=m_norm2_w,
                   norm_f_w=m_norm_f_w)
    small_v = dict(norm1_w=v_norm1_w, b_branch_gate=v_b_branch_gate, ssm_conv_b=v_ssm_conv_b, ssm_dt_bias=v_ssm_dt_bias,
                   ssm_a_log=v_ssm_a_log, ssm_d=v_ssm_d, ssm_norm_w=v_ssm_norm_w, lru_conv_b=v_lru_conv_b, lru_w_r=v_lru_w_r,
                   lru_b_r=v_lru_b_r, lru_w_i=v_lru_w_i, lru_b_i=v_lru_b_i, lru_lambda=v_lru_lambda, norm2_w=v_norm2_w,
                   norm_f_w=v_norm_f_w)
    shapes = [small_w[k].shape for k in small_names]
    conv_shapes = [(4, SSM_CONV_DIM), (4, LRU_WIDTH)]
    g_pack = _pack([small_g[k] for k in small_names] + [d_ssm_cw, d_lru_cw])
    s_send, s_recv, s_arrays, s_token = split_start([g_pack, lax.empty((8,) + g_pack.shape, f32)], g_pack, all_copies(),
                                                    (7,), name="small_start")
    d_w_in_p = mm(hn1, dproj, "tn", after=s_token, name="d_w_in")

    d_w_in_s = _unperm_in_to_shards(d_w_in_p)
    (l_sib,) = pair_exchange([d_w_in_s], name="pair_exchange_late")
    l_pair = pair_add(d_w_in_s, l_sib, idx, name="pair_add_w_in")
    l_land = lax.empty((3,) + l_pair[0].shape[1:], bf16)
    l_send, l_recv, l_arrays, l_token = split_start([l_pair[0], l_land], l_pair[1], reduce_copies(1), (3,),
                                                    name="reduce_late_start")
    grad_x, d_norm1 = mm(dproj, w_in_p, "nt", after=l_token, name="d_hn1",
                         epi=(epi_rms_bwd, [x2, dh1], [norm1_w], [("row", f32), ("vec",)]))

    e_arrays = split_wait(e_send, e_recv, e_arrays, d_norm1, reduce_copies(5), name="reduce_early_wait")
    e_half = [chip_sum(p[1], rb, idx, name="chip_sum_" + k) for p, rb, k in zip(e_pairs, e_arrays[5:], late_names)]
    big_out = {}
    for k, g in zip(late_names, pair_gather(e_half, name="pair_gather_early")):
        big_out[k] = (g,) + tuple(adamw(big_w[k], g, big_m[k], big_v[k], name="adamw_" + k))

    s_arrays = split_wait(s_send, s_recv, s_arrays, d_norm1, all_copies(), name="small_wait")
    g_sum = sum8(lax.dynamic_update_index_in_dim(s_arrays[1], g_pack, 2 * me + ci, 0), name="sum8")
    n1 = jnp.concatenate([d_norm1.reshape(8, 128), loss_tile], axis=0)
    n1_sum = sum8(lax.dynamic_update_index_in_dim(all_exchange(n1, name="all_exchange_norm1"), n1, 2 * me + ci, 0),
                  name="sum8_norm1")
    loss = n1_sum[8, 0]
    g_sum = lax.dynamic_update_slice(g_sum, n1_sum[0:8], (0, 0))
    g_small = _unpack(g_sum, shapes + conv_shapes)
    g_small[-2] = lax.dynamic_slice_in_dim(g_small[-2], me * 768, 768, axis=1)
    g_small[-1] = lax.dynamic_slice_in_dim(g_small[-1], me * 320, 320, axis=1)
    all_names = small_names + ["ssm_conv_w", "lru_conv_w"]
    small_w.update(ssm_conv_w=ssm_conv_w[0], lru_conv_w=lru_conv_w[0])
    small_m.update(ssm_conv_w=m_ssm_conv_w[0], lru_conv_w=m_lru_conv_w[0])
    small_v.update(ssm_conv_w=v_ssm_conv_w[0], lru_conv_w=v_lru_conv_w[0])
    as2d = lambda a: a.reshape(-1, a.shape[-1])
    upd = adamw_many([as2d(small_w[k]) for k in all_names], [as2d(g) for g in g_small],
                     [as2d(small_m[k]) for k in all_names], [as2d(small_v[k]) for k in all_names], name="adamw_small")
    small_out = {}
    for k, g, u in zip(all_names, g_small, upd):
        small_out[k] = (g,) + tuple(o.reshape(g.shape) for o in u)
    l_arrays = split_wait(l_send, l_recv, l_arrays, upd[0][0], reduce_copies(1), name="reduce_late_wait")
    l_half = chip_sum(l_pair[1], l_arrays[1], idx, name="chip_sum_w_in")
    (g_w_in,) = pair_gather([l_half], name="pair_gather_late")
    big_out["w_in"] = (g_w_in,) + tuple(adamw(big_w["w_in"], g_w_in, big_m["w_in"], big_v["w_in"], name="adamw_w_in"))

    order = ["norm1_w", "w_in", "b_branch_gate", "ssm_conv_w", "ssm_conv_b", "ssm_dt_bias", "ssm_a_log", "ssm_d", "ssm_norm_w",
             "w_out_ssm", "lru_conv_w", "lru_conv_b", "lru_w_r", "lru_b_r", "lru_w_i", "lru_b_i", "lru_lambda", "w_out_lru",
             "w_out", "norm2_w", "w_ffn_in", "w_ffn_out", "norm_f_w"]
    outs = [loss, grad_x[None]]
    for which in range(4):
        for k in order:
            if k in big_out:
                outs.append(big_out[k][which][None])
            elif k in ("ssm_conv_w", "lru_conv_w"):
                outs.append(small_out[k][which][None])
            else:
                outs.append(small_out[k][which])
    return tuple(outs)
```

```python
import math

import jax
import jax.numpy as jnp
import numpy as np
from jax import lax
from jax.experimental import pallas as pl
from jax.experimental.pallas import tpu as pltpu

f32 = jnp.float32
bf16 = jnp.bfloat16

D_MODEL = 1024
SSM_D_INNER = 2048
SSM_HEADS = 32
SSM_HEAD_DIM = 64
SSM_GROUPS = 4
SSM_HPG = 8
SSM_D_STATE = 128
SSM_CHUNK = 128
SSM_GROUP_W = 512
SSM_CONV_DIM = 3072
XBC_GROUP_W = 768
LRU_WIDTH = 1280
LRU_BLOCKS = 10
LRU_BLOCK = 128
LRU_C = 8.0
FFN_HIDDEN = 2816
RMS_EPS = 1e-6
IN_PROJ_DIM = 9760
N_CHIPS = 4

OFF_GATES = 0
OFF_Z = 2048
OFF_LX = 4096
OFF_LY = 5376
OFF_DT = 6656
DT_PAD_W = 256
OFF_XBC = 6912
PROJ_W = 9984

ADAM_LR = 0.001
ADAM_B1 = 0.9
ADAM_B2 = 0.999
ADAM_EPS = 1e-08
ADAM_WD = 0.01
ADAM_STEP = 10

MESH = pl.DeviceIdType.MESH
ANY = pl.BlockSpec(memory_space=pl.ANY)

NN = (((1,), (0,)), ((), ()))
NT = (((1,), (1,)), ((), ()))
TN = (((0,), (0,)), ((), ()))


def _pick(n, cap, mult=128):
    best = None
    for t in range(mult, min(n, cap) + 1, mult):
        if n % t == 0:
            best = t
    return best if best is not None else n


def _sigmoid(x):
    return 0.5 * jnp.tanh(0.5 * x) + 0.5


def _softplus(x):
    return jnp.maximum(x, 0.0) + jnp.log(1.0 + jnp.exp(-jnp.abs(x)))


def _silu(x):
    return x * _sigmoid(x)


def _dsilu(x):
    s = _sigmoid(x)
    return s * (1.0 + x * (1.0 - s))


_GELU_K = math.sqrt(2.0 / math.pi)


def _gelu(x):
    return 0.5 * x * (1.0 + jnp.tanh(_GELU_K * (x + 0.044715 * x * x * x)))


def _dgelu(x):
    t = jnp.tanh(_GELU_K * (x + 0.044715 * x * x * x))
    return 0.5 * (1.0 + t) + 0.5 * x * (1.0 - t * t) * _GELU_K * (1.0 + 3.0 * 0.044715 * x * x)


def _expm1(x):
    poly = x * (1.0 + x * (0.5 + x * (1.0 / 6.0 + x * (1.0 / 24.0 + x * (1.0 / 120.0 + x * (1.0 / 720.0))))))
    return jnp.where(jnp.abs(x) < 0.1, poly, jnp.exp(x) - 1.0)


def _dot(a, b, dn):
    return lax.dot_general(a.astype(bf16), b.astype(bf16), dn, preferred_element_type=f32)


def _dot_01(a, b, dn, split, terms):
    r = a if split == 0 else b
    out = None
    for _ in range(terms):
        h = r.astype(bf16)
        r = r - h.astype(f32)
        d = lax.dot_general(h if split == 0 else a.astype(bf16), b.astype(bf16) if split == 0 else h, dn,
                            preferred_element_type=f32)
        out = d if out is None else out + d
    return out


MM_VMEM_BUDGET = 48 * 2 ** 20

def mm(a, b, mode, *, name, add=None, after=None, out_dtype=f32, b_shards=False, out_shards=0, epi=None):
    bs = b.shape[1:] if b_shards else b.shape
    shard_w = b.shape[2] if b_shards else None
    bcols = bs[1] * (b.shape[0] if b_shards else 1)
    if mode == "nn":
        (m, k), (k2, n) = a.shape, (bs[0], bcols)
    elif mode == "nt":
        (m, k), (n, k2) = a.shape, (bs[0], bcols)
    else:
        (k, m), (k2, n) = a.shape, b.shape
    assert k == k2, (a.shape, b.shape, mode)
    tn = _pick(n, 1536)
    if b_shards and mode == "nn":
        tn = shard_w
    if out_shards:
        tn = n // out_shards
    isz = lambda v: jnp.dtype(v.dtype).itemsize
    if epi is not None:
        assert n <= 1536 and not out_shards
        tn = n
        epi_fn, epi_rows, epi_vecs, epi_outs = epi
        tile_bytes = sum(isz(v) for v in epi_rows) + sum(jnp.dtype(o[1]).itemsize for o in epi_outs if o[0] == "row")
    else:
        epi_rows, epi_vecs, epi_outs = [], [], []
        tile_bytes = jnp.dtype(out_dtype).itemsize
    tks = [shard_w] if (b_shards and mode == "nt") else sorted({k, _pick(k, 3328), _pick(k, 2048), _pick(k, 1024)}, reverse=True)

    def vmem_of(tm, tk):
        blocks = tm * tk * isz(a) + tk * tn * isz(b) + tm * tn * (4 * int(add is not None) + tile_bytes)
        return 2 * blocks + 4 * tm * tn * int(k > tk)

    tms = (_pick(m, 1536), _pick(m, 1024), _pick(m, 512))
    if epi is not None:
        fits = [(tk, tm) for tm in tms for tk in tks if vmem_of(tm, tk) <= MM_VMEM_BUDGET]
    else:
        fits = [(tk, tm) for tk in tks for tm in tms if vmem_of(tm, tk) <= MM_VMEM_BUDGET]
    tk, tm = fits[0] if fits else (tks[-1], _pick(m, 256))
    nk = k // tk
    dn = {"nn": NN, "nt": NT, "tn": TN}[mode]
    a_spec = pl.BlockSpec((tk, tm), lambda i, j, kk: (kk, i)) if mode == "tn" else pl.BlockSpec((tm, tk), lambda i, j, kk: (i, kk))
    b_spec = pl.BlockSpec((tn, tk), lambda i, j, kk: (j, kk)) if mode == "nt" else pl.BlockSpec((tk, tn), lambda i, j, kk: (kk, j))
    if b_shards:
        b_spec = (pl.BlockSpec((None, tn, tk), lambda i, j, kk: (kk, j, 0)) if mode == "nt"
                  else pl.BlockSpec((None, tk, tn), lambda i, j, kk: (j, kk, 0)))
    o_spec = pl.BlockSpec((tm, tn), lambda i, j, kk: (i, j))
    out_shape = jax.ShapeDtypeStruct((m, n), out_dtype)
    if out_shards:
        assert add is None
        o_spec = pl.BlockSpec((None, tm, tn), lambda i, j, kk: (j, i, 0))
        out_shape = jax.ShapeDtypeStruct((out_shards, m, tn), out_dtype)
    has_add = add is not None

    n_extra = int(has_add) + int(after is not None)
    n_rows, n_vecs, n_outs = len(epi_rows), len(epi_vecs), len(epi_outs)

    def body(a_ref, b_ref, *rest):
        add_ref = rest[0] if has_add else None
        o_ref = rest[n_extra]

        def finish(r):
            if has_add:
                r = r + add_ref[...]
            if epi is None:
                o_ref[...] = r.astype(out_dtype)
            else:
                e = rest[n_extra:]
                epi_fn(r, e[:n_rows], e[n_rows:n_rows + n_vecs], e[n_rows + n_vecs:n_rows + n_vecs + n_outs],
                       pl.program_id(0) == 0)

        if nk == 1:
            finish(_dot(a_ref[...], b_ref[...], dn))
            return
        acc = rest[-1]
        kk = pl.program_id(2)

        @pl.when(kk == 0)
        def _():
            acc[...] = jnp.zeros_like(acc)

        acc[...] += _dot(a_ref[...], b_ref[...], dn)

        @pl.when(kk == nk - 1)
        def _():
            finish(acc[...])

    ins = [a, b] + ([add] if has_add else []) + ([after] if after is not None else [])
    in_specs = [a_spec, b_spec] + ([o_spec] if has_add else []) + ([ANY] if after is not None else [])
    sem0 = "parallel"
    if epi is not None:
        vec_spec = pl.BlockSpec((1, tn), lambda i, j, kk: (0, 0))
        ins += list(epi_rows) + list(epi_vecs)
        in_specs += [o_spec] * n_rows + [vec_spec] * n_vecs
        o_spec, out_shape = [], []
        for o in epi_outs:
            if o[0] == "row":
                o_spec.append(pl.BlockSpec((tm, tn), lambda i, j, kk: (i, j)))
                out_shape.append(jax.ShapeDtypeStruct((m, n), o[1]))
            elif o[0] == "vec":
                o_spec.append(vec_spec)
                out_shape.append(jax.ShapeDtypeStruct((1, n), f32))
                sem0 = "arbitrary"
            else:
                o_spec.append(pl.BlockSpec((8, 128), lambda i, j, kk: (0, 0)))
                out_shape.append(jax.ShapeDtypeStruct((8, 128), f32))
                sem0 = "arbitrary"
    return pl.pallas_call(
        body, name=name, grid=(m // tm, n // tn, nk), in_specs=in_specs, out_specs=o_spec, out_shape=out_shape,
        scratch_shapes=[pltpu.VMEM((tm, tn), f32)] if nk > 1 else [],
        compiler_params=pltpu.CompilerParams(dimension_semantics=(sem0, sem0, "arbitrary")),
    )(*ins)


def rms_fwd(x, w, *, name):
    t, d = x.shape
    tr = _pick(t, 512, 8)

    def body(x_ref, w_ref, o_ref):
        xv = x_ref[...]
        r = lax.rsqrt(jnp.mean(xv * xv, axis=-1, keepdims=True) + RMS_EPS)
        o_ref[...] = (xv * r * w_ref[...]).astype(bf16)

    return pl.pallas_call(
        body, name=name, grid=(t // tr,),
        in_specs=[pl.BlockSpec((tr, d), lambda i: (i, 0)), pl.BlockSpec((1, d), lambda i: (0, 0))],
        out_specs=pl.BlockSpec((tr, d), lambda i: (i, 0)), out_shape=jax.ShapeDtypeStruct((t, d), bf16),
    )(x, w)


def _rms_bwd_math(xv, wv, dy):
    r = lax.rsqrt(jnp.mean(xv * xv, axis=-1, keepdims=True) + RMS_EPS)
    g = dy * wv
    dx = r * g - xv * (r * r * r) * jnp.mean(g * xv, axis=-1, keepdims=True)
    dw = jnp.sum(dy * xv * r, axis=0, keepdims=True)
    return dx, dw


def epi_rms_fwd(r, rows, vecs, outs, first):
    outs[0][...] = r
    rr = lax.rsqrt(jnp.mean(r * r, axis=-1, keepdims=True) + RMS_EPS)
    outs[1][...] = (r * rr * vecs[0][...]).astype(bf16)


def epi_rms_bwd(r, rows, vecs, outs, first):
    dx, dw = _rms_bwd_math(rows[0][...], vecs[0][...], r)
    dx = dx + rows[1][...]
    outs[0][...] = dx
    if len(outs) == 3:
        outs[1][...] = dx.astype(bf16)
    dw_ref = outs[-1]

    @pl.when(first)
    def _():
        dw_ref[...] = jnp.zeros_like(dw_ref)

    dw_ref[...] += dw


def epi_loss(r, rows, vecs, outs, first):
    wv = vecs[0][...]
    rr = lax.rsqrt(jnp.mean(r * r, axis=-1, keepdims=True) + RMS_EPS)
    err = r * rr * wv - rows[0][...]
    part = 0.5 * jnp.sum(jnp.mean(err * err, axis=-1, keepdims=True), axis=0, keepdims=True)
    dx, dw = _rms_bwd_math(r, wv, err * (1.0 / r.shape[-1]))
    outs[0][...] = dx
    outs[1][...] = dx.astype(bf16)

    @pl.when(first)
    def _():
        outs[2][...] = jnp.zeros_like(outs[2])
        outs[3][...] = jnp.zeros_like(outs[3])

    outs[2][...] += dw
    outs[3][...] += part


CONV_ROWS = 1024
VREG_ELEMS = 8 * 128


def _conv_chunk(tc):
    return 16 if (16 + 8) * tc * 3 > 48 * VREG_ELEMS else 32


def conv_fwd(src, col0, width, w, b, *, silu, name):
    t = src.shape[0]
    tc = _pick(math.gcd(width, col0), 768)
    assert col0 % tc == 0
    cb = col0 // tc
    r = CONV_ROWS
    ch = _conv_chunk(tc)

    def body(u_ref, w_ref, b_ref, *rest):
        ext = rest[-1]
        j = pl.program_id(1)

        @pl.when(j == 0)
        def _():
            ext[0:8, :] = jnp.zeros((8, tc), f32)

        @pl.when(j > 0)
        def _():
            ext[0:8, :] = ext[r:r + 8, :]

        ext[8:r + 8, :] = u_ref[...]
        wv = w_ref[...]
        bv = b_ref[...]

        def chunk(c, carry):
            r0 = pl.multiple_of(c * ch, ch)
            v = ext[pl.ds(r0, ch + 8), :]
            acc = bv + wv[3:4, :] * v[8:, :]
            for s in (1, 2, 3):
                acc = acc + wv[3 - s:4 - s, :] * pltpu.roll(v, s, 0)[8:, :]
            rest[0][pl.ds(r0, ch), :] = acc
            if silu:
                rest[1][pl.ds(r0, ch), :] = _silu(acc)
            return carry

        lax.fori_loop(0, r // ch, chunk, 0)

    tile = pl.BlockSpec((r, tc), lambda c, j: (j, c))
    n_out = 2 if silu else 1
    return pl.pallas_call(
        body, name=name, grid=(width // tc, t // r),
        in_specs=[pl.BlockSpec((r, tc), lambda c, j: (j, cb + c)), pl.BlockSpec((4, tc), lambda c, j: (0, c)),
                  pl.BlockSpec((1, tc), lambda c, j: (0, c))],
        out_specs=[tile] * n_out, out_shape=[jax.ShapeDtypeStruct((t, width), f32)] * n_out,
        scratch_shapes=[pltpu.VMEM((r + 8, tc), f32)],
        compiler_params=pltpu.CompilerParams(dimension_semantics=("parallel", "arbitrary")),
    )(src, w, b)


def conv_bwd(dpost, pre, src, col0, w, dst, *, name):
    t, width = dpost.shape
    tc = _pick(math.gcd(width, col0), 768)
    assert col0 % tc == 0
    cb = col0 // tc
    r = CONV_ROWS
    ch = _conv_chunk(tc)
    nt = t // r
    has_pre = pre is not None

    def body(*refs):
        refs = refs[1:]
        if has_pre:
            d_ref, p_ref, u_ref, w_ref, du_ref, dw_ref, db_ref, ext = refs
        else:
            d_ref, u_ref, w_ref, du_ref, dw_ref, db_ref, ext = refs
        j = pl.program_id(1)

        @pl.when(j == 0)
        def _():
            ext[r:r + 8, :] = jnp.zeros((8, tc), f32)
            dw_ref[...] = jnp.zeros_like(dw_ref)
            db_ref[...] = jnp.zeros_like(db_ref)

        @pl.when(j > 0)
        def _():
            ext[r:r + 8, :] = ext[0:8, :]

        dpre = d_ref[...]
        if has_pre:
            dpre = dpre * _dsilu(p_ref[...])
        ext[0:r, :] = dpre
        wv = w_ref[...]

        def fold(p):
            out = p[0:8, :]
            for i in range(1, ch // 8):
                out = out + p[8 * i:8 * i + 8, :]
            return out

        def chunk(c, sums):
            r0 = pl.multiple_of(c * ch, ch)
            v = ext[pl.ds(r0, ch + 8), :]
            uv = u_ref[pl.ds(r0, ch), :]
            d0 = v[0:ch, :]
            du = wv[3:4, :] * d0
            new = [None] * 5
            new[3] = sums[3] + fold(d0 * uv)
            for s in (1, 2, 3):
                sh = pltpu.roll(v, ch + 8 - s, 0)[0:ch, :]
                du = du + wv[3 - s:4 - s, :] * sh
                new[3 - s] = sums[3 - s] + fold(sh * uv)
            new[4] = sums[4] + fold(d0)
            du_ref[pl.ds(r0, ch), :] = du.astype(bf16)
            return tuple(new)

        sums = lax.fori_loop(0, r // ch, chunk, tuple(jnp.zeros((8, tc), f32) for _ in range(5)))
        for k in range(4):
            dw_ref[k:k + 1, :] += jnp.sum(sums[k], axis=0, keepdims=True)
        db_ref[...] += jnp.sum(sums[4], axis=0, keepdims=True)

    rev = pl.BlockSpec((r, tc), lambda c, j: (nt - 1 - j, c))
    win = pl.BlockSpec((r, tc), lambda c, j: (nt - 1 - j, cb + c))
    in_specs = [ANY, rev] + ([rev] if has_pre else []) + [win, pl.BlockSpec((4, tc), lambda c, j: (0, c))]
    ins = [dst, dpost] + ([pre] if has_pre else []) + [src, w]
    return pl.pallas_call(
        body, name=name, grid=(width // tc, nt), in_specs=in_specs,
        out_specs=[win, pl.BlockSpec((4, tc), lambda c, j: (0, c)), pl.BlockSpec((1, tc), lambda c, j: (0, c))],
        out_shape=[jax.ShapeDtypeStruct(dst.shape, bf16), jax.ShapeDtypeStruct((4, width), f32),
                   jax.ShapeDtypeStruct((1, width), f32)],
        input_output_aliases={0: 0},
        scratch_shapes=[pltpu.VMEM((r + 8, tc), f32)],
        compiler_params=pltpu.CompilerParams(dimension_semantics=("parallel", "arbitrary")),
    )(*ins)


def _ssd_common(xbc_ref, dtr_ref, dtrT_ref, par_row_ref, par_col_ref):
    l = SSM_CHUNK
    x = xbc_ref[:, 0:SSM_GROUP_W]
    bm = xbc_ref[:, SSM_GROUP_W:SSM_GROUP_W + SSM_D_STATE]
    cm = xbc_ref[:, SSM_GROUP_W + SSM_D_STATE:XBC_GROUP_W]
    par_row = par_row_ref[0]
    par_col = par_col_ref[0]
    bias_row, alog_row = par_row[0:1, :], par_row[1:2, :]
    bias_col, alog_col = par_col[:, 0:1], par_col[:, 1:2]
    dtr = dtr_ref[0]
    dt = _softplus(dtr + bias_row)
    dt_t = _softplus(dtrT_ref[0] + bias_col)
    a_row = -jnp.exp(alog_row)
    a_col = -jnp.exp(alog_col)
    li = lax.broadcasted_iota(jnp.int32, (l, l), 0)
    si = lax.broadcasted_iota(jnp.int32, (l, l), 1)
    tri = (li >= si).astype(f32)
    cs = _dot_01(tri, dt * a_row, NN, 1, 3)
    cs_t = _dot_01(dt_t * a_col, tri, NT, 0, 3)
    off = lax.broadcasted_iota(jnp.int32, (SSM_HPG, SSM_GROUP_W), 1) - SSM_HEAD_DIM * lax.broadcasted_iota(
        jnp.int32, (SSM_HPG, SSM_GROUP_W), 0)
    ex = ((off >= 0) & (off < SSM_HEAD_DIM)).astype(f32)
    cs_x = _dot_01(cs, ex, NN, 0, 3)
    cl_x = cs_x[l - 1:l, :]
    return dict(x=x, bm=bm, cm=cm, dtr=dtr, dt=dt, a_row=a_row, bias_row=bias_row, tri=tri, li=li, si=si, cs=cs,
                cs_t=cs_t, ex=ex, dt_x=_dot_01(dt, ex, NN, 0, 2), d_x=_dot_01(par_row, ex, NN, 0, 2)[2:3, :], e_x=jnp.exp(cs_x),
                el_x=jnp.exp(cl_x), dec_x=jnp.exp(cl_x - cs_x))


def ssd_fwd(xbc, dtr, dtr_t, par_row, par_col, *, name):
    t = xbc.shape[0]
    nc = t // SSM_CHUNK
    l, p = SSM_CHUNK, SSM_HEAD_DIM

    def body(xbc_ref, dtr_ref, dtrT_ref, prow_ref, pcol_ref, y_ref, sin_ref, state):
        @pl.when(pl.program_id(1) == 0)
        def _():
            state[...] = jnp.zeros_like(state)

        q = _ssd_common(xbc_ref, dtr_ref, dtrT_ref, prow_ref, pcol_ref)
        st = state[...]
        sin_ref[0] = st
        xd = q["x"] * q["dt_x"]
        g = _dot(q["cm"], q["bm"], NT)
        for r in range(SSM_HPG):
            sl = slice(r * p, (r + 1) * p)
            diff = q["cs"][:, r:r + 1] - q["cs_t"][r:r + 1, :]
            lm = jnp.where(q["li"] >= q["si"], jnp.exp(jnp.minimum(diff, 0.0)), 0.0)
            y_ref[:, sl] = _dot(g * lm, xd[:, sl], NN)
        y_ref[...] += q["e_x"] * _dot(q["cm"], st, NN) + q["d_x"] * q["x"]
        state[...] = q["el_x"] * st + _dot(q["bm"].T, xd * q["dec_x"], NN)

    return pl.pallas_call(
        body, name=name, grid=(SSM_GROUPS, nc),
        in_specs=[pl.BlockSpec((l, XBC_GROUP_W), lambda g, c: (c, g)),
                  pl.BlockSpec((1, l, SSM_HPG), lambda g, c: (g, c, 0)),
                  pl.BlockSpec((1, SSM_HPG, l), lambda g, c: (g, 0, c)),
                  pl.BlockSpec((1, 8, 8), lambda g, c: (g, 0, 0)),
                  pl.BlockSpec((1, 8, 8), lambda g, c: (g, 0, 0))],
        out_specs=[pl.BlockSpec((l, SSM_GROUP_W), lambda g, c: (c, g)),
                   pl.BlockSpec((1, SSM_D_STATE, SSM_GROUP_W), lambda g, c: (c, 0, g))],
        out_shape=[jax.ShapeDtypeStruct((t, SSM_D_INNER), f32),
                   jax.ShapeDtypeStruct((nc, SSM_D_STATE, SSM_D_INNER), f32)],
        scratch_shapes=[pltpu.VMEM((SSM_D_STATE, SSM_GROUP_W), f32)],
        compiler_params=pltpu.CompilerParams(dimension_semantics=("parallel", "arbitrary")),
    )(xbc, dtr, dtr_t, par_row, par_col)


def ssd_bwd(xbc, dtr, dtr_t, par_row, par_col, s_in, dy, *, name):
    t = xbc.shape[0]
    nc = t // SSM_CHUNK
    l, p = SSM_CHUNK, SSM_HEAD_DIM

    def body(xbc_ref, dtr_ref, dtrT_ref, prow_ref, pcol_ref, sin_ref, dy_ref, dxbc_ref, ddtr_ref, dpar_ref,
             dstate, yd_buf, dxd_buf):
        @pl.when(pl.program_id(1) == 0)
        def _():
            dstate[...] = jnp.zeros_like(dstate)
            dpar_ref[...] = jnp.zeros_like(dpar_ref)

        q = _ssd_common(xbc_ref, dtr_ref, dtrT_ref, prow_ref, pcol_ref)
        x, bm, cm, ex, li, si = q["x"], q["bm"], q["cm"], q["ex"], q["li"], q["si"]
        e_x, el_x, dec_x = q["e_x"], q["el_x"], q["dec_x"]
        st = sin_ref[0]
        dst = dstate[...]
        dy = dy_ref[...]
        xd = x * q["dt_x"]
        g = _dot(cm, bm, NT)
        dg = jnp.zeros((l, l), f32)
        for r in range(SSM_HPG):
            sl = slice(r * p, (r + 1) * p)
            diff = q["cs"][:, r:r + 1] - q["cs_t"][r:r + 1, :]
            lm = jnp.where(li >= si, jnp.exp(jnp.minimum(diff, 0.0)), 0.0)
            m = (g * lm).astype(bf16)
            xdh, dyh = xd[:, sl].astype(bf16), dy[:, sl].astype(bf16)
            yd_buf[:, sl] = _dot(m, xdh, NN)
            dxd_buf[:, sl] = _dot(m, dyh, TN)
            dg = dg + _dot(dyh, xdh, NT) * lm
        yd, dxd_diag = yd_buf[...], dxd_buf[...]
        yo = e_x * _dot(cm, st, NN)
        dz = e_x * dy
        wv = _dot(bm, dst, NN)
        xw = xd * wv * dec_x
        row8 = lax.broadcasted_iota(jnp.int32, (l, SSM_HPG), 0)
        dy_b, xd_b = dy.astype(bf16).astype(f32), xd.astype(bf16).astype(f32)
        dcs = _dot_01(dy_b * yd - xd_b * dxd_diag + dy * yo - xw, ex, NT, 0, 3)
        tail = jnp.sum(xw, axis=0, keepdims=True) + el_x * jnp.sum(dst * st, axis=0, keepdims=True)
        dcl = _dot_01(jnp.broadcast_to(tail, (SSM_HPG, SSM_GROUP_W)), ex, NT, 0, 3)[0:1, :]
        dcs = dcs + jnp.where(row8 == l - 1, dcl, 0.0)
        dda = _dot_01(q["tri"], dcs, TN, 1, 3)
        dxd = dxd_diag + dec_x * wv
        ddt = _dot_01(dxd * x, ex, NT, 0, 3) + dda * q["a_row"]
        ddtr = ddt * _sigmoid(q["dtr"] + q["bias_row"])
        ddtr_ref[0] = ddtr
        dd = _dot_01(jnp.broadcast_to(jnp.sum(dy * x, axis=0, keepdims=True), (SSM_HPG, SSM_GROUP_W)), ex, NT, 0, 2)[0:1, :]
        dpar_ref[0, 0:1, :] += jnp.sum(ddtr, axis=0, keepdims=True)
        dpar_ref[0, 1:2, :] += jnp.sum(dda * q["dt"], axis=0, keepdims=True) * q["a_row"]
        dpar_ref[0, 2:3, :] += dd
        dxbc_ref[:, 0:SSM_GROUP_W] = dxd * q["dt_x"] + q["d_x"] * dy
        dxbc_ref[:, SSM_GROUP_W:SSM_GROUP_W + SSM_D_STATE] = _dot(dg, cm, TN) + _dot(xd * dec_x, dst, NT)
        dxbc_ref[:, SSM_GROUP_W + SSM_D_STATE:XBC_GROUP_W] = _dot(dg, bm, NN) + _dot(dz, st, NT)
        dstate[...] = _dot(cm.T, dz, NN) + el_x * dst

    rc = lambda c: nc - 1 - c
    return pl.pallas_call(
        body, name=name, grid=(SSM_GROUPS, nc),
        in_specs=[pl.BlockSpec((l, XBC_GROUP_W), lambda g, c: (rc(c), g)),
                  pl.BlockSpec((1, l, SSM_HPG), lambda g, c: (g, rc(c), 0)),
                  pl.BlockSpec((1, SSM_HPG, l), lambda g, c: (g, 0, rc(c))),
                  pl.BlockSpec((1, 8, 8), lambda g, c: (g, 0, 0)),
                  pl.BlockSpec((1, 8, 8), lambda g, c: (g, 0, 0)),
                  pl.BlockSpec((1, SSM_D_STATE, SSM_GROUP_W), lambda g, c: (rc(c), 0, g)),
                  pl.BlockSpec((l, SSM_GROUP_W), lambda g, c: (rc(c), g))],
        out_specs=[pl.BlockSpec((l, XBC_GROUP_W), lambda g, c: (rc(c), g)),
                   pl.BlockSpec((1, l, SSM_HPG), lambda g, c: (g, rc(c), 0)),
                   pl.BlockSpec((1, 8, 8), lambda g, c: (g, 0, 0))],
        out_shape=[jax.ShapeDtypeStruct((t, SSM_CONV_DIM), f32),
                   jax.ShapeDtypeStruct((SSM_GROUPS, t, SSM_HPG), f32),
                   jax.ShapeDtypeStruct((SSM_GROUPS, 8, 8), f32)],
        scratch_shapes=[pltpu.VMEM((SSM_D_STATE, SSM_GROUP_W), f32), pltpu.VMEM((l, SSM_GROUP_W), f32),
                        pltpu.VMEM((l, SSM_GROUP_W), f32)],
        compiler_params=pltpu.CompilerParams(dimension_semantics=("parallel", "arbitrary")),
    )(xbc, dtr, dtr_t, par_row, par_col, s_in, dy)


def gnorm_fwd(y, proj, w, *, name):
    t = y.shape[0]
    tr = _pick(t, 2048, 8)
    gw = SSM_GROUP_W
    zb = OFF_Z // gw

    def body(y_ref, z_ref, w_ref, o_ref):
        y2 = y_ref[...] * _silu(z_ref[...])
        r = lax.rsqrt(jnp.mean(y2 * y2, axis=-1, keepdims=True) + RMS_EPS)
        o_ref[...] = (y2 * r * w_ref[...]).astype(bf16)

    return pl.pallas_call(
        body, name=name, grid=(SSM_GROUPS, t // tr),
        in_specs=[pl.BlockSpec((tr, gw), lambda g, i: (i, g)), pl.BlockSpec((tr, gw), lambda g, i: (i, zb + g)),
                  pl.BlockSpec((1, gw), lambda g, i: (0, g))],
        out_specs=pl.BlockSpec((tr, gw), lambda g, i: (i, g)), out_shape=jax.ShapeDtypeStruct((t, SSM_D_INNER), bf16),
    )(y, proj, w)


def gnorm_bwd(y, proj, w, dout, dst, *, name):
    t = y.shape[0]
    tr = _pick(t, 2048, 8)
    gw = SSM_GROUP_W
    zb = OFF_Z // gw

    def body(_, y_ref, z_ref, w_ref, do_ref, dy_ref, dz_ref, dw_ref):
        yv, zv = y_ref[...], z_ref[...]
        sz = _silu(zv)
        y2 = yv * sz
        dy2, dw = _rms_bwd_math(y2, w_ref[...], do_ref[...].astype(f32))
        dy_ref[...] = dy2 * sz
        dz_ref[...] = (dy2 * yv * _dsilu(zv)).astype(bf16)

        @pl.when(pl.program_id(1) == 0)
        def _():
            dw_ref[...] = jnp.zeros_like(dw_ref)

        dw_ref[...] += dw

    tile = pl.BlockSpec((tr, gw), lambda g, i: (i, g))
    vec = pl.BlockSpec((1, gw), lambda g, i: (0, g))
    return pl.pallas_call(
        body, name=name, grid=(SSM_GROUPS, t // tr),
        in_specs=[ANY, tile, pl.BlockSpec((tr, gw), lambda g, i: (i, zb + g)), vec, tile],
        out_specs=[tile, pl.BlockSpec((tr, gw), lambda g, i: (i, zb + g)), vec],
        out_shape=[jax.ShapeDtypeStruct((t, SSM_D_INNER), f32), jax.ShapeDtypeStruct(dst.shape, bf16),
                   jax.ShapeDtypeStruct((1, SSM_D_INNER), f32)],
        input_output_aliases={0: 1},
        compiler_params=pltpu.CompilerParams(dimension_semantics=("parallel", "arbitrary")),
    )(dst, y, proj, w, dout)


LRU_ROWS = 2048


def _lru_gates(uv, wr_ref, wi_ref, br_ref, bi_ref, lam_ref):
    rg = _sigmoid(_dot(uv, wr_ref[0], NN) + br_ref[...])
    ig = _sigmoid(_dot(uv, wi_ref[0], NN) + bi_ref[...])
    sp = _softplus(-lam_ref[...])
    la = -LRU_C * rg * sp
    a = jnp.exp(la)
    s = jnp.sqrt(jnp.maximum(-_expm1(2.0 * la), 0.0))
    return rg, ig, sp, la, a, s


def lru_fwd(u, proj, w_r, b_r, w_i, b_i, lam, *, name):
    t = u.shape[0]
    r = LRU_ROWS
    lb = LRU_BLOCK
    yb = OFF_LY // lb

    def body(u_ref, y_ref, wr_ref, br_ref, wi_ref, bi_ref, lam_ref, h_ref, o_ref, carry):
        @pl.when(pl.program_id(1) == 0)
        def _():
            carry[...] = jnp.zeros_like(carry)

        uv = u_ref[...]
        _, ig, _, _, a, s = _lru_gates(uv, wr_ref, wi_ref, br_ref, bi_ref, lam_ref)
        b = s * ig * uv
        row = lax.broadcasted_iota(jnp.int32, (r, lb), 0)
        d = 1
        while d < r:
            keep = row >= d
            b = b + a * jnp.where(keep, pltpu.roll(b, d, 0), 0.0)
            a = a * jnp.where(keep, pltpu.roll(a, d, 0), 1.0)
            d *= 2
        h = b + a * carry[0:1, :]
        carry[0:1, :] = h[r - 1:r, :]
        h_ref[...] = h
        o_ref[...] = (h * _gelu(y_ref[...])).astype(bf16)

    tile = pl.BlockSpec((r, lb), lambda hb, j: (j, hb))
    vec = pl.BlockSpec((1, lb), lambda hb, j: (0, hb))
    wsp = pl.BlockSpec((1, lb, lb), lambda hb, j: (hb, 0, 0))
    return pl.pallas_call(
        body, name=name, grid=(LRU_BLOCKS, t // r),
        in_specs=[tile, pl.BlockSpec((r, lb), lambda hb, j: (j, yb + hb)), wsp, vec, wsp, vec, vec],
        out_specs=[tile, tile],
        out_shape=[jax.ShapeDtypeStruct((t, LRU_WIDTH), f32), jax.ShapeDtypeStruct((t, LRU_WIDTH), bf16)],
        scratch_shapes=[pltpu.VMEM((8, lb), f32)],
        compiler_params=pltpu.CompilerParams(dimension_semantics=("parallel", "arbitrary")),
    )(u, proj, w_r, b_r, w_i, b_i, lam)


def lru_bwd(u, proj, hseq, dout, w_r, b_r, w_i, b_i, lam, dst, *, name):
    t = u.shape[0]
    r = LRU_ROWS
    nt = t // r
    lb = LRU_BLOCK
    yb = OFF_LY // lb

    def body(_, u_ref, y_ref, h_ref, hp_ref, do_ref, wr_ref, br_ref, wi_ref, bi_ref, lam_ref,
             du_ref, dy_ref, dwr_ref, dwi_ref, dbr_ref, dbi_ref, dlam_ref, carry_dh, carry_a):
        j = pl.program_id(1)

        @pl.when(j == 0)
        def _():
            carry_dh[...] = jnp.zeros_like(carry_dh)
            carry_a[...] = jnp.zeros_like(carry_a)
            dwr_ref[...] = jnp.zeros_like(dwr_ref)
            dwi_ref[...] = jnp.zeros_like(dwi_ref)
            dbr_ref[...] = jnp.zeros_like(dbr_ref)
            dbi_ref[...] = jnp.zeros_like(dbi_ref)
            dlam_ref[...] = jnp.zeros_like(dlam_ref)

        uv = u_ref[...]
        yv = y_ref[...]
        hv = h_ref[...]
        dov = do_ref[...]
        rg, ig, sp, la, a, s = _lru_gates(uv, wr_ref, wi_ref, br_ref, bi_ref, lam_ref)
        dy_ref[...] = (dov * hv * _dgelu(yv)).astype(bf16)
        gq = dov * _gelu(yv)
        row = lax.broadcasted_iota(jnp.int32, (r, lb), 0)
        an = jnp.where(row < r - 1, pltpu.roll(a, r - 1, 0), carry_a[0:1, :])
        d = 1
        while d < r:
            keep = row < r - d
            gq = gq + an * jnp.where(keep, pltpu.roll(gq, r - d, 0), 0.0)
            an = an * jnp.where(keep, pltpu.roll(an, r - d, 0), 1.0)
            d *= 2
        dh = gq + an * carry_dh[0:1, :]
        carry_dh[0:1, :] = dh[0:1, :]
        carry_a[0:1, :] = a[0:1, :]
        first = jnp.where(j == nt - 1, 0.0, 1.0) * hp_ref[7:8, :]
        hprev = jnp.where(row >= 1, pltpu.roll(hv, 1, 0), first)
        da = dh * hprev
        iu = ig * uv
        e2 = jnp.exp(2.0 * la)
        dla = da * a - dh * iu * e2 / jnp.maximum(s, 1e-30)
        drp = dla * (-LRU_C * sp) * rg * (1.0 - rg)
        dip = dh * s * uv * ig * (1.0 - ig)
        dlam_ref[...] += jnp.sum(dla * (LRU_C * rg) * _sigmoid(-lam_ref[...]), axis=0, keepdims=True)
        du_ref[...] = dh * s * ig + _dot(drp, wr_ref[0], NT) + _dot(dip, wi_ref[0], NT)
        dwr_ref[0] += _dot(uv, drp, TN)
        dwi_ref[0] += _dot(uv, dip, TN)
        dbr_ref[...] += jnp.sum(drp, axis=0, keepdims=True)
        dbi_ref[...] += jnp.sum(dip, axis=0, keepdims=True)

    rj = lambda j: nt - 1 - j
    tile = pl.BlockSpec((r, lb), lambda hb, j: (rj(j), hb))
    vec = pl.BlockSpec((1, lb), lambda hb, j: (0, hb))
    wsp = pl.BlockSpec((1, lb, lb), lambda hb, j: (hb, 0, 0))
    hprev_spec = pl.BlockSpec((8, lb), lambda hb, j: (jnp.maximum(rj(j) * (r // 8) - 1, 0), hb))
    ywin = pl.BlockSpec((r, lb), lambda hb, j: (rj(j), yb + hb))
    return pl.pallas_call(
        body, name=name, grid=(LRU_BLOCKS, nt),
        in_specs=[ANY, tile, ywin, tile, hprev_spec, tile, wsp, vec, wsp, vec, vec],
        out_specs=[tile, ywin, wsp, wsp, vec, vec, vec],
        out_shape=[jax.ShapeDtypeStruct((t, LRU_WIDTH), f32), jax.ShapeDtypeStruct(dst.shape, bf16),
                   jax.ShapeDtypeStruct((LRU_BLOCKS, lb, lb), f32), jax.ShapeDtypeStruct((LRU_BLOCKS, lb, lb), f32),
                   jax.ShapeDtypeStruct((1, LRU_WIDTH), f32), jax.ShapeDtypeStruct((1, LRU_WIDTH), f32),
                   jax.ShapeDtypeStruct((1, LRU_WIDTH), f32)],
        input_output_aliases={0: 1},
        scratch_shapes=[pltpu.VMEM((8, lb), f32), pltpu.VMEM((8, lb), f32)],
        compiler_params=pltpu.CompilerParams(dimension_semantics=("parallel", "arbitrary")),
    )(dst, u, proj, hseq, hseq, dout, w_r, b_r, w_i, b_i, lam)


def merge_fwd(proj, bg, y_ssm, y_lru, *, name):
    t, d = y_ssm.shape
    tr = _pick(t, 512, 8)
    gb = OFF_GATES // d

    def body(gs_ref, gl_ref, bs_ref, bl_ref, ys_ref, yl_ref, o_ref):
        gs = _sigmoid(gs_ref[...] + bs_ref[...])
        gl = _sigmoid(gl_ref[...] + bl_ref[...])
        o_ref[...] = (gs * ys_ref[...].astype(f32) + gl * yl_ref[...].astype(f32)).astype(bf16)

    row = pl.BlockSpec((tr, d), lambda i: (i, 0))
    return pl.pallas_call(
        body, name=name, grid=(t // tr,),
        in_specs=[pl.BlockSpec((tr, d), lambda i: (i, gb)), pl.BlockSpec((tr, d), lambda i: (i, gb + 1)),
                  pl.BlockSpec((1, d), lambda i: (0, 0)), pl.BlockSpec((1, d), lambda i: (0, 1)), row, row],
        out_specs=row, out_shape=jax.ShapeDtypeStruct((t, d), bf16),
    )(proj, proj, bg, bg, y_ssm, y_lru)


def merge_bwd(proj, bg, y_ssm, y_lru, dmix, *, name):
    t, d = y_ssm.shape
    tr = _pick(t, 512, 8)
    gb = OFF_GATES // d

    def body(gs_ref, gl_ref, bs_ref, bl_ref, ys_ref, yl_ref, dm_ref, dg_ref, dys_ref, dyl_ref, dbg_ref):
        gs = _sigmoid(gs_ref[...] + bs_ref[...])
        gl = _sigmoid(gl_ref[...] + bl_ref[...])
        dm = dm_ref[...].astype(f32)
        dys_ref[...] = (dm * gs).astype(bf16)
        dyl_ref[...] = (dm * gl).astype(bf16)
        dgs = dm * ys_ref[...].astype(f32) * gs * (1.0 - gs)
        dgl = dm * yl_ref[...].astype(f32) * gl * (1.0 - gl)
        dg_ref[:, 0:d] = dgs.astype(bf16)
        dg_ref[:, d:2 * d] = dgl.astype(bf16)

        @pl.when(pl.program_id(0) == 0)
        def _():
            dbg_ref[...] = jnp.zeros_like(dbg_ref)

        dbg_ref[:, 0:d] += jnp.sum(dgs, axis=0, keepdims=True)
        dbg_ref[:, d:2 * d] += jnp.sum(dgl, axis=0, keepdims=True)

    row = pl.BlockSpec((tr, d), lambda i: (i, 0))
    return pl.pallas_call(
        body, name=name, grid=(t // tr,),
        in_specs=[pl.BlockSpec((tr, d), lambda i: (i, gb)), pl.BlockSpec((tr, d), lambda i: (i, gb + 1)),
                  pl.BlockSpec((1, d), lambda i: (0, 0)), pl.BlockSpec((1, d), lambda i: (0, 1)), row, row, row],
        out_specs=[pl.BlockSpec((tr, 2 * d), lambda i: (i, OFF_GATES // (2 * d))), row, row,
                   pl.BlockSpec((1, 2 * d), lambda i: (0, 0))],
        out_shape=[jax.ShapeDtypeStruct((t, PROJ_W), bf16), jax.ShapeDtypeStruct((t, d), bf16),
                   jax.ShapeDtypeStruct((t, d), bf16), jax.ShapeDtypeStruct((1, 2 * d), f32)],
        compiler_params=pltpu.CompilerParams(dimension_semantics=("arbitrary",)),
    )(proj, proj, bg, bg, y_ssm, y_lru, dmix)


def swiglu_fwd(ff, *, name):
    t = ff.shape[0]
    hd = FFN_HIDDEN
    tr = _pick(t, 512, 8)

    def body(f_ref, o_ref):
        o_ref[...] = (_silu(f_ref[:, 0:hd].astype(f32)) * f_ref[:, hd:2 * hd].astype(f32)).astype(bf16)

    return pl.pallas_call(
        body, name=name, grid=(t // tr,), in_specs=[pl.BlockSpec((tr, 2 * hd), lambda i: (i, 0))],
        out_specs=pl.BlockSpec((tr, hd), lambda i: (i, 0)), out_shape=jax.ShapeDtypeStruct((t, hd), bf16),
    )(ff)


def swiglu_bwd(ff, dact, *, name):
    t = ff.shape[0]
    hd = FFN_HIDDEN
    tr = _pick(t, 512, 8)

    def body(f_ref, d_ref, o_ref):
        gate, up, dv = f_ref[:, 0:hd].astype(f32), f_ref[:, hd:2 * hd].astype(f32), d_ref[...].astype(f32)
        o_ref[:, 0:hd] = (dv * up * _dsilu(gate)).astype(bf16)
        o_ref[:, hd:2 * hd] = (dv * _silu(gate)).astype(bf16)

    return pl.pallas_call(
        body, name=name, grid=(t // tr,),
        in_specs=[pl.BlockSpec((tr, 2 * hd), lambda i: (i, 0)), pl.BlockSpec((tr, hd), lambda i: (i, 0))],
        out_specs=pl.BlockSpec((tr, 2 * hd), lambda i: (i, 0)), out_shape=jax.ShapeDtypeStruct((t, 2 * hd), bf16),
    )(ff, dact)


def _adam_math(w, g, m, v):
    m = ADAM_B1 * m + (1.0 - ADAM_B1) * g
    v = ADAM_B2 * v + (1.0 - ADAM_B2) * (g * g)
    m_hat = m / (1.0 - ADAM_B1 ** ADAM_STEP)
    v_hat = v / (1.0 - ADAM_B2 ** ADAM_STEP)
    delta = -ADAM_LR * (m_hat / (jnp.sqrt(v_hat) + ADAM_EPS) + ADAM_WD * w)
    return delta, m, v


def _row_tile(rows, cols):
    cap = max(8, (1 << 19) // cols)
    return _pick(rows, cap, 8) if rows % 8 == 0 else rows


def adamw(w, g, m, v, *, name):
    rows, cols = w.shape
    tr = _row_tile(rows, cols)

    def body(w_ref, g_ref, m_ref, v_ref, d_ref, nm_ref, nv_ref):
        d, nm, nv = _adam_math(w_ref[...], g_ref[...], m_ref[...], v_ref[...])
        d_ref[...] = d
        nm_ref[...] = nm
        nv_ref[...] = nv

    tile = pl.BlockSpec((tr, cols), lambda i: (i, 0))
    return pl.pallas_call(
        body, name=name, grid=(rows // tr,), in_specs=[tile] * 4, out_specs=[tile] * 3,
        out_shape=[jax.ShapeDtypeStruct((rows, cols), f32)] * 3,
    )(w, g, m, v)


def adamw_many(ws, gs, ms, vs, *, name):
    n = len(ws)

    def body(*refs):
        for i in range(n):
            d, nm, nv = _adam_math(refs[i][...], refs[n + i][...], refs[2 * n + i][...], refs[3 * n + i][...])
            refs[4 * n + 3 * i][...] = d
            refs[4 * n + 3 * i + 1][...] = nm
            refs[4 * n + 3 * i + 2][...] = nv

    outs = pl.pallas_call(
        body, name=name, out_shape=[jax.ShapeDtypeStruct(w.shape, f32) for w in ws for _ in range(3)],
    )(*ws, *gs, *ms, *vs)
    return [tuple(outs[3 * i:3 * i + 3]) for i in range(n)]


def pair_add(dw, rbuf, idx, *, name):
    n, rows, cols = dw.shape
    hr = rows // 2
    tr = _row_tile(hr, cols)
    nrt = hr // tr

    def body(idx_ref, a_ref, b_ref, o_ref, own_ref):
        s = a_ref[...] + b_ref[...]
        o_ref[...] = s.astype(bf16)

        @pl.when(pl.program_id(1) == idx_ref[0])
        def _():
            own_ref[...] = s[0]

    return pl.pallas_call(
        body, name=name,
        grid_spec=pltpu.PrefetchScalarGridSpec(
            num_scalar_prefetch=1, grid=(nrt, n),
            in_specs=[pl.BlockSpec((1, tr, cols), lambda i, k, idx: (k, idx[1] * nrt + i, 0)),
                      pl.BlockSpec((1, tr, cols), lambda i, k, idx: (k, i, 0))],
            out_specs=[pl.BlockSpec((1, tr, cols), lambda i, k, idx: (k, i, 0)),
                       pl.BlockSpec((tr, cols), lambda i, k, idx: (i, 0))]),
        out_shape=[jax.ShapeDtypeStruct((n, hr, cols), bf16), jax.ShapeDtypeStruct((hr, cols), f32)],
    )(idx, dw, rbuf)


def chip_sum(own, rbuf, idx, *, name):
    hr, cols = own.shape
    tr = _row_tile(hr, cols)
    nrt = hr // tr

    def body(idx_ref, a_ref, b_ref, o_ref):
        o_ref[...] = ((a_ref[...] + b_ref[0].astype(f32)) + b_ref[1].astype(f32)) + b_ref[2].astype(f32)

    return pl.pallas_call(
        body, name=name,
        grid_spec=pltpu.PrefetchScalarGridSpec(
            num_scalar_prefetch=1, grid=(nrt,),
            in_specs=[pl.BlockSpec((tr, cols), lambda i, idx: (i, 0)),
                      pl.BlockSpec((3, tr, cols), lambda i, idx: (0, i, 0))],
            out_specs=pl.BlockSpec((tr, cols), lambda i, idx: (idx[1] * nrt + i, 0))),
        out_shape=jax.ShapeDtypeStruct((2 * hr, cols), f32),
    )(idx, own, rbuf)


def sum8(rbuf, *, name):
    n, rows, cols = rbuf.shape
    tr = _row_tile(rows, cols * n)

    def body(a_ref, o_ref):
        acc = a_ref[0]
        for k in range(1, n):
            acc = acc + a_ref[k]
        o_ref[...] = acc

    return pl.pallas_call(
        body, name=name, grid=(rows // tr,), in_specs=[pl.BlockSpec((n, tr, cols), lambda i: (0, i, 0))],
        out_specs=pl.BlockSpec((tr, cols), lambda i: (i, 0)), out_shape=jax.ShapeDtypeStruct((rows, cols), f32),
    )(rbuf)


def _coords():
    return lax.axis_index("x"), lax.axis_index("y"), lax.axis_index("c")


def _other_chips(x, y):
    return [(1 - x, y), (x, 1 - y), (1 - x, 1 - y)]


def gather_weights(shards, *, name):
    n = len(shards)
    halves = [s.shape[0] // 2 for s in shards]

    def body(*refs):
        ins, outs = refs[:n], refs[n:2 * n]
        send1, recv1, send2, recv2 = refs[2 * n:]
        x, y, c = _coords()
        me = 2 * x + y
        chips = _other_chips(x, y)
        sibling = (x, y, 1 - c)

        def half(i, k, hc):
            return outs[i].at[k, pl.ds(hc * halves[i], halves[i]), :]

        def ici(i, j):
            return pltpu.make_async_remote_copy(
                src_ref=ins[i].at[pl.ds(c * halves[i], halves[i]), :], dst_ref=half(i, me, c),
                send_sem=send1.at[i, j], recv_sem=recv1.at[i, j], device_id=(*chips[j], c), device_id_type=MESH)

        def landed(i, j):
            kj = 2 * chips[j][0] + chips[j][1]
            return pltpu.make_async_remote_copy(
                src_ref=half(i, kj, c), dst_ref=half(i, kj, c),
                send_sem=send2.at[i, j], recv_sem=recv1.at[i, j], device_id=sibling, device_id_type=MESH)

        def from_sibling(i, j):
            kj = 2 * chips[j][0] + chips[j][1]
            return pltpu.make_async_remote_copy(
                src_ref=half(i, kj, 1 - c), dst_ref=half(i, kj, 1 - c),
                send_sem=send2.at[i, j], recv_sem=recv2.at[i, j], device_id=sibling, device_id_type=MESH)

        def d2d(i, j):
            kj = 2 * chips[j][0] + chips[j][1]
            return pltpu.make_async_remote_copy(
                src_ref=half(i, kj, c), dst_ref=half(i, kj, c),
                send_sem=send2.at[i, j], recv_sem=recv2.at[i, j], device_id=sibling, device_id_type=MESH)

        for j in range(3):
            for i in range(n):
                ici(i, j).start()
        for j in range(3):
            for i in range(n):
                landed(i, j).wait_recv()
                d2d(i, j).start()
        for j in range(3):
            for i in range(n):
                from_sibling(i, j).wait_recv()
        for j in range(3):
            for i in range(n):
                ici(i, j).wait_send()
                d2d(i, j).wait_send()

    return pl.pallas_call(
        body, name=name, in_specs=[ANY] * n, out_specs=[ANY] * n,
        out_shape=[jax.ShapeDtypeStruct((N_CHIPS,) + s.shape, s.dtype) for s in shards],
        scratch_shapes=[pltpu.SemaphoreType.DMA((n, 3))] * 4,
    )(*shards)


def pair_exchange(grads, *, name):
    n = len(grads)
    halves = [g.shape[1] // 2 for g in grads]

    def body(*refs):
        ins, outs = refs[:n], refs[n:2 * n]
        send, recv = refs[2 * n:]
        x, y, c = _coords()
        cps = [pltpu.make_async_remote_copy(
            src_ref=ins[i].at[:, pl.ds((1 - c) * halves[i], halves[i]), :], dst_ref=outs[i],
            send_sem=send.at[i], recv_sem=recv.at[i], device_id=(x, y, 1 - c), device_id_type=MESH) for i in range(n)]
        for cp in cps:
            cp.start()
        for cp in cps:
            cp.wait()

    return pl.pallas_call(
        body, name=name, in_specs=[ANY] * n, out_specs=[ANY] * n,
        out_shape=[jax.ShapeDtypeStruct((N_CHIPS, g.shape[1] // 2, g.shape[2]), g.dtype) for g in grads],
        scratch_shapes=[pltpu.SemaphoreType.DMA((n,))] * 2,
    )(*grads)


def pair_gather(bufs, *, name):
    n = len(bufs)

    def body(*refs):
        ins, outs = refs[:n], refs[n:2 * n]
        send, recv = refs[2 * n:]
        x, y, c = _coords()
        cps = []
        for i in range(n):
            hr = ins[i].shape[0] // 2
            cps.append(pltpu.make_async_remote_copy(
                src_ref=ins[i].at[pl.ds(c * hr, hr), :], dst_ref=outs[i].at[pl.ds(c * hr, hr), :],
                send_sem=send.at[i], recv_sem=recv.at[i], device_id=(x, y, 1 - c), device_id_type=MESH))
        for cp in cps:
            cp.start()
        for i in range(n):
            hr = ins[i].shape[0] // 2
            pltpu.make_async_remote_copy(
                src_ref=ins[i].at[pl.ds((1 - c) * hr, hr), :], dst_ref=outs[i].at[pl.ds((1 - c) * hr, hr), :],
                send_sem=send.at[i], recv_sem=recv.at[i], device_id=(x, y, 1 - c), device_id_type=MESH).wait_recv()
        for cp in cps:
            cp.wait_send()

    return pl.pallas_call(
        body, name=name, in_specs=[ANY] * n, out_specs=[ANY] * n,
        out_shape=[jax.ShapeDtypeStruct(b.shape, b.dtype) for b in bufs],
        input_output_aliases={i: i for i in range(n)},
        scratch_shapes=[pltpu.SemaphoreType.DMA((n,))] * 2,
    )(*bufs)


def all_exchange(buf, *, name):
    rows, cols = buf.shape

    def body(in_ref, out_ref, send, recv):
        x, y, c = _coords()
        me = 4 * x + 2 * y + c
        cps = []
        for d in range(1, 8):
            px = 1 - x if d & 4 else x
            py = 1 - y if d & 2 else y
            pc = 1 - c if d & 1 else c
            cps.append(pltpu.make_async_remote_copy(
                src_ref=in_ref, dst_ref=out_ref.at[me], send_sem=send.at[d - 1], recv_sem=recv.at[d - 1],
                device_id=(px, py, pc), device_id_type=MESH))
        for cp in cps:
            cp.start()
        for d in range(1, 8):
            px = 1 - x if d & 4 else x
            py = 1 - y if d & 2 else y
            pc = 1 - c if d & 1 else c
            src = 4 * px + 2 * py + pc
            pltpu.make_async_remote_copy(
                src_ref=in_ref, dst_ref=out_ref.at[src], send_sem=send.at[d - 1], recv_sem=recv.at[d - 1],
                device_id=(px, py, pc), device_id_type=MESH).wait_recv()
        for cp in cps:
            cp.wait_send()

    return pl.pallas_call(
        body, name=name, in_specs=[ANY], out_specs=ANY,
        out_shape=jax.ShapeDtypeStruct((8, rows, cols), buf.dtype),
        scratch_shapes=[pltpu.SemaphoreType.DMA((7,)), pltpu.SemaphoreType.DMA((7,))],
    )(buf)


HBM = pl.BlockSpec(memory_space=pltpu.HBM)
SEM = pl.BlockSpec(memory_space=pltpu.SEMAPHORE)
EFFECT = pltpu.SideEffectType.DATAFLOW_SIDE_EFFECTING


def split_start(arrays, after, copies, sem_shape, *, name):
    na = len(arrays)

    def body(*refs):
        for cp in copies(refs[:na], refs[na + 1], refs[na + 2]):
            cp.start()
        refs[-1][...] = jnp.zeros((8, 128), f32)

    outs = pl.pallas_call(
        body, name=name,
        out_shape=(pltpu.SemaphoreType.DMA(sem_shape), pltpu.SemaphoreType.DMA(sem_shape),
                   *[pltpu.HBM(a.shape, a.dtype) for a in arrays], jax.ShapeDtypeStruct((8, 128), f32)),
        in_specs=[HBM] * na + [ANY], out_specs=(SEM, SEM, *[HBM] * na, pl.BlockSpec(memory_space=pltpu.VMEM)),
        input_output_aliases={i: 2 + i for i in range(na)},
        compiler_params=pltpu.CompilerParams(has_side_effects=EFFECT),
    )(*[pltpu.with_memory_space_constraint(a, pltpu.HBM) for a in arrays], after)
    return outs[0], outs[1], list(outs[2:2 + na]), outs[-1]


def split_wait(send, recv, arrays, after, copies, *, name):
    na = len(arrays)

    def body(*refs):
        for cp in copies(refs[:na], refs[na], refs[na + 1]):
            cp.wait_send()
            cp.wait_recv()

    outs = pl.pallas_call(
        body, name=name, out_shape=tuple(pltpu.HBM(a.shape, a.dtype) for a in arrays),
        in_specs=[HBM] * na + [SEM, SEM, ANY], out_specs=tuple([HBM] * na),
        input_output_aliases={i: i for i in range(na)},
        compiler_params=pltpu.CompilerParams(has_side_effects=EFFECT),
    )(*arrays, send, recv, after)
    return list(outs)


def gather_copies(n):
    def copies(refs, send, recv):
        x, y, c = _coords()
        me = 2 * x + y
        chips = _other_chips(x, y)
        return [pltpu.make_async_remote_copy(
            src_ref=refs[i], dst_ref=refs[n + i].at[me], send_sem=send.at[3 * i + j], recv_sem=recv.at[3 * i + j],
            device_id=(*chips[j], c), device_id_type=MESH) for j in range(3) for i in range(n)]
    return copies


def pair_copies(n):
    def copies(refs, send, recv):
        x, y, c = _coords()
        cps = []
        for i in range(n):
            hr = refs[i].shape[1] // 2
            cps.append(pltpu.make_async_remote_copy(
                src_ref=refs[i].at[:, pl.ds((1 - c) * hr, hr), :], dst_ref=refs[n + i], send_sem=send.at[i],
                recv_sem=recv.at[i], device_id=(x, y, 1 - c), device_id_type=MESH))
        return cps
    return copies


def all_copies():
    def copies(refs, send, recv):
        x, y, c = _coords()
        me = 4 * x + 2 * y + c
        cps = []
        for d in range(1, 8):
            peer = (1 - x if d & 4 else x, 1 - y if d & 2 else y, 1 - c if d & 1 else c)
            cps.append(pltpu.make_async_remote_copy(
                src_ref=refs[0], dst_ref=refs[1].at[me], send_sem=send.at[d - 1], recv_sem=recv.at[d - 1],
                device_id=peer, device_id_type=MESH))
        return cps
    return copies


def reduce_copies(n):
    def copies(refs, send, recv):
        x, y, c = _coords()
        chips = _other_chips(x, y)
        return [pltpu.make_async_remote_copy(
            src_ref=refs[i].at[2 * chips[j][0] + chips[j][1]], dst_ref=refs[n + i].at[j],
            send_sem=send.at[3 * i + j], recv_sem=recv.at[3 * i + j], device_id=(*chips[j], c), device_id_type=MESH)
            for j in range(3) for i in range(n)]
    return copies


def _pack(arrs):
    flat = []
    for a in arrs:
        v = a.reshape(-1).astype(f32)
        pad = (-v.shape[0]) % 128
        flat.append(jnp.pad(v, (0, pad)) if pad else v)
    v = jnp.concatenate(flat)
    rows = v.shape[0] // 128
    pad_rows = (-rows) % 256
    v = v.reshape(rows, 128)
    return jnp.pad(v, ((0, pad_rows), (0, 0))) if pad_rows else v


def _unpack(buf, shapes):
    out, row = [], 0
    for s in shapes:
        size = math.prod(s)
        rows = -(-size // 128)
        out.append(buf[row:row + rows].reshape(-1)[:size].reshape(s))
        row += rows
    return out


def _ref_of_perm():
    ref = np.arange(IN_PROJ_DIM)
    xbc = ref[4096:7168]
    xbc_p = [np.concatenate([xbc[g * 512:(g + 1) * 512], xbc[2048 + g * 128:2048 + (g + 1) * 128],
                             xbc[2560 + g * 128:2560 + (g + 1) * 128]]) for g in range(SSM_GROUPS)]
    return np.concatenate([ref[0:2048], ref[2048:4096], ref[7200:8480], ref[8480:9760], ref[7168:7200],
                           -np.ones(DT_PAD_W - SSM_HEADS, np.int64)] + xbc_p)


def _runs(vals):
    out, start = [], 0
    for i in range(1, len(vals) + 1):
        if i == len(vals) or not (vals[i] == vals[i - 1] + 1 or (vals[i] < 0 and vals[i - 1] < 0)):
            out.append((start, int(vals[start]), i - start))
            start = i
    return out


def _perm_in_from_shards(g):
    ref_of_perm = _ref_of_perm()
    sw = IN_PROJ_DIM // N_CHIPS
    parts = []
    for _, first, length in _runs(ref_of_perm):
        if first < 0:
            parts.append(jnp.zeros((g.shape[1], length), g.dtype))
            continue
        lo = first
        while lo < first + length:
            k = lo // sw
            hi = min(first + length, (k + 1) * sw)
            parts.append(g[k, :, lo - k * sw:hi - k * sw])
            lo = hi
    return jnp.concatenate(parts, axis=-1)


def _unperm_in_to_shards(w):
    ref_of_perm = _ref_of_perm()
    perm_of_ref = np.zeros(IN_PROJ_DIM, np.int64)
    perm_of_ref[ref_of_perm[ref_of_perm >= 0]] = np.nonzero(ref_of_perm >= 0)[0]
    sw = IN_PROJ_DIM // N_CHIPS
    shards = []
    for k in range(N_CHIPS):
        runs = _runs(perm_of_ref[k * sw:(k + 1) * sw])
        shards.append(jnp.concatenate([w[:, first:first + length] for _, first, length in runs], axis=-1))
    return jnp.stack(shards)


def _perm_xbc_cols(w):
    parts = []
    for g in range(SSM_GROUPS):
        parts += [w[..., g * 512:(g + 1) * 512], w[..., 2048 + g * 128:2048 + (g + 1) * 128],
                  w[..., 2560 + g * 128:2560 + (g + 1) * 128]]
    return jnp.concatenate(parts, axis=-1)


def _unperm_xbc_cols(w):
    xs = [w[..., g * XBC_GROUP_W:g * XBC_GROUP_W + 512] for g in range(SSM_GROUPS)]
    bs = [w[..., g * XBC_GROUP_W + 512:g * XBC_GROUP_W + 640] for g in range(SSM_GROUPS)]
    cs = [w[..., g * XBC_GROUP_W + 640:(g + 1) * XBC_GROUP_W] for g in range(SSM_GROUPS)]
    return jnp.concatenate(xs + bs + cs, axis=-1)


def _from_col_shards(w):
    n, r, c = w.shape
    return jnp.transpose(w, (1, 0, 2)).reshape(r, n * c)


def kernel(x, norm1_w, w_in, b_branch_gate, ssm_conv_w, ssm_conv_b, ssm_dt_bias, ssm_a_log, ssm_d, ssm_norm_w, w_out_ssm, lru_conv_w, lru_conv_b, lru_w_r, lru_b_r, lru_w_i, lru_b_i, lru_lambda, w_out_lru, w_out, norm2_w, w_ffn_in, w_ffn_out, norm_f_w, loss_target, m_norm1_w, m_w_in, m_b_branch_gate, m_ssm_conv_w, m_ssm_conv_b, m_ssm_dt_bias, m_ssm_a_log, m_ssm_d, m_ssm_norm_w, m_w_out_ssm, m_lru_conv_w, m_lru_conv_b, m_lru_w_r, m_lru_b_r, m_lru_w_i, m_lru_b_i, m_lru_lambda, m_w_out_lru, m_w_out, m_norm2_w, m_w_ffn_in, m_w_ffn_out, m_norm_f_w, v_norm1_w, v_w_in, v_b_branch_gate, v_ssm_conv_w, v_ssm_conv_b, v_ssm_dt_bias, v_ssm_a_log, v_ssm_d, v_ssm_norm_w, v_w_out_ssm, v_lru_conv_w, v_lru_conv_b, v_lru_w_r, v_lru_b_r, v_lru_w_i, v_lru_b_i, v_lru_lambda, v_w_out_lru, v_w_out, v_norm2_w, v_w_ffn_in, v_w_ffn_out, v_norm_f_w):
    xi, yi, ci = lax.axis_index("x"), lax.axis_index("y"), lax.axis_index("c")
    me = 2 * xi + yi
    idx = jnp.stack([me, ci]).astype(jnp.int32)
    x2 = x[0]
    tgt = loss_target[0]

    big_names = ["w_in", "w_out_ssm", "w_out_lru", "w_out", "w_ffn_in", "w_ffn_out"]
    big_w = dict(w_in=w_in[0], w_out_ssm=w_out_ssm[0], w_out_lru=w_out_lru[0], w_out=w_out[0], w_ffn_in=w_ffn_in[0],
                 w_ffn_out=w_ffn_out[0])
    big_m = dict(w_in=m_w_in[0], w_out_ssm=m_w_out_ssm[0], w_out_lru=m_w_out_lru[0], w_out=m_w_out[0],
                 w_ffn_in=m_w_ffn_in[0], w_ffn_out=m_w_ffn_out[0])
    big_v = dict(w_in=v_w_in[0], w_out_ssm=v_w_out_ssm[0], w_out_lru=v_w_out_lru[0], w_out=v_w_out[0],
                 w_ffn_in=v_w_ffn_in[0], w_ffn_out=v_w_ffn_out[0])
    conv_pad = jnp.zeros((16, 768), f32).at[0:4, :].set(ssm_conv_w[0]).at[8:12, 0:320].set(lru_conv_w[0])
    mine = [big_w["w_in"].astype(bf16), conv_pad]
    gathered = gather_weights(mine, name="gather_weights")
    g_in, g_conv = [lax.dynamic_update_index_in_dim(g, s, me, 0) for g, s in zip(gathered, mine)]
    w_in_p = _perm_in_from_shards(g_in)
    late_names = big_names[1:]
    late = [big_w[k].astype(bf16) for k in late_names]
    late_lands = [lax.empty((N_CHIPS,) + s.shape, bf16) for s in late]
    g_send, g_recv, g_arrays, g_token = split_start(late + late_lands, g_conv, gather_copies(5), (15,),
                                                    name="gather_late_start")
    ssm_cw_full = _from_col_shards(g_conv[:, 0:4, :])
    lru_cw_full = _from_col_shards(g_conv[:, 8:12, 0:320])
    ssm_cw_p = _perm_xbc_cols(ssm_cw_full)
    ssm_cb_p = _perm_xbc_cols(ssm_conv_b)

    par = jnp.stack([ssm_dt_bias[0], ssm_a_log[0], ssm_d[0]], axis=0).reshape(3, SSM_GROUPS, SSM_HPG)
    par_row = jnp.zeros((SSM_GROUPS, 8, 8), f32).at[:, 0:3, :].set(jnp.transpose(par, (1, 0, 2)))
    par_col = jnp.transpose(par_row, (0, 2, 1))

    hn1 = rms_fwd(x2, norm1_w + g_token[0:1, 0:1], name="rms1_fwd")
    proj = mm(hn1, w_in_p, "nn", name="in_proj")
    t = x2.shape[0]
    dtr = jnp.transpose(proj[:, OFF_DT:OFF_DT + 32].reshape(t, SSM_GROUPS, SSM_HPG), (1, 0, 2))
    dtr_t = jnp.transpose(dtr, (0, 2, 1))
    xbc_pre, xbc_post = conv_fwd(proj, OFF_XBC, SSM_CONV_DIM, ssm_cw_p, ssm_cb_p, silu=True, name="ssm_conv_fwd")
    y_ssd, s_in = ssd_fwd(xbc_post, dtr, dtr_t, par_row, par_col, name="ssd_fwd")
    yn = gnorm_fwd(y_ssd, proj, ssm_norm_w, name="gnorm_fwd")
    g_arrays = split_wait(g_send, g_recv, g_arrays, yn, gather_copies(5), name="gather_late_wait")
    g_out_ssm, g_out_lru, g_out, g_ffn_in, g_ffn_out = [
        lax.dynamic_update_index_in_dim(g, s, me, 0) for g, s in zip(g_arrays[5:], late)]
    w_out_ssm_f = g_out_ssm.reshape(SSM_D_INNER, D_MODEL)
    w_out_lru_f = g_out_lru.reshape(LRU_WIDTH, D_MODEL)
    w_out_f = g_out.reshape(D_MODEL, D_MODEL)
    w_ffn_out_f = g_ffn_out.reshape(FFN_HIDDEN, D_MODEL)
    y_ssm = mm(yn, w_out_ssm_f, "nn", out_dtype=bf16, name="out_ssm")
    (u_lru,) = conv_fwd(proj, OFF_LX, LRU_WIDTH, lru_cw_full, lru_conv_b, silu=False, name="lru_conv_fwd")
    h_lru, o_lru = lru_fwd(u_lru, proj, lru_w_r[0], lru_b_r, lru_w_i[0], lru_b_i, lru_lambda, name="lru_fwd")
    y_lru = mm(o_lru, w_out_lru_f, "nn", out_dtype=bf16, name="out_lru")
    mix = merge_fwd(proj, b_branch_gate, y_ssm, y_lru, name="merge_fwd")
    h1, hn2 = mm(mix, w_out_f, "nn", add=x2, name="out_proj",
                 epi=(epi_rms_fwd, [], [norm2_w], [("row", f32), ("row", bf16)]))
    ff = mm(hn2, g_ffn_in, "nn", b_shards=True, out_dtype=bf16, name="ffn_in")
    act = swiglu_fwd(ff, name="swiglu_fwd")
    dh2, dh2_b, d_norm_f, loss_tile = mm(act, w_ffn_out_f, "nn", add=h1, name="ffn_out",
                                         epi=(epi_loss, [tgt], [norm_f_w.reshape(1, D_MODEL)],
                                              [("row", f32), ("row", bf16), ("vec",), ("tile",)]))

    d_w_ffn_out = mm(act, dh2_b, "tn", name="d_w_ffn_out")
    dact = mm(dh2_b, w_ffn_out_f, "nt", out_dtype=bf16, name="d_act")
    dff = swiglu_bwd(ff, dact, name="swiglu_bwd")
    d_w_ffn_in = mm(hn2, dff, "tn", out_shards=N_CHIPS, name="d_w_ffn_in")
    dh1, dh1_b, d_norm2 = mm(dff, g_ffn_in, "nt", b_shards=True, name="d_hn2",
                             epi=(epi_rms_bwd, [h1, dh2], [norm2_w], [("row", f32), ("row", bf16), ("vec",)]))
    d_w_out = mm(mix, dh1_b, "tn", name="d_w_out")
    dmix = mm(dh1_b, w_out_f, "nt", out_dtype=bf16, name="d_mix")
    dproj, dy_ssm, dy_lru, d_bg = merge_bwd(proj, b_branch_gate, y_ssm, y_lru, dmix, name="merge_bwd")
    d_w_out_ssm = mm(yn, dy_ssm, "tn", name="d_w_out_ssm")
    d_w_out_lru = mm(o_lru, dy_lru, "tn", name="d_w_out_lru")
    early_g = [d_w_out_ssm.reshape(N_CHIPS, 512, D_MODEL), d_w_out_lru.reshape(N_CHIPS, 320, D_MODEL),
               d_w_out.reshape(N_CHIPS, 256, D_MODEL), d_w_ffn_in, d_w_ffn_out.reshape(N_CHIPS, 704, D_MODEL)]
    p_lands = [lax.empty((N_CHIPS, g.shape[1] // 2, g.shape[2]), f32) for g in early_g]
    p_send, p_recv, p_arrays, p_token = split_start(early_g + p_lands, early_g[0], pair_copies(5), (5,),
                                                    name="pair_early_start")
    dyn = mm(dy_ssm, w_out_ssm_f, "nt", out_dtype=bf16, after=p_token, name="d_yn")
    dy_ssd, dproj, d_ssm_norm = gnorm_bwd(y_ssd, proj, ssm_norm_w, dyn, dproj, name="gnorm_bwd")
    p_arrays = split_wait(p_send, p_recv, p_arrays, dy_ssd, pair_copies(5), name="pair_early_wait")
    e_pairs = [pair_add(g, rb, idx, name="pair_add_" + k) for g, rb, k in zip(p_arrays[:5], p_arrays[5:], late_names)]
    e_lands = [lax.empty((3,) + p[0].shape[1:], bf16) for p in e_pairs]
    e_send, e_recv, e_arrays, e_token = split_start([p[0] for p in e_pairs] + e_lands, e_pairs[0][1], reduce_copies(5),
                                                    (15,), name="reduce_early_start")
    dxbc_post, ddtr, dpar = ssd_bwd(xbc_post, dtr, dtr_t, par_row + e_token[0:1, 0:1], par_col, s_in, dy_ssd,
                                    name="ssd_bwd")
    dproj, d_ssm_cw_p, d_ssm_cb_p = conv_bwd(dxbc_post, xbc_pre, proj, OFF_XBC, ssm_cw_p, dproj, name="ssm_conv_bwd")
    do_lru = mm(dy_lru, w_out_lru_f, "nt", name="d_o_lru")
    du_lru, dproj, d_w_r, d_w_i, d_b_r, d_b_i, d_lam = lru_bwd(u_lru, proj, h_lru, do_lru, lru_w_r[0], lru_b_r, lru_w_i[0],
                                                               lru_b_i, lru_lambda, dproj, name="lru_bwd")
    dproj, d_lru_cw, d_lru_cb = conv_bwd(du_lru, None, proj, OFF_LX, lru_cw_full, dproj, name="lru_conv_bwd")
    ddt_cols = jnp.transpose(ddtr, (1, 0, 2)).reshape(t, SSM_HEADS).astype(bf16)
    ddt_cols = jnp.pad(ddt_cols, ((0, 0), (0, DT_PAD_W - SSM_HEADS)))
    dproj = lax.dynamic_update_slice(dproj, ddt_cols, (0, OFF_DT))

    d_ssm_cw = _unperm_xbc_cols(d_ssm_cw_p)
    d_ssm_cb = _unperm_xbc_cols(d_ssm_cb_p)
    dpar_h = jnp.transpose(dpar[:, 0:3, :], (1, 0, 2)).reshape(3, SSM_HEADS)
    small_names = ["norm1_w", "b_branch_gate", "ssm_conv_b", "ssm_dt_bias", "ssm_a_log", "ssm_d", "ssm_norm_w",
                   "lru_conv_b", "lru_w_r", "lru_b_r", "lru_w_i", "lru_b_i", "lru_lambda", "norm2_w", "norm_f_w"]
    small_g = dict(norm1_w=jnp.zeros_like(norm1_w), b_branch_gate=d_bg, ssm_conv_b=d_ssm_cb, ssm_dt_bias=dpar_h[0:1], ssm_a_log=dpar_h[1:2],
                   ssm_d=dpar_h[2:3], ssm_norm_w=d_ssm_norm, lru_conv_b=d_lru_cb, lru_w_r=d_w_r[None], lru_b_r=d_b_r,
                   lru_w_i=d_w_i[None], lru_b_i=d_b_i, lru_lambda=d_lam, norm2_w=d_norm2, norm_f_w=d_norm_f.reshape(D_MODEL))
    small_w = dict(norm1_w=norm1_w, b_branch_gate=b_branch_gate, ssm_conv_b=ssm_conv_b, ssm_dt_bias=ssm_dt_bias,
                   ssm_a_log=ssm_a_log, ssm_d=ssm_d, ssm_norm_w=ssm_norm_w, lru_conv_b=lru_conv_b, lru_w_r=lru_w_r,
                   lru_b_r=lru_b_r, lru_w_i=lru_w_i, lru_b_i=lru_b_i, lru_lambda=lru_lambda, norm2_w=norm2_w, norm_f_w=norm_f_w)
    small_m = dict(norm1_w=m_norm1_w, b_branch_gate=m_b_branch_gate, ssm_conv_b=m_ssm_conv_b, ssm_dt_bias=m_ssm_dt_bias,
                   ssm_a_log=m_ssm_a_log, ssm_d=m_ssm_d, ssm_norm_w=m_ssm_norm_w, lru_conv_b=m_lru_conv_b, lru_w_r=m_lru_w_r,
                   lru_b_r=m_lru_b_r, lru_w_i=m_lru_w_i, lru_b_i=m_lru_b_i, lru_lambda=m_lru_lambda, norm2_w=m_norm2_w,
                   norm_f_w=m_norm_f_w)
    small_v = dict(norm1_w=v_norm1_w, b_branch_gate=v_b_branch_gate, ssm_conv_b=v_ssm_conv_b, ssm_dt_bias=v_ssm_dt_bias,
                   ssm_a_log=v_ssm_a_log, ssm_d=v_ssm_d, ssm_norm_w=v_ssm_norm_w, lru_conv_b=v_lru_conv_b, lru_w_r=v_lru_w_r,
                   lru_b_r=v_lru_b_r, lru_w_i=v_lru_w_i, lru_b_i=v_lru_b_i, lru_lambda=v_lru_lambda, norm2_w=v_norm2_w,
                   norm_f_w=v_norm_f_w)
    shapes = [small_w[k].shape for k in small_names]
    conv_shapes = [(4, SSM_CONV_DIM), (4, LRU_WIDTH)]
    g_pack = _pack([small_g[k] for k in small_names] + [d_ssm_cw, d_lru_cw])
    s_send, s_recv, s_arrays, s_token = split_start([g_pack, lax.empty((8,) + g_pack.shape, f32)], g_pack, all_copies(),
                                                    (7,), name="small_start")
    d_w_in_p = mm(hn1, dproj, "tn", after=s_token, name="d_w_in")

    d_w_in_s = _unperm_in_to_shards(d_w_in_p)
    (l_sib,) = pair_exchange([d_w_in_s], name="pair_exchange_late")
    l_pair = pair_add(d_w_in_s, l_sib, idx, name="pair_add_w_in")
    l_land = lax.empty((3,) + l_pair[0].shape[1:], bf16)
    l_send, l_recv, l_arrays, l_token = split_start([l_pair[0], l_land], l_pair[1], reduce_copies(1), (3,),
                                                    name="reduce_late_start")
    grad_x, d_norm1 = mm(dproj, w_in_p, "nt", after=l_token, name="d_hn1",
                         epi=(epi_rms_bwd, [x2, dh1], [norm1_w], [("row", f32), ("vec",)]))

    e_arrays = split_wait(e_send, e_recv, e_arrays, d_norm1, reduce_copies(5), name="reduce_early_wait")
    e_half = [chip_sum(p[1], rb, idx, name="chip_sum_" + k) for p, rb, k in zip(e_pairs, e_arrays[5:], late_names)]
    big_out = {}
    for k, g in zip(late_names, pair_gather(e_half, name="pair_gather_early")):
        big_out[k] = (g,) + tuple(adamw(big_w[k], g, big_m[k], big_v[k], name="adamw_" + k))

    s_arrays = split_wait(s_send, s_recv, s_arrays, d_norm1, all_copies(), name="small_wait")
    g_sum = sum8(lax.dynamic_update_index_in_dim(s_arrays[1], g_pack, 2 * me + ci, 0), name="sum8")
    n1 = jnp.concatenate([d_norm1.reshape(8, 128), loss_tile], axis=0)
    n1_sum = sum8(lax.dynamic_update_index_in_dim(all_exchange(n1, name="all_exchange_norm1"), n1, 2 * me + ci, 0),
                  name="sum8_norm1")
    loss = n1_sum[8, 0]
    g_sum = lax.dynamic_update_slice(g_sum, n1_sum[0:8], (0, 0))
    g_small = _unpack(g_sum, shapes + conv_shapes)
    g_small[-2] = lax.dynamic_slice_in_dim(g_small[-2], me * 768, 768, axis=1)
    g_small[-1] = lax.dynamic_slice_in_dim(g_small[-1], me * 320, 320, axis=1)
    all_names = small_names + ["ssm_conv_w", "lru_conv_w"]
    small_w.update(ssm_conv_w=ssm_conv_w[0], lru_conv_w=lru_conv_w[0])
    small_m.update(ssm_conv_w=m_ssm_conv_w[0], lru_conv_w=m_lru_conv_w[0])
    small_v.update(ssm_conv_w=v_ssm_conv_w[0], lru_conv_w=v_lru_conv_w[0])
    as2d = lambda a: a.reshape(-1, a.shape[-1])
    upd = adamw_many([as2d(small_w[k]) for k in all_names], [as2d(g) for g in g_small],
                     [as2d(small_m[k]) for k in all_names], [as2d(small_v[k]) for k in all_names], name="adamw_small")
    small_out = {}
    for k, g, u in zip(all_names, g_small, upd):
        small_out[k] = (g,) + tuple(o.reshape(g.shape) for o in u)
    l_arrays = split_wait(l_send, l_recv, l_arrays, upd[0][0], reduce_copies(1), name="reduce_late_wait")
    l_half = chip_sum(l_pair[1], l_arrays[1], idx, name="chip_sum_w_in")
    (g_w_in,) = pair_gather([l_half], name="pair_gather_late")
    big_out["w_in"] = (g_w_in,) + tuple(adamw(big_w["w_in"], g_w_in, big_m["w_in"], big_v["w_in"], name="adamw_w_in"))

    order = ["norm1_w", "w_in", "b_branch_gate", "ssm_conv_w", "ssm_conv_b", "ssm_dt_bias", "ssm_a_log", "ssm_d", "ssm_norm_w",
             "w_out_ssm", "lru_conv_w", "lru_conv_b", "lru_w_r", "lru_b_r", "lru_w_i", "lru_b_i", "lru_lambda", "w_out_lru",
             "w_out", "norm2_w", "w_ffn_in", "w_ffn_out", "norm_f_w"]
    outs = [loss, grad_x[None]]
    for which in range(4):
        for k in order:
            if k in big_out:
                outs.append(big_out[k][which][None])
            elif k in ("ssm_conv_w", "lru_conv_w"):
                outs.append(small_out[k][which][None])
            else:
                outs.append(small_out[k][which])
    return tuple(outs)
```

```python
import math

import jax
import jax.numpy as jnp
import numpy as np
from jax import lax
from jax.experimental import pallas as pl
from jax.experimental.pallas import tpu as pltpu

f32 = jnp.float32
bf16 = jnp.bfloat16

D_MODEL = 1024
SSM_D_INNER = 2048
SSM_HEADS = 32
SSM_HEAD_DIM = 64
SSM_GROUPS = 4
SSM_HPG = 8
SSM_D_STATE = 128
SSM_CHUNK = 128
SSM_GROUP_W = 512
SSM_CONV_DIM = 3072
XBC_GROUP_W = 768
LRU_WIDTH = 1280
LRU_BLOCKS = 10
LRU_BLOCK = 128
LRU_C = 8.0
FFN_HIDDEN = 2816
RMS_EPS = 1e-6
IN_PROJ_DIM = 9760
N_CHIPS = 4

OFF_GATES = 0
OFF_Z = 2048
OFF_LX = 4096
OFF_LY = 5376
OFF_DT = 6656
DT_PAD_W = 256
OFF_XBC = 6912
PROJ_W = 9984

ADAM_LR = 0.001
ADAM_B1 = 0.9
ADAM_B2 = 0.999
ADAM_EPS = 1e-08
ADAM_WD = 0.01
ADAM_STEP = 10

MESH = pl.DeviceIdType.MESH
ANY = pl.BlockSpec(memory_space=pl.ANY)

NN = (((1,), (0,)), ((), ()))
NT = (((1,), (1,)), ((), ()))
TN = (((0,), (0,)), ((), ()))


def _pick(n, cap, mult=128):
    best = None
    for t in range(mult, min(n, cap) + 1, mult):
        if n % t == 0:
            best = t
    return best if best is not None else n


def _sigmoid(x):
    return 0.5 * jnp.tanh(0.5 * x) + 0.5


def _softplus(x):
    return jnp.maximum(x, 0.0) + jnp.log(1.0 + jnp.exp(-jnp.abs(x)))


def _silu(x):
    return x * _sigmoid(x)


def _dsilu(x):
    s = _sigmoid(x)
    return s * (1.0 + x * (1.0 - s))


_GELU_K = math.sqrt(2.0 / math.pi)


def _gelu(x):
    return 0.5 * x * (1.0 + jnp.tanh(_GELU_K * (x + 0.044715 * x * x * x)))


def _dgelu(x):
    t = jnp.tanh(_GELU_K * (x + 0.044715 * x * x * x))
    return 0.5 * (1.0 + t) + 0.5 * x * (1.0 - t * t) * _GELU_K * (1.0 + 3.0 * 0.044715 * x * x)


def _expm1(x):
    poly = x * (1.0 + x * (0.5 + x * (1.0 / 6.0 + x * (1.0 / 24.0 + x * (1.0 / 120.0 + x * (1.0 / 720.0))))))
    return jnp.where(jnp.abs(x) < 0.1, poly, jnp.exp(x) - 1.0)


def _dot(a, b, dn):
    return lax.dot_general(a.astype(bf16), b.astype(bf16), dn, preferred_element_type=f32)


def _dot_01(a, b, dn, split, terms):
    r = a if split == 0 else b
    out = None
    for _ in range(terms):
        h = r.astype(bf16)
        r = r - h.astype(f32)
        d = lax.dot_general(h if split == 0 else a.astype(bf16), b.astype(bf16) if split == 0 else h, dn,
                            preferred_element_type=f32)
        out = d if out is None else out + d
    return out


MM_VMEM_BUDGET = 48 * 2 ** 20

def mm(a, b, mode, *, name, add=None, after=None, out_dtype=f32, b_shards=False, out_shards=0, epi=None):
    bs = b.shape[1:] if b_shards else b.shape
    shard_w = b.shape[2] if b_shards else None
    bcols = bs[1] * (b.shape[0] if b_shards else 1)
    if mode == "nn":
        (m, k), (k2, n) = a.shape, (bs[0], bcols)
    elif mode == "nt":
        (m, k), (n, k2) = a.shape, (bs[0], bcols)
    else:
        (k, m), (k2, n) = a.shape, b.shape
    assert k == k2, (a.shape, b.shape, mode)
    tn = _pick(n, 1536)
    if b_shards and mode == "nn":
        tn = shard_w
    if out_shards:
        tn = n // out_shards
    isz = lambda v: jnp.dtype(v.dtype).itemsize
    if epi is not None:
        assert n <= 1536 and not out_shards
        tn = n
        epi_fn, epi_rows, epi_vecs, epi_outs = epi
        tile_bytes = sum(isz(v) for v in epi_rows) + sum(jnp.dtype(o[1]).itemsize for o in epi_outs if o[0] == "row")
    else:
        epi_rows, epi_vecs, epi_outs = [], [], []
        tile_bytes = jnp.dtype(out_dtype).itemsize
    tks = [shard_w] if (b_shards and mode == "nt") else sorted({k, _pick(k, 3328), _pick(k, 2048), _pick(k, 1024)}, reverse=True)

    def vmem_of(tm, tk):
        blocks = tm * tk * isz(a) + tk * tn * isz(b) + tm * tn * (4 * int(add is not None) + tile_bytes)
        return 2 * blocks + 4 * tm * tn * int(k > tk)

    fits = [(tk, tm) for tk in tks for tm in (_pick(m, 1536), _pick(m, 1024), _pick(m, 512)) if vmem_of(tm, tk) <= MM_VMEM_BUDGET]
    tk, tm = fits[0] if fits else (tks[-1], _pick(m, 256))
    if (m // tm) * (n // tn) * (k // tk) < 8 and tm >= 1024:
        tm = _pick(m, tm // 2)
    nk = k // tk
    dn = {"nn": NN, "nt": NT, "tn": TN}[mode]
    a_spec = pl.BlockSpec((tk, tm), lambda i, j, kk: (kk, i)) if mode == "tn" else pl.BlockSpec((tm, tk), lambda i, j, kk: (i, kk))
    b_spec = pl.BlockSpec((tn, tk), lambda i, j, kk: (j, kk)) if mode == "nt" else pl.BlockSpec((tk, tn), lambda i, j, kk: (kk, j))
    if b_shards:
        b_spec = (pl.BlockSpec((None, tn, tk), lambda i, j, kk: (kk, j, 0)) if mode == "nt"
                  else pl.BlockSpec((None, tk, tn), lambda i, j, kk: (j, kk, 0)))
    o_spec = pl.BlockSpec((tm, tn), lambda i, j, kk: (i, j))
    out_shape = jax.ShapeDtypeStruct((m, n), out_dtype)
    if out_shards:
        assert add is None
        o_spec = pl.BlockSpec((None, tm, tn), lambda i, j, kk: (j, i, 0))
        out_shape = jax.ShapeDtypeStruct((out_shards, m, tn), out_dtype)
    has_add = add is not None

    n_extra = int(has_add) + int(after is not None)
    n_rows, n_vecs, n_outs = len(epi_rows), len(epi_vecs), len(epi_outs)

    def body(a_ref, b_ref, *rest):
        add_ref = rest[0] if has_add else None
        o_ref = rest[n_extra]

        def finish(r):
            if has_add:
                r = r + add_ref[...]
            if epi is None:
                o_ref[...] = r.astype(out_dtype)
            else:
                e = rest[n_extra:]
                epi_fn(r, e[:n_rows], e[n_rows:n_rows + n_vecs], e[n_rows + n_vecs:n_rows + n_vecs + n_outs],
                       pl.program_id(0) == 0)

        if nk == 1:
            finish(_dot(a_ref[...], b_ref[...], dn))
            return
        acc = rest[-1]
        kk = pl.program_id(2)

        @pl.when(kk == 0)
        def _():
            acc[...] = jnp.zeros_like(acc)

        acc[...] += _dot(a_ref[...], b_ref[...], dn)

        @pl.when(kk == nk - 1)
        def _():
            finish(acc[...])

    ins = [a, b] + ([add] if has_add else []) + ([after] if after is not None else [])
    in_specs = [a_spec, b_spec] + ([o_spec] if has_add else []) + ([ANY] if after is not None else [])
    sem0 = "parallel"
    if epi is not None:
        vec_spec = pl.BlockSpec((1, tn), lambda i, j, kk: (0, 0))
        ins += list(epi_rows) + list(epi_vecs)
        in_specs += [o_spec] * n_rows + [vec_spec] * n_vecs
        o_spec, out_shape = [], []
        for o in epi_outs:
            if o[0] == "row":
                o_spec.append(pl.BlockSpec((tm, tn), lambda i, j, kk: (i, j)))
                out_shape.append(jax.ShapeDtypeStruct((m, n), o[1]))
            elif o[0] == "vec":
                o_spec.append(vec_spec)
                out_shape.append(jax.ShapeDtypeStruct((1, n), f32))
                sem0 = "arbitrary"
            else:
                o_spec.append(pl.BlockSpec((8, 128), lambda i, j, kk: (0, 0)))
                out_shape.append(jax.ShapeDtypeStruct((8, 128), f32))
                sem0 = "arbitrary"
    return pl.pallas_call(
        body, name=name, grid=(m // tm, n // tn, nk), in_specs=in_specs, out_specs=o_spec, out_shape=out_shape,
        scratch_shapes=[pltpu.VMEM((tm, tn), f32)] if nk > 1 else [],
        compiler_params=pltpu.CompilerParams(dimension_semantics=(sem0, sem0, "arbitrary")),
    )(*ins)


def rms_fwd(x, w, *, name):
    t, d = x.shape
    tr = _pick(t, 512, 8)

    def body(x_ref, w_ref, o_ref):
        xv = x_ref[...]
        r = lax.rsqrt(jnp.mean(xv * xv, axis=-1, keepdims=True) + RMS_EPS)
        o_ref[...] = (xv * r * w_ref[...]).astype(bf16)

    return pl.pallas_call(
        body, name=name, grid=(t // tr,),
        in_specs=[pl.BlockSpec((tr, d), lambda i: (i, 0)), pl.BlockSpec((1, d), lambda i: (0, 0))],
        out_specs=pl.BlockSpec((tr, d), lambda i: (i, 0)), out_shape=jax.ShapeDtypeStruct((t, d), bf16),
    )(x, w)


def _rms_bwd_math(xv, wv, dy):
    r = lax.rsqrt(jnp.mean(xv * xv, axis=-1, keepdims=True) + RMS_EPS)
    g = dy * wv
    dx = r * g - xv * (r * r * r) * jnp.mean(g * xv, axis=-1, keepdims=True)
    dw = jnp.sum(dy * xv * r, axis=0, keepdims=True)
    return dx, dw


def epi_rms_fwd(r, rows, vecs, outs, first):
    outs[0][...] = r
    rr = lax.rsqrt(jnp.mean(r * r, axis=-1, keepdims=True) + RMS_EPS)
    outs[1][...] = (r * rr * vecs[0][...]).astype(bf16)


def epi_rms_bwd(r, rows, vecs, outs, first):
    dx, dw = _rms_bwd_math(rows[0][...], vecs[0][...], r)
    dx = dx + rows[1][...]
    outs[0][...] = dx
    if len(outs) == 3:
        outs[1][...] = dx.astype(bf16)
    dw_ref = outs[-1]

    @pl.when(first)
    def _():
        dw_ref[...] = jnp.zeros_like(dw_ref)

    dw_ref[...] += dw


def epi_loss(r, rows, vecs, outs, first):
    wv = vecs[0][...]
    rr = lax.rsqrt(jnp.mean(r * r, axis=-1, keepdims=True) + RMS_EPS)
    err = r * rr * wv - rows[0][...]
    part = 0.5 * jnp.sum(jnp.mean(err * err, axis=-1, keepdims=True), axis=0, keepdims=True)
    dx, dw = _rms_bwd_math(r, wv, err * (1.0 / r.shape[-1]))
    outs[0][...] = dx
    outs[1][...] = dx.astype(bf16)

    @pl.when(first)
    def _():
        outs[2][...] = jnp.zeros_like(outs[2])
        outs[3][...] = jnp.zeros_like(outs[3])

    outs[2][...] += dw
    outs[3][...] += part


CONV_ROWS = 1024
VREG_ELEMS = 8 * 128


def _conv_chunk(tc):
    return 16 if (16 + 8) * tc * 3 > 48 * VREG_ELEMS else 32


def conv_fwd(src, col0, width, w, b, *, silu, name):
    t = src.shape[0]
    tc = _pick(math.gcd(width, col0), 768)
    assert col0 % tc == 0
    cb = col0 // tc
    r = CONV_ROWS
    ch = _conv_chunk(tc)

    def body(u_ref, w_ref, b_ref, *rest):
        ext = rest[-1]
        j = pl.program_id(1)

        @pl.when(j == 0)
        def _():
            ext[0:8, :] = jnp.zeros((8, tc), f32)

        @pl.when(j > 0)
        def _():
            ext[0:8, :] = ext[r:r + 8, :]

        ext[8:r + 8, :] = u_ref[...]
        wv = w_ref[...]
        bv = b_ref[...]

        def chunk(c, carry):
            r0 = pl.multiple_of(c * ch, ch)
            v = ext[pl.ds(r0, ch + 8), :]
            acc = bv + wv[3:4, :] * v[8:, :]
            for s in (1, 2, 3):
                acc = acc + wv[3 - s:4 - s, :] * pltpu.roll(v, s, 0)[8:, :]
            rest[0][pl.ds(r0, ch), :] = acc
            if silu:
                rest[1][pl.ds(r0, ch), :] = _silu(acc)
            return carry

        lax.fori_loop(0, r // ch, chunk, 0)

    tile = pl.BlockSpec((r, tc), lambda c, j: (j, c))
    n_out = 2 if silu else 1
    return pl.pallas_call(
        body, name=name, grid=(width // tc, t // r),
        in_specs=[pl.BlockSpec((r, tc), lambda c, j: (j, cb + c)), pl.BlockSpec((4, tc), lambda c, j: (0, c)),
                  pl.BlockSpec((1, tc), lambda c, j: (0, c))],
        out_specs=[tile] * n_out, out_shape=[jax.ShapeDtypeStruct((t, width), f32)] * n_out,
        scratch_shapes=[pltpu.VMEM((r + 8, tc), f32)],
        compiler_params=pltpu.CompilerParams(dimension_semantics=("parallel", "arbitrary")),
    )(src, w, b)


def conv_bwd(dpost, pre, src, col0, w, dst, *, name):
    t, width = dpost.shape
    tc = _pick(math.gcd(width, col0), 768)
    assert col0 % tc == 0
    cb = col0 // tc
    r = CONV_ROWS
    ch = _conv_chunk(tc)
    nt = t // r
    has_pre = pre is not None

    def body(*refs):
        refs = refs[1:]
        if has_pre:
            d_ref, p_ref, u_ref, w_ref, du_ref, dw_ref, db_ref, ext = refs
        else:
            d_ref, u_ref, w_ref, du_ref, dw_ref, db_ref, ext = refs
        j = pl.program_id(1)

        @pl.when(j == 0)
        def _():
            ext[r:r + 8, :] = jnp.zeros((8, tc), f32)
            dw_ref[...] = jnp.zeros_like(dw_ref)
            db_ref[...] = jnp.zeros_like(db_ref)

        @pl.when(j > 0)
        def _():
            ext[r:r + 8, :] = ext[0:8, :]

        dpre = d_ref[...]
        if has_pre:
            dpre = dpre * _dsilu(p_ref[...])
        ext[0:r, :] = dpre
        wv = w_ref[...]

        def fold(p):
            out = p[0:8, :]
            for i in range(1, ch // 8):
                out = out + p[8 * i:8 * i + 8, :]
            return out

        def chunk(c, sums):
            r0 = pl.multiple_of(c * ch, ch)
            v = ext[pl.ds(r0, ch + 8), :]
            uv = u_ref[pl.ds(r0, ch), :]
            d0 = v[0:ch, :]
            du = wv[3:4, :] * d0
            new = [None] * 5
            new[3] = sums[3] + fold(d0 * uv)
            for s in (1, 2, 3):
                sh = pltpu.roll(v, ch + 8 - s, 0)[0:ch, :]
                du = du + wv[3 - s:4 - s, :] * sh
                new[3 - s] = sums[3 - s] + fold(sh * uv)
            new[4] = sums[4] + fold(d0)
            du_ref[pl.ds(r0, ch), :] = du.astype(bf16)
            return tuple(new)

        sums = lax.fori_loop(0, r // ch, chunk, tuple(jnp.zeros((8, tc), f32) for _ in range(5)))
        for k in range(4):
            dw_ref[k:k + 1, :] += jnp.sum(sums[k], axis=0, keepdims=True)
        db_ref[...] += jnp.sum(sums[4], axis=0, keepdims=True)

    rev = pl.BlockSpec((r, tc), lambda c, j: (nt - 1 - j, c))
    win = pl.BlockSpec((r, tc), lambda c, j: (nt - 1 - j, cb + c))
    in_specs = [ANY, rev] + ([rev] if has_pre else []) + [win, pl.BlockSpec((4, tc), lambda c, j: (0, c))]
    ins = [dst, dpost] + ([pre] if has_pre else []) + [src, w]
    return pl.pallas_call(
        body, name=name, grid=(width // tc, nt), in_specs=in_specs,
        out_specs=[win, pl.BlockSpec((4, tc), lambda c, j: (0, c)), pl.BlockSpec((1, tc), lambda c, j: (0, c))],
        out_shape=[jax.ShapeDtypeStruct(dst.shape, bf16), jax.ShapeDtypeStruct((4, width), f32),
                   jax.ShapeDtypeStruct((1, width), f32)],
        input_output_aliases={0: 0},
        scratch_shapes=[pltpu.VMEM((r + 8, tc), f32)],
        compiler_params=pltpu.CompilerParams(dimension_semantics=("parallel", "arbitrary")),
    )(*ins)


def _ssd_common(xbc_ref, dtr_ref, dtrT_ref, par_row_ref, par_col_ref):
    l = SSM_CHUNK
    x = xbc_ref[:, 0:SSM_GROUP_W]
    bm = xbc_ref[:, SSM_GROUP_W:SSM_GROUP_W + SSM_D_STATE]
    cm = xbc_ref[:, SSM_GROUP_W + SSM_D_STATE:XBC_GROUP_W]
    par_row = par_row_ref[0]
    par_col = par_col_ref[0]
    bias_row, alog_row = par_row[0:1, :], par_row[1:2, :]
    bias_col, alog_col = par_col[:, 0:1], par_col[:, 1:2]
    dtr = dtr_ref[0]
    dt = _softplus(dtr + bias_row)
    dt_t = _softplus(dtrT_ref[0] + bias_col)
    a_row = -jnp.exp(alog_row)
    a_col = -jnp.exp(alog_col)
    li = lax.broadcasted_iota(jnp.int32, (l, l), 0)
    si = lax.broadcasted_iota(jnp.int32, (l, l), 1)
    tri = (li >= si).astype(f32)
    cs = _dot_01(tri, dt * a_row, NN, 1, 3)
    cs_t = _dot_01(dt_t * a_col, tri, NT, 0, 3)
    off = lax.broadcasted_iota(jnp.int32, (SSM_HPG, SSM_GROUP_W), 1) - SSM_HEAD_DIM * lax.broadcasted_iota(
        jnp.int32, (SSM_HPG, SSM_GROUP_W), 0)
    ex = ((off >= 0) & (off < SSM_HEAD_DIM)).astype(f32)
    cs_x = _dot_01(cs, ex, NN, 0, 3)
    cl_x = cs_x[l - 1:l, :]
    return dict(x=x, bm=bm, cm=cm, dtr=dtr, dt=dt, a_row=a_row, bias_row=bias_row, tri=tri, li=li, si=si, cs=cs,
                cs_t=cs_t, ex=ex, dt_x=_dot_01(dt, ex, NN, 0, 2), d_x=_dot_01(par_row, ex, NN, 0, 2)[2:3, :], e_x=jnp.exp(cs_x),
                el_x=jnp.exp(cl_x), dec_x=jnp.exp(cl_x - cs_x))


def ssd_fwd(xbc, dtr, dtr_t, par_row, par_col, *, name):
    t = xbc.shape[0]
    nc = t // SSM_CHUNK
    l, p = SSM_CHUNK, SSM_HEAD_DIM

    def body(xbc_ref, dtr_ref, dtrT_ref, prow_ref, pcol_ref, y_ref, sin_ref, state):
        @pl.when(pl.program_id(1) == 0)
        def _():
            state[...] = jnp.zeros_like(state)

        q = _ssd_common(xbc_ref, dtr_ref, dtrT_ref, prow_ref, pcol_ref)
        st = state[...]
        sin_ref[0] = st
        xd = q["x"] * q["dt_x"]
        g = _dot(q["cm"], q["bm"], NT)
        for r in range(SSM_HPG):
            sl = slice(r * p, (r + 1) * p)
            diff = q["cs"][:, r:r + 1] - q["cs_t"][r:r + 1, :]
            lm = jnp.where(q["li"] >= q["si"], jnp.exp(jnp.minimum(diff, 0.0)), 0.0)
            y_ref[:, sl] = _dot(g * lm, xd[:, sl], NN)
        y_ref[...] += q["e_x"] * _dot(q["cm"], st, NN) + q["d_x"] * q["x"]
        state[...] = q["el_x"] * st + _dot(q["bm"].T, xd * q["dec_x"], NN)

    return pl.pallas_call(
        body, name=name, grid=(SSM_GROUPS, nc),
        in_specs=[pl.BlockSpec((l, XBC_GROUP_W), lambda g, c: (c, g)),
                  pl.BlockSpec((1, l, SSM_HPG), lambda g, c: (g, c, 0)),
                  pl.BlockSpec((1, SSM_HPG, l), lambda g, c: (g, 0, c)),
                  pl.BlockSpec((1, 8, 8), lambda g, c: (g, 0, 0)),
                  pl.BlockSpec((1, 8, 8), lambda g, c: (g, 0, 0))],
        out_specs=[pl.BlockSpec((l, SSM_GROUP_W), lambda g, c: (c, g)),
                   pl.BlockSpec((1, SSM_D_STATE, SSM_GROUP_W), lambda g, c: (c, 0, g))],
        out_shape=[jax.ShapeDtypeStruct((t, SSM_D_INNER), f32),
                   jax.ShapeDtypeStruct((nc, SSM_D_STATE, SSM_D_INNER), f32)],
        scratch_shapes=[pltpu.VMEM((SSM_D_STATE, SSM_GROUP_W), f32)],
        compiler_params=pltpu.CompilerParams(dimension_semantics=("parallel", "arbitrary")),
    )(xbc, dtr, dtr_t, par_row, par_col)


def ssd_bwd(xbc, dtr, dtr_t, par_row, par_col, s_in, dy, *, name):
    t = xbc.shape[0]
    nc = t // SSM_CHUNK
    l, p = SSM_CHUNK, SSM_HEAD_DIM

    def body(xbc_ref, dtr_ref, dtrT_ref, prow_ref, pcol_ref, sin_ref, dy_ref, dxbc_ref, ddtr_ref, dpar_ref,
             dstate, yd_buf, dxd_buf):
        @pl.when(pl.program_id(1) == 0)
        def _():
            dstate[...] = jnp.zeros_like(dstate)
            dpar_ref[...] = jnp.zeros_like(dpar_ref)

        q = _ssd_common(xbc_ref, dtr_ref, dtrT_ref, prow_ref, pcol_ref)
        x, bm, cm, ex, li, si = q["x"], q["bm"], q["cm"], q["ex"], q["li"], q["si"]
        e_x, el_x, dec_x = q["e_x"], q["el_x"], q["dec_x"]
        st = sin_ref[0]
        dst = dstate[...]
        dy = dy_ref[...]
        xd = x * q["dt_x"]
        g = _dot(cm, bm, NT)
        dg = jnp.zeros((l, l), f32)
        for r in range(SSM_HPG):
            sl = slice(r * p, (r + 1) * p)
            diff = q["cs"][:, r:r + 1] - q["cs_t"][r:r + 1, :]
            lm = jnp.where(li >= si, jnp.exp(jnp.minimum(diff, 0.0)), 0.0)
            m = (g * lm).astype(bf16)
            xdh, dyh = xd[:, sl].astype(bf16), dy[:, sl].astype(bf16)
            yd_buf[:, sl] = _dot(m, xdh, NN)
            dxd_buf[:, sl] = _dot(m, dyh, TN)
            dg = dg + _dot(dyh, xdh, NT) * lm
        yd, dxd_diag = yd_buf[...], dxd_buf[...]
        yo = e_x * _dot(cm, st, NN)
        dz = e_x * dy
        wv = _dot(bm, dst, NN)
        xw = xd * wv * dec_x
        row8 = lax.broadcasted_iota(jnp.int32, (l, SSM_HPG), 0)
        dy_b, xd_b = dy.astype(bf16).astype(f32), xd.astype(bf16).astype(f32)
        dcs = _dot_01(dy_b * yd - xd_b * dxd_diag + dy * yo - xw, ex, NT, 0, 3)
        tail = jnp.sum(xw, axis=0, keepdims=True) + el_x * jnp.sum(dst * st, axis=0, keepdims=True)
        dcl = _dot_01(jnp.broadcast_to(tail, (SSM_HPG, SSM_GROUP_W)), ex, NT, 0, 3)[0:1, :]
        dcs = dcs + jnp.where(row8 == l - 1, dcl, 0.0)
        dda = _dot_01(q["tri"], dcs, TN, 1, 3)
        dxd = dxd_diag + dec_x * wv
        ddt = _dot_01(dxd * x, ex, NT, 0, 3) + dda * q["a_row"]
        ddtr = ddt * _sigmoid(q["dtr"] + q["bias_row"])
        ddtr_ref[0] = ddtr
        dd = _dot_01(jnp.broadcast_to(jnp.sum(dy * x, axis=0, keepdims=True), (SSM_HPG, SSM_GROUP_W)), ex, NT, 0, 2)[0:1, :]
        dpar_ref[0, 0:1, :] += jnp.sum(ddtr, axis=0, keepdims=True)
        dpar_ref[0, 1:2, :] += jnp.sum(dda * q["dt"], axis=0, keepdims=True) * q["a_row"]
        dpar_ref[0, 2:3, :] += dd
        dxbc_ref[:, 0:SSM_GROUP_W] = dxd * q["dt_x"] + q["d_x"] * dy
        dxbc_ref[:, SSM_GROUP_W:SSM_GROUP_W + SSM_D_STATE] = _dot(dg, cm, TN) + _dot(xd * dec_x, dst, NT)
        dxbc_ref[:, SSM_GROUP_W + SSM_D_STATE:XBC_GROUP_W] = _dot(dg, bm, NN) + _dot(dz, st, NT)
        dstate[...] = _dot(cm.T, dz, NN) + el_x * dst

    rc = lambda c: nc - 1 - c
    return pl.pallas_call(
        body, name=name, grid=(SSM_GROUPS, nc),
        in_specs=[pl.BlockSpec((l, XBC_GROUP_W), lambda g, c: (rc(c), g)),
                  pl.BlockSpec((1, l, SSM_HPG), lambda g, c: (g, rc(c), 0)),
                  pl.BlockSpec((1, SSM_HPG, l), lambda g, c: (g, 0, rc(c))),
                  pl.BlockSpec((1, 8, 8), lambda g, c: (g, 0, 0)),
                  pl.BlockSpec((1, 8, 8), lambda g, c: (g, 0, 0)),
                  pl.BlockSpec((1, SSM_D_STATE, SSM_GROUP_W), lambda g, c: (rc(c), 0, g)),
                  pl.BlockSpec((l, SSM_GROUP_W), lambda g, c: (rc(c), g))],
        out_specs=[pl.BlockSpec((l, XBC_GROUP_W), lambda g, c: (rc(c), g)),
                   pl.BlockSpec((1, l, SSM_HPG), lambda g, c: (g, rc(c), 0)),
                   pl.BlockSpec((1, 8, 8), lambda g, c: (g, 0, 0))],
        out_shape=[jax.ShapeDtypeStruct((t, SSM_CONV_DIM), f32),
                   jax.ShapeDtypeStruct((SSM_GROUPS, t, SSM_HPG), f32),
                   jax.ShapeDtypeStruct((SSM_GROUPS, 8, 8), f32)],
        scratch_shapes=[pltpu.VMEM((SSM_D_STATE, SSM_GROUP_W), f32), pltpu.VMEM((l, SSM_GROUP_W), f32),
                        pltpu.VMEM((l, SSM_GROUP_W), f32)],
        compiler_params=pltpu.CompilerParams(dimension_semantics=("parallel", "arbitrary")),
    )(xbc, dtr, dtr_t, par_row, par_col, s_in, dy)


def gnorm_fwd(y, proj, w, *, name):
    t = y.shape[0]
    tr = _pick(t, 2048, 8)
    gw = SSM_GROUP_W
    zb = OFF_Z // gw

    def body(y_ref, z_ref, w_ref, o_ref):
        y2 = y_ref[...] * _silu(z_ref[...])
        r = lax.rsqrt(jnp.mean(y2 * y2, axis=-1, keepdims=True) + RMS_EPS)
        o_ref[...] = (y2 * r * w_ref[...]).astype(bf16)

    return pl.pallas_call(
        body, name=name, grid=(SSM_GROUPS, t // tr),
        in_specs=[pl.BlockSpec((tr, gw), lambda g, i: (i, g)), pl.BlockSpec((tr, gw), lambda g, i: (i, zb + g)),
                  pl.BlockSpec((1, gw), lambda g, i: (0, g))],
        out_specs=pl.BlockSpec((tr, gw), lambda g, i: (i, g)), out_shape=jax.ShapeDtypeStruct((t, SSM_D_INNER), bf16),
    )(y, proj, w)


def gnorm_bwd(y, proj, w, dout, dst, *, name):
    t = y.shape[0]
    tr = _pick(t, 2048, 8)
    gw = SSM_GROUP_W
    zb = OFF_Z // gw

    def body(_, y_ref, z_ref, w_ref, do_ref, dy_ref, dz_ref, dw_ref):
        yv, zv = y_ref[...], z_ref[...]
        sz = _silu(zv)
        y2 = yv * sz
        dy2, dw = _rms_bwd_math(y2, w_ref[...], do_ref[...].astype(f32))
        dy_ref[...] = dy2 * sz
        dz_ref[...] = (dy2 * yv * _dsilu(zv)).astype(bf16)

        @pl.when(pl.program_id(1) == 0)
        def _():
            dw_ref[...] = jnp.zeros_like(dw_ref)

        dw_ref[...] += dw

    tile = pl.BlockSpec((tr, gw), lambda g, i: (i, g))
    vec = pl.BlockSpec((1, gw), lambda g, i: (0, g))
    return pl.pallas_call(
        body, name=name, grid=(SSM_GROUPS, t // tr),
        in_specs=[ANY, tile, pl.BlockSpec((tr, gw), lambda g, i: (i, zb + g)), vec, tile],
        out_specs=[tile, pl.BlockSpec((tr, gw), lambda g, i: (i, zb + g)), vec],
        out_shape=[jax.ShapeDtypeStruct((t, SSM_D_INNER), f32), jax.ShapeDtypeStruct(dst.shape, bf16),
                   jax.ShapeDtypeStruct((1, SSM_D_INNER), f32)],
        input_output_aliases={0: 1},
        compiler_params=pltpu.CompilerParams(dimension_semantics=("parallel", "arbitrary")),
    )(dst, y, proj, w, dout)


LRU_ROWS = 2048


def _lru_gates(uv, wr_ref, wi_ref, br_ref, bi_ref, lam_ref):
    rg = _sigmoid(_dot(uv, wr_ref[0], NN) + br_ref[...])
    ig = _sigmoid(_dot(uv, wi_ref[0], NN) + bi_ref[...])
    sp = _softplus(-lam_ref[...])
    la = -LRU_C * rg * sp
    a = jnp.exp(la)
    s = jnp.sqrt(jnp.maximum(-_expm1(2.0 * la), 0.0))
    return rg, ig, sp, la, a, s


def lru_fwd(u, proj, w_r, b_r, w_i, b_i, lam, *, name):
    t = u.shape[0]
    r = LRU_ROWS
    lb = LRU_BLOCK
    yb = OFF_LY // lb

    def body(u_ref, y_ref, wr_ref, br_ref, wi_ref, bi_ref, lam_ref, h_ref, o_ref, carry):
        @pl.when(pl.program_id(1) == 0)
        def _():
            carry[...] = jnp.zeros_like(carry)

        uv = u_ref[...]
        _, ig, _, _, a, s = _lru_gates(uv, wr_ref, wi_ref, br_ref, bi_ref, lam_ref)
        b = s * ig * uv
        row = lax.broadcasted_iota(jnp.int32, (r, lb), 0)
        d = 1
        while d < r:
            keep = row >= d
            b = b + a * jnp.where(keep, pltpu.roll(b, d, 0), 0.0)
            a = a * jnp.where(keep, pltpu.roll(a, d, 0), 1.0)
            d *= 2
        h = b + a * carry[0:1, :]
        carry[0:1, :] = h[r - 1:r, :]
        h_ref[...] = h
        o_ref[...] = (h * _gelu(y_ref[...])).astype(bf16)

    tile = pl.BlockSpec((r, lb), lambda hb, j: (j, hb))
    vec = pl.BlockSpec((1, lb), lambda hb, j: (0, hb))
    wsp = pl.BlockSpec((1, lb, lb), lambda hb, j: (hb, 0, 0))
    return pl.pallas_call(
        body, name=name, grid=(LRU_BLOCKS, t // r),
        in_specs=[tile, pl.BlockSpec((r, lb), lambda hb, j: (j, yb + hb)), wsp, vec, wsp, vec, vec],
        out_specs=[tile, tile],
        out_shape=[jax.ShapeDtypeStruct((t, LRU_WIDTH), f32), jax.ShapeDtypeStruct((t, LRU_WIDTH), bf16)],
        scratch_shapes=[pltpu.VMEM((8, lb), f32)],
        compiler_params=pltpu.CompilerParams(dimension_semantics=("parallel", "arbitrary")),
    )(u, proj, w_r, b_r, w_i, b_i, lam)


def lru_bwd(u, proj, hseq, dout, w_r, b_r, w_i, b_i, lam, dst, *, name):
    t = u.shape[0]
    r = LRU_ROWS
    nt = t // r
    lb = LRU_BLOCK
    yb = OFF_LY // lb

    def body(_, u_ref, y_ref, h_ref, hp_ref, do_ref, wr_ref, br_ref, wi_ref, bi_ref, lam_ref,
             du_ref, dy_ref, dwr_ref, dwi_ref, dbr_ref, dbi_ref, dlam_ref, carry_dh, carry_a):
        j = pl.program_id(1)

        @pl.when(j == 0)
        def _():
            carry_dh[...] = jnp.zeros_like(carry_dh)
            carry_a[...] = jnp.zeros_like(carry_a)
            dwr_ref[...] = jnp.zeros_like(dwr_ref)
            dwi_ref[...] = jnp.zeros_like(dwi_ref)
            dbr_ref[...] = jnp.zeros_like(dbr_ref)
            dbi_ref[...] = jnp.zeros_like(dbi_ref)
            dlam_ref[...] = jnp.zeros_like(dlam_ref)

        uv = u_ref[...]
        yv = y_ref[...]
        hv = h_ref[...]
        dov = do_ref[...]
        rg, ig, sp, la, a, s = _lru_gates(uv, wr_ref, wi_ref, br_ref, bi_ref, lam_ref)
        dy_ref[...] = (dov * hv * _dgelu(yv)).astype(bf16)
        gq = dov * _gelu(yv)
        row = lax.broadcasted_iota(jnp.int32, (r, lb), 0)
        an = jnp.where(row < r - 1, pltpu.roll(a, r - 1, 0), carry_a[0:1, :])
        d = 1
        while d < r:
            keep = row < r - d
            gq = gq + an * jnp.where(keep, pltpu.roll(gq, r - d, 0), 0.0)
            an = an * jnp.where(keep, pltpu.roll(an, r - d, 0), 1.0)
            d *= 2
        dh = gq + an * carry_dh[0:1, :]
        carry_dh[0:1, :] = dh[0:1, :]
        carry_a[0:1, :] = a[0:1, :]
        first = jnp.where(j == nt - 1, 0.0, 1.0) * hp_ref[7:8, :]
        hprev = jnp.where(row >= 1, pltpu.roll(hv, 1, 0), first)
        da = dh * hprev
        iu = ig * uv
        e2 = jnp.exp(2.0 * la)
        dla = da * a - dh * iu * e2 / jnp.maximum(s, 1e-30)
        drp = dla * (-LRU_C * sp) * rg * (1.0 - rg)
        dip = dh * s * uv * ig * (1.0 - ig)
        dlam_ref[...] += jnp.sum(dla * (LRU_C * rg) * _sigmoid(-lam_ref[...]), axis=0, keepdims=True)
        du_ref[...] = dh * s * ig + _dot(drp, wr_ref[0], NT) + _dot(dip, wi_ref[0], NT)
        dwr_ref[0] += _dot(uv, drp, TN)
        dwi_ref[0] += _dot(uv, dip, TN)
        dbr_ref[...] += jnp.sum(drp, axis=0, keepdims=True)
        dbi_ref[...] += jnp.sum(dip, axis=0, keepdims=True)

    rj = lambda j: nt - 1 - j
    tile = pl.BlockSpec((r, lb), lambda hb, j: (rj(j), hb))
    vec = pl.BlockSpec((1, lb), lambda hb, j: (0, hb))
    wsp = pl.BlockSpec((1, lb, lb), lambda hb, j: (hb, 0, 0))
    hprev_spec = pl.BlockSpec((8, lb), lambda hb, j: (jnp.maximum(rj(j) * (r // 8) - 1, 0), hb))
    ywin = pl.BlockSpec((r, lb), lambda hb, j: (rj(j), yb + hb))
    return pl.pallas_call(
        body, name=name, grid=(LRU_BLOCKS, nt),
        in_specs=[ANY, tile, ywin, tile, hprev_spec, tile, wsp, vec, wsp, vec, vec],
        out_specs=[tile, ywin, wsp, wsp, vec, vec, vec],
        out_shape=[jax.ShapeDtypeStruct((t, LRU_WIDTH), f32), jax.ShapeDtypeStruct(dst.shape, bf16),
                   jax.ShapeDtypeStruct((LRU_BLOCKS, lb, lb), f32), jax.ShapeDtypeStruct((LRU_BLOCKS, lb, lb), f32),
                   jax.ShapeDtypeStruct((1, LRU_WIDTH), f32), jax.ShapeDtypeStruct((1, LRU_WIDTH), f32),
                   jax.ShapeDtypeStruct((1, LRU_WIDTH), f32)],
        input_output_aliases={0: 1},
        scratch_shapes=[pltpu.VMEM((8, lb), f32), pltpu.VMEM((8, lb), f32)],
        compiler_params=pltpu.CompilerParams(dimension_semantics=("parallel", "arbitrary")),
    )(dst, u, proj, hseq, hseq, dout, w_r, b_r, w_i, b_i, lam)


def merge_fwd(proj, bg, y_ssm, y_lru, *, name):
    t, d = y_ssm.shape
    tr = _pick(t, 512, 8)
    gb = OFF_GATES // d

    def body(gs_ref, gl_ref, bs_ref, bl_ref, ys_ref, yl_ref, o_ref):
        gs = _sigmoid(gs_ref[...] + bs_ref[...])
        gl = _sigmoid(gl_ref[...] + bl_ref[...])
        o_ref[...] = (gs * ys_ref[...].astype(f32) + gl * yl_ref[...].astype(f32)).astype(bf16)

    row = pl.BlockSpec((tr, d), lambda i: (i, 0))
    return pl.pallas_call(
        body, name=name, grid=(t // tr,),
        in_specs=[pl.BlockSpec((tr, d), lambda i: (i, gb)), pl.BlockSpec((tr, d), lambda i: (i, gb + 1)),
                  pl.BlockSpec((1, d), lambda i: (0, 0)), pl.BlockSpec((1, d), lambda i: (0, 1)), row, row],
        out_specs=row, out_shape=jax.ShapeDtypeStruct((t, d), bf16),
    )(proj, proj, bg, bg, y_ssm, y_lru)


def merge_bwd(proj, bg, y_ssm, y_lru, dmix, *, name):
    t, d = y_ssm.shape
    tr = _pick(t, 512, 8)
    gb = OFF_GATES // d

    def body(gs_ref, gl_ref, bs_ref, bl_ref, ys_ref, yl_ref, dm_ref, dg_ref, dys_ref, dyl_ref, dbg_ref):
        gs = _sigmoid(gs_ref[...] + bs_ref[...])
        gl = _sigmoid(gl_ref[...] + bl_ref[...])
        dm = dm_ref[...].astype(f32)
        dys_ref[...] = (dm * gs).astype(bf16)
        dyl_ref[...] = (dm * gl).astype(bf16)
        dgs = dm * ys_ref[...].astype(f32) * gs * (1.0 - gs)
        dgl = dm * yl_ref[...].astype(f32) * gl * (1.0 - gl)
        dg_ref[:, 0:d] = dgs.astype(bf16)
        dg_ref[:, d:2 * d] = dgl.astype(bf16)

        @pl.when(pl.program_id(0) == 0)
        def _():
            dbg_ref[...] = jnp.zeros_like(dbg_ref)

        dbg_ref[:, 0:d] += jnp.sum(dgs, axis=0, keepdims=True)
        dbg_ref[:, d:2 * d] += jnp.sum(dgl, axis=0, keepdims=True)

    row = pl.BlockSpec((tr, d), lambda i: (i, 0))
    return pl.pallas_call(
        body, name=name, grid=(t // tr,),
        in_specs=[pl.BlockSpec((tr, d), lambda i: (i, gb)), pl.BlockSpec((tr, d), lambda i: (i, gb + 1)),
                  pl.BlockSpec((1, d), lambda i: (0, 0)), pl.BlockSpec((1, d), lambda i: (0, 1)), row, row, row],
        out_specs=[pl.BlockSpec((tr, 2 * d), lambda i: (i, OFF_GATES // (2 * d))), row, row,
                   pl.BlockSpec((1, 2 * d), lambda i: (0, 0))],
        out_shape=[jax.ShapeDtypeStruct((t, PROJ_W), bf16), jax.ShapeDtypeStruct((t, d), bf16),
                   jax.ShapeDtypeStruct((t, d), bf16), jax.ShapeDtypeStruct((1, 2 * d), f32)],
        compiler_params=pltpu.CompilerParams(dimension_semantics=("arbitrary",)),
    )(proj, proj, bg, bg, y_ssm, y_lru, dmix)


def swiglu_fwd(ff, *, name):
    t = ff.shape[0]
    hd = FFN_HIDDEN
    tr = _pick(t, 512, 8)

    def body(f_ref, o_ref):
        o_ref[...] = (_silu(f_ref[:, 0:hd].astype(f32)) * f_ref[:, hd:2 * hd].astype(f32)).astype(bf16)

    return pl.pallas_call(
        body, name=name, grid=(t // tr,), in_specs=[pl.BlockSpec((tr, 2 * hd), lambda i: (i, 0))],
        out_specs=pl.BlockSpec((tr, hd), lambda i: (i, 0)), out_shape=jax.ShapeDtypeStruct((t, hd), bf16),
    )(ff)


def swiglu_bwd(ff, dact, *, name):
    t = ff.shape[0]
    hd = FFN_HIDDEN
    tr = _pick(t, 512, 8)

    def body(f_ref, d_ref, o_ref):
        gate, up, dv = f_ref[:, 0:hd].astype(f32), f_ref[:, hd:2 * hd].astype(f32), d_ref[...].astype(f32)
        o_ref[:, 0:hd] = (dv * up * _dsilu(gate)).astype(bf16)
        o_ref[:, hd:2 * hd] = (dv * _silu(gate)).astype(bf16)

    return pl.pallas_call(
        body, name=name, grid=(t // tr,),
        in_specs=[pl.BlockSpec((tr, 2 * hd), lambda i: (i, 0)), pl.BlockSpec((tr, hd), lambda i: (i, 0))],
        out_specs=pl.BlockSpec((tr, 2 * hd), lambda i: (i, 0)), out_shape=jax.ShapeDtypeStruct((t, 2 * hd), bf16),
    )(ff, dact)


def _adam_math(w, g, m, v):
    m = ADAM_B1 * m + (1.0 - ADAM_B1) * g
    v = ADAM_B2 * v + (1.0 - ADAM_B2) * (g * g)
    m_hat = m / (1.0 - ADAM_B1 ** ADAM_STEP)
    v_hat = v / (1.0 - ADAM_B2 ** ADAM_STEP)
    delta = -ADAM_LR * (m_hat / (jnp.sqrt(v_hat) + ADAM_EPS) + ADAM_WD * w)
    return delta, m, v


def _row_tile(rows, cols):
    cap = max(8, (1 << 19) // cols)
    return _pick(rows, cap, 8) if rows % 8 == 0 else rows


def adamw(w, g, m, v, *, name):
    rows, cols = w.shape
    tr = _row_tile(rows, cols)

    def body(w_ref, g_ref, m_ref, v_ref, d_ref, nm_ref, nv_ref):
        d, nm, nv = _adam_math(w_ref[...], g_ref[...], m_ref[...], v_ref[...])
        d_ref[...] = d
        nm_ref[...] = nm
        nv_ref[...] = nv

    tile = pl.BlockSpec((tr, cols), lambda i: (i, 0))
    return pl.pallas_call(
        body, name=name, grid=(rows // tr,), in_specs=[tile] * 4, out_specs=[tile] * 3,
        out_shape=[jax.ShapeDtypeStruct((rows, cols), f32)] * 3,
    )(w, g, m, v)


def adamw_many(ws, gs, ms, vs, *, name):
    n = len(ws)

    def body(*refs):
        for i in range(n):
            d, nm, nv = _adam_math(refs[i][...], refs[n + i][...], refs[2 * n + i][...], refs[3 * n + i][...])
            refs[4 * n + 3 * i][...] = d
            refs[4 * n + 3 * i + 1][...] = nm
            refs[4 * n + 3 * i + 2][...] = nv

    outs = pl.pallas_call(
        body, name=name, out_shape=[jax.ShapeDtypeStruct(w.shape, f32) for w in ws for _ in range(3)],
    )(*ws, *gs, *ms, *vs)
    return [tuple(outs[3 * i:3 * i + 3]) for i in range(n)]


def pair_add(dw, rbuf, idx, *, name):
    n, rows, cols = dw.shape
    hr = rows // 2
    tr = _row_tile(hr, cols)
    nrt = hr // tr

    def body(idx_ref, a_ref, b_ref, o_ref, own_ref):
        s = a_ref[...] + b_ref[...]
        o_ref[...] = s.astype(bf16)

        @pl.when(pl.program_id(1) == idx_ref[0])
        def _():
            own_ref[...] = s[0]

    return pl.pallas_call(
        body, name=name,
        grid_spec=pltpu.PrefetchScalarGridSpec(
            num_scalar_prefetch=1, grid=(nrt, n),
            in_specs=[pl.BlockSpec((1, tr, cols), lambda i, k, idx: (k, idx[1] * nrt + i, 0)),
                      pl.BlockSpec((1, tr, cols), lambda i, k, idx: (k, i, 0))],
            out_specs=[pl.BlockSpec((1, tr, cols), lambda i, k, idx: (k, i, 0)),
                       pl.BlockSpec((tr, cols), lambda i, k, idx: (i, 0))]),
        out_shape=[jax.ShapeDtypeStruct((n, hr, cols), bf16), jax.ShapeDtypeStruct((hr, cols), f32)],
    )(idx, dw, rbuf)


def chip_sum(own, rbuf, idx, *, name):
    hr, cols = own.shape
    tr = _row_tile(hr, cols)
    nrt = hr // tr

    def body(idx_ref, a_ref, b_ref, o_ref):
        o_ref[...] = ((a_ref[...] + b_ref[0].astype(f32)) + b_ref[1].astype(f32)) + b_ref[2].astype(f32)

    return pl.pallas_call(
        body, name=name,
        grid_spec=pltpu.PrefetchScalarGridSpec(
            num_scalar_prefetch=1, grid=(nrt,),
            in_specs=[pl.BlockSpec((tr, cols), lambda i, idx: (i, 0)),
                      pl.BlockSpec((3, tr, cols), lambda i, idx: (0, i, 0))],
            out_specs=pl.BlockSpec((tr, cols), lambda i, idx: (idx[1] * nrt + i, 0))),
        out_shape=jax.ShapeDtypeStruct((2 * hr, cols), f32),
    )(idx, own, rbuf)


def sum8(rbuf, *, name):
    n, rows, cols = rbuf.shape
    tr = _row_tile(rows, cols * n)

    def body(a_ref, o_ref):
        acc = a_ref[0]
        for k in range(1, n):
            acc = acc + a_ref[k]
        o_ref[...] = acc

    return pl.pallas_call(
        body, name=name, grid=(rows // tr,), in_specs=[pl.BlockSpec((n, tr, cols), lambda i: (0, i, 0))],
        out_specs=pl.BlockSpec((tr, cols), lambda i: (i, 0)), out_shape=jax.ShapeDtypeStruct((rows, cols), f32),
    )(rbuf)


def _coords():
    return lax.axis_index("x"), lax.axis_index("y"), lax.axis_index("c")


def _other_chips(x, y):
    return [(1 - x, y), (x, 1 - y), (1 - x, 1 - y)]


def gather_weights(shards, *, name):
    n = len(shards)
    halves = [s.shape[0] // 2 for s in shards]

    def body(*refs):
        ins, outs = refs[:n], refs[n:2 * n]
        send1, recv1, send2, recv2 = refs[2 * n:]
        x, y, c = _coords()
        me = 2 * x + y
        chips = _other_chips(x, y)
        sibling = (x, y, 1 - c)

        def half(i, k, hc):
            return outs[i].at[k, pl.ds(hc * halves[i], halves[i]), :]

        def ici(i, j):
            return pltpu.make_async_remote_copy(
                src_ref=ins[i].at[pl.ds(c * halves[i], halves[i]), :], dst_ref=half(i, me, c),
                send_sem=send1.at[i, j], recv_sem=recv1.at[i, j], device_id=(*chips[j], c), device_id_type=MESH)

        def landed(i, j):
            kj = 2 * chips[j][0] + chips[j][1]
            return pltpu.make_async_remote_copy(
                src_ref=half(i, kj, c), dst_ref=half(i, kj, c),
                send_sem=send2.at[i, j], recv_sem=recv1.at[i, j], device_id=sibling, device_id_type=MESH)

        def from_sibling(i, j):
            kj = 2 * chips[j][0] + chips[j][1]
            return pltpu.make_async_remote_copy(
                src_ref=half(i, kj, 1 - c), dst_ref=half(i, kj, 1 - c),
                send_sem=send2.at[i, j], recv_sem=recv2.at[i, j], device_id=sibling, device_id_type=MESH)

        def d2d(i, j):
            kj = 2 * chips[j][0] + chips[j][1]
            return pltpu.make_async_remote_copy(
                src_ref=half(i, kj, c), dst_ref=half(i, kj, c),
                send_sem=send2.at[i, j], recv_sem=recv2.at[i, j], device_id=sibling, device_id_type=MESH)

        for j in range(3):
            for i in range(n):
                ici(i, j).start()
        for j in range(3):
            for i in range(n):
                landed(i, j).wait_recv()
                d2d(i, j).start()
        for j in range(3):
            for i in range(n):
                from_sibling(i, j).wait_recv()
        for j in range(3):
            for i in range(n):
                ici(i, j).wait_send()
                d2d(i, j).wait_send()

    return pl.pallas_call(
        body, name=name, in_specs=[ANY] * n, out_specs=[ANY] * n,
        out_shape=[jax.ShapeDtypeStruct((N_CHIPS,) + s.shape, s.dtype) for s in shards],
        scratch_shapes=[pltpu.SemaphoreType.DMA((n, 3))] * 4,
    )(*shards)


def pair_exchange(grads, *, name):
    n = len(grads)
    halves = [g.shape[1] // 2 for g in grads]

    def body(*refs):
        ins, outs = refs[:n], refs[n:2 * n]
        send, recv = refs[2 * n:]
        x, y, c = _coords()
        cps = [pltpu.make_async_remote_copy(
            src_ref=ins[i].at[:, pl.ds((1 - c) * halves[i], halves[i]), :], dst_ref=outs[i],
            send_sem=send.at[i], recv_sem=recv.at[i], device_id=(x, y, 1 - c), device_id_type=MESH) for i in range(n)]
        for cp in cps:
            cp.start()
        for cp in cps:
            cp.wait()

    return pl.pallas_call(
        body, name=name, in_specs=[ANY] * n, out_specs=[ANY] * n,
        out_shape=[jax.ShapeDtypeStruct((N_CHIPS, g.shape[1] // 2, g.shape[2]), g.dtype) for g in grads],
        scratch_shapes=[pltpu.SemaphoreType.DMA((n,))] * 2,
    )(*grads)


def pair_gather(bufs, *, name):
    n = len(bufs)

    def body(*refs):
        ins, outs = refs[:n], refs[n:2 * n]
        send, recv = refs[2 * n:]
        x, y, c = _coords()
        cps = []
        for i in range(n):
            hr = ins[i].shape[0] // 2
            cps.append(pltpu.make_async_remote_copy(
                src_ref=ins[i].at[pl.ds(c * hr, hr), :], dst_ref=outs[i].at[pl.ds(c * hr, hr), :],
                send_sem=send.at[i], recv_sem=recv.at[i], device_id=(x, y, 1 - c), device_id_type=MESH))
        for cp in cps:
            cp.start()
        for i in range(n):
            hr = ins[i].shape[0] // 2
            pltpu.make_async_remote_copy(
                src_ref=ins[i].at[pl.ds((1 - c) * hr, hr), :], dst_ref=outs[i].at[pl.ds((1 - c) * hr, hr), :],
                send_sem=send.at[i], recv_sem=recv.at[i], device_id=(x, y, 1 - c), device_id_type=MESH).wait_recv()
        for cp in cps:
            cp.wait_send()

    return pl.pallas_call(
        body, name=name, in_specs=[ANY] * n, out_specs=[ANY] * n,
        out_shape=[jax.ShapeDtypeStruct(b.shape, b.dtype) for b in bufs],
        input_output_aliases={i: i for i in range(n)},
        scratch_shapes=[pltpu.SemaphoreType.DMA((n,))] * 2,
    )(*bufs)


def all_exchange(buf, *, name):
    rows, cols = buf.shape

    def body(in_ref, out_ref, send, recv):
        x, y, c = _coords()
        me = 4 * x + 2 * y + c
        cps = []
        for d in range(1, 8):
            px = 1 - x if d & 4 else x
            py = 1 - y if d & 2 else y
            pc = 1 - c if d & 1 else c
            cps.append(pltpu.make_async_remote_copy(
                src_ref=in_ref, dst_ref=out_ref.at[me], send_sem=send.at[d - 1], recv_sem=recv.at[d - 1],
                device_id=(px, py, pc), device_id_type=MESH))
        for cp in cps:
            cp.start()
        for d in range(1, 8):
            px = 1 - x if d & 4 else x
            py = 1 - y if d & 2 else y
            pc = 1 - c if d & 1 else c
            src = 4 * px + 2 * py + pc
            pltpu.make_async_remote_copy(
                src_ref=in_ref, dst_ref=out_ref.at[src], send_sem=send.at[d - 1], recv_sem=recv.at[d - 1],
                device_id=(px, py, pc), device_id_type=MESH).wait_recv()
        for cp in cps:
            cp.wait_send()

    return pl.pallas_call(
        body, name=name, in_specs=[ANY], out_specs=ANY,
        out_shape=jax.ShapeDtypeStruct((8, rows, cols), buf.dtype),
        scratch_shapes=[pltpu.SemaphoreType.DMA((7,)), pltpu.SemaphoreType.DMA((7,))],
    )(buf)


HBM = pl.BlockSpec(memory_space=pltpu.HBM)
SEM = pl.BlockSpec(memory_space=pltpu.SEMAPHORE)
EFFECT = pltpu.SideEffectType.DATAFLOW_SIDE_EFFECTING


def split_start(arrays, after, copies, sem_shape, *, name):
    na = len(arrays)

    def body(*refs):
        for cp in copies(refs[:na], refs[na + 1], refs[na + 2]):
            cp.start()
        refs[-1][...] = jnp.zeros((8, 128), f32)

    outs = pl.pallas_call(
        body, name=name,
        out_shape=(pltpu.SemaphoreType.DMA(sem_shape), pltpu.SemaphoreType.DMA(sem_shape),
                   *[pltpu.HBM(a.shape, a.dtype) for a in arrays], jax.ShapeDtypeStruct((8, 128), f32)),
        in_specs=[HBM] * na + [ANY], out_specs=(SEM, SEM, *[HBM] * na, pl.BlockSpec(memory_space=pltpu.VMEM)),
        input_output_aliases={i: 2 + i for i in range(na)},
        compiler_params=pltpu.CompilerParams(has_side_effects=EFFECT),
    )(*[pltpu.with_memory_space_constraint(a, pltpu.HBM) for a in arrays], after)
    return outs[0], outs[1], list(outs[2:2 + na]), outs[-1]


def split_wait(send, recv, arrays, after, copies, *, name):
    na = len(arrays)

    def body(*refs):
        for cp in copies(refs[:na], refs[na], refs[na + 1]):
            cp.wait_send()
            cp.wait_recv()

    outs = pl.pallas_call(
        body, name=name, out_shape=tuple(pltpu.HBM(a.shape, a.dtype) for a in arrays),
        in_specs=[HBM] * na + [SEM, SEM, ANY], out_specs=tuple([HBM] * na),
        input_output_aliases={i: i for i in range(na)},
        compiler_params=pltpu.CompilerParams(has_side_effects=EFFECT),
    )(*arrays, send, recv, after)
    return list(outs)


def gather_copies(n):
    def copies(refs, send, recv):
        x, y, c = _coords()
        me = 2 * x + y
        chips = _other_chips(x, y)
        return [pltpu.make_async_remote_copy(
            src_ref=refs[i], dst_ref=refs[n + i].at[me], send_sem=send.at[3 * i + j], recv_sem=recv.at[3 * i + j],
            device_id=(*chips[j], c), device_id_type=MESH) for j in range(3) for i in range(n)]
    return copies


def pair_copies(n):
    def copies(refs, send, recv):
        x, y, c = _coords()
        cps = []
        for i in range(n):
            hr = refs[i].shape[1] // 2
            cps.append(pltpu.make_async_remote_copy(
                src_ref=refs[i].at[:, pl.ds((1 - c) * hr, hr), :], dst_ref=refs[n + i], send_sem=send.at[i],
                recv_sem=recv.at[i], device_id=(x, y, 1 - c), device_id_type=MESH))
        return cps
    return copies


def all_copies():
    def copies(refs, send, recv):
        x, y, c = _coords()
        me = 4 * x + 2 * y + c
        cps = []
        for d in range(1, 8):
            peer = (1 - x if d & 4 else x, 1 - y if d & 2 else y, 1 - c if d & 1 else c)
            cps.append(pltpu.make_async_remote_copy(
                src_ref=refs[0], dst_ref=refs[1].at[me], send_sem=send.at[d - 1], recv_sem=recv.at[d - 1],
                device_id=peer, device_id_type=MESH))
        return cps
    return copies


def reduce_copies(n):
    def copies(refs, send, recv):
        x, y, c = _coords()
        chips = _other_chips(x, y)
        return [pltpu.make_async_remote_copy(
            src_ref=refs[i].at[2 * chips[j][0] + chips[j][1]], dst_ref=refs[n + i].at[j],
            send_sem=send.at[3 * i + j], recv_sem=recv.at[3 * i + j], device_id=(*chips[j], c), device_id_type=MESH)
            for j in range(3) for i in range(n)]
    return copies


def _pack(arrs):
    flat = []
    for a in arrs:
        v = a.reshape(-1).astype(f32)
        pad = (-v.shape[0]) % 128
        flat.append(jnp.pad(v, (0, pad)) if pad else v)
    v = jnp.concatenate(flat)
    rows = v.shape[0] // 128
    pad_rows = (-rows) % 256
    v = v.reshape(rows, 128)
    return jnp.pad(v, ((0, pad_rows), (0, 0))) if pad_rows else v


def _unpack(buf, shapes):
    out, row = [], 0
    for s in shapes:
        size = math.prod(s)
        rows = -(-size // 128)
        out.append(buf[row:row + rows].reshape(-1)[:size].reshape(s))
        row += rows
    return out


def _ref_of_perm():
    ref = np.arange(IN_PROJ_DIM)
    xbc = ref[4096:7168]
    xbc_p = [np.concatenate([xbc[g * 512:(g + 1) * 512], xbc[2048 + g * 128:2048 + (g + 1) * 128],
                             xbc[2560 + g * 128:2560 + (g + 1) * 128]]) for g in range(SSM_GROUPS)]
    return np.concatenate([ref[0:2048], ref[2048:4096], ref[7200:8480], ref[8480:9760], ref[7168:7200],
                           -np.ones(DT_PAD_W - SSM_HEADS, np.int64)] + xbc_p)


def _runs(vals):
    out, start = [], 0
    for i in range(1, len(vals) + 1):
        if i == len(vals) or not (vals[i] == vals[i - 1] + 1 or (vals[i] < 0 and vals[i - 1] < 0)):
            out.append((start, int(vals[start]), i - start))
            start = i
    return out


def _perm_in_from_shards(g):
    ref_of_perm = _ref_of_perm()
    sw = IN_PROJ_DIM // N_CHIPS
    parts = []
    for _, first, length in _runs(ref_of_perm):
        if first < 0:
            parts.append(jnp.zeros((g.shape[1], length), g.dtype))
            continue
        lo = first
        while lo < first + length:
            k = lo // sw
            hi = min(first + length, (k + 1) * sw)
            parts.append(g[k, :, lo - k * sw:hi - k * sw])
            lo = hi
    return jnp.concatenate(parts, axis=-1)


def _unperm_in_to_shards(w):
    ref_of_perm = _ref_of_perm()
    perm_of_ref = np.zeros(IN_PROJ_DIM, np.int64)
    perm_of_ref[ref_of_perm[ref_of_perm >= 0]] = np.nonzero(ref_of_perm >= 0)[0]
    sw = IN_PROJ_DIM // N_CHIPS
    shards = []
    for k in range(N_CHIPS):
        runs = _runs(perm_of_ref[k * sw:(k + 1) * sw])
        shards.append(jnp.concatenate([w[:, first:first + length] for _, first, length in runs], axis=-1))
    return jnp.stack(shards)


def _perm_xbc_cols(w):
    parts = []
    for g in range(SSM_GROUPS):
        parts += [w[..., g * 512:(g + 1) * 512], w[..., 2048 + g * 128:2048 + (g + 1) * 128],
                  w[..., 2560 + g * 128:2560 + (g + 1) * 128]]
    return jnp.concatenate(parts, axis=-1)


def _unperm_xbc_cols(w):
    xs = [w[..., g * XBC_GROUP_W:g * XBC_GROUP_W + 512] for g in range(SSM_GROUPS)]
    bs = [w[..., g * XBC_GROUP_W + 512:g * XBC_GROUP_W + 640] for g in range(SSM_GROUPS)]
    cs = [w[..., g * XBC_GROUP_W + 640:(g + 1) * XBC_GROUP_W] for g in range(SSM_GROUPS)]
    return jnp.concatenate(xs + bs + cs, axis=-1)


def _from_col_shards(w):
    n, r, c = w.shape
    return jnp.transpose(w, (1, 0, 2)).reshape(r, n * c)


def kernel(x, norm1_w, w_in, b_branch_gate, ssm_conv_w, ssm_conv_b, ssm_dt_bias, ssm_a_log, ssm_d, ssm_norm_w, w_out_ssm, lru_conv_w, lru_conv_b, lru_w_r, lru_b_r, lru_w_i, lru_b_i, lru_lambda, w_out_lru, w_out, norm2_w, w_ffn_in, w_ffn_out, norm_f_w, loss_target, m_norm1_w, m_w_in, m_b_branch_gate, m_ssm_conv_w, m_ssm_conv_b, m_ssm_dt_bias, m_ssm_a_log, m_ssm_d, m_ssm_norm_w, m_w_out_ssm, m_lru_conv_w, m_lru_conv_b, m_lru_w_r, m_lru_b_r, m_lru_w_i, m_lru_b_i, m_lru_lambda, m_w_out_lru, m_w_out, m_norm2_w, m_w_ffn_in, m_w_ffn_out, m_norm_f_w, v_norm1_w, v_w_in, v_b_branch_gate, v_ssm_conv_w, v_ssm_conv_b, v_ssm_dt_bias, v_ssm_a_log, v_ssm_d, v_ssm_norm_w, v_w_out_ssm, v_lru_conv_w, v_lru_conv_b, v_lru_w_r, v_lru_b_r, v_lru_w_i, v_lru_b_i, v_lru_lambda, v_w_out_lru, v_w_out, v_norm2_w, v_w_ffn_in, v_w_ffn_out, v_norm_f_w):
    xi, yi, ci = lax.axis_index("x"), lax.axis_index("y"), lax.axis_index("c")
    me = 2 * xi + yi
    idx = jnp.stack([me, ci]).astype(jnp.int32)
    x2 = x[0]
    tgt = loss_target[0]

    big_names = ["w_in", "w_out_ssm", "w_out_lru", "w_out", "w_ffn_in", "w_ffn_out"]
    big_w = dict(w_in=w_in[0], w_out_ssm=w_out_ssm[0], w_out_lru=w_out_lru[0], w_out=w_out[0], w_ffn_in=w_ffn_in[0],
                 w_ffn_out=w_ffn_out[0])
    big_m = dict(w_in=m_w_in[0], w_out_ssm=m_w_out_ssm[0], w_out_lru=m_w_out_lru[0], w_out=m_w_out[0],
                 w_ffn_in=m_w_ffn_in[0], w_ffn_out=m_w_ffn_out[0])
    big_v = dict(w_in=v_w_in[0], w_out_ssm=v_w_out_ssm[0], w_out_lru=v_w_out_lru[0], w_out=v_w_out[0],
                 w_ffn_in=v_w_ffn_in[0], w_ffn_out=v_w_ffn_out[0])
    conv_pad = jnp.zeros((16, 768), f32).at[0:4, :].set(ssm_conv_w[0]).at[8:12, 0:320].set(lru_conv_w[0])
    mine = [big_w["w_in"].astype(bf16), conv_pad]
    gathered = gather_weights(mine, name="gather_weights")
    g_in, g_conv = [lax.dynamic_update_index_in_dim(g, s, me, 0) for g, s in zip(gathered, mine)]
    w_in_p = _perm_in_from_shards(g_in)
    late_names = big_names[1:]
    late = [big_w[k].astype(bf16) for k in late_names]
    late_lands = [lax.empty((N_CHIPS,) + s.shape, bf16) for s in late]
    g_send, g_recv, g_arrays, g_token = split_start(late + late_lands, g_conv, gather_copies(5), (15,),
                                                    name="gather_late_start")
    ssm_cw_full = _from_col_shards(g_conv[:, 0:4, :])
    lru_cw_full = _from_col_shards(g_conv[:, 8:12, 0:320])
    ssm_cw_p = _perm_xbc_cols(ssm_cw_full)
    ssm_cb_p = _perm_xbc_cols(ssm_conv_b)

    par = jnp.stack([ssm_dt_bias[0], ssm_a_log[0], ssm_d[0]], axis=0).reshape(3, SSM_GROUPS, SSM_HPG)
    par_row = jnp.zeros((SSM_GROUPS, 8, 8), f32).at[:, 0:3, :].set(jnp.transpose(par, (1, 0, 2)))
    par_col = jnp.transpose(par_row, (0, 2, 1))

    hn1 = rms_fwd(x2, norm1_w + g_token[0:1, 0:1], name="rms1_fwd")
    proj = mm(hn1, w_in_p, "nn", name="in_proj")
    t = x2.shape[0]
    dtr = jnp.transpose(proj[:, OFF_DT:OFF_DT + 32].reshape(t, SSM_GROUPS, SSM_HPG), (1, 0, 2))
    dtr_t = jnp.transpose(dtr, (0, 2, 1))
    xbc_pre, xbc_post = conv_fwd(proj, OFF_XBC, SSM_CONV_DIM, ssm_cw_p, ssm_cb_p, silu=True, name="ssm_conv_fwd")
    y_ssd, s_in = ssd_fwd(xbc_post, dtr, dtr_t, par_row, par_col, name="ssd_fwd")
    yn = gnorm_fwd(y_ssd, proj, ssm_norm_w, name="gnorm_fwd")
    g_arrays = split_wait(g_send, g_recv, g_arrays, yn, gather_copies(5), name="gather_late_wait")
    g_out_ssm, g_out_lru, g_out, g_ffn_in, g_ffn_out = [
        lax.dynamic_update_index_in_dim(g, s, me, 0) for g, s in zip(g_arrays[5:], late)]
    w_out_ssm_f = g_out_ssm.reshape(SSM_D_INNER, D_MODEL)
    w_out_lru_f = g_out_lru.reshape(LRU_WIDTH, D_MODEL)
    w_out_f = g_out.reshape(D_MODEL, D_MODEL)
    w_ffn_out_f = g_ffn_out.reshape(FFN_HIDDEN, D_MODEL)
    y_ssm = mm(yn, w_out_ssm_f, "nn", out_dtype=bf16, name="out_ssm")
    (u_lru,) = conv_fwd(proj, OFF_LX, LRU_WIDTH, lru_cw_full, lru_conv_b, silu=False, name="lru_conv_fwd")
    h_lru, o_lru = lru_fwd(u_lru, proj, lru_w_r[0], lru_b_r, lru_w_i[0], lru_b_i, lru_lambda, name="lru_fwd")
    y_lru = mm(o_lru, w_out_lru_f, "nn", out_dtype=bf16, name="out_lru")
    mix = merge_fwd(proj, b_branch_gate, y_ssm, y_lru, name="merge_fwd")
    h1, hn2 = mm(mix, w_out_f, "nn", add=x2, name="out_proj",
                 epi=(epi_rms_fwd, [], [norm2_w], [("row", f32), ("row", bf16)]))
    ff = mm(hn2, g_ffn_in, "nn", b_shards=True, out_dtype=bf16, name="ffn_in")
    act = swiglu_fwd(ff, name="swiglu_fwd")
    dh2, dh2_b, d_norm_f, loss_tile = mm(act, w_ffn_out_f, "nn", add=h1, name="ffn_out",
                                         epi=(epi_loss, [tgt], [norm_f_w.reshape(1, D_MODEL)],
                                              [("row", f32), ("row", bf16), ("vec",), ("tile",)]))

    d_w_ffn_out = mm(act, dh2_b, "tn", name="d_w_ffn_out")
    dact = mm(dh2_b, w_ffn_out_f, "nt", out_dtype=bf16, name="d_act")
    dff = swiglu_bwd(ff, dact, name="swiglu_bwd")
    d_w_ffn_in = mm(hn2, dff, "tn", out_shards=N_CHIPS, name="d_w_ffn_in")
    dh1, dh1_b, d_norm2 = mm(dff, g_ffn_in, "nt", b_shards=True, name="d_hn2",
                             epi=(epi_rms_bwd, [h1, dh2], [norm2_w], [("row", f32), ("row", bf16), ("vec",)]))
    d_w_out = mm(mix, dh1_b, "tn", name="d_w_out")
    dmix = mm(dh1_b, w_out_f, "nt", out_dtype=bf16, name="d_mix")
    dproj, dy_ssm, dy_lru, d_bg = merge_bwd(proj, b_branch_gate, y_ssm, y_lru, dmix, name="merge_bwd")
    d_w_out_ssm = mm(yn, dy_ssm, "tn", name="d_w_out_ssm")
    d_w_out_lru = mm(o_lru, dy_lru, "tn", name="d_w_out_lru")
    early_g = [d_w_out_ssm.reshape(N_CHIPS, 512, D_MODEL), d_w_out_lru.reshape(N_CHIPS, 320, D_MODEL),
               d_w_out.reshape(N_CHIPS, 256, D_MODEL), d_w_ffn_in, d_w_ffn_out.reshape(N_CHIPS, 704, D_MODEL)]
    p_lands = [lax.empty((N_CHIPS, g.shape[1] // 2, g.shape[2]), f32) for g in early_g]
    p_send, p_recv, p_arrays, p_token = split_start(early_g + p_lands, early_g[0], pair_copies(5), (5,),
                                                    name="pair_early_start")
    dyn = mm(dy_ssm, w_out_ssm_f, "nt", out_dtype=bf16, after=p_token, name="d_yn")
    dy_ssd, dproj, d_ssm_norm = gnorm_bwd(y_ssd, proj, ssm_norm_w, dyn, dproj, name="gnorm_bwd")
    p_arrays = split_wait(p_send, p_recv, p_arrays, dy_ssd, pair_copies(5), name="pair_early_wait")
    e_pairs = [pair_add(g, rb, idx, name="pair_add_" + k) for g, rb, k in zip(p_arrays[:5], p_arrays[5:], late_names)]
    e_lands = [lax.empty((3,) + p[0].shape[1:], bf16) for p in e_pairs]
    e_send, e_recv, e_arrays, e_token = split_start([p[0] for p in e_pairs] + e_lands, e_pairs[0][1], reduce_copies(5),
                                                    (15,), name="reduce_early_start")
    dxbc_post, ddtr, dpar = ssd_bwd(xbc_post, dtr, dtr_t, par_row + e_token[0:1, 0:1], par_col, s_in, dy_ssd,
                                    name="ssd_bwd")
    dproj, d_ssm_cw_p, d_ssm_cb_p = conv_bwd(dxbc_post, xbc_pre, proj, OFF_XBC, ssm_cw_p, dproj, name="ssm_conv_bwd")
    do_lru = mm(dy_lru, w_out_lru_f, "nt", name="d_o_lru")
    du_lru, dproj, d_w_r, d_w_i, d_b_r, d_b_i, d_lam = lru_bwd(u_lru, proj, h_lru, do_lru, lru_w_r[0], lru_b_r, lru_w_i[0],
                                                               lru_b_i, lru_lambda, dproj, name="lru_bwd")
    dproj, d_lru_cw, d_lru_cb = conv_bwd(du_lru, None, proj, OFF_LX, lru_cw_full, dproj, name="lru_conv_bwd")
    ddt_cols = jnp.transpose(ddtr, (1, 0, 2)).reshape(t, SSM_HEADS).astype(bf16)
    ddt_cols = jnp.pad(ddt_cols, ((0, 0), (0, DT_PAD_W - SSM_HEADS)))
    dproj = lax.dynamic_update_slice(dproj, ddt_cols, (0, OFF_DT))

    d_ssm_cw = _unperm_xbc_cols(d_ssm_cw_p)
    d_ssm_cb = _unperm_xbc_cols(d_ssm_cb_p)
    dpar_h = jnp.transpose(dpar[:, 0:3, :], (1, 0, 2)).reshape(3, SSM_HEADS)
    small_names = ["norm1_w", "b_branch_gate", "ssm_conv_b", "ssm_dt_bias", "ssm_a_log", "ssm_d", "ssm_norm_w",
                   "lru_conv_b", "lru_w_r", "lru_b_r", "lru_w_i", "lru_b_i", "lru_lambda", "norm2_w", "norm_f_w"]
    small_g = dict(norm1_w=jnp.zeros_like(norm1_w), b_branch_gate=d_bg, ssm_conv_b=d_ssm_cb, ssm_dt_bias=dpar_h[0:1], ssm_a_log=dpar_h[1:2],
                   ssm_d=dpar_h[2:3], ssm_norm_w=d_ssm_norm, lru_conv_b=d_lru_cb, lru_w_r=d_w_r[None], lru_b_r=d_b_r,
                   lru_w_i=d_w_i[None], lru_b_i=d_b_i, lru_lambda=d_lam, norm2_w=d_norm2, norm_f_w=d_norm_f.reshape(D_MODEL))
    small_w = dict(norm1_w=norm1_w, b_branch_gate=b_branch_gate, ssm_conv_b=ssm_conv_b, ssm_dt_bias=ssm_dt_bias,
                   ssm_a_log=ssm_a_log, ssm_d=ssm_d, ssm_norm_w=ssm_norm_w, lru_conv_b=lru_conv_b, lru_w_r=lru_w_r,
                   lru_b_r=lru_b_r, lru_w_i=lru_w_i, lru_b_i=lru_b_i, lru_lambda=lru_lambda, norm2_w=norm2_w, norm_f_w=norm_f_w)
    small_m = dict(norm1_w=m_norm1_w, b_branch_gate=m_b_branch_gate, ssm_conv_b=m_ssm_conv_b, ssm_dt_bias=m_ssm_dt_bias,
                   ssm_a_log=m_ssm_a_log, ssm_d=m_ssm_d, ssm_norm_w=m_ssm_norm_w, lru_conv_b=m_lru_conv_b, lru_w_r=m_lru_w_r,
                   lru_b_r=m_lru_b_r, lru_w_i=m_lru_w_i, lru_b_i=m_lru_b_i, lru_lambda=m_lru_lambda, norm2_w=m_norm2_w,
                   norm_f_w=m_norm_f_w)
    small_v = dict(norm1_w=v_norm1_w, b_branch_gate=v_b_branch_gate, ssm_conv_b=v_ssm_conv_b, ssm_dt_bias=v_ssm_dt_bias,
                   ssm_a_log=v_ssm_a_log, ssm_d=v_ssm_d, ssm_norm_w=v_ssm_norm_w, lru_conv_b=v_lru_conv_b, lru_w_r=v_lru_w_r,
                   lru_b_r=v_lru_b_r, lru_w_i=v_lru_w_i, lru_b_i=v_lru_b_i, lru_lambda=v_lru_lambda, norm2_w=v_norm2_w,
                   norm_f_w=v_norm_f_w)
    shapes = [small_w[k].shape for k in small_names]
    conv_shapes = [(4, SSM_CONV_DIM), (4, LRU_WIDTH)]
    g_pack = _pack([small_g[k] for k in small_names] + [d_ssm_cw, d_lru_cw])
    s_send, s_recv, s_arrays, s_token = split_start([g_pack, lax.empty((8,) + g_pack.shape, f32)], g_pack, all_copies(),
                                                    (7,), name="small_start")
    d_w_in_p = mm(hn1, dproj, "tn", after=s_token, name="d_w_in")

    d_w_in_s = _unperm_in_to_shards(d_w_in_p)
    (l_sib,) = pair_exchange([d_w_in_s], name="pair_exchange_late")
    l_pair = pair_add(d_w_in_s, l_sib, idx, name="pair_add_w_in")
    l_land = lax.empty((3,) + l_pair[0].shape[1:], bf16)
    l_send, l_recv, l_arrays, l_token = split_start([l_pair[0], l_land], l_pair[1], reduce_copies(1), (3,),
                                                    name="reduce_late_start")
    grad_x, d_norm1 = mm(dproj, w_in_p, "nt", after=l_token, name="d_hn1",
                         epi=(epi_rms_bwd, [x2, dh1], [norm1_w], [("row", f32), ("vec",)]))

    e_arrays = split_wait(e_send, e_recv, e_arrays, d_norm1, reduce_copies(5), name="reduce_early_wait")
    e_half = [chip_sum(p[1], rb, idx, name="chip_sum_" + k) for p, rb, k in zip(e_pairs, e_arrays[5:], late_names)]
    big_out = {}
    for k, g in zip(late_names, pair_gather(e_half, name="pair_gather_early")):
        big_out[k] = (g,) + tuple(adamw(big_w[k], g, big_m[k], big_v[k], name="adamw_" + k))

    s_arrays = split_wait(s_send, s_recv, s_arrays, d_norm1, all_copies(), name="small_wait")
    g_sum = sum8(lax.dynamic_update_index_in_dim(s_arrays[1], g_pack, 2 * me + ci, 0), name="sum8")
    n1 = jnp.concatenate([d_norm1.reshape(8, 128), loss_tile], axis=0)
    n1_sum = sum8(lax.dynamic_update_index_in_dim(all_exchange(n1, name="all_exchange_norm1"), n1, 2 * me + ci, 0),
                  name="sum8_norm1")
    loss = n1_sum[8, 0]
    g_sum = lax.dynamic_update_slice(g_sum, n1_sum[0:8], (0, 0))
    g_small = _unpack(g_sum, shapes + conv_shapes)
    g_small[-2] = lax.dynamic_slice_in_dim(g_small[-2], me * 768, 768, axis=1)
    g_small[-1] = lax.dynamic_slice_in_dim(g_small[-1], me * 320, 320, axis=1)
    all_names = small_names + ["ssm_conv_w", "lru_conv_w"]
    small_w.update(ssm_conv_w=ssm_conv_w[0], lru_conv_w=lru_conv_w[0])
    small_m.update(ssm_conv_w=m_ssm_conv_w[0], lru_conv_w=m_lru_conv_w[0])
    small_v.update(ssm_conv_w=v_ssm_conv_w[0], lru_conv_w=v_lru_conv_w[0])
    as2d = lambda a: a.reshape(-1, a.shape[-1])
    upd = adamw_many([as2d(small_w[k]) for k in all_names], [as2d(g) for g in g_small],
                     [as2d(small_m[k]) for k in all_names], [as2d(small_v[k]) for k in all_names], name="adamw_small")
    small_out = {}
    for k, g, u in zip(all_names, g_small, upd):
        small_out[k] = (g,) + tuple(o.reshape(g.shape) for o in u)
    l_arrays = split_wait(l_send, l_recv, l_arrays, upd[0][0], reduce_copies(1), name="reduce_late_wait")
    l_half = chip_sum(l_pair[1], l_arrays[1], idx, name="chip_sum_w_in")
    (g_w_in,) = pair_gather([l_half], name="pair_gather_late")
    big_out["w_in"] = (g_w_in,) + tuple(adamw(big_w["w_in"], g_w_in, big_m["w_in"], big_v["w_in"], name="adamw_w_in"))

    order = ["norm1_w", "w_in", "b_branch_gate", "ssm_conv_w", "ssm_conv_b", "ssm_dt_bias", "ssm_a_log", "ssm_d", "ssm_norm_w",
             "w_out_ssm", "lru_conv_w", "lru_conv_b", "lru_w_r", "lru_b_r", "lru_w_i", "lru_b_i", "lru_lambda", "w_out_lru",
             "w_out", "norm2_w", "w_ffn_in", "w_ffn_out", "norm_f_w"]
    outs = [loss, grad_x[None]]
    for which in range(4):
        for k in order:
            if k in big_out:
                outs.append(big_out[k][which][None])
            elif k in ("ssm_conv_w", "lru_conv_w"):
                outs.append(small_out[k][which][None])
            else:
                outs.append(small_out[k][which])
    return tuple(outs)
```

```python
import math

import jax
import jax.numpy as jnp
import numpy as np
from jax import lax
from jax.experimental import pallas as pl
from jax.experimental.pallas import tpu as pltpu

f32 = jnp.float32
bf16 = jnp.bfloat16

D_MODEL = 1024
SSM_D_INNER = 2048
SSM_HEADS = 32
SSM_HEAD_DIM = 64
SSM_GROUPS = 4
SSM_HPG = 8
SSM_D_STATE = 128
SSM_CHUNK = 128
SSM_GROUP_W = 512
SSM_CONV_DIM = 3072
XBC_GROUP_W = 768
LRU_WIDTH = 1280
LRU_BLOCKS = 10
LRU_BLOCK = 128
LRU_C = 8.0
FFN_HIDDEN = 2816
RMS_EPS = 1e-6
IN_PROJ_DIM = 9760
N_CHIPS = 4

OFF_GATES = 0
OFF_Z = 2048
OFF_LX = 4096
OFF_LY = 5376
OFF_DT = 6656
DT_PAD_W = 256
OFF_XBC = 6912
PROJ_W = 9984

ADAM_LR = 0.001
ADAM_B1 = 0.9
ADAM_B2 = 0.999
ADAM_EPS = 1e-08
ADAM_WD = 0.01
ADAM_STEP = 10

MESH = pl.DeviceIdType.MESH
ANY = pl.BlockSpec(memory_space=pl.ANY)

NN = (((1,), (0,)), ((), ()))
NT = (((1,), (1,)), ((), ()))
TN = (((0,), (0,)), ((), ()))


def _pick(n, cap, mult=128):
    best = None
    for t in range(mult, min(n, cap) + 1, mult):
        if n % t == 0:
            best = t
    return best if best is not None else n


def _sigmoid(x):
    return 0.5 * jnp.tanh(0.5 * x) + 0.5


def _softplus(x):
    return jnp.maximum(x, 0.0) + jnp.log(1.0 + jnp.exp(-jnp.abs(x)))


def _silu(x):
    return x * _sigmoid(x)


def _dsilu(x):
    s = _sigmoid(x)
    return s * (1.0 + x * (1.0 - s))


_GELU_K = math.sqrt(2.0 / math.pi)


def _gelu(x):
    return 0.5 * x * (1.0 + jnp.tanh(_GELU_K * (x + 0.044715 * x * x * x)))


def _dgelu(x):
    t = jnp.tanh(_GELU_K * (x + 0.044715 * x * x * x))
    return 0.5 * (1.0 + t) + 0.5 * x * (1.0 - t * t) * _GELU_K * (1.0 + 3.0 * 0.044715 * x * x)


def _expm1(x):
    poly = x * (1.0 + x * (0.5 + x * (1.0 / 6.0 + x * (1.0 / 24.0 + x * (1.0 / 120.0 + x * (1.0 / 720.0))))))
    return jnp.where(jnp.abs(x) < 0.1, poly, jnp.exp(x) - 1.0)


def _dot(a, b, dn):
    return lax.dot_general(a.astype(bf16), b.astype(bf16), dn, preferred_element_type=f32)


def _dot_01(a, b, dn, split, terms):
    r = a if split == 0 else b
    out = None
    for _ in range(terms):
        h = r.astype(bf16)
        r = r - h.astype(f32)
        d = lax.dot_general(h if split == 0 else a.astype(bf16), b.astype(bf16) if split == 0 else h, dn,
                            preferred_element_type=f32)
        out = d if out is None else out + d
    return out


MM_VMEM_BUDGET = 48 * 2 ** 20

def mm(a, b, mode, *, name, add=None, after=None, out_dtype=f32, b_shards=False, out_shards=0, epi=None):
    bs = b.shape[1:] if b_shards else b.shape
    shard_w = b.shape[2] if b_shards else None
    bcols = bs[1] * (b.shape[0] if b_shards else 1)
    if mode == "nn":
        (m, k), (k2, n) = a.shape, (bs[0], bcols)
    elif mode == "nt":
        (m, k), (n, k2) = a.shape, (bs[0], bcols)
    else:
        (k, m), (k2, n) = a.shape, b.shape
    assert k == k2, (a.shape, b.shape, mode)
    tn = _pick(n, 1536)
    if b_shards and mode == "nn":
        tn = shard_w
    if out_shards:
        tn = n // out_shards
    isz = lambda v: jnp.dtype(v.dtype).itemsize
    if epi is not None:
        assert n <= 1536 and not out_shards
        tn = n
        epi_fn, epi_rows, epi_vecs, epi_outs = epi
        tile_bytes = sum(isz(v) for v in epi_rows) + sum(jnp.dtype(o[1]).itemsize for o in epi_outs if o[0] == "row")
    else:
        epi_rows, epi_vecs, epi_outs = [], [], []
        tile_bytes = jnp.dtype(out_dtype).itemsize
    tks = [shard_w] if (b_shards and mode == "nt") else sorted({k, _pick(k, 3328), _pick(k, 2048), _pick(k, 1024)}, reverse=True)

    def vmem_of(tm, tk):
        blocks = tm * tk * isz(a) + tk * tn * isz(b) + tm * tn * (4 * int(add is not None) + tile_bytes)
        return 2 * blocks + 4 * tm * tn * int(k > tk)

    fits = [(tk, tm) for tk in tks for tm in (_pick(m, 1536), _pick(m, 1024), _pick(m, 512)) if vmem_of(tm, tk) <= MM_VMEM_BUDGET]
    tk, tm = fits[0] if fits else (tks[-1], _pick(m, 256))
    nk = k // tk
    dn = {"nn": NN, "nt": NT, "tn": TN}[mode]
    a_spec = pl.BlockSpec((tk, tm), lambda i, j, kk: (kk, i)) if mode == "tn" else pl.BlockSpec((tm, tk), lambda i, j, kk: (i, kk))
    b_spec = pl.BlockSpec((tn, tk), lambda i, j, kk: (j, kk)) if mode == "nt" else pl.BlockSpec((tk, tn), lambda i, j, kk: (kk, j))
    if b_shards:
        b_spec = (pl.BlockSpec((None, tn, tk), lambda i, j, kk: (kk, j, 0)) if mode == "nt"
                  else pl.BlockSpec((None, tk, tn), lambda i, j, kk: (j, kk, 0)))
    o_spec = pl.BlockSpec((tm, tn), lambda i, j, kk: (i, j))
    out_shape = jax.ShapeDtypeStruct((m, n), out_dtype)
    if out_shards:
        assert add is None
        o_spec = pl.BlockSpec((None, tm, tn), lambda i, j, kk: (j, i, 0))
        out_shape = jax.ShapeDtypeStruct((out_shards, m, tn), out_dtype)
    has_add = add is not None

    n_extra = int(has_add) + int(after is not None)
    n_rows, n_vecs, n_outs = len(epi_rows), len(epi_vecs), len(epi_outs)

    def body(a_ref, b_ref, *rest):
        add_ref = rest[0] if has_add else None
        o_ref = rest[n_extra]

        def finish(r):
            if has_add:
                r = r + add_ref[...]
            if epi is None:
                o_ref[...] = r.astype(out_dtype)
            else:
                e = rest[n_extra:]
                epi_fn(r, e[:n_rows], e[n_rows:n_rows + n_vecs], e[n_rows + n_vecs:n_rows + n_vecs + n_outs],
                       pl.program_id(0) == 0)

        if nk == 1:
            finish(_dot(a_ref[...], b_ref[...], dn))
            return
        acc = rest[-1]
        kk = pl.program_id(2)

        @pl.when(kk == 0)
        def _():
            acc[...] = jnp.zeros_like(acc)

        acc[...] += _dot(a_ref[...], b_ref[...], dn)

        @pl.when(kk == nk - 1)
        def _():
            finish(acc[...])

    ins = [a, b] + ([add] if has_add else []) + ([after] if after is not None else [])
    in_specs = [a_spec, b_spec] + ([o_spec] if has_add else []) + ([ANY] if after is not None else [])
    sem0 = "parallel"
    if epi is not None:
        vec_spec = pl.BlockSpec((1, tn), lambda i, j, kk: (0, 0))
        ins += list(epi_rows) + list(epi_vecs)
        in_specs += [o_spec] * n_rows + [vec_spec] * n_vecs
        o_spec, out_shape = [], []
        for o in epi_outs:
            if o[0] == "row":
                o_spec.append(pl.BlockSpec((tm, tn), lambda i, j, kk: (i, j)))
                out_shape.append(jax.ShapeDtypeStruct((m, n), o[1]))
            elif o[0] == "vec":
                o_spec.append(vec_spec)
                out_shape.append(jax.ShapeDtypeStruct((1, n), f32))
                sem0 = "arbitrary"
            else:
                o_spec.append(pl.BlockSpec((8, 128), lambda i, j, kk: (0, 0)))
                out_shape.append(jax.ShapeDtypeStruct((8, 128), f32))
                sem0 = "arbitrary"
    return pl.pallas_call(
        body, name=name, grid=(m // tm, n // tn, nk), in_specs=in_specs, out_specs=o_spec, out_shape=out_shape,
        scratch_shapes=[pltpu.VMEM((tm, tn), f32)] if nk > 1 else [],
        compiler_params=pltpu.CompilerParams(dimension_semantics=(sem0, sem0, "arbitrary")),
    )(*ins)


def rms_fwd(x, w, *, name):
    t, d = x.shape
    tr = _pick(t, 512, 8)

    def body(x_ref, w_ref, o_ref):
        xv = x_ref[...]
        r = lax.rsqrt(jnp.mean(xv * xv, axis=-1, keepdims=True) + RMS_EPS)
        o_ref[...] = (xv * r * w_ref[...]).astype(bf16)

    return pl.pallas_call(
        body, name=name, grid=(t // tr,),
        in_specs=[pl.BlockSpec((tr, d), lambda i: (i, 0)), pl.BlockSpec((1, d), lambda i: (0, 0))],
        out_specs=pl.BlockSpec((tr, d), lambda i: (i, 0)), out_shape=jax.ShapeDtypeStruct((t, d), bf16),
    )(x, w)


def _rms_bwd_math(xv, wv, dy):
    r = lax.rsqrt(jnp.mean(xv * xv, axis=-1, keepdims=True) + RMS_EPS)
    g = dy * wv
    dx = r * g - xv * (r * r * r) * jnp.mean(g * xv, axis=-1, keepdims=True)
    dw = jnp.sum(dy * xv * r, axis=0, keepdims=True)
    return dx, dw


def epi_rms_fwd(r, rows, vecs, outs, first):
    outs[0][...] = r
    rr = lax.rsqrt(jnp.mean(r * r, axis=-1, keepdims=True) + RMS_EPS)
    outs[1][...] = (r * rr * vecs[0][...]).astype(bf16)


def epi_rms_bwd(r, rows, vecs, outs, first):
    dx, dw = _rms_bwd_math(rows[0][...], vecs[0][...], r)
    dx = dx + rows[1][...]
    outs[0][...] = dx
    if len(outs) == 3:
        outs[1][...] = dx.astype(bf16)
    dw_ref = outs[-1]

    @pl.when(first)
    def _():
        dw_ref[...] = jnp.zeros_like(dw_ref)

    dw_ref[...] += dw


def epi_loss(r, rows, vecs, outs, first):
    wv = vecs[0][...]
    rr = lax.rsqrt(jnp.mean(r * r, axis=-1, keepdims=True) + RMS_EPS)
    err = r * rr * wv - rows[0][...]
    part = 0.5 * jnp.sum(jnp.mean(err * err, axis=-1, keepdims=True), axis=0, keepdims=True)
    dx, dw = _rms_bwd_math(r, wv, err * (1.0 / r.shape[-1]))
    outs[0][...] = dx
    outs[1][...] = dx.astype(bf16)

    @pl.when(first)
    def _():
        outs[2][...] = jnp.zeros_like(outs[2])
        outs[3][...] = jnp.zeros_like(outs[3])

    outs[2][...] += dw
    outs[3][...] += part


CONV_ROWS = 1024
VREG_ELEMS = 8 * 128


def _conv_chunk(tc):
    return 16 if (16 + 8) * tc * 3 > 48 * VREG_ELEMS else 32


def conv_fwd(src, col0, width, w, b, *, silu, name):
    t = src.shape[0]
    tc = _pick(math.gcd(width, col0), 768)
    assert col0 % tc == 0
    cb = col0 // tc
    r = CONV_ROWS
    ch = _conv_chunk(tc)

    def body(u_ref, w_ref, b_ref, *rest):
        ext = rest[-1]
        j = pl.program_id(1)

        @pl.when(j == 0)
        def _():
            ext[0:8, :] = jnp.zeros((8, tc), f32)

        @pl.when(j > 0)
        def _():
            ext[0:8, :] = ext[r:r + 8, :]

        ext[8:r + 8, :] = u_ref[...]
        wv = w_ref[...]
        bv = b_ref[...]

        def chunk(c, carry):
            r0 = pl.multiple_of(c * ch, ch)
            v = ext[pl.ds(r0, ch + 8), :]
            acc = bv + wv[3:4, :] * v[8:, :]
            for s in (1, 2, 3):
                acc = acc + wv[3 - s:4 - s, :] * pltpu.roll(v, s, 0)[8:, :]
            rest[0][pl.ds(r0, ch), :] = acc
            if silu:
                rest[1][pl.ds(r0, ch), :] = _silu(acc)
            return carry

        lax.fori_loop(0, r // ch, chunk, 0)

    tile = pl.BlockSpec((r, tc), lambda c, j: (j, c))
    n_out = 2 if silu else 1
    return pl.pallas_call(
        body, name=name, grid=(width // tc, t // r),
        in_specs=[pl.BlockSpec((r, tc), lambda c, j: (j, cb + c)), pl.BlockSpec((4, tc), lambda c, j: (0, c)),
                  pl.BlockSpec((1, tc), lambda c, j: (0, c))],
        out_specs=[tile] * n_out, out_shape=[jax.ShapeDtypeStruct((t, width), f32)] * n_out,
        scratch_shapes=[pltpu.VMEM((r + 8, tc), f32)],
        compiler_params=pltpu.CompilerParams(dimension_semantics=("parallel", "arbitrary")),
    )(src, w, b)


def conv_bwd(dpost, pre, src, col0, w, dst, *, name):
    t, width = dpost.shape
    tc = _pick(math.gcd(width, col0), 768)
    assert col0 % tc == 0
    cb = col0 // tc
    r = CONV_ROWS
    ch = _conv_chunk(tc)
    nt = t // r
    has_pre = pre is not None

    def body(*refs):
        refs = refs[1:]
        if has_pre:
            d_ref, p_ref, u_ref, w_ref, du_ref, dw_ref, db_ref, ext = refs
        else:
            d_ref, u_ref, w_ref, du_ref, dw_ref, db_ref, ext = refs
        j = pl.program_id(1)

        @pl.when(j == 0)
        def _():
            ext[r:r + 8, :] = jnp.zeros((8, tc), f32)
            dw_ref[...] = jnp.zeros_like(dw_ref)
            db_ref[...] = jnp.zeros_like(db_ref)

        @pl.when(j > 0)
        def _():
            ext[r:r + 8, :] = ext[0:8, :]

        def fill(c, carry):
            r0 = pl.multiple_of(c * 64, 64)
            dp = d_ref[pl.ds(r0, 64), :]
            if has_pre:
                dp = dp * _dsilu(p_ref[pl.ds(r0, 64), :])
            ext[pl.ds(r0, 64), :] = dp
            return carry

        lax.fori_loop(0, r // 64, fill, 0)
        wv = w_ref[...]

        def fold(p):
            out = p[0:8, :]
            for i in range(1, ch // 8):
                out = out + p[8 * i:8 * i + 8, :]
            return out

        def chunk(c, sums):
            r0 = pl.multiple_of(c * ch, ch)
            v = ext[pl.ds(r0, ch + 8), :]
            uv = u_ref[pl.ds(r0, ch), :]
            d0 = v[0:ch, :]
            du = wv[3:4, :] * d0
            new = [None] * 5
            new[3] = sums[3] + fold(d0 * uv)
            for s in (1, 2, 3):
                sh = pltpu.roll(v, ch + 8 - s, 0)[0:ch, :]
                du = du + wv[3 - s:4 - s, :] * sh
                new[3 - s] = sums[3 - s] + fold(sh * uv)
            new[4] = sums[4] + fold(d0)
            du_ref[pl.ds(r0, ch), :] = du.astype(bf16)
            return tuple(new)

        sums = lax.fori_loop(0, r // ch, chunk, tuple(jnp.zeros((8, tc), f32) for _ in range(5)))
        for k in range(4):
            dw_ref[k:k + 1, :] += jnp.sum(sums[k], axis=0, keepdims=True)
        db_ref[...] += jnp.sum(sums[4], axis=0, keepdims=True)

    rev = pl.BlockSpec((r, tc), lambda c, j: (nt - 1 - j, c))
    win = pl.BlockSpec((r, tc), lambda c, j: (nt - 1 - j, cb + c))
    in_specs = [ANY, rev] + ([rev] if has_pre else []) + [win, pl.BlockSpec((4, tc), lambda c, j: (0, c))]
    ins = [dst, dpost] + ([pre] if has_pre else []) + [src, w]
    return pl.pallas_call(
        body, name=name, grid=(width // tc, nt), in_specs=in_specs,
        out_specs=[win, pl.BlockSpec((4, tc), lambda c, j: (0, c)), pl.BlockSpec((1, tc), lambda c, j: (0, c))],
        out_shape=[jax.ShapeDtypeStruct(dst.shape, bf16), jax.ShapeDtypeStruct((4, width), f32),
                   jax.ShapeDtypeStruct((1, width), f32)],
        input_output_aliases={0: 0},
        scratch_shapes=[pltpu.VMEM((r + 8, tc), f32)],
        compiler_params=pltpu.CompilerParams(dimension_semantics=("parallel", "arbitrary")),
    )(*ins)


def _ssd_common(xbc_ref, dtr_ref, dtrT_ref, par_row_ref, par_col_ref):
    l = SSM_CHUNK
    x = xbc_ref[:, 0:SSM_GROUP_W]
    bm = xbc_ref[:, SSM_GROUP_W:SSM_GROUP_W + SSM_D_STATE]
    cm = xbc_ref[:, SSM_GROUP_W + SSM_D_STATE:XBC_GROUP_W]
    par_row = par_row_ref[0]
    par_col = par_col_ref[0]
    bias_row, alog_row = par_row[0:1, :], par_row[1:2, :]
    bias_col, alog_col = par_col[:, 0:1], par_col[:, 1:2]
    dtr = dtr_ref[0]
    dt = _softplus(dtr + bias_row)
    dt_t = _softplus(dtrT_ref[0] + bias_col)
    a_row = -jnp.exp(alog_row)
    a_col = -jnp.exp(alog_col)
    li = lax.broadcasted_iota(jnp.int32, (l, l), 0)
    si = lax.broadcasted_iota(jnp.int32, (l, l), 1)
    tri = (li >= si).astype(f32)
    cs = _dot_01(tri, dt * a_row, NN, 1, 3)
    cs_t = _dot_01(dt_t * a_col, tri, NT, 0, 3)
    off = lax.broadcasted_iota(jnp.int32, (SSM_HPG, SSM_GROUP_W), 1) - SSM_HEAD_DIM * lax.broadcasted_iota(
        jnp.int32, (SSM_HPG, SSM_GROUP_W), 0)
    ex = ((off >= 0) & (off < SSM_HEAD_DIM)).astype(f32)
    cs_x = _dot_01(cs, ex, NN, 0, 3)
    cl_x = cs_x[l - 1:l, :]
    return dict(x=x, bm=bm, cm=cm, dtr=dtr, dt=dt, a_row=a_row, bias_row=bias_row, tri=tri, li=li, si=si, cs=cs,
                cs_t=cs_t, ex=ex, dt_x=_dot_01(dt, ex, NN, 0, 2), d_x=_dot_01(par_row, ex, NN, 0, 2)[2:3, :], e_x=jnp.exp(cs_x),
                el_x=jnp.exp(cl_x), dec_x=jnp.exp(cl_x - cs_x))


def ssd_fwd(xbc, dtr, dtr_t, par_row, par_col, *, name):
    t = xbc.shape[0]
    nc = t // SSM_CHUNK
    l, p = SSM_CHUNK, SSM_HEAD_DIM

    def body(xbc_ref, dtr_ref, dtrT_ref, prow_ref, pcol_ref, y_ref, sin_ref, state):
        @pl.when(pl.program_id(1) == 0)
        def _():
            state[...] = jnp.zeros_like(state)

        q = _ssd_common(xbc_ref, dtr_ref, dtrT_ref, prow_ref, pcol_ref)
        st = state[...]
        sin_ref[0] = st
        xd = q["x"] * q["dt_x"]
        g = _dot(q["cm"], q["bm"], NT)
        for r in range(SSM_HPG):
            sl = slice(r * p, (r + 1) * p)
            diff = q["cs"][:, r:r + 1] - q["cs_t"][r:r + 1, :]
            lm = jnp.where(q["li"] >= q["si"], jnp.exp(jnp.minimum(diff, 0.0)), 0.0)
            y_ref[:, sl] = _dot(g * lm, xd[:, sl], NN)
        y_ref[...] += q["e_x"] * _dot(q["cm"], st, NN) + q["d_x"] * q["x"]
        state[...] = q["el_x"] * st + _dot(q["bm"].T, xd * q["dec_x"], NN)

    return pl.pallas_call(
        body, name=name, grid=(SSM_GROUPS, nc),
        in_specs=[pl.BlockSpec((l, XBC_GROUP_W), lambda g, c: (c, g)),
                  pl.BlockSpec((1, l, SSM_HPG), lambda g, c: (g, c, 0)),
                  pl.BlockSpec((1, SSM_HPG, l), lambda g, c: (g, 0, c)),
                  pl.BlockSpec((1, 8, 8), lambda g, c: (g, 0, 0)),
                  pl.BlockSpec((1, 8, 8), lambda g, c: (g, 0, 0))],
        out_specs=[pl.BlockSpec((l, SSM_GROUP_W), lambda g, c: (c, g)),
                   pl.BlockSpec((1, SSM_D_STATE, SSM_GROUP_W), lambda g, c: (c, 0, g))],
        out_shape=[jax.ShapeDtypeStruct((t, SSM_D_INNER), f32),
                   jax.ShapeDtypeStruct((nc, SSM_D_STATE, SSM_D_INNER), f32)],
        scratch_shapes=[pltpu.VMEM((SSM_D_STATE, SSM_GROUP_W), f32)],
        compiler_params=pltpu.CompilerParams(dimension_semantics=("parallel", "arbitrary")),
    )(xbc, dtr, dtr_t, par_row, par_col)


def ssd_bwd(xbc, dtr, dtr_t, par_row, par_col, s_in, dy, *, name):
    t = xbc.shape[0]
    nc = t // SSM_CHUNK
    l, p = SSM_CHUNK, SSM_HEAD_DIM

    def body(xbc_ref, dtr_ref, dtrT_ref, prow_ref, pcol_ref, sin_ref, dy_ref, dxbc_ref, ddtr_ref, dpar_ref,
             dstate, yd_buf, dxd_buf):
        @pl.when(pl.program_id(1) == 0)
        def _():
            dstate[...] = jnp.zeros_like(dstate)
            dpar_ref[...] = jnp.zeros_like(dpar_ref)

        q = _ssd_common(xbc_ref, dtr_ref, dtrT_ref, prow_ref, pcol_ref)
        x, bm, cm, ex, li, si = q["x"], q["bm"], q["cm"], q["ex"], q["li"], q["si"]
        e_x, el_x, dec_x = q["e_x"], q["el_x"], q["dec_x"]
        st = sin_ref[0]
        dst = dstate[...]
        dy = dy_ref[...]
        xd = x * q["dt_x"]
        g = _dot(cm, bm, NT)
        dg = jnp.zeros((l, l), f32)
        for r in range(SSM_HPG):
            sl = slice(r * p, (r + 1) * p)
            diff = q["cs"][:, r:r + 1] - q["cs_t"][r:r + 1, :]
            lm = jnp.where(li >= si, jnp.exp(jnp.minimum(diff, 0.0)), 0.0)
            m = (g * lm).astype(bf16)
            xdh, dyh = xd[:, sl].astype(bf16), dy[:, sl].astype(bf16)
            yd_buf[:, sl] = _dot(m, xdh, NN)
            dxd_buf[:, sl] = _dot(m, dyh, TN)
            dg = dg + _dot(dyh, xdh, NT) * lm
        yd, dxd_diag = yd_buf[...], dxd_buf[...]
        yo = e_x * _dot(cm, st, NN)
        dz = e_x * dy
        wv = _dot(bm, dst, NN)
        xw = xd * wv * dec_x
        row8 = lax.broadcasted_iota(jnp.int32, (l, SSM_HPG), 0)
        dy_b, xd_b = dy.astype(bf16).astype(f32), xd.astype(bf16).astype(f32)
        dcs = _dot_01(dy_b * yd - xd_b * dxd_diag + dy * yo - xw, ex, NT, 0, 3)
        tail = jnp.sum(xw, axis=0, keepdims=True) + el_x * jnp.sum(dst * st, axis=0, keepdims=True)
        dcl = _dot_01(jnp.broadcast_to(tail, (SSM_HPG, SSM_GROUP_W)), ex, NT, 0, 3)[0:1, :]
        dcs = dcs + jnp.where(row8 == l - 1, dcl, 0.0)
        dda = _dot_01(q["tri"], dcs, TN, 1, 3)
        dxd = dxd_diag + dec_x * wv
        ddt = _dot_01(dxd * x, ex, NT, 0, 3) + dda * q["a_row"]
        ddtr = ddt * _sigmoid(q["dtr"] + q["bias_row"])
        ddtr_ref[0] = ddtr
        dd = _dot_01(jnp.broadcast_to(jnp.sum(dy * x, axis=0, keepdims=True), (SSM_HPG, SSM_GROUP_W)), ex, NT, 0, 2)[0:1, :]
        dpar_ref[0, 0:1, :] += jnp.sum(ddtr, axis=0, keepdims=True)
        dpar_ref[0, 1:2, :] += jnp.sum(dda * q["dt"], axis=0, keepdims=True) * q["a_row"]
        dpar_ref[0, 2:3, :] += dd
        dxbc_ref[:, 0:SSM_GROUP_W] = dxd * q["dt_x"] + q["d_x"] * dy
        dxbc_ref[:, SSM_GROUP_W:SSM_GROUP_W + SSM_D_STATE] = _dot(dg, cm, TN) + _dot(xd * dec_x, dst, NT)
        dxbc_ref[:, SSM_GROUP_W + SSM_D_STATE:XBC_GROUP_W] = _dot(dg, bm, NN) + _dot(dz, st, NT)
        dstate[...] = _dot(cm.T, dz, NN) + el_x * dst

    rc = lambda c: nc - 1 - c
    return pl.pallas_call(
        body, name=name, grid=(SSM_GROUPS, nc),
        in_specs=[pl.BlockSpec((l, XBC_GROUP_W), lambda g, c: (rc(c), g)),
                  pl.BlockSpec((1, l, SSM_HPG), lambda g, c: (g, rc(c), 0)),
                  pl.BlockSpec((1, SSM_HPG, l), lambda g, c: (g, 0, rc(c))),
                  pl.BlockSpec((1, 8, 8), lambda g, c: (g, 0, 0)),
                  pl.BlockSpec((1, 8, 8), lambda g, c: (g, 0, 0)),
                  pl.BlockSpec((1, SSM_D_STATE, SSM_GROUP_W), lambda g, c: (rc(c), 0, g)),
                  pl.BlockSpec((l, SSM_GROUP_W), lambda g, c: (rc(c), g))],
        out_specs=[pl.BlockSpec((l, XBC_GROUP_W), lambda g, c: (rc(c), g)),
                   pl.BlockSpec((1, l, SSM_HPG), lambda g, c: (g, rc(c), 0)),
                   pl.BlockSpec((1, 8, 8), lambda g, c: (g, 0, 0))],
        out_shape=[jax.ShapeDtypeStruct((t, SSM_CONV_DIM), f32),
                   jax.ShapeDtypeStruct((SSM_GROUPS, t, SSM_HPG), f32),
                   jax.ShapeDtypeStruct((SSM_GROUPS, 8, 8), f32)],
        scratch_shapes=[pltpu.VMEM((SSM_D_STATE, SSM_GROUP_W), f32), pltpu.VMEM((l, SSM_GROUP_W), f32),
                        pltpu.VMEM((l, SSM_GROUP_W), f32)],
        compiler_params=pltpu.CompilerParams(dimension_semantics=("parallel", "arbitrary")),
    )(xbc, dtr, dtr_t, par_row, par_col, s_in, dy)


def gnorm_fwd(y, proj, w, *, name):
    t = y.shape[0]
    tr = _pick(t, 2048, 8)
    gw = SSM_GROUP_W
    zb = OFF_Z // gw

    def body(y_ref, z_ref, w_ref, o_ref):
        y2 = y_ref[...] * _silu(z_ref[...])
        r = lax.rsqrt(jnp.mean(y2 * y2, axis=-1, keepdims=True) + RMS_EPS)
        o_ref[...] = (y2 * r * w_ref[...]).astype(bf16)

    return pl.pallas_call(
        body, name=name, grid=(SSM_GROUPS, t // tr),
        in_specs=[pl.BlockSpec((tr, gw), lambda g, i: (i, g)), pl.BlockSpec((tr, gw), lambda g, i: (i, zb + g)),
                  pl.BlockSpec((1, gw), lambda g, i: (0, g))],
        out_specs=pl.BlockSpec((tr, gw), lambda g, i: (i, g)), out_shape=jax.ShapeDtypeStruct((t, SSM_D_INNER), bf16),
    )(y, proj, w)


def gnorm_bwd(y, proj, w, dout, dst, *, name):
    t = y.shape[0]
    tr = _pick(t, 2048, 8)
    gw = SSM_GROUP_W
    zb = OFF_Z // gw

    def body(_, y_ref, z_ref, w_ref, do_ref, dy_ref, dz_ref, dw_ref):
        yv, zv = y_ref[...], z_ref[...]
        sz = _silu(zv)
        y2 = yv * sz
        dy2, dw = _rms_bwd_math(y2, w_ref[...], do_ref[...].astype(f32))
        dy_ref[...] = dy2 * sz
        dz_ref[...] = (dy2 * yv * _dsilu(zv)).astype(bf16)

        @pl.when(pl.program_id(1) == 0)
        def _():
            dw_ref[...] = jnp.zeros_like(dw_ref)

        dw_ref[...] += dw

    tile = pl.BlockSpec((tr, gw), lambda g, i: (i, g))
    vec = pl.BlockSpec((1, gw), lambda g, i: (0, g))
    return pl.pallas_call(
        body, name=name, grid=(SSM_GROUPS, t // tr),
        in_specs=[ANY, tile, pl.BlockSpec((tr, gw), lambda g, i: (i, zb + g)), vec, tile],
        out_specs=[tile, pl.BlockSpec((tr, gw), lambda g, i: (i, zb + g)), vec],
        out_shape=[jax.ShapeDtypeStruct((t, SSM_D_INNER), f32), jax.ShapeDtypeStruct(dst.shape, bf16),
                   jax.ShapeDtypeStruct((1, SSM_D_INNER), f32)],
        input_output_aliases={0: 1},
        compiler_params=pltpu.CompilerParams(dimension_semantics=("parallel", "arbitrary")),
    )(dst, y, proj, w, dout)


LRU_ROWS = 2048


def _lru_gates(uv, wr_ref, wi_ref, br_ref, bi_ref, lam_ref):
    rg = _sigmoid(_dot(uv, wr_ref[0], NN) + br_ref[...])
    ig = _sigmoid(_dot(uv, wi_ref[0], NN) + bi_ref[...])
    sp = _softplus(-lam_ref[...])
    la = -LRU_C * rg * sp
    a = jnp.exp(la)
    s = jnp.sqrt(jnp.maximum(-_expm1(2.0 * la), 0.0))
    return rg, ig, sp, la, a, s


def lru_fwd(u, proj, w_r, b_r, w_i, b_i, lam, *, name):
    t = u.shape[0]
    r = LRU_ROWS
    lb = LRU_BLOCK
    yb = OFF_LY // lb

    def body(u_ref, y_ref, wr_ref, br_ref, wi_ref, bi_ref, lam_ref, h_ref, o_ref, carry):
        @pl.when(pl.program_id(1) == 0)
        def _():
            carry[...] = jnp.zeros_like(carry)

        uv = u_ref[...]
        _, ig, _, _, a, s = _lru_gates(uv, wr_ref, wi_ref, br_ref, bi_ref, lam_ref)
        b = s * ig * uv
        row = lax.broadcasted_iota(jnp.int32, (r, lb), 0)
        d = 1
        while d < r:
            keep = row >= d
            b = b + a * jnp.where(keep, pltpu.roll(b, d, 0), 0.0)
            a = a * jnp.where(keep, pltpu.roll(a, d, 0), 1.0)
            d *= 2
        h = b + a * carry[0:1, :]
        carry[0:1, :] = h[r - 1:r, :]
        h_ref[...] = h
        o_ref[...] = (h * _gelu(y_ref[...])).astype(bf16)

    tile = pl.BlockSpec((r, lb), lambda hb, j: (j, hb))
    vec = pl.BlockSpec((1, lb), lambda hb, j: (0, hb))
    wsp = pl.BlockSpec((1, lb, lb), lambda hb, j: (hb, 0, 0))
    return pl.pallas_call(
        body, name=name, grid=(LRU_BLOCKS, t // r),
        in_specs=[tile, pl.BlockSpec((r, lb), lambda hb, j: (j, yb + hb)), wsp, vec, wsp, vec, vec],
        out_specs=[tile, tile],
        out_shape=[jax.ShapeDtypeStruct((t, LRU_WIDTH), f32), jax.ShapeDtypeStruct((t, LRU_WIDTH), bf16)],
        scratch_shapes=[pltpu.VMEM((8, lb), f32)],
        compiler_params=pltpu.CompilerParams(dimension_semantics=("parallel", "arbitrary")),
    )(u, proj, w_r, b_r, w_i, b_i, lam)


def lru_bwd(u, proj, hseq, dout, w_r, b_r, w_i, b_i, lam, dst, *, name):
    t = u.shape[0]
    r = LRU_ROWS
    nt = t // r
    lb = LRU_BLOCK
    yb = OFF_LY // lb

    def body(_, u_ref, y_ref, h_ref, hp_ref, do_ref, wr_ref, br_ref, wi_ref, bi_ref, lam_ref,
             du_ref, dy_ref, dwr_ref, dwi_ref, dbr_ref, dbi_ref, dlam_ref, carry_dh, carry_a):
        j = pl.program_id(1)

        @pl.when(j == 0)
        def _():
            carry_dh[...] = jnp.zeros_like(carry_dh)
            carry_a[...] = jnp.zeros_like(carry_a)
            dwr_ref[...] = jnp.zeros_like(dwr_ref)
            dwi_ref[...] = jnp.zeros_like(dwi_ref)
            dbr_ref[...] = jnp.zeros_like(dbr_ref)
            dbi_ref[...] = jnp.zeros_like(dbi_ref)
            dlam_ref[...] = jnp.zeros_like(dlam_ref)

        uv = u_ref[...]
        yv = y_ref[...]
        hv = h_ref[...]
        dov = do_ref[...]
        rg, ig, sp, la, a, s = _lru_gates(uv, wr_ref, wi_ref, br_ref, bi_ref, lam_ref)
        dy_ref[...] = (dov * hv * _dgelu(yv)).astype(bf16)
        gq = dov * _gelu(yv)
        row = lax.broadcasted_iota(jnp.int32, (r, lb), 0)
        an = jnp.where(row < r - 1, pltpu.roll(a, r - 1, 0), carry_a[0:1, :])
        d = 1
        while d < r:
            keep = row < r - d
            gq = gq + an * jnp.where(keep, pltpu.roll(gq, r - d, 0), 0.0)
            an = an * jnp.where(keep, pltpu.roll(an, r - d, 0), 1.0)
            d *= 2
        dh = gq + an * carry_dh[0:1, :]
        carry_dh[0:1, :] = dh[0:1, :]
        carry_a[0:1, :] = a[0:1, :]
        first = jnp.where(j == nt - 1, 0.0, 1.0) * hp_ref[7:8, :]
        hprev = jnp.where(row >= 1, pltpu.roll(hv, 1, 0), first)
        da = dh * hprev
        iu = ig * uv
        e2 = jnp.exp(2.0 * la)
        dla = da * a - dh * iu * e2 / jnp.maximum(s, 1e-30)
        drp = dla * (-LRU_C * sp) * rg * (1.0 - rg)
        dip = dh * s * uv * ig * (1.0 - ig)
        dlam_ref[...] += jnp.sum(dla * (LRU_C * rg) * _sigmoid(-lam_ref[...]), axis=0, keepdims=True)
        du_ref[...] = dh * s * ig + _dot(drp, wr_ref[0], NT) + _dot(dip, wi_ref[0], NT)
        dwr_ref[0] += _dot(uv, drp, TN)
        dwi_ref[0] += _dot(uv, dip, TN)
        dbr_ref[...] += jnp.sum(drp, axis=0, keepdims=True)
        dbi_ref[...] += jnp.sum(dip, axis=0, keepdims=True)

    rj = lambda j: nt - 1 - j
    tile = pl.BlockSpec((r, lb), lambda hb, j: (rj(j), hb))
    vec = pl.BlockSpec((1, lb), lambda hb, j: (0, hb))
    wsp = pl.BlockSpec((1, lb, lb), lambda hb, j: (hb, 0, 0))
    hprev_spec = pl.BlockSpec((8, lb), lambda hb, j: (jnp.maximum(rj(j) * (r // 8) - 1, 0), hb))
    ywin = pl.BlockSpec((r, lb), lambda hb, j: (rj(j), yb + hb))
    return pl.pallas_call(
        body, name=name, grid=(LRU_BLOCKS, nt),
        in_specs=[ANY, tile, ywin, tile, hprev_spec, tile, wsp, vec, wsp, vec, vec],
        out_specs=[tile, ywin, wsp, wsp, vec, vec, vec],
        out_shape=[jax.ShapeDtypeStruct((t, LRU_WIDTH), f32), jax.ShapeDtypeStruct(dst.shape, bf16),
                   jax.ShapeDtypeStruct((LRU_BLOCKS, lb, lb), f32), jax.ShapeDtypeStruct((LRU_BLOCKS, lb, lb), f32),
                   jax.ShapeDtypeStruct((1, LRU_WIDTH), f32), jax.ShapeDtypeStruct((1, LRU_WIDTH), f32),
                   jax.ShapeDtypeStruct((1, LRU_WIDTH), f32)],
        input_output_aliases={0: 1},
        scratch_shapes=[pltpu.VMEM((8, lb), f32), pltpu.VMEM((8, lb), f32)],
        compiler_params=pltpu.CompilerParams(dimension_semantics=("parallel", "arbitrary")),
    )(dst, u, proj, hseq, hseq, dout, w_r, b_r, w_i, b_i, lam)


def merge_fwd(proj, bg, y_ssm, y_lru, *, name):
    t, d = y_ssm.shape
    tr = _pick(t, 512, 8)
    gb = OFF_GATES // d

    def body(gs_ref, gl_ref, bs_ref, bl_ref, ys_ref, yl_ref, o_ref):
        gs = _sigmoid(gs_ref[...] + bs_ref[...])
        gl = _sigmoid(gl_ref[...] + bl_ref[...])
        o_ref[...] = (gs * ys_ref[...].astype(f32) + gl * yl_ref[...].astype(f32)).astype(bf16)

    row = pl.BlockSpec((tr, d), lambda i: (i, 0))
    return pl.pallas_call(
        body, name=name, grid=(t // tr,),
        in_specs=[pl.BlockSpec((tr, d), lambda i: (i, gb)), pl.BlockSpec((tr, d), lambda i: (i, gb + 1)),
                  pl.BlockSpec((1, d), lambda i: (0, 0)), pl.BlockSpec((1, d), lambda i: (0, 1)), row, row],
        out_specs=row, out_shape=jax.ShapeDtypeStruct((t, d), bf16),
    )(proj, proj, bg, bg, y_ssm, y_lru)


def merge_bwd(proj, bg, y_ssm, y_lru, dmix, *, name):
    t, d = y_ssm.shape
    tr = _pick(t, 512, 8)
    gb = OFF_GATES // d

    def body(gs_ref, gl_ref, bs_ref, bl_ref, ys_ref, yl_ref, dm_ref, dg_ref, dys_ref, dyl_ref, dbg_ref):
        gs = _sigmoid(gs_ref[...] + bs_ref[...])
        gl = _sigmoid(gl_ref[...] + bl_ref[...])
        dm = dm_ref[...].astype(f32)
        dys_ref[...] = (dm * gs).astype(bf16)
        dyl_ref[...] = (dm * gl).astype(bf16)
        dgs = dm * ys_ref[...].astype(f32) * gs * (1.0 - gs)
        dgl = dm * yl_ref[...].astype(f32) * gl * (1.0 - gl)
        dg_ref[:, 0:d] = dgs.astype(bf16)
        dg_ref[:, d:2 * d] = dgl.astype(bf16)

        @pl.when(pl.program_id(0) == 0)
        def _():
            dbg_ref[...] = jnp.zeros_like(dbg_ref)

        dbg_ref[:, 0:d] += jnp.sum(dgs, axis=0, keepdims=True)
        dbg_ref[:, d:2 * d] += jnp.sum(dgl, axis=0, keepdims=True)

    row = pl.BlockSpec((tr, d), lambda i: (i, 0))
    return pl.pallas_call(
        body, name=name, grid=(t // tr,),
        in_specs=[pl.BlockSpec((tr, d), lambda i: (i, gb)), pl.BlockSpec((tr, d), lambda i: (i, gb + 1)),
                  pl.BlockSpec((1, d), lambda i: (0, 0)), pl.BlockSpec((1, d), lambda i: (0, 1)), row, row, row],
        out_specs=[pl.BlockSpec((tr, 2 * d), lambda i: (i, OFF_GATES // (2 * d))), row, row,
                   pl.BlockSpec((1, 2 * d), lambda i: (0, 0))],
        out_shape=[jax.ShapeDtypeStruct((t, PROJ_W), bf16), jax.ShapeDtypeStruct((t, d), bf16),
                   jax.ShapeDtypeStruct((t, d), bf16), jax.ShapeDtypeStruct((1, 2 * d), f32)],
        compiler_params=pltpu.CompilerParams(dimension_semantics=("arbitrary",)),
    )(proj, proj, bg, bg, y_ssm, y_lru, dmix)


def swiglu_fwd(ff, *, name):
    t = ff.shape[0]
    hd = FFN_HIDDEN
    tr = _pick(t, 512, 8)

    def body(f_ref, o_ref):
        o_ref[...] = (_silu(f_ref[:, 0:hd].astype(f32)) * f_ref[:, hd:2 * hd].astype(f32)).astype(bf16)

    return pl.pallas_call(
        body, name=name, grid=(t // tr,), in_specs=[pl.BlockSpec((tr, 2 * hd), lambda i: (i, 0))],
        out_specs=pl.BlockSpec((tr, hd), lambda i: (i, 0)), out_shape=jax.ShapeDtypeStruct((t, hd), bf16),
    )(ff)


def swiglu_bwd(ff, dact, *, name):
    t = ff.shape[0]
    hd = FFN_HIDDEN
    tr = _pick(t, 512, 8)

    def body(f_ref, d_ref, o_ref):
        gate, up, dv = f_ref[:, 0:hd].astype(f32), f_ref[:, hd:2 * hd].astype(f32), d_ref[...].astype(f32)
        o_ref[:, 0:hd] = (dv * up * _dsilu(gate)).astype(bf16)
        o_ref[:, hd:2 * hd] = (dv * _silu(gate)).astype(bf16)

    return pl.pallas_call(
        body, name=name, grid=(t // tr,),
        in_specs=[pl.BlockSpec((tr, 2 * hd), lambda i: (i, 0)), pl.BlockSpec((tr, hd), lambda i: (i, 0))],
        out_specs=pl.BlockSpec((tr, 2 * hd), lambda i: (i, 0)), out_shape=jax.ShapeDtypeStruct((t, 2 * hd), bf16),
    )(ff, dact)


def _adam_math(w, g, m, v):
    m = ADAM_B1 * m + (1.0 - ADAM_B1) * g
    v = ADAM_B2 * v + (1.0 - ADAM_B2) * (g * g)
    m_hat = m / (1.0 - ADAM_B1 ** ADAM_STEP)
    v_hat = v / (1.0 - ADAM_B2 ** ADAM_STEP)
    delta = -ADAM_LR * (m_hat / (jnp.sqrt(v_hat) + ADAM_EPS) + ADAM_WD * w)
    return delta, m, v


def _row_tile(rows, cols):
    cap = max(8, (1 << 19) // cols)
    return _pick(rows, cap, 8) if rows % 8 == 0 else rows


def adamw(w, g, m, v, *, name):
    rows, cols = w.shape
    tr = _row_tile(rows, cols)

    def body(w_ref, g_ref, m_ref, v_ref, d_ref, nm_ref, nv_ref):
        d, nm, nv = _adam_math(w_ref[...], g_ref[...], m_ref[...], v_ref[...])
        d_ref[...] = d
        nm_ref[...] = nm
        nv_ref[...] = nv

    tile = pl.BlockSpec((tr, cols), lambda i: (i, 0))
    return pl.pallas_call(
        body, name=name, grid=(rows // tr,), in_specs=[tile] * 4, out_specs=[tile] * 3,
        out_shape=[jax.ShapeDtypeStruct((rows, cols), f32)] * 3,
    )(w, g, m, v)


def adamw_many(ws, gs, ms, vs, *, name):
    n = len(ws)

    def body(*refs):
        for i in range(n):
            d, nm, nv = _adam_math(refs[i][...], refs[n + i][...], refs[2 * n + i][...], refs[3 * n + i][...])
            refs[4 * n + 3 * i][...] = d
            refs[4 * n + 3 * i + 1][...] = nm
            refs[4 * n + 3 * i + 2][...] = nv

    outs = pl.pallas_call(
        body, name=name, out_shape=[jax.ShapeDtypeStruct(w.shape, f32) for w in ws for _ in range(3)],
    )(*ws, *gs, *ms, *vs)
    return [tuple(outs[3 * i:3 * i + 3]) for i in range(n)]


def pair_add(dw, rbuf, idx, *, name):
    n, rows, cols = dw.shape
    hr = rows // 2
    tr = _row_tile(hr, cols)
    nrt = hr // tr

    def body(idx_ref, a_ref, b_ref, o_ref, own_ref):
        s = a_ref[...] + b_ref[...]
        o_ref[...] = s.astype(bf16)

        @pl.when(pl.program_id(1) == idx_ref[0])
        def _():
            own_ref[...] = s[0]

    return pl.pallas_call(
        body, name=name,
        grid_spec=pltpu.PrefetchScalarGridSpec(
            num_scalar_prefetch=1, grid=(nrt, n),
            in_specs=[pl.BlockSpec((1, tr, cols), lambda i, k, idx: (k, idx[1] * nrt + i, 0)),
                      pl.BlockSpec((1, tr, cols), lambda i, k, idx: (k, i, 0))],
            out_specs=[pl.BlockSpec((1, tr, cols), lambda i, k, idx: (k, i, 0)),
                       pl.BlockSpec((tr, cols), lambda i, k, idx: (i, 0))]),
        out_shape=[jax.ShapeDtypeStruct((n, hr, cols), bf16), jax.ShapeDtypeStruct((hr, cols), f32)],
    )(idx, dw, rbuf)


def chip_sum(own, rbuf, idx, *, name):
    hr, cols = own.shape
    tr = _row_tile(hr, cols)
    nrt = hr // tr

    def body(idx_ref, a_ref, b_ref, o_ref):
        o_ref[...] = ((a_ref[...] + b_ref[0].astype(f32)) + b_ref[1].astype(f32)) + b_ref[2].astype(f32)

    return pl.pallas_call(
        body, name=name,
        grid_spec=pltpu.PrefetchScalarGridSpec(
            num_scalar_prefetch=1, grid=(nrt,),
            in_specs=[pl.BlockSpec((tr, cols), lambda i, idx: (i, 0)),
                      pl.BlockSpec((3, tr, cols), lambda i, idx: (0, i, 0))],
            out_specs=pl.BlockSpec((tr, cols), lambda i, idx: (idx[1] * nrt + i, 0))),
        out_shape=jax.ShapeDtypeStruct((2 * hr, cols), f32),
    )(idx, own, rbuf)


def sum8(rbuf, *, name):
    n, rows, cols = rbuf.shape
    tr = _row_tile(rows, cols * n)

    def body(a_ref, o_ref):
        acc = a_ref[0]
        for k in range(1, n):
            acc = acc + a_ref[k]
        o_ref[...] = acc

    return pl.pallas_call(
        body, name=name, grid=(rows // tr,), in_specs=[pl.BlockSpec((n, tr, cols), lambda i: (0, i, 0))],
        out_specs=pl.BlockSpec((tr, cols), lambda i: (i, 0)), out_shape=jax.ShapeDtypeStruct((rows, cols), f32),
    )(rbuf)


def _coords():
    return lax.axis_index("x"), lax.axis_index("y"), lax.axis_index("c")


def _other_chips(x, y):
    return [(1 - x, y), (x, 1 - y), (1 - x, 1 - y)]


def gather_weights(shards, *, name):
    n = len(shards)
    halves = [s.shape[0] // 2 for s in shards]

    def body(*refs):
        ins, outs = refs[:n], refs[n:2 * n]
        send1, recv1, send2, recv2 = refs[2 * n:]
        x, y, c = _coords()
        me = 2 * x + y
        chips = _other_chips(x, y)
        sibling = (x, y, 1 - c)

        def half(i, k, hc):
            return outs[i].at[k, pl.ds(hc * halves[i], halves[i]), :]

        def ici(i, j):
            return pltpu.make_async_remote_copy(
                src_ref=ins[i].at[pl.ds(c * halves[i], halves[i]), :], dst_ref=half(i, me, c),
                send_sem=send1.at[i, j], recv_sem=recv1.at[i, j], device_id=(*chips[j], c), device_id_type=MESH)

        def landed(i, j):
            kj = 2 * chips[j][0] + chips[j][1]
            return pltpu.make_async_remote_copy(
                src_ref=half(i, kj, c), dst_ref=half(i, kj, c),
                send_sem=send2.at[i, j], recv_sem=recv1.at[i, j], device_id=sibling, device_id_type=MESH)

        def from_sibling(i, j):
            kj = 2 * chips[j][0] + chips[j][1]
            return pltpu.make_async_remote_copy(
                src_ref=half(i, kj, 1 - c), dst_ref=half(i, kj, 1 - c),
                send_sem=send2.at[i, j], recv_sem=recv2.at[i, j], device_id=sibling, device_id_type=MESH)

        def d2d(i, j):
            kj = 2 * chips[j][0] + chips[j][1]
            return pltpu.make_async_remote_copy(
                src_ref=half(i, kj, c), dst_ref=half(i, kj, c),
                send_sem=send2.at[i, j], recv_sem=recv2.at[i, j], device_id=sibling, device_id_type=MESH)

        for j in range(3):
            for i in range(n):
                ici(i, j).start()
        for j in range(3):
            for i in range(n):
                landed(i, j).wait_recv()
                d2d(i, j).start()
        for j in range(3):
            for i in range(n):
                from_sibling(i, j).wait_recv()
        for j in range(3):
            for i in range(n):
                ici(i, j).wait_send()
                d2d(i, j).wait_send()

    return pl.pallas_call(
        body, name=name, in_specs=[ANY] * n, out_specs=[ANY] * n,
        out_shape=[jax.ShapeDtypeStruct((N_CHIPS,) + s.shape, s.dtype) for s in shards],
        scratch_shapes=[pltpu.SemaphoreType.DMA((n, 3))] * 4,
    )(*shards)


def pair_exchange(grads, *, name):
    n = len(grads)
    halves = [g.shape[1] // 2 for g in grads]

    def body(*refs):
        ins, outs = refs[:n], refs[n:2 * n]
        send, recv = refs[2 * n:]
        x, y, c = _coords()
        cps = [pltpu.make_async_remote_copy(
            src_ref=ins[i].at[:, pl.ds((1 - c) * halves[i], halves[i]), :], dst_ref=outs[i],
            send_sem=send.at[i], recv_sem=recv.at[i], device_id=(x, y, 1 - c), device_id_type=MESH) for i in range(n)]
        for cp in cps:
            cp.start()
        for cp in cps:
            cp.wait()

    return pl.pallas_call(
        body, name=name, in_specs=[ANY] * n, out_specs=[ANY] * n,
        out_shape=[jax.ShapeDtypeStruct((N_CHIPS, g.shape[1] // 2, g.shape[2]), g.dtype) for g in grads],
        scratch_shapes=[pltpu.SemaphoreType.DMA((n,))] * 2,
    )(*grads)


def pair_gather(bufs, *, name):
    n = len(bufs)

    def body(*refs):
        ins, outs = refs[:n], refs[n:2 * n]
        send, recv = refs[2 * n:]
        x, y, c = _coords()
        cps = []
        for i in range(n):
            hr = ins[i].shape[0] // 2
            cps.append(pltpu.make_async_remote_copy(
                src_ref=ins[i].at[pl.ds(c * hr, hr), :], dst_ref=outs[i].at[pl.ds(c * hr, hr), :],
                send_sem=send.at[i], recv_sem=recv.at[i], device_id=(x, y, 1 - c), device_id_type=MESH))
        for cp in cps:
            cp.start()
        for i in range(n):
            hr = ins[i].shape[0] // 2
            pltpu.make_async_remote_copy(
                src_ref=ins[i].at[pl.ds((1 - c) * hr, hr), :], dst_ref=outs[i].at[pl.ds((1 - c) * hr, hr), :],
                send_sem=send.at[i], recv_sem=recv.at[i], device_id=(x, y, 1 - c), device_id_type=MESH).wait_recv()
        for cp in cps:
            cp.wait_send()

    return pl.pallas_call(
        body, name=name, in_specs=[ANY] * n, out_specs=[ANY] * n,
        out_shape=[jax.ShapeDtypeStruct(b.shape, b.dtype) for b in bufs],
        input_output_aliases={i: i for i in range(n)},
        scratch_shapes=[pltpu.SemaphoreType.DMA((n,))] * 2,
    )(*bufs)


def all_exchange(buf, *, name):
    rows, cols = buf.shape

    def body(in_ref, out_ref, send, recv):
        x, y, c = _coords()
        me = 4 * x + 2 * y + c
        cps = []
        for d in range(1, 8):
            px = 1 - x if d & 4 else x
            py = 1 - y if d & 2 else y
            pc = 1 - c if d & 1 else c
            cps.append(pltpu.make_async_remote_copy(
                src_ref=in_ref, dst_ref=out_ref.at[me], send_sem=send.at[d - 1], recv_sem=recv.at[d - 1],
                device_id=(px, py, pc), device_id_type=MESH))
        for cp in cps:
            cp.start()
        for d in range(1, 8):
            px = 1 - x if d & 4 else x
            py = 1 - y if d & 2 else y
            pc = 1 - c if d & 1 else c
            src = 4 * px + 2 * py + pc
            pltpu.make_async_remote_copy(
                src_ref=in_ref, dst_ref=out_ref.at[src], send_sem=send.at[d - 1], recv_sem=recv.at[d - 1],
                device_id=(px, py, pc), device_id_type=MESH).wait_recv()
        for cp in cps:
            cp.wait_send()

    return pl.pallas_call(
        body, name=name, in_specs=[ANY], out_specs=ANY,
        out_shape=jax.ShapeDtypeStruct((8, rows, cols), buf.dtype),
        scratch_shapes=[pltpu.SemaphoreType.DMA((7,)), pltpu.SemaphoreType.DMA((7,))],
    )(buf)


HBM = pl.BlockSpec(memory_space=pltpu.HBM)
SEM = pl.BlockSpec(memory_space=pltpu.SEMAPHORE)
EFFECT = pltpu.SideEffectType.DATAFLOW_SIDE_EFFECTING


def split_start(arrays, after, copies, sem_shape, *, name):
    na = len(arrays)

    def body(*refs):
        for cp in copies(refs[:na], refs[na + 1], refs[na + 2]):
            cp.start()
        refs[-1][...] = jnp.zeros((8, 128), f32)

    outs = pl.pallas_call(
        body, name=name,
        out_shape=(pltpu.SemaphoreType.DMA(sem_shape), pltpu.SemaphoreType.DMA(sem_shape),
                   *[pltpu.HBM(a.shape, a.dtype) for a in arrays], jax.ShapeDtypeStruct((8, 128), f32)),
        in_specs=[HBM] * na + [ANY], out_specs=(SEM, SEM, *[HBM] * na, pl.BlockSpec(memory_space=pltpu.VMEM)),
        input_output_aliases={i: 2 + i for i in range(na)},
        compiler_params=pltpu.CompilerParams(has_side_effects=EFFECT),
    )(*[pltpu.with_memory_space_constraint(a, pltpu.HBM) for a in arrays], after)
    return outs[0], outs[1], list(outs[2:2 + na]), outs[-1]


def split_wait(send, recv, arrays, after, copies, *, name):
    na = len(arrays)

    def body(*refs):
        for cp in copies(refs[:na], refs[na], refs[na + 1]):
            cp.wait_send()
            cp.wait_recv()

    outs = pl.pallas_call(
        body, name=name, out_shape=tuple(pltpu.HBM(a.shape, a.dtype) for a in arrays),
        in_specs=[HBM] * na + [SEM, SEM, ANY], out_specs=tuple([HBM] * na),
        input_output_aliases={i: i for i in range(na)},
        compiler_params=pltpu.CompilerParams(has_side_effects=EFFECT),
    )(*arrays, send, recv, after)
    return list(outs)


def gather_copies(n):
    def copies(refs, send, recv):
        x, y, c = _coords()
        me = 2 * x + y
        chips = _other_chips(x, y)
        return [pltpu.make_async_remote_copy(
            src_ref=refs[i], dst_ref=refs[n + i].at[me], send_sem=send.at[3 * i + j], recv_sem=recv.at[3 * i + j],
            device_id=(*chips[j], c), device_id_type=MESH) for j in range(3) for i in range(n)]
    return copies


def pair_copies(n):
    def copies(refs, send, recv):
        x, y, c = _coords()
        cps = []
        for i in range(n):
            hr = refs[i].shape[1] // 2
            cps.append(pltpu.make_async_remote_copy(
                src_ref=refs[i].at[:, pl.ds((1 - c) * hr, hr), :], dst_ref=refs[n + i], send_sem=send.at[i],
                recv_sem=recv.at[i], device_id=(x, y, 1 - c), device_id_type=MESH))
        return cps
    return copies


def all_copies():
    def copies(refs, send, recv):
        x, y, c = _coords()
        me = 4 * x + 2 * y + c
        cps = []
        for d in range(1, 8):
            peer = (1 - x if d & 4 else x, 1 - y if d & 2 else y, 1 - c if d & 1 else c)
            cps.append(pltpu.make_async_remote_copy(
                src_ref=refs[0], dst_ref=refs[1].at[me], send_sem=send.at[d - 1], recv_sem=recv.at[d - 1],
                device_id=peer, device_id_type=MESH))
        return cps
    return copies


def reduce_copies(n):
    def copies(refs, send, recv):
        x, y, c = _coords()
        chips = _other_chips(x, y)
        return [pltpu.make_async_remote_copy(
            src_ref=refs[i].at[2 * chips[j][0] + chips[j][1]], dst_ref=refs[n + i].at[j],
            send_sem=send.at[3 * i + j], recv_sem=recv.at[3 * i + j], device_id=(*chips[j], c), device_id_type=MESH)
            for j in range(3) for i in range(n)]
    return copies


def _pack(arrs):
    flat = []
    for a in arrs:
        v = a.reshape(-1).astype(f32)
        pad = (-v.shape[0]) % 128
        flat.append(jnp.pad(v, (0, pad)) if pad else v)
    v = jnp.concatenate(flat)
    rows = v.shape[0] // 128
    pad_rows = (-rows) % 256
    v = v.reshape(rows, 128)
    return jnp.pad(v, ((0, pad_rows), (0, 0))) if pad_rows else v


def _unpack(buf, shapes):
    out, row = [], 0
    for s in shapes:
        size = math.prod(s)
        rows = -(-size // 128)
        out.append(buf[row:row + rows].reshape(-1)[:size].reshape(s))
        row += rows
    return out


def _ref_of_perm():
    ref = np.arange(IN_PROJ_DIM)
    xbc = ref[4096:7168]
    xbc_p = [np.concatenate([xbc[g * 512:(g + 1) * 512], xbc[2048 + g * 128:2048 + (g + 1) * 128],
                             xbc[2560 + g * 128:2560 + (g + 1) * 128]]) for g in range(SSM_GROUPS)]
    return np.concatenate([ref[0:2048], ref[2048:4096], ref[7200:8480], ref[8480:9760], ref[7168:7200],
                           -np.ones(DT_PAD_W - SSM_HEADS, np.int64)] + xbc_p)


def _runs(vals):
    out, start = [], 0
    for i in range(1, len(vals) + 1):
        if i == len(vals) or not (vals[i] == vals[i - 1] + 1 or (vals[i] < 0 and vals[i - 1] < 0)):
            out.append((start, int(vals[start]), i - start))
            start = i
    return out


def _perm_in_from_shards(g):
    ref_of_perm = _ref_of_perm()
    sw = IN_PROJ_DIM // N_CHIPS
    parts = []
    for _, first, length in _runs(ref_of_perm):
        if first < 0:
            parts.append(jnp.zeros((g.shape[1], length), g.dtype))
            continue
        lo = first
        while lo < first + length:
            k = lo // sw
            hi = min(first + length, (k + 1) * sw)
            parts.append(g[k, :, lo - k * sw:hi - k * sw])
            lo = hi
    return jnp.concatenate(parts, axis=-1)


def _unperm_in_to_shards(w):
    ref_of_perm = _ref_of_perm()
    perm_of_ref = np.zeros(IN_PROJ_DIM, np.int64)
    perm_of_ref[ref_of_perm[ref_of_perm >= 0]] = np.nonzero(ref_of_perm >= 0)[0]
    sw = IN_PROJ_DIM // N_CHIPS
    shards = []
    for k in range(N_CHIPS):
        runs = _runs(perm_of_ref[k * sw:(k + 1) * sw])
        shards.append(jnp.concatenate([w[:, first:first + length] for _, first, length in runs], axis=-1))
    return jnp.stack(shards)


def _perm_xbc_cols(w):
    parts = []
    for g in range(SSM_GROUPS):
        parts += [w[..., g * 512:(g + 1) * 512], w[..., 2048 + g * 128:2048 + (g + 1) * 128],
                  w[..., 2560 + g * 128:2560 + (g + 1) * 128]]
    return jnp.concatenate(parts, axis=-1)


def _unperm_xbc_cols(w):
    xs = [w[..., g * XBC_GROUP_W:g * XBC_GROUP_W + 512] for g in range(SSM_GROUPS)]
    bs = [w[..., g * XBC_GROUP_W + 512:g * XBC_GROUP_W + 640] for g in range(SSM_GROUPS)]
    cs = [w[..., g * XBC_GROUP_W + 640:(g + 1) * XBC_GROUP_W] for g in range(SSM_GROUPS)]
    return jnp.concatenate(xs + bs + cs, axis=-1)


def _from_col_shards(w):
    n, r, c = w.shape
    return jnp.transpose(w, (1, 0, 2)).reshape(r, n * c)


def kernel(x, norm1_w, w_in, b_branch_gate, ssm_conv_w, ssm_conv_b, ssm_dt_bias, ssm_a_log, ssm_d, ssm_norm_w, w_out_ssm, lru_conv_w, lru_conv_b, lru_w_r, lru_b_r, lru_w_i, lru_b_i, lru_lambda, w_out_lru, w_out, norm2_w, w_ffn_in, w_ffn_out, norm_f_w, loss_target, m_norm1_w, m_w_in, m_b_branch_gate, m_ssm_conv_w, m_ssm_conv_b, m_ssm_dt_bias, m_ssm_a_log, m_ssm_d, m_ssm_norm_w, m_w_out_ssm, m_lru_conv_w, m_lru_conv_b, m_lru_w_r, m_lru_b_r, m_lru_w_i, m_lru_b_i, m_lru_lambda, m_w_out_lru, m_w_out, m_norm2_w, m_w_ffn_in, m_w_ffn_out, m_norm_f_w, v_norm1_w, v_w_in, v_b_branch_gate, v_ssm_conv_w, v_ssm_conv_b, v_ssm_dt_bias, v_ssm_a_log, v_ssm_d, v_ssm_norm_w, v_w_out_ssm, v_lru_conv_w, v_lru_conv_b, v_lru_w_r, v_lru_b_r, v_lru_w_i, v_lru_b_i, v_lru_lambda, v_w_out_lru, v_w_out, v_norm2_w, v_w_ffn_in, v_w_ffn_out, v_norm_f_w):
    xi, yi, ci = lax.axis_index("x"), lax.axis_index("y"), lax.axis_index("c")
    me = 2 * xi + yi
    idx = jnp.stack([me, ci]).astype(jnp.int32)
    x2 = x[0]
    tgt = loss_target[0]

    big_names = ["w_in", "w_out_ssm", "w_out_lru", "w_out", "w_ffn_in", "w_ffn_out"]
    big_w = dict(w_in=w_in[0], w_out_ssm=w_out_ssm[0], w_out_lru=w_out_lru[0], w_out=w_out[0], w_ffn_in=w_ffn_in[0],
                 w_ffn_out=w_ffn_out[0])
    big_m = dict(w_in=m_w_in[0], w_out_ssm=m_w_out_ssm[0], w_out_lru=m_w_out_lru[0], w_out=m_w_out[0],
                 w_ffn_in=m_w_ffn_in[0], w_ffn_out=m_w_ffn_out[0])
    big_v = dict(w_in=v_w_in[0], w_out_ssm=v_w_out_ssm[0], w_out_lru=v_w_out_lru[0], w_out=v_w_out[0],
                 w_ffn_in=v_w_ffn_in[0], w_ffn_out=v_w_ffn_out[0])
    conv_pad = jnp.zeros((16, 768), f32).at[0:4, :].set(ssm_conv_w[0]).at[8:12, 0:320].set(lru_conv_w[0])
    mine = [big_w["w_in"].astype(bf16), conv_pad]
    gathered = gather_weights(mine, name="gather_weights")
    g_in, g_conv = [lax.dynamic_update_index_in_dim(g, s, me, 0) for g, s in zip(gathered, mine)]
    w_in_p = _perm_in_from_shards(g_in)
    late_names = big_names[1:]
    late = [big_w[k].astype(bf16) for k in late_names]
    late_lands = [lax.empty((N_CHIPS,) + s.shape, bf16) for s in late]
    g_send, g_recv, g_arrays, g_token = split_start(late + late_lands, g_conv, gather_copies(5), (15,),
                                                    name="gather_late_start")
    ssm_cw_full = _from_col_shards(g_conv[:, 0:4, :])
    lru_cw_full = _from_col_shards(g_conv[:, 8:12, 0:320])
    ssm_cw_p = _perm_xbc_cols(ssm_cw_full)
    ssm_cb_p = _perm_xbc_cols(ssm_conv_b)

    par = jnp.stack([ssm_dt_bias[0], ssm_a_log[0], ssm_d[0]], axis=0).reshape(3, SSM_GROUPS, SSM_HPG)
    par_row = jnp.zeros((SSM_GROUPS, 8, 8), f32).at[:, 0:3, :].set(jnp.transpose(par, (1, 0, 2)))
    par_col = jnp.transpose(par_row, (0, 2, 1))

    hn1 = rms_fwd(x2, norm1_w + g_token[0:1, 0:1], name="rms1_fwd")
    proj = mm(hn1, w_in_p, "nn", name="in_proj")
    t = x2.shape[0]
    dtr = jnp.transpose(proj[:, OFF_DT:OFF_DT + 32].reshape(t, SSM_GROUPS, SSM_HPG), (1, 0, 2))
    dtr_t = jnp.transpose(dtr, (0, 2, 1))
    xbc_pre, xbc_post = conv_fwd(proj, OFF_XBC, SSM_CONV_DIM, ssm_cw_p, ssm_cb_p, silu=True, name="ssm_conv_fwd")
    y_ssd, s_in = ssd_fwd(xbc_post, dtr, dtr_t, par_row, par_col, name="ssd_fwd")
    yn = gnorm_fwd(y_ssd, proj, ssm_norm_w, name="gnorm_fwd")
    g_arrays = split_wait(g_send, g_recv, g_arrays, yn, gather_copies(5), name="gather_late_wait")
    g_out_ssm, g_out_lru, g_out, g_ffn_in, g_ffn_out = [
        lax.dynamic_update_index_in_dim(g, s, me, 0) for g, s in zip(g_arrays[5:], late)]
    w_out_ssm_f = g_out_ssm.reshape(SSM_D_INNER, D_MODEL)
    w_out_lru_f = g_out_lru.reshape(LRU_WIDTH, D_MODEL)
    w_out_f = g_out.reshape(D_MODEL, D_MODEL)
    w_ffn_out_f = g_ffn_out.reshape(FFN_HIDDEN, D_MODEL)
    y_ssm = mm(yn, w_out_ssm_f, "nn", out_dtype=bf16, name="out_ssm")
    (u_lru,) = conv_fwd(proj, OFF_LX, LRU_WIDTH, lru_cw_full, lru_conv_b, silu=False, name="lru_conv_fwd")
    h_lru, o_lru = lru_fwd(u_lru, proj, lru_w_r[0], lru_b_r, lru_w_i[0], lru_b_i, lru_lambda, name="lru_fwd")
    y_lru = mm(o_lru, w_out_lru_f, "nn", out_dtype=bf16, name="out_lru")
    mix = merge_fwd(proj, b_branch_gate, y_ssm, y_lru, name="merge_fwd")
    h1, hn2 = mm(mix, w_out_f, "nn", add=x2, name="out_proj",
                 epi=(epi_rms_fwd, [], [norm2_w], [("row", f32), ("row", bf16)]))
    ff = mm(hn2, g_ffn_in, "nn", b_shards=True, out_dtype=bf16, name="ffn_in")
    act = swiglu_fwd(ff, name="swiglu_fwd")
    dh2, dh2_b, d_norm_f, loss_tile = mm(act, w_ffn_out_f, "nn", add=h1, name="ffn_out",
                                         epi=(epi_loss, [tgt], [norm_f_w.reshape(1, D_MODEL)],
                                              [("row", f32), ("row", bf16), ("vec",), ("tile",)]))

    d_w_ffn_out = mm(act, dh2_b, "tn", name="d_w_ffn_out")
    dact = mm(dh2_b, w_ffn_out_f, "nt", out_dtype=bf16, name="d_act")
    dff = swiglu_bwd(ff, dact, name="swiglu_bwd")
    d_w_ffn_in = mm(hn2, dff, "tn", out_shards=N_CHIPS, name="d_w_ffn_in")
    dh1, dh1_b, d_norm2 = mm(dff, g_ffn_in, "nt", b_shards=True, name="d_hn2",
                             epi=(epi_rms_bwd, [h1, dh2], [norm2_w], [("row", f32), ("row", bf16), ("vec",)]))
    d_w_out = mm(mix, dh1_b, "tn", name="d_w_out")
    dmix = mm(dh1_b, w_out_f, "nt", out_dtype=bf16, name="d_mix")
    dproj, dy_ssm, dy_lru, d_bg = merge_bwd(proj, b_branch_gate, y_ssm, y_lru, dmix, name="merge_bwd")
    d_w_out_ssm = mm(yn, dy_ssm, "tn", name="d_w_out_ssm")
    d_w_out_lru = mm(o_lru, dy_lru, "tn", name="d_w_out_lru")
    early_g = [d_w_out_ssm.reshape(N_CHIPS, 512, D_MODEL), d_w_out_lru.reshape(N_CHIPS, 320, D_MODEL),
               d_w_out.reshape(N_CHIPS, 256, D_MODEL), d_w_ffn_in, d_w_ffn_out.reshape(N_CHIPS, 704, D_MODEL)]
    p_lands = [lax.empty((N_CHIPS, g.shape[1] // 2, g.shape[2]), f32) for g in early_g]
    p_send, p_recv, p_arrays, p_token = split_start(early_g + p_lands, early_g[0], pair_copies(5), (5,),
                                                    name="pair_early_start")
    dyn = mm(dy_ssm, w_out_ssm_f, "nt", out_dtype=bf16, after=p_token, name="d_yn")
    dy_ssd, dproj, d_ssm_norm = gnorm_bwd(y_ssd, proj, ssm_norm_w, dyn, dproj, name="gnorm_bwd")
    p_arrays = split_wait(p_send, p_recv, p_arrays, dy_ssd, pair_copies(5), name="pair_early_wait")
    e_pairs = [pair_add(g, rb, idx, name="pair_add_" + k) for g, rb, k in zip(p_arrays[:5], p_arrays[5:], late_names)]
    e_lands = [lax.empty((3,) + p[0].shape[1:], bf16) for p in e_pairs]
    e_send, e_recv, e_arrays, e_token = split_start([p[0] for p in e_pairs] + e_lands, e_pairs[0][1], reduce_copies(5),
                                                    (15,), name="reduce_early_start")
    dxbc_post, ddtr, dpar = ssd_bwd(xbc_post, dtr, dtr_t, par_row + e_token[0:1, 0:1], par_col, s_in, dy_ssd,
                                    name="ssd_bwd")
    dproj, d_ssm_cw_p, d_ssm_cb_p = conv_bwd(dxbc_post, xbc_pre, proj, OFF_XBC, ssm_cw_p, dproj, name="ssm_conv_bwd")
    do_lru = mm(dy_lru, w_out_lru_f, "nt", name="d_o_lru")
    du_lru, dproj, d_w_r, d_w_i, d_b_r, d_b_i, d_lam = lru_bwd(u_lru, proj, h_lru, do_lru, lru_w_r[0], lru_b_r, lru_w_i[0],
                                                               lru_b_i, lru_lambda, dproj, name="lru_bwd")
    dproj, d_lru_cw, d_lru_cb = conv_bwd(du_lru, None, proj, OFF_LX, lru_cw_full, dproj, name="lru_conv_bwd")
    ddt_cols = jnp.transpose(ddtr, (1, 0, 2)).reshape(t, SSM_HEADS).astype(bf16)
    ddt_cols = jnp.pad(ddt_cols, ((0, 0), (0, DT_PAD_W - SSM_HEADS)))
    dproj = lax.dynamic_update_slice(dproj, ddt_cols, (0, OFF_DT))

    d_ssm_cw = _unperm_xbc_cols(d_ssm_cw_p)
    d_ssm_cb = _unperm_xbc_cols(d_ssm_cb_p)
    dpar_h = jnp.transpose(dpar[:, 0:3, :], (1, 0, 2)).reshape(3, SSM_HEADS)
    small_names = ["norm1_w", "b_branch_gate", "ssm_conv_b", "ssm_dt_bias", "ssm_a_log", "ssm_d", "ssm_norm_w",
                   "lru_conv_b", "lru_w_r", "lru_b_r", "lru_w_i", "lru_b_i", "lru_lambda", "norm2_w", "norm_f_w"]
    small_g = dict(norm1_w=jnp.zeros_like(norm1_w), b_branch_gate=d_bg, ssm_conv_b=d_ssm_cb, ssm_dt_bias=dpar_h[0:1], ssm_a_log=dpar_h[1:2],
                   ssm_d=dpar_h[2:3], ssm_norm_w=d_ssm_norm, lru_conv_b=d_lru_cb, lru_w_r=d_w_r[None], lru_b_r=d_b_r,
                   lru_w_i=d_w_i[None], lru_b_i=d_b_i, lru_lambda=d_lam, norm2_w=d_norm2, norm_f_w=d_norm_f.reshape(D_MODEL))
    small_w = dict(norm1_w=norm1_w, b_branch_gate=b_branch_gate, ssm_conv_b=ssm_conv_b, ssm_dt_bias=ssm_dt_bias,
                   ssm_a_log=ssm_a_log, ssm_d=ssm_d, ssm_norm_w=ssm_norm_w, lru_conv_b=lru_conv_b, lru_w_r=lru_w_r,
                   lru_b_r=lru_b_r, lru_w_i=lru_w_i, lru_b_i=lru_b_i, lru_lambda=lru_lambda, norm2_w=norm2_w, norm_f_w=norm_f_w)
    small_m = dict(norm1_w=m_norm1_w, b_branch_gate=m_b_branch_gate, ssm_conv_b=m_ssm_conv_b, ssm_dt_bias=m_ssm_dt_bias,
                   ssm_a_log=m_ssm_a_log, ssm_d=m_ssm_d, ssm_norm_w=m_ssm_norm_w, lru_conv_b=m_lru_conv_b, lru_w_r=m_lru_w_r,
                   lru_b_r=m_lru_b_r, lru_w_i=m_lru_w_i, lru_b_i=m_lru_b_i, lru_lambda=m_lru_lambda, norm2_w=m_norm2_w,
                   norm_f_w=m_norm_f_w)
    small_v = dict(norm1_w=v_norm1_w, b_branch_gate=v_b_branch_gate, ssm_conv_b=v_ssm_conv_b, ssm_dt_bias=v_ssm_dt_bias,
                   ssm_a_log=v_ssm_a_log, ssm_d=v_ssm_d, ssm_norm_w=v_ssm_norm_w, lru_conv_b=v_lru_conv_b, lru_w_r=v_lru_w_r,
                   lru_b_r=v_lru_b_r, lru_w_i=v_lru_w_i, lru_b_i=v_lru_b_i, lru_lambda=v_lru_lambda, norm2_w=v_norm2_w,
                   norm_f_w=v_norm_f_w)
    shapes = [small_w[k].shape for k in small_names]
    conv_shapes = [(4, SSM_CONV_DIM), (4, LRU_WIDTH)]
    g_pack = _pack([small_g[k] for k in small_names] + [d_ssm_cw, d_lru_cw])
    s_send, s_recv, s_arrays, s_token = split_start([g_pack, lax.empty((8,) + g_pack.shape, f32)], g_pack, all_copies(),
                                                    (7,), name="small_start")
    d_w_in_p = mm(hn1, dproj, "tn", after=s_token, name="d_w_in")

    d_w_in_s = _unperm_in_to_shards(d_w_in_p)
    (l_sib,) = pair_exchange([d_w_in_s], name="pair_exchange_late")
    l_pair = pair_add(d_w_in_s, l_sib, idx, name="pair_add_w_in")
    l_land = lax.empty((3,) + l_pair[0].shape[1:], bf16)
    l_send, l_recv, l_arrays, l_token = split_start([l_pair[0], l_land], l_pair[1], reduce_copies(1), (3,),
                                                    name="reduce_late_start")
    grad_x, d_norm1 = mm(dproj, w_in_p, "nt", after=l_token, name="d_hn1",
                         epi=(epi_rms_bwd, [x2, dh1], [norm1_w], [("row", f32), ("vec",)]))

    e_arrays = split_wait(e_send, e_recv, e_arrays, d_norm1, reduce_copies(5), name="reduce_early_wait")
    e_half = [chip_sum(p[1], rb, idx, name="chip_sum_" + k) for p, rb, k in zip(e_pairs, e_arrays[5:], late_names)]
    big_out = {}
    for k, g in zip(late_names, pair_gather(e_half, name="pair_gather_early")):
        big_out[k] = (g,) + tuple(adamw(big_w[k], g, big_m[k], big_v[k], name="adamw_" + k))

    s_arrays = split_wait(s_send, s_recv, s_arrays, d_norm1, all_copies(), name="small_wait")
    g_sum = sum8(lax.dynamic_update_index_in_dim(s_arrays[1], g_pack, 2 * me + ci, 0), name="sum8")
    n1 = jnp.concatenate([d_norm1.reshape(8, 128), loss_tile], axis=0)
    n1_sum = sum8(lax.dynamic_update_index_in_dim(all_exchange(n1, name="all_exchange_norm1"), n1, 2 * me + ci, 0),
                  name="sum8_norm1")
    loss = n1_sum[8, 0]
    g_sum = lax.dynamic_update_slice(g_sum, n1_sum[0:8], (0, 0))
    g_small = _unpack(g_sum, shapes + conv_shapes)
    g_small[-2] = lax.dynamic_slice_in_dim(g_small[-2], me * 768, 768, axis=1)
    g_small[-1] = lax.dynamic_slice_in_dim(g_small[-1], me * 320, 320, axis=1)
    all_names = small_names + ["ssm_conv_w", "lru_conv_w"]
    small_w.update(ssm_conv_w=ssm_conv_w[0], lru_conv_w=lru_conv_w[0])
    small_m.update(ssm_conv_w=m_ssm_conv_w[0], lru_conv_w=m_lru_conv_w[0])
    small_v.update(ssm_conv_w=v_ssm_conv_w[0], lru_conv_w=v_lru_conv_w[0])
    as2d = lambda a: a.reshape(-1, a.shape[-1])
    upd = adamw_many([as2d(small_w[k]) for k in all_names], [as2d(g) for g in g_small],
                     [as2d(small_m[k]) for k in all_names], [as2d(small_v[k]) for k in all_names], name="adamw_small")
    small_out = {}
    for k, g, u in zip(all_names, g_small, upd):
        small_out[k] = (g,) + tuple(o.reshape(g.shape) for o in u)
    l_arrays = split_wait(l_send, l_recv, l_arrays, upd[0][0], reduce_copies(1), name="reduce_late_wait")
    l_half = chip_sum(l_pair[1], l_arrays[1], idx, name="chip_sum_w_in")
    (g_w_in,) = pair_gather([l_half], name="pair_gather_late")
    big_out["w_in"] = (g_w_in,) + tuple(adamw(big_w["w_in"], g_w_in, big_m["w_in"], big_v["w_in"], name="adamw_w_in"))

    order = ["norm1_w", "w_in", "b_branch_gate", "ssm_conv_w", "ssm_conv_b", "ssm_dt_bias", "ssm_a_log", "ssm_d", "ssm_norm_w",
             "w_out_ssm", "lru_conv_w", "lru_conv_b", "lru_w_r", "lru_b_r", "lru_w_i", "lru_b_i", "lru_lambda", "w_out_lru",
             "w_out", "norm2_w", "w_ffn_in", "w_ffn_out", "norm_f_w"]
    outs = [loss, grad_x[None]]
    for which in range(4):
        for k in order:
            if k in big_out:
                outs.append(big_out[k][which][None])
            elif k in ("ssm_conv_w", "lru_conv_w"):
                outs.append(small_out[k][which][None])
            else:
                outs.append(small_out[k][which])
    return tuple(outs)
```
